```python
import math
import numpy as np
import jax
import jax.numpy as jnp
from jax import lax

D_MODEL = 2048
BATCH = 8
SEQ = 8192
DEPTH = 2

GRID_W = 64
CTX_LEN = 256
HEAD_DIM = 128
NA_HEADS = 8
NB_Q_HEADS = 8
NB_KV_HEADS = 2
NA_ROWS = 8
NA_COLS = 16
NA_QCOLS = 16
NA_KCOLS = NA_QCOLS + NA_COLS
SW_RADIUS = 128
SW_BLOCK = 128
ROPE_BASE = 10000.0
SSM_GROUP = 16
SSM_GROUPS = D_MODEL // SSM_GROUP
SSM_STATE = 64
D_FF = -(-8 * D_MODEL // (3 * 256)) * 256
A_WIDTH = NA_HEADS * HEAD_DIM
B_Q_WIDTH = NB_Q_HEADS * HEAD_DIM
B_KV_WIDTH = NB_KV_HEADS * HEAD_DIM
IN_WIDTH = 3 * A_WIDTH + B_Q_WIDTH + 2 * B_KV_WIDTH
MIX_WIDTH = A_WIDTH + B_Q_WIDTH
IN_SPLITS = (A_WIDTH, 2 * A_WIDTH, 3 * A_WIDTH, 3 * A_WIDTH + B_Q_WIDTH, 3 * A_WIDTH + B_Q_WIDTH + B_KV_WIDTH)
EPS = 1e-6
NEG_INF = -1e30

kernel_name = 'hybrid_natten_swa_s5_prefix_dit'


def rms_norm(x, g):
    xf = x.astype(jnp.float32)
    y = xf * lax.rsqrt(jnp.mean(xf * xf, axis=-1, keepdims=True) + EPS)
    return (y * g.astype(jnp.float32)).astype(x.dtype)


def ada_mod(cvec, w, b):
    return jnp.split(jax.nn.silu(cvec) @ w + b, 6, axis=-1)


def swiglu(h, w1, w3, w2):
    return (jax.nn.silu(h @ w1) * (h @ w3)) @ w2


def axial_rope(x, pos_r, pos_c):
    half = x.shape[-1] // 2
    quarter = half // 2
    inv_freq = ROPE_BASE ** (-jnp.arange(quarter, dtype=jnp.float32) / quarter)
    xf = x.astype(jnp.float32)

    def rotate(xa, pos):
        ang = pos[:, None] * inv_freq[None, :]
        cos = jnp.cos(ang)[None, :, None, :]
        sin = jnp.sin(ang)[None, :, None, :]
        x1, x2 = xa[..., :quarter], xa[..., quarter:]
        return jnp.concatenate([x1 * cos - x2 * sin, x2 * cos + x1 * sin], axis=-1)

    return jnp.concatenate([rotate(xf[..., :half], pos_r), rotate(xf[..., half:], pos_c)], axis=-1).astype(x.dtype)


def context_attention(q, k, v, sink):
    b, m, hq, d = q.shape
    hkv = k.shape[2]
    grp = hq // hkv
    qg = q.reshape(b, m, hkv, grp, d)
    s = jnp.einsum('bqhgd,bkhd->bhgqk', qg, k, preferred_element_type=jnp.float32) * (d ** -0.5)
    if sink is not None:
        s_sink = jnp.broadcast_to(sink.astype(jnp.float32).reshape(1, hkv, grp, 1, 1), s.shape[:-1] + (1,))
        s = jnp.concatenate([s, s_sink], axis=-1)
    p = jax.nn.softmax(s, axis=-1)[..., :m].astype(v.dtype)
    return jnp.einsum('bhgqk,bkhd->bqhgd', p, v).reshape(b, m, hq, d)


def neighbourhood_attention(q, k, v, k_ctx, v_ctx, rpb):
    b, seq, h, d = q.shape
    rows = seq // GRID_W
    kh = min(NA_ROWS, rows)
    ncb = GRID_W // NA_QCOLS
    r = jnp.arange(rows)
    key_rows = jnp.clip(r - kh // 2, 0, rows - kh)[:, None] + jnp.arange(kh)[None, :]
    blk = jnp.arange(ncb)
    key_cols = (jnp.clip(blk * NA_QCOLS - NA_COLS // 2, 0, GRID_W - NA_KCOLS)[:, None]
                + jnp.arange(NA_KCOLS)[None, :])
    q_cols = blk[:, None] * NA_QCOLS + jnp.arange(NA_QCOLS)[None, :]
    win_start = jnp.clip(q_cols - NA_COLS // 2, 0, GRID_W - NA_COLS)[..., None]
    kc_b = key_cols[:, None, :]
    col_valid = (kc_b >= win_start) & (kc_b < win_start + NA_COLS)
    row_idx = key_rows - r[:, None] + NA_ROWS - 1
    col_idx = jnp.clip(kc_b - q_cols[..., None] + NA_COLS - 1, 0, 2 * NA_COLS - 2)
    bias = rpb.astype(jnp.float32)[:, row_idx[:, None, None, :, None], col_idx[None, :, :, None, :]]

    gather_r = key_rows[:, None, :, None]
    gather_c = key_cols[None, :, None, :]
    kg = k.reshape(b, rows, GRID_W, h, d)[:, gather_r, gather_c]
    vg = v.reshape(b, rows, GRID_W, h, d)[:, gather_r, gather_c]
    qg = q.reshape(b, rows, ncb, NA_QCOLS, h, d)
    scale = d ** -0.5
    n_loc = kh * NA_KCOLS
    s_loc = jnp.einsum('brnqhd,brnikhd->bhrnqik', qg, kg, preferred_element_type=jnp.float32) * scale + bias[None]
    s_loc = jnp.where(col_valid[:, :, None, :], s_loc, NEG_INF).reshape(b, h, rows, ncb, NA_QCOLS, n_loc)
    s_ctx = jnp.einsum('brnqhd,bmhd->bhrnqm', qg, k_ctx, preferred_element_type=jnp.float32) * scale
    p = jax.nn.softmax(jnp.concatenate([s_loc, s_ctx], axis=-1), axis=-1).astype(v.dtype)
    p_loc = p[..., :n_loc].reshape(b, h, rows, ncb, NA_QCOLS, kh, NA_KCOLS)
    o = (jnp.einsum('bhrnqik,brnikhd->brnqhd', p_loc, vg)
         + jnp.einsum('bhrnqm,bmhd->brnqhd', p[..., n_loc:], v_ctx))
    return o.reshape(b, seq, h, d)


def window_attention(q, k, v, k_ctx, v_ctx, sink):
    b, seq, hq, d = q.shape
    hkv = k.shape[2]
    grp = hq // hkv
    nb = seq // SW_BLOCK
    qb = q.reshape(b, nb, SW_BLOCK, hkv, grp, d)

    def band(t):
        tb = jnp.pad(t, ((0, 0), (SW_BLOCK, SW_BLOCK), (0, 0), (0, 0))).reshape(b, nb + 2, SW_BLOCK, hkv, d)
        return jnp.concatenate([tb[:, :-2], tb[:, 1:-1], tb[:, 2:]], axis=2)

    kb, vb = band(k), band(v)
    blocks = jnp.arange(nb)[:, None]
    q_pos = blocks * SW_BLOCK + jnp.arange(SW_BLOCK)[None, :]
    k_pos = (blocks - 1) * SW_BLOCK + jnp.arange(3 * SW_BLOCK)[None, :]
    k_pos_b = k_pos[:, None, :]
    valid = (jnp.abs(k_pos_b - q_pos[:, :, None]) <= SW_RADIUS) & (k_pos_b >= 0) & (k_pos_b < seq)
    scale = d ** -0.5
    s_loc = jnp.einsum('bnqhgd,bnkhd->bhgnqk', qb, kb, preferred_element_type=jnp.float32) * scale
    s_loc = jnp.where(valid, s_loc, NEG_INF)
    s_ctx = jnp.einsum('bnqhgd,bmhd->bhgnqm', qb, k_ctx, preferred_element_type=jnp.float32) * scale
    s_sink = jnp.broadcast_to(sink.astype(jnp.float32).reshape(1, hkv, grp, 1, 1, 1), s_loc.shape[:-1] + (1,))
    p = jax.nn.softmax(jnp.concatenate([s_loc, s_ctx, s_sink], axis=-1), axis=-1).astype(v.dtype)
    n_loc = 3 * SW_BLOCK
    m = k_ctx.shape[1]
    o = (jnp.einsum('bhgnqk,bnkhd->bnqhgd', p[..., :n_loc], vb)
         + jnp.einsum('bhgnqm,bmhd->bnqhgd', p[..., n_loc:n_loc + m], v_ctx))
    return o.reshape(b, seq, hq, d)


def hybrid_attention(hx, hc, w_in, w_out, rpb, sink, pos_r, pos_c, need_ctx):
    def project(h):
        b, t, _ = h.shape
        parts = jnp.split(h @ w_in, IN_SPLITS, axis=-1)
        return [z.reshape(b, t, -1, HEAD_DIM) for z in parts]

    b, seq, _ = hx.shape
    qa, ka, va, qb, kb, vb = project(hx)
    qa_c, ka_c, va_c, qb_c, kb_c, vb_c = project(hc)
    oa = neighbourhood_attention(qa, ka, va, ka_c, va_c, rpb)
    ob = window_attention(axial_rope(qb, pos_r, pos_c), axial_rope(kb, pos_r, pos_c), vb, kb_c, vb_c, sink)
    yx = jnp.concatenate([oa.reshape(b, seq, A_WIDTH), ob.reshape(b, seq, B_Q_WIDTH)], axis=-1) @ w_out
    if not need_ctx:
        return yx, None
    m = hc.shape[1]
    oa_c = context_attention(qa_c, ka_c, va_c, None)
    ob_c = context_attention(qb_c, kb_c, vb_c, sink)
    yc = jnp.concatenate([oa_c.reshape(b, m, A_WIDTH), ob_c.reshape(b, m, B_Q_WIDTH)], axis=-1) @ w_out
    return yx, yc


def s5_discretise(a_re, a_im, log_dt, b_re, b_im):
    ar = a_re.astype(jnp.float32)
    ai = a_im.astype(jnp.float32)
    dt = jnp.exp(log_dt.astype(jnp.float32))[:, None]
    mag = jnp.exp(ar * dt)
    lam_r, lam_i = mag * jnp.cos(ai * dt), mag * jnp.sin(ai * dt)
    den = ar * ar + ai * ai
    nr = lam_r - 1.0
    coef_r = (nr * ar + lam_i * ai) / den
    coef_i = (lam_i * ar - nr * ai) / den
    br, bi = b_re.astype(jnp.float32), b_im.astype(jnp.float32)
    bbar_r = coef_r[..., None] * br - coef_i[..., None] * bi
    bbar_i = coef_r[..., None] * bi + coef_i[..., None] * br
    return lam_r, lam_i, bbar_r, bbar_i


def complex_diag_scan(lam_r, lam_i, u_r, u_i, reverse):
    a_r = jnp.broadcast_to(lam_r, u_r.shape)
    a_i = jnp.broadcast_to(lam_i, u_i.shape)

    def combine(e1, e2):
        a1r, a1i, b1r, b1i = e1
        a2r, a2i, b2r, b2i = e2
        return (a2r * a1r - a2i * a1i, a2r * a1i + a2i * a1r,
                a2r * b1r - a2i * b1i + b2r, a2r * b1i + a2i * b1r + b2i)

    return lax.associative_scan(combine, (a_r, a_i, u_r, u_i), reverse=reverse, axis=1)


def s5_direction(ux, uc, a_re, a_im, log_dt, b_re, b_im, c_re, c_im, reverse, need_ctx):
    lam_r, lam_i, bbar_r, bbar_i = s5_discretise(a_re, a_im, log_dt, b_re, b_im)
    cr, ci = c_re.astype(jnp.float32), c_im.astype(jnp.float32)

    def drive(u):
        ug = u.astype(jnp.float32).reshape(u.shape[0], u.shape[1], SSM_GROUPS, SSM_GROUP)
        return jnp.einsum('btgh,gph->btgp', ug, bbar_r), jnp.einsum('btgh,gph->btgp', ug, bbar_i)

    def readout(s_r, s_i):
        y = jnp.einsum('btgp,ghp->btgh', s_r, cr) - jnp.einsum('btgp,ghp->btgh', s_i, ci)
        return y.reshape(y.shape[0], y.shape[1], -1)

    uc_r, uc_i = drive(uc)
    _, _, sc_r, sc_i = complex_diag_scan(lam_r, lam_i, uc_r, uc_i, reverse)
    end = 0 if reverse else -1
    s0_r, s0_i = sc_r[:, end][:, None], sc_i[:, end][:, None]
    ux_r, ux_i = drive(ux)
    pw_r, pw_i, sx_r, sx_i = complex_diag_scan(lam_r, lam_i, ux_r, ux_i, reverse)
    sx_r = sx_r + pw_r * s0_r - pw_i * s0_i
    sx_i = sx_i + pw_r * s0_i + pw_i * s0_r
    y_ctx = readout(sc_r, sc_i) if need_ctx else None
    return readout(sx_r, sx_i), y_ctx


def s5_glu_mixer(ux, uc, a_re, a_im, log_dt, b_re, b_im, c_re, c_im, d_skip, w_glu, b_glu, need_ctx):
    d = d_skip.astype(jnp.float32)
    y_x = d * ux.astype(jnp.float32)
    y_c = d * uc.astype(jnp.float32) if need_ctx else None
    for direction in range(2):
        yx_d, yc_d = s5_direction(ux, uc, a_re[direction], a_im[direction], log_dt[direction],
                                  b_re[direction], b_im[direction], c_re[direction], c_im[direction],
                                  direction == 1, need_ctx)
        y_x = y_x + yx_d
        if need_ctx:
            y_c = y_c + yc_d

    def glu(y, dtype):
        z = jax.nn.gelu(y).astype(dtype) @ w_glu + b_glu
        val, gate = jnp.split(z, 2, axis=-1)
        return val * jax.nn.sigmoid(gate)

    return glu(y_x, ux.dtype), (glu(y_c, uc.dtype) if need_ctx else None)


def _fwd_setup_inputs(seed: int = 0) -> dict:
    key = jax.random.key(seed)
    keys = iter(jax.random.split(key, 32))
    f32 = jnp.float32

    def normal(shape, scale):
        return jax.random.normal(next(keys), shape, f32) * scale

    d, g, h, p = D_MODEL, SSM_GROUPS, SSM_GROUP, SSM_STATE
    n_attn, n_ssm = (DEPTH + 1) // 2, DEPTH // 2
    a_im0 = math.pi * jnp.arange(p, dtype=f32)
    return {
        'x': normal((BATCH, SEQ, d), 1.0),
        'c': normal((BATCH, d), 1.0),
        'ctx': normal((BATCH, CTX_LEN, d), 1.0),
        'c_ctx': normal((d,), 1.0),
        'ada_w': normal((DEPTH, d, 6 * d), 0.5 * d ** -0.5),
        'ada_b': normal((DEPTH, 6 * d), 0.01),
        'norm_mix': 1.0 + normal((DEPTH, d), 0.02),
        'norm_ffn': 1.0 + normal((DEPTH, d), 0.02),
        'ffn_w1': normal((DEPTH, d, D_FF), d ** -0.5),
        'ffn_w3': normal((DEPTH, d, D_FF), d ** -0.5),
        'ffn_w2': normal((DEPTH, D_FF, d), D_FF ** -0.5),
        'attn_w_in': normal((n_attn, d, IN_WIDTH), d ** -0.5),
        'attn_w_out': normal((n_attn, MIX_WIDTH, d), MIX_WIDTH ** -0.5),
        'attn_rpb': normal((n_attn, NA_HEADS, 2 * NA_ROWS - 1, 2 * NA_COLS - 1), 0.02),
        'attn_sink': normal((n_attn, NB_Q_HEADS), 0.5),
        'ssm_a_re': -0.5 + normal((n_ssm, 2, g, p), 0.01),
        'ssm_a_im': a_im0 + normal((n_ssm, 2, g, p), 0.01),
        'ssm_log_dt': jax.random.uniform(next(keys), (n_ssm, 2, g), f32, math.log(1e-3), math.log(1e-1)),
        'ssm_b_re': normal((n_ssm, 2, g, p, h), (2 * h) ** -0.5),
        'ssm_b_im': normal((n_ssm, 2, g, p, h), (2 * h) ** -0.5),
        'ssm_c_re': normal((n_ssm, 2, g, h, p), p ** -0.5),
        'ssm_c_im': normal((n_ssm, 2, g, h, p), p ** -0.5),
        'ssm_d': normal((n_ssm, d), 0.5),
        'ssm_w_glu': normal((n_ssm, d, 2 * d), d ** -0.5),
        'ssm_b_glu': normal((n_ssm, 2 * d), 0.01),
        'norm_final': 1.0 + normal((d,), 0.02),
    }


def _fwd_reference(x, c, ctx, c_ctx, ada_w, ada_b, norm_mix, norm_ffn, ffn_w1, ffn_w3, ffn_w2,
              attn_w_in, attn_w_out, attn_rpb, attn_sink,
              ssm_a_re, ssm_a_im, ssm_log_dt, ssm_b_re, ssm_b_im, ssm_c_re, ssm_c_im,
              ssm_d, ssm_w_glu, ssm_b_glu, norm_final):
    seq = x.shape[1]
    t = jnp.arange(seq)
    pos_r = (t // GRID_W).astype(jnp.float32)
    pos_c = (t % GRID_W).astype(jnp.float32)
    for layer in range(DEPTH):
        need_ctx = layer < DEPTH - 1
        i = layer // 2
        sh1, sc1, g1, sh2, sc2, g2 = [m[:, None, :] for m in ada_mod(c, ada_w[layer], ada_b[layer])]
        csh1, csc1, cg1, csh2, csc2, cg2 = ada_mod(c_ctx, ada_w[layer], ada_b[layer])
        hx = rms_norm(x, norm_mix[layer]) * (1.0 + sc1) + sh1
        hc = rms_norm(ctx, norm_mix[layer]) * (1.0 + csc1) + csh1
        if layer % 2 == 0:
            yx, yc = hybrid_attention(hx, hc, attn_w_in[i], attn_w_out[i], attn_rpb[i], attn_sink[i],
                                      pos_r, pos_c, need_ctx)
        else:
            yx, yc = s5_glu_mixer(hx, hc, ssm_a_re[i], ssm_a_im[i], ssm_log_dt[i], ssm_b_re[i], ssm_b_im[i],
                                  ssm_c_re[i], ssm_c_im[i], ssm_d[i], ssm_w_glu[i], ssm_b_glu[i], need_ctx)
        x = x + g1 * yx
        x = x + g2 * swiglu(rms_norm(x, norm_ffn[layer]) * (1.0 + sc2) + sh2,
                            ffn_w1[layer], ffn_w3[layer], ffn_w2[layer])
        if need_ctx:
            ctx = ctx + cg1 * yc
            ctx = ctx + cg2 * swiglu(rms_norm(ctx, norm_ffn[layer]) * (1.0 + csc2) + csh2,
                                     ffn_w1[layer], ffn_w3[layer], ffn_w2[layer])
    return rms_norm(x, norm_final)


import jax as _jax
import jax.numpy as _jnp

TWIN_FORMAT = 'train_step'
FWD_PARAMS = ['x', 'c', 'ctx', 'c_ctx', 'ada_w', 'ada_b', 'norm_mix', 'norm_ffn', 'ffn_w1', 'ffn_w3', 'ffn_w2', 'attn_w_in', 'attn_w_out', 'attn_rpb', 'attn_sink', 'ssm_a_re', 'ssm_a_im', 'ssm_log_dt', 'ssm_b_re', 'ssm_b_im', 'ssm_c_re', 'ssm_c_im', 'ssm_d', 'ssm_w_glu', 'ssm_b_glu', 'norm_final']
TWIN_WEIGHTS = ['c_ctx', 'ada_w', 'ada_b', 'norm_mix', 'norm_ffn', 'ffn_w1', 'ffn_w3', 'ffn_w2', 'attn_w_in', 'attn_w_out', 'attn_rpb', 'attn_sink', 'ssm_a_re', 'ssm_a_im', 'ssm_log_dt', 'ssm_b_re', 'ssm_b_im', 'ssm_c_re', 'ssm_c_im', 'ssm_d', 'ssm_w_glu', 'ssm_b_glu', 'norm_final']
TWIN_DIFF_INPUT = 'x'
TWIN_INPUTS = ['x', 'c', 'ctx', 'c_ctx', 'ada_w', 'ada_b', 'norm_mix', 'norm_ffn', 'ffn_w1', 'ffn_w3', 'ffn_w2', 'attn_w_in', 'attn_w_out', 'attn_rpb', 'attn_sink', 'ssm_a_re', 'ssm_a_im', 'ssm_log_dt', 'ssm_b_re', 'ssm_b_im', 'ssm_c_re', 'ssm_c_im', 'ssm_d', 'ssm_w_glu', 'ssm_b_glu', 'norm_final', 'loss_target', 'm_c_ctx', 'm_ada_w', 'm_ada_b', 'm_norm_mix', 'm_norm_ffn', 'm_ffn_w1', 'm_ffn_w3', 'm_ffn_w2', 'm_attn_w_in', 'm_attn_w_out', 'm_attn_rpb', 'm_attn_sink', 'm_ssm_a_re', 'm_ssm_a_im', 'm_ssm_log_dt', 'm_ssm_b_re', 'm_ssm_b_im', 'm_ssm_c_re', 'm_ssm_c_im', 'm_ssm_d', 'm_ssm_w_glu', 'm_ssm_b_glu', 'm_norm_final', 'v_c_ctx', 'v_ada_w', 'v_ada_b', 'v_norm_mix', 'v_norm_ffn', 'v_ffn_w1', 'v_ffn_w3', 'v_ffn_w2', 'v_attn_w_in', 'v_attn_w_out', 'v_attn_rpb', 'v_attn_sink', 'v_ssm_a_re', 'v_ssm_a_im', 'v_ssm_log_dt', 'v_ssm_b_re', 'v_ssm_b_im', 'v_ssm_c_re', 'v_ssm_c_im', 'v_ssm_d', 'v_ssm_w_glu', 'v_ssm_b_glu', 'v_norm_final']
TWIN_OUTPUTS = ['loss', 'grad_x', 'grad_c_ctx', 'grad_ada_w', 'grad_ada_b', 'grad_norm_mix', 'grad_norm_ffn', 'grad_ffn_w1', 'grad_ffn_w3', 'grad_ffn_w2', 'grad_attn_w_in', 'grad_attn_w_out', 'grad_attn_rpb', 'grad_attn_sink', 'grad_ssm_a_re', 'grad_ssm_a_im', 'grad_ssm_log_dt', 'grad_ssm_b_re', 'grad_ssm_b_im', 'grad_ssm_c_re', 'grad_ssm_c_im', 'grad_ssm_d', 'grad_ssm_w_glu', 'grad_ssm_b_glu', 'grad_norm_final', 'delta_c_ctx', 'delta_ada_w', 'delta_ada_b', 'delta_norm_mix', 'delta_norm_ffn', 'delta_ffn_w1', 'delta_ffn_w3', 'delta_ffn_w2', 'delta_attn_w_in', 'delta_attn_w_out', 'delta_attn_rpb', 'delta_attn_sink', 'delta_ssm_a_re', 'delta_ssm_a_im', 'delta_ssm_log_dt', 'delta_ssm_b_re', 'delta_ssm_b_im', 'delta_ssm_c_re', 'delta_ssm_c_im', 'delta_ssm_d', 'delta_ssm_w_glu', 'delta_ssm_b_glu', 'delta_norm_final', 'new_m_c_ctx', 'new_m_ada_w', 'new_m_ada_b', 'new_m_norm_mix', 'new_m_norm_ffn', 'new_m_ffn_w1', 'new_m_ffn_w3', 'new_m_ffn_w2', 'new_m_attn_w_in', 'new_m_attn_w_out', 'new_m_attn_rpb', 'new_m_attn_sink', 'new_m_ssm_a_re', 'new_m_ssm_a_im', 'new_m_ssm_log_dt', 'new_m_ssm_b_re', 'new_m_ssm_b_im', 'new_m_ssm_c_re', 'new_m_ssm_c_im', 'new_m_ssm_d', 'new_m_ssm_w_glu', 'new_m_ssm_b_glu', 'new_m_norm_final', 'new_v_c_ctx', 'new_v_ada_w', 'new_v_ada_b', 'new_v_norm_mix', 'new_v_norm_ffn', 'new_v_ffn_w1', 'new_v_ffn_w3', 'new_v_ffn_w2', 'new_v_attn_w_in', 'new_v_attn_w_out', 'new_v_attn_rpb', 'new_v_attn_sink', 'new_v_ssm_a_re', 'new_v_ssm_a_im', 'new_v_ssm_log_dt', 'new_v_ssm_b_re', 'new_v_ssm_b_im', 'new_v_ssm_c_re', 'new_v_ssm_c_im', 'new_v_ssm_d', 'new_v_ssm_w_glu', 'new_v_ssm_b_glu', 'new_v_norm_final']
TWIN_LEAF_KINDS = {'loss': 'loss', 'grad_x': 'grad_x', 'grad_c_ctx': 'grad_w', 'grad_ada_w': 'grad_w', 'grad_ada_b': 'grad_w', 'grad_norm_mix': 'grad_w', 'grad_norm_ffn': 'grad_w', 'grad_ffn_w1': 'grad_w', 'grad_ffn_w3': 'grad_w', 'grad_ffn_w2': 'grad_w', 'grad_attn_w_in': 'grad_w', 'grad_attn_w_out': 'grad_w', 'grad_attn_rpb': 'grad_w', 'grad_attn_sink': 'grad_w', 'grad_ssm_a_re': 'grad_w', 'grad_ssm_a_im': 'grad_w', 'grad_ssm_log_dt': 'grad_w', 'grad_ssm_b_re': 'grad_w', 'grad_ssm_b_im': 'grad_w', 'grad_ssm_c_re': 'grad_w', 'grad_ssm_c_im': 'grad_w', 'grad_ssm_d': 'grad_w', 'grad_ssm_w_glu': 'grad_w', 'grad_ssm_b_glu': 'grad_w', 'grad_norm_final': 'grad_w', 'delta_c_ctx': 'delta_w', 'delta_ada_w': 'delta_w', 'delta_ada_b': 'delta_w', 'delta_norm_mix': 'delta_w', 'delta_norm_ffn': 'delta_w', 'delta_ffn_w1': 'delta_w', 'delta_ffn_w3': 'delta_w', 'delta_ffn_w2': 'delta_w', 'delta_attn_w_in': 'delta_w', 'delta_attn_w_out': 'delta_w', 'delta_attn_rpb': 'delta_w', 'delta_attn_sink': 'delta_w', 'delta_ssm_a_re': 'delta_w', 'delta_ssm_a_im': 'delta_w', 'delta_ssm_log_dt': 'delta_w', 'delta_ssm_b_re': 'delta_w', 'delta_ssm_b_im': 'delta_w', 'delta_ssm_c_re': 'delta_w', 'delta_ssm_c_im': 'delta_w', 'delta_ssm_d': 'delta_w', 'delta_ssm_w_glu': 'delta_w', 'delta_ssm_b_glu': 'delta_w', 'delta_norm_final': 'delta_w', 'new_m_c_ctx': 'new_m', 'new_m_ada_w': 'new_m', 'new_m_ada_b': 'new_m', 'new_m_norm_mix': 'new_m', 'new_m_norm_ffn': 'new_m', 'new_m_ffn_w1': 'new_m', 'new_m_ffn_w3': 'new_m', 'new_m_ffn_w2': 'new_m', 'new_m_attn_w_in': 'new_m', 'new_m_attn_w_out': 'new_m', 'new_m_attn_rpb': 'new_m', 'new_m_attn_sink': 'new_m', 'new_m_ssm_a_re': 'new_m', 'new_m_ssm_a_im': 'new_m', 'new_m_ssm_log_dt': 'new_m', 'new_m_ssm_b_re': 'new_m', 'new_m_ssm_b_im': 'new_m', 'new_m_ssm_c_re': 'new_m', 'new_m_ssm_c_im': 'new_m', 'new_m_ssm_d': 'new_m', 'new_m_ssm_w_glu': 'new_m', 'new_m_ssm_b_glu': 'new_m', 'new_m_norm_final': 'new_m', 'new_v_c_ctx': 'new_v', 'new_v_ada_w': 'new_v', 'new_v_ada_b': 'new_v', 'new_v_norm_mix': 'new_v', 'new_v_norm_ffn': 'new_v', 'new_v_ffn_w1': 'new_v', 'new_v_ffn_w3': 'new_v', 'new_v_ffn_w2': 'new_v', 'new_v_attn_w_in': 'new_v', 'new_v_attn_w_out': 'new_v', 'new_v_attn_rpb': 'new_v', 'new_v_attn_sink': 'new_v', 'new_v_ssm_a_re': 'new_v', 'new_v_ssm_a_im': 'new_v', 'new_v_ssm_log_dt': 'new_v', 'new_v_ssm_b_re': 'new_v', 'new_v_ssm_b_im': 'new_v', 'new_v_ssm_c_re': 'new_v', 'new_v_ssm_c_im': 'new_v', 'new_v_ssm_d': 'new_v', 'new_v_ssm_w_glu': 'new_v', 'new_v_ssm_b_glu': 'new_v', 'new_v_norm_final': 'new_v'}


def _forward(args):
    return _fwd_reference(*[args[k] for k in FWD_PARAMS])


def _output_shape():
    def fwd():
        inp = _fwd_setup_inputs(0)
        return _fwd_reference(*[inp[k] for k in FWD_PARAMS])
    out = _jax.eval_shape(fwd)
    return out.shape, out.dtype

N_MICROBATCH = 1
ADAM_LR = 0.001
ADAM_B1 = 0.9
ADAM_B2 = 0.999
ADAM_EPS = 1e-08
ADAM_WD = 0.01
ADAM_STEP = 10
PER_EXAMPLE_BATCH_AXIS = {'x': 0, 'c': 0, 'ctx': 0, 'loss_target': 0}
SHARED_INPUTS = []
_WEIGHT_DTYPES = {'c_ctx': _jnp.float32, 'ada_w': _jnp.float32, 'ada_b': _jnp.float32, 'norm_mix': _jnp.float32, 'norm_ffn': _jnp.float32, 'ffn_w1': _jnp.float32, 'ffn_w3': _jnp.float32, 'ffn_w2': _jnp.float32, 'attn_w_in': _jnp.float32, 'attn_w_out': _jnp.float32, 'attn_rpb': _jnp.float32, 'attn_sink': _jnp.float32, 'ssm_a_re': _jnp.float32, 'ssm_a_im': _jnp.float32, 'ssm_log_dt': _jnp.float32, 'ssm_b_re': _jnp.float32, 'ssm_b_im': _jnp.float32, 'ssm_c_re': _jnp.float32, 'ssm_c_im': _jnp.float32, 'ssm_d': _jnp.float32, 'ssm_w_glu': _jnp.float32, 'ssm_b_glu': _jnp.float32, 'norm_final': _jnp.float32}
MOMENT_SCALE = {'c_ctx': 8.833936e-03, 'ada_w': 2.696707e-02, 'ada_b': 4.773012e-02, 'norm_mix': 7.494379e-03, 'norm_ffn': 3.652818e-02, 'ffn_w1': 1.621886e-02, 'ffn_w3': 1.568720e-02, 'ffn_w2': 2.604014e-02, 'attn_w_in': 7.901402e-03, 'attn_w_out': 9.690250e-03, 'attn_rpb': 1.511484e-03, 'attn_sink': 3.487486e-04, 'ssm_a_re': 1.288984e-03, 'ssm_a_im': 9.932165e-04, 'ssm_log_dt': 5.194214e-01, 'ssm_b_re': 6.954891e-04, 'ssm_b_im': 7.146346e-04, 'ssm_c_re': 9.933807e-04, 'ssm_c_im': 1.018065e-03, 'ssm_d': 1.416077e-02, 'ssm_w_glu': 4.882031e-03, 'ssm_b_glu': 1.348508e-02, 'norm_final': 3.196059e+01}


def _to_microbatches(a, axis):
    t = _jnp.moveaxis(a, axis, 0)
    t = t.reshape((N_MICROBATCH, t.shape[0] // N_MICROBATCH) + t.shape[1:])
    return _jnp.moveaxis(t, 1, axis + 1)


def setup_inputs(seed: int = 0) -> dict:
    inp = _fwd_setup_inputs(seed)
    key = _jax.random.fold_in(_jax.random.key(seed), 7919)
    shape, _ = _output_shape()
    out = dict(inp)
    out["loss_target"] = _jax.random.normal(_jax.random.fold_in(key, 0), shape, _jnp.float32)
    for i, name in enumerate(TWIN_WEIGHTS):
        w = inp[name].astype(_jnp.float32)
        if MOMENT_SCALE is None:
            s = _jnp.sqrt(_jnp.mean(_jnp.square(w)) + 1e-30)
        else:
            s = MOMENT_SCALE[name]
        km, kv = _jax.random.split(_jax.random.fold_in(key, i + 1))
        out[name] = w
        out["m_" + name] = s * _jax.random.normal(km, w.shape, _jnp.float32)
        out["v_" + name] = (s * s) * _jax.random.uniform(kv, w.shape, _jnp.float32, 0.5, 1.5)
    if N_MICROBATCH > 1:
        for name, axis in PER_EXAMPLE_BATCH_AXIS.items():
            out[name] = _to_microbatches(out[name], axis)
    return {'x': out['x'], 'c': out['c'], 'ctx': out['ctx'], 'c_ctx': out['c_ctx'], 'ada_w': out['ada_w'], 'ada_b': out['ada_b'], 'norm_mix': out['norm_mix'], 'norm_ffn': out['norm_ffn'], 'ffn_w1': out['ffn_w1'], 'ffn_w3': out['ffn_w3'], 'ffn_w2': out['ffn_w2'], 'attn_w_in': out['attn_w_in'], 'attn_w_out': out['attn_w_out'], 'attn_rpb': out['attn_rpb'], 'attn_sink': out['attn_sink'], 'ssm_a_re': out['ssm_a_re'], 'ssm_a_im': out['ssm_a_im'], 'ssm_log_dt': out['ssm_log_dt'], 'ssm_b_re': out['ssm_b_re'], 'ssm_b_im': out['ssm_b_im'], 'ssm_c_re': out['ssm_c_re'], 'ssm_c_im': out['ssm_c_im'], 'ssm_d': out['ssm_d'], 'ssm_w_glu': out['ssm_w_glu'], 'ssm_b_glu': out['ssm_b_glu'], 'norm_final': out['norm_final'], 'loss_target': out['loss_target'], 'm_c_ctx': out['m_c_ctx'], 'm_ada_w': out['m_ada_w'], 'm_ada_b': out['m_ada_b'], 'm_norm_mix': out['m_norm_mix'], 'm_norm_ffn': out['m_norm_ffn'], 'm_ffn_w1': out['m_ffn_w1'], 'm_ffn_w3': out['m_ffn_w3'], 'm_ffn_w2': out['m_ffn_w2'], 'm_attn_w_in': out['m_attn_w_in'], 'm_attn_w_out': out['m_attn_w_out'], 'm_attn_rpb': out['m_attn_rpb'], 'm_attn_sink': out['m_attn_sink'], 'm_ssm_a_re': out['m_ssm_a_re'], 'm_ssm_a_im': out['m_ssm_a_im'], 'm_ssm_log_dt': out['m_ssm_log_dt'], 'm_ssm_b_re': out['m_ssm_b_re'], 'm_ssm_b_im': out['m_ssm_b_im'], 'm_ssm_c_re': out['m_ssm_c_re'], 'm_ssm_c_im': out['m_ssm_c_im'], 'm_ssm_d': out['m_ssm_d'], 'm_ssm_w_glu': out['m_ssm_w_glu'], 'm_ssm_b_glu': out['m_ssm_b_glu'], 'm_norm_final': out['m_norm_final'], 'v_c_ctx': out['v_c_ctx'], 'v_ada_w': out['v_ada_w'], 'v_ada_b': out['v_ada_b'], 'v_norm_mix': out['v_norm_mix'], 'v_norm_ffn': out['v_norm_ffn'], 'v_ffn_w1': out['v_ffn_w1'], 'v_ffn_w3': out['v_ffn_w3'], 'v_ffn_w2': out['v_ffn_w2'], 'v_attn_w_in': out['v_attn_w_in'], 'v_attn_w_out': out['v_attn_w_out'], 'v_attn_rpb': out['v_attn_rpb'], 'v_attn_sink': out['v_attn_sink'], 'v_ssm_a_re': out['v_ssm_a_re'], 'v_ssm_a_im': out['v_ssm_a_im'], 'v_ssm_log_dt': out['v_ssm_log_dt'], 'v_ssm_b_re': out['v_ssm_b_re'], 'v_ssm_b_im': out['v_ssm_b_im'], 'v_ssm_c_re': out['v_ssm_c_re'], 'v_ssm_c_im': out['v_ssm_c_im'], 'v_ssm_d': out['v_ssm_d'], 'v_ssm_w_glu': out['v_ssm_w_glu'], 'v_ssm_b_glu': out['v_ssm_b_glu'], 'v_norm_final': out['v_norm_final']}


def _loss(weights, diff, rest, loss_target):
    with _jax.named_scope("forward"):
        args = {**rest, TWIN_DIFF_INPUT: diff, **{k: w.astype(_WEIGHT_DTYPES[k]) for k, w in weights.items()}}
        y = _forward(args)
    with _jax.named_scope("loss_head"):
        err = _jnp.square(y.astype(_jnp.float32) - loss_target)
        return 0.5 * _jnp.sum(_jnp.mean(err, axis=-1)) if err.ndim else 0.5 * err


def _adamw(w, g, m, v):
    m = ADAM_B1 * m + (1.0 - ADAM_B1) * g
    v = ADAM_B2 * v + (1.0 - ADAM_B2) * _jnp.square(g)
    m_hat = m / (1.0 - ADAM_B1 ** ADAM_STEP)
    v_hat = v / (1.0 - ADAM_B2 ** ADAM_STEP)
    delta = -ADAM_LR * (m_hat / (_jnp.sqrt(v_hat) + ADAM_EPS) + ADAM_WD * w)
    return delta, m, v


def reference(x, c, ctx, c_ctx, ada_w, ada_b, norm_mix, norm_ffn, ffn_w1, ffn_w3, ffn_w2, attn_w_in, attn_w_out, attn_rpb, attn_sink, ssm_a_re, ssm_a_im, ssm_log_dt, ssm_b_re, ssm_b_im, ssm_c_re, ssm_c_im, ssm_d, ssm_w_glu, ssm_b_glu, norm_final, loss_target, m_c_ctx, m_ada_w, m_ada_b, m_norm_mix, m_norm_ffn, m_ffn_w1, m_ffn_w3, m_ffn_w2, m_attn_w_in, m_attn_w_out, m_attn_rpb, m_attn_sink, m_ssm_a_re, m_ssm_a_im, m_ssm_log_dt, m_ssm_b_re, m_ssm_b_im, m_ssm_c_re, m_ssm_c_im, m_ssm_d, m_ssm_w_glu, m_ssm_b_glu, m_norm_final, v_c_ctx, v_ada_w, v_ada_b, v_norm_mix, v_norm_ffn, v_ffn_w1, v_ffn_w3, v_ffn_w2, v_attn_w_in, v_attn_w_out, v_attn_rpb, v_attn_sink, v_ssm_a_re, v_ssm_a_im, v_ssm_log_dt, v_ssm_b_re, v_ssm_b_im, v_ssm_c_re, v_ssm_c_im, v_ssm_d, v_ssm_w_glu, v_ssm_b_glu, v_norm_final):
    given = dict(x=x, c=c, ctx=ctx, c_ctx=c_ctx, ada_w=ada_w, ada_b=ada_b, norm_mix=norm_mix, norm_ffn=norm_ffn, ffn_w1=ffn_w1, ffn_w3=ffn_w3, ffn_w2=ffn_w2, attn_w_in=attn_w_in, attn_w_out=attn_w_out, attn_rpb=attn_rpb, attn_sink=attn_sink, ssm_a_re=ssm_a_re, ssm_a_im=ssm_a_im, ssm_log_dt=ssm_log_dt, ssm_b_re=ssm_b_re, ssm_b_im=ssm_b_im, ssm_c_re=ssm_c_re, ssm_c_im=ssm_c_im, ssm_d=ssm_d, ssm_w_glu=ssm_w_glu, ssm_b_glu=ssm_b_glu, norm_final=norm_final, loss_target=loss_target, m_c_ctx=m_c_ctx, m_ada_w=m_ada_w, m_ada_b=m_ada_b, m_norm_mix=m_norm_mix, m_norm_ffn=m_norm_ffn, m_ffn_w1=m_ffn_w1, m_ffn_w3=m_ffn_w3, m_ffn_w2=m_ffn_w2, m_attn_w_in=m_attn_w_in, m_attn_w_out=m_attn_w_out, m_attn_rpb=m_attn_rpb, m_attn_sink=m_attn_sink, m_ssm_a_re=m_ssm_a_re, m_ssm_a_im=m_ssm_a_im, m_ssm_log_dt=m_ssm_log_dt, m_ssm_b_re=m_ssm_b_re, m_ssm_b_im=m_ssm_b_im, m_ssm_c_re=m_ssm_c_re, m_ssm_c_im=m_ssm_c_im, m_ssm_d=m_ssm_d, m_ssm_w_glu=m_ssm_w_glu, m_ssm_b_glu=m_ssm_b_glu, m_norm_final=m_norm_final, v_c_ctx=v_c_ctx, v_ada_w=v_ada_w, v_ada_b=v_ada_b, v_norm_mix=v_norm_mix, v_norm_ffn=v_norm_ffn, v_ffn_w1=v_ffn_w1, v_ffn_w3=v_ffn_w3, v_ffn_w2=v_ffn_w2, v_attn_w_in=v_attn_w_in, v_attn_w_out=v_attn_w_out, v_attn_rpb=v_attn_rpb, v_attn_sink=v_attn_sink, v_ssm_a_re=v_ssm_a_re, v_ssm_a_im=v_ssm_a_im, v_ssm_log_dt=v_ssm_log_dt, v_ssm_b_re=v_ssm_b_re, v_ssm_b_im=v_ssm_b_im, v_ssm_c_re=v_ssm_c_re, v_ssm_c_im=v_ssm_c_im, v_ssm_d=v_ssm_d, v_ssm_w_glu=v_ssm_w_glu, v_ssm_b_glu=v_ssm_b_glu, v_norm_final=v_norm_final)
    weights = {n: given[n] for n in TWIN_WEIGHTS}
    shared = {n: given[n] for n in SHARED_INPUTS}
    per_example = {n: given[n] for n in ['x', 'c', 'ctx']}
    grad_fn = _jax.value_and_grad(_loss, argnums=(0, 1))

    def one_microbatch(ex, loss_target):
        ex = dict(ex)
        diff = ex.pop(TWIN_DIFF_INPUT)
        return grad_fn(weights, diff, {**shared, **ex}, loss_target)

    if N_MICROBATCH == 1:
        loss, (grad_w, grad_x) = one_microbatch(per_example, given["loss_target"])
    else:
        def body(carry, xs):
            loss_sum, grad_sum = carry
            l_k, (gw_k, gx_k) = one_microbatch(xs[0], xs[1])
            with _jax.named_scope("update"):
                return (loss_sum + l_k, _jax.tree.map(_jnp.add, grad_sum, gw_k)), gx_k

        init = (_jnp.zeros((), _jnp.float32), _jax.tree.map(_jnp.zeros_like, weights))
        (loss, grad_w), grad_x = _jax.lax.scan(body, init, (per_example, given["loss_target"]))
    with _jax.named_scope("update"):
        delta_w, new_m, new_v = {}, {}, {}
        for n in TWIN_WEIGHTS:
            delta_w[n], new_m[n], new_v[n] = _adamw(weights[n], grad_w[n], given["m_" + n], given["v_" + n])
    return (loss, grad_x, *[grad_w[n] for n in TWIN_WEIGHTS], *[delta_w[n] for n in TWIN_WEIGHTS],
            *[new_m[n] for n in TWIN_WEIGHTS], *[new_v[n] for n in TWIN_WEIGHTS])
```

```python
import functools
import math

import numpy as np
import jax
import jax.numpy as jnp
from jax import lax
from jax.experimental import pallas as pl
from jax.experimental.pallas import tpu as pltpu

f32, bf16 = jnp.float32, jnp.bfloat16

HEAD_DIM = 128
GRID_W = 64
NA_HEADS = 8
NB_Q_HEADS = 8
NB_KV_HEADS = 2
NB_GROUP = NB_Q_HEADS // NB_KV_HEADS
NA_ROWS = 8
NA_COLS = 16
SW_RADIUS = 128
ROPE_BASE = 10000.0
SSM_GROUP = 16
SSM_STATE = 64
EPS = 1e-6
NEG_INF = -1e30
ADAM_LR, ADAM_B1, ADAM_B2, ADAM_EPS, ADAM_WD, ADAM_STEP = 0.001, 0.9, 0.999, 1e-08, 0.01, 10

N_DEV = 8
BLK = 256
SCAN_SEG = 8
GROUPS_PER_CHUNK = 8
V7X_VMEM_LIMIT = 56 * 2 ** 20

NT = (((1,), (1,)), ((), ()))
TN = (((0,), (0,)), ((), ()))


def _params(sem):
    return pltpu.CompilerParams(dimension_semantics=sem, vmem_limit_bytes=V7X_VMEM_LIMIT)


def _pick(n, cands):
    for c in cands:
        if n % c == 0:
            return c
    return n


def _mm(a, b, mode, out_dtype, name, a2=None, b2=None):
    if mode == "tn":
        kdim, m, n = a.shape[0], a.shape[1], b.shape[1]
    elif mode == "nt":
        m, kdim, n = a.shape[0], a.shape[1], b.shape[0]
    else:
        m, kdim, n = a.shape[0], a.shape[1], b.shape[1]
    tm = _pick(m, (1024, 768, 512, 256, 128))
    tn = _pick(n, (512, 256, 128))
    tk = _pick(kdim, (2048, 1024, 768, 512, 256, 128))
    nk = kdim // tk
    if mode == "tn":
        a_spec = pl.BlockSpec((tk, tm), lambda i, j, k: (k, i))
        b_spec = pl.BlockSpec((tk, tn), lambda i, j, k: (k, j))
        dims = TN
    elif mode == "nt":
        a_spec = pl.BlockSpec((tm, tk), lambda i, j, k: (i, k))
        b_spec = pl.BlockSpec((tn, tk), lambda i, j, k: (j, k))
        dims = NT
    else:
        a_spec = pl.BlockSpec((tm, tk), lambda i, j, k: (i, k))
        b_spec = pl.BlockSpec((tk, tn), lambda i, j, k: (k, j))
        dims = (((1,), (0,)), ((), ()))

    pairs = 1 if a2 is None else 2

    def body(*refs):
        o_ref, acc_ref = refs[2 * pairs:]
        k = pl.program_id(2)

        @pl.when(k == 0)
        def _():
            acc_ref[...] = jnp.zeros_like(acc_ref)

        for p in range(pairs):
            acc_ref[...] += lax.dot_general(refs[2 * p][...].astype(bf16), refs[2 * p + 1][...].astype(bf16), dims,
                                            preferred_element_type=f32)

        @pl.when(k == nk - 1)
        def _():
            o_ref[...] = acc_ref[...].astype(o_ref.dtype)

    args = (a, b) if a2 is None else (a, b, a2, b2)
    return pl.pallas_call(
        body, name=name, grid=(m // tm, n // tn, nk),
        in_specs=[a_spec, b_spec] * pairs, out_specs=pl.BlockSpec((tm, tn), lambda i, j, k: (i, j)),
        out_shape=jax.ShapeDtypeStruct((m, n), out_dtype),
        scratch_shapes=[pltpu.VMEM((tm, tn), f32)],
        compiler_params=_params(("parallel", "parallel", "arbitrary")),
    )(*args)


def _small_dot(a, b, name):
    def body(a_ref, b_ref, o_ref):
        o_ref[...] = jnp.dot(a_ref[...], b_ref[...], precision=lax.Precision.HIGHEST, preferred_element_type=f32)

    return pl.pallas_call(body, name=name, out_shape=jax.ShapeDtypeStruct((a.shape[0], b.shape[1]), f32))(a, b)


def _group_of(i, n_x_tiles, n_groups):
    return jnp.where(i >= n_x_tiles, n_groups - 1, 0)


def _rowwise(f, rows, vecs, outs, *, nrows, n_x_rows, name, tm=BLK, tc=None):
    n_x_tiles = n_x_rows // tm
    grid = (nrows // tm,) if tc is None else (nrows // tm, rows[0].shape[1] // tc)

    def rspec(cols):
        if tc is None:
            return pl.BlockSpec((tm, cols), lambda i: (i, 0))
        return pl.BlockSpec((tm, tc), lambda i, j: (i, j))

    def vspec(v):
        g = v.shape[0]
        if tc is None:
            return pl.BlockSpec((None, 1, v.shape[2]), lambda i: (_group_of(i, n_x_tiles, g), 0, 0))
        return pl.BlockSpec((None, 1, tc), lambda i, j: (_group_of(i, n_x_tiles, g), 0, j))

    nr, nv = len(rows), len(vecs)

    def body(*refs):
        ins = [r[...] for r in refs[:nr + nv]]
        res = f(*ins)
        for o_ref, val in zip(refs[nr + nv:], res):
            o_ref[...] = val.astype(o_ref.dtype)

    return pl.pallas_call(
        body, name=name, grid=grid,
        in_specs=[rspec(r.shape[1]) for r in rows] + [vspec(v) for v in vecs],
        out_specs=[rspec(c) for c, _ in outs],
        out_shape=[jax.ShapeDtypeStruct((nrows, c), d) for c, d in outs],
        compiler_params=_params(("parallel",) * len(grid)),
    )(*rows, *vecs)


def _rowwise_bwd(f, rows, vecs, cts, row_grads, *, nrows, n_x_rows, name, tm=BLK, tc=None, residual=None,
                 residual_rows=None):
    n_x_tiles = n_x_rows // tm
    n_tiles = nrows // tm
    res_tiles = None if residual_rows is None else residual_rows // tm
    grid = (n_tiles,) if tc is None else (rows[0].shape[1] // tc, n_tiles)
    row_of = (lambda *g: g[0]) if tc is None else (lambda *g: g[1])

    def rspec(cols):
        if tc is None:
            return pl.BlockSpec((tm, cols), lambda i: (i, 0))
        return pl.BlockSpec((tm, tc), lambda j, i: (i, j))

    def vspec(v):
        g = v.shape[0]
        if tc is None:
            return pl.BlockSpec((None, 1, v.shape[2]), lambda i: (_group_of(i, n_x_tiles, g), 0, 0))
        return pl.BlockSpec((None, 1, tc), lambda j, i: (_group_of(i, n_x_tiles, g), 0, j))

    nr, nv, nc = len(rows), len(vecs), len(cts)
    gidx = sorted(row_grads)

    def body(*refs):
        i = row_of(*[pl.program_id(d) for d in range(len(grid))])
        row_vals = [r[...].astype(f32) for r in refs[:nr]]
        vec_vals = [jnp.broadcast_to(r[...].astype(f32), (tm, r.shape[-1])) for r in refs[nr:nr + nv]]
        ct_vals = [r[...].astype(f32) for r in refs[nr + nv:nr + nv + nc]]
        out_refs = refs[nr + nv + nc + (residual is not None):]
        res, vjp = jax.vjp(lambda *a: tuple(o.astype(f32) for o in f(*a)), *row_vals, *vec_vals)
        grads = list(vjp(tuple(ct_vals)))
        if residual is not None:
            extra = refs[nr + nv + nc][...]
            if res_tiles is not None:
                extra = jnp.where(i < res_tiles, extra, 0.0)
            grads[0] = grads[0] + extra
        for o_ref, k in zip(out_refs[:len(gidx)], gidx):
            o_ref[...] = grads[k].astype(o_ref.dtype)
        for o_ref, g, v in zip(out_refs[len(gidx):], grads[nr:], vecs):
            part = jnp.sum(g, axis=0, keepdims=True)
            first = (i == 0) if v.shape[0] == 1 else ((i == 0) | (i == n_x_tiles))

            @pl.when(first)
            def _():
                o_ref[...] = part

            @pl.when(jnp.logical_not(first))
            def _():
                o_ref[...] += part

    sem = ("arbitrary",) if tc is None else ("parallel", "arbitrary")
    res_specs, res_args = [], []
    if residual is not None:
        assert tc is None
        clamp = (lambda i: i) if res_tiles is None else (lambda i: jnp.minimum(i, res_tiles - 1))
        res_specs = [pl.BlockSpec((tm, residual.shape[1]), lambda i: (clamp(i), 0))]
        res_args = [residual]
    return pl.pallas_call(
        body, name=name, grid=grid,
        in_specs=[rspec(r.shape[1]) for r in rows] + [vspec(v) for v in vecs] + [rspec(c.shape[1]) for c in cts]
        + res_specs,
        out_specs=[rspec(rows[k].shape[1]) for k in gidx] + [vspec(v) for v in vecs],
        out_shape=[jax.ShapeDtypeStruct((nrows, rows[k].shape[1]), row_grads[k]) for k in gidx]
        + [jax.ShapeDtypeStruct(v.shape, f32) for v in vecs],
        compiler_params=_params(sem),
    )(*rows, *vecs, *cts, *res_args)


def _rms(x, w):
    return x * lax.rsqrt(jnp.mean(x * x, axis=-1, keepdims=True) + EPS) * w


def _f_normmod(x, w, sc, sh):
    return (_rms(x.astype(f32), w) * (1.0 + sc) + sh,)


def _f_gated_add(x, y, g):
    return (x + g * y.astype(f32),)


def _f_swiglu(a, b):
    a = a.astype(f32)
    return (jax.nn.silu(a) * b.astype(f32),)


def _f_glu_pre(u, yf, yr, d):
    return (jax.nn.gelu(d * u + yf + yr),)


def _f_glu_post(x, zv, zg, g, bv, bg):
    return (x + g * ((zv.astype(f32) + bv) * jax.nn.sigmoid(zg.astype(f32) + bg)),)


def _f_scale(y, g):
    return (g * y.astype(f32),)


def _f_silu(x):
    return (jax.nn.silu(x.astype(f32)),)


def _loss_head(x, tgt, w, *, nrows, name):
    d = x.shape[1]
    tm = BLK

    def body(x_ref, t_ref, w_ref, dx_ref, dw_ref, loss_ref):
        i = pl.program_id(0)
        wb = jnp.broadcast_to(w_ref[...], (tm, d))
        y, vjp = jax.vjp(_rms, x_ref[...], wb)
        e = y - t_ref[...]
        dx, dwb = vjp(e * (1.0 / d))
        dx_ref[...] = dx
        dw = jnp.sum(dwb, axis=0, keepdims=True)
        part = jnp.full((8, 128), 0.5 / d, f32) * jnp.sum(e * e)

        @pl.when(i == 0)
        def _():
            dw_ref[...] = dw
            loss_ref[...] = part

        @pl.when(i > 0)
        def _():
            dw_ref[...] += dw
            loss_ref[...] += part

    row = pl.BlockSpec((tm, d), lambda i: (i, 0))
    return pl.pallas_call(
        body, name=name, grid=(nrows // tm,),
        in_specs=[row, row, pl.BlockSpec((1, d), lambda i: (0, 0))],
        out_specs=[row, pl.BlockSpec((1, d), lambda i: (0, 0)), pl.BlockSpec((8, 128), lambda i: (0, 0))],
        out_shape=[jax.ShapeDtypeStruct((nrows, d), f32), jax.ShapeDtypeStruct((1, d), f32),
                   jax.ShapeDtypeStruct((8, 128), f32)],
        compiler_params=_params(("arbitrary",)),
    )(x, tgt, w)


def _adamw(w, g, m, v, name):
    r, c = w.shape
    tr = _pick(r, (512, 256, 128, 64, 32, 16, 8))
    tcol = _pick(c, (1024, 512)) if c % 128 == 0 else c

    def body(w_ref, g_ref, m_ref, v_ref, d_ref, m2_ref, v2_ref):
        gg = g_ref[...]
        m2 = ADAM_B1 * m_ref[...] + (1.0 - ADAM_B1) * gg
        v2 = ADAM_B2 * v_ref[...] + (1.0 - ADAM_B2) * (gg * gg)
        m_hat = m2 / (1.0 - ADAM_B1 ** ADAM_STEP)
        v_hat = v2 / (1.0 - ADAM_B2 ** ADAM_STEP)
        d_ref[...] = -ADAM_LR * (m_hat / (jnp.sqrt(v_hat) + ADAM_EPS) + ADAM_WD * w_ref[...])
        m2_ref[...] = m2
        v2_ref[...] = v2

    spec = pl.BlockSpec((tr, tcol), lambda i, j: (i, j))
    return pl.pallas_call(
        body, name=name, grid=(r // tr, c // tcol), in_specs=[spec] * 4, out_specs=[spec] * 3,
        out_shape=[jax.ShapeDtypeStruct((r, c), f32)] * 3,
        compiler_params=_params(("parallel", "parallel")),
    )(w, g, m, v)


def _swap_quarters(x):
    lane = lax.broadcasted_iota(jnp.int32, x.shape, 1)
    return jnp.where((lane & 63) < 32, pltpu.roll(x, 96, 1), pltpu.roll(x, 32, 1))


def _rope_tables(n_x, n_ctx):
    t = np.arange(n_x)
    quarter = HEAD_DIM // 4
    inv = ROPE_BASE ** (-np.arange(quarter, dtype=np.float64) / quarter)
    ar = (t // GRID_W)[:, None] * inv[None]
    ac = (t % GRID_W)[:, None] * inv[None]
    cos = np.concatenate([np.cos(ar), np.cos(ar), np.cos(ac), np.cos(ac)], axis=1)
    sin = np.concatenate([-np.sin(ar), np.sin(ar), -np.sin(ac), np.sin(ac)], axis=1)
    cos = np.concatenate([cos, np.ones((n_ctx, HEAD_DIM))], axis=0)
    sin = np.concatenate([sin, np.zeros((n_ctx, HEAD_DIM))], axis=0)
    return jnp.asarray(cos, f32), jnp.asarray(sin, f32)


A_W = NA_HEADS * HEAD_DIM
QB0, KB0, VB0 = 3 * A_W, 3 * A_W + NB_Q_HEADS * HEAD_DIM, 3 * A_W + (NB_Q_HEADS + NB_KV_HEADS) * HEAD_DIM
IN_W = VB0 + NB_KV_HEADS * HEAD_DIM


def _qkv_post(qkv, cos, sin, name):
    n = qkv.shape[0]

    def body(x_ref, c_ref, s_ref, o_ref):
        c, s = c_ref[...], s_ref[...]
        o_ref[:, :QB0] = x_ref[:, :QB0].astype(bf16)
        for col in range(QB0, VB0, HEAD_DIM):
            x = x_ref[:, col:col + HEAD_DIM]
            o_ref[:, col:col + HEAD_DIM] = (x * c + _swap_quarters(x) * s).astype(bf16)
        o_ref[:, VB0:] = x_ref[:, VB0:].astype(bf16)

    row = lambda c: pl.BlockSpec((BLK, c), lambda i: (i, 0))
    return pl.pallas_call(
        body, name=name, grid=(n // BLK,), in_specs=[row(IN_W), row(HEAD_DIM), row(HEAD_DIM)],
        out_specs=row(IN_W), out_shape=jax.ShapeDtypeStruct((n, IN_W), bf16),
        compiler_params=_params(("parallel",)),
    )(qkv, cos, sin)


def _qkv_post_bwd(parts, cos, sin, name):
    n = parts[0].shape[0]

    def body(qa, ka, va, qb, kb, vb, c_ref, s_ref, o_ref):
        c, s = c_ref[...], s_ref[...]
        o_ref[:, 0:A_W] = qa[...]
        o_ref[:, A_W:2 * A_W] = ka[...]
        o_ref[:, 2 * A_W:QB0] = va[...]
        for src, col0, width in ((qb, QB0, KB0 - QB0), (kb, KB0, VB0 - KB0)):
            for off in range(0, width, HEAD_DIM):
                g = src[:, off:off + HEAD_DIM].astype(f32)
                o_ref[:, col0 + off:col0 + off + HEAD_DIM] = (g * c + _swap_quarters(g * s)).astype(bf16)
        o_ref[:, VB0:] = vb[...]

    row = lambda c: pl.BlockSpec((BLK, c), lambda i: (i, 0))
    return pl.pallas_call(
        body, name=name, grid=(n // BLK,),
        in_specs=[row(p.shape[1]) for p in parts] + [row(HEAD_DIM), row(HEAD_DIM)],
        out_specs=row(IN_W), out_shape=jax.ShapeDtypeStruct((n, IN_W), bf16),
        compiler_params=_params(("parallel",)),
    )(*parts, cos, sin)


def _valid(kind, qpos, kpos, n_x):
    ok = (kpos >= 0) & (kpos < n_x) & (qpos >= 0) & (qpos < n_x)
    if kind == "na":
        rows = n_x // GRID_W
        qr, qc = lax.shift_right_arithmetic(qpos, 6), qpos & (GRID_W - 1)
        kr, kc = lax.shift_right_arithmetic(kpos, 6), kpos & (GRID_W - 1)
        kr0 = jnp.clip(qr - NA_ROWS // 2, 0, rows - NA_ROWS)
        ws = jnp.clip(qc - NA_COLS // 2, 0, GRID_W - NA_COLS)
        return ok & (kr >= kr0) & (kr < kr0 + NA_ROWS) & (kc >= ws) & (kc < ws + NA_COLS)
    return ok & (jnp.abs(kpos - qpos) <= SW_RADIUS)


def _stack_heads(x, g):
    if g == 1:
        return x
    return jnp.concatenate([x[:, a * HEAD_DIM:(a + 1) * HEAD_DIM] for a in range(g)], axis=0)


def _unstack_heads(x, g):
    if g == 1:
        return x
    r = x.shape[0] // g
    return jnp.concatenate([x[a * r:(a + 1) * r] for a in range(g)], axis=1)


def _tile_rows(x, g):
    return x if g == 1 else jnp.concatenate([x] * g, axis=0)


class _AttnCfg:
    def __init__(self, kind, n_x, n_tot):
        self.kind, self.n_x, self.n_tot = kind, n_x, n_tot
        self.n_xb, self.n_blk = n_x // BLK, n_tot // BLK
        if kind == "na":
            self.g, self.nkv, self.q0, self.k0, self.v0 = 1, NA_HEADS, 0, NA_HEADS, 2 * NA_HEADS
        else:
            self.g, self.nkv = NB_GROUP, NB_KV_HEADS
            self.q0, self.k0, self.v0 = QB0 // (NB_GROUP * HEAD_DIM), KB0 // HEAD_DIM, VB0 // HEAD_DIM
        self.r = BLK * self.g
        self.qw = HEAD_DIM * self.g
        self.scale = HEAD_DIM ** -0.5


def _attn_fwd(qkv, cfg, extra, name):
    g, r, qw, n_xb, n_x = cfg.g, cfg.r, cfg.qw, cfg.n_xb, cfg.n_x
    last = n_xb - 1

    def body(q_ref, kp, ko, kn, vp, vo, vn, kc_ref, vc_ref, ex_ref, o_ref, lse_ref):
        i = pl.program_id(1)
        q = _stack_heads(q_ref[...], g)
        kw = jnp.concatenate([kp[...], ko[...], kn[...]], axis=0)
        vw = jnp.concatenate([vp[...], vo[...], vn[...]], axis=0)
        s = lax.dot_general(q, kw, NT, preferred_element_type=f32) * cfg.scale
        if cfg.kind == "na":
            s = s + ex_ref[...]
        qpos = i * BLK + lax.broadcasted_iota(jnp.int32, (BLK, 3 * BLK), 0)
        kpos = (i - 1) * BLK + lax.broadcasted_iota(jnp.int32, (BLK, 3 * BLK), 1)
        s = jnp.where(_tile_rows(_valid(cfg.kind, qpos, kpos, n_x), g), s, NEG_INF)
        sc = lax.dot_general(q, kc_ref[...], NT, preferred_element_type=f32) * cfg.scale
        m = jnp.maximum(jnp.max(s, axis=-1, keepdims=True), jnp.max(sc, axis=-1, keepdims=True))
        if cfg.kind == "swa":
            m = jnp.maximum(m, ex_ref[...])
        p, pc = jnp.exp(s - m), jnp.exp(sc - m)
        l = jnp.sum(p, axis=-1, keepdims=True) + jnp.sum(pc, axis=-1, keepdims=True)
        if cfg.kind == "swa":
            l = l + jnp.exp(ex_ref[...] - m)
        o = jnp.dot(p.astype(bf16), vw, preferred_element_type=f32) + jnp.dot(pc.astype(bf16), vc_ref[...],
                                                                             preferred_element_type=f32)
        o_ref[...] = _unstack_heads(o / l, g).astype(bf16)
        lse_ref[...] = m + jnp.log(l)

    kv = lambda col0, f: pl.BlockSpec((BLK, HEAD_DIM), lambda h, i: (f(i), col0 + h))
    prev = lambda i: jnp.clip(i - 1, 0, last)
    own = lambda i: jnp.minimum(i, last)
    nxt = lambda i: jnp.minimum(i + 1, last)
    ctx = lambda i: n_xb
    if cfg.kind == "na":
        ex_spec = pl.BlockSpec((None, BLK, 3 * BLK), lambda h, i: (h, 0, 0))
    else:
        ex_spec = pl.BlockSpec((None, r, 1), lambda h, i: (h, 0, 0))
    return pl.pallas_call(
        body, name=name, grid=(cfg.nkv, cfg.n_blk),
        in_specs=[pl.BlockSpec((BLK, qw), lambda h, i: (i, cfg.q0 + h)),
                  kv(cfg.k0, prev), kv(cfg.k0, own), kv(cfg.k0, nxt),
                  kv(cfg.v0, prev), kv(cfg.v0, own), kv(cfg.v0, nxt),
                  kv(cfg.k0, ctx), kv(cfg.v0, ctx), ex_spec],
        out_specs=[pl.BlockSpec((BLK, qw), lambda h, i: (i, h)),
                   pl.BlockSpec((None, None, r, 1), lambda h, i: (h, i, 0, 0))],
        out_shape=[jax.ShapeDtypeStruct((cfg.n_tot, cfg.nkv * qw), bf16),
                   jax.ShapeDtypeStruct((cfg.nkv, cfg.n_blk, r, 1), f32)],
        compiler_params=_params(("parallel", "parallel")),
    )(qkv, qkv, qkv, qkv, qkv, qkv, qkv, qkv, qkv, extra)


def _attn_bwd_q(qkv, o, do, lse, cfg, extra, name):
    g, r, qw, n_xb, n_x = cfg.g, cfg.r, cfg.qw, cfg.n_xb, cfg.n_x
    last = n_xb - 1
    do_col0 = 0 if cfg.kind == "na" else (NA_HEADS * HEAD_DIM) // qw

    def body(q_ref, kp, ko, kn, vp, vo, vn, kc_ref, vc_ref, ex_ref, o_ref, do_ref, lse_ref,
             dq_ref, delta_ref, dex_ref):
        i = pl.program_id(1)
        q = _stack_heads(q_ref[...], g)
        dout = _stack_heads(do_ref[...], g)
        out = _stack_heads(o_ref[...], g)
        delta = jnp.sum(dout.astype(f32) * out.astype(f32), axis=-1, keepdims=True)
        delta_ref[...] = delta
        kw = jnp.concatenate([kp[...], ko[...], kn[...]], axis=0)
        vw = jnp.concatenate([vp[...], vo[...], vn[...]], axis=0)
        s = lax.dot_general(q, kw, NT, preferred_element_type=f32) * cfg.scale
        if cfg.kind == "na":
            s = s + ex_ref[...]
        qpos = i * BLK + lax.broadcasted_iota(jnp.int32, (BLK, 3 * BLK), 0)
        kpos = (i - 1) * BLK + lax.broadcasted_iota(jnp.int32, (BLK, 3 * BLK), 1)
        s = jnp.where(_tile_rows(_valid(cfg.kind, qpos, kpos, n_x), g), s, NEG_INF)
        sc = lax.dot_general(q, kc_ref[...], NT, preferred_element_type=f32) * cfg.scale
        lse_v = lse_ref[...]
        p, pc = jnp.exp(s - lse_v), jnp.exp(sc - lse_v)
        dp = lax.dot_general(dout, vw, NT, preferred_element_type=f32)
        dpc = lax.dot_general(dout, vc_ref[...], NT, preferred_element_type=f32)
        ds, dsc = p * (dp - delta), pc * (dpc - delta)
        dq = jnp.dot(ds.astype(bf16), kw, preferred_element_type=f32) + jnp.dot(dsc.astype(bf16), kc_ref[...],
                                                                             preferred_element_type=f32)
        dq_ref[...] = _unstack_heads(dq * cfg.scale, g).astype(bf16)
        dex = ds if cfg.kind == "na" else -jnp.exp(ex_ref[...] - lse_v) * delta

        @pl.when(i == 0)
        def _():
            dex_ref[...] = dex

        @pl.when(i > 0)
        def _():
            dex_ref[...] += dex

    kv = lambda col0, f: pl.BlockSpec((BLK, HEAD_DIM), lambda h, i: (f(i), col0 + h))
    prev = lambda i: jnp.clip(i - 1, 0, last)
    own = lambda i: jnp.minimum(i, last)
    nxt = lambda i: jnp.minimum(i + 1, last)
    ctx = lambda i: n_xb
    if cfg.kind == "na":
        ex_spec = pl.BlockSpec((None, BLK, 3 * BLK), lambda h, i: (h, 0, 0))
    else:
        ex_spec = pl.BlockSpec((None, r, 1), lambda h, i: (h, 0, 0))
    stat = pl.BlockSpec((None, None, r, 1), lambda h, i: (h, i, 0, 0))
    return pl.pallas_call(
        body, name=name, grid=(cfg.nkv, cfg.n_blk),
        in_specs=[pl.BlockSpec((BLK, qw), lambda h, i: (i, cfg.q0 + h)),
                  kv(cfg.k0, prev), kv(cfg.k0, own), kv(cfg.k0, nxt),
                  kv(cfg.v0, prev), kv(cfg.v0, own), kv(cfg.v0, nxt),
                  kv(cfg.k0, ctx), kv(cfg.v0, ctx), ex_spec,
                  pl.BlockSpec((BLK, qw), lambda h, i: (i, h)),
                  pl.BlockSpec((BLK, qw), lambda h, i: (i, do_col0 + h)), stat],
        out_specs=[pl.BlockSpec((BLK, qw), lambda h, i: (i, h)), stat, ex_spec],
        out_shape=[jax.ShapeDtypeStruct((cfg.n_tot, cfg.nkv * qw), bf16),
                   jax.ShapeDtypeStruct((cfg.nkv, cfg.n_blk, r, 1), f32),
                   jax.ShapeDtypeStruct(extra.shape, f32)],
        compiler_params=_params(("parallel", "arbitrary")),
    )(qkv, qkv, qkv, qkv, qkv, qkv, qkv, qkv, qkv, extra, o, do, lse)


def _attn_bwd_kv(qkv, do, lse, delta, cfg, bias_t, name):
    g, r, qw, n_xb, n_x, n_blk = cfg.g, cfg.r, cfg.qw, cfg.n_xb, cfg.n_x, cfg.n_blk
    last = n_xb - 1
    do_col0 = 0 if cfg.kind == "na" else (NA_HEADS * HEAD_DIM) // qw
    has_bias = cfg.kind == "na"

    def body(*refs):
        qs, dos, lses, dels = refs[0:3], refs[3:6], refs[6:9], refs[9:12]
        qj_ref, doj_ref, lsej_ref, delj_ref, k_ref, v_ref, kc_ref, vc_ref = refs[12:20]
        rest = refs[20:]
        if has_bias:
            b_ref, rest = rest[0], rest[1:]
        dk_ref, dv_ref, dkc_acc, dvc_acc = rest
        j = pl.program_id(1)

        @pl.when(j == 0)
        def _():
            dkc_acc[...] = jnp.zeros_like(dkc_acc)
            dvc_acc[...] = jnp.zeros_like(dvc_acc)

        qj, doj = _stack_heads(qj_ref[...], g), _stack_heads(doj_ref[...], g)
        sc = lax.dot_general(qj, kc_ref[...], NT, preferred_element_type=f32) * cfg.scale
        pc = jnp.exp(sc - lsej_ref[...])
        dvc_acc[...] += lax.dot_general(pc.astype(bf16), doj, TN, preferred_element_type=f32)
        dpc = lax.dot_general(doj, vc_ref[...], NT, preferred_element_type=f32)
        dsc = pc * (dpc - delj_ref[...])
        dkc_acc[...] += lax.dot_general(dsc.astype(bf16), qj, TN, preferred_element_type=f32) * cfg.scale

        @pl.when(j < n_xb)
        def _():
            qw_all = jnp.concatenate([_stack_heads(x[...], g) for x in qs], axis=0)
            do_all = jnp.concatenate([_stack_heads(x[...], g) for x in dos], axis=0)
            lse_all = jnp.concatenate([x[...] for x in lses], axis=0)
            del_all = jnp.concatenate([x[...] for x in dels], axis=0)
            s = lax.dot_general(qw_all, k_ref[...], NT, preferred_element_type=f32) * cfg.scale
            if has_bias:
                s = s + b_ref[...]
            kpos = j * BLK + lax.broadcasted_iota(jnp.int32, (BLK, BLK), 1)
            masks = []
            for w in range(3):
                qpos = (j - 1 + w) * BLK + lax.broadcasted_iota(jnp.int32, (BLK, BLK), 0)
                masks.append(_tile_rows(_valid(cfg.kind, qpos, kpos, n_x), g))
            s = jnp.where(jnp.concatenate(masks, axis=0), s, NEG_INF)
            p = jnp.exp(s - lse_all)
            dv_ref[...] = lax.dot_general(p.astype(bf16), do_all, TN, preferred_element_type=f32).astype(bf16)
            dp = lax.dot_general(do_all, v_ref[...], NT, preferred_element_type=f32)
            ds = p * (dp - del_all)
            dk_ref[...] = (lax.dot_general(ds.astype(bf16), qw_all, TN, preferred_element_type=f32)
                           * cfg.scale).astype(bf16)

        @pl.when(j == n_xb)
        def _():
            dk_ref[...] = dkc_acc[...].astype(bf16)
            dv_ref[...] = dvc_acc[...].astype(bf16)

    prev = lambda j: jnp.clip(j - 1, 0, last)
    own = lambda j: jnp.minimum(j, last)
    nxt = lambda j: jnp.minimum(j + 1, last)
    same = lambda j: j
    qspec = lambda f: pl.BlockSpec((BLK, qw), lambda h, j: (f(j), cfg.q0 + h))
    dospec = lambda f: pl.BlockSpec((BLK, qw), lambda h, j: (f(j), do_col0 + h))
    stat = lambda f: pl.BlockSpec((None, None, r, 1), lambda h, j: (h, f(j), 0, 0))
    kv = lambda col0, f: pl.BlockSpec((BLK, HEAD_DIM), lambda h, j: (f(j), col0 + h))
    ctx = lambda j: n_xb
    in_specs = ([qspec(f) for f in (prev, own, nxt)] + [dospec(f) for f in (prev, own, nxt)]
                + [stat(f) for f in (prev, own, nxt)] * 2
                + [qspec(same), dospec(same), stat(same), stat(same),
                   kv(cfg.k0, same), kv(cfg.v0, same), kv(cfg.k0, ctx), kv(cfg.v0, ctx)])
    args = [qkv] * 3 + [do] * 3 + [lse] * 3 + [delta] * 3 + [qkv, do, lse, delta, qkv, qkv, qkv, qkv]
    if has_bias:
        in_specs.append(pl.BlockSpec((None, 3 * BLK, BLK), lambda h, j: (h, 0, 0)))
        args.append(bias_t)
    out = pl.BlockSpec((BLK, HEAD_DIM), lambda h, j: (j, h))
    return pl.pallas_call(
        body, name=name, grid=(cfg.nkv, n_blk), in_specs=in_specs, out_specs=[out, out],
        out_shape=[jax.ShapeDtypeStruct((cfg.n_tot, cfg.nkv * HEAD_DIM), bf16)] * 2,
        scratch_shapes=[pltpu.VMEM((BLK, HEAD_DIM), f32)] * 2,
        compiler_params=_params(("parallel", "arbitrary")),
    )(*args)


def _rpb_index(transposed):
    if transposed:
        return np.array([[a - b + 11 for a in range(4)] for b in range(12)])
    return np.array([[b - a + 3 for b in range(12)] for a in range(4)])


def _toeplitz_basis():
    qc, kc = np.meshgrid(np.arange(GRID_W), np.arange(GRID_W), indexing="ij")
    e = (kc - qc + NA_COLS - 1)[None] == np.arange(2 * NA_COLS - 1)[:, None, None]
    return e.reshape(2 * NA_COLS - 1, GRID_W * GRID_W).astype(np.float32)


def _whole(f, ins, outs, name):
    def body(*refs):
        res = f(*[r[...] for r in refs[:len(ins)]])
        for o_ref, val in zip(refs[len(ins):], res):
            o_ref[...] = val.astype(o_ref.dtype)

    return pl.pallas_call(body, name=name,
                          out_shape=[jax.ShapeDtypeStruct(s, d) for s, d in outs])(*ins)


def _whole_bwd(f, ins, cts, name):
    n = len(ins)

    def body(*refs):
        _, vjp = jax.vjp(f, *[r[...] for r in refs[:n]])
        grads = vjp(tuple(r[...] for r in refs[n:n + len(cts)]))
        for o_ref, g in zip(refs[n + len(cts):], grads):
            o_ref[...] = g

    return pl.pallas_call(body, name=name,
                          out_shape=[jax.ShapeDtypeStruct(a.shape, f32) for a in ins])(*ins, *cts)


def _f_discretise(ar, ai, ldt, br, bi):
    dt = jnp.exp(ldt)
    mag = jnp.exp(ar * dt)
    lam_r, lam_i = mag * jnp.cos(ai * dt), mag * jnp.sin(ai * dt)
    den = ar * ar + ai * ai
    nr = lam_r - 1.0
    coef_r = (nr * ar + lam_i * ai) / den
    coef_i = (lam_i * ar - nr * ai) / den
    return (lam_r, lam_i, coef_r[None] * br - coef_i[None] * bi, coef_r[None] * bi + coef_i[None] * br)


SEG_LEN = BLK // SCAN_SEG
N_STATE = GROUPS_PER_CHUNK * SSM_STATE


def _scan_perm(reverse):
    r = np.arange(BLK)
    t = (r % SCAN_SEG) * SEG_LEN + r // SCAN_SEG
    if reverse:
        t = BLK - 1 - t
    pm = np.zeros((BLK, BLK), np.float32)
    pm[r, t] = 1.0
    return jnp.asarray(pm, bf16), jnp.asarray(pm.T, bf16)


def _block_order(n_xb, n_blk, reverse):
    n_cb = n_blk - n_xb
    if reverse:
        return lambda cc: jnp.where(cc < n_cb, n_blk - 1 - cc, n_xb - 1 - (cc - n_cb))
    return lambda cc: jnp.where(cc < n_cb, n_xb + cc, cc - n_cb)


def _unpermute(pmt, y):
    hi = y.astype(bf16)
    lo = (y - hi.astype(f32)).astype(bf16)
    return jnp.dot(pmt, hi, preferred_element_type=f32) + jnp.dot(pmt, lo, preferred_element_type=f32)


def _lam_pow(lr, li):
    for _ in range(int(math.log2(SEG_LEN))):
        lr, li = lr * lr - li * li, 2.0 * lr * li
    return lr, li


def _s5_fwd(u, lam, bblk, cblk, n_x, reverse, name):
    n_tot, d = u.shape
    nq, n_blk, n_xb = d // 128, n_tot // BLK, n_x // BLK
    order = _block_order(n_xb, n_blk, reverse)
    pm, pmt = _scan_perm(reverse)
    ns = N_STATE

    def body(u_ref, pm_ref, pmt_ref, lam_ref, b_ref, c_ref, y_ref, cp_ref, bu_ref, st_ref, carry_ref):
        cc = pl.program_id(1)

        @pl.when(cc == 0)
        def _():
            carry_ref[...] = jnp.zeros_like(carry_ref)

        up = jnp.dot(pm_ref[...], u_ref[...].astype(bf16), preferred_element_type=f32).astype(bf16)
        bu_ref[0] = jnp.dot(up, b_ref[0], preferred_element_type=f32)
        bu_ref[1] = jnp.dot(up, b_ref[1], preferred_element_type=f32)
        lr, li = lam_ref[0:1, :], lam_ref[1:2, :]
        lrb, lib = jnp.broadcast_to(lr, (SCAN_SEG, ns)), jnp.broadcast_to(li, (SCAN_SEG, ns))

        def step(j, s, store):
            sr, si = s
            off = pl.multiple_of(j * SCAN_SEG, SCAN_SEG)
            nsr = lrb * sr - lib * si + bu_ref[0, pl.ds(off, SCAN_SEG), :]
            nsi = lrb * si + lib * sr + bu_ref[1, pl.ds(off, SCAN_SEG), :]
            if store:
                st_ref[0, pl.ds(off, SCAN_SEG), :] = nsr
                st_ref[1, pl.ds(off, SCAN_SEG), :] = nsi
            return nsr, nsi

        zero = jnp.zeros((SCAN_SEG, ns), f32)
        er, ei = lax.fori_loop(0, SEG_LEN, lambda j, s: step(j, s, False), (zero, zero))
        pr, pi = _lam_pow(lr, li)
        cr, ci = carry_ref[0, 0:1, :], carry_ref[1, 0:1, :]
        rows_r, rows_i = [], []
        for k in range(SCAN_SEG):
            rows_r.append(cr)
            rows_i.append(ci)
            cr, ci = pr * cr - pi * ci + er[k:k + 1], pr * ci + pi * cr + ei[k:k + 1]
        carry_ref[0] = jnp.broadcast_to(cr, (SCAN_SEG, ns))
        carry_ref[1] = jnp.broadcast_to(ci, (SCAN_SEG, ns))
        cpr, cpi = jnp.concatenate(rows_r, axis=0), jnp.concatenate(rows_i, axis=0)
        cp_ref[0] = cpr
        cp_ref[1] = cpi
        lax.fori_loop(0, SEG_LEN, lambda j, s: step(j, s, True), (cpr, cpi))
        yp = (jnp.dot(st_ref[0].astype(bf16), c_ref[0], preferred_element_type=f32)
              - jnp.dot(st_ref[1].astype(bf16), c_ref[1], preferred_element_type=f32))
        y_ref[...] = _unpermute(pmt_ref[...], yp)

    full = lambda shape: pl.BlockSpec(shape, lambda q, cc: (0,) * len(shape))
    return pl.pallas_call(
        body, name=name, grid=(nq, n_blk),
        in_specs=[pl.BlockSpec((BLK, 128), lambda q, cc: (order(cc), q)), full((BLK, BLK)), full((BLK, BLK)),
                  pl.BlockSpec((None, 2, ns), lambda q, cc: (q, 0, 0)),
                  pl.BlockSpec((None, 2, 128, ns), lambda q, cc: (q, 0, 0, 0)),
                  pl.BlockSpec((None, 2, ns, 128), lambda q, cc: (q, 0, 0, 0))],
        out_specs=[pl.BlockSpec((BLK, 128), lambda q, cc: (order(cc), q)),
                   pl.BlockSpec((None, None, 2, SCAN_SEG, ns), lambda q, cc: (q, cc, 0, 0, 0))],
        out_shape=[jax.ShapeDtypeStruct((n_tot, d), f32),
                   jax.ShapeDtypeStruct((nq, n_blk, 2, SCAN_SEG, ns), f32)],
        scratch_shapes=[pltpu.VMEM((2, BLK, ns), f32), pltpu.VMEM((2, BLK, ns), f32),
                        pltpu.VMEM((2, SCAN_SEG, ns), f32)],
        compiler_params=_params(("parallel", "arbitrary")),
    )(u, pm, pmt, lam, bblk, cblk)


def _s5_bwd(u, dy, cprev, lam, bblk, cblk, bblk_t, cblk_t, n_x, reverse, name, add=None, skip=None):
    n_tot, d = u.shape
    nq, n_blk, n_xb = d // 128, n_tot // BLK, n_x // BLK
    order = _block_order(n_xb, n_blk, reverse)
    pm, pmt = _scan_perm(reverse)
    ns = N_STATE
    blk_of = lambda step: order(n_blk - 1 - step)
    has_add, has_skip = add is not None, skip is not None

    def body(*refs):
        u_ref, dy_ref, cp_ref, pm_ref, pmt_ref, lam_ref, b_ref, c_ref, bt_ref, ct_ref = refs[:10]
        rest = refs[10:]
        if has_add:
            add_ref, rest = rest[0], rest[1:]
        if has_skip:
            skip_ref, rest = rest[0], rest[1:]
        du_ref, dlam_ref, db_ref, dc_ref, bu_ref, st_ref, ds_ref, acarry_ref = rest
        step_id = pl.program_id(1)
        is_x = blk_of(step_id) < n_xb

        @pl.when(step_id == 0)
        def _():
            acarry_ref[...] = jnp.zeros_like(acarry_ref)
            dlam_ref[...] = jnp.zeros_like(dlam_ref)
            db_ref[...] = jnp.zeros_like(db_ref)
            dc_ref[...] = jnp.zeros_like(dc_ref)

        up = jnp.dot(pm_ref[...], u_ref[...].astype(bf16), preferred_element_type=f32).astype(bf16)
        bu_ref[0] = jnp.dot(up, b_ref[0], preferred_element_type=f32)
        bu_ref[1] = jnp.dot(up, b_ref[1], preferred_element_type=f32)
        lr, li = lam_ref[0:1, :], lam_ref[1:2, :]
        lrb, lib = jnp.broadcast_to(lr, (SCAN_SEG, ns)), jnp.broadcast_to(li, (SCAN_SEG, ns))
        cpr, cpi = cp_ref[0], cp_ref[1]

        def fstep(j, s):
            sr, si = s
            off = pl.multiple_of(j * SCAN_SEG, SCAN_SEG)
            nsr = lrb * sr - lib * si + bu_ref[0, pl.ds(off, SCAN_SEG), :]
            nsi = lrb * si + lib * sr + bu_ref[1, pl.ds(off, SCAN_SEG), :]
            st_ref[0, pl.ds(off, SCAN_SEG), :] = nsr
            st_ref[1, pl.ds(off, SCAN_SEG), :] = nsi
            return nsr, nsi

        lax.fori_loop(0, SEG_LEN, fstep, (cpr, cpi))

        dyv = jnp.where(is_x, dy_ref[...], 0.0)
        dyp = jnp.dot(pm_ref[...], dyv.astype(bf16), preferred_element_type=f32).astype(bf16)
        ds_ref[0] = jnp.dot(dyp, ct_ref[0], preferred_element_type=f32)
        ds_ref[1] = -jnp.dot(dyp, ct_ref[1], preferred_element_type=f32)

        def adj(j, a):
            ar, ai = a
            off = pl.multiple_of(j * SCAN_SEG, SCAN_SEG)
            nar = ds_ref[0, pl.ds(off, SCAN_SEG), :] + lrb * ar + lib * ai
            nai = ds_ref[1, pl.ds(off, SCAN_SEG), :] - lib * ar + lrb * ai
            return nar, nai

        zero = jnp.zeros((SCAN_SEG, ns), f32)
        er, ei = lax.fori_loop(0, SEG_LEN, lambda jj, a: adj(SEG_LEN - 1 - jj, a), (zero, zero))
        pr, pi = _lam_pow(lr, li)
        nr_, ni_ = acarry_ref[0, 0:1, :], acarry_ref[1, 0:1, :]
        rows_r, rows_i = [None] * SCAN_SEG, [None] * SCAN_SEG
        for k in range(SCAN_SEG - 1, -1, -1):
            rows_r[k], rows_i[k] = nr_, ni_
            nr_, ni_ = er[k:k + 1] + pr * nr_ + pi * ni_, ei[k:k + 1] + pr * ni_ - pi * nr_
        acarry_ref[0] = jnp.broadcast_to(nr_, (SCAN_SEG, ns))
        acarry_ref[1] = jnp.broadcast_to(ni_, (SCAN_SEG, ns))
        an_r, an_i = jnp.concatenate(rows_r, axis=0), jnp.concatenate(rows_i, axis=0)

        def adj2(jj, carry):
            ar, ai, glr, gli = carry
            j = SEG_LEN - 1 - jj
            nar, nai = adj(j, (ar, ai))
            off = pl.multiple_of(j * SCAN_SEG, SCAN_SEG)
            ds_ref[0, pl.ds(off, SCAN_SEG), :] = nar
            ds_ref[1, pl.ds(off, SCAN_SEG), :] = nai
            poff = pl.multiple_of((j - 1) * SCAN_SEG, SCAN_SEG)
            spr, spi = st_ref[0, pl.ds(poff, SCAN_SEG), :], st_ref[1, pl.ds(poff, SCAN_SEG), :]
            return nar, nai, glr + nar * spr + nai * spi, gli - nar * spi + nai * spr

        ar, ai, glr, gli = lax.fori_loop(0, SEG_LEN - 1, adj2, (an_r, an_i, zero, zero))
        nar, nai = adj(0, (ar, ai))
        ds_ref[0, 0:SCAN_SEG, :] = nar
        ds_ref[1, 0:SCAN_SEG, :] = nai
        glr = glr + nar * cpr + nai * cpi
        gli = gli - nar * cpi + nai * cpr
        dlam_ref[0:1, :] += jnp.sum(glr, axis=0, keepdims=True)
        dlam_ref[1:2, :] += jnp.sum(gli, axis=0, keepdims=True)

        a_r, a_i = ds_ref[0].astype(bf16), ds_ref[1].astype(bf16)
        dup = jnp.dot(a_r, bt_ref[0], preferred_element_type=f32) + jnp.dot(a_i, bt_ref[1],
                                                                         preferred_element_type=f32)
        du = _unpermute(pmt_ref[...], dup)
        if has_add:
            du = du + add_ref[...]
        if has_skip:
            du = du + skip_ref[...] * dyv
        du_ref[...] = du
        db_ref[0] += lax.dot_general(up, a_r, TN, preferred_element_type=f32)
        db_ref[1] += lax.dot_general(up, a_i, TN, preferred_element_type=f32)
        dc_ref[0] += lax.dot_general(st_ref[0].astype(bf16), dyp, TN, preferred_element_type=f32)
        dc_ref[1] -= lax.dot_general(st_ref[1].astype(bf16), dyp, TN, preferred_element_type=f32)

    full = lambda shape: pl.BlockSpec(shape, lambda q, s: (0,) * len(shape))
    tok = pl.BlockSpec((BLK, 128), lambda q, s: (blk_of(s), q))
    in_specs = [tok, pl.BlockSpec((BLK, 128), lambda q, s: (jnp.minimum(blk_of(s), n_xb - 1), q)),
                pl.BlockSpec((None, None, 2, SCAN_SEG, ns), lambda q, s: (q, n_blk - 1 - s, 0, 0, 0)),
                full((BLK, BLK)), full((BLK, BLK)),
                pl.BlockSpec((None, 2, ns), lambda q, s: (q, 0, 0)),
                pl.BlockSpec((None, 2, 128, ns), lambda q, s: (q, 0, 0, 0)),
                pl.BlockSpec((None, 2, ns, 128), lambda q, s: (q, 0, 0, 0)),
                pl.BlockSpec((None, 2, ns, 128), lambda q, s: (q, 0, 0, 0)),
                pl.BlockSpec((None, 2, 128, ns), lambda q, s: (q, 0, 0, 0))]
    args = [u, dy, cprev, pm, pmt, lam, bblk, cblk, bblk_t, cblk_t]
    if has_add:
        in_specs.append(tok)
        args.append(add)
    if has_skip:
        in_specs.append(pl.BlockSpec((1, 128), lambda q, s: (0, q)))
        args.append(skip)
    return pl.pallas_call(
        body, name=name, grid=(nq, n_blk), in_specs=in_specs,
        out_specs=[tok, pl.BlockSpec((None, 2, ns), lambda q, s: (q, 0, 0)),
                   pl.BlockSpec((None, 2, 128, ns), lambda q, s: (q, 0, 0, 0)),
                   pl.BlockSpec((None, 2, ns, 128), lambda q, s: (q, 0, 0, 0))],
        out_shape=[jax.ShapeDtypeStruct((n_tot, d), f32), jax.ShapeDtypeStruct((nq, 2, ns), f32),
                   jax.ShapeDtypeStruct((nq, 2, 128, ns), f32), jax.ShapeDtypeStruct((nq, 2, ns, 128), f32)],
        scratch_shapes=[pltpu.VMEM((2, BLK, ns), f32), pltpu.VMEM((2, BLK, ns), f32),
                        pltpu.VMEM((2, BLK, ns), f32), pltpu.VMEM((2, SCAN_SEG, ns), f32)],
        compiler_params=_params(("parallel", "arbitrary")),
    )(*args)


MESH = pl.DeviceIdType.MESH
GROUP_SIZE = {"c": 2, "xy": 4, "xyc": 8}


def _group(axes):
    x, y, c = lax.axis_index("x"), lax.axis_index("y"), lax.axis_index("c")
    if axes == "c":
        return c, lambda o: (x, y, c ^ 1)
    if axes == "xy":
        return 2 * x + y, lambda o: (x ^ (o >> 1), y ^ (o & 1), c)
    return 4 * x + 2 * y + c, lambda o: (x ^ (o >> 2), y ^ ((o >> 1) & 1), c ^ (o & 1))


def _exchange(src, axes, gather, name):
    n = GROUP_SIZE[axes]
    out_shape = (n,) + src.shape if gather else src.shape

    def body(src_ref, out_ref, send_sems, recv_sems, local_sem):
        me, peer = _group(axes)
        local = pltpu.make_async_copy(src_ref if gather else src_ref.at[me], out_ref.at[me], local_sem)
        local.start()

        def copy(o, src_idx, dst_idx):
            return pltpu.make_async_remote_copy(
                src_ref=src_ref if gather else src_ref.at[src_idx], dst_ref=out_ref.at[dst_idx],
                send_sem=send_sems.at[o - 1], recv_sem=recv_sems.at[o - 1],
                device_id=peer(o), device_id_type=MESH)

        sends = [copy(o, me ^ o, me) for o in range(1, n)]
        for cp in sends:
            cp.start()
        for o in range(1, n):
            copy(o, me, me ^ o).wait_recv()
        for cp in sends:
            cp.wait_send()
        local.wait()

    return pl.pallas_call(
        body, name=name, out_shape=jax.ShapeDtypeStruct(out_shape, src.dtype),
        in_specs=[pl.BlockSpec(memory_space=pl.ANY)], out_specs=pl.BlockSpec(memory_space=pl.ANY),
        scratch_shapes=[pltpu.SemaphoreType.DMA((n - 1,)), pltpu.SemaphoreType.DMA((n - 1,)),
                        pltpu.SemaphoreType.DMA(())],
        compiler_params=pltpu.CompilerParams(has_side_effects=True),
    )(src)


def _sum0(x, name):
    n, r, c = x.shape
    tr = _pick(r, (512, 256, 128, 64, 32, 16, 8))

    def body(x_ref, o_ref):
        acc = x_ref[0]
        for k in range(1, n):
            acc = acc + x_ref[k]
        o_ref[...] = acc

    return pl.pallas_call(
        body, name=name, grid=(r // tr,), in_specs=[pl.BlockSpec((n, tr, c), lambda i: (0, i, 0))],
        out_specs=pl.BlockSpec((tr, c), lambda i: (i, 0)), out_shape=jax.ShapeDtypeStruct((r, c), f32),
        compiler_params=_params(("parallel",)),
    )(x)


LANES = 1024


def _pack(parts, dtype):
    flat = jnp.concatenate([p.astype(dtype).reshape(-1) for p in parts])
    pad = (-flat.shape[0]) % (16 * LANES)
    if pad:
        flat = jnp.concatenate([flat, jnp.zeros((pad,), dtype)])
    return flat.reshape(-1, LANES)


def _unpack(flat, shapes):
    out, off = [], 0
    for s in shapes:
        size = int(np.prod(s))
        out.append(flat[off:off + size].reshape(s))
        off += size
    return out


def _vec(a):
    return a.reshape(-1, 1, a.shape[-1])


def _local_step(xa, tgt, mods, wts, sm, n_x):
    n_tot, d = xa.shape
    kw = dict(nrows=n_tot, n_x_rows=n_x)
    kx = dict(nrows=n_x, n_x_rows=n_x)
    mv = lambda l, k: mods[l, :, k][:, None, :]
    mx = lambda l, k: mods[l, 0:1, k][:, None, :]
    nmix, nffn = sm["norm_mix"], sm["norm_ffn"]
    cos, sin = _rope_tables(n_x, n_tot - n_x)
    cfg_a, cfg_b = _AttnCfg("na", n_x, n_tot), _AttnCfg("swa", n_x, n_tot)
    ff = wts["w1"][0].shape[1]
    tcf = _pick(ff, (512, 256, 128))

    h1 = _rowwise(_f_normmod, [xa], [_vec(nmix[0:1]), mv(0, 1), mv(0, 0)], [(d, bf16)], name="l0_norm_mix", **kw)[0]
    qkv32 = _mm(h1, wts["w_in"], "nn", f32, "l0_in_proj")
    qkv = _qkv_post(qkv32, cos, sin, "l0_qkv_post")
    rpb2 = jnp.zeros((128, 128), f32).at[:NA_HEADS * 15, :31].set(sm["rpb"].reshape(NA_HEADS * 15, 31))
    basis = jnp.zeros((128, GRID_W * GRID_W), f32).at[:31].set(_toeplitz_basis())
    tz = _small_dot(rpb2, basis, "rpb_expand")[:NA_HEADS * 15].reshape(NA_HEADS, 15, GRID_W, GRID_W)
    bias = tz[:, _rpb_index(False)].transpose(0, 1, 3, 2, 4).reshape(NA_HEADS, BLK, 3 * BLK)
    bias_t = tz[:, _rpb_index(True)].transpose(0, 1, 3, 2, 4).reshape(NA_HEADS, 3 * BLK, BLK)
    sink_rows = jnp.repeat(sm["sink"].reshape(NB_KV_HEADS, NB_GROUP, 1), BLK, axis=1).reshape(
        NB_KV_HEADS, NB_GROUP * BLK, 1)
    oa, lse_a = _attn_fwd(qkv, cfg_a, bias, "l0_na_fwd")
    ob, lse_b = _attn_fwd(qkv, cfg_b, sink_rows, "l0_swa_fwd")
    o = jnp.concatenate([oa, ob], axis=1)
    y1 = _mm(o, wts["w_out"], "nn", f32, "l0_out_proj")
    xb = _rowwise(_f_gated_add, [xa, y1], [mv(0, 2)], [(d, f32)], name="l0_res_mix", **kw)[0]

    def ffn_fwd(xin, l, vec_of, kk, tag):
        h = _rowwise(_f_normmod, [xin], [_vec(nffn[l:l + 1]), vec_of(l, 4), vec_of(l, 3)], [(d, bf16)],
                     name=tag + "_norm_ffn", **kk)[0]
        a = _mm(h, wts["w1"][l], "nn", bf16, tag + "_ffn_w1")
        b = _mm(h, wts["w3"][l], "nn", bf16, tag + "_ffn_w3")
        u = _rowwise(_f_swiglu, [a, b], [], [(ff, bf16)], name=tag + "_swiglu", tc=tcf, **kk)[0]
        fo = _mm(u, wts["w2"][l], "nn", f32, tag + "_ffn_w2")
        xo = _rowwise(_f_gated_add, [xin, fo], [vec_of(l, 5)], [(d, f32)], name=tag + "_res_ffn", **kk)[0]
        return xo, (h, a, b, u, fo)

    def ffn_bwd(d_out, xin, saved, l, vec_of, kk, tag):
        h, a, b, u, fo = saved
        rows = kk["nrows"]
        dfo, dg2 = _rowwise_bwd(_f_gated_add, [xin, fo], [vec_of(l, 5)], [d_out], {1: bf16},
                                name=tag + "_res_ffn_bwd", **kk)
        du = _mm(dfo, wts["w2"][l], "nt", bf16, tag + "_ffn_w2_dx")
        dw2 = _mm(u[:rows], dfo, "tn", f32, tag + "_ffn_w2_dw")
        da, db = _rowwise_bwd(_f_swiglu, [a, b], [], [du], {0: bf16, 1: bf16}, name=tag + "_swiglu_bwd",
                              tc=tcf, **kk)
        dh = _mm(da, wts["w1"][l], "nt", f32, tag + "_ffn_w13_dx", a2=db, b2=wts["w3"][l])
        dw1 = _mm(h[:rows], da, "tn", f32, tag + "_ffn_w1_dw")
        dw3 = _mm(h[:rows], db, "tn", f32, tag + "_ffn_w3_dw")
        dxin, dnw, dsc, dsh = _rowwise_bwd(
            _f_normmod, [xin], [_vec(nffn[l:l + 1]), vec_of(l, 4), vec_of(l, 3)], [dh], {0: f32},
            name=tag + "_norm_ffn_bwd", residual=d_out, **kk)
        return dxin, dw1, dw3, dw2, dnw, dsc, dsh, dg2

    xc, ffn0 = ffn_fwd(xb, 0, mv, kw, "l0")

    hs = _rowwise(_f_normmod, [xc], [_vec(nmix[1:2]), mv(1, 1), mv(1, 0)], [(d, f32)], name="l1_norm_mix", **kw)[0]
    g2n = sm["a_re"].shape[1]
    nq = d // 128
    ar2, ai2 = sm["a_re"].reshape(2 * g2n, SSM_STATE), sm["a_im"].reshape(2 * g2n, SSM_STATE)
    ldt2 = sm["log_dt"].reshape(2 * g2n, 1)
    bt_re = sm["b_re"].transpose(3, 0, 1, 2).reshape(SSM_GROUP, 2 * g2n, SSM_STATE)
    bt_im = sm["b_im"].transpose(3, 0, 1, 2).reshape(SSM_GROUP, 2 * g2n, SSM_STATE)
    disc_in = [ar2, ai2, ldt2, bt_re, bt_im]
    lam_r, lam_i, bbar_r, bbar_i = _whole(
        _f_discretise, disc_in,
        [((2 * g2n, SSM_STATE), f32)] * 2 + [((SSM_GROUP, 2 * g2n, SSM_STATE), f32)] * 2, "s5_discretise")
    eye = jnp.eye(GROUPS_PER_CHUNK, dtype=f32)

    def blockdiag_b(bbar):
        t = bbar.reshape(SSM_GROUP, 2, nq, GROUPS_PER_CHUNK, SSM_STATE)
        return jnp.einsum("hdqgp,gk->dqghkp", t, eye).reshape(2, nq, 128, N_STATE)

    def blockdiag_c(cw):
        t = cw.reshape(2, nq, GROUPS_PER_CHUNK, SSM_GROUP, SSM_STATE)
        return jnp.einsum("dqghp,gk->dqgpkh", t, eye).reshape(2, nq, N_STATE, 128)

    lam = jnp.stack([lam_r.reshape(2, nq, N_STATE), lam_i.reshape(2, nq, N_STATE)], axis=2)
    bblk = jnp.stack([blockdiag_b(bbar_r), blockdiag_b(bbar_i)], axis=2)
    cblk = jnp.stack([blockdiag_c(sm["c_re"]), blockdiag_c(sm["c_im"])], axis=2)
    bblk16, cblk16 = bblk.astype(bf16), cblk.astype(bf16)
    bblk_t, cblk_t = bblk16.transpose(0, 1, 2, 4, 3), cblk16.transpose(0, 1, 2, 4, 3)
    ys, cps = [], []
    for dr in range(2):
        yd, cp = _s5_fwd(hs, lam[dr], bblk16[dr], cblk16[dr], n_x, dr == 1, "l1_s5_fwd%d" % dr)
        ys.append(yd)
        cps.append(cp)
    dvec = sm["ssm_d"].reshape(1, 1, d)
    gy = _rowwise(_f_glu_pre, [hs, ys[0], ys[1]], [dvec], [(d, bf16)], name="l1_glu_pre", **kx)[0]
    zv = _mm(gy, wts["w_glu_v"], "nn", f32, "l1_glu_val")
    zg = _mm(gy, wts["w_glu_g"], "nn", f32, "l1_glu_gate")
    bv, bg = sm["b_glu"][:d].reshape(1, 1, d), sm["b_glu"][d:].reshape(1, 1, d)
    xd = _rowwise(_f_glu_post, [xc, zv, zg], [mx(1, 2), bv, bg], [(d, f32)], name="l1_glu_post", **kx)[0]
    xe, ffn1 = ffn_fwd(xd, 1, mx, kx, "l1")

    d_xe, d_nfinal, loss_blk = _loss_head(xe, tgt, sm["norm_final"].reshape(1, d), nrows=n_x, name="loss_head")
    d_xd, dw1_1, dw3_1, dw2_1, dnffn1, dsc2_1, dsh2_1, dg2_1 = ffn_bwd(d_xe, xd, ffn1, 1, mx, kx, "l1")
    dzv, dzg, dg1_1, dbv, dbg = _rowwise_bwd(_f_glu_post, [xc, zv, zg], [mx(1, 2), bv, bg], [d_xd],
                                             {1: bf16, 2: bf16}, name="l1_glu_post_bwd", **kx)
    dgy = _mm(dzv, wts["w_glu_v"], "nt", f32, "l1_glu_dx", a2=dzg, b2=wts["w_glu_g"])
    dwglu_v = _mm(gy, dzv, "tn", f32, "l1_glu_val_dw")
    dwglu_g = _mm(gy, dzg, "tn", f32, "l1_glu_gate_dw")
    dy, d_dvec = _rowwise_bwd(_f_glu_pre, [hs, ys[0], ys[1]], [dvec], [dgy], {1: f32}, name="l1_glu_pre_bwd", **kx)
    du0, dlam0, db0, dc0 = _s5_bwd(hs, dy, cps[0], lam[0], bblk16[0], cblk16[0], bblk_t[0], cblk_t[0], n_x, False,
                                   "l1_s5_bwd0", skip=sm["ssm_d"].reshape(1, d))
    du1, dlam1, db1, dc1 = _s5_bwd(hs, dy, cps[1], lam[1], bblk16[1], cblk16[1], bblk_t[1], cblk_t[1], n_x, True,
                                   "l1_s5_bwd1", add=du0)
    d_xc, dnmix1, dsc1_1, dsh1_1 = _rowwise_bwd(
        _f_normmod, [xc], [_vec(nmix[1:2]), mv(1, 1), mv(1, 0)], [du1], {0: f32},
        name="l1_norm_mix_bwd", residual=d_xd, residual_rows=n_x, **kw)
    dlam = jnp.stack([dlam0, dlam1])
    dlam_r, dlam_i = dlam[:, :, 0].reshape(2 * g2n, SSM_STATE), dlam[:, :, 1].reshape(2 * g2n, SSM_STATE)
    dbb = jnp.stack([db0, db1]).reshape(2, nq, 2, GROUPS_PER_CHUNK, SSM_GROUP, GROUPS_PER_CHUNK, SSM_STATE)
    dbbar = jnp.einsum("dqcghkp,gk->chdqgp", dbb, eye).reshape(2, SSM_GROUP, 2 * g2n, SSM_STATE)
    dcc = jnp.stack([dc0, dc1]).reshape(2, nq, 2, GROUPS_PER_CHUNK, SSM_STATE, GROUPS_PER_CHUNK, SSM_GROUP)
    dcw = jnp.einsum("dqcgpkh,gk->cdqghp", dcc, eye).reshape(2, 2, g2n, SSM_GROUP, SSM_STATE)
    d_ar, d_ai, d_ldt, d_btr, d_bti = _whole_bwd(_f_discretise, disc_in, [dlam_r, dlam_i, dbbar[0], dbbar[1]],
                                                 "s5_discretise_bwd")
    to_b = lambda t: t.reshape(SSM_GROUP, 2, g2n, SSM_STATE).transpose(1, 2, 3, 0)

    d_xb, dw1_0, dw3_0, dw2_0, dnffn0, dsc2_0, dsh2_0, dg2_0 = ffn_bwd(d_xc, xb, ffn0, 0, mv, kw, "l0")
    dy1, dg1_0 = _rowwise_bwd(_f_gated_add, [xa, y1], [mv(0, 2)], [d_xb], {1: bf16}, name="l0_res_mix_bwd", **kw)
    d_o = _mm(dy1, wts["w_out"], "nt", bf16, "l0_out_proj_dx")
    dw_out = _mm(o, dy1, "tn", f32, "l0_out_proj_dw")
    dqa, delta_a, dbias = _attn_bwd_q(qkv, oa, d_o, lse_a, cfg_a, bias, "l0_na_bwd_q")
    dka, dva = _attn_bwd_kv(qkv, d_o, lse_a, delta_a, cfg_a, bias_t, "l0_na_bwd_kv")
    dqb, delta_b, dsink_rows = _attn_bwd_q(qkv, ob, d_o, lse_b, cfg_b, sink_rows, "l0_swa_bwd_q")
    dkb, dvb = _attn_bwd_kv(qkv, d_o, lse_b, delta_b, cfg_b, None, "l0_swa_bwd_kv")
    d_qkv = _qkv_post_bwd([dqa, dka, dva, dqb, dkb, dvb], cos, sin, "l0_qkv_post_bwd")
    dh1 = _mm(d_qkv, wts["w_in"], "nt", f32, "l0_in_proj_dx")
    dw_in = _mm(h1, d_qkv, "tn", f32, "l0_in_proj_dw")
    d_xa, dnmix0, dsc1_0, dsh1_0 = _rowwise_bwd(
        _f_normmod, [xa], [_vec(nmix[0:1]), mv(0, 1), mv(0, 0)], [dh1], {0: f32},
        name="l0_norm_mix_bwd", residual=d_xb, **kw)
    dtz = jnp.zeros((NA_HEADS, 15, GRID_W, GRID_W), f32).at[:, _rpb_index(False)].add(
        dbias.reshape(NA_HEADS, 4, GRID_W, 12, GRID_W).transpose(0, 1, 3, 2, 4))
    dtz2 = jnp.zeros((128, GRID_W * GRID_W), f32).at[:NA_HEADS * 15].set(dtz.reshape(NA_HEADS * 15, -1))
    d_rpb = _small_dot(dtz2, basis.T, "rpb_expand_bwd")[:NA_HEADS * 15, :31].reshape(sm["rpb"].shape)
    d_sink = dsink_rows.reshape(NB_KV_HEADS * NB_GROUP, BLK).sum(axis=1)

    zero_c = jnp.zeros((1, 1, d), f32)
    both = lambda gx: jnp.concatenate([gx, zero_c], axis=0)
    dmods = jnp.stack([
        jnp.stack([dsh1_0, dsc1_0, dg1_0, dsh2_0, dsc2_0, dg2_0], axis=2),
        jnp.stack([dsh1_1, dsc1_1, both(dg1_1), both(dsh2_1), both(dsc2_1), both(dg2_1)], axis=2),
    ])[:, :, 0]
    big = dict(w1=jnp.stack([dw1_0, dw1_1]), w3=jnp.stack([dw3_0, dw3_1]), w2=jnp.stack([dw2_0, dw2_1]),
               w_in=dw_in, w_out=dw_out, w_glu=jnp.concatenate([dwglu_v, dwglu_g], axis=1))
    small = dict(
        norm_mix=jnp.concatenate([dnmix0[0], dnmix1[0]]), norm_ffn=jnp.concatenate([dnffn0[0], dnffn1[0]]),
        rpb=d_rpb, sink=d_sink, a_re=d_ar.reshape(sm["a_re"].shape), a_im=d_ai.reshape(sm["a_im"].shape),
        log_dt=d_ldt.reshape(sm["log_dt"].shape), b_re=to_b(d_btr), b_im=to_b(d_bti), c_re=dcw[0], c_im=dcw[1],
        ssm_d=d_dvec.reshape(d), b_glu=jnp.concatenate([dbv.reshape(d), dbg.reshape(d)]),
        norm_final=d_nfinal.reshape(d))
    return loss_blk[0, 0], d_xa, big, small, dmods


WEIGHTS = ("c_ctx", "ada_w", "ada_b", "norm_mix", "norm_ffn", "ffn_w1", "ffn_w3", "ffn_w2", "attn_w_in",
           "attn_w_out", "attn_rpb", "attn_sink", "ssm_a_re", "ssm_a_im", "ssm_log_dt", "ssm_b_re", "ssm_b_im",
           "ssm_c_re", "ssm_c_im", "ssm_d", "ssm_w_glu", "ssm_b_glu", "norm_final")
SHARDED_BIG = ("ffn_w1", "ffn_w3", "ffn_w2", "attn_w_in", "attn_w_out", "ssm_w_glu")
BIG = ("ada_w",) + SHARDED_BIG
SMALL = tuple(n for n in WEIGHTS if n not in BIG)


def _blocks_to_full(blk, kind):
    if kind == "cols2":
        t = blk.transpose(2, 3, 1, 0, 4)
        return t.reshape(t.shape[0], t.shape[1], -1)
    if kind == "rows2":
        t = blk.transpose(2, 1, 0, 3, 4)
        return t.reshape(t.shape[0], -1, t.shape[4])
    raise ValueError(kind)


def _full_to_blocks(full, kind):
    if kind == "cols2":
        l, k, n = full.shape
        return full.reshape(l, k, 4, 2, n // N_DEV).transpose(3, 2, 0, 1, 4)
    l, n, k = full.shape
    return full.reshape(l, 4, 2, n // N_DEV, k).transpose(2, 1, 0, 3, 4)


BIG_KIND = dict(ffn_w1="cols2", ffn_w3="cols2", ffn_w2="rows2", attn_w_in="cols2", attn_w_out="rows2",
                ssm_w_glu="cols2")


def kernel(x, c, ctx, c_ctx, ada_w, ada_b, norm_mix, norm_ffn, ffn_w1, ffn_w3, ffn_w2, attn_w_in, attn_w_out, attn_rpb, attn_sink, ssm_a_re, ssm_a_im, ssm_log_dt, ssm_b_re, ssm_b_im, ssm_c_re, ssm_c_im, ssm_d, ssm_w_glu, ssm_b_glu, norm_final, loss_target, m_c_ctx, m_ada_w, m_ada_b, m_norm_mix, m_norm_ffn, m_ffn_w1, m_ffn_w3, m_ffn_w2, m_attn_w_in, m_attn_w_out, m_attn_rpb, m_attn_sink, m_ssm_a_re, m_ssm_a_im, m_ssm_log_dt, m_ssm_b_re, m_ssm_b_im, m_ssm_c_re, m_ssm_c_im, m_ssm_d, m_ssm_w_glu, m_ssm_b_glu, m_norm_final, v_c_ctx, v_ada_w, v_ada_b, v_norm_mix, v_norm_ffn, v_ffn_w1, v_ffn_w3, v_ffn_w2, v_attn_w_in, v_attn_w_out, v_attn_rpb, v_attn_sink, v_ssm_a_re, v_ssm_a_im, v_ssm_log_dt, v_ssm_b_re, v_ssm_b_im, v_ssm_c_re, v_ssm_c_im, v_ssm_d, v_ssm_w_glu, v_ssm_b_glu, v_norm_final):
    p = dict(locals())
    w = {n: p[n] for n in WEIGHTS}
    me = 4 * lax.axis_index("x") + 2 * lax.axis_index("y") + lax.axis_index("c")
    n_x, d = x.shape[1], x.shape[2]
    cols = ada_w.shape[2]
    d8 = d // N_DEV

    first = jnp.concatenate([c[0], ssm_d[0], ssm_b_glu[0]])[None]
    g0 = _exchange(first, "xyc", True, "gather_vectors")[:, 0]
    c_all, d_full, bglu_full = g0[:, :d], g0[:, d:d + d8].reshape(d), g0[:, d + d8:].reshape(2 * d)
    cc = jnp.concatenate([c_all, c_ctx[None], jnp.zeros((16 - N_DEV - 1, d), f32)])
    sc_all = _whole(_f_silu, [cc], [((16, d), f32)], "silu_c")[0]
    my_cols = lambda a: lax.dynamic_slice_in_dim(a, me * cols, cols, axis=a.ndim - 1)
    mod_loc = jnp.stack([_mm(sc_all, ada_w[l], "nn", f32, "ada_mod%d" % l) for l in range(2)])
    mod_loc = mod_loc + my_cols(ada_b)[:, None, :]
    mg = _exchange(mod_loc.reshape(32, cols), "xyc", True, "gather_mod")
    mod_all = mg.reshape(N_DEV, 2, 16, cols).transpose(1, 2, 0, 3).reshape(2, 16, N_DEV * cols)
    mod_x = lax.dynamic_index_in_dim(mod_all, me, axis=1, keepdims=False)
    mods = jnp.stack([mod_x, mod_all[:, N_DEV]], axis=1).reshape(2, 2, 6, d)

    local_shapes = [w[n].shape for n in SHARDED_BIG]
    wpack = _pack([w[n] for n in SHARDED_BIG], bf16)
    wg = _exchange(_exchange(wpack, "xy", True, "gather_w_xy"), "c", True, "gather_w_c")
    wflat = wg.reshape(2, 4, -1)
    full, off = {}, 0
    for n, s in zip(SHARDED_BIG, local_shapes):
        size = int(np.prod(s))
        full[n] = _blocks_to_full(wflat[:, :, off:off + size].reshape((2, 4) + s), BIG_KIND[n])
        off += size
    wts = dict(w1=[full["ffn_w1"][0], full["ffn_w1"][1]], w3=[full["ffn_w3"][0], full["ffn_w3"][1]],
               w2=[full["ffn_w2"][0], full["ffn_w2"][1]], w_in=full["attn_w_in"][0], w_out=full["attn_w_out"][0],
               w_glu_v=full["ssm_w_glu"][0][:, :d], w_glu_g=full["ssm_w_glu"][0][:, d:])
    sm = dict(norm_mix=norm_mix, norm_ffn=norm_ffn, rpb=attn_rpb[0], sink=attn_sink[0], a_re=ssm_a_re[0],
              a_im=ssm_a_im[0], log_dt=ssm_log_dt[0], b_re=ssm_b_re[0], b_im=ssm_b_im[0], c_re=ssm_c_re[0],
              c_im=ssm_c_im[0], ssm_d=d_full, b_glu=bglu_full, norm_final=norm_final)

    xa = jnp.concatenate([x[0], ctx[0]], axis=0)
    loss_part, d_xa, big, small, dmods = _local_step(xa, loss_target[0], mods, wts, sm, n_x)

    gfull = dict(ffn_w1=big["w1"], ffn_w3=big["w3"], ffn_w2=big["w2"], attn_w_in=big["w_in"][None],
                 attn_w_out=big["w_out"][None], ssm_w_glu=big["w_glu"][None])
    gflat = jnp.concatenate([_full_to_blocks(gfull[n], BIG_KIND[n]).reshape(2, 4, -1) for n in SHARDED_BIG], axis=2)
    pad = (-gflat.shape[2]) % (16 * LANES)
    gpack = jnp.pad(gflat, ((0, 0), (0, 0), (0, pad))).reshape(2, 4, -1, LANES)
    rows = gpack.shape[2]
    r1 = _exchange(gpack, "c", False, "scatter_g_c")
    s1 = _sum0(r1.reshape(2, 4 * rows, LANES), "sum_g_c").reshape(4, rows, LANES)
    r2 = _exchange(s1, "xy", False, "scatter_g_xy")
    gmine = _unpack(_sum0(r2, "sum_g_xy").reshape(-1), local_shapes)
    grads = dict(zip(SHARDED_BIG, gmine))

    small_names = ("norm_mix", "norm_ffn", "rpb", "sink", "a_re", "a_im", "log_dt", "b_re", "b_im", "c_re", "c_im",
                   "ssm_d", "b_glu", "norm_final")
    parts = [loss_part.reshape(1)] + [small[n] for n in small_names] + [dmods[:, 0].reshape(2, 6 * d),
                                                                        dmods[:, 1].reshape(2, 6 * d)]
    shapes = [q.shape for q in parts]
    sg = _exchange(_pack(parts, f32), "xyc", True, "gather_small")
    tot = _unpack(_sum0(sg, "sum_small").reshape(-1), shapes)
    loss = tot[0][0]
    ts = dict(zip(small_names, tot[1:1 + len(small_names)]))
    tot_dmod_x, tot_dmod_c = tot[-2], tot[-1]
    off_x = sum(int(np.prod(s)) for s in shapes[:-2])
    dmod_x_all = sg.reshape(N_DEV, -1)[:, off_x:off_x + 12 * d].reshape(N_DEV, 2, 6 * d)
    dm = jnp.concatenate([dmod_x_all.transpose(1, 0, 2), tot_dmod_c[:, None, :],
                          jnp.zeros((2, 16 - N_DEV - 1, 6 * d), f32)], axis=1)
    dm_loc = my_cols(dm)
    grads["ada_w"] = jnp.stack([_mm(sc_all, dm_loc[l], "tn", f32, "ada_w_grad%d" % l) for l in range(2)])
    grads["ada_b"] = tot_dmod_x + tot_dmod_c
    dsc_part = _mm(dm_loc[0], ada_w[0], "nt", f32, "silu_c_grad", a2=dm_loc[1], b2=ada_w[1])[N_DEV:N_DEV + 1]
    dsc = _sum0(_exchange(dsc_part, "xyc", True, "gather_cctx"), "sum_cctx")
    grads["c_ctx"] = _whole_bwd(_f_silu, [c_ctx[None]], [dsc], "silu_cctx_bwd")[0][0]
    grads.update(norm_mix=ts["norm_mix"], norm_ffn=ts["norm_ffn"], attn_rpb=ts["rpb"][None],
                 attn_sink=ts["sink"][None], ssm_a_re=ts["a_re"][None], ssm_a_im=ts["a_im"][None],
                 ssm_log_dt=ts["log_dt"][None], ssm_b_re=ts["b_re"][None], ssm_b_im=ts["b_im"][None],
                 ssm_c_re=ts["c_re"][None], ssm_c_im=ts["c_im"][None],
                 ssm_d=lax.dynamic_slice_in_dim(ts["ssm_d"], me * d8, d8)[None],
                 ssm_b_glu=lax.dynamic_slice_in_dim(ts["b_glu"], me * 2 * d8, 2 * d8)[None],
                 norm_final=ts["norm_final"])

    delta, new_m, new_v = {}, {}, {}
    for n in BIG:
        two_d = lambda a: a.reshape(-1, a.shape[-1])
        dl, m2, v2 = _adamw(two_d(w[n]), two_d(grads[n]), two_d(p["m_" + n]), two_d(p["v_" + n]), "adamw_" + n)
        delta[n], new_m[n], new_v[n] = (t.reshape(w[n].shape) for t in (dl, m2, v2))
    sshapes = [w[n].shape for n in SMALL]
    packs = [_pack([src[n] for n in SMALL], f32) for src in
             (w, grads, {n: p["m_" + n] for n in SMALL}, {n: p["v_" + n] for n in SMALL})]
    for store, t in zip((delta, new_m, new_v), _adamw(*packs, "adamw_small")):
        store.update(zip(SMALL, _unpack(t.reshape(-1), sshapes)))

    return (loss, d_xa[:n_x][None], *[grads[n] for n in WEIGHTS], *[delta[n] for n in WEIGHTS],
            *[new_m[n] for n in WEIGHTS], *[new_v[n] for n in WEIGHTS])
```

```python
import functools
import math

import numpy as np
import jax
import jax.numpy as jnp
from jax import lax
from jax.experimental import pallas as pl
from jax.experimental.pallas import tpu as pltpu

f32, bf16 = jnp.float32, jnp.bfloat16

HEAD_DIM = 128
GRID_W = 64
NA_HEADS = 8
NB_Q_HEADS = 8
NB_KV_HEADS = 2
NB_GROUP = NB_Q_HEADS // NB_KV_HEADS
NA_ROWS = 8
NA_COLS = 16
SW_RADIUS = 128
ROPE_BASE = 10000.0
SSM_GROUP = 16
SSM_STATE = 64
EPS = 1e-6
NEG_INF = -1e30
ADAM_LR, ADAM_B1, ADAM_B2, ADAM_EPS, ADAM_WD, ADAM_STEP = 0.001, 0.9, 0.999, 1e-08, 0.01, 10

N_DEV = 8
BLK = 256
SCAN_SEG = 8
GROUPS_PER_CHUNK = 8
V7X_VMEM_LIMIT = 56 * 2 ** 20
MM_VMEM_BUDGET = 36 * 2 ** 20

NT = (((1,), (1,)), ((), ()))
TN = (((0,), (0,)), ((), ()))


def _params(sem):
    return pltpu.CompilerParams(dimension_semantics=sem, vmem_limit_bytes=V7X_VMEM_LIMIT)


def _pick(n, cands):
    for c in cands:
        if n % c == 0:
            return c
    return n


def _mm(a, b, mode, out_dtype, name, a2=None, b2=None):
    if mode == "tn":
        kdim, m, n = a.shape[0], a.shape[1], b.shape[1]
    elif mode == "nt":
        m, kdim, n = a.shape[0], a.shape[1], b.shape[0]
    else:
        m, kdim, n = a.shape[0], a.shape[1], b.shape[1]
    if mode == "tn":
        tm = _pick(m, (2048, 1024, 512, 256, 128))
        tk = _pick(kdim, (1024, 768, 512, 256, 128))
    else:
        tm = _pick(m, (1024, 768, 512, 256, 128))
        tk = _pick(kdim, (2048, 1024, 768, 512, 256, 128))
    in_bytes = max(a.dtype.itemsize, b.dtype.itemsize) * (1 if a2 is None else 2)
    out_bytes = jnp.dtype(out_dtype).itemsize

    def vmem(tn_):
        return 2 * in_bytes * tk * (tm + tn_) + (4 + 2 * out_bytes) * tm * tn_

    tn = next((t for t in (2048, 1024, 512, 256, 128) if n % t == 0 and vmem(t) <= MM_VMEM_BUDGET), n)
    nk = kdim // tk
    if mode == "tn":
        a_spec = pl.BlockSpec((tk, tm), lambda i, j, k: (k, i))
        b_spec = pl.BlockSpec((tk, tn), lambda i, j, k: (k, j))
        dims = TN
    elif mode == "nt":
        a_spec = pl.BlockSpec((tm, tk), lambda i, j, k: (i, k))
        b_spec = pl.BlockSpec((tn, tk), lambda i, j, k: (j, k))
        dims = NT
    else:
        a_spec = pl.BlockSpec((tm, tk), lambda i, j, k: (i, k))
        b_spec = pl.BlockSpec((tk, tn), lambda i, j, k: (k, j))
        dims = (((1,), (0,)), ((), ()))

    pairs = 1 if a2 is None else 2

    def body(*refs):
        o_ref, acc_ref = refs[2 * pairs:]
        k = pl.program_id(2)

        @pl.when(k == 0)
        def _():
            acc_ref[...] = jnp.zeros_like(acc_ref)

        for p in range(pairs):
            acc_ref[...] += lax.dot_general(refs[2 * p][...].astype(bf16), refs[2 * p + 1][...].astype(bf16), dims,
                                            preferred_element_type=f32)

        @pl.when(k == nk - 1)
        def _():
            o_ref[...] = acc_ref[...].astype(o_ref.dtype)

    args = (a, b) if a2 is None else (a, b, a2, b2)
    return pl.pallas_call(
        body, name=name, grid=(m // tm, n // tn, nk),
        in_specs=[a_spec, b_spec] * pairs, out_specs=pl.BlockSpec((tm, tn), lambda i, j, k: (i, j)),
        out_shape=jax.ShapeDtypeStruct((m, n), out_dtype),
        scratch_shapes=[pltpu.VMEM((tm, tn), f32)],
        compiler_params=_params(("parallel", "parallel", "arbitrary")),
    )(*args)


def _small_dot(a, b, name):
    def body(a_ref, b_ref, o_ref):
        o_ref[...] = jnp.dot(a_ref[...], b_ref[...], precision=lax.Precision.HIGHEST, preferred_element_type=f32)

    return pl.pallas_call(body, name=name, out_shape=jax.ShapeDtypeStruct((a.shape[0], b.shape[1]), f32))(a, b)


def _group_of(i, n_x_tiles, n_groups):
    return jnp.where(i >= n_x_tiles, n_groups - 1, 0)


def _rowwise(f, rows, vecs, outs, *, nrows, n_x_rows, name, tm=BLK, tc=None):
    n_x_tiles = n_x_rows // tm
    grid = (nrows // tm,) if tc is None else (nrows // tm, rows[0].shape[1] // tc)

    def rspec(cols):
        if tc is None:
            return pl.BlockSpec((tm, cols), lambda i: (i, 0))
        return pl.BlockSpec((tm, tc), lambda i, j: (i, j))

    def vspec(v):
        g = v.shape[0]
        if tc is None:
            return pl.BlockSpec((None, 1, v.shape[2]), lambda i: (_group_of(i, n_x_tiles, g), 0, 0))
        return pl.BlockSpec((None, 1, tc), lambda i, j: (_group_of(i, n_x_tiles, g), 0, j))

    nr, nv = len(rows), len(vecs)

    def body(*refs):
        ins = [r[...] for r in refs[:nr + nv]]
        res = f(*ins)
        for o_ref, val in zip(refs[nr + nv:], res):
            o_ref[...] = val.astype(o_ref.dtype)

    return pl.pallas_call(
        body, name=name, grid=grid,
        in_specs=[rspec(r.shape[1]) for r in rows] + [vspec(v) for v in vecs],
        out_specs=[rspec(c) for c, _ in outs],
        out_shape=[jax.ShapeDtypeStruct((nrows, c), d) for c, d in outs],
        compiler_params=_params(("parallel",) * len(grid)),
    )(*rows, *vecs)


def _rowwise_bwd(f, rows, vecs, cts, row_grads, *, nrows, n_x_rows, name, tm=BLK, tc=None, residual=None,
                 residual_rows=None):
    n_x_tiles = n_x_rows // tm
    n_tiles = nrows // tm
    res_tiles = None if residual_rows is None else residual_rows // tm
    grid = (n_tiles,) if tc is None else (rows[0].shape[1] // tc, n_tiles)
    row_of = (lambda *g: g[0]) if tc is None else (lambda *g: g[1])

    def rspec(cols):
        if tc is None:
            return pl.BlockSpec((tm, cols), lambda i: (i, 0))
        return pl.BlockSpec((tm, tc), lambda j, i: (i, j))

    def vspec(v):
        g = v.shape[0]
        if tc is None:
            return pl.BlockSpec((None, 1, v.shape[2]), lambda i: (_group_of(i, n_x_tiles, g), 0, 0))
        return pl.BlockSpec((None, 1, tc), lambda j, i: (_group_of(i, n_x_tiles, g), 0, j))

    nr, nv, nc = len(rows), len(vecs), len(cts)
    gidx = sorted(row_grads)

    def body(*refs):
        i = row_of(*[pl.program_id(d) for d in range(len(grid))])
        row_vals = [r[...].astype(f32) for r in refs[:nr]]
        vec_vals = [jnp.broadcast_to(r[...].astype(f32), (tm, r.shape[-1])) for r in refs[nr:nr + nv]]
        ct_vals = [r[...].astype(f32) for r in refs[nr + nv:nr + nv + nc]]
        out_refs = refs[nr + nv + nc + (residual is not None):]
        res, vjp = jax.vjp(lambda *a: tuple(o.astype(f32) for o in f(*a)), *row_vals, *vec_vals)
        grads = list(vjp(tuple(ct_vals)))
        if residual is not None:
            extra = refs[nr + nv + nc][...]
            if res_tiles is not None:
                extra = jnp.where(i < res_tiles, extra, 0.0)
            grads[0] = grads[0] + extra
        for o_ref, k in zip(out_refs[:len(gidx)], gidx):
            o_ref[...] = grads[k].astype(o_ref.dtype)
        for o_ref, g, v in zip(out_refs[len(gidx):], grads[nr:], vecs):
            part = jnp.sum(g, axis=0, keepdims=True)
            first = (i == 0) if v.shape[0] == 1 else ((i == 0) | (i == n_x_tiles))

            @pl.when(first)
            def _():
                o_ref[...] = part

            @pl.when(jnp.logical_not(first))
            def _():
                o_ref[...] += part

    sem = ("arbitrary",) if tc is None else ("parallel", "arbitrary")
    res_specs, res_args = [], []
    if residual is not None:
        assert tc is None
        clamp = (lambda i: i) if res_tiles is None else (lambda i: jnp.minimum(i, res_tiles - 1))
        res_specs = [pl.BlockSpec((tm, residual.shape[1]), lambda i: (clamp(i), 0))]
        res_args = [residual]
    return pl.pallas_call(
        body, name=name, grid=grid,
        in_specs=[rspec(r.shape[1]) for r in rows] + [vspec(v) for v in vecs] + [rspec(c.shape[1]) for c in cts]
        + res_specs,
        out_specs=[rspec(rows[k].shape[1]) for k in gidx] + [vspec(v) for v in vecs],
        out_shape=[jax.ShapeDtypeStruct((nrows, rows[k].shape[1]), row_grads[k]) for k in gidx]
        + [jax.ShapeDtypeStruct(v.shape, f32) for v in vecs],
        compiler_params=_params(sem),
    )(*rows, *vecs, *cts, *res_args)


def _rms(x, w):
    return x * lax.rsqrt(jnp.mean(x * x, axis=-1, keepdims=True) + EPS) * w


def _f_normmod(x, w, sc, sh):
    return (_rms(x.astype(f32), w) * (1.0 + sc) + sh,)


def _f_gated_add(x, y, g):
    return (x + g * y.astype(f32),)


def _f_swiglu(a, b):
    a = a.astype(f32)
    return (jax.nn.silu(a) * b.astype(f32),)


def _f_glu_pre(u, yf, yr, d):
    return (jax.nn.gelu(d * u + yf + yr),)


def _f_glu_post(x, zv, zg, g, bv, bg):
    return (x + g * ((zv.astype(f32) + bv) * jax.nn.sigmoid(zg.astype(f32) + bg)),)


def _f_scale(y, g):
    return (g * y.astype(f32),)


def _f_silu(x):
    return (jax.nn.silu(x.astype(f32)),)


def _loss_head(x, tgt, w, *, nrows, name):
    d = x.shape[1]
    tm = BLK

    def body(x_ref, t_ref, w_ref, dx_ref, dw_ref, loss_ref):
        i = pl.program_id(0)
        wb = jnp.broadcast_to(w_ref[...], (tm, d))
        y, vjp = jax.vjp(_rms, x_ref[...], wb)
        e = y - t_ref[...]
        dx, dwb = vjp(e * (1.0 / d))
        dx_ref[...] = dx
        dw = jnp.sum(dwb, axis=0, keepdims=True)
        part = jnp.full((8, 128), 0.5 / d, f32) * jnp.sum(e * e)

        @pl.when(i == 0)
        def _():
            dw_ref[...] = dw
            loss_ref[...] = part

        @pl.when(i > 0)
        def _():
            dw_ref[...] += dw
            loss_ref[...] += part

    row = pl.BlockSpec((tm, d), lambda i: (i, 0))
    return pl.pallas_call(
        body, name=name, grid=(nrows // tm,),
        in_specs=[row, row, pl.BlockSpec((1, d), lambda i: (0, 0))],
        out_specs=[row, pl.BlockSpec((1, d), lambda i: (0, 0)), pl.BlockSpec((8, 128), lambda i: (0, 0))],
        out_shape=[jax.ShapeDtypeStruct((nrows, d), f32), jax.ShapeDtypeStruct((1, d), f32),
                   jax.ShapeDtypeStruct((8, 128), f32)],
        compiler_params=_params(("arbitrary",)),
    )(x, tgt, w)


def _adamw(w, g, m, v, name):
    r, c = w.shape
    tr = _pick(r, (512, 256, 128, 64, 32, 16, 8))
    tcol = _pick(c, (1024, 512)) if c % 128 == 0 else c

    def body(w_ref, g_ref, m_ref, v_ref, d_ref, m2_ref, v2_ref):
        gg = g_ref[...]
        m2 = ADAM_B1 * m_ref[...] + (1.0 - ADAM_B1) * gg
        v2 = ADAM_B2 * v_ref[...] + (1.0 - ADAM_B2) * (gg * gg)
        m_hat = m2 / (1.0 - ADAM_B1 ** ADAM_STEP)
        v_hat = v2 / (1.0 - ADAM_B2 ** ADAM_STEP)
        d_ref[...] = -ADAM_LR * (m_hat / (jnp.sqrt(v_hat) + ADAM_EPS) + ADAM_WD * w_ref[...])
        m2_ref[...] = m2
        v2_ref[...] = v2

    spec = pl.BlockSpec((tr, tcol), lambda i, j: (i, j))
    return pl.pallas_call(
        body, name=name, grid=(r // tr, c // tcol), in_specs=[spec] * 4, out_specs=[spec] * 3,
        out_shape=[jax.ShapeDtypeStruct((r, c), f32)] * 3,
        compiler_params=_params(("parallel", "parallel")),
    )(w, g, m, v)


def _swap_quarters(x):
    lane = lax.broadcasted_iota(jnp.int32, x.shape, 1)
    return jnp.where((lane & 63) < 32, pltpu.roll(x, 96, 1), pltpu.roll(x, 32, 1))


def _rope_tables(n_x, n_ctx):
    t = np.arange(n_x)
    quarter = HEAD_DIM // 4
    inv = ROPE_BASE ** (-np.arange(quarter, dtype=np.float64) / quarter)
    ar = (t // GRID_W)[:, None] * inv[None]
    ac = (t % GRID_W)[:, None] * inv[None]
    cos = np.concatenate([np.cos(ar), np.cos(ar), np.cos(ac), np.cos(ac)], axis=1)
    sin = np.concatenate([-np.sin(ar), np.sin(ar), -np.sin(ac), np.sin(ac)], axis=1)
    cos = np.concatenate([cos, np.ones((n_ctx, HEAD_DIM))], axis=0)
    sin = np.concatenate([sin, np.zeros((n_ctx, HEAD_DIM))], axis=0)
    return jnp.asarray(cos, f32), jnp.asarray(sin, f32)


A_W = NA_HEADS * HEAD_DIM
QB0, KB0, VB0 = 3 * A_W, 3 * A_W + NB_Q_HEADS * HEAD_DIM, 3 * A_W + (NB_Q_HEADS + NB_KV_HEADS) * HEAD_DIM
IN_W = VB0 + NB_KV_HEADS * HEAD_DIM


def _qkv_post(qkv, cos, sin, name):
    n = qkv.shape[0]

    def body(x_ref, c_ref, s_ref, o_ref):
        c, s = c_ref[...], s_ref[...]
        o_ref[:, :QB0] = x_ref[:, :QB0].astype(bf16)
        for col in range(QB0, VB0, HEAD_DIM):
            x = x_ref[:, col:col + HEAD_DIM]
            o_ref[:, col:col + HEAD_DIM] = (x * c + _swap_quarters(x) * s).astype(bf16)
        o_ref[:, VB0:] = x_ref[:, VB0:].astype(bf16)

    row = lambda c: pl.BlockSpec((BLK, c), lambda i: (i, 0))
    return pl.pallas_call(
        body, name=name, grid=(n // BLK,), in_specs=[row(IN_W), row(HEAD_DIM), row(HEAD_DIM)],
        out_specs=row(IN_W), out_shape=jax.ShapeDtypeStruct((n, IN_W), bf16),
        compiler_params=_params(("parallel",)),
    )(qkv, cos, sin)


def _qkv_post_bwd(parts, cos, sin, name):
    n = parts[0].shape[0]

    def body(qa, ka, va, qb, kb, vb, c_ref, s_ref, o_ref):
        c, s = c_ref[...], s_ref[...]
        o_ref[:, 0:A_W] = qa[...]
        o_ref[:, A_W:2 * A_W] = ka[...]
        o_ref[:, 2 * A_W:QB0] = va[...]
        for src, col0, width in ((qb, QB0, KB0 - QB0), (kb, KB0, VB0 - KB0)):
            for off in range(0, width, HEAD_DIM):
                g = src[:, off:off + HEAD_DIM].astype(f32)
                o_ref[:, col0 + off:col0 + off + HEAD_DIM] = (g * c + _swap_quarters(g * s)).astype(bf16)
        o_ref[:, VB0:] = vb[...]

    row = lambda c: pl.BlockSpec((BLK, c), lambda i: (i, 0))
    return pl.pallas_call(
        body, name=name, grid=(n // BLK,),
        in_specs=[row(p.shape[1]) for p in parts] + [row(HEAD_DIM), row(HEAD_DIM)],
        out_specs=row(IN_W), out_shape=jax.ShapeDtypeStruct((n, IN_W), bf16),
        compiler_params=_params(("parallel",)),
    )(*parts, cos, sin)


def _valid(kind, qpos, kpos, n_x):
    ok = (kpos >= 0) & (kpos < n_x) & (qpos >= 0) & (qpos < n_x)
    if kind == "na":
        rows = n_x // GRID_W
        qr, qc = lax.shift_right_arithmetic(qpos, 6), qpos & (GRID_W - 1)
        kr, kc = lax.shift_right_arithmetic(kpos, 6), kpos & (GRID_W - 1)
        kr0 = jnp.clip(qr - NA_ROWS // 2, 0, rows - NA_ROWS)
        ws = jnp.clip(qc - NA_COLS // 2, 0, GRID_W - NA_COLS)
        return ok & (kr >= kr0) & (kr < kr0 + NA_ROWS) & (kc >= ws) & (kc < ws + NA_COLS)
    return ok & (jnp.abs(kpos - qpos) <= SW_RADIUS)


def _stack_heads(x, g):
    if g == 1:
        return x
    return jnp.concatenate([x[:, a * HEAD_DIM:(a + 1) * HEAD_DIM] for a in range(g)], axis=0)


def _unstack_heads(x, g):
    if g == 1:
        return x
    r = x.shape[0] // g
    return jnp.concatenate([x[a * r:(a + 1) * r] for a in range(g)], axis=1)


def _tile_rows(x, g):
    return x if g == 1 else jnp.concatenate([x] * g, axis=0)


class _AttnCfg:
    def __init__(self, kind, n_x, n_tot):
        self.kind, self.n_x, self.n_tot = kind, n_x, n_tot
        self.n_xb, self.n_blk = n_x // BLK, n_tot // BLK
        if kind == "na":
            self.g, self.nkv, self.q0, self.k0, self.v0 = 1, NA_HEADS, 0, NA_HEADS, 2 * NA_HEADS
        else:
            self.g, self.nkv = NB_GROUP, NB_KV_HEADS
            self.q0, self.k0, self.v0 = QB0 // (NB_GROUP * HEAD_DIM), KB0 // HEAD_DIM, VB0 // HEAD_DIM
        self.r = BLK * self.g
        self.qw = HEAD_DIM * self.g
        self.scale = HEAD_DIM ** -0.5


def _attn_fwd(qkv, cfg, extra, name):
    g, r, qw, n_xb, n_x = cfg.g, cfg.r, cfg.qw, cfg.n_xb, cfg.n_x
    last = n_xb - 1

    def body(q_ref, kp, ko, kn, vp, vo, vn, kc_ref, vc_ref, ex_ref, o_ref, lse_ref):
        i = pl.program_id(1)
        q = _stack_heads(q_ref[...], g)
        kw = jnp.concatenate([kp[...], ko[...], kn[...]], axis=0)
        vw = jnp.concatenate([vp[...], vo[...], vn[...]], axis=0)
        s = lax.dot_general(q, kw, NT, preferred_element_type=f32) * cfg.scale
        if cfg.kind == "na":
            s = s + ex_ref[...]
        qpos = i * BLK + lax.broadcasted_iota(jnp.int32, (BLK, 3 * BLK), 0)
        kpos = (i - 1) * BLK + lax.broadcasted_iota(jnp.int32, (BLK, 3 * BLK), 1)
        s = jnp.where(_tile_rows(_valid(cfg.kind, qpos, kpos, n_x), g), s, NEG_INF)
        sc = lax.dot_general(q, kc_ref[...], NT, preferred_element_type=f32) * cfg.scale
        m = jnp.maximum(jnp.max(s, axis=-1, keepdims=True), jnp.max(sc, axis=-1, keepdims=True))
        if cfg.kind == "swa":
            m = jnp.maximum(m, ex_ref[...])
        p, pc = jnp.exp(s - m), jnp.exp(sc - m)
        l = jnp.sum(p, axis=-1, keepdims=True) + jnp.sum(pc, axis=-1, keepdims=True)
        if cfg.kind == "swa":
            l = l + jnp.exp(ex_ref[...] - m)
        o = jnp.dot(p.astype(bf16), vw, preferred_element_type=f32) + jnp.dot(pc.astype(bf16), vc_ref[...],
                                                                             preferred_element_type=f32)
        o_ref[...] = _unstack_heads(o / l, g).astype(bf16)
        lse_ref[...] = m + jnp.log(l)

    kv = lambda col0, f: pl.BlockSpec((BLK, HEAD_DIM), lambda h, i: (f(i), col0 + h))
    prev = lambda i: jnp.clip(i - 1, 0, last)
    own = lambda i: jnp.minimum(i, last)
    nxt = lambda i: jnp.minimum(i + 1, last)
    ctx = lambda i: n_xb
    if cfg.kind == "na":
        ex_spec = pl.BlockSpec((None, BLK, 3 * BLK), lambda h, i: (h, 0, 0))
    else:
        ex_spec = pl.BlockSpec((None, r, 1), lambda h, i: (h, 0, 0))
    return pl.pallas_call(
        body, name=name, grid=(cfg.nkv, cfg.n_blk),
        in_specs=[pl.BlockSpec((BLK, qw), lambda h, i: (i, cfg.q0 + h)),
                  kv(cfg.k0, prev), kv(cfg.k0, own), kv(cfg.k0, nxt),
                  kv(cfg.v0, prev), kv(cfg.v0, own), kv(cfg.v0, nxt),
                  kv(cfg.k0, ctx), kv(cfg.v0, ctx), ex_spec],
        out_specs=[pl.BlockSpec((BLK, qw), lambda h, i: (i, h)),
                   pl.BlockSpec((None, None, r, 1), lambda h, i: (h, i, 0, 0))],
        out_shape=[jax.ShapeDtypeStruct((cfg.n_tot, cfg.nkv * qw), bf16),
                   jax.ShapeDtypeStruct((cfg.nkv, cfg.n_blk, r, 1), f32)],
        compiler_params=_params(("parallel", "parallel")),
    )(qkv, qkv, qkv, qkv, qkv, qkv, qkv, qkv, qkv, extra)


def _attn_bwd_q(qkv, o, do, lse, cfg, extra, name):
    g, r, qw, n_xb, n_x = cfg.g, cfg.r, cfg.qw, cfg.n_xb, cfg.n_x
    last = n_xb - 1
    do_col0 = 0 if cfg.kind == "na" else (NA_HEADS * HEAD_DIM) // qw

    def body(q_ref, kp, ko, kn, vp, vo, vn, kc_ref, vc_ref, ex_ref, o_ref, do_ref, lse_ref,
             dq_ref, delta_ref, dex_ref):
        i = pl.program_id(1)
        q = _stack_heads(q_ref[...], g)
        dout = _stack_heads(do_ref[...], g)
        out = _stack_heads(o_ref[...], g)
        delta = jnp.sum(dout.astype(f32) * out.astype(f32), axis=-1, keepdims=True)
        delta_ref[...] = delta
        kw = jnp.concatenate([kp[...], ko[...], kn[...]], axis=0)
        vw = jnp.concatenate([vp[...], vo[...], vn[...]], axis=0)
        s = lax.dot_general(q, kw, NT, preferred_element_type=f32) * cfg.scale
        if cfg.kind == "na":
            s = s + ex_ref[...]
        qpos = i * BLK + lax.broadcasted_iota(jnp.int32, (BLK, 3 * BLK), 0)
        kpos = (i - 1) * BLK + lax.broadcasted_iota(jnp.int32, (BLK, 3 * BLK), 1)
        s = jnp.where(_tile_rows(_valid(cfg.kind, qpos, kpos, n_x), g), s, NEG_INF)
        sc = lax.dot_general(q, kc_ref[...], NT, preferred_element_type=f32) * cfg.scale
        lse_v = lse_ref[...]
        p, pc = jnp.exp(s - lse_v), jnp.exp(sc - lse_v)
        dp = lax.dot_general(dout, vw, NT, preferred_element_type=f32)
        dpc = lax.dot_general(dout, vc_ref[...], NT, preferred_element_type=f32)
        ds, dsc = p * (dp - delta), pc * (dpc - delta)
        dq = jnp.dot(ds.astype(bf16), kw, preferred_element_type=f32) + jnp.dot(dsc.astype(bf16), kc_ref[...],
                                                                             preferred_element_type=f32)
        dq_ref[...] = _unstack_heads(dq * cfg.scale, g).astype(bf16)
        dex = ds if cfg.kind == "na" else -jnp.exp(ex_ref[...] - lse_v) * delta

        @pl.when(i == 0)
        def _():
            dex_ref[...] = dex

        @pl.when(i > 0)
        def _():
            dex_ref[...] += dex

    kv = lambda col0, f: pl.BlockSpec((BLK, HEAD_DIM), lambda h, i: (f(i), col0 + h))
    prev = lambda i: jnp.clip(i - 1, 0, last)
    own = lambda i: jnp.minimum(i, last)
    nxt = lambda i: jnp.minimum(i + 1, last)
    ctx = lambda i: n_xb
    if cfg.kind == "na":
        ex_spec = pl.BlockSpec((None, BLK, 3 * BLK), lambda h, i: (h, 0, 0))
    else:
        ex_spec = pl.BlockSpec((None, r, 1), lambda h, i: (h, 0, 0))
    stat = pl.BlockSpec((None, None, r, 1), lambda h, i: (h, i, 0, 0))
    return pl.pallas_call(
        body, name=name, grid=(cfg.nkv, cfg.n_blk),
        in_specs=[pl.BlockSpec((BLK, qw), lambda h, i: (i, cfg.q0 + h)),
                  kv(cfg.k0, prev), kv(cfg.k0, own), kv(cfg.k0, nxt),
                  kv(cfg.v0, prev), kv(cfg.v0, own), kv(cfg.v0, nxt),
                  kv(cfg.k0, ctx), kv(cfg.v0, ctx), ex_spec,
                  pl.BlockSpec((BLK, qw), lambda h, i: (i, h)),
                  pl.BlockSpec((BLK, qw), lambda h, i: (i, do_col0 + h)), stat],
        out_specs=[pl.BlockSpec((BLK, qw), lambda h, i: (i, h)), stat, ex_spec],
        out_shape=[jax.ShapeDtypeStruct((cfg.n_tot, cfg.nkv * qw), bf16),
                   jax.ShapeDtypeStruct((cfg.nkv, cfg.n_blk, r, 1), f32),
                   jax.ShapeDtypeStruct(extra.shape, f32)],
        compiler_params=_params(("parallel", "arbitrary")),
    )(qkv, qkv, qkv, qkv, qkv, qkv, qkv, qkv, qkv, extra, o, do, lse)


def _attn_bwd_kv(qkv, do, lse, delta, cfg, bias_t, name):
    g, r, qw, n_xb, n_x, n_blk = cfg.g, cfg.r, cfg.qw, cfg.n_xb, cfg.n_x, cfg.n_blk
    last = n_xb - 1
    do_col0 = 0 if cfg.kind == "na" else (NA_HEADS * HEAD_DIM) // qw
    has_bias = cfg.kind == "na"

    def body(*refs):
        qs, dos, lses, dels = refs[0:3], refs[3:6], refs[6:9], refs[9:12]
        qj_ref, doj_ref, lsej_ref, delj_ref, k_ref, v_ref, kc_ref, vc_ref = refs[12:20]
        rest = refs[20:]
        if has_bias:
            b_ref, rest = rest[0], rest[1:]
        dk_ref, dv_ref, dkc_acc, dvc_acc = rest
        j = pl.program_id(1)

        @pl.when(j == 0)
        def _():
            dkc_acc[...] = jnp.zeros_like(dkc_acc)
            dvc_acc[...] = jnp.zeros_like(dvc_acc)

        qj, doj = _stack_heads(qj_ref[...], g), _stack_heads(doj_ref[...], g)
        sc = lax.dot_general(qj, kc_ref[...], NT, preferred_element_type=f32) * cfg.scale
        pc = jnp.exp(sc - lsej_ref[...])
        dvc_acc[...] += lax.dot_general(pc.astype(bf16), doj, TN, preferred_element_type=f32)
        dpc = lax.dot_general(doj, vc_ref[...], NT, preferred_element_type=f32)
        dsc = pc * (dpc - delj_ref[...])
        dkc_acc[...] += lax.dot_general(dsc.astype(bf16), qj, TN, preferred_element_type=f32) * cfg.scale

        @pl.when(j < n_xb)
        def _():
            qw_all = jnp.concatenate([_stack_heads(x[...], g) for x in qs], axis=0)
            do_all = jnp.concatenate([_stack_heads(x[...], g) for x in dos], axis=0)
            lse_all = jnp.concatenate([x[...] for x in lses], axis=0)
            del_all = jnp.concatenate([x[...] for x in dels], axis=0)
            s = lax.dot_general(qw_all, k_ref[...], NT, preferred_element_type=f32) * cfg.scale
            if has_bias:
                s = s + b_ref[...]
            kpos = j * BLK + lax.broadcasted_iota(jnp.int32, (BLK, BLK), 1)
            masks = []
            for w in range(3):
                qpos = (j - 1 + w) * BLK + lax.broadcasted_iota(jnp.int32, (BLK, BLK), 0)
                masks.append(_tile_rows(_valid(cfg.kind, qpos, kpos, n_x), g))
            s = jnp.where(jnp.concatenate(masks, axis=0), s, NEG_INF)
            p = jnp.exp(s - lse_all)
            dv_ref[...] = lax.dot_general(p.astype(bf16), do_all, TN, preferred_element_type=f32).astype(bf16)
            dp = lax.dot_general(do_all, v_ref[...], NT, preferred_element_type=f32)
            ds = p * (dp - del_all)
            dk_ref[...] = (lax.dot_general(ds.astype(bf16), qw_all, TN, preferred_element_type=f32)
                           * cfg.scale).astype(bf16)

        @pl.when(j == n_xb)
        def _():
            dk_ref[...] = dkc_acc[...].astype(bf16)
            dv_ref[...] = dvc_acc[...].astype(bf16)

    prev = lambda j: jnp.clip(j - 1, 0, last)
    own = lambda j: jnp.minimum(j, last)
    nxt = lambda j: jnp.minimum(j + 1, last)
    same = lambda j: j
    qspec = lambda f: pl.BlockSpec((BLK, qw), lambda h, j: (f(j), cfg.q0 + h))
    dospec = lambda f: pl.BlockSpec((BLK, qw), lambda h, j: (f(j), do_col0 + h))
    stat = lambda f: pl.BlockSpec((None, None, r, 1), lambda h, j: (h, f(j), 0, 0))
    kv = lambda col0, f: pl.BlockSpec((BLK, HEAD_DIM), lambda h, j: (f(j), col0 + h))
    ctx = lambda j: n_xb
    in_specs = ([qspec(f) for f in (prev, own, nxt)] + [dospec(f) for f in (prev, own, nxt)]
                + [stat(f) for f in (prev, own, nxt)] * 2
                + [qspec(same), dospec(same), stat(same), stat(same),
                   kv(cfg.k0, same), kv(cfg.v0, same), kv(cfg.k0, ctx), kv(cfg.v0, ctx)])
    args = [qkv] * 3 + [do] * 3 + [lse] * 3 + [delta] * 3 + [qkv, do, lse, delta, qkv, qkv, qkv, qkv]
    if has_bias:
        in_specs.append(pl.BlockSpec((None, 3 * BLK, BLK), lambda h, j: (h, 0, 0)))
        args.append(bias_t)
    out = pl.BlockSpec((BLK, HEAD_DIM), lambda h, j: (j, h))
    return pl.pallas_call(
        body, name=name, grid=(cfg.nkv, n_blk), in_specs=in_specs, out_specs=[out, out],
        out_shape=[jax.ShapeDtypeStruct((cfg.n_tot, cfg.nkv * HEAD_DIM), bf16)] * 2,
        scratch_shapes=[pltpu.VMEM((BLK, HEAD_DIM), f32)] * 2,
        compiler_params=_params(("parallel", "arbitrary")),
    )(*args)


def _rpb_index(transposed):
    if transposed:
        return np.array([[a - b + 11 for a in range(4)] for b in range(12)])
    return np.array([[b - a + 3 for b in range(12)] for a in range(4)])


def _toeplitz_basis():
    qc, kc = np.meshgrid(np.arange(GRID_W), np.arange(GRID_W), indexing="ij")
    e = (kc - qc + NA_COLS - 1)[None] == np.arange(2 * NA_COLS - 1)[:, None, None]
    return e.reshape(2 * NA_COLS - 1, GRID_W * GRID_W).astype(np.float32)


def _whole(f, ins, outs, name):
    def body(*refs):
        res = f(*[r[...] for r in refs[:len(ins)]])
        for o_ref, val in zip(refs[len(ins):], res):
            o_ref[...] = val.astype(o_ref.dtype)

    return pl.pallas_call(body, name=name,
                          out_shape=[jax.ShapeDtypeStruct(s, d) for s, d in outs])(*ins)


def _whole_bwd(f, ins, cts, name):
    n = len(ins)

    def body(*refs):
        _, vjp = jax.vjp(f, *[r[...] for r in refs[:n]])
        grads = vjp(tuple(r[...] for r in refs[n:n + len(cts)]))
        for o_ref, g in zip(refs[n + len(cts):], grads):
            o_ref[...] = g

    return pl.pallas_call(body, name=name,
                          out_shape=[jax.ShapeDtypeStruct(a.shape, f32) for a in ins])(*ins, *cts)


def _f_discretise(ar, ai, ldt, br, bi):
    dt = jnp.exp(ldt)
    mag = jnp.exp(ar * dt)
    lam_r, lam_i = mag * jnp.cos(ai * dt), mag * jnp.sin(ai * dt)
    den = ar * ar + ai * ai
    nr = lam_r - 1.0
    coef_r = (nr * ar + lam_i * ai) / den
    coef_i = (lam_i * ar - nr * ai) / den
    return (lam_r, lam_i, coef_r[None] * br - coef_i[None] * bi, coef_r[None] * bi + coef_i[None] * br)


SEG_LEN = BLK // SCAN_SEG
N_STATE = GROUPS_PER_CHUNK * SSM_STATE


def _scan_perm(reverse):
    r = np.arange(BLK)
    t = (r % SCAN_SEG) * SEG_LEN + r // SCAN_SEG
    if reverse:
        t = BLK - 1 - t
    pm = np.zeros((BLK, BLK), np.float32)
    pm[r, t] = 1.0
    return jnp.asarray(pm, bf16), jnp.asarray(pm.T, bf16)


def _block_order(n_xb, n_blk, reverse):
    n_cb = n_blk - n_xb
    if reverse:
        return lambda cc: jnp.where(cc < n_cb, n_blk - 1 - cc, n_xb - 1 - (cc - n_cb))
    return lambda cc: jnp.where(cc < n_cb, n_xb + cc, cc - n_cb)


def _unpermute(pmt, y):
    hi = y.astype(bf16)
    lo = (y - hi.astype(f32)).astype(bf16)
    return jnp.dot(pmt, hi, preferred_element_type=f32) + jnp.dot(pmt, lo, preferred_element_type=f32)


def _lam_pow(lr, li):
    for _ in range(int(math.log2(SEG_LEN))):
        lr, li = lr * lr - li * li, 2.0 * lr * li
    return lr, li


def _s5_fwd(u, lam, bblk, cblk, n_x, reverse, name):
    n_tot, d = u.shape
    nq, n_blk, n_xb = d // 128, n_tot // BLK, n_x // BLK
    order = _block_order(n_xb, n_blk, reverse)
    pm, pmt = _scan_perm(reverse)
    ns = N_STATE

    def body(u_ref, pm_ref, pmt_ref, lam_ref, b_ref, c_ref, y_ref, cp_ref, bu_ref, st_ref, carry_ref):
        cc = pl.program_id(1)

        @pl.when(cc == 0)
        def _():
            carry_ref[...] = jnp.zeros_like(carry_ref)

        up = jnp.dot(pm_ref[...], u_ref[...].astype(bf16), preferred_element_type=f32).astype(bf16)
        bu_ref[0] = jnp.dot(up, b_ref[0], preferred_element_type=f32)
        bu_ref[1] = jnp.dot(up, b_ref[1], preferred_element_type=f32)
        lr, li = lam_ref[0:1, :], lam_ref[1:2, :]
        lrb, lib = jnp.broadcast_to(lr, (SCAN_SEG, ns)), jnp.broadcast_to(li, (SCAN_SEG, ns))

        def step(j, s, store):
            sr, si = s
            off = pl.multiple_of(j * SCAN_SEG, SCAN_SEG)
            nsr = lrb * sr - lib * si + bu_ref[0, pl.ds(off, SCAN_SEG), :]
            nsi = lrb * si + lib * sr + bu_ref[1, pl.ds(off, SCAN_SEG), :]
            if store:
                st_ref[0, pl.ds(off, SCAN_SEG), :] = nsr
                st_ref[1, pl.ds(off, SCAN_SEG), :] = nsi
            return nsr, nsi

        zero = jnp.zeros((SCAN_SEG, ns), f32)
        er, ei = lax.fori_loop(0, SEG_LEN, lambda j, s: step(j, s, False), (zero, zero))
        pr, pi = _lam_pow(lr, li)
        cr, ci = carry_ref[0, 0:1, :], carry_ref[1, 0:1, :]
        rows_r, rows_i = [], []
        for k in range(SCAN_SEG):
            rows_r.append(cr)
            rows_i.append(ci)
            cr, ci = pr * cr - pi * ci + er[k:k + 1], pr * ci + pi * cr + ei[k:k + 1]
        carry_ref[0] = jnp.broadcast_to(cr, (SCAN_SEG, ns))
        carry_ref[1] = jnp.broadcast_to(ci, (SCAN_SEG, ns))
        cpr, cpi = jnp.concatenate(rows_r, axis=0), jnp.concatenate(rows_i, axis=0)
        cp_ref[0] = cpr
        cp_ref[1] = cpi
        lax.fori_loop(0, SEG_LEN, lambda j, s: step(j, s, True), (cpr, cpi))
        yp = (jnp.dot(st_ref[0].astype(bf16), c_ref[0], preferred_element_type=f32)
              - jnp.dot(st_ref[1].astype(bf16), c_ref[1], preferred_element_type=f32))
        y_ref[...] = _unpermute(pmt_ref[...], yp)

    full = lambda shape: pl.BlockSpec(shape, lambda q, cc: (0,) * len(shape))
    return pl.pallas_call(
        body, name=name, grid=(nq, n_blk),
        in_specs=[pl.BlockSpec((BLK, 128), lambda q, cc: (order(cc), q)), full((BLK, BLK)), full((BLK, BLK)),
                  pl.BlockSpec((None, 2, ns), lambda q, cc: (q, 0, 0)),
                  pl.BlockSpec((None, 2, 128, ns), lambda q, cc: (q, 0, 0, 0)),
                  pl.BlockSpec((None, 2, ns, 128), lambda q, cc: (q, 0, 0, 0))],
        out_specs=[pl.BlockSpec((BLK, 128), lambda q, cc: (order(cc), q)),
                   pl.BlockSpec((None, None, 2, SCAN_SEG, ns), lambda q, cc: (q, cc, 0, 0, 0))],
        out_shape=[jax.ShapeDtypeStruct((n_tot, d), f32),
                   jax.ShapeDtypeStruct((nq, n_blk, 2, SCAN_SEG, ns), f32)],
        scratch_shapes=[pltpu.VMEM((2, BLK, ns), f32), pltpu.VMEM((2, BLK, ns), f32),
                        pltpu.VMEM((2, SCAN_SEG, ns), f32)],
        compiler_params=_params(("parallel", "arbitrary")),
    )(u, pm, pmt, lam, bblk, cblk)


def _s5_bwd(u, dy, cprev, lam, bblk, cblk, bblk_t, cblk_t, n_x, reverse, name, add=None, skip=None):
    n_tot, d = u.shape
    nq, n_blk, n_xb = d // 128, n_tot // BLK, n_x // BLK
    order = _block_order(n_xb, n_blk, reverse)
    pm, pmt = _scan_perm(reverse)
    ns = N_STATE
    blk_of = lambda step: order(n_blk - 1 - step)
    has_add, has_skip = add is not None, skip is not None

    def body(*refs):
        u_ref, dy_ref, cp_ref, pm_ref, pmt_ref, lam_ref, b_ref, c_ref, bt_ref, ct_ref = refs[:10]
        rest = refs[10:]
        if has_add:
            add_ref, rest = rest[0], rest[1:]
        if has_skip:
            skip_ref, rest = rest[0], rest[1:]
        du_ref, dlam_ref, db_ref, dc_ref, bu_ref, st_ref, ds_ref, acarry_ref = rest
        step_id = pl.program_id(1)
        is_x = blk_of(step_id) < n_xb

        @pl.when(step_id == 0)
        def _():
            acarry_ref[...] = jnp.zeros_like(acarry_ref)
            dlam_ref[...] = jnp.zeros_like(dlam_ref)
            db_ref[...] = jnp.zeros_like(db_ref)
            dc_ref[...] = jnp.zeros_like(dc_ref)

        up = jnp.dot(pm_ref[...], u_ref[...].astype(bf16), preferred_element_type=f32).astype(bf16)
        bu_ref[0] = jnp.dot(up, b_ref[0], preferred_element_type=f32)
        bu_ref[1] = jnp.dot(up, b_ref[1], preferred_element_type=f32)
        lr, li = lam_ref[0:1, :], lam_ref[1:2, :]
        lrb, lib = jnp.broadcast_to(lr, (SCAN_SEG, ns)), jnp.broadcast_to(li, (SCAN_SEG, ns))
        cpr, cpi = cp_ref[0], cp_ref[1]

        def fstep(j, s):
            sr, si = s
            off = pl.multiple_of(j * SCAN_SEG, SCAN_SEG)
            nsr = lrb * sr - lib * si + bu_ref[0, pl.ds(off, SCAN_SEG), :]
            nsi = lrb * si + lib * sr + bu_ref[1, pl.ds(off, SCAN_SEG), :]
            st_ref[0, pl.ds(off, SCAN_SEG), :] = nsr
            st_ref[1, pl.ds(off, SCAN_SEG), :] = nsi
            return nsr, nsi

        lax.fori_loop(0, SEG_LEN, fstep, (cpr, cpi))

        dyv = jnp.where(is_x, dy_ref[...], 0.0)
        dyp = jnp.dot(pm_ref[...], dyv.astype(bf16), preferred_element_type=f32).astype(bf16)
        ds_ref[0] = jnp.dot(dyp, ct_ref[0], preferred_element_type=f32)
        ds_ref[1] = -jnp.dot(dyp, ct_ref[1], preferred_element_type=f32)

        def adj(j, a):
            ar, ai = a
            off = pl.multiple_of(j * SCAN_SEG, SCAN_SEG)
            nar = ds_ref[0, pl.ds(off, SCAN_SEG), :] + lrb * ar + lib * ai
            nai = ds_ref[1, pl.ds(off, SCAN_SEG), :] - lib * ar + lrb * ai
            return nar, nai

        zero = jnp.zeros((SCAN_SEG, ns), f32)
        er, ei = lax.fori_loop(0, SEG_LEN, lambda jj, a: adj(SEG_LEN - 1 - jj, a), (zero, zero))
        pr, pi = _lam_pow(lr, li)
        nr_, ni_ = acarry_ref[0, 0:1, :], acarry_ref[1, 0:1, :]
        rows_r, rows_i = [None] * SCAN_SEG, [None] * SCAN_SEG
        for k in range(SCAN_SEG - 1, -1, -1):
            rows_r[k], rows_i[k] = nr_, ni_
            nr_, ni_ = er[k:k + 1] + pr * nr_ + pi * ni_, ei[k:k + 1] + pr * ni_ - pi * nr_
        acarry_ref[0] = jnp.broadcast_to(nr_, (SCAN_SEG, ns))
        acarry_ref[1] = jnp.broadcast_to(ni_, (SCAN_SEG, ns))
        an_r, an_i = jnp.concatenate(rows_r, axis=0), jnp.concatenate(rows_i, axis=0)

        def adj2(jj, carry):
            ar, ai, glr, gli = carry
            j = SEG_LEN - 1 - jj
            nar, nai = adj(j, (ar, ai))
            off = pl.multiple_of(j * SCAN_SEG, SCAN_SEG)
            ds_ref[0, pl.ds(off, SCAN_SEG), :] = nar
            ds_ref[1, pl.ds(off, SCAN_SEG), :] = nai
            poff = pl.multiple_of((j - 1) * SCAN_SEG, SCAN_SEG)
            spr, spi = st_ref[0, pl.ds(poff, SCAN_SEG), :], st_ref[1, pl.ds(poff, SCAN_SEG), :]
            return nar, nai, glr + nar * spr + nai * spi, gli - nar * spi + nai * spr

        ar, ai, glr, gli = lax.fori_loop(0, SEG_LEN - 1, adj2, (an_r, an_i, zero, zero))
        nar, nai = adj(0, (ar, ai))
        ds_ref[0, 0:SCAN_SEG, :] = nar
        ds_ref[1, 0:SCAN_SEG, :] = nai
        glr = glr + nar * cpr + nai * cpi
        gli = gli - nar * cpi + nai * cpr
        dlam_ref[0:1, :] += jnp.sum(glr, axis=0, keepdims=True)
        dlam_ref[1:2, :] += jnp.sum(gli, axis=0, keepdims=True)

        a_r, a_i = ds_ref[0].astype(bf16), ds_ref[1].astype(bf16)
        dup = jnp.dot(a_r, bt_ref[0], preferred_element_type=f32) + jnp.dot(a_i, bt_ref[1],
                                                                         preferred_element_type=f32)
        du = _unpermute(pmt_ref[...], dup)
        if has_add:
            du = du + add_ref[...]
        if has_skip:
            du = du + skip_ref[...] * dyv
        du_ref[...] = du
        db_ref[0] += lax.dot_general(up, a_r, TN, preferred_element_type=f32)
        db_ref[1] += lax.dot_general(up, a_i, TN, preferred_element_type=f32)
        dc_ref[0] += lax.dot_general(st_ref[0].astype(bf16), dyp, TN, preferred_element_type=f32)
        dc_ref[1] -= lax.dot_general(st_ref[1].astype(bf16), dyp, TN, preferred_element_type=f32)

    full = lambda shape: pl.BlockSpec(shape, lambda q, s: (0,) * len(shape))
    tok = pl.BlockSpec((BLK, 128), lambda q, s: (blk_of(s), q))
    in_specs = [tok, pl.BlockSpec((BLK, 128), lambda q, s: (jnp.minimum(blk_of(s), n_xb - 1), q)),
                pl.BlockSpec((None, None, 2, SCAN_SEG, ns), lambda q, s: (q, n_blk - 1 - s, 0, 0, 0)),
                full((BLK, BLK)), full((BLK, BLK)),
                pl.BlockSpec((None, 2, ns), lambda q, s: (q, 0, 0)),
                pl.BlockSpec((None, 2, 128, ns), lambda q, s: (q, 0, 0, 0)),
                pl.BlockSpec((None, 2, ns, 128), lambda q, s: (q, 0, 0, 0)),
                pl.BlockSpec((None, 2, ns, 128), lambda q, s: (q, 0, 0, 0)),
                pl.BlockSpec((None, 2, 128, ns), lambda q, s: (q, 0, 0, 0))]
    args = [u, dy, cprev, pm, pmt, lam, bblk, cblk, bblk_t, cblk_t]
    if has_add:
        in_specs.append(tok)
        args.append(add)
    if has_skip:
        in_specs.append(pl.BlockSpec((1, 128), lambda q, s: (0, q)))
        args.append(skip)
    return pl.pallas_call(
        body, name=name, grid=(nq, n_blk), in_specs=in_specs,
        out_specs=[tok, pl.BlockSpec((None, 2, ns), lambda q, s: (q, 0, 0)),
                   pl.BlockSpec((None, 2, 128, ns), lambda q, s: (q, 0, 0, 0)),
                   pl.BlockSpec((None, 2, ns, 128), lambda q, s: (q, 0, 0, 0))],
        out_shape=[jax.ShapeDtypeStruct((n_tot, d), f32), jax.ShapeDtypeStruct((nq, 2, ns), f32),
                   jax.ShapeDtypeStruct((nq, 2, 128, ns), f32), jax.ShapeDtypeStruct((nq, 2, ns, 128), f32)],
        scratch_shapes=[pltpu.VMEM((2, BLK, ns), f32), pltpu.VMEM((2, BLK, ns), f32),
                        pltpu.VMEM((2, BLK, ns), f32), pltpu.VMEM((2, SCAN_SEG, ns), f32)],
        compiler_params=_params(("parallel", "arbitrary")),
    )(*args)


MESH = pl.DeviceIdType.MESH
GROUP_SIZE = {"c": 2, "xy": 4, "xyc": 8}
MIN_CHUNK_BYTES = 512 * 1024


def _group(axes):
    x, y, c = lax.axis_index("x"), lax.axis_index("y"), lax.axis_index("c")
    if axes == "c":
        return c, lambda o: (x, y, c ^ 1)
    if axes == "xy":
        return 2 * x + y, lambda o: (x ^ (o >> 1), y ^ (o & 1), c)
    return 4 * x + 2 * y + c, lambda o: (x ^ (o >> 2), y ^ ((o >> 1) & 1), c ^ (o & 1))


def _exchange(src, axes, gather, name):
    n = GROUP_SIZE[axes]
    out_shape = (n,) + src.shape if gather else src.shape
    slab = out_shape[1:]
    slab_bytes = int(np.prod(slab)) * src.dtype.itemsize
    nch = next((k for k in (16, 8, 4, 2) if slab[0] % (16 * k) == 0 and slab_bytes // k >= MIN_CHUNK_BYTES), 1)
    step = slab[0] // nch

    def body(src_ref, out_ref, send_sems, recv_sems, local_sem):
        me, peer = _group(axes)
        local = pltpu.make_async_copy(src_ref if gather else src_ref.at[me], out_ref.at[me], local_sem)
        local.start()

        def copy(o, k, src_idx, dst_idx):
            rows = pl.ds(k * step, step)
            return pltpu.make_async_remote_copy(
                src_ref=(src_ref if gather else src_ref.at[src_idx]).at[rows], dst_ref=out_ref.at[dst_idx].at[rows],
                send_sem=send_sems.at[o - 1, k], recv_sem=recv_sems.at[o - 1, k],
                device_id=peer(o), device_id_type=MESH)

        sends = [copy(o, k, me ^ o, me) for k in range(nch) for o in range(1, n)]
        for cp in sends:
            cp.start()
        for k in range(nch):
            for o in range(1, n):
                copy(o, k, me, me ^ o).wait_recv()
        for cp in sends:
            cp.wait_send()
        local.wait()

    return pl.pallas_call(
        body, name=name, out_shape=jax.ShapeDtypeStruct(out_shape, src.dtype),
        in_specs=[pl.BlockSpec(memory_space=pl.ANY)], out_specs=pl.BlockSpec(memory_space=pl.ANY),
        scratch_shapes=[pltpu.SemaphoreType.DMA((n - 1, nch)), pltpu.SemaphoreType.DMA((n - 1, nch)),
                        pltpu.SemaphoreType.DMA(())],
        compiler_params=pltpu.CompilerParams(has_side_effects=True),
    )(src)


def _sum0(x, name):
    n, r, c = x.shape
    tr = _pick(r, (512, 256, 128, 64, 32, 16, 8))

    def body(x_ref, o_ref):
        acc = x_ref[0]
        for k in range(1, n):
            acc = acc + x_ref[k]
        o_ref[...] = acc

    return pl.pallas_call(
        body, name=name, grid=(r // tr,), in_specs=[pl.BlockSpec((n, tr, c), lambda i: (0, i, 0))],
        out_specs=pl.BlockSpec((tr, c), lambda i: (i, 0)), out_shape=jax.ShapeDtypeStruct((r, c), f32),
        compiler_params=_params(("parallel",)),
    )(x)


LANES = 1024


def _pack(parts, dtype):
    flat = jnp.concatenate([p.astype(dtype).reshape(-1) for p in parts])
    pad = (-flat.shape[0]) % (16 * LANES)
    if pad:
        flat = jnp.concatenate([flat, jnp.zeros((pad,), dtype)])
    return flat.reshape(-1, LANES)


def _unpack(flat, shapes):
    out, off = [], 0
    for s in shapes:
        size = int(np.prod(s))
        out.append(flat[off:off + size].reshape(s))
        off += size
    return out


def _vec(a):
    return a.reshape(-1, 1, a.shape[-1])


def _local_step(xa, tgt, mods, wts, sm, n_x):
    n_tot, d = xa.shape
    kw = dict(nrows=n_tot, n_x_rows=n_x)
    kx = dict(nrows=n_x, n_x_rows=n_x)
    mv = lambda l, k: mods[l, :, k][:, None, :]
    mx = lambda l, k: mods[l, 0:1, k][:, None, :]
    nmix, nffn = sm["norm_mix"], sm["norm_ffn"]
    cos, sin = _rope_tables(n_x, n_tot - n_x)
    cfg_a, cfg_b = _AttnCfg("na", n_x, n_tot), _AttnCfg("swa", n_x, n_tot)
    ff = wts["w1"][0].shape[1]
    tcf = _pick(ff, (512, 256, 128))

    h1 = _rowwise(_f_normmod, [xa], [_vec(nmix[0:1]), mv(0, 1), mv(0, 0)], [(d, bf16)], name="l0_norm_mix", **kw)[0]
    qkv32 = _mm(h1, wts["w_in"], "nn", f32, "l0_in_proj")
    qkv = _qkv_post(qkv32, cos, sin, "l0_qkv_post")
    rpb2 = jnp.zeros((128, 128), f32).at[:NA_HEADS * 15, :31].set(sm["rpb"].reshape(NA_HEADS * 15, 31))
    basis = jnp.zeros((128, GRID_W * GRID_W), f32).at[:31].set(_toeplitz_basis())
    tz = _small_dot(rpb2, basis, "rpb_expand")[:NA_HEADS * 15].reshape(NA_HEADS, 15, GRID_W, GRID_W)
    bias = jnp.stack([tz[:, 3 - a:15 - a] for a in range(4)], axis=1).transpose(0, 1, 3, 2, 4).reshape(
        NA_HEADS, BLK, 3 * BLK)
    bias_t = jnp.stack([tz[:, a:a + 12][:, ::-1] for a in range(4)], axis=2).transpose(0, 1, 3, 2, 4).reshape(
        NA_HEADS, 3 * BLK, BLK)
    sink_rows = jnp.repeat(sm["sink"].reshape(NB_KV_HEADS, NB_GROUP, 1), BLK, axis=1).reshape(
        NB_KV_HEADS, NB_GROUP * BLK, 1)
    oa, lse_a = _attn_fwd(qkv, cfg_a, bias, "l0_na_fwd")
    ob, lse_b = _attn_fwd(qkv, cfg_b, sink_rows, "l0_swa_fwd")
    o = jnp.concatenate([oa, ob], axis=1)
    y1 = _mm(o, wts["w_out"], "nn", f32, "l0_out_proj")
    xb = _rowwise(_f_gated_add, [xa, y1], [mv(0, 2)], [(d, f32)], name="l0_res_mix", **kw)[0]

    def ffn_fwd(xin, l, vec_of, kk, tag):
        h = _rowwise(_f_normmod, [xin], [_vec(nffn[l:l + 1]), vec_of(l, 4), vec_of(l, 3)], [(d, bf16)],
                     name=tag + "_norm_ffn", **kk)[0]
        a = _mm(h, wts["w1"][l], "nn", bf16, tag + "_ffn_w1")
        b = _mm(h, wts["w3"][l], "nn", bf16, tag + "_ffn_w3")
        u = _rowwise(_f_swiglu, [a, b], [], [(ff, bf16)], name=tag + "_swiglu", tc=tcf, **kk)[0]
        fo = _mm(u, wts["w2"][l], "nn", f32, tag + "_ffn_w2")
        xo = _rowwise(_f_gated_add, [xin, fo], [vec_of(l, 5)], [(d, f32)], name=tag + "_res_ffn", **kk)[0]
        return xo, (h, a, b, u, fo)

    def ffn_bwd(d_out, xin, saved, l, vec_of, kk, tag):
        h, a, b, u, fo = saved
        rows = kk["nrows"]
        dfo, dg2 = _rowwise_bwd(_f_gated_add, [xin, fo], [vec_of(l, 5)], [d_out], {1: bf16},
                                name=tag + "_res_ffn_bwd", **kk)
        du = _mm(dfo, wts["w2"][l], "nt", bf16, tag + "_ffn_w2_dx")
        dw2 = _mm(u[:rows], dfo, "tn", f32, tag + "_ffn_w2_dw")
        da, db = _rowwise_bwd(_f_swiglu, [a, b], [], [du], {0: bf16, 1: bf16}, name=tag + "_swiglu_bwd",
                              tc=tcf, **kk)
        dh = _mm(da, wts["w1"][l], "nt", f32, tag + "_ffn_w13_dx", a2=db, b2=wts["w3"][l])
        dw1 = _mm(h[:rows], da, "tn", f32, tag + "_ffn_w1_dw")
        dw3 = _mm(h[:rows], db, "tn", f32, tag + "_ffn_w3_dw")
        dxin, dnw, dsc, dsh = _rowwise_bwd(
            _f_normmod, [xin], [_vec(nffn[l:l + 1]), vec_of(l, 4), vec_of(l, 3)], [dh], {0: f32},
            name=tag + "_norm_ffn_bwd", residual=d_out, **kk)
        return dxin, dw1, dw3, dw2, dnw, dsc, dsh, dg2

    xc, ffn0 = ffn_fwd(xb, 0, mv, kw, "l0")

    hs = _rowwise(_f_normmod, [xc], [_vec(nmix[1:2]), mv(1, 1), mv(1, 0)], [(d, f32)], name="l1_norm_mix", **kw)[0]
    g2n = sm["a_re"].shape[1]
    nq = d // 128
    ar2, ai2 = sm["a_re"].reshape(2 * g2n, SSM_STATE), sm["a_im"].reshape(2 * g2n, SSM_STATE)
    ldt2 = sm["log_dt"].reshape(2 * g2n, 1)
    bt_re = sm["b_re"].transpose(3, 0, 1, 2).reshape(SSM_GROUP, 2 * g2n, SSM_STATE)
    bt_im = sm["b_im"].transpose(3, 0, 1, 2).reshape(SSM_GROUP, 2 * g2n, SSM_STATE)
    disc_in = [ar2, ai2, ldt2, bt_re, bt_im]
    lam_r, lam_i, bbar_r, bbar_i = _whole(
        _f_discretise, disc_in,
        [((2 * g2n, SSM_STATE), f32)] * 2 + [((SSM_GROUP, 2 * g2n, SSM_STATE), f32)] * 2, "s5_discretise")
    eye = jnp.eye(GROUPS_PER_CHUNK, dtype=f32)
    eye6 = eye[None, None, :, None, :, None]

    def blockdiag_b(bbar):
        t = bbar.reshape(SSM_GROUP, 2, nq, GROUPS_PER_CHUNK, SSM_STATE).transpose(1, 2, 3, 0, 4)
        return (t[:, :, :, :, None, :] * eye6).reshape(2, nq, 128, N_STATE)

    def blockdiag_c(cw):
        t = cw.reshape(2, nq, GROUPS_PER_CHUNK, SSM_GROUP, SSM_STATE).transpose(0, 1, 2, 4, 3)
        return (t[:, :, :, :, None, :] * eye6).reshape(2, nq, N_STATE, 128)

    lam = jnp.stack([lam_r.reshape(2, nq, N_STATE), lam_i.reshape(2, nq, N_STATE)], axis=2)
    bblk = jnp.stack([blockdiag_b(bbar_r), blockdiag_b(bbar_i)], axis=2)
    cblk = jnp.stack([blockdiag_c(sm["c_re"]), blockdiag_c(sm["c_im"])], axis=2)
    bblk16, cblk16 = bblk.astype(bf16), cblk.astype(bf16)
    bblk_t, cblk_t = bblk16.transpose(0, 1, 2, 4, 3), cblk16.transpose(0, 1, 2, 4, 3)
    ys, cps = [], []
    for dr in range(2):
        yd, cp = _s5_fwd(hs, lam[dr], bblk16[dr], cblk16[dr], n_x, dr == 1, "l1_s5_fwd%d" % dr)
        ys.append(yd)
        cps.append(cp)
    dvec = sm["ssm_d"].reshape(1, 1, d)
    gy = _rowwise(_f_glu_pre, [hs, ys[0], ys[1]], [dvec], [(d, bf16)], name="l1_glu_pre", **kx)[0]
    zv = _mm(gy, wts["w_glu_v"], "nn", f32, "l1_glu_val")
    zg = _mm(gy, wts["w_glu_g"], "nn", f32, "l1_glu_gate")
    bv, bg = sm["b_glu"][:d].reshape(1, 1, d), sm["b_glu"][d:].reshape(1, 1, d)
    xd = _rowwise(_f_glu_post, [xc, zv, zg], [mx(1, 2), bv, bg], [(d, f32)], name="l1_glu_post", **kx)[0]
    xe, ffn1 = ffn_fwd(xd, 1, mx, kx, "l1")

    d_xe, d_nfinal, loss_blk = _loss_head(xe, tgt, sm["norm_final"].reshape(1, d), nrows=n_x, name="loss_head")
    d_xd, dw1_1, dw3_1, dw2_1, dnffn1, dsc2_1, dsh2_1, dg2_1 = ffn_bwd(d_xe, xd, ffn1, 1, mx, kx, "l1")
    dzv, dzg, dg1_1, dbv, dbg = _rowwise_bwd(_f_glu_post, [xc, zv, zg], [mx(1, 2), bv, bg], [d_xd],
                                             {1: bf16, 2: bf16}, name="l1_glu_post_bwd", **kx)
    dgy = _mm(dzv, wts["w_glu_v"], "nt", f32, "l1_glu_dx", a2=dzg, b2=wts["w_glu_g"])
    dwglu_v = _mm(gy, dzv, "tn", f32, "l1_glu_val_dw")
    dwglu_g = _mm(gy, dzg, "tn", f32, "l1_glu_gate_dw")
    dy, d_dvec = _rowwise_bwd(_f_glu_pre, [hs, ys[0], ys[1]], [dvec], [dgy], {1: f32}, name="l1_glu_pre_bwd", **kx)
    du0, dlam0, db0, dc0 = _s5_bwd(hs, dy, cps[0], lam[0], bblk16[0], cblk16[0], bblk_t[0], cblk_t[0], n_x, False,
                                   "l1_s5_bwd0", skip=sm["ssm_d"].reshape(1, d))
    du1, dlam1, db1, dc1 = _s5_bwd(hs, dy, cps[1], lam[1], bblk16[1], cblk16[1], bblk_t[1], cblk_t[1], n_x, True,
                                   "l1_s5_bwd1", add=du0)
    d_xc, dnmix1, dsc1_1, dsh1_1 = _rowwise_bwd(
        _f_normmod, [xc], [_vec(nmix[1:2]), mv(1, 1), mv(1, 0)], [du1], {0: f32},
        name="l1_norm_mix_bwd", residual=d_xd, residual_rows=n_x, **kw)
    dlam = jnp.stack([dlam0, dlam1])
    dlam_r, dlam_i = dlam[:, :, 0].reshape(2 * g2n, SSM_STATE), dlam[:, :, 1].reshape(2 * g2n, SSM_STATE)
    dbb = jnp.stack([db0, db1]).reshape(2, nq, 2, GROUPS_PER_CHUNK, SSM_GROUP, GROUPS_PER_CHUNK, SSM_STATE)
    eye7 = eye[None, None, None, :, None, :, None]
    dbbar = (dbb * eye7).sum(axis=5).transpose(2, 4, 0, 1, 3, 5).reshape(2, SSM_GROUP, 2 * g2n, SSM_STATE)
    dcc = jnp.stack([dc0, dc1]).reshape(2, nq, 2, GROUPS_PER_CHUNK, SSM_STATE, GROUPS_PER_CHUNK, SSM_GROUP)
    dcw = (dcc * eye7).sum(axis=5).transpose(2, 0, 1, 3, 5, 4).reshape(2, 2, g2n, SSM_GROUP, SSM_STATE)
    d_ar, d_ai, d_ldt, d_btr, d_bti = _whole_bwd(_f_discretise, disc_in, [dlam_r, dlam_i, dbbar[0], dbbar[1]],
                                                 "s5_discretise_bwd")
    to_b = lambda t: t.reshape(SSM_GROUP, 2, g2n, SSM_STATE).transpose(1, 2, 3, 0)

    d_xb, dw1_0, dw3_0, dw2_0, dnffn0, dsc2_0, dsh2_0, dg2_0 = ffn_bwd(d_xc, xb, ffn0, 0, mv, kw, "l0")
    dy1, dg1_0 = _rowwise_bwd(_f_gated_add, [xa, y1], [mv(0, 2)], [d_xb], {1: bf16}, name="l0_res_mix_bwd", **kw)
    d_o = _mm(dy1, wts["w_out"], "nt", bf16, "l0_out_proj_dx")
    dw_out = _mm(o, dy1, "tn", f32, "l0_out_proj_dw")
    dqa, delta_a, dbias = _attn_bwd_q(qkv, oa, d_o, lse_a, cfg_a, bias, "l0_na_bwd_q")
    dka, dva = _attn_bwd_kv(qkv, d_o, lse_a, delta_a, cfg_a, bias_t, "l0_na_bwd_kv")
    dqb, delta_b, dsink_rows = _attn_bwd_q(qkv, ob, d_o, lse_b, cfg_b, sink_rows, "l0_swa_bwd_q")
    dkb, dvb = _attn_bwd_kv(qkv, d_o, lse_b, delta_b, cfg_b, None, "l0_swa_bwd_kv")
    d_qkv = _qkv_post_bwd([dqa, dka, dva, dqb, dkb, dvb], cos, sin, "l0_qkv_post_bwd")
    dh1 = _mm(d_qkv, wts["w_in"], "nt", f32, "l0_in_proj_dx")
    dw_in = _mm(h1, d_qkv, "tn", f32, "l0_in_proj_dw")
    d_xa, dnmix0, dsc1_0, dsh1_0 = _rowwise_bwd(
        _f_normmod, [xa], [_vec(nmix[0:1]), mv(0, 1), mv(0, 0)], [dh1], {0: f32},
        name="l0_norm_mix_bwd", residual=d_xb, **kw)
    dbias5 = dbias.reshape(NA_HEADS, 4, GRID_W, 12, GRID_W).transpose(0, 1, 3, 2, 4)
    dtz = sum(jnp.pad(dbias5[:, a], ((0, 0), (3 - a, a), (0, 0), (0, 0))) for a in range(4))
    dtz2 = jnp.zeros((128, GRID_W * GRID_W), f32).at[:NA_HEADS * 15].set(dtz.reshape(NA_HEADS * 15, -1))
    d_rpb = _small_dot(dtz2, basis.T, "rpb_expand_bwd")[:NA_HEADS * 15, :31].reshape(sm["rpb"].shape)
    d_sink = dsink_rows.reshape(NB_KV_HEADS * NB_GROUP, BLK).sum(axis=1)

    zero_c = jnp.zeros((1, 1, d), f32)
    both = lambda gx: jnp.concatenate([gx, zero_c], axis=0)
    dmods = jnp.stack([
        jnp.stack([dsh1_0, dsc1_0, dg1_0, dsh2_0, dsc2_0, dg2_0], axis=2),
        jnp.stack([dsh1_1, dsc1_1, both(dg1_1), both(dsh2_1), both(dsc2_1), both(dg2_1)], axis=2),
    ])[:, :, 0]
    big = dict(w1=jnp.stack([dw1_0, dw1_1]), w3=jnp.stack([dw3_0, dw3_1]), w2=jnp.stack([dw2_0, dw2_1]),
               w_in=dw_in, w_out=dw_out, w_glu=jnp.concatenate([dwglu_v, dwglu_g], axis=1))
    small = dict(
        norm_mix=jnp.concatenate([dnmix0[0], dnmix1[0]]), norm_ffn=jnp.concatenate([dnffn0[0], dnffn1[0]]),
        rpb=d_rpb, sink=d_sink, a_re=d_ar.reshape(sm["a_re"].shape), a_im=d_ai.reshape(sm["a_im"].shape),
        log_dt=d_ldt.reshape(sm["log_dt"].shape), b_re=to_b(d_btr), b_im=to_b(d_bti), c_re=dcw[0], c_im=dcw[1],
        ssm_d=d_dvec.reshape(d), b_glu=jnp.concatenate([dbv.reshape(d), dbg.reshape(d)]),
        norm_final=d_nfinal.reshape(d))
    return loss_blk[0, 0], d_xa, big, small, dmods


WEIGHTS = ("c_ctx", "ada_w", "ada_b", "norm_mix", "norm_ffn", "ffn_w1", "ffn_w3", "ffn_w2", "attn_w_in",
           "attn_w_out", "attn_rpb", "attn_sink", "ssm_a_re", "ssm_a_im", "ssm_log_dt", "ssm_b_re", "ssm_b_im",
           "ssm_c_re", "ssm_c_im", "ssm_d", "ssm_w_glu", "ssm_b_glu", "norm_final")
SHARDED_BIG = ("ffn_w1", "ffn_w3", "ffn_w2", "attn_w_in", "attn_w_out", "ssm_w_glu")
BIG = ("ada_w",) + SHARDED_BIG
SMALL = tuple(n for n in WEIGHTS if n not in BIG)


def _blocks_to_full(blk, kind):
    if kind == "cols2":
        t = blk.transpose(2, 3, 1, 0, 4)
        return t.reshape(t.shape[0], t.shape[1], -1)
    if kind == "rows2":
        t = blk.transpose(2, 1, 0, 3, 4)
        return t.reshape(t.shape[0], -1, t.shape[4])
    raise ValueError(kind)


def _full_to_blocks(full, kind):
    if kind == "cols2":
        l, k, n = full.shape
        return full.reshape(l, k, 4, 2, n // N_DEV).transpose(3, 2, 0, 1, 4)
    l, n, k = full.shape
    return full.reshape(l, 4, 2, n // N_DEV, k).transpose(2, 1, 0, 3, 4)


BIG_KIND = dict(ffn_w1="cols2", ffn_w3="cols2", ffn_w2="rows2", attn_w_in="cols2", attn_w_out="rows2",
                ssm_w_glu="cols2")


def kernel(x, c, ctx, c_ctx, ada_w, ada_b, norm_mix, norm_ffn, ffn_w1, ffn_w3, ffn_w2, attn_w_in, attn_w_out, attn_rpb, attn_sink, ssm_a_re, ssm_a_im, ssm_log_dt, ssm_b_re, ssm_b_im, ssm_c_re, ssm_c_im, ssm_d, ssm_w_glu, ssm_b_glu, norm_final, loss_target, m_c_ctx, m_ada_w, m_ada_b, m_norm_mix, m_norm_ffn, m_ffn_w1, m_ffn_w3, m_ffn_w2, m_attn_w_in, m_attn_w_out, m_attn_rpb, m_attn_sink, m_ssm_a_re, m_ssm_a_im, m_ssm_log_dt, m_ssm_b_re, m_ssm_b_im, m_ssm_c_re, m_ssm_c_im, m_ssm_d, m_ssm_w_glu, m_ssm_b_glu, m_norm_final, v_c_ctx, v_ada_w, v_ada_b, v_norm_mix, v_norm_ffn, v_ffn_w1, v_ffn_w3, v_ffn_w2, v_attn_w_in, v_attn_w_out, v_attn_rpb, v_attn_sink, v_ssm_a_re, v_ssm_a_im, v_ssm_log_dt, v_ssm_b_re, v_ssm_b_im, v_ssm_c_re, v_ssm_c_im, v_ssm_d, v_ssm_w_glu, v_ssm_b_glu, v_norm_final):
    p = dict(locals())
    w = {n: p[n] for n in WEIGHTS}
    me = 4 * lax.axis_index("x") + 2 * lax.axis_index("y") + lax.axis_index("c")
    n_x, d = x.shape[1], x.shape[2]
    cols = ada_w.shape[2]
    d8 = d // N_DEV

    first = jnp.concatenate([c[0], ssm_d[0], ssm_b_glu[0]])[None]
    g0 = _exchange(first, "xyc", True, "gather_vectors")[:, 0]
    c_all, d_full, bglu_full = g0[:, :d], g0[:, d:d + d8].reshape(d), g0[:, d + d8:].reshape(2 * d)
    cc = jnp.concatenate([c_all, c_ctx[None], jnp.zeros((16 - N_DEV - 1, d), f32)])
    sc_all = _whole(_f_silu, [cc], [((16, d), f32)], "silu_c")[0]
    my_cols = lambda a: lax.dynamic_slice_in_dim(a, me * cols, cols, axis=a.ndim - 1)
    mod_loc = jnp.stack([_mm(sc_all, ada_w[l], "nn", f32, "ada_mod%d" % l) for l in range(2)])
    mod_loc = mod_loc + my_cols(ada_b)[:, None, :]
    mg = _exchange(mod_loc.reshape(32, cols), "xyc", True, "gather_mod")
    mod_all = mg.reshape(N_DEV, 2, 16, cols).transpose(1, 2, 0, 3).reshape(2, 16, N_DEV * cols)
    mod_x = lax.dynamic_index_in_dim(mod_all, me, axis=1, keepdims=False)
    mods = jnp.stack([mod_x, mod_all[:, N_DEV]], axis=1).reshape(2, 2, 6, d)

    local_shapes = [w[n].shape for n in SHARDED_BIG]
    wpack = _pack([w[n] for n in SHARDED_BIG], bf16)
    wg = _exchange(_exchange(wpack, "xy", True, "gather_w_xy").reshape(-1, LANES), "c", True, "gather_w_c")
    wflat = wg.reshape(2, 4, -1)
    full, off = {}, 0
    for n, s in zip(SHARDED_BIG, local_shapes):
        size = int(np.prod(s))
        full[n] = _blocks_to_full(wflat[:, :, off:off + size].reshape((2, 4) + s), BIG_KIND[n])
        off += size
    wts = dict(w1=[full["ffn_w1"][0], full["ffn_w1"][1]], w3=[full["ffn_w3"][0], full["ffn_w3"][1]],
               w2=[full["ffn_w2"][0], full["ffn_w2"][1]], w_in=full["attn_w_in"][0], w_out=full["attn_w_out"][0],
               w_glu_v=full["ssm_w_glu"][0][:, :d], w_glu_g=full["ssm_w_glu"][0][:, d:])
    sm = dict(norm_mix=norm_mix, norm_ffn=norm_ffn, rpb=attn_rpb[0], sink=attn_sink[0], a_re=ssm_a_re[0],
              a_im=ssm_a_im[0], log_dt=ssm_log_dt[0], b_re=ssm_b_re[0], b_im=ssm_b_im[0], c_re=ssm_c_re[0],
              c_im=ssm_c_im[0], ssm_d=d_full, b_glu=bglu_full, norm_final=norm_final)

    xa = jnp.concatenate([x[0], ctx[0]], axis=0)
    loss_part, d_xa, big, small, dmods = _local_step(xa, loss_target[0], mods, wts, sm, n_x)

    gfull = dict(ffn_w1=big["w1"], ffn_w3=big["w3"], ffn_w2=big["w2"], attn_w_in=big["w_in"][None],
                 attn_w_out=big["w_out"][None], ssm_w_glu=big["w_glu"][None])
    gflat = jnp.concatenate([_full_to_blocks(gfull[n], BIG_KIND[n]).reshape(2, 4, -1) for n in SHARDED_BIG], axis=2)
    pad = (-gflat.shape[2]) % (16 * LANES)
    gpack = jnp.pad(gflat, ((0, 0), (0, 0), (0, pad))).reshape(2, -1, LANES)
    rows = gpack.shape[1] // 4
    r1 = _exchange(gpack, "c", False, "scatter_g_c")
    s1 = _sum0(r1, "sum_g_c").reshape(4, rows, LANES)
    r2 = _exchange(s1, "xy", False, "scatter_g_xy")
    gmine = _unpack(_sum0(r2, "sum_g_xy").reshape(-1), local_shapes)
    grads = dict(zip(SHARDED_BIG, gmine))

    small_names = ("norm_mix", "norm_ffn", "rpb", "sink", "a_re", "a_im", "log_dt", "b_re", "b_im", "c_re", "c_im",
                   "ssm_d", "b_glu", "norm_final")
    parts = [loss_part.reshape(1)] + [small[n] for n in small_names] + [dmods[:, 0].reshape(2, 6 * d),
                                                                        dmods[:, 1].reshape(2, 6 * d)]
    shapes = [q.shape for q in parts]
    sg = _exchange(_pack(parts, f32), "xyc", True, "gather_small")
    tot = _unpack(_sum0(sg, "sum_small").reshape(-1), shapes)
    loss = tot[0][0]
    ts = dict(zip(small_names, tot[1:1 + len(small_names)]))
    tot_dmod_x, tot_dmod_c = tot[-2], tot[-1]
    off_x = sum(int(np.prod(s)) for s in shapes[:-2])
    dmod_x_all = sg.reshape(N_DEV, -1)[:, off_x:off_x + 12 * d].reshape(N_DEV, 2, 6 * d)
    dm = jnp.concatenate([dmod_x_all.transpose(1, 0, 2), tot_dmod_c[:, None, :],
                          jnp.zeros((2, 16 - N_DEV - 1, 6 * d), f32)], axis=1)
    dm_loc = my_cols(dm)
    grads["ada_w"] = jnp.stack([_mm(sc_all, dm_loc[l], "tn", f32, "ada_w_grad%d" % l) for l in range(2)])
    grads["ada_b"] = tot_dmod_x + tot_dmod_c
    dsc_part = _mm(dm_loc[0], ada_w[0], "nt", f32, "silu_c_grad", a2=dm_loc[1], b2=ada_w[1])[N_DEV:N_DEV + 1]
    dsc = _sum0(_exchange(dsc_part, "xyc", True, "gather_cctx"), "sum_cctx")
    grads["c_ctx"] = _whole_bwd(_f_silu, [c_ctx[None]], [dsc], "silu_cctx_bwd")[0][0]
    grads.update(norm_mix=ts["norm_mix"], norm_ffn=ts["norm_ffn"], attn_rpb=ts["rpb"][None],
                 attn_sink=ts["sink"][None], ssm_a_re=ts["a_re"][None], ssm_a_im=ts["a_im"][None],
                 ssm_log_dt=ts["log_dt"][None], ssm_b_re=ts["b_re"][None], ssm_b_im=ts["b_im"][None],
                 ssm_c_re=ts["c_re"][None], ssm_c_im=ts["c_im"][None],
                 ssm_d=lax.dynamic_slice_in_dim(ts["ssm_d"], me * d8, d8)[None],
                 ssm_b_glu=lax.dynamic_slice_in_dim(ts["b_glu"], me * 2 * d8, 2 * d8)[None],
                 norm_final=ts["norm_final"])

    delta, new_m, new_v = {}, {}, {}
    for n in BIG:
        two_d = lambda a: a.reshape(-1, a.shape[-1])
        dl, m2, v2 = _adamw(two_d(w[n]), two_d(grads[n]), two_d(p["m_" + n]), two_d(p["v_" + n]), "adamw_" + n)
        delta[n], new_m[n], new_v[n] = (t.reshape(w[n].shape) for t in (dl, m2, v2))
    sshapes = [w[n].shape for n in SMALL]
    packs = [_pack([src[n] for n in SMALL], f32) for src in
             (w, grads, {n: p["m_" + n] for n in SMALL}, {n: p["v_" + n] for n in SMALL})]
    for store, t in zip((delta, new_m, new_v), _adamw(*packs, "adamw_small")):
        store.update(zip(SMALL, _unpack(t.reshape(-1), sshapes)))

    return (loss, d_xa[:n_x][None], *[grads[n] for n in WEIGHTS], *[delta[n] for n in WEIGHTS],
            *[new_m[n] for n in WEIGHTS], *[new_v[n] for n in WEIGHTS])
```

```python
import functools
import math

import numpy as np
import jax
import jax.numpy as jnp
from jax import lax
from jax.experimental import pallas as pl
from jax.experimental.pallas import tpu as pltpu

f32, bf16 = jnp.float32, jnp.bfloat16

HEAD_DIM = 128
GRID_W = 64
NA_HEADS = 8
NB_Q_HEADS = 8
NB_KV_HEADS = 2
NB_GROUP = NB_Q_HEADS // NB_KV_HEADS
NA_ROWS = 8
NA_COLS = 16
SW_RADIUS = 128
ROPE_BASE = 10000.0
SSM_GROUP = 16
SSM_STATE = 64
EPS = 1e-6
NEG_INF = -1e30
ADAM_LR, ADAM_B1, ADAM_B2, ADAM_EPS, ADAM_WD, ADAM_STEP = 0.001, 0.9, 0.999, 1e-08, 0.01, 10

N_DEV = 8
BLK = 256
SCAN_SEG = 8
GROUPS_PER_CHUNK = 8
V7X_VMEM_LIMIT = 56 * 2 ** 20
MM_VMEM_BUDGET = 36 * 2 ** 20

NT = (((1,), (1,)), ((), ()))
TN = (((0,), (0,)), ((), ()))


def _params(sem):
    return pltpu.CompilerParams(dimension_semantics=sem, vmem_limit_bytes=V7X_VMEM_LIMIT)


def _pick(n, cands):
    for c in cands:
        if n % c == 0:
            return c
    return n


def _mm(a, b, mode, out_dtype, name, a2=None, b2=None):
    if mode == "tn":
        kdim, m, n = a.shape[0], a.shape[1], b.shape[1]
    elif mode == "nt":
        m, kdim, n = a.shape[0], a.shape[1], b.shape[0]
    else:
        m, kdim, n = a.shape[0], a.shape[1], b.shape[1]
    if mode == "tn":
        tm = _pick(m, (2048, 1024, 512, 256, 128))
        tk = _pick(kdim, (1024, 768, 512, 256, 128))
    else:
        tm = _pick(m, (1024, 768, 512, 256, 128))
        tk = _pick(kdim, (2048, 1024, 768, 512, 256, 128))
    in_bytes = max(a.dtype.itemsize, b.dtype.itemsize) * (1 if a2 is None else 2)
    out_bytes = jnp.dtype(out_dtype).itemsize

    def vmem(tn_):
        return 2 * in_bytes * tk * (tm + tn_) + (4 + 2 * out_bytes) * tm * tn_

    tn = next((t for t in (2048, 1024, 512, 256, 128) if n % t == 0 and vmem(t) <= MM_VMEM_BUDGET), n)
    nk = kdim // tk
    if mode == "tn":
        a_spec = pl.BlockSpec((tk, tm), lambda i, j, k: (k, i))
        b_spec = pl.BlockSpec((tk, tn), lambda i, j, k: (k, j))
        dims = TN
    elif mode == "nt":
        a_spec = pl.BlockSpec((tm, tk), lambda i, j, k: (i, k))
        b_spec = pl.BlockSpec((tn, tk), lambda i, j, k: (j, k))
        dims = NT
    else:
        a_spec = pl.BlockSpec((tm, tk), lambda i, j, k: (i, k))
        b_spec = pl.BlockSpec((tk, tn), lambda i, j, k: (k, j))
        dims = (((1,), (0,)), ((), ()))

    pairs = 1 if a2 is None else 2

    def body(*refs):
        o_ref, acc_ref = refs[2 * pairs:]
        k = pl.program_id(2)

        @pl.when(k == 0)
        def _():
            acc_ref[...] = jnp.zeros_like(acc_ref)

        for p in range(pairs):
            acc_ref[...] += lax.dot_general(refs[2 * p][...].astype(bf16), refs[2 * p + 1][...].astype(bf16), dims,
                                            preferred_element_type=f32)

        @pl.when(k == nk - 1)
        def _():
            o_ref[...] = acc_ref[...].astype(o_ref.dtype)

    args = (a, b) if a2 is None else (a, b, a2, b2)
    return pl.pallas_call(
        body, name=name, grid=(m // tm, n // tn, nk),
        in_specs=[a_spec, b_spec] * pairs, out_specs=pl.BlockSpec((tm, tn), lambda i, j, k: (i, j)),
        out_shape=jax.ShapeDtypeStruct((m, n), out_dtype),
        scratch_shapes=[pltpu.VMEM((tm, tn), f32)],
        compiler_params=_params(("parallel", "parallel", "arbitrary")),
    )(*args)


def _small_dot(a, b, name):
    def body(a_ref, b_ref, o_ref):
        o_ref[...] = jnp.dot(a_ref[...], b_ref[...], precision=lax.Precision.HIGHEST, preferred_element_type=f32)

    return pl.pallas_call(body, name=name, out_shape=jax.ShapeDtypeStruct((a.shape[0], b.shape[1]), f32))(a, b)


def _group_of(i, n_x_tiles, n_groups):
    return jnp.where(i >= n_x_tiles, n_groups - 1, 0)


def _rowwise(f, rows, vecs, outs, *, nrows, n_x_rows, name, tm=BLK, tc=None):
    n_x_tiles = n_x_rows // tm
    grid = (nrows // tm,) if tc is None else (nrows // tm, rows[0].shape[1] // tc)

    def rspec(cols):
        if tc is None:
            return pl.BlockSpec((tm, cols), lambda i: (i, 0))
        return pl.BlockSpec((tm, tc), lambda i, j: (i, j))

    def vspec(v):
        g = v.shape[0]
        if tc is None:
            return pl.BlockSpec((None, 1, v.shape[2]), lambda i: (_group_of(i, n_x_tiles, g), 0, 0))
        return pl.BlockSpec((None, 1, tc), lambda i, j: (_group_of(i, n_x_tiles, g), 0, j))

    nr, nv = len(rows), len(vecs)

    def body(*refs):
        ins = [r[...] for r in refs[:nr + nv]]
        res = f(*ins)
        for o_ref, val in zip(refs[nr + nv:], res):
            o_ref[...] = val.astype(o_ref.dtype)

    return pl.pallas_call(
        body, name=name, grid=grid,
        in_specs=[rspec(r.shape[1]) for r in rows] + [vspec(v) for v in vecs],
        out_specs=[rspec(c) for c, _ in outs],
        out_shape=[jax.ShapeDtypeStruct((nrows, c), d) for c, d in outs],
        compiler_params=_params(("parallel",) * len(grid)),
    )(*rows, *vecs)


def _rowwise_bwd(f, rows, vecs, cts, row_grads, *, nrows, n_x_rows, name, tm=BLK, tc=None, residual=None,
                 residual_rows=None):
    n_x_tiles = n_x_rows // tm
    n_tiles = nrows // tm
    res_tiles = None if residual_rows is None else residual_rows // tm
    grid = (n_tiles,) if tc is None else (rows[0].shape[1] // tc, n_tiles)
    row_of = (lambda *g: g[0]) if tc is None else (lambda *g: g[1])

    def rspec(cols):
        if tc is None:
            return pl.BlockSpec((tm, cols), lambda i: (i, 0))
        return pl.BlockSpec((tm, tc), lambda j, i: (i, j))

    def vspec(v):
        g = v.shape[0]
        if tc is None:
            return pl.BlockSpec((None, 1, v.shape[2]), lambda i: (_group_of(i, n_x_tiles, g), 0, 0))
        return pl.BlockSpec((None, 1, tc), lambda j, i: (_group_of(i, n_x_tiles, g), 0, j))

    nr, nv, nc = len(rows), len(vecs), len(cts)
    gidx = sorted(row_grads)

    def body(*refs):
        i = row_of(*[pl.program_id(d) for d in range(len(grid))])
        row_vals = [r[...].astype(f32) for r in refs[:nr]]
        vec_vals = [jnp.broadcast_to(r[...].astype(f32), (tm, r.shape[-1])) for r in refs[nr:nr + nv]]
        ct_vals = [r[...].astype(f32) for r in refs[nr + nv:nr + nv + nc]]
        out_refs = refs[nr + nv + nc + (residual is not None):]
        res, vjp = jax.vjp(lambda *a: tuple(o.astype(f32) for o in f(*a)), *row_vals, *vec_vals)
        grads = list(vjp(tuple(ct_vals)))
        if residual is not None:
            extra = refs[nr + nv + nc][...]
            if res_tiles is not None:
                extra = jnp.where(i < res_tiles, extra, 0.0)
            grads[0] = grads[0] + extra
        for o_ref, k in zip(out_refs[:len(gidx)], gidx):
            o_ref[...] = grads[k].astype(o_ref.dtype)
        for o_ref, g, v in zip(out_refs[len(gidx):], grads[nr:], vecs):
            part = jnp.sum(g, axis=0, keepdims=True)
            first = (i == 0) if v.shape[0] == 1 else ((i == 0) | (i == n_x_tiles))

            @pl.when(first)
            def _():
                o_ref[...] = part

            @pl.when(jnp.logical_not(first))
            def _():
                o_ref[...] += part

    sem = ("arbitrary",) if tc is None else ("parallel", "arbitrary")
    res_specs, res_args = [], []
    if residual is not None:
        assert tc is None
        clamp = (lambda i: i) if res_tiles is None else (lambda i: jnp.minimum(i, res_tiles - 1))
        res_specs = [pl.BlockSpec((tm, residual.shape[1]), lambda i: (clamp(i), 0))]
        res_args = [residual]
    return pl.pallas_call(
        body, name=name, grid=grid,
        in_specs=[rspec(r.shape[1]) for r in rows] + [vspec(v) for v in vecs] + [rspec(c.shape[1]) for c in cts]
        + res_specs,
        out_specs=[rspec(rows[k].shape[1]) for k in gidx] + [vspec(v) for v in vecs],
        out_shape=[jax.ShapeDtypeStruct((nrows, rows[k].shape[1]), row_grads[k]) for k in gidx]
        + [jax.ShapeDtypeStruct(v.shape, f32) for v in vecs],
        compiler_params=_params(sem),
    )(*rows, *vecs, *cts, *res_args)


def _rms(x, w):
    return x * lax.rsqrt(jnp.mean(x * x, axis=-1, keepdims=True) + EPS) * w


def _f_normmod(x, w, sc, sh):
    return (_rms(x.astype(f32), w) * (1.0 + sc) + sh,)


def _f_gated_add(x, y, g):
    return (x + g * y.astype(f32),)


def _f_swiglu(a, b):
    a = a.astype(f32)
    return (jax.nn.silu(a) * b.astype(f32),)


def _f_glu_pre(u, yf, yr, d):
    return (jax.nn.gelu(d * u + yf + yr),)


def _f_glu_post(x, zv, zg, g, bv, bg):
    return (x + g * ((zv.astype(f32) + bv) * jax.nn.sigmoid(zg.astype(f32) + bg)),)


def _f_scale(y, g):
    return (g * y.astype(f32),)


def _f_silu(x):
    return (jax.nn.silu(x.astype(f32)),)


def _loss_head(x, tgt, w, *, nrows, name):
    d = x.shape[1]
    tm = BLK

    def body(x_ref, t_ref, w_ref, dx_ref, dw_ref, loss_ref):
        i = pl.program_id(0)
        wb = jnp.broadcast_to(w_ref[...], (tm, d))
        y, vjp = jax.vjp(_rms, x_ref[...], wb)
        e = y - t_ref[...]
        dx, dwb = vjp(e * (1.0 / d))
        dx_ref[...] = dx
        dw = jnp.sum(dwb, axis=0, keepdims=True)
        part = jnp.full((8, 128), 0.5 / d, f32) * jnp.sum(e * e)

        @pl.when(i == 0)
        def _():
            dw_ref[...] = dw
            loss_ref[...] = part

        @pl.when(i > 0)
        def _():
            dw_ref[...] += dw
            loss_ref[...] += part

    row = pl.BlockSpec((tm, d), lambda i: (i, 0))
    return pl.pallas_call(
        body, name=name, grid=(nrows // tm,),
        in_specs=[row, row, pl.BlockSpec((1, d), lambda i: (0, 0))],
        out_specs=[row, pl.BlockSpec((1, d), lambda i: (0, 0)), pl.BlockSpec((8, 128), lambda i: (0, 0))],
        out_shape=[jax.ShapeDtypeStruct((nrows, d), f32), jax.ShapeDtypeStruct((1, d), f32),
                   jax.ShapeDtypeStruct((8, 128), f32)],
        compiler_params=_params(("arbitrary",)),
    )(x, tgt, w)


def _adamw(w, g, m, v, name):
    r, c = w.shape
    tr = _pick(r, (512, 256, 128, 64, 32, 16, 8))
    tcol = _pick(c, (1024, 512)) if c % 128 == 0 else c

    def body(w_ref, g_ref, m_ref, v_ref, d_ref, m2_ref, v2_ref):
        gg = g_ref[...]
        m2 = ADAM_B1 * m_ref[...] + (1.0 - ADAM_B1) * gg
        v2 = ADAM_B2 * v_ref[...] + (1.0 - ADAM_B2) * (gg * gg)
        m_hat = m2 / (1.0 - ADAM_B1 ** ADAM_STEP)
        v_hat = v2 / (1.0 - ADAM_B2 ** ADAM_STEP)
        d_ref[...] = -ADAM_LR * (m_hat / (jnp.sqrt(v_hat) + ADAM_EPS) + ADAM_WD * w_ref[...])
        m2_ref[...] = m2
        v2_ref[...] = v2

    spec = pl.BlockSpec((tr, tcol), lambda i, j: (i, j))
    return pl.pallas_call(
        body, name=name, grid=(r // tr, c // tcol), in_specs=[spec] * 4, out_specs=[spec] * 3,
        out_shape=[jax.ShapeDtypeStruct((r, c), f32)] * 3,
        compiler_params=_params(("parallel", "parallel")),
    )(w, g, m, v)


def _swap_quarters(x):
    lane = lax.broadcasted_iota(jnp.int32, x.shape, 1)
    return jnp.where((lane & 63) < 32, pltpu.roll(x, 96, 1), pltpu.roll(x, 32, 1))


def _rope_tables(n_x, n_ctx):
    t = np.arange(n_x)
    quarter = HEAD_DIM // 4
    inv = ROPE_BASE ** (-np.arange(quarter, dtype=np.float64) / quarter)
    ar = (t // GRID_W)[:, None] * inv[None]
    ac = (t % GRID_W)[:, None] * inv[None]
    cos = np.concatenate([np.cos(ar), np.cos(ar), np.cos(ac), np.cos(ac)], axis=1)
    sin = np.concatenate([-np.sin(ar), np.sin(ar), -np.sin(ac), np.sin(ac)], axis=1)
    cos = np.concatenate([cos, np.ones((n_ctx, HEAD_DIM))], axis=0)
    sin = np.concatenate([sin, np.zeros((n_ctx, HEAD_DIM))], axis=0)
    return jnp.asarray(cos, f32), jnp.asarray(sin, f32)


A_W = NA_HEADS * HEAD_DIM
QB0, KB0, VB0 = 3 * A_W, 3 * A_W + NB_Q_HEADS * HEAD_DIM, 3 * A_W + (NB_Q_HEADS + NB_KV_HEADS) * HEAD_DIM
IN_W = VB0 + NB_KV_HEADS * HEAD_DIM


def _qkv_post(qkv, cos, sin, name):
    n = qkv.shape[0]

    def body(x_ref, c_ref, s_ref, o_ref):
        c, s = c_ref[...], s_ref[...]
        o_ref[:, :QB0] = x_ref[:, :QB0].astype(bf16)
        for col in range(QB0, VB0, HEAD_DIM):
            x = x_ref[:, col:col + HEAD_DIM]
            o_ref[:, col:col + HEAD_DIM] = (x * c + _swap_quarters(x) * s).astype(bf16)
        o_ref[:, VB0:] = x_ref[:, VB0:].astype(bf16)

    row = lambda c: pl.BlockSpec((BLK, c), lambda i: (i, 0))
    return pl.pallas_call(
        body, name=name, grid=(n // BLK,), in_specs=[row(IN_W), row(HEAD_DIM), row(HEAD_DIM)],
        out_specs=row(IN_W), out_shape=jax.ShapeDtypeStruct((n, IN_W), bf16),
        compiler_params=_params(("parallel",)),
    )(qkv, cos, sin)


def _qkv_post_bwd(parts, cos, sin, name):
    n = parts[0].shape[0]

    def body(qa, ka, va, qb, kb, vb, c_ref, s_ref, o_ref):
        c, s = c_ref[...], s_ref[...]
        o_ref[:, 0:A_W] = qa[...]
        o_ref[:, A_W:2 * A_W] = ka[...]
        o_ref[:, 2 * A_W:QB0] = va[...]
        for src, col0, width in ((qb, QB0, KB0 - QB0), (kb, KB0, VB0 - KB0)):
            for off in range(0, width, HEAD_DIM):
                g = src[:, off:off + HEAD_DIM].astype(f32)
                o_ref[:, col0 + off:col0 + off + HEAD_DIM] = (g * c + _swap_quarters(g * s)).astype(bf16)
        o_ref[:, VB0:] = vb[...]

    row = lambda c: pl.BlockSpec((BLK, c), lambda i: (i, 0))
    return pl.pallas_call(
        body, name=name, grid=(n // BLK,),
        in_specs=[row(p.shape[1]) for p in parts] + [row(HEAD_DIM), row(HEAD_DIM)],
        out_specs=row(IN_W), out_shape=jax.ShapeDtypeStruct((n, IN_W), bf16),
        compiler_params=_params(("parallel",)),
    )(*parts, cos, sin)


def _valid(kind, qpos, kpos, n_x):
    ok = (kpos >= 0) & (kpos < n_x) & (qpos >= 0) & (qpos < n_x)
    if kind == "na":
        rows = n_x // GRID_W
        qr, qc = lax.shift_right_arithmetic(qpos, 6), qpos & (GRID_W - 1)
        kr, kc = lax.shift_right_arithmetic(kpos, 6), kpos & (GRID_W - 1)
        kr0 = jnp.clip(qr - NA_ROWS // 2, 0, rows - NA_ROWS)
        ws = jnp.clip(qc - NA_COLS // 2, 0, GRID_W - NA_COLS)
        return ok & (kr >= kr0) & (kr < kr0 + NA_ROWS) & (kc >= ws) & (kc < ws + NA_COLS)
    return ok & (jnp.abs(kpos - qpos) <= SW_RADIUS)


def _stack_heads(x, g):
    if g == 1:
        return x
    return jnp.concatenate([x[:, a * HEAD_DIM:(a + 1) * HEAD_DIM] for a in range(g)], axis=0)


def _unstack_heads(x, g):
    if g == 1:
        return x
    r = x.shape[0] // g
    return jnp.concatenate([x[a * r:(a + 1) * r] for a in range(g)], axis=1)


def _tile_rows(x, g):
    return x if g == 1 else jnp.concatenate([x] * g, axis=0)


class _AttnCfg:
    def __init__(self, kind, n_x, n_tot):
        self.kind, self.n_x, self.n_tot = kind, n_x, n_tot
        self.n_xb, self.n_blk = n_x // BLK, n_tot // BLK
        if kind == "na":
            self.g, self.nkv, self.q0, self.k0, self.v0 = 1, NA_HEADS, 0, NA_HEADS, 2 * NA_HEADS
        else:
            self.g, self.nkv = NB_GROUP, NB_KV_HEADS
            self.q0, self.k0, self.v0 = QB0 // (NB_GROUP * HEAD_DIM), KB0 // HEAD_DIM, VB0 // HEAD_DIM
        self.r = BLK * self.g
        self.qw = HEAD_DIM * self.g
        self.scale = HEAD_DIM ** -0.5


def _attn_fwd(qkv, cfg, extra, name):
    g, r, qw, n_xb, n_x = cfg.g, cfg.r, cfg.qw, cfg.n_xb, cfg.n_x
    last = n_xb - 1

    def body(q_ref, kp, ko, kn, vp, vo, vn, kc_ref, vc_ref, ex_ref, o_ref, lse_ref):
        i = pl.program_id(1)
        q = _stack_heads(q_ref[...], g)
        kw = jnp.concatenate([kp[...], ko[...], kn[...]], axis=0)
        vw = jnp.concatenate([vp[...], vo[...], vn[...]], axis=0)
        s = lax.dot_general(q, kw, NT, preferred_element_type=f32) * cfg.scale
        if cfg.kind == "na":
            s = s + ex_ref[...]
        qpos = i * BLK + lax.broadcasted_iota(jnp.int32, (BLK, 3 * BLK), 0)
        kpos = (i - 1) * BLK + lax.broadcasted_iota(jnp.int32, (BLK, 3 * BLK), 1)
        s = jnp.where(_tile_rows(_valid(cfg.kind, qpos, kpos, n_x), g), s, NEG_INF)
        sc = lax.dot_general(q, kc_ref[...], NT, preferred_element_type=f32) * cfg.scale
        m = jnp.maximum(jnp.max(s, axis=-1, keepdims=True), jnp.max(sc, axis=-1, keepdims=True))
        if cfg.kind == "swa":
            m = jnp.maximum(m, ex_ref[...])
        p, pc = jnp.exp(s - m), jnp.exp(sc - m)
        l = jnp.sum(p, axis=-1, keepdims=True) + jnp.sum(pc, axis=-1, keepdims=True)
        if cfg.kind == "swa":
            l = l + jnp.exp(ex_ref[...] - m)
        o = jnp.dot(p.astype(bf16), vw, preferred_element_type=f32) + jnp.dot(pc.astype(bf16), vc_ref[...],
                                                                             preferred_element_type=f32)
        o_ref[...] = _unstack_heads(o / l, g).astype(bf16)
        lse_ref[...] = m + jnp.log(l)

    kv = lambda col0, f: pl.BlockSpec((BLK, HEAD_DIM), lambda h, i: (f(i), col0 + h))
    prev = lambda i: jnp.clip(i - 1, 0, last)
    own = lambda i: jnp.minimum(i, last)
    nxt = lambda i: jnp.minimum(i + 1, last)
    ctx = lambda i: n_xb
    if cfg.kind == "na":
        ex_spec = pl.BlockSpec((None, BLK, 3 * BLK), lambda h, i: (h, 0, 0))
    else:
        ex_spec = pl.BlockSpec((None, r, 1), lambda h, i: (h, 0, 0))
    return pl.pallas_call(
        body, name=name, grid=(cfg.nkv, cfg.n_blk),
        in_specs=[pl.BlockSpec((BLK, qw), lambda h, i: (i, cfg.q0 + h)),
                  kv(cfg.k0, prev), kv(cfg.k0, own), kv(cfg.k0, nxt),
                  kv(cfg.v0, prev), kv(cfg.v0, own), kv(cfg.v0, nxt),
                  kv(cfg.k0, ctx), kv(cfg.v0, ctx), ex_spec],
        out_specs=[pl.BlockSpec((BLK, qw), lambda h, i: (i, h)),
                   pl.BlockSpec((None, None, r, 1), lambda h, i: (h, i, 0, 0))],
        out_shape=[jax.ShapeDtypeStruct((cfg.n_tot, cfg.nkv * qw), bf16),
                   jax.ShapeDtypeStruct((cfg.nkv, cfg.n_blk, r, 1), f32)],
        compiler_params=_params(("parallel", "parallel")),
    )(qkv, qkv, qkv, qkv, qkv, qkv, qkv, qkv, qkv, extra)


def _attn_bwd_q(qkv, o, do, lse, cfg, extra, name):
    g, r, qw, n_xb, n_x = cfg.g, cfg.r, cfg.qw, cfg.n_xb, cfg.n_x
    last = n_xb - 1
    do_col0 = 0 if cfg.kind == "na" else (NA_HEADS * HEAD_DIM) // qw

    def body(q_ref, kp, ko, kn, vp, vo, vn, kc_ref, vc_ref, ex_ref, o_ref, do_ref, lse_ref,
             dq_ref, delta_ref, dex_ref):
        i = pl.program_id(1)
        q = _stack_heads(q_ref[...], g)
        dout = _stack_heads(do_ref[...], g)
        out = _stack_heads(o_ref[...], g)
        delta = jnp.sum(dout.astype(f32) * out.astype(f32), axis=-1, keepdims=True)
        delta_ref[...] = delta
        kw = jnp.concatenate([kp[...], ko[...], kn[...]], axis=0)
        vw = jnp.concatenate([vp[...], vo[...], vn[...]], axis=0)
        s = lax.dot_general(q, kw, NT, preferred_element_type=f32) * cfg.scale
        if cfg.kind == "na":
            s = s + ex_ref[...]
        qpos = i * BLK + lax.broadcasted_iota(jnp.int32, (BLK, 3 * BLK), 0)
        kpos = (i - 1) * BLK + lax.broadcasted_iota(jnp.int32, (BLK, 3 * BLK), 1)
        s = jnp.where(_tile_rows(_valid(cfg.kind, qpos, kpos, n_x), g), s, NEG_INF)
        sc = lax.dot_general(q, kc_ref[...], NT, preferred_element_type=f32) * cfg.scale
        lse_v = lse_ref[...]
        p, pc = jnp.exp(s - lse_v), jnp.exp(sc - lse_v)
        dp = lax.dot_general(dout, vw, NT, preferred_element_type=f32)
        dpc = lax.dot_general(dout, vc_ref[...], NT, preferred_element_type=f32)
        ds, dsc = p * (dp - delta), pc * (dpc - delta)
        dq = jnp.dot(ds.astype(bf16), kw, preferred_element_type=f32) + jnp.dot(dsc.astype(bf16), kc_ref[...],
                                                                             preferred_element_type=f32)
        dq_ref[...] = _unstack_heads(dq * cfg.scale, g).astype(bf16)
        dex = ds if cfg.kind == "na" else -jnp.exp(ex_ref[...] - lse_v) * delta

        @pl.when(i == 0)
        def _():
            dex_ref[...] = dex

        @pl.when(i > 0)
        def _():
            dex_ref[...] += dex

    kv = lambda col0, f: pl.BlockSpec((BLK, HEAD_DIM), lambda h, i: (f(i), col0 + h))
    prev = lambda i: jnp.clip(i - 1, 0, last)
    own = lambda i: jnp.minimum(i, last)
    nxt = lambda i: jnp.minimum(i + 1, last)
    ctx = lambda i: n_xb
    if cfg.kind == "na":
        ex_spec = pl.BlockSpec((None, BLK, 3 * BLK), lambda h, i: (h, 0, 0))
    else:
        ex_spec = pl.BlockSpec((None, r, 1), lambda h, i: (h, 0, 0))
    stat = pl.BlockSpec((None, None, r, 1), lambda h, i: (h, i, 0, 0))
    return pl.pallas_call(
        body, name=name, grid=(cfg.nkv, cfg.n_blk),
        in_specs=[pl.BlockSpec((BLK, qw), lambda h, i: (i, cfg.q0 + h)),
                  kv(cfg.k0, prev), kv(cfg.k0, own), kv(cfg.k0, nxt),
                  kv(cfg.v0, prev), kv(cfg.v0, own), kv(cfg.v0, nxt),
                  kv(cfg.k0, ctx), kv(cfg.v0, ctx), ex_spec,
                  pl.BlockSpec((BLK, qw), lambda h, i: (i, h)),
                  pl.BlockSpec((BLK, qw), lambda h, i: (i, do_col0 + h)), stat],
        out_specs=[pl.BlockSpec((BLK, qw), lambda h, i: (i, h)), stat, ex_spec],
        out_shape=[jax.ShapeDtypeStruct((cfg.n_tot, cfg.nkv * qw), bf16),
                   jax.ShapeDtypeStruct((cfg.nkv, cfg.n_blk, r, 1), f32),
                   jax.ShapeDtypeStruct(extra.shape, f32)],
        compiler_params=_params(("parallel", "arbitrary")),
    )(qkv, qkv, qkv, qkv, qkv, qkv, qkv, qkv, qkv, extra, o, do, lse)


def _attn_bwd_kv(qkv, do, lse, delta, cfg, bias_t, name):
    g, r, qw, n_xb, n_x, n_blk = cfg.g, cfg.r, cfg.qw, cfg.n_xb, cfg.n_x, cfg.n_blk
    last = n_xb - 1
    do_col0 = 0 if cfg.kind == "na" else (NA_HEADS * HEAD_DIM) // qw
    has_bias = cfg.kind == "na"

    def body(*refs):
        qs, dos, lses, dels = refs[0:3], refs[3:6], refs[6:9], refs[9:12]
        qj_ref, doj_ref, lsej_ref, delj_ref, k_ref, v_ref, kc_ref, vc_ref = refs[12:20]
        rest = refs[20:]
        if has_bias:
            b_ref, rest = rest[0], rest[1:]
        dk_ref, dv_ref, dkc_acc, dvc_acc = rest
        j = pl.program_id(1)

        @pl.when(j == 0)
        def _():
            dkc_acc[...] = jnp.zeros_like(dkc_acc)
            dvc_acc[...] = jnp.zeros_like(dvc_acc)

        qj, doj = _stack_heads(qj_ref[...], g), _stack_heads(doj_ref[...], g)
        sc = lax.dot_general(qj, kc_ref[...], NT, preferred_element_type=f32) * cfg.scale
        pc = jnp.exp(sc - lsej_ref[...])
        dvc_acc[...] += lax.dot_general(pc.astype(bf16), doj, TN, preferred_element_type=f32)
        dpc = lax.dot_general(doj, vc_ref[...], NT, preferred_element_type=f32)
        dsc = pc * (dpc - delj_ref[...])
        dkc_acc[...] += lax.dot_general(dsc.astype(bf16), qj, TN, preferred_element_type=f32) * cfg.scale

        @pl.when(j < n_xb)
        def _():
            qw_all = jnp.concatenate([_stack_heads(x[...], g) for x in qs], axis=0)
            do_all = jnp.concatenate([_stack_heads(x[...], g) for x in dos], axis=0)
            lse_all = jnp.concatenate([x[...] for x in lses], axis=0)
            del_all = jnp.concatenate([x[...] for x in dels], axis=0)
            s = lax.dot_general(qw_all, k_ref[...], NT, preferred_element_type=f32) * cfg.scale
            if has_bias:
                s = s + b_ref[...]
            kpos = j * BLK + lax.broadcasted_iota(jnp.int32, (BLK, BLK), 1)
            masks = []
            for w in range(3):
                qpos = (j - 1 + w) * BLK + lax.broadcasted_iota(jnp.int32, (BLK, BLK), 0)
                masks.append(_tile_rows(_valid(cfg.kind, qpos, kpos, n_x), g))
            s = jnp.where(jnp.concatenate(masks, axis=0), s, NEG_INF)
            p = jnp.exp(s - lse_all)
            dv_ref[...] = lax.dot_general(p.astype(bf16), do_all, TN, preferred_element_type=f32).astype(bf16)
            dp = lax.dot_general(do_all, v_ref[...], NT, preferred_element_type=f32)
            ds = p * (dp - del_all)
            dk_ref[...] = (lax.dot_general(ds.astype(bf16), qw_all, TN, preferred_element_type=f32)
                           * cfg.scale).astype(bf16)

        @pl.when(j == n_xb)
        def _():
            dk_ref[...] = dkc_acc[...].astype(bf16)
            dv_ref[...] = dvc_acc[...].astype(bf16)

    prev = lambda j: jnp.clip(j - 1, 0, last)
    own = lambda j: jnp.minimum(j, last)
    nxt = lambda j: jnp.minimum(j + 1, last)
    same = lambda j: j
    qspec = lambda f: pl.BlockSpec((BLK, qw), lambda h, j: (f(j), cfg.q0 + h))
    dospec = lambda f: pl.BlockSpec((BLK, qw), lambda h, j: (f(j), do_col0 + h))
    stat = lambda f: pl.BlockSpec((None, None, r, 1), lambda h, j: (h, f(j), 0, 0))
    kv = lambda col0, f: pl.BlockSpec((BLK, HEAD_DIM), lambda h, j: (f(j), col0 + h))
    ctx = lambda j: n_xb
    in_specs = ([qspec(f) for f in (prev, own, nxt)] + [dospec(f) for f in (prev, own, nxt)]
                + [stat(f) for f in (prev, own, nxt)] * 2
                + [qspec(same), dospec(same), stat(same), stat(same),
                   kv(cfg.k0, same), kv(cfg.v0, same), kv(cfg.k0, ctx), kv(cfg.v0, ctx)])
    args = [qkv] * 3 + [do] * 3 + [lse] * 3 + [delta] * 3 + [qkv, do, lse, delta, qkv, qkv, qkv, qkv]
    if has_bias:
        in_specs.append(pl.BlockSpec((None, 3 * BLK, BLK), lambda h, j: (h, 0, 0)))
        args.append(bias_t)
    out = pl.BlockSpec((BLK, HEAD_DIM), lambda h, j: (j, h))
    return pl.pallas_call(
        body, name=name, grid=(cfg.nkv, n_blk), in_specs=in_specs, out_specs=[out, out],
        out_shape=[jax.ShapeDtypeStruct((cfg.n_tot, cfg.nkv * HEAD_DIM), bf16)] * 2,
        scratch_shapes=[pltpu.VMEM((BLK, HEAD_DIM), f32)] * 2,
        compiler_params=_params(("parallel", "arbitrary")),
    )(*args)


def _rpb_index(transposed):
    if transposed:
        return np.array([[a - b + 11 for a in range(4)] for b in range(12)])
    return np.array([[b - a + 3 for b in range(12)] for a in range(4)])


def _toeplitz_basis():
    qc, kc = np.meshgrid(np.arange(GRID_W), np.arange(GRID_W), indexing="ij")
    e = (kc - qc + NA_COLS - 1)[None] == np.arange(2 * NA_COLS - 1)[:, None, None]
    return e.reshape(2 * NA_COLS - 1, GRID_W * GRID_W).astype(np.float32)


def _whole(f, ins, outs, name):
    def body(*refs):
        res = f(*[r[...] for r in refs[:len(ins)]])
        for o_ref, val in zip(refs[len(ins):], res):
            o_ref[...] = val.astype(o_ref.dtype)

    return pl.pallas_call(body, name=name,
                          out_shape=[jax.ShapeDtypeStruct(s, d) for s, d in outs])(*ins)


def _whole_bwd(f, ins, cts, name):
    n = len(ins)

    def body(*refs):
        _, vjp = jax.vjp(f, *[r[...] for r in refs[:n]])
        grads = vjp(tuple(r[...] for r in refs[n:n + len(cts)]))
        for o_ref, g in zip(refs[n + len(cts):], grads):
            o_ref[...] = g

    return pl.pallas_call(body, name=name,
                          out_shape=[jax.ShapeDtypeStruct(a.shape, f32) for a in ins])(*ins, *cts)


def _f_discretise(ar, ai, ldt, br, bi):
    dt = jnp.exp(ldt)
    mag = jnp.exp(ar * dt)
    lam_r, lam_i = mag * jnp.cos(ai * dt), mag * jnp.sin(ai * dt)
    den = ar * ar + ai * ai
    nr = lam_r - 1.0
    coef_r = (nr * ar + lam_i * ai) / den
    coef_i = (lam_i * ar - nr * ai) / den
    return (lam_r, lam_i, coef_r[None] * br - coef_i[None] * bi, coef_r[None] * bi + coef_i[None] * br)


SEG_LEN = BLK // SCAN_SEG
N_STATE = GROUPS_PER_CHUNK * SSM_STATE


def _scan_perm(reverse):
    r = np.arange(BLK)
    t = (r % SCAN_SEG) * SEG_LEN + r // SCAN_SEG
    if reverse:
        t = BLK - 1 - t
    pm = np.zeros((BLK, BLK), np.float32)
    pm[r, t] = 1.0
    return jnp.asarray(pm, bf16), jnp.asarray(pm.T, bf16)


def _block_order(n_xb, n_blk, reverse):
    n_cb = n_blk - n_xb
    if reverse:
        return lambda cc: jnp.where(cc < n_cb, n_blk - 1 - cc, n_xb - 1 - (cc - n_cb))
    return lambda cc: jnp.where(cc < n_cb, n_xb + cc, cc - n_cb)


def _unpermute(pmt, y):
    hi = y.astype(bf16)
    lo = (y - hi.astype(f32)).astype(bf16)
    return jnp.dot(pmt, hi, preferred_element_type=f32) + jnp.dot(pmt, lo, preferred_element_type=f32)


def _lam_pow(lr, li):
    for _ in range(int(math.log2(SEG_LEN))):
        lr, li = lr * lr - li * li, 2.0 * lr * li
    return lr, li


def _s5_fwd(u, lam, bblk, cblk, n_x, reverse, name):
    n_tot, d = u.shape
    nq, n_blk, n_xb = d // 128, n_tot // BLK, n_x // BLK
    order = _block_order(n_xb, n_blk, reverse)
    pm, pmt = _scan_perm(reverse)
    ns = N_STATE

    def body(u_ref, pm_ref, pmt_ref, lam_ref, b_ref, c_ref, y_ref, cp_ref, bu_ref, st_ref, carry_ref):
        cc = pl.program_id(1)

        @pl.when(cc == 0)
        def _():
            carry_ref[...] = jnp.zeros_like(carry_ref)

        up = jnp.dot(pm_ref[...], u_ref[...].astype(bf16), preferred_element_type=f32).astype(bf16)
        bu_ref[0] = jnp.dot(up, b_ref[0], preferred_element_type=f32)
        bu_ref[1] = jnp.dot(up, b_ref[1], preferred_element_type=f32)
        lr, li = lam_ref[0:1, :], lam_ref[1:2, :]
        lrb, lib = jnp.broadcast_to(lr, (SCAN_SEG, ns)), jnp.broadcast_to(li, (SCAN_SEG, ns))

        def step(j, s, store):
            sr, si = s
            off = pl.multiple_of(j * SCAN_SEG, SCAN_SEG)
            nsr = lrb * sr - lib * si + bu_ref[0, pl.ds(off, SCAN_SEG), :]
            nsi = lrb * si + lib * sr + bu_ref[1, pl.ds(off, SCAN_SEG), :]
            if store:
                st_ref[0, pl.ds(off, SCAN_SEG), :] = nsr
                st_ref[1, pl.ds(off, SCAN_SEG), :] = nsi
            return nsr, nsi

        zero = jnp.zeros((SCAN_SEG, ns), f32)
        er, ei = lax.fori_loop(0, SEG_LEN, lambda j, s: step(j, s, False), (zero, zero))
        pr, pi = _lam_pow(lr, li)
        cr, ci = carry_ref[0, 0:1, :], carry_ref[1, 0:1, :]
        rows_r, rows_i = [], []
        for k in range(SCAN_SEG):
            rows_r.append(cr)
            rows_i.append(ci)
            cr, ci = pr * cr - pi * ci + er[k:k + 1], pr * ci + pi * cr + ei[k:k + 1]
        carry_ref[0] = jnp.broadcast_to(cr, (SCAN_SEG, ns))
        carry_ref[1] = jnp.broadcast_to(ci, (SCAN_SEG, ns))
        cpr, cpi = jnp.concatenate(rows_r, axis=0), jnp.concatenate(rows_i, axis=0)
        cp_ref[0] = cpr
        cp_ref[1] = cpi
        lax.fori_loop(0, SEG_LEN, lambda j, s: step(j, s, True), (cpr, cpi))
        yp = (jnp.dot(st_ref[0].astype(bf16), c_ref[0], preferred_element_type=f32)
              - jnp.dot(st_ref[1].astype(bf16), c_ref[1], preferred_element_type=f32))
        y_ref[...] = _unpermute(pmt_ref[...], yp)

    full = lambda shape: pl.BlockSpec(shape, lambda q, cc: (0,) * len(shape))
    return pl.pallas_call(
        body, name=name, grid=(nq, n_blk),
        in_specs=[pl.BlockSpec((BLK, 128), lambda q, cc: (order(cc), q)), full((BLK, BLK)), full((BLK, BLK)),
                  pl.BlockSpec((None, 2, ns), lambda q, cc: (q, 0, 0)),
                  pl.BlockSpec((None, 2, 128, ns), lambda q, cc: (q, 0, 0, 0)),
                  pl.BlockSpec((None, 2, ns, 128), lambda q, cc: (q, 0, 0, 0))],
        out_specs=[pl.BlockSpec((BLK, 128), lambda q, cc: (order(cc), q)),
                   pl.BlockSpec((None, None, 2, SCAN_SEG, ns), lambda q, cc: (q, cc, 0, 0, 0))],
        out_shape=[jax.ShapeDtypeStruct((n_tot, d), f32),
                   jax.ShapeDtypeStruct((nq, n_blk, 2, SCAN_SEG, ns), f32)],
        scratch_shapes=[pltpu.VMEM((2, BLK, ns), f32), pltpu.VMEM((2, BLK, ns), f32),
                        pltpu.VMEM((2, SCAN_SEG, ns), f32)],
        compiler_params=_params(("parallel", "arbitrary")),
    )(u, pm, pmt, lam, bblk, cblk)


def _s5_bwd(u, dy, cprev, lam, bblk, cblk, bblk_t, cblk_t, n_x, reverse, name, add=None, skip=None):
    n_tot, d = u.shape
    nq, n_blk, n_xb = d // 128, n_tot // BLK, n_x // BLK
    order = _block_order(n_xb, n_blk, reverse)
    pm, pmt = _scan_perm(reverse)
    ns = N_STATE
    blk_of = lambda step: order(n_blk - 1 - step)
    has_add, has_skip = add is not None, skip is not None

    def body(*refs):
        u_ref, dy_ref, cp_ref, pm_ref, pmt_ref, lam_ref, b_ref, c_ref, bt_ref, ct_ref = refs[:10]
        rest = refs[10:]
        if has_add:
            add_ref, rest = rest[0], rest[1:]
        if has_skip:
            skip_ref, rest = rest[0], rest[1:]
        du_ref, dlam_ref, db_ref, dc_ref, bu_ref, st_ref, ds_ref, acarry_ref = rest
        step_id = pl.program_id(1)
        is_x = blk_of(step_id) < n_xb

        @pl.when(step_id == 0)
        def _():
            acarry_ref[...] = jnp.zeros_like(acarry_ref)
            dlam_ref[...] = jnp.zeros_like(dlam_ref)
            db_ref[...] = jnp.zeros_like(db_ref)
            dc_ref[...] = jnp.zeros_like(dc_ref)

        up = jnp.dot(pm_ref[...], u_ref[...].astype(bf16), preferred_element_type=f32).astype(bf16)
        bu_ref[0] = jnp.dot(up, b_ref[0], preferred_element_type=f32)
        bu_ref[1] = jnp.dot(up, b_ref[1], preferred_element_type=f32)
        lr, li = lam_ref[0:1, :], lam_ref[1:2, :]
        lrb, lib = jnp.broadcast_to(lr, (SCAN_SEG, ns)), jnp.broadcast_to(li, (SCAN_SEG, ns))
        cpr, cpi = cp_ref[0], cp_ref[1]

        def fstep(j, s):
            sr, si = s
            off = pl.multiple_of(j * SCAN_SEG, SCAN_SEG)
            nsr = lrb * sr - lib * si + bu_ref[0, pl.ds(off, SCAN_SEG), :]
            nsi = lrb * si + lib * sr + bu_ref[1, pl.ds(off, SCAN_SEG), :]
            st_ref[0, pl.ds(off, SCAN_SEG), :] = nsr
            st_ref[1, pl.ds(off, SCAN_SEG), :] = nsi
            return nsr, nsi

        lax.fori_loop(0, SEG_LEN, fstep, (cpr, cpi))

        dyv = jnp.where(is_x, dy_ref[...], 0.0)
        dyp = jnp.dot(pm_ref[...], dyv.astype(bf16), preferred_element_type=f32).astype(bf16)
        ds_ref[0] = jnp.dot(dyp, ct_ref[0], preferred_element_type=f32)
        ds_ref[1] = -jnp.dot(dyp, ct_ref[1], preferred_element_type=f32)

        def adj(j, a):
            ar, ai = a
            off = pl.multiple_of(j * SCAN_SEG, SCAN_SEG)
            nar = ds_ref[0, pl.ds(off, SCAN_SEG), :] + lrb * ar + lib * ai
            nai = ds_ref[1, pl.ds(off, SCAN_SEG), :] - lib * ar + lrb * ai
            return nar, nai

        zero = jnp.zeros((SCAN_SEG, ns), f32)
        er, ei = lax.fori_loop(0, SEG_LEN, lambda jj, a: adj(SEG_LEN - 1 - jj, a), (zero, zero))
        pr, pi = _lam_pow(lr, li)
        nr_, ni_ = acarry_ref[0, 0:1, :], acarry_ref[1, 0:1, :]
        rows_r, rows_i = [None] * SCAN_SEG, [None] * SCAN_SEG
        for k in range(SCAN_SEG - 1, -1, -1):
            rows_r[k], rows_i[k] = nr_, ni_
            nr_, ni_ = er[k:k + 1] + pr * nr_ + pi * ni_, ei[k:k + 1] + pr * ni_ - pi * nr_
        acarry_ref[0] = jnp.broadcast_to(nr_, (SCAN_SEG, ns))
        acarry_ref[1] = jnp.broadcast_to(ni_, (SCAN_SEG, ns))
        an_r, an_i = jnp.concatenate(rows_r, axis=0), jnp.concatenate(rows_i, axis=0)

        def adj2(jj, carry):
            ar, ai, glr, gli = carry
            j = SEG_LEN - 1 - jj
            nar, nai = adj(j, (ar, ai))
            off = pl.multiple_of(j * SCAN_SEG, SCAN_SEG)
            ds_ref[0, pl.ds(off, SCAN_SEG), :] = nar
            ds_ref[1, pl.ds(off, SCAN_SEG), :] = nai
            poff = pl.multiple_of((j - 1) * SCAN_SEG, SCAN_SEG)
            spr, spi = st_ref[0, pl.ds(poff, SCAN_SEG), :], st_ref[1, pl.ds(poff, SCAN_SEG), :]
            return nar, nai, glr + nar * spr + nai * spi, gli - nar * spi + nai * spr

        ar, ai, glr, gli = lax.fori_loop(0, SEG_LEN - 1, adj2, (an_r, an_i, zero, zero))
        nar, nai = adj(0, (ar, ai))
        ds_ref[0, 0:SCAN_SEG, :] = nar
        ds_ref[1, 0:SCAN_SEG, :] = nai
        glr = glr + nar * cpr + nai * cpi
        gli = gli - nar * cpi + nai * cpr
        dlam_ref[0:1, :] += jnp.sum(glr, axis=0, keepdims=True)
        dlam_ref[1:2, :] += jnp.sum(gli, axis=0, keepdims=True)

        a_r, a_i = ds_ref[0].astype(bf16), ds_ref[1].astype(bf16)
        dup = jnp.dot(a_r, bt_ref[0], preferred_element_type=f32) + jnp.dot(a_i, bt_ref[1],
                                                                         preferred_element_type=f32)
        du = _unpermute(pmt_ref[...], dup)
        if has_add:
            du = du + add_ref[...]
        if has_skip:
            du = du + skip_ref[...] * dyv
        du_ref[...] = du
        db_ref[0] += lax.dot_general(up, a_r, TN, preferred_element_type=f32)
        db_ref[1] += lax.dot_general(up, a_i, TN, preferred_element_type=f32)
        dc_ref[0] += lax.dot_general(st_ref[0].astype(bf16), dyp, TN, preferred_element_type=f32)
        dc_ref[1] -= lax.dot_general(st_ref[1].astype(bf16), dyp, TN, preferred_element_type=f32)

    full = lambda shape: pl.BlockSpec(shape, lambda q, s: (0,) * len(shape))
    tok = pl.BlockSpec((BLK, 128), lambda q, s: (blk_of(s), q))
    in_specs = [tok, pl.BlockSpec((BLK, 128), lambda q, s: (jnp.minimum(blk_of(s), n_xb - 1), q)),
                pl.BlockSpec((None, None, 2, SCAN_SEG, ns), lambda q, s: (q, n_blk - 1 - s, 0, 0, 0)),
                full((BLK, BLK)), full((BLK, BLK)),
                pl.BlockSpec((None, 2, ns), lambda q, s: (q, 0, 0)),
                pl.BlockSpec((None, 2, 128, ns), lambda q, s: (q, 0, 0, 0)),
                pl.BlockSpec((None, 2, ns, 128), lambda q, s: (q, 0, 0, 0)),
                pl.BlockSpec((None, 2, ns, 128), lambda q, s: (q, 0, 0, 0)),
                pl.BlockSpec((None, 2, 128, ns), lambda q, s: (q, 0, 0, 0))]
    args = [u, dy, cprev, pm, pmt, lam, bblk, cblk, bblk_t, cblk_t]
    if has_add:
        in_specs.append(tok)
        args.append(add)
    if has_skip:
        in_specs.append(pl.BlockSpec((1, 128), lambda q, s: (0, q)))
        args.append(skip)
    return pl.pallas_call(
        body, name=name, grid=(nq, n_blk), in_specs=in_specs,
        out_specs=[tok, pl.BlockSpec((None, 2, ns), lambda q, s: (q, 0, 0)),
                   pl.BlockSpec((None, 2, 128, ns), lambda q, s: (q, 0, 0, 0)),
                   pl.BlockSpec((None, 2, ns, 128), lambda q, s: (q, 0, 0, 0))],
        out_shape=[jax.ShapeDtypeStruct((n_tot, d), f32), jax.ShapeDtypeStruct((nq, 2, ns), f32),
                   jax.ShapeDtypeStruct((nq, 2, 128, ns), f32), jax.ShapeDtypeStruct((nq, 2, ns, 128), f32)],
        scratch_shapes=[pltpu.VMEM((2, BLK, ns), f32), pltpu.VMEM((2, BLK, ns), f32),
                        pltpu.VMEM((2, BLK, ns), f32), pltpu.VMEM((2, SCAN_SEG, ns), f32)],
        compiler_params=_params(("parallel", "arbitrary")),
    )(*args)


MESH = pl.DeviceIdType.MESH
GROUP_SIZE = {"c": 2, "xy": 4, "xyc": 8}
MIN_CHUNK_BYTES = 512 * 1024


def _group(axes):
    x, y, c = lax.axis_index("x"), lax.axis_index("y"), lax.axis_index("c")
    if axes == "c":
        return c, lambda o: (x, y, c ^ 1)
    if axes == "xy":
        return 2 * x + y, lambda o: (x ^ (o >> 1), y ^ (o & 1), c)
    return 4 * x + 2 * y + c, lambda o: (x ^ (o >> 2), y ^ ((o >> 1) & 1), c ^ (o & 1))


def _exchange(src, axes, gather, name):
    n = GROUP_SIZE[axes]
    out_shape = (n,) + src.shape if gather else src.shape
    slab = out_shape[1:]
    slab_bytes = int(np.prod(slab)) * src.dtype.itemsize
    nch = next((k for k in (16, 8, 4, 2) if slab[0] % (16 * k) == 0 and slab_bytes // k >= MIN_CHUNK_BYTES), 1)
    step = slab[0] // nch

    def body(src_ref, out_ref, send_sems, recv_sems, local_sem):
        me, peer = _group(axes)
        local = pltpu.make_async_copy(src_ref if gather else src_ref.at[me], out_ref.at[me], local_sem)
        local.start()

        def copy(o, k, src_idx, dst_idx):
            rows = pl.ds(k * step, step)
            return pltpu.make_async_remote_copy(
                src_ref=(src_ref if gather else src_ref.at[src_idx]).at[rows], dst_ref=out_ref.at[dst_idx].at[rows],
                send_sem=send_sems.at[o - 1, k], recv_sem=recv_sems.at[o - 1, k],
                device_id=peer(o), device_id_type=MESH)

        sends = [copy(o, k, me ^ o, me) for k in range(nch) for o in range(1, n)]
        for cp in sends:
            cp.start()
        for k in range(nch):
            for o in range(1, n):
                copy(o, k, me, me ^ o).wait_recv()
        for cp in sends:
            cp.wait_send()
        local.wait()

    return pl.pallas_call(
        body, name=name, out_shape=jax.ShapeDtypeStruct(out_shape, src.dtype),
        in_specs=[pl.BlockSpec(memory_space=pl.ANY)], out_specs=pl.BlockSpec(memory_space=pl.ANY),
        scratch_shapes=[pltpu.SemaphoreType.DMA((n - 1, nch)), pltpu.SemaphoreType.DMA((n - 1, nch)),
                        pltpu.SemaphoreType.DMA(())],
        compiler_params=pltpu.CompilerParams(has_side_effects=True),
    )(src)


def _sum0(x, name):
    n, r, c = x.shape
    tr = _pick(r, (512, 256, 128, 64, 32, 16, 8))

    def body(x_ref, o_ref):
        acc = x_ref[0].astype(f32)
        for k in range(1, n):
            acc = acc + x_ref[k].astype(f32)
        o_ref[...] = acc

    return pl.pallas_call(
        body, name=name, grid=(r // tr,), in_specs=[pl.BlockSpec((n, tr, c), lambda i: (0, i, 0))],
        out_specs=pl.BlockSpec((tr, c), lambda i: (i, 0)), out_shape=jax.ShapeDtypeStruct((r, c), f32),
        compiler_params=_params(("parallel",)),
    )(x)


LANES = 1024


def _pack(parts, dtype):
    flat = jnp.concatenate([p.astype(dtype).reshape(-1) for p in parts])
    pad = (-flat.shape[0]) % (16 * LANES)
    if pad:
        flat = jnp.concatenate([flat, jnp.zeros((pad,), dtype)])
    return flat.reshape(-1, LANES)


def _unpack(flat, shapes):
    out, off = [], 0
    for s in shapes:
        size = int(np.prod(s))
        out.append(flat[off:off + size].reshape(s))
        off += size
    return out


def _vec(a):
    return a.reshape(-1, 1, a.shape[-1])


def _local_step(xa, tgt, mods, wts, sm, n_x):
    n_tot, d = xa.shape
    kw = dict(nrows=n_tot, n_x_rows=n_x)
    kx = dict(nrows=n_x, n_x_rows=n_x)
    mv = lambda l, k: mods[l, :, k][:, None, :]
    mx = lambda l, k: mods[l, 0:1, k][:, None, :]
    nmix, nffn = sm["norm_mix"], sm["norm_ffn"]
    cos, sin = _rope_tables(n_x, n_tot - n_x)
    cfg_a, cfg_b = _AttnCfg("na", n_x, n_tot), _AttnCfg("swa", n_x, n_tot)
    ff = wts["w1"][0].shape[1]
    tcf = _pick(ff, (512, 256, 128))

    h1 = _rowwise(_f_normmod, [xa], [_vec(nmix[0:1]), mv(0, 1), mv(0, 0)], [(d, bf16)], name="l0_norm_mix", **kw)[0]
    qkv32 = _mm(h1, wts["w_in"], "nn", f32, "l0_in_proj")
    qkv = _qkv_post(qkv32, cos, sin, "l0_qkv_post")
    rpb2 = jnp.zeros((128, 128), f32).at[:NA_HEADS * 15, :31].set(sm["rpb"].reshape(NA_HEADS * 15, 31))
    basis = jnp.zeros((128, GRID_W * GRID_W), f32).at[:31].set(_toeplitz_basis())
    tz = _small_dot(rpb2, basis, "rpb_expand")[:NA_HEADS * 15].reshape(NA_HEADS, 15, GRID_W, GRID_W)
    bias = jnp.stack([tz[:, 3 - a:15 - a] for a in range(4)], axis=1).transpose(0, 1, 3, 2, 4).reshape(
        NA_HEADS, BLK, 3 * BLK)
    bias_t = jnp.stack([tz[:, a:a + 12][:, ::-1] for a in range(4)], axis=2).transpose(0, 1, 3, 2, 4).reshape(
        NA_HEADS, 3 * BLK, BLK)
    sink_rows = jnp.repeat(sm["sink"].reshape(NB_KV_HEADS, NB_GROUP, 1), BLK, axis=1).reshape(
        NB_KV_HEADS, NB_GROUP * BLK, 1)
    oa, lse_a = _attn_fwd(qkv, cfg_a, bias, "l0_na_fwd")
    ob, lse_b = _attn_fwd(qkv, cfg_b, sink_rows, "l0_swa_fwd")
    o = jnp.concatenate([oa, ob], axis=1)
    y1 = _mm(o, wts["w_out"], "nn", f32, "l0_out_proj")
    xb = _rowwise(_f_gated_add, [xa, y1], [mv(0, 2)], [(d, f32)], name="l0_res_mix", **kw)[0]

    def ffn_fwd(xin, l, vec_of, kk, tag):
        h = _rowwise(_f_normmod, [xin], [_vec(nffn[l:l + 1]), vec_of(l, 4), vec_of(l, 3)], [(d, bf16)],
                     name=tag + "_norm_ffn", **kk)[0]
        a = _mm(h, wts["w1"][l], "nn", bf16, tag + "_ffn_w1")
        b = _mm(h, wts["w3"][l], "nn", bf16, tag + "_ffn_w3")
        u = _rowwise(_f_swiglu, [a, b], [], [(ff, bf16)], name=tag + "_swiglu", tc=tcf, **kk)[0]
        fo = _mm(u, wts["w2"][l], "nn", f32, tag + "_ffn_w2")
        xo = _rowwise(_f_gated_add, [xin, fo], [vec_of(l, 5)], [(d, f32)], name=tag + "_res_ffn", **kk)[0]
        return xo, (h, a, b, u, fo)

    def ffn_bwd(d_out, xin, saved, l, vec_of, kk, tag):
        h, a, b, u, fo = saved
        rows = kk["nrows"]
        dfo, dg2 = _rowwise_bwd(_f_gated_add, [xin, fo], [vec_of(l, 5)], [d_out], {1: bf16},
                                name=tag + "_res_ffn_bwd", **kk)
        du = _mm(dfo, wts["w2"][l], "nt", bf16, tag + "_ffn_w2_dx")
        dw2 = _mm(u[:rows], dfo, "tn", f32, tag + "_ffn_w2_dw")
        da, db = _rowwise_bwd(_f_swiglu, [a, b], [], [du], {0: bf16, 1: bf16}, name=tag + "_swiglu_bwd",
                              tc=tcf, **kk)
        dh = _mm(da, wts["w1"][l], "nt", f32, tag + "_ffn_w13_dx", a2=db, b2=wts["w3"][l])
        dw1 = _mm(h[:rows], da, "tn", f32, tag + "_ffn_w1_dw")
        dw3 = _mm(h[:rows], db, "tn", f32, tag + "_ffn_w3_dw")
        dxin, dnw, dsc, dsh = _rowwise_bwd(
            _f_normmod, [xin], [_vec(nffn[l:l + 1]), vec_of(l, 4), vec_of(l, 3)], [dh], {0: f32},
            name=tag + "_norm_ffn_bwd", residual=d_out, **kk)
        return dxin, dw1, dw3, dw2, dnw, dsc, dsh, dg2

    xc, ffn0 = ffn_fwd(xb, 0, mv, kw, "l0")

    hs = _rowwise(_f_normmod, [xc], [_vec(nmix[1:2]), mv(1, 1), mv(1, 0)], [(d, f32)], name="l1_norm_mix", **kw)[0]
    g2n = sm["a_re"].shape[1]
    nq = d // 128
    ar2, ai2 = sm["a_re"].reshape(2 * g2n, SSM_STATE), sm["a_im"].reshape(2 * g2n, SSM_STATE)
    ldt2 = sm["log_dt"].reshape(2 * g2n, 1)
    bt_re = sm["b_re"].transpose(3, 0, 1, 2).reshape(SSM_GROUP, 2 * g2n, SSM_STATE)
    bt_im = sm["b_im"].transpose(3, 0, 1, 2).reshape(SSM_GROUP, 2 * g2n, SSM_STATE)
    disc_in = [ar2, ai2, ldt2, bt_re, bt_im]
    lam_r, lam_i, bbar_r, bbar_i = _whole(
        _f_discretise, disc_in,
        [((2 * g2n, SSM_STATE), f32)] * 2 + [((SSM_GROUP, 2 * g2n, SSM_STATE), f32)] * 2, "s5_discretise")
    eye = jnp.eye(GROUPS_PER_CHUNK, dtype=f32)
    eye6 = eye[None, None, :, None, :, None]

    def blockdiag_b(bbar):
        t = bbar.reshape(SSM_GROUP, 2, nq, GROUPS_PER_CHUNK, SSM_STATE).transpose(1, 2, 3, 0, 4)
        return (t[:, :, :, :, None, :] * eye6).reshape(2, nq, 128, N_STATE)

    def blockdiag_c(cw):
        t = cw.reshape(2, nq, GROUPS_PER_CHUNK, SSM_GROUP, SSM_STATE).transpose(0, 1, 2, 4, 3)
        return (t[:, :, :, :, None, :] * eye6).reshape(2, nq, N_STATE, 128)

    lam = jnp.stack([lam_r.reshape(2, nq, N_STATE), lam_i.reshape(2, nq, N_STATE)], axis=2)
    bblk = jnp.stack([blockdiag_b(bbar_r), blockdiag_b(bbar_i)], axis=2)
    cblk = jnp.stack([blockdiag_c(sm["c_re"]), blockdiag_c(sm["c_im"])], axis=2)
    bblk16, cblk16 = bblk.astype(bf16), cblk.astype(bf16)
    bblk_t, cblk_t = bblk16.transpose(0, 1, 2, 4, 3), cblk16.transpose(0, 1, 2, 4, 3)
    ys, cps = [], []
    for dr in range(2):
        yd, cp = _s5_fwd(hs, lam[dr], bblk16[dr], cblk16[dr], n_x, dr == 1, "l1_s5_fwd%d" % dr)
        ys.append(yd)
        cps.append(cp)
    dvec = sm["ssm_d"].reshape(1, 1, d)
    gy = _rowwise(_f_glu_pre, [hs, ys[0], ys[1]], [dvec], [(d, bf16)], name="l1_glu_pre", **kx)[0]
    zv = _mm(gy, wts["w_glu_v"], "nn", f32, "l1_glu_val")
    zg = _mm(gy, wts["w_glu_g"], "nn", f32, "l1_glu_gate")
    bv, bg = sm["b_glu"][:d].reshape(1, 1, d), sm["b_glu"][d:].reshape(1, 1, d)
    xd = _rowwise(_f_glu_post, [xc, zv, zg], [mx(1, 2), bv, bg], [(d, f32)], name="l1_glu_post", **kx)[0]
    xe, ffn1 = ffn_fwd(xd, 1, mx, kx, "l1")

    d_xe, d_nfinal, loss_blk = _loss_head(xe, tgt, sm["norm_final"].reshape(1, d), nrows=n_x, name="loss_head")
    d_xd, dw1_1, dw3_1, dw2_1, dnffn1, dsc2_1, dsh2_1, dg2_1 = ffn_bwd(d_xe, xd, ffn1, 1, mx, kx, "l1")
    dzv, dzg, dg1_1, dbv, dbg = _rowwise_bwd(_f_glu_post, [xc, zv, zg], [mx(1, 2), bv, bg], [d_xd],
                                             {1: bf16, 2: bf16}, name="l1_glu_post_bwd", **kx)
    dgy = _mm(dzv, wts["w_glu_v"], "nt", f32, "l1_glu_dx", a2=dzg, b2=wts["w_glu_g"])
    dwglu_v = _mm(gy, dzv, "tn", f32, "l1_glu_val_dw")
    dwglu_g = _mm(gy, dzg, "tn", f32, "l1_glu_gate_dw")
    dy, d_dvec = _rowwise_bwd(_f_glu_pre, [hs, ys[0], ys[1]], [dvec], [dgy], {1: f32}, name="l1_glu_pre_bwd", **kx)
    du0, dlam0, db0, dc0 = _s5_bwd(hs, dy, cps[0], lam[0], bblk16[0], cblk16[0], bblk_t[0], cblk_t[0], n_x, False,
                                   "l1_s5_bwd0", skip=sm["ssm_d"].reshape(1, d))
    du1, dlam1, db1, dc1 = _s5_bwd(hs, dy, cps[1], lam[1], bblk16[1], cblk16[1], bblk_t[1], cblk_t[1], n_x, True,
                                   "l1_s5_bwd1", add=du0)
    d_xc, dnmix1, dsc1_1, dsh1_1 = _rowwise_bwd(
        _f_normmod, [xc], [_vec(nmix[1:2]), mv(1, 1), mv(1, 0)], [du1], {0: f32},
        name="l1_norm_mix_bwd", residual=d_xd, residual_rows=n_x, **kw)
    dlam = jnp.stack([dlam0, dlam1])
    dlam_r, dlam_i = dlam[:, :, 0].reshape(2 * g2n, SSM_STATE), dlam[:, :, 1].reshape(2 * g2n, SSM_STATE)
    dbb = jnp.stack([db0, db1]).reshape(2, nq, 2, GROUPS_PER_CHUNK, SSM_GROUP, GROUPS_PER_CHUNK, SSM_STATE)
    eye7 = eye[None, None, None, :, None, :, None]
    dbbar = (dbb * eye7).sum(axis=5).transpose(2, 4, 0, 1, 3, 5).reshape(2, SSM_GROUP, 2 * g2n, SSM_STATE)
    dcc = jnp.stack([dc0, dc1]).reshape(2, nq, 2, GROUPS_PER_CHUNK, SSM_STATE, GROUPS_PER_CHUNK, SSM_GROUP)
    dcw = (dcc * eye7).sum(axis=5).transpose(2, 0, 1, 3, 5, 4).reshape(2, 2, g2n, SSM_GROUP, SSM_STATE)
    d_ar, d_ai, d_ldt, d_btr, d_bti = _whole_bwd(_f_discretise, disc_in, [dlam_r, dlam_i, dbbar[0], dbbar[1]],
                                                 "s5_discretise_bwd")
    to_b = lambda t: t.reshape(SSM_GROUP, 2, g2n, SSM_STATE).transpose(1, 2, 3, 0)

    d_xb, dw1_0, dw3_0, dw2_0, dnffn0, dsc2_0, dsh2_0, dg2_0 = ffn_bwd(d_xc, xb, ffn0, 0, mv, kw, "l0")
    dy1, dg1_0 = _rowwise_bwd(_f_gated_add, [xa, y1], [mv(0, 2)], [d_xb], {1: bf16}, name="l0_res_mix_bwd", **kw)
    d_o = _mm(dy1, wts["w_out"], "nt", bf16, "l0_out_proj_dx")
    dw_out = _mm(o, dy1, "tn", f32, "l0_out_proj_dw")
    dqa, delta_a, dbias = _attn_bwd_q(qkv, oa, d_o, lse_a, cfg_a, bias, "l0_na_bwd_q")
    dka, dva = _attn_bwd_kv(qkv, d_o, lse_a, delta_a, cfg_a, bias_t, "l0_na_bwd_kv")
    dqb, delta_b, dsink_rows = _attn_bwd_q(qkv, ob, d_o, lse_b, cfg_b, sink_rows, "l0_swa_bwd_q")
    dkb, dvb = _attn_bwd_kv(qkv, d_o, lse_b, delta_b, cfg_b, None, "l0_swa_bwd_kv")
    d_qkv = _qkv_post_bwd([dqa, dka, dva, dqb, dkb, dvb], cos, sin, "l0_qkv_post_bwd")
    dh1 = _mm(d_qkv, wts["w_in"], "nt", f32, "l0_in_proj_dx")
    dw_in = _mm(h1, d_qkv, "tn", f32, "l0_in_proj_dw")
    d_xa, dnmix0, dsc1_0, dsh1_0 = _rowwise_bwd(
        _f_normmod, [xa], [_vec(nmix[0:1]), mv(0, 1), mv(0, 0)], [dh1], {0: f32},
        name="l0_norm_mix_bwd", residual=d_xb, **kw)
    dbias5 = dbias.reshape(NA_HEADS, 4, GRID_W, 12, GRID_W).transpose(0, 1, 3, 2, 4)
    dtz = sum(jnp.pad(dbias5[:, a], ((0, 0), (3 - a, a), (0, 0), (0, 0))) for a in range(4))
    dtz2 = jnp.zeros((128, GRID_W * GRID_W), f32).at[:NA_HEADS * 15].set(dtz.reshape(NA_HEADS * 15, -1))
    d_rpb = _small_dot(dtz2, basis.T, "rpb_expand_bwd")[:NA_HEADS * 15, :31].reshape(sm["rpb"].shape)
    d_sink = dsink_rows.reshape(NB_KV_HEADS * NB_GROUP, BLK).sum(axis=1)

    zero_c = jnp.zeros((1, 1, d), f32)
    both = lambda gx: jnp.concatenate([gx, zero_c], axis=0)
    dmods = jnp.stack([
        jnp.stack([dsh1_0, dsc1_0, dg1_0, dsh2_0, dsc2_0, dg2_0], axis=2),
        jnp.stack([dsh1_1, dsc1_1, both(dg1_1), both(dsh2_1), both(dsc2_1), both(dg2_1)], axis=2),
    ])[:, :, 0]
    big = dict(w1=jnp.stack([dw1_0, dw1_1]), w3=jnp.stack([dw3_0, dw3_1]), w2=jnp.stack([dw2_0, dw2_1]),
               w_in=dw_in, w_out=dw_out, w_glu=jnp.concatenate([dwglu_v, dwglu_g], axis=1))
    small = dict(
        norm_mix=jnp.concatenate([dnmix0[0], dnmix1[0]]), norm_ffn=jnp.concatenate([dnffn0[0], dnffn1[0]]),
        rpb=d_rpb, sink=d_sink, a_re=d_ar.reshape(sm["a_re"].shape), a_im=d_ai.reshape(sm["a_im"].shape),
        log_dt=d_ldt.reshape(sm["log_dt"].shape), b_re=to_b(d_btr), b_im=to_b(d_bti), c_re=dcw[0], c_im=dcw[1],
        ssm_d=d_dvec.reshape(d), b_glu=jnp.concatenate([dbv.reshape(d), dbg.reshape(d)]),
        norm_final=d_nfinal.reshape(d))
    return loss_blk[0, 0], d_xa, big, small, dmods


WEIGHTS = ("c_ctx", "ada_w", "ada_b", "norm_mix", "norm_ffn", "ffn_w1", "ffn_w3", "ffn_w2", "attn_w_in",
           "attn_w_out", "attn_rpb", "attn_sink", "ssm_a_re", "ssm_a_im", "ssm_log_dt", "ssm_b_re", "ssm_b_im",
           "ssm_c_re", "ssm_c_im", "ssm_d", "ssm_w_glu", "ssm_b_glu", "norm_final")
SHARDED_BIG = ("ffn_w1", "ffn_w3", "ffn_w2", "attn_w_in", "attn_w_out", "ssm_w_glu")
BIG = ("ada_w",) + SHARDED_BIG
SMALL = tuple(n for n in WEIGHTS if n not in BIG)


def _blocks_to_full(blk, kind):
    if kind == "cols2":
        t = blk.transpose(1, 2, 0, 3)
        return t.reshape(t.shape[0], t.shape[1], -1)
    if kind == "rows2":
        t = blk.transpose(1, 0, 2, 3)
        return t.reshape(t.shape[0], -1, t.shape[3])
    raise ValueError(kind)


def _full_to_blocks(full, kind):
    if kind == "cols2":
        l, k, n = full.shape
        return full.reshape(l, k, N_DEV, n // N_DEV).transpose(2, 0, 1, 3)
    l, n, k = full.shape
    return full.reshape(l, N_DEV, n // N_DEV, k).transpose(1, 0, 2, 3)


BIG_KIND = dict(ffn_w1="cols2", ffn_w3="cols2", ffn_w2="rows2", attn_w_in="cols2", attn_w_out="rows2",
                ssm_w_glu="cols2")


def kernel(x, c, ctx, c_ctx, ada_w, ada_b, norm_mix, norm_ffn, ffn_w1, ffn_w3, ffn_w2, attn_w_in, attn_w_out, attn_rpb, attn_sink, ssm_a_re, ssm_a_im, ssm_log_dt, ssm_b_re, ssm_b_im, ssm_c_re, ssm_c_im, ssm_d, ssm_w_glu, ssm_b_glu, norm_final, loss_target, m_c_ctx, m_ada_w, m_ada_b, m_norm_mix, m_norm_ffn, m_ffn_w1, m_ffn_w3, m_ffn_w2, m_attn_w_in, m_attn_w_out, m_attn_rpb, m_attn_sink, m_ssm_a_re, m_ssm_a_im, m_ssm_log_dt, m_ssm_b_re, m_ssm_b_im, m_ssm_c_re, m_ssm_c_im, m_ssm_d, m_ssm_w_glu, m_ssm_b_glu, m_norm_final, v_c_ctx, v_ada_w, v_ada_b, v_norm_mix, v_norm_ffn, v_ffn_w1, v_ffn_w3, v_ffn_w2, v_attn_w_in, v_attn_w_out, v_attn_rpb, v_attn_sink, v_ssm_a_re, v_ssm_a_im, v_ssm_log_dt, v_ssm_b_re, v_ssm_b_im, v_ssm_c_re, v_ssm_c_im, v_ssm_d, v_ssm_w_glu, v_ssm_b_glu, v_norm_final):
    p = dict(locals())
    w = {n: p[n] for n in WEIGHTS}
    me = 4 * lax.axis_index("x") + 2 * lax.axis_index("y") + lax.axis_index("c")
    n_x, d = x.shape[1], x.shape[2]
    cols = ada_w.shape[2]
    d8 = d // N_DEV

    first = jnp.concatenate([c[0], ssm_d[0], ssm_b_glu[0]])[None]
    g0 = _exchange(first, "xyc", True, "gather_vectors")[:, 0]
    c_all, d_full, bglu_full = g0[:, :d], g0[:, d:d + d8].reshape(d), g0[:, d + d8:].reshape(2 * d)
    cc = jnp.concatenate([c_all, c_ctx[None], jnp.zeros((16 - N_DEV - 1, d), f32)])
    sc_all = _whole(_f_silu, [cc], [((16, d), f32)], "silu_c")[0]
    my_cols = lambda a: lax.dynamic_slice_in_dim(a, me * cols, cols, axis=a.ndim - 1)
    mod_loc = jnp.stack([_mm(sc_all, ada_w[l], "nn", f32, "ada_mod%d" % l) for l in range(2)])
    mod_loc = mod_loc + my_cols(ada_b)[:, None, :]
    mg = _exchange(mod_loc.reshape(32, cols), "xyc", True, "gather_mod")
    mod_all = mg.reshape(N_DEV, 2, 16, cols).transpose(1, 2, 0, 3).reshape(2, 16, N_DEV * cols)
    mod_x = lax.dynamic_index_in_dim(mod_all, me, axis=1, keepdims=False)
    mods = jnp.stack([mod_x, mod_all[:, N_DEV]], axis=1).reshape(2, 2, 6, d)

    full = {}
    for n in SHARDED_BIG:
        blk = _exchange(w[n].astype(bf16).reshape(-1, w[n].shape[-1]), "xyc", True, "gather_" + n)
        full[n] = _blocks_to_full(blk.reshape((N_DEV,) + w[n].shape), BIG_KIND[n])
    wts = dict(w1=[full["ffn_w1"][0], full["ffn_w1"][1]], w3=[full["ffn_w3"][0], full["ffn_w3"][1]],
               w2=[full["ffn_w2"][0], full["ffn_w2"][1]], w_in=full["attn_w_in"][0], w_out=full["attn_w_out"][0],
               w_glu_v=full["ssm_w_glu"][0][:, :d], w_glu_g=full["ssm_w_glu"][0][:, d:])
    sm = dict(norm_mix=norm_mix, norm_ffn=norm_ffn, rpb=attn_rpb[0], sink=attn_sink[0], a_re=ssm_a_re[0],
              a_im=ssm_a_im[0], log_dt=ssm_log_dt[0], b_re=ssm_b_re[0], b_im=ssm_b_im[0], c_re=ssm_c_re[0],
              c_im=ssm_c_im[0], ssm_d=d_full, b_glu=bglu_full, norm_final=norm_final)

    xa = jnp.concatenate([x[0], ctx[0]], axis=0)
    loss_part, d_xa, big, small, dmods = _local_step(xa, loss_target[0], mods, wts, sm, n_x)

    gfull = dict(ffn_w1=big["w1"], ffn_w3=big["w3"], ffn_w2=big["w2"], attn_w_in=big["w_in"][None],
                 attn_w_out=big["w_out"][None], ssm_w_glu=big["w_glu"][None])
    grads = {}
    for n in SHARDED_BIG:
        blk = _full_to_blocks(gfull[n], BIG_KIND[n]).astype(bf16).reshape(N_DEV, -1, w[n].shape[-1])
        grads[n] = _sum0(_exchange(blk, "xyc", False, "scatter_" + n), "sum_" + n).reshape(w[n].shape)

    small_names = ("norm_mix", "norm_ffn", "rpb", "sink", "a_re", "a_im", "log_dt", "b_re", "b_im", "c_re", "c_im",
                   "ssm_d", "b_glu", "norm_final")
    parts = [loss_part.reshape(1)] + [small[n] for n in small_names] + [dmods[:, 0].reshape(2, 6 * d),
                                                                        dmods[:, 1].reshape(2, 6 * d)]
    shapes = [q.shape for q in parts]
    sg = _exchange(_pack(parts, f32), "xyc", True, "gather_small")
    tot = _unpack(_sum0(sg, "sum_small").reshape(-1), shapes)
    loss = tot[0][0]
    ts = dict(zip(small_names, tot[1:1 + len(small_names)]))
    tot_dmod_x, tot_dmod_c = tot[-2], tot[-1]
    off_x = sum(int(np.prod(s)) for s in shapes[:-2])
    dmod_x_all = sg.reshape(N_DEV, -1)[:, off_x:off_x + 12 * d].reshape(N_DEV, 2, 6 * d)
    dm = jnp.concatenate([dmod_x_all.transpose(1, 0, 2), tot_dmod_c[:, None, :],
                          jnp.zeros((2, 16 - N_DEV - 1, 6 * d), f32)], axis=1)
    dm_loc = my_cols(dm)
    grads["ada_w"] = jnp.stack([_mm(sc_all, dm_loc[l], "tn", f32, "ada_w_grad%d" % l) for l in range(2)])
    grads["ada_b"] = tot_dmod_x + tot_dmod_c
    dsc_part = _mm(dm_loc[0], ada_w[0], "nt", f32, "silu_c_grad", a2=dm_loc[1], b2=ada_w[1])[N_DEV:N_DEV + 1]
    dsc = _sum0(_exchange(dsc_part, "xyc", True, "gather_cctx"), "sum_cctx")
    grads["c_ctx"] = _whole_bwd(_f_silu, [c_ctx[None]], [dsc], "silu_cctx_bwd")[0][0]
    grads.update(norm_mix=ts["norm_mix"], norm_ffn=ts["norm_ffn"], attn_rpb=ts["rpb"][None],
                 attn_sink=ts["sink"][None], ssm_a_re=ts["a_re"][None], ssm_a_im=ts["a_im"][None],
                 ssm_log_dt=ts["log_dt"][None], ssm_b_re=ts["b_re"][None], ssm_b_im=ts["b_im"][None],
                 ssm_c_re=ts["c_re"][None], ssm_c_im=ts["c_im"][None],
                 ssm_d=lax.dynamic_slice_in_dim(ts["ssm_d"], me * d8, d8)[None],
                 ssm_b_glu=lax.dynamic_slice_in_dim(ts["b_glu"], me * 2 * d8, 2 * d8)[None],
                 norm_final=ts["norm_final"])

    delta, new_m, new_v = {}, {}, {}
    for n in BIG:
        two_d = lambda a: a.reshape(-1, a.shape[-1])
        dl, m2, v2 = _adamw(two_d(w[n]), two_d(grads[n]), two_d(p["m_" + n]), two_d(p["v_" + n]), "adamw_" + n)
        delta[n], new_m[n], new_v[n] = (t.reshape(w[n].shape) for t in (dl, m2, v2))
    sshapes = [w[n].shape for n in SMALL]
    packs = [_pack([src[n] for n in SMALL], f32) for src in
             (w, grads, {n: p["m_" + n] for n in SMALL}, {n: p["v_" + n] for n in SMALL})]
    for store, t in zip((delta, new_m, new_v), _adamw(*packs, "adamw_small")):
        store.update(zip(SMALL, _unpack(t.reshape(-1), sshapes)))

    return (loss, d_xa[:n_x][None], *[grads[n] for n in WEIGHTS], *[delta[n] for n in WEIGHTS],
            *[new_m[n] for n in WEIGHTS], *[new_v[n] for n in WEIGHTS])
```

```python
import functools
import math

import numpy as np
import jax
import jax.numpy as jnp
from jax import lax
from jax.experimental import pallas as pl
from jax.experimental.pallas import tpu as pltpu

f32, bf16 = jnp.float32, jnp.bfloat16

HEAD_DIM = 128
GRID_W = 64
NA_HEADS = 8
NB_Q_HEADS = 8
NB_KV_HEADS = 2
NB_GROUP = NB_Q_HEADS // NB_KV_HEADS
NA_ROWS = 8
NA_COLS = 16
SW_RADIUS = 128
ROPE_BASE = 10000.0
SSM_GROUP = 16
SSM_STATE = 64
EPS = 1e-6
NEG_INF = -1e30
ADAM_LR, ADAM_B1, ADAM_B2, ADAM_EPS, ADAM_WD, ADAM_STEP = 0.001, 0.9, 0.999, 1e-08, 0.01, 10

N_DEV = 8
BLK = 256
SCAN_SEG = 8
GROUPS_PER_CHUNK = 8
V7X_VMEM_LIMIT = 56 * 2 ** 20
MM_VMEM_BUDGET = 36 * 2 ** 20

NT = (((1,), (1,)), ((), ()))
TN = (((0,), (0,)), ((), ()))


def _params(sem):
    return pltpu.CompilerParams(dimension_semantics=sem, vmem_limit_bytes=V7X_VMEM_LIMIT)


def _pick(n, cands):
    for c in cands:
        if n % c == 0:
            return c
    return n


def _mm(a, b, mode, out_dtype, name, a2=None, b2=None):
    if mode == "tn":
        kdim, m, n = a.shape[0], a.shape[1], b.shape[1]
    elif mode == "nt":
        m, kdim, n = a.shape[0], a.shape[1], b.shape[0]
    else:
        m, kdim, n = a.shape[0], a.shape[1], b.shape[1]
    if mode == "tn":
        tm = _pick(m, (2048, 1024, 512, 256, 128))
        tk = _pick(kdim, (1024, 768, 512, 256, 128))
    else:
        tm = _pick(m, (1024, 768, 512, 256, 128))
        tk = _pick(kdim, (2048, 1024, 768, 512, 256, 128))
    in_bytes = max(a.dtype.itemsize, b.dtype.itemsize) * (1 if a2 is None else 2)
    out_bytes = jnp.dtype(out_dtype).itemsize

    def vmem(tn_):
        return 2 * in_bytes * tk * (tm + tn_) + (4 + 2 * out_bytes) * tm * tn_

    tn = next((t for t in (2048, 1024, 512, 256, 128) if n % t == 0 and vmem(t) <= MM_VMEM_BUDGET), n)
    nk = kdim // tk
    if mode == "tn":
        a_spec = pl.BlockSpec((tk, tm), lambda i, j, k: (k, i))
        b_spec = pl.BlockSpec((tk, tn), lambda i, j, k: (k, j))
        dims = TN
    elif mode == "nt":
        a_spec = pl.BlockSpec((tm, tk), lambda i, j, k: (i, k))
        b_spec = pl.BlockSpec((tn, tk), lambda i, j, k: (j, k))
        dims = NT
    else:
        a_spec = pl.BlockSpec((tm, tk), lambda i, j, k: (i, k))
        b_spec = pl.BlockSpec((tk, tn), lambda i, j, k: (k, j))
        dims = (((1,), (0,)), ((), ()))

    pairs = 1 if a2 is None else 2

    def body(*refs):
        o_ref, acc_ref = refs[2 * pairs:]
        k = pl.program_id(2)

        @pl.when(k == 0)
        def _():
            acc_ref[...] = jnp.zeros_like(acc_ref)

        for p in range(pairs):
            acc_ref[...] += lax.dot_general(refs[2 * p][...].astype(bf16), refs[2 * p + 1][...].astype(bf16), dims,
                                            preferred_element_type=f32)

        @pl.when(k == nk - 1)
        def _():
            o_ref[...] = acc_ref[...].astype(o_ref.dtype)

    args = (a, b) if a2 is None else (a, b, a2, b2)
    return pl.pallas_call(
        body, name=name, grid=(m // tm, n // tn, nk),
        in_specs=[a_spec, b_spec] * pairs, out_specs=pl.BlockSpec((tm, tn), lambda i, j, k: (i, j)),
        out_shape=jax.ShapeDtypeStruct((m, n), out_dtype),
        scratch_shapes=[pltpu.VMEM((tm, tn), f32)],
        compiler_params=_params(("parallel", "parallel", "arbitrary")),
    )(*args)


def _ffn_up(h, w1, w3, name):
    m, kdim = h.shape
    n = w1.shape[1]
    tm, tn = _pick(m, (1024, 768, 512, 256, 128)), _pick(n, (512, 256, 128))

    def body(h_ref, w1_ref, w3_ref, a_ref, b_ref, u_ref):
        hv = h_ref[...]
        a = jnp.dot(hv, w1_ref[...], preferred_element_type=f32).astype(bf16)
        b = jnp.dot(hv, w3_ref[...], preferred_element_type=f32).astype(bf16)
        a_ref[...] = a
        b_ref[...] = b
        u_ref[...] = (jax.nn.silu(a.astype(f32)) * b.astype(f32)).astype(bf16)

    wspec = pl.BlockSpec((kdim, tn), lambda i, j: (0, j))
    ospec = pl.BlockSpec((tm, tn), lambda i, j: (i, j))
    return pl.pallas_call(
        body, name=name, grid=(m // tm, n // tn),
        in_specs=[pl.BlockSpec((tm, kdim), lambda i, j: (i, 0)), wspec, wspec], out_specs=[ospec] * 3,
        out_shape=[jax.ShapeDtypeStruct((m, n), bf16)] * 3, compiler_params=_params(("parallel", "parallel")),
    )(h, w1, w3)


def _ffn_down_bwd(g, w2, a, b, name):
    m, kdim = g.shape
    n = w2.shape[0]
    tm, tn = _pick(m, (1024, 768, 512, 256, 128)), _pick(n, (512, 256, 128))

    def body(g_ref, w_ref, a_ref, b_ref, da_ref, db_ref):
        du = lax.dot_general(g_ref[...], w_ref[...], NT, preferred_element_type=f32)
        _, vjp = jax.vjp(lambda p, q: jax.nn.silu(p) * q, a_ref[...].astype(f32), b_ref[...].astype(f32))
        da, db = vjp(du)
        da_ref[...] = da.astype(bf16)
        db_ref[...] = db.astype(bf16)

    ospec = pl.BlockSpec((tm, tn), lambda i, j: (i, j))
    return pl.pallas_call(
        body, name=name, grid=(m // tm, n // tn),
        in_specs=[pl.BlockSpec((tm, kdim), lambda i, j: (i, 0)), pl.BlockSpec((tn, kdim), lambda i, j: (j, 0)),
                  ospec, ospec],
        out_specs=[ospec] * 2, out_shape=[jax.ShapeDtypeStruct((m, n), bf16)] * 2,
        compiler_params=_params(("parallel", "parallel")),
    )(g, w2, a, b)


def _small_dot(a, b, name):
    def body(a_ref, b_ref, o_ref):
        o_ref[...] = jnp.dot(a_ref[...], b_ref[...], precision=lax.Precision.HIGHEST, preferred_element_type=f32)

    return pl.pallas_call(body, name=name, out_shape=jax.ShapeDtypeStruct((a.shape[0], b.shape[1]), f32))(a, b)


def _group_of(i, n_x_tiles, n_groups):
    return jnp.where(i >= n_x_tiles, n_groups - 1, 0)


def _rowwise(f, rows, vecs, outs, *, nrows, n_x_rows, name, tm=BLK, tc=None):
    n_x_tiles = n_x_rows // tm
    grid = (nrows // tm,) if tc is None else (nrows // tm, rows[0].shape[1] // tc)

    def rspec(cols):
        if tc is None:
            return pl.BlockSpec((tm, cols), lambda i: (i, 0))
        return pl.BlockSpec((tm, tc), lambda i, j: (i, j))

    def vspec(v):
        g = v.shape[0]
        if tc is None:
            return pl.BlockSpec((None, 1, v.shape[2]), lambda i: (_group_of(i, n_x_tiles, g), 0, 0))
        return pl.BlockSpec((None, 1, tc), lambda i, j: (_group_of(i, n_x_tiles, g), 0, j))

    nr, nv = len(rows), len(vecs)

    def body(*refs):
        ins = [r[...] for r in refs[:nr + nv]]
        res = f(*ins)
        for o_ref, val in zip(refs[nr + nv:], res):
            o_ref[...] = val.astype(o_ref.dtype)

    return pl.pallas_call(
        body, name=name, grid=grid,
        in_specs=[rspec(r.shape[1]) for r in rows] + [vspec(v) for v in vecs],
        out_specs=[rspec(c) for c, _ in outs],
        out_shape=[jax.ShapeDtypeStruct((nrows, c), d) for c, d in outs],
        compiler_params=_params(("parallel",) * len(grid)),
    )(*rows, *vecs)


def _rowwise_bwd(f, rows, vecs, cts, row_grads, *, nrows, n_x_rows, name, tm=BLK, tc=None, residual=None,
                 residual_rows=None):
    n_x_tiles = n_x_rows // tm
    n_tiles = nrows // tm
    res_tiles = None if residual_rows is None else residual_rows // tm
    grid = (n_tiles,) if tc is None else (rows[0].shape[1] // tc, n_tiles)
    row_of = (lambda *g: g[0]) if tc is None else (lambda *g: g[1])

    def rspec(cols):
        if tc is None:
            return pl.BlockSpec((tm, cols), lambda i: (i, 0))
        return pl.BlockSpec((tm, tc), lambda j, i: (i, j))

    def vspec(v):
        g = v.shape[0]
        if tc is None:
            return pl.BlockSpec((None, 1, v.shape[2]), lambda i: (_group_of(i, n_x_tiles, g), 0, 0))
        return pl.BlockSpec((None, 1, tc), lambda j, i: (_group_of(i, n_x_tiles, g), 0, j))

    nr, nv, nc = len(rows), len(vecs), len(cts)
    gidx = sorted(row_grads)

    def body(*refs):
        i = row_of(*[pl.program_id(d) for d in range(len(grid))])
        row_vals = [r[...].astype(f32) for r in refs[:nr]]
        vec_vals = [jnp.broadcast_to(r[...].astype(f32), (tm, r.shape[-1])) for r in refs[nr:nr + nv]]
        ct_vals = [r[...].astype(f32) for r in refs[nr + nv:nr + nv + nc]]
        out_refs = refs[nr + nv + nc + (residual is not None):]
        res, vjp = jax.vjp(lambda *a: tuple(o.astype(f32) for o in f(*a)), *row_vals, *vec_vals)
        grads = list(vjp(tuple(ct_vals)))
        if residual is not None:
            extra = refs[nr + nv + nc][...]
            if res_tiles is not None:
                extra = jnp.where(i < res_tiles, extra, 0.0)
            grads[0] = grads[0] + extra
        for o_ref, k in zip(out_refs[:len(gidx)], gidx):
            o_ref[...] = grads[k].astype(o_ref.dtype)
        for o_ref, g, v in zip(out_refs[len(gidx):], grads[nr:], vecs):
            part = jnp.sum(g, axis=0, keepdims=True)
            first = (i == 0) if v.shape[0] == 1 else ((i == 0) | (i == n_x_tiles))

            @pl.when(first)
            def _():
                o_ref[...] = part

            @pl.when(jnp.logical_not(first))
            def _():
                o_ref[...] += part

    sem = ("arbitrary",) if tc is None else ("parallel", "arbitrary")
    res_specs, res_args = [], []
    if residual is not None:
        assert tc is None
        clamp = (lambda i: i) if res_tiles is None else (lambda i: jnp.minimum(i, res_tiles - 1))
        res_specs = [pl.BlockSpec((tm, residual.shape[1]), lambda i: (clamp(i), 0))]
        res_args = [residual]
    return pl.pallas_call(
        body, name=name, grid=grid,
        in_specs=[rspec(r.shape[1]) for r in rows] + [vspec(v) for v in vecs] + [rspec(c.shape[1]) for c in cts]
        + res_specs,
        out_specs=[rspec(rows[k].shape[1]) for k in gidx] + [vspec(v) for v in vecs],
        out_shape=[jax.ShapeDtypeStruct((nrows, rows[k].shape[1]), row_grads[k]) for k in gidx]
        + [jax.ShapeDtypeStruct(v.shape, f32) for v in vecs],
        compiler_params=_params(sem),
    )(*rows, *vecs, *cts, *res_args)


def _rms(x, w):
    return x * lax.rsqrt(jnp.mean(x * x, axis=-1, keepdims=True) + EPS) * w


def _f_normmod(x, w, sc, sh):
    return (_rms(x.astype(f32), w) * (1.0 + sc) + sh,)


def _f_gated_add(x, y, g):
    return (x + g * y.astype(f32),)


def _f_glu_post(x, zv, zg, g, bv, bg):
    return (x + g * ((zv.astype(f32) + bv) * jax.nn.sigmoid(zg.astype(f32) + bg)),)


def _f_silu(x):
    return (jax.nn.silu(x.astype(f32)),)


def _loss_head(x, tgt, w, *, nrows, name):
    d = x.shape[1]
    tm = BLK

    def body(x_ref, t_ref, w_ref, dx_ref, dw_ref, loss_ref):
        i = pl.program_id(0)
        wb = jnp.broadcast_to(w_ref[...], (tm, d))
        y, vjp = jax.vjp(_rms, x_ref[...], wb)
        e = y - t_ref[...]
        dx, dwb = vjp(e * (1.0 / d))
        dx_ref[...] = dx
        dw = jnp.sum(dwb, axis=0, keepdims=True)
        part = jnp.full((8, 128), 0.5 / d, f32) * jnp.sum(e * e)

        @pl.when(i == 0)
        def _():
            dw_ref[...] = dw
            loss_ref[...] = part

        @pl.when(i > 0)
        def _():
            dw_ref[...] += dw
            loss_ref[...] += part

    row = pl.BlockSpec((tm, d), lambda i: (i, 0))
    return pl.pallas_call(
        body, name=name, grid=(nrows // tm,),
        in_specs=[row, row, pl.BlockSpec((1, d), lambda i: (0, 0))],
        out_specs=[row, pl.BlockSpec((1, d), lambda i: (0, 0)), pl.BlockSpec((8, 128), lambda i: (0, 0))],
        out_shape=[jax.ShapeDtypeStruct((nrows, d), f32), jax.ShapeDtypeStruct((1, d), f32),
                   jax.ShapeDtypeStruct((8, 128), f32)],
        compiler_params=_params(("arbitrary",)),
    )(x, tgt, w)


def _adamw(w, g, m, v, name):
    r, c = w.shape
    tr = _pick(r, (512, 256, 128, 64, 32, 16, 8))
    tcol = _pick(c, (1024, 512)) if c % 128 == 0 else c

    def body(w_ref, g_ref, m_ref, v_ref, d_ref, m2_ref, v2_ref):
        gg = g_ref[...]
        m2 = ADAM_B1 * m_ref[...] + (1.0 - ADAM_B1) * gg
        v2 = ADAM_B2 * v_ref[...] + (1.0 - ADAM_B2) * (gg * gg)
        m_hat = m2 / (1.0 - ADAM_B1 ** ADAM_STEP)
        v_hat = v2 / (1.0 - ADAM_B2 ** ADAM_STEP)
        d_ref[...] = -ADAM_LR * (m_hat / (jnp.sqrt(v_hat) + ADAM_EPS) + ADAM_WD * w_ref[...])
        m2_ref[...] = m2
        v2_ref[...] = v2

    spec = pl.BlockSpec((tr, tcol), lambda i, j: (i, j))
    return pl.pallas_call(
        body, name=name, grid=(r // tr, c // tcol), in_specs=[spec] * 4, out_specs=[spec] * 3,
        out_shape=[jax.ShapeDtypeStruct((r, c), f32)] * 3,
        compiler_params=_params(("parallel", "parallel")),
    )(w, g, m, v)


def _swap_quarters(x):
    lane = lax.broadcasted_iota(jnp.int32, x.shape, 1)
    return jnp.where((lane & 63) < 32, pltpu.roll(x, 96, 1), pltpu.roll(x, 32, 1))


def _rope_tables(n_x, n_ctx):
    t = np.arange(n_x)
    quarter = HEAD_DIM // 4
    inv = ROPE_BASE ** (-np.arange(quarter, dtype=np.float64) / quarter)
    ar = (t // GRID_W)[:, None] * inv[None]
    ac = (t % GRID_W)[:, None] * inv[None]
    cos = np.concatenate([np.cos(ar), np.cos(ar), np.cos(ac), np.cos(ac)], axis=1)
    sin = np.concatenate([-np.sin(ar), np.sin(ar), -np.sin(ac), np.sin(ac)], axis=1)
    cos = np.concatenate([cos, np.ones((n_ctx, HEAD_DIM))], axis=0)
    sin = np.concatenate([sin, np.zeros((n_ctx, HEAD_DIM))], axis=0)
    return jnp.asarray(cos, f32), jnp.asarray(sin, f32)


A_W = NA_HEADS * HEAD_DIM
QB0, KB0, VB0 = 3 * A_W, 3 * A_W + NB_Q_HEADS * HEAD_DIM, 3 * A_W + (NB_Q_HEADS + NB_KV_HEADS) * HEAD_DIM
IN_W = VB0 + NB_KV_HEADS * HEAD_DIM


def _qkv_post(qkv, cos, sin, name):
    n = qkv.shape[0]

    def body(x_ref, c_ref, s_ref, o_ref):
        c, s = c_ref[...], s_ref[...]
        o_ref[:, :QB0] = x_ref[:, :QB0].astype(bf16)
        for col in range(QB0, VB0, HEAD_DIM):
            x = x_ref[:, col:col + HEAD_DIM]
            o_ref[:, col:col + HEAD_DIM] = (x * c + _swap_quarters(x) * s).astype(bf16)
        o_ref[:, VB0:] = x_ref[:, VB0:].astype(bf16)

    row = lambda c: pl.BlockSpec((BLK, c), lambda i: (i, 0))
    return pl.pallas_call(
        body, name=name, grid=(n // BLK,), in_specs=[row(IN_W), row(HEAD_DIM), row(HEAD_DIM)],
        out_specs=row(IN_W), out_shape=jax.ShapeDtypeStruct((n, IN_W), bf16),
        compiler_params=_params(("parallel",)),
    )(qkv, cos, sin)


def _qkv_post_bwd(parts, cos, sin, name):
    n = parts[0].shape[0]

    def body(qa, ka, va, qb, kb, vb, c_ref, s_ref, o_ref):
        c, s = c_ref[...], s_ref[...]
        o_ref[:, 0:A_W] = qa[...]
        o_ref[:, A_W:2 * A_W] = ka[...]
        o_ref[:, 2 * A_W:QB0] = va[...]
        for src, col0, width in ((qb, QB0, KB0 - QB0), (kb, KB0, VB0 - KB0)):
            for off in range(0, width, HEAD_DIM):
                g = src[:, off:off + HEAD_DIM].astype(f32)
                o_ref[:, col0 + off:col0 + off + HEAD_DIM] = (g * c + _swap_quarters(g * s)).astype(bf16)
        o_ref[:, VB0:] = vb[...]

    row = lambda c: pl.BlockSpec((BLK, c), lambda i: (i, 0))
    return pl.pallas_call(
        body, name=name, grid=(n // BLK,),
        in_specs=[row(p.shape[1]) for p in parts] + [row(HEAD_DIM), row(HEAD_DIM)],
        out_specs=row(IN_W), out_shape=jax.ShapeDtypeStruct((n, IN_W), bf16),
        compiler_params=_params(("parallel",)),
    )(*parts, cos, sin)


def _valid(kind, qpos, kpos, n_x):
    ok = (kpos >= 0) & (kpos < n_x) & (qpos >= 0) & (qpos < n_x)
    if kind == "na":
        rows = n_x // GRID_W
        qr, qc = lax.shift_right_arithmetic(qpos, 6), qpos & (GRID_W - 1)
        kr, kc = lax.shift_right_arithmetic(kpos, 6), kpos & (GRID_W - 1)
        kr0 = jnp.clip(qr - NA_ROWS // 2, 0, rows - NA_ROWS)
        ws = jnp.clip(qc - NA_COLS // 2, 0, GRID_W - NA_COLS)
        return ok & (kr >= kr0) & (kr < kr0 + NA_ROWS) & (kc >= ws) & (kc < ws + NA_COLS)
    return ok & (jnp.abs(kpos - qpos) <= SW_RADIUS)


def _stack_heads(x, g):
    if g == 1:
        return x
    return jnp.concatenate([x[:, a * HEAD_DIM:(a + 1) * HEAD_DIM] for a in range(g)], axis=0)


def _unstack_heads(x, g):
    if g == 1:
        return x
    r = x.shape[0] // g
    return jnp.concatenate([x[a * r:(a + 1) * r] for a in range(g)], axis=1)


def _tile_rows(x, g):
    return x if g == 1 else jnp.concatenate([x] * g, axis=0)


class _AttnCfg:
    def __init__(self, kind, n_x, n_tot):
        self.kind, self.n_x, self.n_tot = kind, n_x, n_tot
        self.n_xb, self.n_blk = n_x // BLK, n_tot // BLK
        if kind == "na":
            self.g, self.nkv, self.q0, self.k0, self.v0 = 1, NA_HEADS, 0, NA_HEADS, 2 * NA_HEADS
        else:
            self.g, self.nkv = NB_GROUP, NB_KV_HEADS
            self.q0, self.k0, self.v0 = QB0 // (NB_GROUP * HEAD_DIM), KB0 // HEAD_DIM, VB0 // HEAD_DIM
        self.r = BLK * self.g
        self.qw = HEAD_DIM * self.g
        self.scale = HEAD_DIM ** -0.5


def _attn_fwd(qkv, cfg, extra, name):
    g, r, qw, n_xb, n_x = cfg.g, cfg.r, cfg.qw, cfg.n_xb, cfg.n_x
    last = n_xb - 1

    def body(q_ref, kp, ko, kn, vp, vo, vn, kc_ref, vc_ref, ex_ref, o_ref, lse_ref):
        i = pl.program_id(1)
        q = _stack_heads(q_ref[...], g)
        kw = jnp.concatenate([kp[...], ko[...], kn[...]], axis=0)
        vw = jnp.concatenate([vp[...], vo[...], vn[...]], axis=0)
        s = lax.dot_general(q, kw, NT, preferred_element_type=f32) * cfg.scale
        if cfg.kind == "na":
            s = s + ex_ref[...]
        qpos = i * BLK + lax.broadcasted_iota(jnp.int32, (BLK, 3 * BLK), 0)
        kpos = (i - 1) * BLK + lax.broadcasted_iota(jnp.int32, (BLK, 3 * BLK), 1)
        s = jnp.where(_tile_rows(_valid(cfg.kind, qpos, kpos, n_x), g), s, NEG_INF)
        sc = lax.dot_general(q, kc_ref[...], NT, preferred_element_type=f32) * cfg.scale
        m = jnp.maximum(jnp.max(s, axis=-1, keepdims=True), jnp.max(sc, axis=-1, keepdims=True))
        if cfg.kind == "swa":
            m = jnp.maximum(m, ex_ref[...])
        p, pc = jnp.exp(s - m), jnp.exp(sc - m)
        l = jnp.sum(p, axis=-1, keepdims=True) + jnp.sum(pc, axis=-1, keepdims=True)
        if cfg.kind == "swa":
            l = l + jnp.exp(ex_ref[...] - m)
        o = jnp.dot(p.astype(bf16), vw, preferred_element_type=f32) + jnp.dot(pc.astype(bf16), vc_ref[...],
                                                                             preferred_element_type=f32)
        o_ref[...] = _unstack_heads(o / l, g).astype(bf16)
        lse_ref[...] = m + jnp.log(l)

    kv = lambda col0, f: pl.BlockSpec((BLK, HEAD_DIM), lambda h, i: (f(i), col0 + h))
    prev = lambda i: jnp.clip(i - 1, 0, last)
    own = lambda i: jnp.minimum(i, last)
    nxt = lambda i: jnp.minimum(i + 1, last)
    ctx = lambda i: n_xb
    if cfg.kind == "na":
        ex_spec = pl.BlockSpec((None, BLK, 3 * BLK), lambda h, i: (h, 0, 0))
    else:
        ex_spec = pl.BlockSpec((None, r, 1), lambda h, i: (h, 0, 0))
    return pl.pallas_call(
        body, name=name, grid=(cfg.nkv, cfg.n_blk),
        in_specs=[pl.BlockSpec((BLK, qw), lambda h, i: (i, cfg.q0 + h)),
                  kv(cfg.k0, prev), kv(cfg.k0, own), kv(cfg.k0, nxt),
                  kv(cfg.v0, prev), kv(cfg.v0, own), kv(cfg.v0, nxt),
                  kv(cfg.k0, ctx), kv(cfg.v0, ctx), ex_spec],
        out_specs=[pl.BlockSpec((BLK, qw), lambda h, i: (i, h)),
                   pl.BlockSpec((None, None, r, 1), lambda h, i: (h, i, 0, 0))],
        out_shape=[jax.ShapeDtypeStruct((cfg.n_tot, cfg.nkv * qw), bf16),
                   jax.ShapeDtypeStruct((cfg.nkv, cfg.n_blk, r, 1), f32)],
        compiler_params=_params(("parallel", "parallel")),
    )(qkv, qkv, qkv, qkv, qkv, qkv, qkv, qkv, qkv, extra)


def _attn_bwd_q(qkv, o, do, lse, cfg, extra, name):
    g, r, qw, n_xb, n_x = cfg.g, cfg.r, cfg.qw, cfg.n_xb, cfg.n_x
    last = n_xb - 1
    do_col0 = 0 if cfg.kind == "na" else (NA_HEADS * HEAD_DIM) // qw

    def body(q_ref, kp, ko, kn, vp, vo, vn, kc_ref, vc_ref, ex_ref, o_ref, do_ref, lse_ref,
             dq_ref, delta_ref, dex_ref):
        i = pl.program_id(1)
        q = _stack_heads(q_ref[...], g)
        dout = _stack_heads(do_ref[...], g)
        out = _stack_heads(o_ref[...], g)
        delta = jnp.sum(dout.astype(f32) * out.astype(f32), axis=-1, keepdims=True)
        delta_ref[...] = delta
        kw = jnp.concatenate([kp[...], ko[...], kn[...]], axis=0)
        vw = jnp.concatenate([vp[...], vo[...], vn[...]], axis=0)
        s = lax.dot_general(q, kw, NT, preferred_element_type=f32) * cfg.scale
        if cfg.kind == "na":
            s = s + ex_ref[...]
        qpos = i * BLK + lax.broadcasted_iota(jnp.int32, (BLK, 3 * BLK), 0)
        kpos = (i - 1) * BLK + lax.broadcasted_iota(jnp.int32, (BLK, 3 * BLK), 1)
        s = jnp.where(_tile_rows(_valid(cfg.kind, qpos, kpos, n_x), g), s, NEG_INF)
        sc = lax.dot_general(q, kc_ref[...], NT, preferred_element_type=f32) * cfg.scale
        lse_v = lse_ref[...]
        p, pc = jnp.exp(s - lse_v), jnp.exp(sc - lse_v)
        dp = lax.dot_general(dout, vw, NT, preferred_element_type=f32)
        dpc = lax.dot_general(dout, vc_ref[...], NT, preferred_element_type=f32)
        ds, dsc = p * (dp - delta), pc * (dpc - delta)
        dq = jnp.dot(ds.astype(bf16), kw, preferred_element_type=f32) + jnp.dot(dsc.astype(bf16), kc_ref[...],
                                                                             preferred_element_type=f32)
        dq_ref[...] = _unstack_heads(dq * cfg.scale, g).astype(bf16)
        dex = ds if cfg.kind == "na" else -jnp.exp(ex_ref[...] - lse_v) * delta

        @pl.when(i == 0)
        def _():
            dex_ref[...] = dex

        @pl.when(i > 0)
        def _():
            dex_ref[...] += dex

    kv = lambda col0, f: pl.BlockSpec((BLK, HEAD_DIM), lambda h, i: (f(i), col0 + h))
    prev = lambda i: jnp.clip(i - 1, 0, last)
    own = lambda i: jnp.minimum(i, last)
    nxt = lambda i: jnp.minimum(i + 1, last)
    ctx = lambda i: n_xb
    if cfg.kind == "na":
        ex_spec = pl.BlockSpec((None, BLK, 3 * BLK), lambda h, i: (h, 0, 0))
    else:
        ex_spec = pl.BlockSpec((None, r, 1), lambda h, i: (h, 0, 0))
    stat = pl.BlockSpec((None, None, r, 1), lambda h, i: (h, i, 0, 0))
    return pl.pallas_call(
        body, name=name, grid=(cfg.nkv, cfg.n_blk),
        in_specs=[pl.BlockSpec((BLK, qw), lambda h, i: (i, cfg.q0 + h)),
                  kv(cfg.k0, prev), kv(cfg.k0, own), kv(cfg.k0, nxt),
                  kv(cfg.v0, prev), kv(cfg.v0, own), kv(cfg.v0, nxt),
                  kv(cfg.k0, ctx), kv(cfg.v0, ctx), ex_spec,
                  pl.BlockSpec((BLK, qw), lambda h, i: (i, h)),
                  pl.BlockSpec((BLK, qw), lambda h, i: (i, do_col0 + h)), stat],
        out_specs=[pl.BlockSpec((BLK, qw), lambda h, i: (i, h)), stat, ex_spec],
        out_shape=[jax.ShapeDtypeStruct((cfg.n_tot, cfg.nkv * qw), bf16),
                   jax.ShapeDtypeStruct((cfg.nkv, cfg.n_blk, r, 1), f32),
                   jax.ShapeDtypeStruct(extra.shape, f32)],
        compiler_params=_params(("parallel", "arbitrary")),
    )(qkv, qkv, qkv, qkv, qkv, qkv, qkv, qkv, qkv, extra, o, do, lse)


def _attn_bwd_kv(qkv, do, lse, delta, cfg, bias_t, name):
    g, r, qw, n_xb, n_x, n_blk = cfg.g, cfg.r, cfg.qw, cfg.n_xb, cfg.n_x, cfg.n_blk
    last = n_xb - 1
    do_col0 = 0 if cfg.kind == "na" else (NA_HEADS * HEAD_DIM) // qw
    has_bias = cfg.kind == "na"

    def body(*refs):
        qs, dos, lses, dels = refs[0:3], refs[3:6], refs[6:9], refs[9:12]
        qj_ref, doj_ref, lsej_ref, delj_ref, k_ref, v_ref, kc_ref, vc_ref = refs[12:20]
        rest = refs[20:]
        if has_bias:
            b_ref, rest = rest[0], rest[1:]
        dk_ref, dv_ref, dkc_acc, dvc_acc = rest
        j = pl.program_id(1)

        @pl.when(j == 0)
        def _():
            dkc_acc[...] = jnp.zeros_like(dkc_acc)
            dvc_acc[...] = jnp.zeros_like(dvc_acc)

        qj, doj = _stack_heads(qj_ref[...], g), _stack_heads(doj_ref[...], g)
        sc = lax.dot_general(qj, kc_ref[...], NT, preferred_element_type=f32) * cfg.scale
        pc = jnp.exp(sc - lsej_ref[...])
        dvc_acc[...] += lax.dot_general(pc.astype(bf16), doj, TN, preferred_element_type=f32)
        dpc = lax.dot_general(doj, vc_ref[...], NT, preferred_element_type=f32)
        dsc = pc * (dpc - delj_ref[...])
        dkc_acc[...] += lax.dot_general(dsc.astype(bf16), qj, TN, preferred_element_type=f32) * cfg.scale

        @pl.when(j < n_xb)
        def _():
            qw_all = jnp.concatenate([_stack_heads(x[...], g) for x in qs], axis=0)
            do_all = jnp.concatenate([_stack_heads(x[...], g) for x in dos], axis=0)
            lse_all = jnp.concatenate([x[...] for x in lses], axis=0)
            del_all = jnp.concatenate([x[...] for x in dels], axis=0)
            s = lax.dot_general(qw_all, k_ref[...], NT, preferred_element_type=f32) * cfg.scale
            if has_bias:
                s = s + b_ref[...]
            kpos = j * BLK + lax.broadcasted_iota(jnp.int32, (BLK, BLK), 1)
            masks = []
            for w in range(3):
                qpos = (j - 1 + w) * BLK + lax.broadcasted_iota(jnp.int32, (BLK, BLK), 0)
                masks.append(_tile_rows(_valid(cfg.kind, qpos, kpos, n_x), g))
            s = jnp.where(jnp.concatenate(masks, axis=0), s, NEG_INF)
            p = jnp.exp(s - lse_all)
            dv_ref[...] = lax.dot_general(p.astype(bf16), do_all, TN, preferred_element_type=f32).astype(bf16)
            dp = lax.dot_general(do_all, v_ref[...], NT, preferred_element_type=f32)
            ds = p * (dp - del_all)
            dk_ref[...] = (lax.dot_general(ds.astype(bf16), qw_all, TN, preferred_element_type=f32)
                           * cfg.scale).astype(bf16)

        @pl.when(j == n_xb)
        def _():
            dk_ref[...] = dkc_acc[...].astype(bf16)
            dv_ref[...] = dvc_acc[...].astype(bf16)

    prev = lambda j: jnp.clip(j - 1, 0, last)
    own = lambda j: jnp.minimum(j, last)
    nxt = lambda j: jnp.minimum(j + 1, last)
    same = lambda j: j
    qspec = lambda f: pl.BlockSpec((BLK, qw), lambda h, j: (f(j), cfg.q0 + h))
    dospec = lambda f: pl.BlockSpec((BLK, qw), lambda h, j: (f(j), do_col0 + h))
    stat = lambda f: pl.BlockSpec((None, None, r, 1), lambda h, j: (h, f(j), 0, 0))
    kv = lambda col0, f: pl.BlockSpec((BLK, HEAD_DIM), lambda h, j: (f(j), col0 + h))
    ctx = lambda j: n_xb
    in_specs = ([qspec(f) for f in (prev, own, nxt)] + [dospec(f) for f in (prev, own, nxt)]
                + [stat(f) for f in (prev, own, nxt)] * 2
                + [qspec(same), dospec(same), stat(same), stat(same),
                   kv(cfg.k0, same), kv(cfg.v0, same), kv(cfg.k0, ctx), kv(cfg.v0, ctx)])
    args = [qkv] * 3 + [do] * 3 + [lse] * 3 + [delta] * 3 + [qkv, do, lse, delta, qkv, qkv, qkv, qkv]
    if has_bias:
        in_specs.append(pl.BlockSpec((None, 3 * BLK, BLK), lambda h, j: (h, 0, 0)))
        args.append(bias_t)
    out = pl.BlockSpec((BLK, HEAD_DIM), lambda h, j: (j, h))
    return pl.pallas_call(
        body, name=name, grid=(cfg.nkv, n_blk), in_specs=in_specs, out_specs=[out, out],
        out_shape=[jax.ShapeDtypeStruct((cfg.n_tot, cfg.nkv * HEAD_DIM), bf16)] * 2,
        scratch_shapes=[pltpu.VMEM((BLK, HEAD_DIM), f32)] * 2,
        compiler_params=_params(("parallel", "arbitrary")),
    )(*args)


def _toeplitz_basis():
    qc, kc = np.meshgrid(np.arange(GRID_W), np.arange(GRID_W), indexing="ij")
    e = (kc - qc + NA_COLS - 1)[None] == np.arange(2 * NA_COLS - 1)[:, None, None]
    return e.reshape(2 * NA_COLS - 1, GRID_W * GRID_W).astype(np.float32)


def _whole(f, ins, outs, name):
    def body(*refs):
        res = f(*[r[...] for r in refs[:len(ins)]])
        for o_ref, val in zip(refs[len(ins):], res):
            o_ref[...] = val.astype(o_ref.dtype)

    return pl.pallas_call(body, name=name,
                          out_shape=[jax.ShapeDtypeStruct(s, d) for s, d in outs])(*ins)


def _whole_bwd(f, ins, cts, name):
    n = len(ins)

    def body(*refs):
        _, vjp = jax.vjp(f, *[r[...] for r in refs[:n]])
        grads = vjp(tuple(r[...] for r in refs[n:n + len(cts)]))
        for o_ref, g in zip(refs[n + len(cts):], grads):
            o_ref[...] = g

    return pl.pallas_call(body, name=name,
                          out_shape=[jax.ShapeDtypeStruct(a.shape, f32) for a in ins])(*ins, *cts)


def _f_discretise(ar, ai, ldt, br, bi):
    dt = jnp.exp(ldt)
    mag = jnp.exp(ar * dt)
    lam_r, lam_i = mag * jnp.cos(ai * dt), mag * jnp.sin(ai * dt)
    den = ar * ar + ai * ai
    nr = lam_r - 1.0
    coef_r = (nr * ar + lam_i * ai) / den
    coef_i = (lam_i * ar - nr * ai) / den
    return (lam_r, lam_i, coef_r[None] * br - coef_i[None] * bi, coef_r[None] * bi + coef_i[None] * br)


SEG_LEN = BLK // SCAN_SEG
N_STATE = GROUPS_PER_CHUNK * SSM_STATE


def _scan_perm():
    r = np.arange(BLK)
    t = (r % SCAN_SEG) * SEG_LEN + r // SCAN_SEG
    pm = np.zeros((BLK, BLK), np.float32)
    pm[r, t] = 1.0
    return jnp.asarray(pm, bf16), jnp.asarray(pm.T, bf16)


def _row_perm(x, pm, name, *, out_dtype, split, add=None, add_rows=None):
    n, c = x.shape
    add_tiles = None if add is None else add_rows // BLK

    def body(*refs):
        x_ref, pm_ref = refs[0], refs[1]
        o_ref = refs[-1]
        xv = x_ref[...]
        if split:
            out = _unpermute(pm_ref[...], xv.astype(f32))
        else:
            out = jnp.dot(pm_ref[...], xv.astype(bf16), preferred_element_type=f32)
        if add is not None:
            out = out + jnp.where(pl.program_id(0) < add_tiles, refs[2][...], 0.0)
        o_ref[...] = out.astype(o_ref.dtype)

    row = pl.BlockSpec((BLK, c), lambda i: (i, 0))
    in_specs, args = [row, pl.BlockSpec((BLK, BLK), lambda i: (0, 0))], [x, pm]
    if add is not None:
        in_specs.append(pl.BlockSpec((BLK, c), lambda i: (jnp.minimum(i, add_tiles - 1), 0)))
        args.append(add)
    return pl.pallas_call(
        body, name=name, grid=(n // BLK,), in_specs=in_specs, out_specs=row,
        out_shape=jax.ShapeDtypeStruct((n, c), out_dtype), compiler_params=_params(("parallel",)),
    )(*args)


def _glu_pre(u, yf, yr, dvec, pmt, n_x, name):
    d = u.shape[1]

    def body(u_ref, yf_ref, yr_ref, d_ref, pmt_ref, o_ref):
        y = d_ref[...] * u_ref[...] + _unpermute(pmt_ref[...], yf_ref[...] + yr_ref[...])
        o_ref[...] = jax.nn.gelu(y).astype(bf16)

    row = pl.BlockSpec((BLK, d), lambda i: (i, 0))
    return pl.pallas_call(
        body, name=name, grid=(n_x // BLK,),
        in_specs=[row, row, row, pl.BlockSpec((1, d), lambda i: (0, 0)), pl.BlockSpec((BLK, BLK), lambda i: (0, 0))],
        out_specs=row, out_shape=jax.ShapeDtypeStruct((n_x, d), bf16), compiler_params=_params(("parallel",)),
    )(u, yf, yr, dvec, pmt)


def _glu_pre_bwd(u, yf, yr, dvec, pm, pmt, dgy, n_x, name):
    d = u.shape[1]

    def body(u_ref, yf_ref, yr_ref, d_ref, pm_ref, pmt_ref, ct_ref, dyp_ref, dud_ref, dd_ref):
        uv = u_ref[...]
        y = d_ref[...] * uv + _unpermute(pmt_ref[...], yf_ref[...] + yr_ref[...])
        _, vjp = jax.vjp(jax.nn.gelu, y)
        dy = vjp(ct_ref[...])[0]
        dyp_ref[...] = jnp.dot(pm_ref[...], dy.astype(bf16), preferred_element_type=f32).astype(bf16)
        dud_ref[...] = d_ref[...] * dy
        part = jnp.sum(dy * uv, axis=0, keepdims=True)

        @pl.when(pl.program_id(0) == 0)
        def _():
            dd_ref[...] = part

        @pl.when(pl.program_id(0) > 0)
        def _():
            dd_ref[...] += part

    row = pl.BlockSpec((BLK, d), lambda i: (i, 0))
    vec = pl.BlockSpec((1, d), lambda i: (0, 0))
    sq = pl.BlockSpec((BLK, BLK), lambda i: (0, 0))
    return pl.pallas_call(
        body, name=name, grid=(n_x // BLK,), in_specs=[row, row, row, vec, sq, sq, row],
        out_specs=[row, row, vec],
        out_shape=[jax.ShapeDtypeStruct((n_x, d), bf16), jax.ShapeDtypeStruct((n_x, d), f32),
                   jax.ShapeDtypeStruct((1, d), f32)],
        compiler_params=_params(("arbitrary",)),
    )(u, yf, yr, dvec, pm, pmt, dgy)


def _block_order(n_xb, n_blk, reverse):
    n_cb = n_blk - n_xb
    if reverse:
        return lambda cc: jnp.where(cc < n_cb, n_blk - 1 - cc, n_xb - 1 - (cc - n_cb))
    return lambda cc: jnp.where(cc < n_cb, n_xb + cc, cc - n_cb)


def _unpermute(pmt, y):
    hi = y.astype(bf16)
    lo = (y - hi.astype(f32)).astype(bf16)
    return jnp.dot(pmt, hi, preferred_element_type=f32) + jnp.dot(pmt, lo, preferred_element_type=f32)


def _lam_pow(lr, li):
    for _ in range(int(math.log2(SEG_LEN))):
        lr, li = lr * lr - li * li, 2.0 * lr * li
    return lr, li


def _s5_fwd(up, lam, bblk, cblk, n_x, reverse, name):
    n_tot, d = up.shape
    nq, n_blk, n_xb = d // 128, n_tot // BLK, n_x // BLK
    order = _block_order(n_xb, n_blk, reverse)
    ns = N_STATE
    seg_order = range(SCAN_SEG - 1, -1, -1) if reverse else range(SCAN_SEG)

    def body(u_ref, lam_ref, b_ref, c_ref, y_ref, cp_ref, bu_ref, st_ref, carry_ref):
        cc = pl.program_id(1)

        @pl.when(cc == 0)
        def _():
            carry_ref[...] = jnp.zeros_like(carry_ref)

        up_v = u_ref[...]
        bu_ref[0] = jnp.dot(up_v, b_ref[0], preferred_element_type=f32)
        bu_ref[1] = jnp.dot(up_v, b_ref[1], preferred_element_type=f32)
        lr, li = lam_ref[0:1, :], lam_ref[1:2, :]
        lrb, lib = jnp.broadcast_to(lr, (SCAN_SEG, ns)), jnp.broadcast_to(li, (SCAN_SEG, ns))

        def step(p, s, store):
            sr, si = s
            j = SEG_LEN - 1 - p if reverse else p
            off = pl.multiple_of(j * SCAN_SEG, SCAN_SEG)
            nsr = lrb * sr - lib * si + bu_ref[0, pl.ds(off, SCAN_SEG), :]
            nsi = lrb * si + lib * sr + bu_ref[1, pl.ds(off, SCAN_SEG), :]
            if store:
                st_ref[0, pl.ds(off, SCAN_SEG), :] = nsr
                st_ref[1, pl.ds(off, SCAN_SEG), :] = nsi
            return nsr, nsi

        zero = jnp.zeros((SCAN_SEG, ns), f32)
        er, ei = lax.fori_loop(0, SEG_LEN, lambda j, s: step(j, s, False), (zero, zero))
        pr, pi = _lam_pow(lr, li)
        cr, ci = carry_ref[0, 0:1, :], carry_ref[1, 0:1, :]
        rows_r, rows_i = [None] * SCAN_SEG, [None] * SCAN_SEG
        for k in seg_order:
            rows_r[k], rows_i[k] = cr, ci
            cr, ci = pr * cr - pi * ci + er[k:k + 1], pr * ci + pi * cr + ei[k:k + 1]
        carry_ref[0] = jnp.broadcast_to(cr, (SCAN_SEG, ns))
        carry_ref[1] = jnp.broadcast_to(ci, (SCAN_SEG, ns))
        cpr, cpi = jnp.concatenate(rows_r, axis=0), jnp.concatenate(rows_i, axis=0)
        cp_ref[0] = cpr
        cp_ref[1] = cpi
        lax.fori_loop(0, SEG_LEN, lambda j, s: step(j, s, True), (cpr, cpi))
        y_ref[...] = (jnp.dot(st_ref[0].astype(bf16), c_ref[0], preferred_element_type=f32)
                      - jnp.dot(st_ref[1].astype(bf16), c_ref[1], preferred_element_type=f32))

    return pl.pallas_call(
        body, name=name, grid=(nq, n_blk),
        in_specs=[pl.BlockSpec((BLK, 128), lambda q, cc: (order(cc), q)),
                  pl.BlockSpec((None, 2, ns), lambda q, cc: (q, 0, 0)),
                  pl.BlockSpec((None, 2, 128, ns), lambda q, cc: (q, 0, 0, 0)),
                  pl.BlockSpec((None, 2, ns, 128), lambda q, cc: (q, 0, 0, 0))],
        out_specs=[pl.BlockSpec((BLK, 128), lambda q, cc: (order(cc), q)),
                   pl.BlockSpec((None, None, 2, SCAN_SEG, ns), lambda q, cc: (q, cc, 0, 0, 0))],
        out_shape=[jax.ShapeDtypeStruct((n_tot, d), f32),
                   jax.ShapeDtypeStruct((nq, n_blk, 2, SCAN_SEG, ns), f32)],
        scratch_shapes=[pltpu.VMEM((2, BLK, ns), f32), pltpu.VMEM((2, BLK, ns), f32),
                        pltpu.VMEM((2, SCAN_SEG, ns), f32)],
        compiler_params=_params(("parallel", "arbitrary")),
    )(up, lam, bblk, cblk)


def _s5_bwd(u, dy, cprev, lam, bblk, cblk, bblk_t, cblk_t, n_x, reverse, name, add=None):
    n_tot, d = u.shape
    nq, n_blk, n_xb = d // 128, n_tot // BLK, n_x // BLK
    order = _block_order(n_xb, n_blk, reverse)
    ns = N_STATE
    blk_of = lambda step: order(n_blk - 1 - step)
    has_add = add is not None
    jof = (lambda p: SEG_LEN - 1 - p) if reverse else (lambda p: p)
    adj_seg_order = range(SCAN_SEG) if reverse else range(SCAN_SEG - 1, -1, -1)

    def body(*refs):
        u_ref, dy_ref, cp_ref, lam_ref, b_ref, c_ref, bt_ref, ct_ref = refs[:8]
        rest = refs[8:]
        if has_add:
            add_ref, rest = rest[0], rest[1:]
        du_ref, dlam_ref, db_ref, dc_ref, bu_ref, st_ref, ds_ref, acarry_ref = rest
        step_id = pl.program_id(1)
        is_x = blk_of(step_id) < n_xb

        @pl.when(step_id == 0)
        def _():
            acarry_ref[...] = jnp.zeros_like(acarry_ref)
            dlam_ref[...] = jnp.zeros_like(dlam_ref)
            db_ref[...] = jnp.zeros_like(db_ref)
            dc_ref[...] = jnp.zeros_like(dc_ref)

        up = u_ref[...]
        bu_ref[0] = jnp.dot(up, b_ref[0], preferred_element_type=f32)
        bu_ref[1] = jnp.dot(up, b_ref[1], preferred_element_type=f32)
        lr, li = lam_ref[0:1, :], lam_ref[1:2, :]
        lrb, lib = jnp.broadcast_to(lr, (SCAN_SEG, ns)), jnp.broadcast_to(li, (SCAN_SEG, ns))
        cpr, cpi = cp_ref[0], cp_ref[1]

        def fstep(p, s):
            sr, si = s
            off = pl.multiple_of(jof(p) * SCAN_SEG, SCAN_SEG)
            nsr = lrb * sr - lib * si + bu_ref[0, pl.ds(off, SCAN_SEG), :]
            nsi = lrb * si + lib * sr + bu_ref[1, pl.ds(off, SCAN_SEG), :]
            st_ref[0, pl.ds(off, SCAN_SEG), :] = nsr
            st_ref[1, pl.ds(off, SCAN_SEG), :] = nsi
            return nsr, nsi

        lax.fori_loop(0, SEG_LEN, fstep, (cpr, cpi))

        dyp = jnp.where(is_x, dy_ref[...], jnp.zeros_like(dy_ref))
        ds_ref[0] = jnp.dot(dyp, ct_ref[0], preferred_element_type=f32)
        ds_ref[1] = -jnp.dot(dyp, ct_ref[1], preferred_element_type=f32)

        def adj(p, a):
            ar, ai = a
            off = pl.multiple_of(jof(p) * SCAN_SEG, SCAN_SEG)
            nar = ds_ref[0, pl.ds(off, SCAN_SEG), :] + lrb * ar + lib * ai
            nai = ds_ref[1, pl.ds(off, SCAN_SEG), :] - lib * ar + lrb * ai
            return nar, nai

        zero = jnp.zeros((SCAN_SEG, ns), f32)
        er, ei = lax.fori_loop(0, SEG_LEN, lambda jj, a: adj(SEG_LEN - 1 - jj, a), (zero, zero))
        pr, pi = _lam_pow(lr, li)
        nr_, ni_ = acarry_ref[0, 0:1, :], acarry_ref[1, 0:1, :]
        rows_r, rows_i = [None] * SCAN_SEG, [None] * SCAN_SEG
        for k in adj_seg_order:
            rows_r[k], rows_i[k] = nr_, ni_
            nr_, ni_ = er[k:k + 1] + pr * nr_ + pi * ni_, ei[k:k + 1] + pr * ni_ - pi * nr_
        acarry_ref[0] = jnp.broadcast_to(nr_, (SCAN_SEG, ns))
        acarry_ref[1] = jnp.broadcast_to(ni_, (SCAN_SEG, ns))
        an_r, an_i = jnp.concatenate(rows_r, axis=0), jnp.concatenate(rows_i, axis=0)

        def adj2(jj, carry):
            ar, ai, glr, gli = carry
            p = SEG_LEN - 1 - jj
            nar, nai = adj(p, (ar, ai))
            off = pl.multiple_of(jof(p) * SCAN_SEG, SCAN_SEG)
            ds_ref[0, pl.ds(off, SCAN_SEG), :] = nar
            ds_ref[1, pl.ds(off, SCAN_SEG), :] = nai
            poff = pl.multiple_of(jof(p - 1) * SCAN_SEG, SCAN_SEG)
            spr, spi = st_ref[0, pl.ds(poff, SCAN_SEG), :], st_ref[1, pl.ds(poff, SCAN_SEG), :]
            return nar, nai, glr + nar * spr + nai * spi, gli - nar * spi + nai * spr

        ar, ai, glr, gli = lax.fori_loop(0, SEG_LEN - 1, adj2, (an_r, an_i, zero, zero))
        nar, nai = adj(0, (ar, ai))
        first = jof(0) * SCAN_SEG
        ds_ref[0, first:first + SCAN_SEG, :] = nar
        ds_ref[1, first:first + SCAN_SEG, :] = nai
        glr = glr + nar * cpr + nai * cpi
        gli = gli - nar * cpi + nai * cpr
        dlam_ref[0:1, :] += jnp.sum(glr, axis=0, keepdims=True)
        dlam_ref[1:2, :] += jnp.sum(gli, axis=0, keepdims=True)

        a_r, a_i = ds_ref[0].astype(bf16), ds_ref[1].astype(bf16)
        du = jnp.dot(a_r, bt_ref[0], preferred_element_type=f32) + jnp.dot(a_i, bt_ref[1],
                                                                        preferred_element_type=f32)
        if has_add:
            du = du + add_ref[...]
        du_ref[...] = du
        db_ref[0] += lax.dot_general(up, a_r, TN, preferred_element_type=f32)
        db_ref[1] += lax.dot_general(up, a_i, TN, preferred_element_type=f32)
        dc_ref[0] += lax.dot_general(st_ref[0].astype(bf16), dyp, TN, preferred_element_type=f32)
        dc_ref[1] -= lax.dot_general(st_ref[1].astype(bf16), dyp, TN, preferred_element_type=f32)

    tok = pl.BlockSpec((BLK, 128), lambda q, s: (blk_of(s), q))
    in_specs = [tok, pl.BlockSpec((BLK, 128), lambda q, s: (jnp.minimum(blk_of(s), n_xb - 1), q)),
                pl.BlockSpec((None, None, 2, SCAN_SEG, ns), lambda q, s: (q, n_blk - 1 - s, 0, 0, 0)),
                pl.BlockSpec((None, 2, ns), lambda q, s: (q, 0, 0)),
                pl.BlockSpec((None, 2, 128, ns), lambda q, s: (q, 0, 0, 0)),
                pl.BlockSpec((None, 2, ns, 128), lambda q, s: (q, 0, 0, 0)),
                pl.BlockSpec((None, 2, ns, 128), lambda q, s: (q, 0, 0, 0)),
                pl.BlockSpec((None, 2, 128, ns), lambda q, s: (q, 0, 0, 0))]
    args = [u, dy, cprev, lam, bblk, cblk, bblk_t, cblk_t]
    if has_add:
        in_specs.append(tok)
        args.append(add)
    return pl.pallas_call(
        body, name=name, grid=(nq, n_blk), in_specs=in_specs,
        out_specs=[tok, pl.BlockSpec((None, 2, ns), lambda q, s: (q, 0, 0)),
                   pl.BlockSpec((None, 2, 128, ns), lambda q, s: (q, 0, 0, 0)),
                   pl.BlockSpec((None, 2, ns, 128), lambda q, s: (q, 0, 0, 0))],
        out_shape=[jax.ShapeDtypeStruct((n_tot, d), f32), jax.ShapeDtypeStruct((nq, 2, ns), f32),
                   jax.ShapeDtypeStruct((nq, 2, 128, ns), f32), jax.ShapeDtypeStruct((nq, 2, ns, 128), f32)],
        scratch_shapes=[pltpu.VMEM((2, BLK, ns), f32), pltpu.VMEM((2, BLK, ns), f32),
                        pltpu.VMEM((2, BLK, ns), f32), pltpu.VMEM((2, SCAN_SEG, ns), f32)],
        compiler_params=_params(("parallel", "arbitrary")),
    )(*args)


MESH = pl.DeviceIdType.MESH
GROUP_SIZE = {"c": 2, "xy": 4, "xyc": 8}
MIN_CHUNK_BYTES = 512 * 1024


def _group(axes):
    x, y, c = lax.axis_index("x"), lax.axis_index("y"), lax.axis_index("c")
    if axes == "c":
        return c, lambda o: (x, y, c ^ 1)
    if axes == "xy":
        return 2 * x + y, lambda o: (x ^ (o >> 1), y ^ (o & 1), c)
    return 4 * x + 2 * y + c, lambda o: (x ^ (o >> 2), y ^ ((o >> 1) & 1), c ^ (o & 1))


def _exchange(src, axes, gather, name):
    n = GROUP_SIZE[axes]
    out_shape = (n,) + src.shape if gather else src.shape
    slab = out_shape[1:]
    slab_bytes = int(np.prod(slab)) * src.dtype.itemsize
    nch = next((k for k in (16, 8, 4, 2) if slab[0] % (16 * k) == 0 and slab_bytes // k >= MIN_CHUNK_BYTES), 1)
    step = slab[0] // nch

    def body(src_ref, out_ref, send_sems, recv_sems, local_sem):
        me, peer = _group(axes)
        local = pltpu.make_async_copy(src_ref if gather else src_ref.at[me], out_ref.at[me], local_sem)
        local.start()

        def copy(o, k, src_idx, dst_idx):
            rows = pl.ds(k * step, step)
            return pltpu.make_async_remote_copy(
                src_ref=(src_ref if gather else src_ref.at[src_idx]).at[rows], dst_ref=out_ref.at[dst_idx].at[rows],
                send_sem=send_sems.at[o - 1, k], recv_sem=recv_sems.at[o - 1, k],
                device_id=peer(o), device_id_type=MESH)

        sends = [copy(o, k, me ^ o, me) for k in range(nch) for o in range(1, n)]
        for cp in sends:
            cp.start()
        for k in range(nch):
            for o in range(1, n):
                copy(o, k, me, me ^ o).wait_recv()
        for cp in sends:
            cp.wait_send()
        local.wait()

    return pl.pallas_call(
        body, name=name, out_shape=jax.ShapeDtypeStruct(out_shape, src.dtype),
        in_specs=[pl.BlockSpec(memory_space=pl.ANY)], out_specs=pl.BlockSpec(memory_space=pl.ANY),
        scratch_shapes=[pltpu.SemaphoreType.DMA((n - 1, nch)), pltpu.SemaphoreType.DMA((n - 1, nch)),
                        pltpu.SemaphoreType.DMA(())],
        compiler_params=pltpu.CompilerParams(has_side_effects=True),
    )(src)


def _sum0(x, name):
    n, r, c = x.shape
    tr = _pick(r, (512, 256, 128, 64, 32, 16, 8))

    def body(x_ref, o_ref):
        acc = x_ref[0].astype(f32)
        for k in range(1, n):
            acc = acc + x_ref[k].astype(f32)
        o_ref[...] = acc

    return pl.pallas_call(
        body, name=name, grid=(r // tr,), in_specs=[pl.BlockSpec((n, tr, c), lambda i: (0, i, 0))],
        out_specs=pl.BlockSpec((tr, c), lambda i: (i, 0)), out_shape=jax.ShapeDtypeStruct((r, c), f32),
        compiler_params=_params(("parallel",)),
    )(x)


LANES = 1024


def _pack(parts, dtype):
    flat = jnp.concatenate([p.astype(dtype).reshape(-1) for p in parts])
    pad = (-flat.shape[0]) % (16 * LANES)
    if pad:
        flat = jnp.concatenate([flat, jnp.zeros((pad,), dtype)])
    return flat.reshape(-1, LANES)


def _unpack(flat, shapes):
    out, off = [], 0
    for s in shapes:
        size = int(np.prod(s))
        out.append(flat[off:off + size].reshape(s))
        off += size
    return out


def _vec(a):
    return a.reshape(-1, 1, a.shape[-1])


def _local_step(xa, tgt, mods, wts, sm, n_x):
    n_tot, d = xa.shape
    kw = dict(nrows=n_tot, n_x_rows=n_x)
    kx = dict(nrows=n_x, n_x_rows=n_x)
    mv = lambda l, k: mods[l, :, k][:, None, :]
    mx = lambda l, k: mods[l, 0:1, k][:, None, :]
    nmix, nffn = sm["norm_mix"], sm["norm_ffn"]
    cos, sin = _rope_tables(n_x, n_tot - n_x)
    cfg_a, cfg_b = _AttnCfg("na", n_x, n_tot), _AttnCfg("swa", n_x, n_tot)

    h1 = _rowwise(_f_normmod, [xa], [_vec(nmix[0:1]), mv(0, 1), mv(0, 0)], [(d, bf16)], name="l0_norm_mix", **kw)[0]
    qkv32 = _mm(h1, wts["w_in"], "nn", f32, "l0_in_proj")
    qkv = _qkv_post(qkv32, cos, sin, "l0_qkv_post")
    rpb2 = jnp.zeros((128, 128), f32).at[:NA_HEADS * 15, :31].set(sm["rpb"].reshape(NA_HEADS * 15, 31))
    basis = jnp.zeros((128, GRID_W * GRID_W), f32).at[:31].set(_toeplitz_basis())
    tz = _small_dot(rpb2, basis, "rpb_expand")[:NA_HEADS * 15].reshape(NA_HEADS, 15, GRID_W, GRID_W)
    bias = jnp.stack([tz[:, 3 - a:15 - a] for a in range(4)], axis=1).transpose(0, 1, 3, 2, 4).reshape(
        NA_HEADS, BLK, 3 * BLK)
    bias_t = jnp.stack([tz[:, a:a + 12][:, ::-1] for a in range(4)], axis=2).transpose(0, 1, 3, 2, 4).reshape(
        NA_HEADS, 3 * BLK, BLK)
    sink_rows = jnp.repeat(sm["sink"].reshape(NB_KV_HEADS, NB_GROUP, 1), BLK, axis=1).reshape(
        NB_KV_HEADS, NB_GROUP * BLK, 1)
    oa, lse_a = _attn_fwd(qkv, cfg_a, bias, "l0_na_fwd")
    ob, lse_b = _attn_fwd(qkv, cfg_b, sink_rows, "l0_swa_fwd")
    o = jnp.concatenate([oa, ob], axis=1)
    y1 = _mm(o, wts["w_out"], "nn", f32, "l0_out_proj")
    xb = _rowwise(_f_gated_add, [xa, y1], [mv(0, 2)], [(d, f32)], name="l0_res_mix", **kw)[0]

    def ffn_fwd(xin, l, vec_of, kk, tag):
        h = _rowwise(_f_normmod, [xin], [_vec(nffn[l:l + 1]), vec_of(l, 4), vec_of(l, 3)], [(d, bf16)],
                     name=tag + "_norm_ffn", **kk)[0]
        a, b, u = _ffn_up(h, wts["w1"][l], wts["w3"][l], tag + "_ffn_up")
        fo = _mm(u, wts["w2"][l], "nn", f32, tag + "_ffn_w2")
        xo = _rowwise(_f_gated_add, [xin, fo], [vec_of(l, 5)], [(d, f32)], name=tag + "_res_ffn", **kk)[0]
        return xo, (h, a, b, u, fo)

    def ffn_bwd(d_out, xin, saved, l, vec_of, kk, tag):
        h, a, b, u, fo = saved
        rows = kk["nrows"]
        dfo, dg2 = _rowwise_bwd(_f_gated_add, [xin, fo], [vec_of(l, 5)], [d_out], {1: bf16},
                                name=tag + "_res_ffn_bwd", **kk)
        dw2 = _mm(u[:rows], dfo, "tn", f32, tag + "_ffn_w2_dw")
        da, db = _ffn_down_bwd(dfo, wts["w2"][l], a, b, tag + "_ffn_down_bwd")
        dh = _mm(da, wts["w1"][l], "nt", f32, tag + "_ffn_w13_dx", a2=db, b2=wts["w3"][l])
        dw1 = _mm(h[:rows], da, "tn", f32, tag + "_ffn_w1_dw")
        dw3 = _mm(h[:rows], db, "tn", f32, tag + "_ffn_w3_dw")
        dxin, dnw, dsc, dsh = _rowwise_bwd(
            _f_normmod, [xin], [_vec(nffn[l:l + 1]), vec_of(l, 4), vec_of(l, 3)], [dh], {0: f32},
            name=tag + "_norm_ffn_bwd", residual=d_out, **kk)
        return dxin, dw1, dw3, dw2, dnw, dsc, dsh, dg2

    xc, ffn0 = ffn_fwd(xb, 0, mv, kw, "l0")

    hs = _rowwise(_f_normmod, [xc], [_vec(nmix[1:2]), mv(1, 1), mv(1, 0)], [(d, f32)], name="l1_norm_mix", **kw)[0]
    g2n = sm["a_re"].shape[1]
    nq = d // 128
    ar2, ai2 = sm["a_re"].reshape(2 * g2n, SSM_STATE), sm["a_im"].reshape(2 * g2n, SSM_STATE)
    ldt2 = sm["log_dt"].reshape(2 * g2n, 1)
    bt_re = sm["b_re"].transpose(3, 0, 1, 2).reshape(SSM_GROUP, 2 * g2n, SSM_STATE)
    bt_im = sm["b_im"].transpose(3, 0, 1, 2).reshape(SSM_GROUP, 2 * g2n, SSM_STATE)
    disc_in = [ar2, ai2, ldt2, bt_re, bt_im]
    lam_r, lam_i, bbar_r, bbar_i = _whole(
        _f_discretise, disc_in,
        [((2 * g2n, SSM_STATE), f32)] * 2 + [((SSM_GROUP, 2 * g2n, SSM_STATE), f32)] * 2, "s5_discretise")
    eye = jnp.eye(GROUPS_PER_CHUNK, dtype=f32)
    eye6 = eye[None, None, :, None, :, None]

    def blockdiag_b(bbar):
        t = bbar.reshape(SSM_GROUP, 2, nq, GROUPS_PER_CHUNK, SSM_STATE).transpose(1, 2, 3, 0, 4)
        return (t[:, :, :, :, None, :] * eye6).reshape(2, nq, 128, N_STATE)

    def blockdiag_c(cw):
        t = cw.reshape(2, nq, GROUPS_PER_CHUNK, SSM_GROUP, SSM_STATE).transpose(0, 1, 2, 4, 3)
        return (t[:, :, :, :, None, :] * eye6).reshape(2, nq, N_STATE, 128)

    lam = jnp.stack([lam_r.reshape(2, nq, N_STATE), lam_i.reshape(2, nq, N_STATE)], axis=2)
    bblk = jnp.stack([blockdiag_b(bbar_r), blockdiag_b(bbar_i)], axis=2)
    cblk = jnp.stack([blockdiag_c(sm["c_re"]), blockdiag_c(sm["c_im"])], axis=2)
    bblk16, cblk16 = bblk.astype(bf16), cblk.astype(bf16)
    bblk_t, cblk_t = bblk16.transpose(0, 1, 2, 4, 3), cblk16.transpose(0, 1, 2, 4, 3)
    pm, pmt = _scan_perm()
    hs_p = _row_perm(hs, pm, "l1_s5_perm", out_dtype=bf16, split=False)
    ys, cps = [], []
    for dr in range(2):
        yd, cp = _s5_fwd(hs_p, lam[dr], bblk16[dr], cblk16[dr], n_x, dr == 1, "l1_s5_fwd%d" % dr)
        ys.append(yd)
        cps.append(cp)
    dvec = sm["ssm_d"].reshape(1, d)
    gy = _glu_pre(hs, ys[0], ys[1], dvec, pmt, n_x, "l1_glu_pre")
    zv = _mm(gy, wts["w_glu_v"], "nn", f32, "l1_glu_val")
    zg = _mm(gy, wts["w_glu_g"], "nn", f32, "l1_glu_gate")
    bv, bg = sm["b_glu"][:d].reshape(1, 1, d), sm["b_glu"][d:].reshape(1, 1, d)
    xd = _rowwise(_f_glu_post, [xc, zv, zg], [mx(1, 2), bv, bg], [(d, f32)], name="l1_glu_post", **kx)[0]
    xe, ffn1 = ffn_fwd(xd, 1, mx, kx, "l1")

    d_xe, d_nfinal, loss_blk = _loss_head(xe, tgt, sm["norm_final"].reshape(1, d), nrows=n_x, name="loss_head")
    d_xd, dw1_1, dw3_1, dw2_1, dnffn1, dsc2_1, dsh2_1, dg2_1 = ffn_bwd(d_xe, xd, ffn1, 1, mx, kx, "l1")
    dzv, dzg, dg1_1, dbv, dbg = _rowwise_bwd(_f_glu_post, [xc, zv, zg], [mx(1, 2), bv, bg], [d_xd],
                                             {1: bf16, 2: bf16}, name="l1_glu_post_bwd", **kx)
    dgy = _mm(dzv, wts["w_glu_v"], "nt", f32, "l1_glu_dx", a2=dzg, b2=wts["w_glu_g"])
    dwglu_v = _mm(gy, dzv, "tn", f32, "l1_glu_val_dw")
    dwglu_g = _mm(gy, dzg, "tn", f32, "l1_glu_gate_dw")
    dy_p, du_skip, d_dvec = _glu_pre_bwd(hs, ys[0], ys[1], dvec, pm, pmt, dgy, n_x, "l1_glu_pre_bwd")
    du0, dlam0, db0, dc0 = _s5_bwd(hs_p, dy_p, cps[0], lam[0], bblk16[0], cblk16[0], bblk_t[0], cblk_t[0], n_x,
                                   False, "l1_s5_bwd0")
    du1, dlam1, db1, dc1 = _s5_bwd(hs_p, dy_p, cps[1], lam[1], bblk16[1], cblk16[1], bblk_t[1], cblk_t[1], n_x,
                                   True, "l1_s5_bwd1", add=du0)
    d_hs = _row_perm(du1, pmt, "l1_s5_unperm", out_dtype=f32, split=True, add=du_skip, add_rows=n_x)
    d_xc, dnmix1, dsc1_1, dsh1_1 = _rowwise_bwd(
        _f_normmod, [xc], [_vec(nmix[1:2]), mv(1, 1), mv(1, 0)], [d_hs], {0: f32},
        name="l1_norm_mix_bwd", residual=d_xd, residual_rows=n_x, **kw)
    dlam = jnp.stack([dlam0, dlam1])
    dlam_r, dlam_i = dlam[:, :, 0].reshape(2 * g2n, SSM_STATE), dlam[:, :, 1].reshape(2 * g2n, SSM_STATE)
    dbb = jnp.stack([db0, db1]).reshape(2, nq, 2, GROUPS_PER_CHUNK, SSM_GROUP, GROUPS_PER_CHUNK, SSM_STATE)
    eye7 = eye[None, None, None, :, None, :, None]
    dbbar = (dbb * eye7).sum(axis=5).transpose(2, 4, 0, 1, 3, 5).reshape(2, SSM_GROUP, 2 * g2n, SSM_STATE)
    dcc = jnp.stack([dc0, dc1]).reshape(2, nq, 2, GROUPS_PER_CHUNK, SSM_STATE, GROUPS_PER_CHUNK, SSM_GROUP)
    dcw = (dcc * eye7).sum(axis=5).transpose(2, 0, 1, 3, 5, 4).reshape(2, 2, g2n, SSM_GROUP, SSM_STATE)
    d_ar, d_ai, d_ldt, d_btr, d_bti = _whole_bwd(_f_discretise, disc_in, [dlam_r, dlam_i, dbbar[0], dbbar[1]],
                                                 "s5_discretise_bwd")
    to_b = lambda t: t.reshape(SSM_GROUP, 2, g2n, SSM_STATE).transpose(1, 2, 3, 0)

    d_xb, dw1_0, dw3_0, dw2_0, dnffn0, dsc2_0, dsh2_0, dg2_0 = ffn_bwd(d_xc, xb, ffn0, 0, mv, kw, "l0")
    dy1, dg1_0 = _rowwise_bwd(_f_gated_add, [xa, y1], [mv(0, 2)], [d_xb], {1: bf16}, name="l0_res_mix_bwd", **kw)
    d_o = _mm(dy1, wts["w_out"], "nt", bf16, "l0_out_proj_dx")
    dw_out = _mm(o, dy1, "tn", f32, "l0_out_proj_dw")
    dqa, delta_a, dbias = _attn_bwd_q(qkv, oa, d_o, lse_a, cfg_a, bias, "l0_na_bwd_q")
    dka, dva = _attn_bwd_kv(qkv, d_o, lse_a, delta_a, cfg_a, bias_t, "l0_na_bwd_kv")
    dqb, delta_b, dsink_rows = _attn_bwd_q(qkv, ob, d_o, lse_b, cfg_b, sink_rows, "l0_swa_bwd_q")
    dkb, dvb = _attn_bwd_kv(qkv, d_o, lse_b, delta_b, cfg_b, None, "l0_swa_bwd_kv")
    d_qkv = _qkv_post_bwd([dqa, dka, dva, dqb, dkb, dvb], cos, sin, "l0_qkv_post_bwd")
    dh1 = _mm(d_qkv, wts["w_in"], "nt", f32, "l0_in_proj_dx")
    dw_in = _mm(h1, d_qkv, "tn", f32, "l0_in_proj_dw")
    d_xa, dnmix0, dsc1_0, dsh1_0 = _rowwise_bwd(
        _f_normmod, [xa], [_vec(nmix[0:1]), mv(0, 1), mv(0, 0)], [dh1], {0: f32},
        name="l0_norm_mix_bwd", residual=d_xb, **kw)
    dbias5 = dbias.reshape(NA_HEADS, 4, GRID_W, 12, GRID_W).transpose(0, 1, 3, 2, 4)
    dtz = sum(jnp.pad(dbias5[:, a], ((0, 0), (3 - a, a), (0, 0), (0, 0))) for a in range(4))
    dtz2 = jnp.zeros((128, GRID_W * GRID_W), f32).at[:NA_HEADS * 15].set(dtz.reshape(NA_HEADS * 15, -1))
    d_rpb = _small_dot(dtz2, basis.T, "rpb_expand_bwd")[:NA_HEADS * 15, :31].reshape(sm["rpb"].shape)
    d_sink = dsink_rows.reshape(NB_KV_HEADS * NB_GROUP, BLK).sum(axis=1)

    zero_c = jnp.zeros((1, 1, d), f32)
    both = lambda gx: jnp.concatenate([gx, zero_c], axis=0)
    dmods = jnp.stack([
        jnp.stack([dsh1_0, dsc1_0, dg1_0, dsh2_0, dsc2_0, dg2_0], axis=2),
        jnp.stack([dsh1_1, dsc1_1, both(dg1_1), both(dsh2_1), both(dsc2_1), both(dg2_1)], axis=2),
    ])[:, :, 0]
    big = dict(w1=jnp.stack([dw1_0, dw1_1]), w3=jnp.stack([dw3_0, dw3_1]), w2=jnp.stack([dw2_0, dw2_1]),
               w_in=dw_in, w_out=dw_out, w_glu=jnp.concatenate([dwglu_v, dwglu_g], axis=1))
    small = dict(
        norm_mix=jnp.concatenate([dnmix0[0], dnmix1[0]]), norm_ffn=jnp.concatenate([dnffn0[0], dnffn1[0]]),
        rpb=d_rpb, sink=d_sink, a_re=d_ar.reshape(sm["a_re"].shape), a_im=d_ai.reshape(sm["a_im"].shape),
        log_dt=d_ldt.reshape(sm["log_dt"].shape), b_re=to_b(d_btr), b_im=to_b(d_bti), c_re=dcw[0], c_im=dcw[1],
        ssm_d=d_dvec.reshape(d), b_glu=jnp.concatenate([dbv.reshape(d), dbg.reshape(d)]),
        norm_final=d_nfinal.reshape(d))
    return loss_blk[0, 0], d_xa, big, small, dmods


WEIGHTS = ("c_ctx", "ada_w", "ada_b", "norm_mix", "norm_ffn", "ffn_w1", "ffn_w3", "ffn_w2", "attn_w_in",
           "attn_w_out", "attn_rpb", "attn_sink", "ssm_a_re", "ssm_a_im", "ssm_log_dt", "ssm_b_re", "ssm_b_im",
           "ssm_c_re", "ssm_c_im", "ssm_d", "ssm_w_glu", "ssm_b_glu", "norm_final")
SHARDED_BIG = ("ffn_w1", "ffn_w3", "ffn_w2", "attn_w_in", "attn_w_out", "ssm_w_glu")
BIG = ("ada_w",) + SHARDED_BIG
SMALL = tuple(n for n in WEIGHTS if n not in BIG)


def _blocks_to_full(blk, kind):
    if kind == "cols2":
        t = blk.transpose(1, 2, 0, 3)
        return t.reshape(t.shape[0], t.shape[1], -1)
    if kind == "rows2":
        t = blk.transpose(1, 0, 2, 3)
        return t.reshape(t.shape[0], -1, t.shape[3])
    raise ValueError(kind)


def _full_to_blocks(full, kind):
    if kind == "cols2":
        l, k, n = full.shape
        return full.reshape(l, k, N_DEV, n // N_DEV).transpose(2, 0, 1, 3)
    l, n, k = full.shape
    return full.reshape(l, N_DEV, n // N_DEV, k).transpose(1, 0, 2, 3)


BIG_KIND = dict(ffn_w1="cols2", ffn_w3="cols2", ffn_w2="rows2", attn_w_in="cols2", attn_w_out="rows2",
                ssm_w_glu="cols2")


def kernel(x, c, ctx, c_ctx, ada_w, ada_b, norm_mix, norm_ffn, ffn_w1, ffn_w3, ffn_w2, attn_w_in, attn_w_out, attn_rpb, attn_sink, ssm_a_re, ssm_a_im, ssm_log_dt, ssm_b_re, ssm_b_im, ssm_c_re, ssm_c_im, ssm_d, ssm_w_glu, ssm_b_glu, norm_final, loss_target, m_c_ctx, m_ada_w, m_ada_b, m_norm_mix, m_norm_ffn, m_ffn_w1, m_ffn_w3, m_ffn_w2, m_attn_w_in, m_attn_w_out, m_attn_rpb, m_attn_sink, m_ssm_a_re, m_ssm_a_im, m_ssm_log_dt, m_ssm_b_re, m_ssm_b_im, m_ssm_c_re, m_ssm_c_im, m_ssm_d, m_ssm_w_glu, m_ssm_b_glu, m_norm_final, v_c_ctx, v_ada_w, v_ada_b, v_norm_mix, v_norm_ffn, v_ffn_w1, v_ffn_w3, v_ffn_w2, v_attn_w_in, v_attn_w_out, v_attn_rpb, v_attn_sink, v_ssm_a_re, v_ssm_a_im, v_ssm_log_dt, v_ssm_b_re, v_ssm_b_im, v_ssm_c_re, v_ssm_c_im, v_ssm_d, v_ssm_w_glu, v_ssm_b_glu, v_norm_final):
    p = dict(locals())
    w = {n: p[n] for n in WEIGHTS}
    me = 4 * lax.axis_index("x") + 2 * lax.axis_index("y") + lax.axis_index("c")
    n_x, d = x.shape[1], x.shape[2]
    cols = ada_w.shape[2]
    d8 = d // N_DEV

    first = jnp.concatenate([c[0], ssm_d[0], ssm_b_glu[0]])[None]
    g0 = _exchange(first, "xyc", True, "gather_vectors")[:, 0]
    c_all, d_full, bglu_full = g0[:, :d], g0[:, d:d + d8].reshape(d), g0[:, d + d8:].reshape(2 * d)
    cc = jnp.concatenate([c_all, c_ctx[None], jnp.zeros((16 - N_DEV - 1, d), f32)])
    sc_all = _whole(_f_silu, [cc], [((16, d), f32)], "silu_c")[0]
    my_cols = lambda a: lax.dynamic_slice_in_dim(a, me * cols, cols, axis=a.ndim - 1)
    mod_loc = jnp.stack([_mm(sc_all, ada_w[l], "nn", f32, "ada_mod%d" % l) for l in range(2)])
    mod_loc = mod_loc + my_cols(ada_b)[:, None, :]
    mg = _exchange(mod_loc.reshape(32, cols), "xyc", True, "gather_mod")
    mod_all = mg.reshape(N_DEV, 2, 16, cols).transpose(1, 2, 0, 3).reshape(2, 16, N_DEV * cols)
    mod_x = lax.dynamic_index_in_dim(mod_all, me, axis=1, keepdims=False)
    mods = jnp.stack([mod_x, mod_all[:, N_DEV]], axis=1).reshape(2, 2, 6, d)

    full = {}
    for n in SHARDED_BIG:
        blk = _exchange(w[n].astype(bf16).reshape(-1, w[n].shape[-1]), "xyc", True, "gather_" + n)
        full[n] = _blocks_to_full(blk.reshape((N_DEV,) + w[n].shape), BIG_KIND[n])
    wts = dict(w1=[full["ffn_w1"][0], full["ffn_w1"][1]], w3=[full["ffn_w3"][0], full["ffn_w3"][1]],
               w2=[full["ffn_w2"][0], full["ffn_w2"][1]], w_in=full["attn_w_in"][0], w_out=full["attn_w_out"][0],
               w_glu_v=full["ssm_w_glu"][0][:, :d], w_glu_g=full["ssm_w_glu"][0][:, d:])
    sm = dict(norm_mix=norm_mix, norm_ffn=norm_ffn, rpb=attn_rpb[0], sink=attn_sink[0], a_re=ssm_a_re[0],
              a_im=ssm_a_im[0], log_dt=ssm_log_dt[0], b_re=ssm_b_re[0], b_im=ssm_b_im[0], c_re=ssm_c_re[0],
              c_im=ssm_c_im[0], ssm_d=d_full, b_glu=bglu_full, norm_final=norm_final)

    xa = jnp.concatenate([x[0], ctx[0]], axis=0)
    loss_part, d_xa, big, small, dmods = _local_step(xa, loss_target[0], mods, wts, sm, n_x)

    gfull = dict(ffn_w1=big["w1"], ffn_w3=big["w3"], ffn_w2=big["w2"], attn_w_in=big["w_in"][None],
                 attn_w_out=big["w_out"][None], ssm_w_glu=big["w_glu"][None])
    grads = {}
    for n in SHARDED_BIG:
        blk = _full_to_blocks(gfull[n], BIG_KIND[n]).astype(bf16).reshape(N_DEV, -1, w[n].shape[-1])
        grads[n] = _sum0(_exchange(blk, "xyc", False, "scatter_" + n), "sum_" + n).reshape(w[n].shape)

    small_names = ("norm_mix", "norm_ffn", "rpb", "sink", "a_re", "a_im", "log_dt", "b_re", "b_im", "c_re", "c_im",
                   "ssm_d", "b_glu", "norm_final")
    parts = [loss_part.reshape(1)] + [small[n] for n in small_names] + [dmods[:, 0].reshape(2, 6 * d),
                                                                        dmods[:, 1].reshape(2, 6 * d)]
    shapes = [q.shape for q in parts]
    sg = _exchange(_pack(parts, f32), "xyc", True, "gather_small")
    tot = _unpack(_sum0(sg, "sum_small").reshape(-1), shapes)
    loss = tot[0][0]
    ts = dict(zip(small_names, tot[1:1 + len(small_names)]))
    tot_dmod_x, tot_dmod_c = tot[-2], tot[-1]
    off_x = sum(int(np.prod(s)) for s in shapes[:-2])
    dmod_x_all = sg.reshape(N_DEV, -1)[:, off_x:off_x + 12 * d].reshape(N_DEV, 2, 6 * d)
    dm = jnp.concatenate([dmod_x_all.transpose(1, 0, 2), tot_dmod_c[:, None, :],
                          jnp.zeros((2, 16 - N_DEV - 1, 6 * d), f32)], axis=1)
    dm_loc = my_cols(dm)
    grads["ada_w"] = jnp.stack([_mm(sc_all, dm_loc[l], "tn", f32, "ada_w_grad%d" % l) for l in range(2)])
    grads["ada_b"] = tot_dmod_x + tot_dmod_c
    dsc_part = _mm(dm_loc[0], ada_w[0], "nt", f32, "silu_c_grad", a2=dm_loc[1], b2=ada_w[1])[N_DEV:N_DEV + 1]
    dsc = _sum0(_exchange(dsc_part, "xyc", True, "gather_cctx"), "sum_cctx")
    grads["c_ctx"] = _whole_bwd(_f_silu, [c_ctx[None]], [dsc], "silu_cctx_bwd")[0][0]
    grads.update(norm_mix=ts["norm_mix"], norm_ffn=ts["norm_ffn"], attn_rpb=ts["rpb"][None],
                 attn_sink=ts["sink"][None], ssm_a_re=ts["a_re"][None], ssm_a_im=ts["a_im"][None],
                 ssm_log_dt=ts["log_dt"][None], ssm_b_re=ts["b_re"][None], ssm_b_im=ts["b_im"][None],
                 ssm_c_re=ts["c_re"][None], ssm_c_im=ts["c_im"][None],
                 ssm_d=lax.dynamic_slice_in_dim(ts["ssm_d"], me * d8, d8)[None],
                 ssm_b_glu=lax.dynamic_slice_in_dim(ts["b_glu"], me * 2 * d8, 2 * d8)[None],
                 norm_final=ts["norm_final"])

    delta, new_m, new_v = {}, {}, {}
    for n in BIG:
        two_d = lambda a: a.reshape(-1, a.shape[-1])
        dl, m2, v2 = _adamw(two_d(w[n]), two_d(grads[n]), two_d(p["m_" + n]), two_d(p["v_" + n]), "adamw_" + n)
        delta[n], new_m[n], new_v[n] = (t.reshape(w[n].shape) for t in (dl, m2, v2))
    sshapes = [w[n].shape for n in SMALL]
    packs = [_pack([src[n] for n in SMALL], f32) for src in
             (w, grads, {n: p["m_" + n] for n in SMALL}, {n: p["v_" + n] for n in SMALL})]
    for store, t in zip((delta, new_m, new_v), _adamw(*packs, "adamw_small")):
        store.update(zip(SMALL, _unpack(t.reshape(-1), sshapes)))

    return (loss, d_xa[:n_x][None], *[grads[n] for n in WEIGHTS], *[delta[n] for n in WEIGHTS],
            *[new_m[n] for n in WEIGHTS], *[new_v[n] for n in WEIGHTS])
```

```python
import functools
import math

import numpy as np
import jax
import jax.numpy as jnp
from jax import lax
from jax.experimental import pallas as pl
from jax.experimental.pallas import tpu as pltpu

f32, bf16 = jnp.float32, jnp.bfloat16

HEAD_DIM = 128
GRID_W = 64
NA_HEADS = 8
NB_Q_HEADS = 8
NB_KV_HEADS = 2
NB_GROUP = NB_Q_HEADS // NB_KV_HEADS
NA_ROWS = 8
NA_COLS = 16
SW_RADIUS = 128
ROPE_BASE = 10000.0
SSM_GROUP = 16
SSM_STATE = 64
EPS = 1e-6
NEG_INF = -1e30
ADAM_LR, ADAM_B1, ADAM_B2, ADAM_EPS, ADAM_WD, ADAM_STEP = 0.001, 0.9, 0.999, 1e-08, 0.01, 10

N_DEV = 8
BLK = 256
SCAN_SEG = 8
GROUPS_PER_CHUNK = 8
V7X_VMEM_LIMIT = 56 * 2 ** 20
MM_VMEM_BUDGET = 36 * 2 ** 20
MXU_FLOPS_PER_US = 7e8
CARRIER_US = dict(na_fwd=400, swa_fwd=250, na_bwd_q=430, na_bwd_kv=640, swa_bwd_q=220, swa_bwd_kv=320,
                  s5_fwd=600, s5_bwd=1200)

NT = (((1,), (1,)), ((), ()))
TN = (((0,), (0,)), ((), ()))


def _params(sem, side_effects=False):
    return pltpu.CompilerParams(dimension_semantics=sem, vmem_limit_bytes=V7X_VMEM_LIMIT,
                                has_side_effects=side_effects)


def _pick(n, cands):
    for c in cands:
        if n % c == 0:
            return c
    return n


MESH = pl.DeviceIdType.MESH
MIN_CHUNK_BYTES = 512 * 1024
EXCHANGE_BYTES_PER_US = 74e3
RIDER_OVERHANG = 1.5


def _peers():
    x, y, c = lax.axis_index("x"), lax.axis_index("y"), lax.axis_index("c")
    return 4 * x + 2 * y + c, lambda o: (x ^ (o >> 2), y ^ ((o >> 1) & 1), c ^ (o & 1))


class _Exchange:
    def __init__(self, src, gather):
        self.src, self.gather, self.result = src, gather, None
        self.out_shape = (N_DEV,) + src.shape if gather else src.shape
        slab = self.out_shape[1:]
        slab_bytes = int(np.prod(slab)) * src.dtype.itemsize
        self.nch = next((k for k in (4, 2) if slab[0] % (16 * k) == 0 and slab_bytes // k >= MIN_CHUNK_BYTES), 1)
        self.step = slab[0] // self.nch
        self.cost_us = (N_DEV - 1) * slab_bytes / EXCHANGE_BYTES_PER_US

    def scratch(self):
        return [pltpu.SemaphoreType.DMA((N_DEV - 1, self.nch)), pltpu.SemaphoreType.DMA((N_DEV - 1, self.nch)),
                pltpu.SemaphoreType.DMA(())]

    def _copies(self, src_ref, out_ref, send_sems, recv_sems, local_sem):
        me, peer = _peers()
        local = pltpu.make_async_copy(src_ref if self.gather else src_ref.at[me], out_ref.at[me], local_sem)

        def copy(o, k, src_idx, dst_idx):
            rows = pl.ds(k * self.step, self.step)
            return pltpu.make_async_remote_copy(
                src_ref=(src_ref if self.gather else src_ref.at[src_idx]).at[rows],
                dst_ref=out_ref.at[dst_idx].at[rows],
                send_sem=send_sems.at[o - 1, k], recv_sem=recv_sems.at[o - 1, k],
                device_id=peer(o), device_id_type=MESH)

        pairs = [(o, k) for k in range(self.nch) for o in range(1, N_DEV)]
        sends = [copy(o, k, me ^ o, me) for o, k in pairs]
        recvs = [copy(o, k, me, me ^ o) for o, k in pairs]
        return local, sends, recvs

    def start(self, *refs):
        local, sends, _ = self._copies(*refs)
        local.start()
        for cp in sends:
            cp.start()

    def wait(self, *refs):
        local, sends, recvs = self._copies(*refs)
        for cp in recvs:
            cp.wait_recv()
        for cp in sends:
            cp.wait_send()
        local.wait()


def _run_exchange(ex, name):
    def body(src_ref, out_ref, *sems):
        ex.start(src_ref, out_ref, *sems)
        ex.wait(src_ref, out_ref, *sems)

    ex.result = pl.pallas_call(
        body, name=name, out_shape=jax.ShapeDtypeStruct(ex.out_shape, ex.src.dtype),
        in_specs=[pl.BlockSpec(memory_space=pl.ANY)], out_specs=pl.BlockSpec(memory_space=pl.ANY),
        scratch_shapes=ex.scratch(), compiler_params=pltpu.CompilerParams(has_side_effects=True),
    )(ex.src)
    return ex.result


def _call(body, args, *, name, grid, in_specs, out_specs, out_shape, scratch_shapes=(), sem, rider=None):
    if not rider:
        return pl.pallas_call(body, name=name, grid=grid, in_specs=in_specs, out_specs=out_specs,
                              out_shape=out_shape, scratch_shapes=list(scratch_shapes),
                              compiler_params=_params(sem))(*args)
    n_in, n_out, n_sc, n_ex = len(in_specs), len(out_specs), len(scratch_shapes), len(rider)

    def wrapped(*refs):
        ins, ex_src = refs[:n_in], refs[n_in:n_in + n_ex]
        outs = refs[n_in + n_ex:n_in + n_ex + n_out]
        ex_out = refs[n_in + n_ex + n_out:n_in + 2 * n_ex + n_out]
        scr = refs[n_in + 2 * n_ex + n_out:n_in + 2 * n_ex + n_out + n_sc]
        sems = refs[n_in + 2 * n_ex + n_out + n_sc:]
        ids = [pl.program_id(a) for a in range(len(grid))]
        first = functools.reduce(jnp.logical_and, [i == 0 for i in ids])
        last = functools.reduce(jnp.logical_and, [i == g - 1 for i, g in zip(ids, grid)])

        @pl.when(first)
        def _():
            for e, ex in enumerate(rider):
                ex.start(ex_src[e], ex_out[e], *sems[3 * e:3 * e + 3])

        body(*ins, *outs, *scr)

        @pl.when(last)
        def _():
            for e, ex in enumerate(rider):
                ex.wait(ex_src[e], ex_out[e], *sems[3 * e:3 * e + 3])

    anyspace = pl.BlockSpec(memory_space=pl.ANY)
    res = pl.pallas_call(
        wrapped, name=name, grid=grid, in_specs=list(in_specs) + [anyspace] * n_ex,
        out_specs=list(out_specs) + [anyspace] * n_ex,
        out_shape=list(out_shape) + [jax.ShapeDtypeStruct(ex.out_shape, ex.src.dtype) for ex in rider],
        scratch_shapes=list(scratch_shapes) + [s for ex in rider for s in ex.scratch()],
        compiler_params=_params(("arbitrary",) * len(grid), side_effects=True),
    )(*args, *[ex.src for ex in rider])
    for ex, r in zip(rider, res[n_out:]):
        ex.result = r
    return list(res[:n_out])


class _Comm:
    def __init__(self):
        self.queue, self.count = [], 0

    def post(self, src, gather):
        ex = _Exchange(src, gather)
        self.queue.append(ex)
        return ex

    def take(self, budget_us):
        rider = []
        for ex in list(self.queue):
            if budget_us > 0 and ex.cost_us <= RIDER_OVERHANG * budget_us:
                budget_us -= ex.cost_us
                rider.append(ex)
                self.queue.remove(ex)
        return rider

    def finish(self, ex):
        while ex.result is None:
            self.count += 1
            _run_exchange(self.queue.pop(0), "exchange%d" % self.count)
        return ex.result


def _mm(a, b, mode, out_dtype, name, a2=None, b2=None, comm=None):
    if mode == "tn":
        kdim, m, n = a.shape[0], a.shape[1], b.shape[1]
    elif mode == "nt":
        m, kdim, n = a.shape[0], a.shape[1], b.shape[0]
    else:
        m, kdim, n = a.shape[0], a.shape[1], b.shape[1]
    if mode == "tn":
        tm = _pick(m, (2048, 1024, 512, 256, 128))
        tk = _pick(kdim, (1024, 768, 512, 256, 128))
    else:
        tm = _pick(m, (1024, 768, 512, 256, 128))
        tk = _pick(kdim, (2048, 1024, 768, 512, 256, 128))
    in_bytes = max(a.dtype.itemsize, b.dtype.itemsize) * (1 if a2 is None else 2)
    out_bytes = jnp.dtype(out_dtype).itemsize

    def vmem(tn_):
        return 2 * in_bytes * tk * (tm + tn_) + (4 + 2 * out_bytes) * tm * tn_

    tn = next((t for t in (2048, 1024, 512, 256, 128) if n % t == 0 and vmem(t) <= MM_VMEM_BUDGET), n)
    nk = kdim // tk
    if mode == "tn":
        a_spec = pl.BlockSpec((tk, tm), lambda i, j, k: (k, i))
        b_spec = pl.BlockSpec((tk, tn), lambda i, j, k: (k, j))
        dims = TN
    elif mode == "nt":
        a_spec = pl.BlockSpec((tm, tk), lambda i, j, k: (i, k))
        b_spec = pl.BlockSpec((tn, tk), lambda i, j, k: (j, k))
        dims = NT
    else:
        a_spec = pl.BlockSpec((tm, tk), lambda i, j, k: (i, k))
        b_spec = pl.BlockSpec((tk, tn), lambda i, j, k: (k, j))
        dims = (((1,), (0,)), ((), ()))

    pairs = 1 if a2 is None else 2

    def body(*refs):
        o_ref, acc_ref = refs[2 * pairs:]
        k = pl.program_id(2)

        @pl.when(k == 0)
        def _():
            acc_ref[...] = jnp.zeros_like(acc_ref)

        for p in range(pairs):
            acc_ref[...] += lax.dot_general(refs[2 * p][...].astype(bf16), refs[2 * p + 1][...].astype(bf16), dims,
                                            preferred_element_type=f32)

        @pl.when(k == nk - 1)
        def _():
            o_ref[...] = acc_ref[...].astype(o_ref.dtype)

    args = (a, b) if a2 is None else (a, b, a2, b2)
    rider = comm.take(2.0 * pairs * m * n * kdim / MXU_FLOPS_PER_US) if comm else None
    return _call(
        body, args, name=name, grid=(m // tm, n // tn, nk),
        in_specs=[a_spec, b_spec] * pairs, out_specs=[pl.BlockSpec((tm, tn), lambda i, j, k: (i, j))],
        out_shape=[jax.ShapeDtypeStruct((m, n), out_dtype)],
        scratch_shapes=[pltpu.VMEM((tm, tn), f32)], sem=("parallel", "parallel", "arbitrary"), rider=rider)[0]


def _ffn_up(h, w1, w3, name, comm=None):
    m, kdim = h.shape
    n = w1.shape[1]
    tm, tn = _pick(m, (1024, 768, 512, 256, 128)), _pick(n, (512, 256, 128))

    def body(h_ref, w1_ref, w3_ref, a_ref, b_ref, u_ref):
        hv = h_ref[...]
        a = jnp.dot(hv, w1_ref[...], preferred_element_type=f32).astype(bf16)
        b = jnp.dot(hv, w3_ref[...], preferred_element_type=f32).astype(bf16)
        a_ref[...] = a
        b_ref[...] = b
        u_ref[...] = (jax.nn.silu(a.astype(f32)) * b.astype(f32)).astype(bf16)

    wspec = pl.BlockSpec((kdim, tn), lambda i, j: (0, j))
    ospec = pl.BlockSpec((tm, tn), lambda i, j: (i, j))
    rider = comm.take(4.0 * m * n * kdim / MXU_FLOPS_PER_US) if comm else None
    return _call(
        body, (h, w1, w3), name=name, grid=(m // tm, n // tn),
        in_specs=[pl.BlockSpec((tm, kdim), lambda i, j: (i, 0)), wspec, wspec], out_specs=[ospec] * 3,
        out_shape=[jax.ShapeDtypeStruct((m, n), bf16)] * 3, sem=("parallel", "parallel"), rider=rider)


def _ffn_down_bwd(g, w2, a, b, name, comm=None):
    m, kdim = g.shape
    n = w2.shape[0]
    tm, tn = _pick(m, (1024, 768, 512, 256, 128)), _pick(n, (512, 256, 128))

    def body(g_ref, w_ref, a_ref, b_ref, da_ref, db_ref):
        du = lax.dot_general(g_ref[...], w_ref[...], NT, preferred_element_type=f32)
        _, vjp = jax.vjp(lambda p, q: jax.nn.silu(p) * q, a_ref[...].astype(f32), b_ref[...].astype(f32))
        da, db = vjp(du)
        da_ref[...] = da.astype(bf16)
        db_ref[...] = db.astype(bf16)

    ospec = pl.BlockSpec((tm, tn), lambda i, j: (i, j))
    rider = comm.take(2.0 * m * n * kdim / MXU_FLOPS_PER_US) if comm else None
    return _call(
        body, (g, w2, a, b), name=name, grid=(m // tm, n // tn),
        in_specs=[pl.BlockSpec((tm, kdim), lambda i, j: (i, 0)), pl.BlockSpec((tn, kdim), lambda i, j: (j, 0)),
                  ospec, ospec],
        out_specs=[ospec] * 2, out_shape=[jax.ShapeDtypeStruct((m, n), bf16)] * 2,
        sem=("parallel", "parallel"), rider=rider)


def _small_dot(a, b, name):
    def body(a_ref, b_ref, o_ref):
        o_ref[...] = jnp.dot(a_ref[...], b_ref[...], precision=lax.Precision.HIGHEST, preferred_element_type=f32)

    return pl.pallas_call(body, name=name, out_shape=jax.ShapeDtypeStruct((a.shape[0], b.shape[1]), f32))(a, b)


def _group_of(i, n_x_tiles, n_groups):
    return jnp.where(i >= n_x_tiles, n_groups - 1, 0)


def _rowwise(f, rows, vecs, outs, *, nrows, n_x_rows, name, tm=BLK, tc=None):
    n_x_tiles = n_x_rows // tm
    grid = (nrows // tm,) if tc is None else (nrows // tm, rows[0].shape[1] // tc)

    def rspec(cols):
        if tc is None:
            return pl.BlockSpec((tm, cols), lambda i: (i, 0))
        return pl.BlockSpec((tm, tc), lambda i, j: (i, j))

    def vspec(v):
        g = v.shape[0]
        if tc is None:
            return pl.BlockSpec((None, 1, v.shape[2]), lambda i: (_group_of(i, n_x_tiles, g), 0, 0))
        return pl.BlockSpec((None, 1, tc), lambda i, j: (_group_of(i, n_x_tiles, g), 0, j))

    nr, nv = len(rows), len(vecs)

    def body(*refs):
        ins = [r[...] for r in refs[:nr + nv]]
        res = f(*ins)
        for o_ref, val in zip(refs[nr + nv:], res):
            o_ref[...] = val.astype(o_ref.dtype)

    return pl.pallas_call(
        body, name=name, grid=grid,
        in_specs=[rspec(r.shape[1]) for r in rows] + [vspec(v) for v in vecs],
        out_specs=[rspec(c) for c, _ in outs],
        out_shape=[jax.ShapeDtypeStruct((nrows, c), d) for c, d in outs],
        compiler_params=_params(("parallel",) * len(grid)),
    )(*rows, *vecs)


def _rowwise_bwd(f, rows, vecs, cts, row_grads, *, nrows, n_x_rows, name, tm=BLK, tc=None, residual=None,
                 residual_rows=None):
    n_x_tiles = n_x_rows // tm
    n_tiles = nrows // tm
    res_tiles = None if residual_rows is None else residual_rows // tm
    grid = (n_tiles,) if tc is None else (rows[0].shape[1] // tc, n_tiles)
    row_of = (lambda *g: g[0]) if tc is None else (lambda *g: g[1])

    def rspec(cols):
        if tc is None:
            return pl.BlockSpec((tm, cols), lambda i: (i, 0))
        return pl.BlockSpec((tm, tc), lambda j, i: (i, j))

    def vspec(v):
        g = v.shape[0]
        if tc is None:
            return pl.BlockSpec((None, 1, v.shape[2]), lambda i: (_group_of(i, n_x_tiles, g), 0, 0))
        return pl.BlockSpec((None, 1, tc), lambda j, i: (_group_of(i, n_x_tiles, g), 0, j))

    nr, nv, nc = len(rows), len(vecs), len(cts)
    gidx = sorted(row_grads)

    def body(*refs):
        i = row_of(*[pl.program_id(d) for d in range(len(grid))])
        row_vals = [r[...].astype(f32) for r in refs[:nr]]
        vec_vals = [jnp.broadcast_to(r[...].astype(f32), (tm, r.shape[-1])) for r in refs[nr:nr + nv]]
        ct_vals = [r[...].astype(f32) for r in refs[nr + nv:nr + nv + nc]]
        out_refs = refs[nr + nv + nc + (residual is not None):]
        res, vjp = jax.vjp(lambda *a: tuple(o.astype(f32) for o in f(*a)), *row_vals, *vec_vals)
        grads = list(vjp(tuple(ct_vals)))
        if residual is not None:
            extra = refs[nr + nv + nc][...]
            if res_tiles is not None:
                extra = jnp.where(i < res_tiles, extra, 0.0)
            grads[0] = grads[0] + extra
        for o_ref, k in zip(out_refs[:len(gidx)], gidx):
            o_ref[...] = grads[k].astype(o_ref.dtype)
        for o_ref, g, v in zip(out_refs[len(gidx):], grads[nr:], vecs):
            part = jnp.sum(g, axis=0, keepdims=True)
            first = (i == 0) if v.shape[0] == 1 else ((i == 0) | (i == n_x_tiles))

            @pl.when(first)
            def _():
                o_ref[...] = part

            @pl.when(jnp.logical_not(first))
            def _():
                o_ref[...] += part

    sem = ("arbitrary",) if tc is None else ("parallel", "arbitrary")
    res_specs, res_args = [], []
    if residual is not None:
        assert tc is None
        clamp = (lambda i: i) if res_tiles is None else (lambda i: jnp.minimum(i, res_tiles - 1))
        res_specs = [pl.BlockSpec((tm, residual.shape[1]), lambda i: (clamp(i), 0))]
        res_args = [residual]
    return pl.pallas_call(
        body, name=name, grid=grid,
        in_specs=[rspec(r.shape[1]) for r in rows] + [vspec(v) for v in vecs] + [rspec(c.shape[1]) for c in cts]
        + res_specs,
        out_specs=[rspec(rows[k].shape[1]) for k in gidx] + [vspec(v) for v in vecs],
        out_shape=[jax.ShapeDtypeStruct((nrows, rows[k].shape[1]), row_grads[k]) for k in gidx]
        + [jax.ShapeDtypeStruct(v.shape, f32) for v in vecs],
        compiler_params=_params(sem),
    )(*rows, *vecs, *cts, *res_args)


def _rms(x, w):
    return x * lax.rsqrt(jnp.mean(x * x, axis=-1, keepdims=True) + EPS) * w


def _f_normmod(x, w, sc, sh):
    return (_rms(x.astype(f32), w) * (1.0 + sc) + sh,)


def _f_gated_add(x, y, g):
    return (x + g * y.astype(f32),)


def _f_glu_post(x, zv, zg, g, bv, bg):
    return (x + g * ((zv.astype(f32) + bv) * jax.nn.sigmoid(zg.astype(f32) + bg)),)


def _f_silu(x):
    return (jax.nn.silu(x.astype(f32)),)


def _loss_head(x, tgt, w, *, nrows, name):
    d = x.shape[1]
    tm = BLK

    def body(x_ref, t_ref, w_ref, dx_ref, dw_ref, loss_ref):
        i = pl.program_id(0)
        wb = jnp.broadcast_to(w_ref[...], (tm, d))
        y, vjp = jax.vjp(_rms, x_ref[...], wb)
        e = y - t_ref[...]
        dx, dwb = vjp(e * (1.0 / d))
        dx_ref[...] = dx
        dw = jnp.sum(dwb, axis=0, keepdims=True)
        part = jnp.full((8, 128), 0.5 / d, f32) * jnp.sum(e * e)

        @pl.when(i == 0)
        def _():
            dw_ref[...] = dw
            loss_ref[...] = part

        @pl.when(i > 0)
        def _():
            dw_ref[...] += dw
            loss_ref[...] += part

    row = pl.BlockSpec((tm, d), lambda i: (i, 0))
    return pl.pallas_call(
        body, name=name, grid=(nrows // tm,),
        in_specs=[row, row, pl.BlockSpec((1, d), lambda i: (0, 0))],
        out_specs=[row, pl.BlockSpec((1, d), lambda i: (0, 0)), pl.BlockSpec((8, 128), lambda i: (0, 0))],
        out_shape=[jax.ShapeDtypeStruct((nrows, d), f32), jax.ShapeDtypeStruct((1, d), f32),
                   jax.ShapeDtypeStruct((8, 128), f32)],
        compiler_params=_params(("arbitrary",)),
    )(x, tgt, w)


def _adamw(w, g, m, v, name):
    r, c = w.shape
    tr = _pick(r, (512, 256, 128, 64, 32, 16, 8))
    tcol = _pick(c, (1024, 512)) if c % 128 == 0 else c

    def body(w_ref, g_ref, m_ref, v_ref, d_ref, m2_ref, v2_ref):
        gg = g_ref[...]
        m2 = ADAM_B1 * m_ref[...] + (1.0 - ADAM_B1) * gg
        v2 = ADAM_B2 * v_ref[...] + (1.0 - ADAM_B2) * (gg * gg)
        m_hat = m2 / (1.0 - ADAM_B1 ** ADAM_STEP)
        v_hat = v2 / (1.0 - ADAM_B2 ** ADAM_STEP)
        d_ref[...] = -ADAM_LR * (m_hat / (jnp.sqrt(v_hat) + ADAM_EPS) + ADAM_WD * w_ref[...])
        m2_ref[...] = m2
        v2_ref[...] = v2

    spec = pl.BlockSpec((tr, tcol), lambda i, j: (i, j))
    return pl.pallas_call(
        body, name=name, grid=(r // tr, c // tcol), in_specs=[spec] * 4, out_specs=[spec] * 3,
        out_shape=[jax.ShapeDtypeStruct((r, c), f32)] * 3,
        compiler_params=_params(("parallel", "parallel")),
    )(w, g, m, v)


def _swap_quarters(x):
    lane = lax.broadcasted_iota(jnp.int32, x.shape, 1)
    return jnp.where((lane & 63) < 32, pltpu.roll(x, 96, 1), pltpu.roll(x, 32, 1))


def _rope_tables(n_x, n_ctx):
    t = np.arange(n_x)
    quarter = HEAD_DIM // 4
    inv = ROPE_BASE ** (-np.arange(quarter, dtype=np.float64) / quarter)
    ar = (t // GRID_W)[:, None] * inv[None]
    ac = (t % GRID_W)[:, None] * inv[None]
    cos = np.concatenate([np.cos(ar), np.cos(ar), np.cos(ac), np.cos(ac)], axis=1)
    sin = np.concatenate([-np.sin(ar), np.sin(ar), -np.sin(ac), np.sin(ac)], axis=1)
    cos = np.concatenate([cos, np.ones((n_ctx, HEAD_DIM))], axis=0)
    sin = np.concatenate([sin, np.zeros((n_ctx, HEAD_DIM))], axis=0)
    return jnp.asarray(cos, f32), jnp.asarray(sin, f32)


A_W = NA_HEADS * HEAD_DIM
QB0, KB0, VB0 = 3 * A_W, 3 * A_W + NB_Q_HEADS * HEAD_DIM, 3 * A_W + (NB_Q_HEADS + NB_KV_HEADS) * HEAD_DIM
IN_W = VB0 + NB_KV_HEADS * HEAD_DIM


def _qkv_post(qkv, cos, sin, name):
    n = qkv.shape[0]

    def body(x_ref, c_ref, s_ref, o_ref):
        c, s = c_ref[...], s_ref[...]
        o_ref[:, :QB0] = x_ref[:, :QB0].astype(bf16)
        for col in range(QB0, VB0, HEAD_DIM):
            x = x_ref[:, col:col + HEAD_DIM]
            o_ref[:, col:col + HEAD_DIM] = (x * c + _swap_quarters(x) * s).astype(bf16)
        o_ref[:, VB0:] = x_ref[:, VB0:].astype(bf16)

    row = lambda c: pl.BlockSpec((BLK, c), lambda i: (i, 0))
    return pl.pallas_call(
        body, name=name, grid=(n // BLK,), in_specs=[row(IN_W), row(HEAD_DIM), row(HEAD_DIM)],
        out_specs=row(IN_W), out_shape=jax.ShapeDtypeStruct((n, IN_W), bf16),
        compiler_params=_params(("parallel",)),
    )(qkv, cos, sin)


def _qkv_post_bwd(parts, cos, sin, name):
    n = parts[0].shape[0]

    def body(qa, ka, va, qb, kb, vb, c_ref, s_ref, o_ref):
        c, s = c_ref[...], s_ref[...]
        o_ref[:, 0:A_W] = qa[...]
        o_ref[:, A_W:2 * A_W] = ka[...]
        o_ref[:, 2 * A_W:QB0] = va[...]
        for src, col0, width in ((qb, QB0, KB0 - QB0), (kb, KB0, VB0 - KB0)):
            for off in range(0, width, HEAD_DIM):
                g = src[:, off:off + HEAD_DIM].astype(f32)
                o_ref[:, col0 + off:col0 + off + HEAD_DIM] = (g * c + _swap_quarters(g * s)).astype(bf16)
        o_ref[:, VB0:] = vb[...]

    row = lambda c: pl.BlockSpec((BLK, c), lambda i: (i, 0))
    return pl.pallas_call(
        body, name=name, grid=(n // BLK,),
        in_specs=[row(p.shape[1]) for p in parts] + [row(HEAD_DIM), row(HEAD_DIM)],
        out_specs=row(IN_W), out_shape=jax.ShapeDtypeStruct((n, IN_W), bf16),
        compiler_params=_params(("parallel",)),
    )(*parts, cos, sin)


def _valid(kind, qpos, kpos, n_x):
    ok = (kpos >= 0) & (kpos < n_x) & (qpos >= 0) & (qpos < n_x)
    if kind == "na":
        rows = n_x // GRID_W
        qr, qc = lax.shift_right_arithmetic(qpos, 6), qpos & (GRID_W - 1)
        kr, kc = lax.shift_right_arithmetic(kpos, 6), kpos & (GRID_W - 1)
        kr0 = jnp.clip(qr - NA_ROWS // 2, 0, rows - NA_ROWS)
        ws = jnp.clip(qc - NA_COLS // 2, 0, GRID_W - NA_COLS)
        return ok & (kr >= kr0) & (kr < kr0 + NA_ROWS) & (kc >= ws) & (kc < ws + NA_COLS)
    return ok & (jnp.abs(kpos - qpos) <= SW_RADIUS)


def _stack_heads(x, g):
    if g == 1:
        return x
    return jnp.concatenate([x[:, a * HEAD_DIM:(a + 1) * HEAD_DIM] for a in range(g)], axis=0)


def _unstack_heads(x, g):
    if g == 1:
        return x
    r = x.shape[0] // g
    return jnp.concatenate([x[a * r:(a + 1) * r] for a in range(g)], axis=1)


def _tile_rows(x, g):
    return x if g == 1 else jnp.concatenate([x] * g, axis=0)


class _AttnCfg:
    def __init__(self, kind, n_x, n_tot):
        self.kind, self.n_x, self.n_tot = kind, n_x, n_tot
        self.n_xb, self.n_blk = n_x // BLK, n_tot // BLK
        if kind == "na":
            self.g, self.nkv, self.q0, self.k0, self.v0 = 1, NA_HEADS, 0, NA_HEADS, 2 * NA_HEADS
        else:
            self.g, self.nkv = NB_GROUP, NB_KV_HEADS
            self.q0, self.k0, self.v0 = QB0 // (NB_GROUP * HEAD_DIM), KB0 // HEAD_DIM, VB0 // HEAD_DIM
        self.r = BLK * self.g
        self.qw = HEAD_DIM * self.g
        self.scale = HEAD_DIM ** -0.5


def _attn_fwd(qkv, cfg, extra, name, rider=None):
    g, r, qw, n_xb, n_x = cfg.g, cfg.r, cfg.qw, cfg.n_xb, cfg.n_x
    last = n_xb - 1

    def body(q_ref, kp, ko, kn, vp, vo, vn, kc_ref, vc_ref, ex_ref, o_ref, lse_ref):
        i = pl.program_id(1)
        q = _stack_heads(q_ref[...], g)
        kw = jnp.concatenate([kp[...], ko[...], kn[...]], axis=0)
        vw = jnp.concatenate([vp[...], vo[...], vn[...]], axis=0)
        s = lax.dot_general(q, kw, NT, preferred_element_type=f32) * cfg.scale
        if cfg.kind == "na":
            s = s + ex_ref[...]
        qpos = i * BLK + lax.broadcasted_iota(jnp.int32, (BLK, 3 * BLK), 0)
        kpos = (i - 1) * BLK + lax.broadcasted_iota(jnp.int32, (BLK, 3 * BLK), 1)
        s = jnp.where(_tile_rows(_valid(cfg.kind, qpos, kpos, n_x), g), s, NEG_INF)
        sc = lax.dot_general(q, kc_ref[...], NT, preferred_element_type=f32) * cfg.scale
        m = jnp.maximum(jnp.max(s, axis=-1, keepdims=True), jnp.max(sc, axis=-1, keepdims=True))
        if cfg.kind == "swa":
            m = jnp.maximum(m, ex_ref[...])
        p, pc = jnp.exp(s - m), jnp.exp(sc - m)
        l = jnp.sum(p, axis=-1, keepdims=True) + jnp.sum(pc, axis=-1, keepdims=True)
        if cfg.kind == "swa":
            l = l + jnp.exp(ex_ref[...] - m)
        o = jnp.dot(p.astype(bf16), vw, preferred_element_type=f32) + jnp.dot(pc.astype(bf16), vc_ref[...],
                                                                             preferred_element_type=f32)
        o_ref[...] = _unstack_heads(o / l, g).astype(bf16)
        lse_ref[...] = m + jnp.log(l)

    kv = lambda col0, f: pl.BlockSpec((BLK, HEAD_DIM), lambda h, i: (f(i), col0 + h))
    prev = lambda i: jnp.clip(i - 1, 0, last)
    own = lambda i: jnp.minimum(i, last)
    nxt = lambda i: jnp.minimum(i + 1, last)
    ctx = lambda i: n_xb
    if cfg.kind == "na":
        ex_spec = pl.BlockSpec((None, BLK, 3 * BLK), lambda h, i: (h, 0, 0))
    else:
        ex_spec = pl.BlockSpec((None, r, 1), lambda h, i: (h, 0, 0))
    return _call(
        body, (qkv,) * 9 + (extra,), name=name, grid=(cfg.nkv, cfg.n_blk),
        in_specs=[pl.BlockSpec((BLK, qw), lambda h, i: (i, cfg.q0 + h)),
                  kv(cfg.k0, prev), kv(cfg.k0, own), kv(cfg.k0, nxt),
                  kv(cfg.v0, prev), kv(cfg.v0, own), kv(cfg.v0, nxt),
                  kv(cfg.k0, ctx), kv(cfg.v0, ctx), ex_spec],
        out_specs=[pl.BlockSpec((BLK, qw), lambda h, i: (i, h)),
                   pl.BlockSpec((None, None, r, 1), lambda h, i: (h, i, 0, 0))],
        out_shape=[jax.ShapeDtypeStruct((cfg.n_tot, cfg.nkv * qw), bf16),
                   jax.ShapeDtypeStruct((cfg.nkv, cfg.n_blk, r, 1), f32)],
        sem=("parallel", "parallel"), rider=rider)


def _attn_bwd_q(qkv, o, do, lse, cfg, extra, name, rider=None):
    g, r, qw, n_xb, n_x = cfg.g, cfg.r, cfg.qw, cfg.n_xb, cfg.n_x
    last = n_xb - 1
    do_col0 = 0 if cfg.kind == "na" else (NA_HEADS * HEAD_DIM) // qw

    def body(q_ref, kp, ko, kn, vp, vo, vn, kc_ref, vc_ref, ex_ref, o_ref, do_ref, lse_ref,
             dq_ref, delta_ref, dex_ref):
        i = pl.program_id(1)
        q = _stack_heads(q_ref[...], g)
        dout = _stack_heads(do_ref[...], g)
        out = _stack_heads(o_ref[...], g)
        delta = jnp.sum(dout.astype(f32) * out.astype(f32), axis=-1, keepdims=True)
        delta_ref[...] = delta
        kw = jnp.concatenate([kp[...], ko[...], kn[...]], axis=0)
        vw = jnp.concatenate([vp[...], vo[...], vn[...]], axis=0)
        s = lax.dot_general(q, kw, NT, preferred_element_type=f32) * cfg.scale
        if cfg.kind == "na":
            s = s + ex_ref[...]
        qpos = i * BLK + lax.broadcasted_iota(jnp.int32, (BLK, 3 * BLK), 0)
        kpos = (i - 1) * BLK + lax.broadcasted_iota(jnp.int32, (BLK, 3 * BLK), 1)
        s = jnp.where(_tile_rows(_valid(cfg.kind, qpos, kpos, n_x), g), s, NEG_INF)
        sc = lax.dot_general(q, kc_ref[...], NT, preferred_element_type=f32) * cfg.scale
        lse_v = lse_ref[...]
        p, pc = jnp.exp(s - lse_v), jnp.exp(sc - lse_v)
        dp = lax.dot_general(dout, vw, NT, preferred_element_type=f32)
        dpc = lax.dot_general(dout, vc_ref[...], NT, preferred_element_type=f32)
        ds, dsc = p * (dp - delta), pc * (dpc - delta)
        dq = jnp.dot(ds.astype(bf16), kw, preferred_element_type=f32) + jnp.dot(dsc.astype(bf16), kc_ref[...],
                                                                             preferred_element_type=f32)
        dq_ref[...] = _unstack_heads(dq * cfg.scale, g).astype(bf16)
        dex = ds if cfg.kind == "na" else -jnp.exp(ex_ref[...] - lse_v) * delta

        @pl.when(i == 0)
        def _():
            dex_ref[...] = dex

        @pl.when(i > 0)
        def _():
            dex_ref[...] += dex

    kv = lambda col0, f: pl.BlockSpec((BLK, HEAD_DIM), lambda h, i: (f(i), col0 + h))
    prev = lambda i: jnp.clip(i - 1, 0, last)
    own = lambda i: jnp.minimum(i, last)
    nxt = lambda i: jnp.minimum(i + 1, last)
    ctx = lambda i: n_xb
    if cfg.kind == "na":
        ex_spec = pl.BlockSpec((None, BLK, 3 * BLK), lambda h, i: (h, 0, 0))
    else:
        ex_spec = pl.BlockSpec((None, r, 1), lambda h, i: (h, 0, 0))
    stat = pl.BlockSpec((None, None, r, 1), lambda h, i: (h, i, 0, 0))
    return _call(
        body, (qkv,) * 9 + (extra, o, do, lse), name=name, grid=(cfg.nkv, cfg.n_blk),
        in_specs=[pl.BlockSpec((BLK, qw), lambda h, i: (i, cfg.q0 + h)),
                  kv(cfg.k0, prev), kv(cfg.k0, own), kv(cfg.k0, nxt),
                  kv(cfg.v0, prev), kv(cfg.v0, own), kv(cfg.v0, nxt),
                  kv(cfg.k0, ctx), kv(cfg.v0, ctx), ex_spec,
                  pl.BlockSpec((BLK, qw), lambda h, i: (i, h)),
                  pl.BlockSpec((BLK, qw), lambda h, i: (i, do_col0 + h)), stat],
        out_specs=[pl.BlockSpec((BLK, qw), lambda h, i: (i, h)), stat, ex_spec],
        out_shape=[jax.ShapeDtypeStruct((cfg.n_tot, cfg.nkv * qw), bf16),
                   jax.ShapeDtypeStruct((cfg.nkv, cfg.n_blk, r, 1), f32),
                   jax.ShapeDtypeStruct(extra.shape, f32)],
        sem=("parallel", "arbitrary"), rider=rider)


def _attn_bwd_kv(qkv, do, lse, delta, cfg, bias_t, name, rider=None):
    g, r, qw, n_xb, n_x, n_blk = cfg.g, cfg.r, cfg.qw, cfg.n_xb, cfg.n_x, cfg.n_blk
    last = n_xb - 1
    do_col0 = 0 if cfg.kind == "na" else (NA_HEADS * HEAD_DIM) // qw
    has_bias = cfg.kind == "na"

    def body(*refs):
        qs, dos, lses, dels = refs[0:3], refs[3:6], refs[6:9], refs[9:12]
        qj_ref, doj_ref, lsej_ref, delj_ref, k_ref, v_ref, kc_ref, vc_ref = refs[12:20]
        rest = refs[20:]
        if has_bias:
            b_ref, rest = rest[0], rest[1:]
        dk_ref, dv_ref, dkc_acc, dvc_acc = rest
        j = pl.program_id(1)

        @pl.when(j == 0)
        def _():
            dkc_acc[...] = jnp.zeros_like(dkc_acc)
            dvc_acc[...] = jnp.zeros_like(dvc_acc)

        qj, doj = _stack_heads(qj_ref[...], g), _stack_heads(doj_ref[...], g)
        sc = lax.dot_general(qj, kc_ref[...], NT, preferred_element_type=f32) * cfg.scale
        pc = jnp.exp(sc - lsej_ref[...])
        dvc_acc[...] += lax.dot_general(pc.astype(bf16), doj, TN, preferred_element_type=f32)
        dpc = lax.dot_general(doj, vc_ref[...], NT, preferred_element_type=f32)
        dsc = pc * (dpc - delj_ref[...])
        dkc_acc[...] += lax.dot_general(dsc.astype(bf16), qj, TN, preferred_element_type=f32) * cfg.scale

        @pl.when(j < n_xb)
        def _():
            qw_all = jnp.concatenate([_stack_heads(x[...], g) for x in qs], axis=0)
            do_all = jnp.concatenate([_stack_heads(x[...], g) for x in dos], axis=0)
            lse_all = jnp.concatenate([x[...] for x in lses], axis=0)
            del_all = jnp.concatenate([x[...] for x in dels], axis=0)
            s = lax.dot_general(qw_all, k_ref[...], NT, preferred_element_type=f32) * cfg.scale
            if has_bias:
                s = s + b_ref[...]
            kpos = j * BLK + lax.broadcasted_iota(jnp.int32, (BLK, BLK), 1)
            masks = []
            for w in range(3):
                qpos = (j - 1 + w) * BLK + lax.broadcasted_iota(jnp.int32, (BLK, BLK), 0)
                masks.append(_tile_rows(_valid(cfg.kind, qpos, kpos, n_x), g))
            s = jnp.where(jnp.concatenate(masks, axis=0), s, NEG_INF)
            p = jnp.exp(s - lse_all)
            dv_ref[...] = lax.dot_general(p.astype(bf16), do_all, TN, preferred_element_type=f32).astype(bf16)
            dp = lax.dot_general(do_all, v_ref[...], NT, preferred_element_type=f32)
            ds = p * (dp - del_all)
            dk_ref[...] = (lax.dot_general(ds.astype(bf16), qw_all, TN, preferred_element_type=f32)
                           * cfg.scale).astype(bf16)

        @pl.when(j == n_xb)
        def _():
            dk_ref[...] = dkc_acc[...].astype(bf16)
            dv_ref[...] = dvc_acc[...].astype(bf16)

    prev = lambda j: jnp.clip(j - 1, 0, last)
    own = lambda j: jnp.minimum(j, last)
    nxt = lambda j: jnp.minimum(j + 1, last)
    same = lambda j: j
    qspec = lambda f: pl.BlockSpec((BLK, qw), lambda h, j: (f(j), cfg.q0 + h))
    dospec = lambda f: pl.BlockSpec((BLK, qw), lambda h, j: (f(j), do_col0 + h))
    stat = lambda f: pl.BlockSpec((None, None, r, 1), lambda h, j: (h, f(j), 0, 0))
    kv = lambda col0, f: pl.BlockSpec((BLK, HEAD_DIM), lambda h, j: (f(j), col0 + h))
    ctx = lambda j: n_xb
    in_specs = ([qspec(f) for f in (prev, own, nxt)] + [dospec(f) for f in (prev, own, nxt)]
                + [stat(f) for f in (prev, own, nxt)] * 2
                + [qspec(same), dospec(same), stat(same), stat(same),
                   kv(cfg.k0, same), kv(cfg.v0, same), kv(cfg.k0, ctx), kv(cfg.v0, ctx)])
    args = [qkv] * 3 + [do] * 3 + [lse] * 3 + [delta] * 3 + [qkv, do, lse, delta, qkv, qkv, qkv, qkv]
    if has_bias:
        in_specs.append(pl.BlockSpec((None, 3 * BLK, BLK), lambda h, j: (h, 0, 0)))
        args.append(bias_t)
    out = pl.BlockSpec((BLK, HEAD_DIM), lambda h, j: (j, h))
    return _call(
        body, args, name=name, grid=(cfg.nkv, n_blk), in_specs=in_specs, out_specs=[out, out],
        out_shape=[jax.ShapeDtypeStruct((cfg.n_tot, cfg.nkv * HEAD_DIM), bf16)] * 2,
        scratch_shapes=[pltpu.VMEM((BLK, HEAD_DIM), f32)] * 2, sem=("parallel", "arbitrary"), rider=rider)


def _toeplitz_basis():
    qc, kc = np.meshgrid(np.arange(GRID_W), np.arange(GRID_W), indexing="ij")
    e = (kc - qc + NA_COLS - 1)[None] == np.arange(2 * NA_COLS - 1)[:, None, None]
    return e.reshape(2 * NA_COLS - 1, GRID_W * GRID_W).astype(np.float32)


def _whole(f, ins, outs, name):
    def body(*refs):
        res = f(*[r[...] for r in refs[:len(ins)]])
        for o_ref, val in zip(refs[len(ins):], res):
            o_ref[...] = val.astype(o_ref.dtype)

    return pl.pallas_call(body, name=name,
                          out_shape=[jax.ShapeDtypeStruct(s, d) for s, d in outs])(*ins)


def _whole_bwd(f, ins, cts, name):
    n = len(ins)

    def body(*refs):
        _, vjp = jax.vjp(f, *[r[...] for r in refs[:n]])
        grads = vjp(tuple(r[...] for r in refs[n:n + len(cts)]))
        for o_ref, g in zip(refs[n + len(cts):], grads):
            o_ref[...] = g

    return pl.pallas_call(body, name=name,
                          out_shape=[jax.ShapeDtypeStruct(a.shape, f32) for a in ins])(*ins, *cts)


def _f_discretise(ar, ai, ldt, br, bi):
    dt = jnp.exp(ldt)
    mag = jnp.exp(ar * dt)
    lam_r, lam_i = mag * jnp.cos(ai * dt), mag * jnp.sin(ai * dt)
    den = ar * ar + ai * ai
    nr = lam_r - 1.0
    coef_r = (nr * ar + lam_i * ai) / den
    coef_i = (lam_i * ar - nr * ai) / den
    return (lam_r, lam_i, coef_r[None] * br - coef_i[None] * bi, coef_r[None] * bi + coef_i[None] * br)


SEG_LEN = BLK // SCAN_SEG
N_STATE = GROUPS_PER_CHUNK * SSM_STATE


def _scan_perm():
    r = np.arange(BLK)
    t = (r % SCAN_SEG) * SEG_LEN + r // SCAN_SEG
    pm = np.zeros((BLK, BLK), np.float32)
    pm[r, t] = 1.0
    return jnp.asarray(pm, bf16), jnp.asarray(pm.T, bf16)


def _row_perm(x, pm, name, *, out_dtype, split, add=None, add_rows=None):
    n, c = x.shape
    add_tiles = None if add is None else add_rows // BLK

    def body(*refs):
        x_ref, pm_ref = refs[0], refs[1]
        o_ref = refs[-1]
        xv = x_ref[...]
        if split:
            out = _unpermute(pm_ref[...], xv.astype(f32))
        else:
            out = jnp.dot(pm_ref[...], xv.astype(bf16), preferred_element_type=f32)
        if add is not None:
            out = out + jnp.where(pl.program_id(0) < add_tiles, refs[2][...], 0.0)
        o_ref[...] = out.astype(o_ref.dtype)

    row = pl.BlockSpec((BLK, c), lambda i: (i, 0))
    in_specs, args = [row, pl.BlockSpec((BLK, BLK), lambda i: (0, 0))], [x, pm]
    if add is not None:
        in_specs.append(pl.BlockSpec((BLK, c), lambda i: (jnp.minimum(i, add_tiles - 1), 0)))
        args.append(add)
    return pl.pallas_call(
        body, name=name, grid=(n // BLK,), in_specs=in_specs, out_specs=row,
        out_shape=jax.ShapeDtypeStruct((n, c), out_dtype), compiler_params=_params(("parallel",)),
    )(*args)


def _glu_pre(u, yf, yr, dvec, pmt, n_x, name):
    d = u.shape[1]

    def body(u_ref, yf_ref, yr_ref, d_ref, pmt_ref, o_ref):
        y = d_ref[...] * u_ref[...] + _unpermute(pmt_ref[...], yf_ref[...] + yr_ref[...])
        o_ref[...] = jax.nn.gelu(y).astype(bf16)

    row = pl.BlockSpec((BLK, d), lambda i: (i, 0))
    return pl.pallas_call(
        body, name=name, grid=(n_x // BLK,),
        in_specs=[row, row, row, pl.BlockSpec((1, d), lambda i: (0, 0)), pl.BlockSpec((BLK, BLK), lambda i: (0, 0))],
        out_specs=row, out_shape=jax.ShapeDtypeStruct((n_x, d), bf16), compiler_params=_params(("parallel",)),
    )(u, yf, yr, dvec, pmt)


def _glu_pre_bwd(u, yf, yr, dvec, pm, pmt, dgy, n_x, name):
    d = u.shape[1]

    def body(u_ref, yf_ref, yr_ref, d_ref, pm_ref, pmt_ref, ct_ref, dyp_ref, dud_ref, dd_ref):
        uv = u_ref[...]
        y = d_ref[...] * uv + _unpermute(pmt_ref[...], yf_ref[...] + yr_ref[...])
        _, vjp = jax.vjp(jax.nn.gelu, y)
        dy = vjp(ct_ref[...])[0]
        dyp_ref[...] = jnp.dot(pm_ref[...], dy.astype(bf16), preferred_element_type=f32).astype(bf16)
        dud_ref[...] = d_ref[...] * dy
        part = jnp.sum(dy * uv, axis=0, keepdims=True)

        @pl.when(pl.program_id(0) == 0)
        def _():
            dd_ref[...] = part

        @pl.when(pl.program_id(0) > 0)
        def _():
            dd_ref[...] += part

    row = pl.BlockSpec((BLK, d), lambda i: (i, 0))
    vec = pl.BlockSpec((1, d), lambda i: (0, 0))
    sq = pl.BlockSpec((BLK, BLK), lambda i: (0, 0))
    return pl.pallas_call(
        body, name=name, grid=(n_x // BLK,), in_specs=[row, row, row, vec, sq, sq, row],
        out_specs=[row, row, vec],
        out_shape=[jax.ShapeDtypeStruct((n_x, d), bf16), jax.ShapeDtypeStruct((n_x, d), f32),
                   jax.ShapeDtypeStruct((1, d), f32)],
        compiler_params=_params(("arbitrary",)),
    )(u, yf, yr, dvec, pm, pmt, dgy)


def _block_order(n_xb, n_blk, reverse):
    n_cb = n_blk - n_xb
    if reverse:
        return lambda cc: jnp.where(cc < n_cb, n_blk - 1 - cc, n_xb - 1 - (cc - n_cb))
    return lambda cc: jnp.where(cc < n_cb, n_xb + cc, cc - n_cb)


def _unpermute(pmt, y):
    hi = y.astype(bf16)
    lo = (y - hi.astype(f32)).astype(bf16)
    return jnp.dot(pmt, hi, preferred_element_type=f32) + jnp.dot(pmt, lo, preferred_element_type=f32)


def _lam_pow(lr, li):
    for _ in range(int(math.log2(SEG_LEN))):
        lr, li = lr * lr - li * li, 2.0 * lr * li
    return lr, li


def _s5_fwd(up, lam, bblk, cblk, n_x, reverse, name, rider=None):
    n_tot, d = up.shape
    nq, n_blk, n_xb = d // 128, n_tot // BLK, n_x // BLK
    order = _block_order(n_xb, n_blk, reverse)
    ns = N_STATE
    seg_order = range(SCAN_SEG - 1, -1, -1) if reverse else range(SCAN_SEG)

    def body(u_ref, lam_ref, b_ref, c_ref, y_ref, cp_ref, bu_ref, st_ref, carry_ref):
        cc = pl.program_id(1)

        @pl.when(cc == 0)
        def _():
            carry_ref[...] = jnp.zeros_like(carry_ref)

        up_v = u_ref[...]
        bu_ref[0] = jnp.dot(up_v, b_ref[0], preferred_element_type=f32)
        bu_ref[1] = jnp.dot(up_v, b_ref[1], preferred_element_type=f32)
        lr, li = lam_ref[0:1, :], lam_ref[1:2, :]
        lrb, lib = jnp.broadcast_to(lr, (SCAN_SEG, ns)), jnp.broadcast_to(li, (SCAN_SEG, ns))

        def step(p, s, store):
            sr, si = s
            j = SEG_LEN - 1 - p if reverse else p
            off = pl.multiple_of(j * SCAN_SEG, SCAN_SEG)
            nsr = lrb * sr - lib * si + bu_ref[0, pl.ds(off, SCAN_SEG), :]
            nsi = lrb * si + lib * sr + bu_ref[1, pl.ds(off, SCAN_SEG), :]
            if store:
                st_ref[0, pl.ds(off, SCAN_SEG), :] = nsr
                st_ref[1, pl.ds(off, SCAN_SEG), :] = nsi
            return nsr, nsi

        zero = jnp.zeros((SCAN_SEG, ns), f32)
        er, ei = lax.fori_loop(0, SEG_LEN, lambda j, s: step(j, s, False), (zero, zero))
        pr, pi = _lam_pow(lr, li)
        cr, ci = carry_ref[0, 0:1, :], carry_ref[1, 0:1, :]
        rows_r, rows_i = [None] * SCAN_SEG, [None] * SCAN_SEG
        for k in seg_order:
            rows_r[k], rows_i[k] = cr, ci
            cr, ci = pr * cr - pi * ci + er[k:k + 1], pr * ci + pi * cr + ei[k:k + 1]
        carry_ref[0] = jnp.broadcast_to(cr, (SCAN_SEG, ns))
        carry_ref[1] = jnp.broadcast_to(ci, (SCAN_SEG, ns))
        cpr, cpi = jnp.concatenate(rows_r, axis=0), jnp.concatenate(rows_i, axis=0)
        cp_ref[0] = cpr
        cp_ref[1] = cpi
        lax.fori_loop(0, SEG_LEN, lambda j, s: step(j, s, True), (cpr, cpi))
        y_ref[...] = (jnp.dot(st_ref[0].astype(bf16), c_ref[0], preferred_element_type=f32)
                      - jnp.dot(st_ref[1].astype(bf16), c_ref[1], preferred_element_type=f32))

    return _call(
        body, (up, lam, bblk, cblk), name=name, grid=(nq, n_blk),
        in_specs=[pl.BlockSpec((BLK, 128), lambda q, cc: (order(cc), q)),
                  pl.BlockSpec((None, 2, ns), lambda q, cc: (q, 0, 0)),
                  pl.BlockSpec((None, 2, 128, ns), lambda q, cc: (q, 0, 0, 0)),
                  pl.BlockSpec((None, 2, ns, 128), lambda q, cc: (q, 0, 0, 0))],
        out_specs=[pl.BlockSpec((BLK, 128), lambda q, cc: (order(cc), q)),
                   pl.BlockSpec((None, None, 2, SCAN_SEG, ns), lambda q, cc: (q, cc, 0, 0, 0))],
        out_shape=[jax.ShapeDtypeStruct((n_tot, d), f32),
                   jax.ShapeDtypeStruct((nq, n_blk, 2, SCAN_SEG, ns), f32)],
        scratch_shapes=[pltpu.VMEM((2, BLK, ns), f32), pltpu.VMEM((2, BLK, ns), f32),
                        pltpu.VMEM((2, SCAN_SEG, ns), f32)],
        sem=("parallel", "arbitrary"), rider=rider)


def _s5_bwd(u, dy, cprev, lam, bblk, cblk, bblk_t, cblk_t, n_x, reverse, name, add=None, rider=None):
    n_tot, d = u.shape
    nq, n_blk, n_xb = d // 128, n_tot // BLK, n_x // BLK
    order = _block_order(n_xb, n_blk, reverse)
    ns = N_STATE
    blk_of = lambda step: order(n_blk - 1 - step)
    has_add = add is not None
    jof = (lambda p: SEG_LEN - 1 - p) if reverse else (lambda p: p)
    adj_seg_order = range(SCAN_SEG) if reverse else range(SCAN_SEG - 1, -1, -1)

    def body(*refs):
        u_ref, dy_ref, cp_ref, lam_ref, b_ref, c_ref, bt_ref, ct_ref = refs[:8]
        rest = refs[8:]
        if has_add:
            add_ref, rest = rest[0], rest[1:]
        du_ref, dlam_ref, db_ref, dc_ref, bu_ref, st_ref, ds_ref, acarry_ref = rest
        step_id = pl.program_id(1)
        is_x = blk_of(step_id) < n_xb

        @pl.when(step_id == 0)
        def _():
            acarry_ref[...] = jnp.zeros_like(acarry_ref)
            dlam_ref[...] = jnp.zeros_like(dlam_ref)
            db_ref[...] = jnp.zeros_like(db_ref)
            dc_ref[...] = jnp.zeros_like(dc_ref)

        up = u_ref[...]
        bu_ref[0] = jnp.dot(up, b_ref[0], preferred_element_type=f32)
        bu_ref[1] = jnp.dot(up, b_ref[1], preferred_element_type=f32)
        lr, li = lam_ref[0:1, :], lam_ref[1:2, :]
        lrb, lib = jnp.broadcast_to(lr, (SCAN_SEG, ns)), jnp.broadcast_to(li, (SCAN_SEG, ns))
        cpr, cpi = cp_ref[0], cp_ref[1]

        def fstep(p, s):
            sr, si = s
            off = pl.multiple_of(jof(p) * SCAN_SEG, SCAN_SEG)
            nsr = lrb * sr - lib * si + bu_ref[0, pl.ds(off, SCAN_SEG), :]
            nsi = lrb * si + lib * sr + bu_ref[1, pl.ds(off, SCAN_SEG), :]
            st_ref[0, pl.ds(off, SCAN_SEG), :] = nsr
            st_ref[1, pl.ds(off, SCAN_SEG), :] = nsi
            return nsr, nsi

        lax.fori_loop(0, SEG_LEN, fstep, (cpr, cpi))

        dyp = jnp.where(is_x, dy_ref[...], jnp.zeros_like(dy_ref))
        ds_ref[0] = jnp.dot(dyp, ct_ref[0], preferred_element_type=f32)
        ds_ref[1] = -jnp.dot(dyp, ct_ref[1], preferred_element_type=f32)

        def adj(p, a):
            ar, ai = a
            off = pl.multiple_of(jof(p) * SCAN_SEG, SCAN_SEG)
            nar = ds_ref[0, pl.ds(off, SCAN_SEG), :] + lrb * ar + lib * ai
            nai = ds_ref[1, pl.ds(off, SCAN_SEG), :] - lib * ar + lrb * ai
            return nar, nai

        zero = jnp.zeros((SCAN_SEG, ns), f32)
        er, ei = lax.fori_loop(0, SEG_LEN, lambda jj, a: adj(SEG_LEN - 1 - jj, a), (zero, zero))
        pr, pi = _lam_pow(lr, li)
        nr_, ni_ = acarry_ref[0, 0:1, :], acarry_ref[1, 0:1, :]
        rows_r, rows_i = [None] * SCAN_SEG, [None] * SCAN_SEG
        for k in adj_seg_order:
            rows_r[k], rows_i[k] = nr_, ni_
            nr_, ni_ = er[k:k + 1] + pr * nr_ + pi * ni_, ei[k:k + 1] + pr * ni_ - pi * nr_
        acarry_ref[0] = jnp.broadcast_to(nr_, (SCAN_SEG, ns))
        acarry_ref[1] = jnp.broadcast_to(ni_, (SCAN_SEG, ns))
        an_r, an_i = jnp.concatenate(rows_r, axis=0), jnp.concatenate(rows_i, axis=0)

        def adj2(jj, carry):
            ar, ai, glr, gli = carry
            p = SEG_LEN - 1 - jj
            nar, nai = adj(p, (ar, ai))
            off = pl.multiple_of(jof(p) * SCAN_SEG, SCAN_SEG)
            ds_ref[0, pl.ds(off, SCAN_SEG), :] = nar
            ds_ref[1, pl.ds(off, SCAN_SEG), :] = nai
            poff = pl.multiple_of(jof(p - 1) * SCAN_SEG, SCAN_SEG)
            spr, spi = st_ref[0, pl.ds(poff, SCAN_SEG), :], st_ref[1, pl.ds(poff, SCAN_SEG), :]
            return nar, nai, glr + nar * spr + nai * spi, gli - nar * spi + nai * spr

        ar, ai, glr, gli = lax.fori_loop(0, SEG_LEN - 1, adj2, (an_r, an_i, zero, zero))
        nar, nai = adj(0, (ar, ai))
        first = jof(0) * SCAN_SEG
        ds_ref[0, first:first + SCAN_SEG, :] = nar
        ds_ref[1, first:first + SCAN_SEG, :] = nai
        glr = glr + nar * cpr + nai * cpi
        gli = gli - nar * cpi + nai * cpr
        dlam_ref[0:1, :] += jnp.sum(glr, axis=0, keepdims=True)
        dlam_ref[1:2, :] += jnp.sum(gli, axis=0, keepdims=True)

        a_r, a_i = ds_ref[0].astype(bf16), ds_ref[1].astype(bf16)
        du = jnp.dot(a_r, bt_ref[0], preferred_element_type=f32) + jnp.dot(a_i, bt_ref[1],
                                                                        preferred_element_type=f32)
        if has_add:
            du = du + add_ref[...]
        du_ref[...] = du
        db_ref[0] += lax.dot_general(up, a_r, TN, preferred_element_type=f32)
        db_ref[1] += lax.dot_general(up, a_i, TN, preferred_element_type=f32)
        dc_ref[0] += lax.dot_general(st_ref[0].astype(bf16), dyp, TN, preferred_element_type=f32)
        dc_ref[1] -= lax.dot_general(st_ref[1].astype(bf16), dyp, TN, preferred_element_type=f32)

    tok = pl.BlockSpec((BLK, 128), lambda q, s: (blk_of(s), q))
    in_specs = [tok, pl.BlockSpec((BLK, 128), lambda q, s: (jnp.minimum(blk_of(s), n_xb - 1), q)),
                pl.BlockSpec((None, None, 2, SCAN_SEG, ns), lambda q, s: (q, n_blk - 1 - s, 0, 0, 0)),
                pl.BlockSpec((None, 2, ns), lambda q, s: (q, 0, 0)),
                pl.BlockSpec((None, 2, 128, ns), lambda q, s: (q, 0, 0, 0)),
                pl.BlockSpec((None, 2, ns, 128), lambda q, s: (q, 0, 0, 0)),
                pl.BlockSpec((None, 2, ns, 128), lambda q, s: (q, 0, 0, 0)),
                pl.BlockSpec((None, 2, 128, ns), lambda q, s: (q, 0, 0, 0))]
    args = [u, dy, cprev, lam, bblk, cblk, bblk_t, cblk_t]
    if has_add:
        in_specs.append(tok)
        args.append(add)
    return _call(
        body, args, name=name, grid=(nq, n_blk), in_specs=in_specs,
        out_specs=[tok, pl.BlockSpec((None, 2, ns), lambda q, s: (q, 0, 0)),
                   pl.BlockSpec((None, 2, 128, ns), lambda q, s: (q, 0, 0, 0)),
                   pl.BlockSpec((None, 2, ns, 128), lambda q, s: (q, 0, 0, 0))],
        out_shape=[jax.ShapeDtypeStruct((n_tot, d), f32), jax.ShapeDtypeStruct((nq, 2, ns), f32),
                   jax.ShapeDtypeStruct((nq, 2, 128, ns), f32), jax.ShapeDtypeStruct((nq, 2, ns, 128), f32)],
        scratch_shapes=[pltpu.VMEM((2, BLK, ns), f32), pltpu.VMEM((2, BLK, ns), f32),
                        pltpu.VMEM((2, BLK, ns), f32), pltpu.VMEM((2, SCAN_SEG, ns), f32)],
        sem=("parallel", "arbitrary"), rider=rider)


def _exchange(src, gather, name):
    return _run_exchange(_Exchange(src, gather), name)


def _sum0(x, name):
    n, r, c = x.shape
    tr = _pick(r, (512, 256, 128, 64, 32, 16, 8))

    def body(x_ref, o_ref):
        acc = x_ref[0].astype(f32)
        for k in range(1, n):
            acc = acc + x_ref[k].astype(f32)
        o_ref[...] = acc

    return pl.pallas_call(
        body, name=name, grid=(r // tr,), in_specs=[pl.BlockSpec((n, tr, c), lambda i: (0, i, 0))],
        out_specs=pl.BlockSpec((tr, c), lambda i: (i, 0)), out_shape=jax.ShapeDtypeStruct((r, c), f32),
        compiler_params=_params(("parallel",)),
    )(x)


LANES = 1024


def _pack(parts, dtype):
    flat = jnp.concatenate([p.astype(dtype).reshape(-1) for p in parts])
    pad = (-flat.shape[0]) % (16 * LANES)
    if pad:
        flat = jnp.concatenate([flat, jnp.zeros((pad,), dtype)])
    return flat.reshape(-1, LANES)


def _unpack(flat, shapes):
    out, off = [], 0
    for s in shapes:
        size = int(np.prod(s))
        out.append(flat[off:off + size].reshape(s))
        off += size
    return out


def _vec(a):
    return a.reshape(-1, 1, a.shape[-1])


def _local_step(xa, tgt, mods, wts, sm, n_x):
    n_tot, d = xa.shape
    kw = dict(nrows=n_tot, n_x_rows=n_x)
    kx = dict(nrows=n_x, n_x_rows=n_x)
    mv = lambda l, k: mods[l, :, k][:, None, :]
    mx = lambda l, k: mods[l, 0:1, k][:, None, :]
    nmix, nffn = sm["norm_mix"], sm["norm_ffn"]
    cos, sin = _rope_tables(n_x, n_tot - n_x)
    cfg_a, cfg_b = _AttnCfg("na", n_x, n_tot), _AttnCfg("swa", n_x, n_tot)

    h1 = _rowwise(_f_normmod, [xa], [_vec(nmix[0:1]), mv(0, 1), mv(0, 0)], [(d, bf16)], name="l0_norm_mix", **kw)[0]
    qkv32 = _mm(h1, wts("w_in"), "nn", f32, "l0_in_proj", comm=wts)
    qkv = _qkv_post(qkv32, cos, sin, "l0_qkv_post")
    rpb2 = jnp.zeros((128, 128), f32).at[:NA_HEADS * 15, :31].set(sm["rpb"].reshape(NA_HEADS * 15, 31))
    basis = jnp.zeros((128, GRID_W * GRID_W), f32).at[:31].set(_toeplitz_basis())
    tz = _small_dot(rpb2, basis, "rpb_expand")[:NA_HEADS * 15].reshape(NA_HEADS, 15, GRID_W, GRID_W)
    bias = jnp.stack([tz[:, 3 - a:15 - a] for a in range(4)], axis=1).transpose(0, 1, 3, 2, 4).reshape(
        NA_HEADS, BLK, 3 * BLK)
    bias_t = jnp.stack([tz[:, a:a + 12][:, ::-1] for a in range(4)], axis=2).transpose(0, 1, 3, 2, 4).reshape(
        NA_HEADS, 3 * BLK, BLK)
    sink_rows = jnp.repeat(sm["sink"].reshape(NB_KV_HEADS, NB_GROUP, 1), BLK, axis=1).reshape(
        NB_KV_HEADS, NB_GROUP * BLK, 1)
    oa, lse_a = _attn_fwd(qkv, cfg_a, bias, "l0_na_fwd", rider=wts.take(CARRIER_US["na_fwd"]))
    ob, lse_b = _attn_fwd(qkv, cfg_b, sink_rows, "l0_swa_fwd", rider=wts.take(CARRIER_US["swa_fwd"]))
    o = jnp.concatenate([oa, ob], axis=1)
    y1 = _mm(o, wts("w_out"), "nn", f32, "l0_out_proj", comm=wts)
    xb = _rowwise(_f_gated_add, [xa, y1], [mv(0, 2)], [(d, f32)], name="l0_res_mix", **kw)[0]

    def ffn_fwd(xin, l, vec_of, kk, tag):
        h = _rowwise(_f_normmod, [xin], [_vec(nffn[l:l + 1]), vec_of(l, 4), vec_of(l, 3)], [(d, bf16)],
                     name=tag + "_norm_ffn", **kk)[0]
        a, b, u = _ffn_up(h, wts("w1", l), wts("w3", l), tag + "_ffn_up", comm=wts)
        fo = _mm(u, wts("w2", l), "nn", f32, tag + "_ffn_w2", comm=wts)
        xo = _rowwise(_f_gated_add, [xin, fo], [vec_of(l, 5)], [(d, f32)], name=tag + "_res_ffn", **kk)[0]
        return xo, (h, a, b, u, fo)

    def ffn_bwd(d_out, xin, saved, l, vec_of, kk, tag):
        h, a, b, u, fo = saved
        rows = kk["nrows"]
        dfo, dg2 = _rowwise_bwd(_f_gated_add, [xin, fo], [vec_of(l, 5)], [d_out], {1: bf16},
                                name=tag + "_res_ffn_bwd", **kk)
        wts.grad("w2", l, _mm(u[:rows], dfo, "tn", f32, tag + "_ffn_w2_dw", comm=wts))
        da, db = _ffn_down_bwd(dfo, wts("w2", l), a, b, tag + "_ffn_down_bwd", comm=wts)
        dh = _mm(da, wts("w1", l), "nt", f32, tag + "_ffn_w13_dx", a2=db, b2=wts("w3", l), comm=wts)
        wts.grad("w1", l, _mm(h[:rows], da, "tn", f32, tag + "_ffn_w1_dw", comm=wts))
        wts.grad("w3", l, _mm(h[:rows], db, "tn", f32, tag + "_ffn_w3_dw", comm=wts))
        dxin, dnw, dsc, dsh = _rowwise_bwd(
            _f_normmod, [xin], [_vec(nffn[l:l + 1]), vec_of(l, 4), vec_of(l, 3)], [dh], {0: f32},
            name=tag + "_norm_ffn_bwd", residual=d_out, **kk)
        return dxin, dnw, dsc, dsh, dg2

    xc, ffn0 = ffn_fwd(xb, 0, mv, kw, "l0")

    hs = _rowwise(_f_normmod, [xc], [_vec(nmix[1:2]), mv(1, 1), mv(1, 0)], [(d, f32)], name="l1_norm_mix", **kw)[0]
    g2n = sm["a_re"].shape[1]
    nq = d // 128
    ar2, ai2 = sm["a_re"].reshape(2 * g2n, SSM_STATE), sm["a_im"].reshape(2 * g2n, SSM_STATE)
    ldt2 = sm["log_dt"].reshape(2 * g2n, 1)
    bt_re = sm["b_re"].transpose(3, 0, 1, 2).reshape(SSM_GROUP, 2 * g2n, SSM_STATE)
    bt_im = sm["b_im"].transpose(3, 0, 1, 2).reshape(SSM_GROUP, 2 * g2n, SSM_STATE)
    disc_in = [ar2, ai2, ldt2, bt_re, bt_im]
    lam_r, lam_i, bbar_r, bbar_i = _whole(
        _f_discretise, disc_in,
        [((2 * g2n, SSM_STATE), f32)] * 2 + [((SSM_GROUP, 2 * g2n, SSM_STATE), f32)] * 2, "s5_discretise")
    eye = jnp.eye(GROUPS_PER_CHUNK, dtype=f32)
    eye6 = eye[None, None, :, None, :, None]

    def blockdiag_b(bbar):
        t = bbar.reshape(SSM_GROUP, 2, nq, GROUPS_PER_CHUNK, SSM_STATE).transpose(1, 2, 3, 0, 4)
        return (t[:, :, :, :, None, :] * eye6).reshape(2, nq, 128, N_STATE)

    def blockdiag_c(cw):
        t = cw.reshape(2, nq, GROUPS_PER_CHUNK, SSM_GROUP, SSM_STATE).transpose(0, 1, 2, 4, 3)
        return (t[:, :, :, :, None, :] * eye6).reshape(2, nq, N_STATE, 128)

    lam = jnp.stack([lam_r.reshape(2, nq, N_STATE), lam_i.reshape(2, nq, N_STATE)], axis=2)
    bblk = jnp.stack([blockdiag_b(bbar_r), blockdiag_b(bbar_i)], axis=2)
    cblk = jnp.stack([blockdiag_c(sm["c_re"]), blockdiag_c(sm["c_im"])], axis=2)
    bblk16, cblk16 = bblk.astype(bf16), cblk.astype(bf16)
    bblk_t, cblk_t = bblk16.transpose(0, 1, 2, 4, 3), cblk16.transpose(0, 1, 2, 4, 3)
    pm, pmt = _scan_perm()
    hs_p = _row_perm(hs, pm, "l1_s5_perm", out_dtype=bf16, split=False)
    ys, cps = [], []
    for dr in range(2):
        yd, cp = _s5_fwd(hs_p, lam[dr], bblk16[dr], cblk16[dr], n_x, dr == 1, "l1_s5_fwd%d" % dr,
                         rider=wts.take(CARRIER_US["s5_fwd"]))
        ys.append(yd)
        cps.append(cp)
    dvec = sm["ssm_d"].reshape(1, d)
    gy = _glu_pre(hs, ys[0], ys[1], dvec, pmt, n_x, "l1_glu_pre")
    zv = _mm(gy, wts("w_glu_v"), "nn", f32, "l1_glu_val", comm=wts)
    zg = _mm(gy, wts("w_glu_g"), "nn", f32, "l1_glu_gate", comm=wts)
    bv, bg = sm["b_glu"][:d].reshape(1, 1, d), sm["b_glu"][d:].reshape(1, 1, d)
    xd = _rowwise(_f_glu_post, [xc, zv, zg], [mx(1, 2), bv, bg], [(d, f32)], name="l1_glu_post", **kx)[0]
    xe, ffn1 = ffn_fwd(xd, 1, mx, kx, "l1")

    d_xe, d_nfinal, loss_blk = _loss_head(xe, tgt, sm["norm_final"].reshape(1, d), nrows=n_x, name="loss_head")
    d_xd, dnffn1, dsc2_1, dsh2_1, dg2_1 = ffn_bwd(d_xe, xd, ffn1, 1, mx, kx, "l1")
    dzv, dzg, dg1_1, dbv, dbg = _rowwise_bwd(_f_glu_post, [xc, zv, zg], [mx(1, 2), bv, bg], [d_xd],
                                             {1: bf16, 2: bf16}, name="l1_glu_post_bwd", **kx)
    dgy = _mm(dzv, wts("w_glu_v"), "nt", f32, "l1_glu_dx", a2=dzg, b2=wts("w_glu_g"), comm=wts)
    dwglu_v = _mm(gy, dzv, "tn", f32, "l1_glu_val_dw", comm=wts)
    dwglu_g = _mm(gy, dzg, "tn", f32, "l1_glu_gate_dw", comm=wts)
    wts.grad("w_glu", 0, jnp.concatenate([dwglu_v, dwglu_g], axis=1))
    dy_p, du_skip, d_dvec = _glu_pre_bwd(hs, ys[0], ys[1], dvec, pm, pmt, dgy, n_x, "l1_glu_pre_bwd")
    du0, dlam0, db0, dc0 = _s5_bwd(hs_p, dy_p, cps[0], lam[0], bblk16[0], cblk16[0], bblk_t[0], cblk_t[0], n_x,
                                   False, "l1_s5_bwd0", rider=wts.take(CARRIER_US["s5_bwd"]))
    du1, dlam1, db1, dc1 = _s5_bwd(hs_p, dy_p, cps[1], lam[1], bblk16[1], cblk16[1], bblk_t[1], cblk_t[1], n_x,
                                   True, "l1_s5_bwd1", add=du0, rider=wts.take(CARRIER_US["s5_bwd"]))
    d_hs = _row_perm(du1, pmt, "l1_s5_unperm", out_dtype=f32, split=True, add=du_skip, add_rows=n_x)
    d_xc, dnmix1, dsc1_1, dsh1_1 = _rowwise_bwd(
        _f_normmod, [xc], [_vec(nmix[1:2]), mv(1, 1), mv(1, 0)], [d_hs], {0: f32},
        name="l1_norm_mix_bwd", residual=d_xd, residual_rows=n_x, **kw)
    dlam = jnp.stack([dlam0, dlam1])
    dlam_r, dlam_i = dlam[:, :, 0].reshape(2 * g2n, SSM_STATE), dlam[:, :, 1].reshape(2 * g2n, SSM_STATE)
    dbb = jnp.stack([db0, db1]).reshape(2, nq, 2, GROUPS_PER_CHUNK, SSM_GROUP, GROUPS_PER_CHUNK, SSM_STATE)
    eye7 = eye[None, None, None, :, None, :, None]
    dbbar = (dbb * eye7).sum(axis=5).transpose(2, 4, 0, 1, 3, 5).reshape(2, SSM_GROUP, 2 * g2n, SSM_STATE)
    dcc = jnp.stack([dc0, dc1]).reshape(2, nq, 2, GROUPS_PER_CHUNK, SSM_STATE, GROUPS_PER_CHUNK, SSM_GROUP)
    dcw = (dcc * eye7).sum(axis=5).transpose(2, 0, 1, 3, 5, 4).reshape(2, 2, g2n, SSM_GROUP, SSM_STATE)
    d_ar, d_ai, d_ldt, d_btr, d_bti = _whole_bwd(_f_discretise, disc_in, [dlam_r, dlam_i, dbbar[0], dbbar[1]],
                                                 "s5_discretise_bwd")
    to_b = lambda t: t.reshape(SSM_GROUP, 2, g2n, SSM_STATE).transpose(1, 2, 3, 0)
    wts.early(dict(
        loss=loss_blk[0, 0].reshape(1), a_re=d_ar.reshape(sm["a_re"].shape), a_im=d_ai.reshape(sm["a_im"].shape),
        log_dt=d_ldt.reshape(sm["log_dt"].shape), b_re=to_b(d_btr), b_im=to_b(d_bti), c_re=dcw[0], c_im=dcw[1],
        ssm_d=d_dvec.reshape(d), b_glu=jnp.concatenate([dbv.reshape(d), dbg.reshape(d)]),
        norm_final=d_nfinal.reshape(d)))

    d_xb, dnffn0, dsc2_0, dsh2_0, dg2_0 = ffn_bwd(d_xc, xb, ffn0, 0, mv, kw, "l0")
    dy1, dg1_0 = _rowwise_bwd(_f_gated_add, [xa, y1], [mv(0, 2)], [d_xb], {1: bf16}, name="l0_res_mix_bwd", **kw)
    d_o = _mm(dy1, wts("w_out"), "nt", bf16, "l0_out_proj_dx", comm=wts)
    wts.grad("w_out", 0, _mm(o, dy1, "tn", f32, "l0_out_proj_dw", comm=wts))
    dqa, delta_a, dbias = _attn_bwd_q(qkv, oa, d_o, lse_a, cfg_a, bias, "l0_na_bwd_q",
                                      rider=wts.take(CARRIER_US["na_bwd_q"]))
    dka, dva = _attn_bwd_kv(qkv, d_o, lse_a, delta_a, cfg_a, bias_t, "l0_na_bwd_kv",
                            rider=wts.take(CARRIER_US["na_bwd_kv"]))
    dqb, delta_b, dsink_rows = _attn_bwd_q(qkv, ob, d_o, lse_b, cfg_b, sink_rows, "l0_swa_bwd_q",
                                           rider=wts.take(CARRIER_US["swa_bwd_q"]))
    dkb, dvb = _attn_bwd_kv(qkv, d_o, lse_b, delta_b, cfg_b, None, "l0_swa_bwd_kv",
                            rider=wts.take(CARRIER_US["swa_bwd_kv"]))
    d_qkv = _qkv_post_bwd([dqa, dka, dva, dqb, dkb, dvb], cos, sin, "l0_qkv_post_bwd")
    dh1 = _mm(d_qkv, wts("w_in"), "nt", f32, "l0_in_proj_dx", comm=wts)
    wts.grad("w_in", 0, _mm(h1, d_qkv, "tn", f32, "l0_in_proj_dw", comm=wts))
    d_xa, dnmix0, dsc1_0, dsh1_0 = _rowwise_bwd(
        _f_normmod, [xa], [_vec(nmix[0:1]), mv(0, 1), mv(0, 0)], [dh1], {0: f32},
        name="l0_norm_mix_bwd", residual=d_xb, **kw)
    dbias5 = dbias.reshape(NA_HEADS, 4, GRID_W, 12, GRID_W).transpose(0, 1, 3, 2, 4)
    dtz = sum(jnp.pad(dbias5[:, a], ((0, 0), (3 - a, a), (0, 0), (0, 0))) for a in range(4))
    dtz2 = jnp.zeros((128, GRID_W * GRID_W), f32).at[:NA_HEADS * 15].set(dtz.reshape(NA_HEADS * 15, -1))
    d_rpb = _small_dot(dtz2, basis.T, "rpb_expand_bwd")[:NA_HEADS * 15, :31].reshape(sm["rpb"].shape)
    d_sink = dsink_rows.reshape(NB_KV_HEADS * NB_GROUP, BLK).sum(axis=1)

    zero_c = jnp.zeros((1, 1, d), f32)
    both = lambda gx: jnp.concatenate([gx, zero_c], axis=0)
    dmods = jnp.stack([
        jnp.stack([dsh1_0, dsc1_0, dg1_0, dsh2_0, dsc2_0, dg2_0], axis=2),
        jnp.stack([dsh1_1, dsc1_1, both(dg1_1), both(dsh2_1), both(dsc2_1), both(dg2_1)], axis=2),
    ])[:, :, 0]
    late = dict(norm_mix=jnp.concatenate([dnmix0[0], dnmix1[0]]), norm_ffn=jnp.concatenate([dnffn0[0], dnffn1[0]]),
                rpb=d_rpb, sink=d_sink)
    return d_xa, late, dmods


WEIGHTS = ("c_ctx", "ada_w", "ada_b", "norm_mix", "norm_ffn", "ffn_w1", "ffn_w3", "ffn_w2", "attn_w_in",
           "attn_w_out", "attn_rpb", "attn_sink", "ssm_a_re", "ssm_a_im", "ssm_log_dt", "ssm_b_re", "ssm_b_im",
           "ssm_c_re", "ssm_c_im", "ssm_d", "ssm_w_glu", "ssm_b_glu", "norm_final")
SHARDED_BIG = ("ffn_w1", "ffn_w3", "ffn_w2", "attn_w_in", "attn_w_out", "ssm_w_glu")
BIG = ("ada_w",) + SHARDED_BIG
SMALL = tuple(n for n in WEIGHTS if n not in BIG)


BIG_SLABS = dict(w_in=("attn_w_in", "cols"), w_out=("attn_w_out", "rows"), w1=("ffn_w1", "cols"),
                 w3=("ffn_w3", "cols"), w2=("ffn_w2", "rows"), w_glu=("ssm_w_glu", "cols"))
GATHER_ORDER = (("w_in", 0), ("w_out", 0), ("w1", 0), ("w3", 0), ("w2", 0), ("w_glu", 0), ("w1", 1), ("w3", 1),
                ("w2", 1))


class _Weights(_Comm):
    def __init__(self, w, pack_early):
        super().__init__()
        self.w, self.pack_early = w, pack_early
        self.gathers = {key: self.post(w[BIG_SLABS[key[0]][0]][key[1]].astype(bf16), True) for key in GATHER_ORDER}
        self.full, self.scatters, self.early_gather = {}, {}, None

    def __call__(self, name, layer=0):
        key = ("w_glu", 0) if name.startswith("w_glu") else (name, layer)
        if key not in self.full:
            blk = self.finish(self.gathers[key])
            if BIG_SLABS[key[0]][1] == "cols":
                self.full[key] = blk.transpose(1, 0, 2).reshape(blk.shape[1], -1)
            else:
                self.full[key] = blk.reshape(-1, blk.shape[2])
        full = self.full[key]
        if name.startswith("w_glu"):
            half = full.shape[1] // 2
            return full[:, :half] if name == "w_glu_v" else full[:, half:]
        return full

    def grad(self, name, layer, dw):
        if BIG_SLABS[name][1] == "cols":
            blk = dw.reshape(dw.shape[0], N_DEV, -1).transpose(1, 0, 2)
        else:
            blk = dw.reshape(N_DEV, -1, dw.shape[1])
        self.scatters[(name, layer)] = self.post(blk.astype(bf16), False)

    def early(self, parts):
        self.early_shapes = {k: v.shape for k, v in parts.items()}
        self.early_gather = self.post(self.pack_early(parts), True)

    def reduced(self, name):
        short = next(k for k, v in BIG_SLABS.items() if v[0] == name)
        layers = [l for (n, l) in self.scatters if n == short]
        parts = [_sum0(self.finish(self.scatters[(short, l)]), "sum_%s%d" % (short, l)) for l in sorted(layers)]
        return jnp.stack(parts).reshape(self.w[name].shape)


def kernel(x, c, ctx, c_ctx, ada_w, ada_b, norm_mix, norm_ffn, ffn_w1, ffn_w3, ffn_w2, attn_w_in, attn_w_out, attn_rpb, attn_sink, ssm_a_re, ssm_a_im, ssm_log_dt, ssm_b_re, ssm_b_im, ssm_c_re, ssm_c_im, ssm_d, ssm_w_glu, ssm_b_glu, norm_final, loss_target, m_c_ctx, m_ada_w, m_ada_b, m_norm_mix, m_norm_ffn, m_ffn_w1, m_ffn_w3, m_ffn_w2, m_attn_w_in, m_attn_w_out, m_attn_rpb, m_attn_sink, m_ssm_a_re, m_ssm_a_im, m_ssm_log_dt, m_ssm_b_re, m_ssm_b_im, m_ssm_c_re, m_ssm_c_im, m_ssm_d, m_ssm_w_glu, m_ssm_b_glu, m_norm_final, v_c_ctx, v_ada_w, v_ada_b, v_norm_mix, v_norm_ffn, v_ffn_w1, v_ffn_w3, v_ffn_w2, v_attn_w_in, v_attn_w_out, v_attn_rpb, v_attn_sink, v_ssm_a_re, v_ssm_a_im, v_ssm_log_dt, v_ssm_b_re, v_ssm_b_im, v_ssm_c_re, v_ssm_c_im, v_ssm_d, v_ssm_w_glu, v_ssm_b_glu, v_norm_final):
    p = dict(locals())
    w = {n: p[n] for n in WEIGHTS}
    me = 4 * lax.axis_index("x") + 2 * lax.axis_index("y") + lax.axis_index("c")
    n_x, d = x.shape[1], x.shape[2]
    cols = ada_w.shape[2]
    d8 = d // N_DEV

    first = jnp.concatenate([c[0], ssm_d[0], ssm_b_glu[0]])[None]
    g0 = _exchange(first, True, "gather_vectors")[:, 0]
    c_all, d_full, bglu_full = g0[:, :d], g0[:, d:d + d8].reshape(d), g0[:, d + d8:].reshape(2 * d)
    cc = jnp.concatenate([c_all, c_ctx[None], jnp.zeros((16 - N_DEV - 1, d), f32)])
    sc_all = _whole(_f_silu, [cc], [((16, d), f32)], "silu_c")[0]
    my_cols = lambda a: lax.dynamic_slice_in_dim(a, me * cols, cols, axis=a.ndim - 1)
    mod_loc = jnp.stack([_mm(sc_all, ada_w[l], "nn", f32, "ada_mod%d" % l) for l in range(2)])
    mod_loc = mod_loc + my_cols(ada_b)[:, None, :]
    mg = _exchange(mod_loc.reshape(32, cols), True, "gather_mod")
    mod_all = mg.reshape(N_DEV, 2, 16, cols).transpose(1, 2, 0, 3).reshape(2, 16, N_DEV * cols)
    mod_x = lax.dynamic_index_in_dim(mod_all, me, axis=1, keepdims=False)
    mods = jnp.stack([mod_x, mod_all[:, N_DEV]], axis=1).reshape(2, 2, 6, d)

    wts = _Weights(w, lambda parts: _pack(list(parts.values()), f32))
    sm = dict(norm_mix=norm_mix, norm_ffn=norm_ffn, rpb=attn_rpb[0], sink=attn_sink[0], a_re=ssm_a_re[0],
              a_im=ssm_a_im[0], log_dt=ssm_log_dt[0], b_re=ssm_b_re[0], b_im=ssm_b_im[0], c_re=ssm_c_re[0],
              c_im=ssm_c_im[0], ssm_d=d_full, b_glu=bglu_full, norm_final=norm_final)

    xa = jnp.concatenate([x[0], ctx[0]], axis=0)
    d_xa, late, dmods = _local_step(xa, loss_target[0], mods, wts, sm, n_x)

    late_names = ("norm_mix", "norm_ffn", "rpb", "sink")
    parts = [late[n] for n in late_names] + [dmods[:, 0].reshape(2, 6 * d), dmods[:, 1].reshape(2, 6 * d)]
    shapes = [q.shape for q in parts]
    late_gather = wts.post(_pack(parts, f32), True)
    grads = {n: wts.reduced(n) for n in SHARDED_BIG}
    sg = wts.finish(late_gather)
    tot = _unpack(_sum0(sg, "sum_small").reshape(-1), shapes)
    early = _unpack(_sum0(wts.finish(wts.early_gather), "sum_small_early").reshape(-1),
                    list(wts.early_shapes.values()))
    ts = dict(zip(late_names, tot[:len(late_names)]), **dict(zip(wts.early_shapes, early)))
    loss = ts["loss"][0]
    tot_dmod_x, tot_dmod_c = tot[-2], tot[-1]
    off_x = sum(int(np.prod(s)) for s in shapes[:-2])
    dmod_x_all = sg.reshape(N_DEV, -1)[:, off_x:off_x + 12 * d].reshape(N_DEV, 2, 6 * d)
    dm = jnp.concatenate([dmod_x_all.transpose(1, 0, 2), tot_dmod_c[:, None, :],
                          jnp.zeros((2, 16 - N_DEV - 1, 6 * d), f32)], axis=1)
    dm_loc = my_cols(dm)
    grads["ada_w"] = jnp.stack([_mm(sc_all, dm_loc[l], "tn", f32, "ada_w_grad%d" % l) for l in range(2)])
    grads["ada_b"] = tot_dmod_x + tot_dmod_c
    dsc_part = _mm(dm_loc[0], ada_w[0], "nt", f32, "silu_c_grad", a2=dm_loc[1], b2=ada_w[1])[N_DEV:N_DEV + 1]
    dsc = _sum0(_exchange(dsc_part, True, "gather_cctx"), "sum_cctx")
    grads["c_ctx"] = _whole_bwd(_f_silu, [c_ctx[None]], [dsc], "silu_cctx_bwd")[0][0]
    grads.update(norm_mix=ts["norm_mix"], norm_ffn=ts["norm_ffn"], attn_rpb=ts["rpb"][None],
                 attn_sink=ts["sink"][None], ssm_a_re=ts["a_re"][None], ssm_a_im=ts["a_im"][None],
                 ssm_log_dt=ts["log_dt"][None], ssm_b_re=ts["b_re"][None], ssm_b_im=ts["b_im"][None],
                 ssm_c_re=ts["c_re"][None], ssm_c_im=ts["c_im"][None],
                 ssm_d=lax.dynamic_slice_in_dim(ts["ssm_d"], me * d8, d8)[None],
                 ssm_b_glu=lax.dynamic_slice_in_dim(ts["b_glu"], me * 2 * d8, 2 * d8)[None],
                 norm_final=ts["norm_final"])

    delta, new_m, new_v = {}, {}, {}
    for n in BIG:
        two_d = lambda a: a.reshape(-1, a.shape[-1])
        dl, m2, v2 = _adamw(two_d(w[n]), two_d(grads[n]), two_d(p["m_" + n]), two_d(p["v_" + n]), "adamw_" + n)
        delta[n], new_m[n], new_v[n] = (t.reshape(w[n].shape) for t in (dl, m2, v2))
    sshapes = [w[n].shape for n in SMALL]
    packs = [_pack([src[n] for n in SMALL], f32) for src in
             (w, grads, {n: p["m_" + n] for n in SMALL}, {n: p["v_" + n] for n in SMALL})]
    for store, t in zip((delta, new_m, new_v), _adamw(*packs, "adamw_small")):
        store.update(zip(SMALL, _unpack(t.reshape(-1), sshapes)))

    return (loss, d_xa[:n_x][None], *[grads[n] for n in WEIGHTS], *[delta[n] for n in WEIGHTS],
            *[new_m[n] for n in WEIGHTS], *[new_v[n] for n in WEIGHTS])
```

```python
import functools
import math

import numpy as np
import jax
import jax.numpy as jnp
from jax import lax
from jax.experimental import pallas as pl
from jax.experimental.pallas import tpu as pltpu

f32, bf16 = jnp.float32, jnp.bfloat16

HEAD_DIM = 128
GRID_W = 64
NA_HEADS = 8
NB_Q_HEADS = 8
NB_KV_HEADS = 2
NB_GROUP = NB_Q_HEADS // NB_KV_HEADS
NA_ROWS = 8
NA_COLS = 16
SW_RADIUS = 128
ROPE_BASE = 10000.0
SSM_GROUP = 16
SSM_STATE = 64
EPS = 1e-6
NEG_INF = -1e30
ADAM_LR, ADAM_B1, ADAM_B2, ADAM_EPS, ADAM_WD, ADAM_STEP = 0.001, 0.9, 0.999, 1e-08, 0.01, 10

N_DEV = 8
BLK = 256
SCAN_SEG = 8
GROUPS_PER_CHUNK = 16
V7X_VMEM_LIMIT = 56 * 2 ** 20
MM_VMEM_BUDGET = 36 * 2 ** 20
MXU_FLOPS_PER_US = 7e8
CARRIER_US = dict(na_fwd=400, swa_fwd=250, na_bwd_q=430, na_bwd_kv=640, swa_bwd_q=220, swa_bwd_kv=320,
                  s5_fwd=600, s5_bwd=1200)

NT = (((1,), (1,)), ((), ()))
TN = (((0,), (0,)), ((), ()))


def _params(sem, side_effects=False):
    return pltpu.CompilerParams(dimension_semantics=sem, vmem_limit_bytes=V7X_VMEM_LIMIT,
                                has_side_effects=side_effects)


def _pick(n, cands):
    for c in cands:
        if n % c == 0:
            return c
    return n


MESH = pl.DeviceIdType.MESH
MIN_CHUNK_BYTES = 512 * 1024
EXCHANGE_BYTES_PER_US = 74e3
RIDER_OVERHANG = 1.5


def _peers():
    x, y, c = lax.axis_index("x"), lax.axis_index("y"), lax.axis_index("c")
    return 4 * x + 2 * y + c, lambda o: (x ^ (o >> 2), y ^ ((o >> 1) & 1), c ^ (o & 1))


class _Exchange:
    def __init__(self, src, gather):
        self.src, self.gather, self.result = src, gather, None
        self.out_shape = (N_DEV,) + src.shape if gather else src.shape
        slab = self.out_shape[1:]
        slab_bytes = int(np.prod(slab)) * src.dtype.itemsize
        self.nch = next((k for k in (4, 2) if slab[0] % (16 * k) == 0 and slab_bytes // k >= MIN_CHUNK_BYTES), 1)
        self.step = slab[0] // self.nch
        self.cost_us = (N_DEV - 1) * slab_bytes / EXCHANGE_BYTES_PER_US

    def scratch(self):
        return [pltpu.SemaphoreType.DMA((N_DEV - 1, self.nch)), pltpu.SemaphoreType.DMA((N_DEV - 1, self.nch)),
                pltpu.SemaphoreType.DMA(())]

    def _copies(self, src_ref, out_ref, send_sems, recv_sems, local_sem):
        me, peer = _peers()
        local = pltpu.make_async_copy(src_ref if self.gather else src_ref.at[me], out_ref.at[me], local_sem)

        def copy(o, k, src_idx, dst_idx):
            rows = pl.ds(k * self.step, self.step)
            return pltpu.make_async_remote_copy(
                src_ref=(src_ref if self.gather else src_ref.at[src_idx]).at[rows],
                dst_ref=out_ref.at[dst_idx].at[rows],
                send_sem=send_sems.at[o - 1, k], recv_sem=recv_sems.at[o - 1, k],
                device_id=peer(o), device_id_type=MESH)

        pairs = [(o, k) for k in range(self.nch) for o in range(1, N_DEV)]
        sends = [copy(o, k, me ^ o, me) for o, k in pairs]
        recvs = [copy(o, k, me, me ^ o) for o, k in pairs]
        return local, sends, recvs

    def start(self, *refs):
        local, sends, _ = self._copies(*refs)
        local.start()
        for cp in sends:
            cp.start()

    def wait(self, *refs):
        local, sends, recvs = self._copies(*refs)
        for cp in recvs:
            cp.wait_recv()
        for cp in sends:
            cp.wait_send()
        local.wait()


def _run_exchange(ex, name):
    def body(src_ref, out_ref, *sems):
        ex.start(src_ref, out_ref, *sems)
        ex.wait(src_ref, out_ref, *sems)

    ex.result = pl.pallas_call(
        body, name=name, out_shape=jax.ShapeDtypeStruct(ex.out_shape, ex.src.dtype),
        in_specs=[pl.BlockSpec(memory_space=pl.ANY)], out_specs=pl.BlockSpec(memory_space=pl.ANY),
        scratch_shapes=ex.scratch(), compiler_params=pltpu.CompilerParams(has_side_effects=True),
    )(ex.src)
    return ex.result


def _call(body, args, *, name, grid, in_specs, out_specs, out_shape, scratch_shapes=(), sem, rider=None):
    if not rider:
        return pl.pallas_call(body, name=name, grid=grid, in_specs=in_specs, out_specs=out_specs,
                              out_shape=out_shape, scratch_shapes=list(scratch_shapes),
                              compiler_params=_params(sem))(*args)
    n_in, n_out, n_sc, n_ex = len(in_specs), len(out_specs), len(scratch_shapes), len(rider)

    def wrapped(*refs):
        ins, ex_src = refs[:n_in], refs[n_in:n_in + n_ex]
        outs = refs[n_in + n_ex:n_in + n_ex + n_out]
        ex_out = refs[n_in + n_ex + n_out:n_in + 2 * n_ex + n_out]
        scr = refs[n_in + 2 * n_ex + n_out:n_in + 2 * n_ex + n_out + n_sc]
        sems = refs[n_in + 2 * n_ex + n_out + n_sc:]
        ids = [pl.program_id(a) for a in range(len(grid))]
        first = functools.reduce(jnp.logical_and, [i == 0 for i in ids])
        last = functools.reduce(jnp.logical_and, [i == g - 1 for i, g in zip(ids, grid)])

        @pl.when(first)
        def _():
            for e, ex in enumerate(rider):
                ex.start(ex_src[e], ex_out[e], *sems[3 * e:3 * e + 3])

        body(*ins, *outs, *scr)

        @pl.when(last)
        def _():
            for e, ex in enumerate(rider):
                ex.wait(ex_src[e], ex_out[e], *sems[3 * e:3 * e + 3])

    anyspace = pl.BlockSpec(memory_space=pl.ANY)
    res = pl.pallas_call(
        wrapped, name=name, grid=grid, in_specs=list(in_specs) + [anyspace] * n_ex,
        out_specs=list(out_specs) + [anyspace] * n_ex,
        out_shape=list(out_shape) + [jax.ShapeDtypeStruct(ex.out_shape, ex.src.dtype) for ex in rider],
        scratch_shapes=list(scratch_shapes) + [s for ex in rider for s in ex.scratch()],
        compiler_params=_params(("arbitrary",) * len(grid), side_effects=True),
    )(*args, *[ex.src for ex in rider])
    for ex, r in zip(rider, res[n_out:]):
        ex.result = r
    return list(res[:n_out])


class _Comm:
    def __init__(self):
        self.queue, self.count = [], 0

    def post(self, src, gather):
        ex = _Exchange(src, gather)
        self.queue.append(ex)
        return ex

    def take(self, budget_us):
        rider = []
        for ex in list(self.queue):
            if budget_us > 0 and ex.cost_us <= RIDER_OVERHANG * budget_us:
                budget_us -= ex.cost_us
                rider.append(ex)
                self.queue.remove(ex)
        return rider

    def finish(self, ex):
        while ex.result is None:
            self.count += 1
            _run_exchange(self.queue.pop(0), "exchange%d" % self.count)
        return ex.result


def _mm(a, b, mode, out_dtype, name, a2=None, b2=None, comm=None):
    if mode == "tn":
        kdim, m, n = a.shape[0], a.shape[1], b.shape[1]
    elif mode == "nt":
        m, kdim, n = a.shape[0], a.shape[1], b.shape[0]
    else:
        m, kdim, n = a.shape[0], a.shape[1], b.shape[1]
    if mode == "tn":
        tm = _pick(m, (2048, 1024, 512, 256, 128))
        tk = _pick(kdim, (1024, 768, 512, 256, 128))
    else:
        tm = _pick(m, (1024, 768, 512, 256, 128))
        tk = _pick(kdim, (2048, 1024, 768, 512, 256, 128))
    in_bytes = max(a.dtype.itemsize, b.dtype.itemsize) * (1 if a2 is None else 2)
    out_bytes = jnp.dtype(out_dtype).itemsize

    def vmem(tn_):
        return 2 * in_bytes * tk * (tm + tn_) + (4 + 2 * out_bytes) * tm * tn_

    tn = next((t for t in (2048, 1024, 512, 256, 128) if n % t == 0 and vmem(t) <= MM_VMEM_BUDGET), n)
    nk = kdim // tk
    if mode == "tn":
        a_spec = pl.BlockSpec((tk, tm), lambda i, j, k: (k, i))
        b_spec = pl.BlockSpec((tk, tn), lambda i, j, k: (k, j))
        dims = TN
    elif mode == "nt":
        a_spec = pl.BlockSpec((tm, tk), lambda i, j, k: (i, k))
        b_spec = pl.BlockSpec((tn, tk), lambda i, j, k: (j, k))
        dims = NT
    else:
        a_spec = pl.BlockSpec((tm, tk), lambda i, j, k: (i, k))
        b_spec = pl.BlockSpec((tk, tn), lambda i, j, k: (k, j))
        dims = (((1,), (0,)), ((), ()))

    pairs = 1 if a2 is None else 2

    def body(*refs):
        o_ref, acc_ref = refs[2 * pairs:]
        k = pl.program_id(2)

        @pl.when(k == 0)
        def _():
            acc_ref[...] = jnp.zeros_like(acc_ref)

        for p in range(pairs):
            acc_ref[...] += lax.dot_general(refs[2 * p][...].astype(bf16), refs[2 * p + 1][...].astype(bf16), dims,
                                            preferred_element_type=f32)

        @pl.when(k == nk - 1)
        def _():
            o_ref[...] = acc_ref[...].astype(o_ref.dtype)

    args = (a, b) if a2 is None else (a, b, a2, b2)
    rider = comm.take(2.0 * pairs * m * n * kdim / MXU_FLOPS_PER_US) if comm else None
    return _call(
        body, args, name=name, grid=(m // tm, n // tn, nk),
        in_specs=[a_spec, b_spec] * pairs, out_specs=[pl.BlockSpec((tm, tn), lambda i, j, k: (i, j))],
        out_shape=[jax.ShapeDtypeStruct((m, n), out_dtype)],
        scratch_shapes=[pltpu.VMEM((tm, tn), f32)], sem=("parallel", "parallel", "arbitrary"), rider=rider)[0]


def _ffn_up(h, w1, w3, name, comm=None):
    m, kdim = h.shape
    n = w1.shape[1]
    tm, tn = _pick(m, (1024, 768, 512, 256, 128)), _pick(n, (512, 256, 128))

    def body(h_ref, w1_ref, w3_ref, a_ref, b_ref, u_ref):
        hv = h_ref[...]
        a = jnp.dot(hv, w1_ref[...], preferred_element_type=f32).astype(bf16)
        b = jnp.dot(hv, w3_ref[...], preferred_element_type=f32).astype(bf16)
        a_ref[...] = a
        b_ref[...] = b
        u_ref[...] = (jax.nn.silu(a.astype(f32)) * b.astype(f32)).astype(bf16)

    wspec = pl.BlockSpec((kdim, tn), lambda i, j: (0, j))
    ospec = pl.BlockSpec((tm, tn), lambda i, j: (i, j))
    rider = comm.take(4.0 * m * n * kdim / MXU_FLOPS_PER_US) if comm else None
    return _call(
        body, (h, w1, w3), name=name, grid=(m // tm, n // tn),
        in_specs=[pl.BlockSpec((tm, kdim), lambda i, j: (i, 0)), wspec, wspec], out_specs=[ospec] * 3,
        out_shape=[jax.ShapeDtypeStruct((m, n), bf16)] * 3, sem=("parallel", "parallel"), rider=rider)


def _ffn_down_bwd(g, w2, a, b, name, comm=None):
    m, kdim = g.shape
    n = w2.shape[0]
    tm, tn = _pick(m, (1024, 768, 512, 256, 128)), _pick(n, (512, 256, 128))

    def body(g_ref, w_ref, a_ref, b_ref, da_ref, db_ref):
        du = lax.dot_general(g_ref[...], w_ref[...], NT, preferred_element_type=f32)
        _, vjp = jax.vjp(lambda p, q: jax.nn.silu(p) * q, a_ref[...].astype(f32), b_ref[...].astype(f32))
        da, db = vjp(du)
        da_ref[...] = da.astype(bf16)
        db_ref[...] = db.astype(bf16)

    ospec = pl.BlockSpec((tm, tn), lambda i, j: (i, j))
    rider = comm.take(2.0 * m * n * kdim / MXU_FLOPS_PER_US) if comm else None
    return _call(
        body, (g, w2, a, b), name=name, grid=(m // tm, n // tn),
        in_specs=[pl.BlockSpec((tm, kdim), lambda i, j: (i, 0)), pl.BlockSpec((tn, kdim), lambda i, j: (j, 0)),
                  ospec, ospec],
        out_specs=[ospec] * 2, out_shape=[jax.ShapeDtypeStruct((m, n), bf16)] * 2,
        sem=("parallel", "parallel"), rider=rider)


def _small_dot(a, b, name):
    def body(a_ref, b_ref, o_ref):
        o_ref[...] = jnp.dot(a_ref[...], b_ref[...], precision=lax.Precision.HIGHEST, preferred_element_type=f32)

    return pl.pallas_call(body, name=name, out_shape=jax.ShapeDtypeStruct((a.shape[0], b.shape[1]), f32))(a, b)


def _group_of(i, n_x_tiles, n_groups):
    return jnp.where(i >= n_x_tiles, n_groups - 1, 0)


def _rowwise(f, rows, vecs, outs, *, nrows, n_x_rows, name, tm=BLK, tc=None):
    n_x_tiles = n_x_rows // tm
    grid = (nrows // tm,) if tc is None else (nrows // tm, rows[0].shape[1] // tc)

    def rspec(cols):
        if tc is None:
            return pl.BlockSpec((tm, cols), lambda i: (i, 0))
        return pl.BlockSpec((tm, tc), lambda i, j: (i, j))

    def vspec(v):
        g = v.shape[0]
        if tc is None:
            return pl.BlockSpec((None, 1, v.shape[2]), lambda i: (_group_of(i, n_x_tiles, g), 0, 0))
        return pl.BlockSpec((None, 1, tc), lambda i, j: (_group_of(i, n_x_tiles, g), 0, j))

    nr, nv = len(rows), len(vecs)

    def body(*refs):
        ins = [r[...] for r in refs[:nr + nv]]
        res = f(*ins)
        for o_ref, val in zip(refs[nr + nv:], res):
            o_ref[...] = val.astype(o_ref.dtype)

    return pl.pallas_call(
        body, name=name, grid=grid,
        in_specs=[rspec(r.shape[1]) for r in rows] + [vspec(v) for v in vecs],
        out_specs=[rspec(c) for c, _ in outs],
        out_shape=[jax.ShapeDtypeStruct((nrows, c), d) for c, d in outs],
        compiler_params=_params(("parallel",) * len(grid)),
    )(*rows, *vecs)


def _rowwise_bwd(f, rows, vecs, cts, row_grads, *, nrows, n_x_rows, name, tm=BLK, tc=None, residual=None,
                 residual_rows=None):
    n_x_tiles = n_x_rows // tm
    n_tiles = nrows // tm
    res_tiles = None if residual_rows is None else residual_rows // tm
    grid = (n_tiles,) if tc is None else (rows[0].shape[1] // tc, n_tiles)
    row_of = (lambda *g: g[0]) if tc is None else (lambda *g: g[1])

    def rspec(cols):
        if tc is None:
            return pl.BlockSpec((tm, cols), lambda i: (i, 0))
        return pl.BlockSpec((tm, tc), lambda j, i: (i, j))

    def vspec(v):
        g = v.shape[0]
        if tc is None:
            return pl.BlockSpec((None, 1, v.shape[2]), lambda i: (_group_of(i, n_x_tiles, g), 0, 0))
        return pl.BlockSpec((None, 1, tc), lambda j, i: (_group_of(i, n_x_tiles, g), 0, j))

    nr, nv, nc = len(rows), len(vecs), len(cts)
    gidx = sorted(row_grads)

    def body(*refs):
        i = row_of(*[pl.program_id(d) for d in range(len(grid))])
        row_vals = [r[...].astype(f32) for r in refs[:nr]]
        vec_vals = [jnp.broadcast_to(r[...].astype(f32), (tm, r.shape[-1])) for r in refs[nr:nr + nv]]
        ct_vals = [r[...].astype(f32) for r in refs[nr + nv:nr + nv + nc]]
        out_refs = refs[nr + nv + nc + (residual is not None):]
        res, vjp = jax.vjp(lambda *a: tuple(o.astype(f32) for o in f(*a)), *row_vals, *vec_vals)
        grads = list(vjp(tuple(ct_vals)))
        if residual is not None:
            extra = refs[nr + nv + nc][...]
            if res_tiles is not None:
                extra = jnp.where(i < res_tiles, extra, 0.0)
            grads[0] = grads[0] + extra
        for o_ref, k in zip(out_refs[:len(gidx)], gidx):
            o_ref[...] = grads[k].astype(o_ref.dtype)
        for o_ref, g, v in zip(out_refs[len(gidx):], grads[nr:], vecs):
            part = jnp.sum(g, axis=0, keepdims=True)
            first = (i == 0) if v.shape[0] == 1 else ((i == 0) | (i == n_x_tiles))

            @pl.when(first)
            def _():
                o_ref[...] = part

            @pl.when(jnp.logical_not(first))
            def _():
                o_ref[...] += part

    sem = ("arbitrary",) if tc is None else ("parallel", "arbitrary")
    res_specs, res_args = [], []
    if residual is not None:
        assert tc is None
        clamp = (lambda i: i) if res_tiles is None else (lambda i: jnp.minimum(i, res_tiles - 1))
        res_specs = [pl.BlockSpec((tm, residual.shape[1]), lambda i: (clamp(i), 0))]
        res_args = [residual]
    return pl.pallas_call(
        body, name=name, grid=grid,
        in_specs=[rspec(r.shape[1]) for r in rows] + [vspec(v) for v in vecs] + [rspec(c.shape[1]) for c in cts]
        + res_specs,
        out_specs=[rspec(rows[k].shape[1]) for k in gidx] + [vspec(v) for v in vecs],
        out_shape=[jax.ShapeDtypeStruct((nrows, rows[k].shape[1]), row_grads[k]) for k in gidx]
        + [jax.ShapeDtypeStruct(v.shape, f32) for v in vecs],
        compiler_params=_params(sem),
    )(*rows, *vecs, *cts, *res_args)


def _rms(x, w):
    return x * lax.rsqrt(jnp.mean(x * x, axis=-1, keepdims=True) + EPS) * w


def _f_normmod(x, w, sc, sh):
    return (_rms(x.astype(f32), w) * (1.0 + sc) + sh,)


def _f_gated_add(x, y, g):
    return (x + g * y.astype(f32),)


def _f_glu_post(x, zv, zg, g, bv, bg):
    return (x + g * ((zv.astype(f32) + bv) * jax.nn.sigmoid(zg.astype(f32) + bg)),)


def _f_silu(x):
    return (jax.nn.silu(x.astype(f32)),)


def _loss_head(x, tgt, w, *, nrows, name):
    d = x.shape[1]
    tm = BLK

    def body(x_ref, t_ref, w_ref, dx_ref, dw_ref, loss_ref):
        i = pl.program_id(0)
        wb = jnp.broadcast_to(w_ref[...], (tm, d))
        y, vjp = jax.vjp(_rms, x_ref[...], wb)
        e = y - t_ref[...]
        dx, dwb = vjp(e * (1.0 / d))
        dx_ref[...] = dx
        dw = jnp.sum(dwb, axis=0, keepdims=True)
        part = jnp.full((8, 128), 0.5 / d, f32) * jnp.sum(e * e)

        @pl.when(i == 0)
        def _():
            dw_ref[...] = dw
            loss_ref[...] = part

        @pl.when(i > 0)
        def _():
            dw_ref[...] += dw
            loss_ref[...] += part

    row = pl.BlockSpec((tm, d), lambda i: (i, 0))
    return pl.pallas_call(
        body, name=name, grid=(nrows // tm,),
        in_specs=[row, row, pl.BlockSpec((1, d), lambda i: (0, 0))],
        out_specs=[row, pl.BlockSpec((1, d), lambda i: (0, 0)), pl.BlockSpec((8, 128), lambda i: (0, 0))],
        out_shape=[jax.ShapeDtypeStruct((nrows, d), f32), jax.ShapeDtypeStruct((1, d), f32),
                   jax.ShapeDtypeStruct((8, 128), f32)],
        compiler_params=_params(("arbitrary",)),
    )(x, tgt, w)


def _adamw(w, g, m, v, name):
    r, c = w.shape
    tr = _pick(r, (512, 256, 128, 64, 32, 16, 8))
    tcol = _pick(c, (1024, 512)) if c % 128 == 0 else c

    def body(w_ref, g_ref, m_ref, v_ref, d_ref, m2_ref, v2_ref):
        gg = g_ref[...]
        m2 = ADAM_B1 * m_ref[...] + (1.0 - ADAM_B1) * gg
        v2 = ADAM_B2 * v_ref[...] + (1.0 - ADAM_B2) * (gg * gg)
        m_hat = m2 / (1.0 - ADAM_B1 ** ADAM_STEP)
        v_hat = v2 / (1.0 - ADAM_B2 ** ADAM_STEP)
        d_ref[...] = -ADAM_LR * (m_hat / (jnp.sqrt(v_hat) + ADAM_EPS) + ADAM_WD * w_ref[...])
        m2_ref[...] = m2
        v2_ref[...] = v2

    spec = pl.BlockSpec((tr, tcol), lambda i, j: (i, j))
    return pl.pallas_call(
        body, name=name, grid=(r // tr, c // tcol), in_specs=[spec] * 4, out_specs=[spec] * 3,
        out_shape=[jax.ShapeDtypeStruct((r, c), f32)] * 3,
        compiler_params=_params(("parallel", "parallel")),
    )(w, g, m, v)


def _swap_quarters(x):
    lane = lax.broadcasted_iota(jnp.int32, x.shape, 1)
    return jnp.where((lane & 63) < 32, pltpu.roll(x, 96, 1), pltpu.roll(x, 32, 1))


def _rope_tables(n_x, n_ctx):
    t = np.arange(n_x)
    quarter = HEAD_DIM // 4
    inv = ROPE_BASE ** (-np.arange(quarter, dtype=np.float64) / quarter)
    ar = (t // GRID_W)[:, None] * inv[None]
    ac = (t % GRID_W)[:, None] * inv[None]
    cos = np.concatenate([np.cos(ar), np.cos(ar), np.cos(ac), np.cos(ac)], axis=1)
    sin = np.concatenate([-np.sin(ar), np.sin(ar), -np.sin(ac), np.sin(ac)], axis=1)
    cos = np.concatenate([cos, np.ones((n_ctx, HEAD_DIM))], axis=0)
    sin = np.concatenate([sin, np.zeros((n_ctx, HEAD_DIM))], axis=0)
    return jnp.asarray(cos, f32), jnp.asarray(sin, f32)


A_W = NA_HEADS * HEAD_DIM
QB0, KB0, VB0 = 3 * A_W, 3 * A_W + NB_Q_HEADS * HEAD_DIM, 3 * A_W + (NB_Q_HEADS + NB_KV_HEADS) * HEAD_DIM
IN_W = VB0 + NB_KV_HEADS * HEAD_DIM


def _qkv_post(qkv, cos, sin, name):
    n = qkv.shape[0]

    def body(x_ref, c_ref, s_ref, o_ref):
        c, s = c_ref[...], s_ref[...]
        o_ref[:, :QB0] = x_ref[:, :QB0].astype(bf16)
        for col in range(QB0, VB0, HEAD_DIM):
            x = x_ref[:, col:col + HEAD_DIM]
            o_ref[:, col:col + HEAD_DIM] = (x * c + _swap_quarters(x) * s).astype(bf16)
        o_ref[:, VB0:] = x_ref[:, VB0:].astype(bf16)

    row = lambda c: pl.BlockSpec((BLK, c), lambda i: (i, 0))
    return pl.pallas_call(
        body, name=name, grid=(n // BLK,), in_specs=[row(IN_W), row(HEAD_DIM), row(HEAD_DIM)],
        out_specs=row(IN_W), out_shape=jax.ShapeDtypeStruct((n, IN_W), bf16),
        compiler_params=_params(("parallel",)),
    )(qkv, cos, sin)


def _qkv_post_bwd(parts, cos, sin, name):
    n = parts[0].shape[0]

    def body(qa, ka, va, qb, kb, vb, c_ref, s_ref, o_ref):
        c, s = c_ref[...], s_ref[...]
        o_ref[:, 0:A_W] = qa[...]
        o_ref[:, A_W:2 * A_W] = ka[...]
        o_ref[:, 2 * A_W:QB0] = va[...]
        for src, col0, width in ((qb, QB0, KB0 - QB0), (kb, KB0, VB0 - KB0)):
            for off in range(0, width, HEAD_DIM):
                g = src[:, off:off + HEAD_DIM].astype(f32)
                o_ref[:, col0 + off:col0 + off + HEAD_DIM] = (g * c + _swap_quarters(g * s)).astype(bf16)
        o_ref[:, VB0:] = vb[...]

    row = lambda c: pl.BlockSpec((BLK, c), lambda i: (i, 0))
    return pl.pallas_call(
        body, name=name, grid=(n // BLK,),
        in_specs=[row(p.shape[1]) for p in parts] + [row(HEAD_DIM), row(HEAD_DIM)],
        out_specs=row(IN_W), out_shape=jax.ShapeDtypeStruct((n, IN_W), bf16),
        compiler_params=_params(("parallel",)),
    )(*parts, cos, sin)


def _valid(kind, qpos, kpos, n_x):
    ok = (kpos >= 0) & (kpos < n_x) & (qpos >= 0) & (qpos < n_x)
    if kind == "na":
        rows = n_x // GRID_W
        qr, qc = lax.shift_right_arithmetic(qpos, 6), qpos & (GRID_W - 1)
        kr, kc = lax.shift_right_arithmetic(kpos, 6), kpos & (GRID_W - 1)
        kr0 = jnp.clip(qr - NA_ROWS // 2, 0, rows - NA_ROWS)
        ws = jnp.clip(qc - NA_COLS // 2, 0, GRID_W - NA_COLS)
        return ok & (kr >= kr0) & (kr < kr0 + NA_ROWS) & (kc >= ws) & (kc < ws + NA_COLS)
    return ok & (jnp.abs(kpos - qpos) <= SW_RADIUS)


def _stack_heads(x, g):
    if g == 1:
        return x
    return jnp.concatenate([x[:, a * HEAD_DIM:(a + 1) * HEAD_DIM] for a in range(g)], axis=0)


def _unstack_heads(x, g):
    if g == 1:
        return x
    r = x.shape[0] // g
    return jnp.concatenate([x[a * r:(a + 1) * r] for a in range(g)], axis=1)


def _add_per_head(s, m, g):
    if g == 1:
        return s + m
    return (s.reshape((g,) + m.shape) + m[None]).reshape(s.shape)


class _AttnCfg:
    def __init__(self, kind, n_x, n_tot):
        self.kind, self.n_x, self.n_tot = kind, n_x, n_tot
        self.n_xb, self.n_blk = n_x // BLK, n_tot // BLK
        if kind == "na":
            self.g, self.nkv, self.q0, self.k0, self.v0 = 1, NA_HEADS, 0, NA_HEADS, 2 * NA_HEADS
        else:
            self.g, self.nkv = NB_GROUP, NB_KV_HEADS
            self.q0, self.k0, self.v0 = QB0 // (NB_GROUP * HEAD_DIM), KB0 // HEAD_DIM, VB0 // HEAD_DIM
        self.r = BLK * self.g
        self.qw = HEAD_DIM * self.g
        self.scale = HEAD_DIM ** -0.5
        last = self.n_xb - 1
        rows = lax.broadcasted_iota(jnp.int32, (BLK, 3 * BLK), 0)
        cols = lax.broadcasted_iota(jnp.int32, (BLK, 3 * BLK), 1)
        self.mask_q = jnp.stack([jnp.where(_valid(kind, i * BLK + rows, (i - 1) * BLK + cols, n_x), 0.0, NEG_INF)
                                 for i in (0, 1, last, self.n_xb)]).astype(f32)
        self.mask_k = jnp.stack([jnp.where(_valid(kind, (j - 1) * BLK + cols.T, j * BLK + rows.T, n_x), 0.0, NEG_INF)
                                 for j in (0, 1, 2, last - 1, last)]).astype(f32)
        self.pattern = lambda i: jnp.where(i == 0, 0, jnp.where(i < last, 1, jnp.where(i == last, 2, 3)))
        self.pattern_k = lambda j: jnp.where(j <= 1, j, jnp.where(j < last - 1, 2, jnp.where(j == last - 1, 3, 4)))


def _attn_fwd(qkv, cfg, extra, name, rider=None):
    g, r, qw, n_xb, n_x = cfg.g, cfg.r, cfg.qw, cfg.n_xb, cfg.n_x
    last = n_xb - 1

    def body(q_ref, kp, ko, kn, vp, vo, vn, kc_ref, vc_ref, ex_ref, m_ref, o_ref, lse_ref):
        q = _stack_heads(q_ref[...], g)
        kw = jnp.concatenate([kp[...], ko[...], kn[...]], axis=0)
        vw = jnp.concatenate([vp[...], vo[...], vn[...]], axis=0)
        s = lax.dot_general(q, kw, NT, preferred_element_type=f32) * cfg.scale
        if cfg.kind == "na":
            s = s + ex_ref[...]
        s = _add_per_head(s, m_ref[...], g)
        sc = lax.dot_general(q, kc_ref[...], NT, preferred_element_type=f32) * cfg.scale
        m = jnp.maximum(jnp.max(s, axis=-1, keepdims=True), jnp.max(sc, axis=-1, keepdims=True))
        if cfg.kind == "swa":
            m = jnp.maximum(m, ex_ref[...])
        p, pc = jnp.exp(s - m), jnp.exp(sc - m)
        l = jnp.sum(p, axis=-1, keepdims=True) + jnp.sum(pc, axis=-1, keepdims=True)
        if cfg.kind == "swa":
            l = l + jnp.exp(ex_ref[...] - m)
        o = jnp.dot(p.astype(bf16), vw, preferred_element_type=f32) + jnp.dot(pc.astype(bf16), vc_ref[...],
                                                                             preferred_element_type=f32)
        o_ref[...] = _unstack_heads(o / l, g).astype(bf16)
        lse_ref[...] = m + jnp.log(l)

    kv = lambda col0, f: pl.BlockSpec((BLK, HEAD_DIM), lambda h, i: (f(i), col0 + h))
    prev = lambda i: jnp.clip(i - 1, 0, last)
    own = lambda i: jnp.minimum(i, last)
    nxt = lambda i: jnp.minimum(i + 1, last)
    ctx = lambda i: n_xb
    if cfg.kind == "na":
        ex_spec = pl.BlockSpec((None, BLK, 3 * BLK), lambda h, i: (h, 0, 0))
    else:
        ex_spec = pl.BlockSpec((None, r, 1), lambda h, i: (h, 0, 0))
    m_spec = pl.BlockSpec((None, BLK, 3 * BLK), lambda h, i: (cfg.pattern(i), 0, 0))
    return _call(
        body, (qkv,) * 9 + (extra, cfg.mask_q), name=name, grid=(cfg.nkv, cfg.n_blk),
        in_specs=[pl.BlockSpec((BLK, qw), lambda h, i: (i, cfg.q0 + h)),
                  kv(cfg.k0, prev), kv(cfg.k0, own), kv(cfg.k0, nxt),
                  kv(cfg.v0, prev), kv(cfg.v0, own), kv(cfg.v0, nxt),
                  kv(cfg.k0, ctx), kv(cfg.v0, ctx), ex_spec, m_spec],
        out_specs=[pl.BlockSpec((BLK, qw), lambda h, i: (i, h)),
                   pl.BlockSpec((None, None, r, 1), lambda h, i: (h, i, 0, 0))],
        out_shape=[jax.ShapeDtypeStruct((cfg.n_tot, cfg.nkv * qw), bf16),
                   jax.ShapeDtypeStruct((cfg.nkv, cfg.n_blk, r, 1), f32)],
        sem=("parallel", "parallel"), rider=rider)


def _attn_bwd_q(qkv, o, do, lse, cfg, extra, name, rider=None):
    g, r, qw, n_xb, n_x = cfg.g, cfg.r, cfg.qw, cfg.n_xb, cfg.n_x
    last = n_xb - 1
    do_col0 = 0 if cfg.kind == "na" else (NA_HEADS * HEAD_DIM) // qw

    def body(q_ref, kp, ko, kn, vp, vo, vn, kc_ref, vc_ref, ex_ref, o_ref, do_ref, lse_ref, m_ref,
             dq_ref, delta_ref, dex_ref):
        i = pl.program_id(1)
        q = _stack_heads(q_ref[...], g)
        dout = _stack_heads(do_ref[...], g)
        out = _stack_heads(o_ref[...], g)
        delta = jnp.sum(dout.astype(f32) * out.astype(f32), axis=-1, keepdims=True)
        delta_ref[...] = delta
        kw = jnp.concatenate([kp[...], ko[...], kn[...]], axis=0)
        vw = jnp.concatenate([vp[...], vo[...], vn[...]], axis=0)
        s = lax.dot_general(q, kw, NT, preferred_element_type=f32) * cfg.scale
        if cfg.kind == "na":
            s = s + ex_ref[...]
        s = _add_per_head(s, m_ref[...], g)
        sc = lax.dot_general(q, kc_ref[...], NT, preferred_element_type=f32) * cfg.scale
        lse_v = lse_ref[...]
        p, pc = jnp.exp(s - lse_v), jnp.exp(sc - lse_v)
        dp = lax.dot_general(dout, vw, NT, preferred_element_type=f32)
        dpc = lax.dot_general(dout, vc_ref[...], NT, preferred_element_type=f32)
        ds, dsc = p * (dp - delta), pc * (dpc - delta)
        dq = jnp.dot(ds.astype(bf16), kw, preferred_element_type=f32) + jnp.dot(dsc.astype(bf16), kc_ref[...],
                                                                             preferred_element_type=f32)
        dq_ref[...] = _unstack_heads(dq * cfg.scale, g).astype(bf16)
        dex = ds if cfg.kind == "na" else -jnp.exp(ex_ref[...] - lse_v) * delta

        @pl.when(i == 0)
        def _():
            dex_ref[...] = dex

        @pl.when(i > 0)
        def _():
            dex_ref[...] += dex

    kv = lambda col0, f: pl.BlockSpec((BLK, HEAD_DIM), lambda h, i: (f(i), col0 + h))
    prev = lambda i: jnp.clip(i - 1, 0, last)
    own = lambda i: jnp.minimum(i, last)
    nxt = lambda i: jnp.minimum(i + 1, last)
    ctx = lambda i: n_xb
    if cfg.kind == "na":
        ex_spec = pl.BlockSpec((None, BLK, 3 * BLK), lambda h, i: (h, 0, 0))
    else:
        ex_spec = pl.BlockSpec((None, r, 1), lambda h, i: (h, 0, 0))
    stat = pl.BlockSpec((None, None, r, 1), lambda h, i: (h, i, 0, 0))
    m_spec = pl.BlockSpec((None, BLK, 3 * BLK), lambda h, i: (cfg.pattern(i), 0, 0))
    return _call(
        body, (qkv,) * 9 + (extra, o, do, lse, cfg.mask_q), name=name, grid=(cfg.nkv, cfg.n_blk),
        in_specs=[pl.BlockSpec((BLK, qw), lambda h, i: (i, cfg.q0 + h)),
                  kv(cfg.k0, prev), kv(cfg.k0, own), kv(cfg.k0, nxt),
                  kv(cfg.v0, prev), kv(cfg.v0, own), kv(cfg.v0, nxt),
                  kv(cfg.k0, ctx), kv(cfg.v0, ctx), ex_spec,
                  pl.BlockSpec((BLK, qw), lambda h, i: (i, h)),
                  pl.BlockSpec((BLK, qw), lambda h, i: (i, do_col0 + h)), stat, m_spec],
        out_specs=[pl.BlockSpec((BLK, qw), lambda h, i: (i, h)), stat, ex_spec],
        out_shape=[jax.ShapeDtypeStruct((cfg.n_tot, cfg.nkv * qw), bf16),
                   jax.ShapeDtypeStruct((cfg.nkv, cfg.n_blk, r, 1), f32),
                   jax.ShapeDtypeStruct(extra.shape, f32)],
        sem=("parallel", "arbitrary"), rider=rider)


def _attn_bwd_kv(qkv, do, lse, delta, cfg, bias_t, name, rider=None):
    g, r, qw, n_xb, n_x, n_blk = cfg.g, cfg.r, cfg.qw, cfg.n_xb, cfg.n_x, cfg.n_blk
    last = n_xb - 1
    do_col0 = 0 if cfg.kind == "na" else (NA_HEADS * HEAD_DIM) // qw
    has_bias = cfg.kind == "na"

    def body(*refs):
        qs, dos, lses, dels = refs[0:3], refs[3:6], refs[6:9], refs[9:12]
        qj_ref, doj_ref, lsej_ref, delj_ref, k_ref, v_ref, kc_ref, vc_ref, m_ref = refs[12:21]
        rest = refs[21:]
        if has_bias:
            b_ref, rest = rest[0], rest[1:]
        dk_ref, dv_ref, dkc_acc, dvc_acc = rest
        j = pl.program_id(1)

        @pl.when(j == 0)
        def _():
            dkc_acc[...] = jnp.zeros_like(dkc_acc)
            dvc_acc[...] = jnp.zeros_like(dvc_acc)

        qj, doj = _stack_heads(qj_ref[...], g), _stack_heads(doj_ref[...], g)
        sc = lax.dot_general(qj, kc_ref[...], NT, preferred_element_type=f32) * cfg.scale
        pc = jnp.exp(sc - lsej_ref[...])
        dvc_acc[...] += lax.dot_general(pc.astype(bf16), doj, TN, preferred_element_type=f32)
        dpc = lax.dot_general(doj, vc_ref[...], NT, preferred_element_type=f32)
        dsc = pc * (dpc - delj_ref[...])
        dkc_acc[...] += lax.dot_general(dsc.astype(bf16), qj, TN, preferred_element_type=f32) * cfg.scale

        @pl.when(j < n_xb)
        def _():
            qw_all = jnp.concatenate([_stack_heads(x[...], g) for x in qs], axis=0)
            do_all = jnp.concatenate([_stack_heads(x[...], g) for x in dos], axis=0)
            lse_all = jnp.concatenate([x[...] for x in lses], axis=0)
            del_all = jnp.concatenate([x[...] for x in dels], axis=0)
            s = lax.dot_general(qw_all, k_ref[...], NT, preferred_element_type=f32) * cfg.scale
            if has_bias:
                s = s + b_ref[...]
            if g == 1:
                s = s + m_ref[...]
            else:
                s = (s.reshape(3, g, BLK, BLK) + m_ref[...].reshape(3, 1, BLK, BLK)).reshape(3 * r, BLK)
            p = jnp.exp(s - lse_all)
            dv_ref[...] = lax.dot_general(p.astype(bf16), do_all, TN, preferred_element_type=f32).astype(bf16)
            dp = lax.dot_general(do_all, v_ref[...], NT, preferred_element_type=f32)
            ds = p * (dp - del_all)
            dk_ref[...] = (lax.dot_general(ds.astype(bf16), qw_all, TN, preferred_element_type=f32)
                           * cfg.scale).astype(bf16)

        @pl.when(j == n_xb)
        def _():
            dk_ref[...] = dkc_acc[...].astype(bf16)
            dv_ref[...] = dvc_acc[...].astype(bf16)

    prev = lambda j: jnp.clip(j - 1, 0, last)
    own = lambda j: jnp.minimum(j, last)
    nxt = lambda j: jnp.minimum(j + 1, last)
    same = lambda j: j
    qspec = lambda f: pl.BlockSpec((BLK, qw), lambda h, j: (f(j), cfg.q0 + h))
    dospec = lambda f: pl.BlockSpec((BLK, qw), lambda h, j: (f(j), do_col0 + h))
    stat = lambda f: pl.BlockSpec((None, None, r, 1), lambda h, j: (h, f(j), 0, 0))
    kv = lambda col0, f: pl.BlockSpec((BLK, HEAD_DIM), lambda h, j: (f(j), col0 + h))
    ctx = lambda j: n_xb
    in_specs = ([qspec(f) for f in (prev, own, nxt)] + [dospec(f) for f in (prev, own, nxt)]
                + [stat(f) for f in (prev, own, nxt)] * 2
                + [qspec(same), dospec(same), stat(same), stat(same),
                   kv(cfg.k0, same), kv(cfg.v0, same), kv(cfg.k0, ctx), kv(cfg.v0, ctx),
                   pl.BlockSpec((None, 3 * BLK, BLK), lambda h, j: (cfg.pattern_k(j), 0, 0))])
    args = [qkv] * 3 + [do] * 3 + [lse] * 3 + [delta] * 3 + [qkv, do, lse, delta, qkv, qkv, qkv, qkv, cfg.mask_k]
    if has_bias:
        in_specs.append(pl.BlockSpec((None, 3 * BLK, BLK), lambda h, j: (h, 0, 0)))
        args.append(bias_t)
    out = pl.BlockSpec((BLK, HEAD_DIM), lambda h, j: (j, h))
    return _call(
        body, args, name=name, grid=(cfg.nkv, n_blk), in_specs=in_specs, out_specs=[out, out],
        out_shape=[jax.ShapeDtypeStruct((cfg.n_tot, cfg.nkv * HEAD_DIM), bf16)] * 2,
        scratch_shapes=[pltpu.VMEM((BLK, HEAD_DIM), f32)] * 2, sem=("parallel", "arbitrary"), rider=rider)


def _toeplitz_basis():
    qc, kc = np.meshgrid(np.arange(GRID_W), np.arange(GRID_W), indexing="ij")
    e = (kc - qc + NA_COLS - 1)[None] == np.arange(2 * NA_COLS - 1)[:, None, None]
    return e.reshape(2 * NA_COLS - 1, GRID_W * GRID_W).astype(np.float32)


def _whole(f, ins, outs, name):
    def body(*refs):
        res = f(*[r[...] for r in refs[:len(ins)]])
        for o_ref, val in zip(refs[len(ins):], res):
            o_ref[...] = val.astype(o_ref.dtype)

    return pl.pallas_call(body, name=name,
                          out_shape=[jax.ShapeDtypeStruct(s, d) for s, d in outs])(*ins)


def _whole_bwd(f, ins, cts, name):
    n = len(ins)

    def body(*refs):
        _, vjp = jax.vjp(f, *[r[...] for r in refs[:n]])
        grads = vjp(tuple(r[...] for r in refs[n:n + len(cts)]))
        for o_ref, g in zip(refs[n + len(cts):], grads):
            o_ref[...] = g

    return pl.pallas_call(body, name=name,
                          out_shape=[jax.ShapeDtypeStruct(a.shape, f32) for a in ins])(*ins, *cts)


def _f_discretise(ar, ai, ldt, br, bi):
    dt = jnp.exp(ldt)
    mag = jnp.exp(ar * dt)
    lam_r, lam_i = mag * jnp.cos(ai * dt), mag * jnp.sin(ai * dt)
    den = ar * ar + ai * ai
    nr = lam_r - 1.0
    coef_r = (nr * ar + lam_i * ai) / den
    coef_i = (lam_i * ar - nr * ai) / den
    return (lam_r, lam_i, coef_r[None] * br - coef_i[None] * bi, coef_r[None] * bi + coef_i[None] * br)


SEG_LEN = BLK // SCAN_SEG
N_STATE = GROUPS_PER_CHUNK * SSM_STATE
CHUNK_CH = GROUPS_PER_CHUNK * SSM_GROUP


def _scan_perm():
    r = np.arange(BLK)
    t = (r % SCAN_SEG) * SEG_LEN + r // SCAN_SEG
    pm = np.zeros((BLK, BLK), np.float32)
    pm[r, t] = 1.0
    return jnp.asarray(pm, bf16), jnp.asarray(pm.T, bf16)


def _row_perm(x, pm, name, *, out_dtype, split, add=None, add_rows=None):
    n, c = x.shape
    add_tiles = None if add is None else add_rows // BLK

    def body(*refs):
        x_ref, pm_ref = refs[0], refs[1]
        o_ref = refs[-1]
        xv = x_ref[...]
        if split:
            out = _unpermute(pm_ref[...], xv.astype(f32))
        else:
            out = jnp.dot(pm_ref[...], xv.astype(bf16), preferred_element_type=f32)
        if add is not None:
            out = out + jnp.where(pl.program_id(0) < add_tiles, refs[2][...], 0.0)
        o_ref[...] = out.astype(o_ref.dtype)

    row = pl.BlockSpec((BLK, c), lambda i: (i, 0))
    in_specs, args = [row, pl.BlockSpec((BLK, BLK), lambda i: (0, 0))], [x, pm]
    if add is not None:
        in_specs.append(pl.BlockSpec((BLK, c), lambda i: (jnp.minimum(i, add_tiles - 1), 0)))
        args.append(add)
    return pl.pallas_call(
        body, name=name, grid=(n // BLK,), in_specs=in_specs, out_specs=row,
        out_shape=jax.ShapeDtypeStruct((n, c), out_dtype), compiler_params=_params(("parallel",)),
    )(*args)


def _glu_pre(u, yf, yr, dvec, pmt, n_x, name):
    d = u.shape[1]

    def body(u_ref, yf_ref, yr_ref, d_ref, pmt_ref, o_ref):
        y = d_ref[...] * u_ref[...] + _unpermute(pmt_ref[...], yf_ref[...] + yr_ref[...])
        o_ref[...] = jax.nn.gelu(y).astype(bf16)

    row = pl.BlockSpec((BLK, d), lambda i: (i, 0))
    return pl.pallas_call(
        body, name=name, grid=(n_x // BLK,),
        in_specs=[row, row, row, pl.BlockSpec((1, d), lambda i: (0, 0)), pl.BlockSpec((BLK, BLK), lambda i: (0, 0))],
        out_specs=row, out_shape=jax.ShapeDtypeStruct((n_x, d), bf16), compiler_params=_params(("parallel",)),
    )(u, yf, yr, dvec, pmt)


def _glu_pre_bwd(u, yf, yr, dvec, pm, pmt, dgy, n_x, name):
    d = u.shape[1]

    def body(u_ref, yf_ref, yr_ref, d_ref, pm_ref, pmt_ref, ct_ref, dyp_ref, dud_ref, dd_ref):
        uv = u_ref[...]
        y = d_ref[...] * uv + _unpermute(pmt_ref[...], yf_ref[...] + yr_ref[...])
        _, vjp = jax.vjp(jax.nn.gelu, y)
        dy = vjp(ct_ref[...])[0]
        dyp_ref[...] = jnp.dot(pm_ref[...], dy.astype(bf16), preferred_element_type=f32).astype(bf16)
        dud_ref[...] = d_ref[...] * dy
        part = jnp.sum(dy * uv, axis=0, keepdims=True)

        @pl.when(pl.program_id(0) == 0)
        def _():
            dd_ref[...] = part

        @pl.when(pl.program_id(0) > 0)
        def _():
            dd_ref[...] += part

    row = pl.BlockSpec((BLK, d), lambda i: (i, 0))
    vec = pl.BlockSpec((1, d), lambda i: (0, 0))
    sq = pl.BlockSpec((BLK, BLK), lambda i: (0, 0))
    return pl.pallas_call(
        body, name=name, grid=(n_x // BLK,), in_specs=[row, row, row, vec, sq, sq, row],
        out_specs=[row, row, vec],
        out_shape=[jax.ShapeDtypeStruct((n_x, d), bf16), jax.ShapeDtypeStruct((n_x, d), f32),
                   jax.ShapeDtypeStruct((1, d), f32)],
        compiler_params=_params(("arbitrary",)),
    )(u, yf, yr, dvec, pm, pmt, dgy)


def _block_order(n_xb, n_blk, reverse):
    n_cb = n_blk - n_xb
    if reverse:
        return lambda cc: jnp.where(cc < n_cb, n_blk - 1 - cc, n_xb - 1 - (cc - n_cb))
    return lambda cc: jnp.where(cc < n_cb, n_xb + cc, cc - n_cb)


def _unpermute(pmt, y):
    hi = y.astype(bf16)
    lo = (y - hi.astype(f32)).astype(bf16)
    return jnp.dot(pmt, hi, preferred_element_type=f32) + jnp.dot(pmt, lo, preferred_element_type=f32)


def _lam_pow(lr, li):
    for _ in range(int(math.log2(SEG_LEN))):
        lr, li = lr * lr - li * li, 2.0 * lr * li
    return lr, li


def _s5_fwd(up, lam, bblk, cblk, n_x, reverse, name, rider=None):
    n_tot, d = up.shape
    nq, n_blk, n_xb = d // CHUNK_CH, n_tot // BLK, n_x // BLK
    order = _block_order(n_xb, n_blk, reverse)
    ns = N_STATE
    seg_order = range(SCAN_SEG - 1, -1, -1) if reverse else range(SCAN_SEG)

    def body(u_ref, lam_ref, b_ref, c_ref, y_ref, cp_ref, bu_ref, st_ref, carry_ref):
        cc = pl.program_id(1)

        @pl.when(cc == 0)
        def _():
            carry_ref[...] = jnp.zeros_like(carry_ref)

        up_v = u_ref[...]
        bu_ref[0] = jnp.dot(up_v, b_ref[0], preferred_element_type=f32)
        bu_ref[1] = jnp.dot(up_v, b_ref[1], preferred_element_type=f32)
        lr, li = lam_ref[0:1, :], lam_ref[1:2, :]
        lrb, lib = jnp.broadcast_to(lr, (SCAN_SEG, ns)), jnp.broadcast_to(li, (SCAN_SEG, ns))

        def step(p, s, store):
            sr, si = s
            j = SEG_LEN - 1 - p if reverse else p
            off = pl.multiple_of(j * SCAN_SEG, SCAN_SEG)
            nsr = lrb * sr - lib * si + bu_ref[0, pl.ds(off, SCAN_SEG), :]
            nsi = lrb * si + lib * sr + bu_ref[1, pl.ds(off, SCAN_SEG), :]
            if store:
                st_ref[0, pl.ds(off, SCAN_SEG), :] = nsr
                st_ref[1, pl.ds(off, SCAN_SEG), :] = nsi
            return nsr, nsi

        zero = jnp.zeros((SCAN_SEG, ns), f32)
        er, ei = lax.fori_loop(0, SEG_LEN, lambda j, s: step(j, s, False), (zero, zero))
        pr, pi = _lam_pow(lr, li)
        cr, ci = carry_ref[0, 0:1, :], carry_ref[1, 0:1, :]
        rows_r, rows_i = [None] * SCAN_SEG, [None] * SCAN_SEG
        for k in seg_order:
            rows_r[k], rows_i[k] = cr, ci
            cr, ci = pr * cr - pi * ci + er[k:k + 1], pr * ci + pi * cr + ei[k:k + 1]
        carry_ref[0] = jnp.broadcast_to(cr, (SCAN_SEG, ns))
        carry_ref[1] = jnp.broadcast_to(ci, (SCAN_SEG, ns))
        cpr, cpi = jnp.concatenate(rows_r, axis=0), jnp.concatenate(rows_i, axis=0)
        cp_ref[0] = cpr
        cp_ref[1] = cpi
        lax.fori_loop(0, SEG_LEN, lambda j, s: step(j, s, True), (cpr, cpi))
        y_ref[...] = (jnp.dot(st_ref[0].astype(bf16), c_ref[0], preferred_element_type=f32)
                      - jnp.dot(st_ref[1].astype(bf16), c_ref[1], preferred_element_type=f32))

    return _call(
        body, (up, lam, bblk, cblk), name=name, grid=(nq, n_blk),
        in_specs=[pl.BlockSpec((BLK, CHUNK_CH), lambda q,cc: (order(cc), q)),
                  pl.BlockSpec((None, 2, ns), lambda q, cc: (q, 0, 0)),
                  pl.BlockSpec((None, 2, CHUNK_CH, ns), lambda q,cc: (q, 0, 0, 0)),
                  pl.BlockSpec((None, 2, ns, CHUNK_CH), lambda q,cc: (q, 0, 0, 0))],
        out_specs=[pl.BlockSpec((BLK, CHUNK_CH), lambda q,cc: (order(cc), q)),
                   pl.BlockSpec((None, None, 2, SCAN_SEG, ns), lambda q, cc: (q, cc, 0, 0, 0))],
        out_shape=[jax.ShapeDtypeStruct((n_tot, d), f32),
                   jax.ShapeDtypeStruct((nq, n_blk, 2, SCAN_SEG, ns), f32)],
        scratch_shapes=[pltpu.VMEM((2, BLK, ns), f32), pltpu.VMEM((2, BLK, ns), f32),
                        pltpu.VMEM((2, SCAN_SEG, ns), f32)],
        sem=("parallel", "arbitrary"), rider=rider)


def _s5_bwd(u, dy, cprev, lam, bblk, cblk, bblk_t, cblk_t, n_x, reverse, name, add=None, rider=None):
    n_tot, d = u.shape
    nq, n_blk, n_xb = d // CHUNK_CH, n_tot // BLK, n_x // BLK
    order = _block_order(n_xb, n_blk, reverse)
    ns = N_STATE
    blk_of = lambda step: order(n_blk - 1 - step)
    has_add = add is not None
    jof = (lambda p: SEG_LEN - 1 - p) if reverse else (lambda p: p)
    adj_seg_order = range(SCAN_SEG) if reverse else range(SCAN_SEG - 1, -1, -1)

    def body(*refs):
        u_ref, dy_ref, cp_ref, lam_ref, b_ref, c_ref, bt_ref, ct_ref = refs[:8]
        rest = refs[8:]
        if has_add:
            add_ref, rest = rest[0], rest[1:]
        du_ref, dlam_ref, db_ref, dc_ref, bu_ref, st_ref, ds_ref, acarry_ref = rest
        step_id = pl.program_id(1)
        is_x = blk_of(step_id) < n_xb

        @pl.when(step_id == 0)
        def _():
            acarry_ref[...] = jnp.zeros_like(acarry_ref)
            dlam_ref[...] = jnp.zeros_like(dlam_ref)
            db_ref[...] = jnp.zeros_like(db_ref)
            dc_ref[...] = jnp.zeros_like(dc_ref)

        up = u_ref[...]
        bu_ref[0] = jnp.dot(up, b_ref[0], preferred_element_type=f32)
        bu_ref[1] = jnp.dot(up, b_ref[1], preferred_element_type=f32)
        lr, li = lam_ref[0:1, :], lam_ref[1:2, :]
        lrb, lib = jnp.broadcast_to(lr, (SCAN_SEG, ns)), jnp.broadcast_to(li, (SCAN_SEG, ns))
        cpr, cpi = cp_ref[0], cp_ref[1]

        def fstep(p, s):
            sr, si = s
            off = pl.multiple_of(jof(p) * SCAN_SEG, SCAN_SEG)
            nsr = lrb * sr - lib * si + bu_ref[0, pl.ds(off, SCAN_SEG), :]
            nsi = lrb * si + lib * sr + bu_ref[1, pl.ds(off, SCAN_SEG), :]
            st_ref[0, pl.ds(off, SCAN_SEG), :] = nsr
            st_ref[1, pl.ds(off, SCAN_SEG), :] = nsi
            return nsr, nsi

        lax.fori_loop(0, SEG_LEN, fstep, (cpr, cpi))

        dyp = jnp.where(is_x, dy_ref[...], jnp.zeros_like(dy_ref))
        ds_ref[0] = jnp.dot(dyp, ct_ref[0], preferred_element_type=f32)
        ds_ref[1] = -jnp.dot(dyp, ct_ref[1], preferred_element_type=f32)

        def adj(p, a):
            ar, ai = a
            off = pl.multiple_of(jof(p) * SCAN_SEG, SCAN_SEG)
            nar = ds_ref[0, pl.ds(off, SCAN_SEG), :] + lrb * ar + lib * ai
            nai = ds_ref[1, pl.ds(off, SCAN_SEG), :] - lib * ar + lrb * ai
            return nar, nai

        zero = jnp.zeros((SCAN_SEG, ns), f32)
        er, ei = lax.fori_loop(0, SEG_LEN, lambda jj, a: adj(SEG_LEN - 1 - jj, a), (zero, zero))
        pr, pi = _lam_pow(lr, li)
        nr_, ni_ = acarry_ref[0, 0:1, :], acarry_ref[1, 0:1, :]
        rows_r, rows_i = [None] * SCAN_SEG, [None] * SCAN_SEG
        for k in adj_seg_order:
            rows_r[k], rows_i[k] = nr_, ni_
            nr_, ni_ = er[k:k + 1] + pr * nr_ + pi * ni_, ei[k:k + 1] + pr * ni_ - pi * nr_
        acarry_ref[0] = jnp.broadcast_to(nr_, (SCAN_SEG, ns))
        acarry_ref[1] = jnp.broadcast_to(ni_, (SCAN_SEG, ns))
        an_r, an_i = jnp.concatenate(rows_r, axis=0), jnp.concatenate(rows_i, axis=0)

        def adj2(jj, carry):
            ar, ai, glr, gli = carry
            p = SEG_LEN - 1 - jj
            nar, nai = adj(p, (ar, ai))
            off = pl.multiple_of(jof(p) * SCAN_SEG, SCAN_SEG)
            ds_ref[0, pl.ds(off, SCAN_SEG), :] = nar
            ds_ref[1, pl.ds(off, SCAN_SEG), :] = nai
            poff = pl.multiple_of(jof(p - 1) * SCAN_SEG, SCAN_SEG)
            spr, spi = st_ref[0, pl.ds(poff, SCAN_SEG), :], st_ref[1, pl.ds(poff, SCAN_SEG), :]
            return nar, nai, glr + nar * spr + nai * spi, gli - nar * spi + nai * spr

        ar, ai, glr, gli = lax.fori_loop(0, SEG_LEN - 1, adj2, (an_r, an_i, zero, zero))
        nar, nai = adj(0, (ar, ai))
        first = jof(0) * SCAN_SEG
        ds_ref[0, first:first + SCAN_SEG, :] = nar
        ds_ref[1, first:first + SCAN_SEG, :] = nai
        glr = glr + nar * cpr + nai * cpi
        gli = gli - nar * cpi + nai * cpr
        dlam_ref[0:1, :] += jnp.sum(glr, axis=0, keepdims=True)
        dlam_ref[1:2, :] += jnp.sum(gli, axis=0, keepdims=True)

        a_r, a_i = ds_ref[0].astype(bf16), ds_ref[1].astype(bf16)
        du = jnp.dot(a_r, bt_ref[0], preferred_element_type=f32) + jnp.dot(a_i, bt_ref[1],
                                                                        preferred_element_type=f32)
        if has_add:
            du = du + add_ref[...]
        du_ref[...] = du
        db_ref[0] += lax.dot_general(up, a_r, TN, preferred_element_type=f32)
        db_ref[1] += lax.dot_general(up, a_i, TN, preferred_element_type=f32)
        dc_ref[0] += lax.dot_general(st_ref[0].astype(bf16), dyp, TN, preferred_element_type=f32)
        dc_ref[1] -= lax.dot_general(st_ref[1].astype(bf16), dyp, TN, preferred_element_type=f32)

    tok = pl.BlockSpec((BLK, CHUNK_CH), lambda q,s: (blk_of(s), q))
    in_specs = [tok, pl.BlockSpec((BLK, CHUNK_CH), lambda q,s: (jnp.minimum(blk_of(s), n_xb - 1), q)),
                pl.BlockSpec((None, None, 2, SCAN_SEG, ns), lambda q, s: (q, n_blk - 1 - s, 0, 0, 0)),
                pl.BlockSpec((None, 2, ns), lambda q, s: (q, 0, 0)),
                pl.BlockSpec((None, 2, CHUNK_CH, ns), lambda q,s: (q, 0, 0, 0)),
                pl.BlockSpec((None, 2, ns, CHUNK_CH), lambda q,s: (q, 0, 0, 0)),
                pl.BlockSpec((None, 2, ns, CHUNK_CH), lambda q,s: (q, 0, 0, 0)),
                pl.BlockSpec((None, 2, CHUNK_CH, ns), lambda q,s: (q, 0, 0, 0))]
    args = [u, dy, cprev, lam, bblk, cblk, bblk_t, cblk_t]
    if has_add:
        in_specs.append(tok)
        args.append(add)
    return _call(
        body, args, name=name, grid=(nq, n_blk), in_specs=in_specs,
        out_specs=[tok, pl.BlockSpec((None, 2, ns), lambda q, s: (q, 0, 0)),
                   pl.BlockSpec((None, 2, CHUNK_CH, ns), lambda q,s: (q, 0, 0, 0)),
                   pl.BlockSpec((None, 2, ns, CHUNK_CH), lambda q,s: (q, 0, 0, 0))],
        out_shape=[jax.ShapeDtypeStruct((n_tot, d), f32), jax.ShapeDtypeStruct((nq, 2, ns), f32),
                   jax.ShapeDtypeStruct((nq, 2, CHUNK_CH, ns), f32),
                   jax.ShapeDtypeStruct((nq, 2, ns, CHUNK_CH), f32)],
        scratch_shapes=[pltpu.VMEM((2, BLK, ns), f32), pltpu.VMEM((2, BLK, ns), f32),
                        pltpu.VMEM((2, BLK, ns), f32), pltpu.VMEM((2, SCAN_SEG, ns), f32)],
        sem=("parallel", "arbitrary"), rider=rider)


def _exchange(src, gather, name):
    return _run_exchange(_Exchange(src, gather), name)


def _sum0(x, name):
    n, r, c = x.shape
    tr = _pick(r, (512, 256, 128, 64, 32, 16, 8))

    def body(x_ref, o_ref):
        acc = x_ref[0].astype(f32)
        for k in range(1, n):
            acc = acc + x_ref[k].astype(f32)
        o_ref[...] = acc

    return pl.pallas_call(
        body, name=name, grid=(r // tr,), in_specs=[pl.BlockSpec((n, tr, c), lambda i: (0, i, 0))],
        out_specs=pl.BlockSpec((tr, c), lambda i: (i, 0)), out_shape=jax.ShapeDtypeStruct((r, c), f32),
        compiler_params=_params(("parallel",)),
    )(x)


LANES = 1024


def _pack(parts, dtype):
    flat = jnp.concatenate([p.astype(dtype).reshape(-1) for p in parts])
    pad = (-flat.shape[0]) % (16 * LANES)
    if pad:
        flat = jnp.concatenate([flat, jnp.zeros((pad,), dtype)])
    return flat.reshape(-1, LANES)


def _unpack(flat, shapes):
    out, off = [], 0
    for s in shapes:
        size = int(np.prod(s))
        out.append(flat[off:off + size].reshape(s))
        off += size
    return out


def _vec(a):
    return a.reshape(-1, 1, a.shape[-1])


def _local_step(xa, tgt, mods, wts, sm, n_x):
    n_tot, d = xa.shape
    kw = dict(nrows=n_tot, n_x_rows=n_x)
    kx = dict(nrows=n_x, n_x_rows=n_x)
    mv = lambda l, k: mods[l, :, k][:, None, :]
    mx = lambda l, k: mods[l, 0:1, k][:, None, :]
    nmix, nffn = sm["norm_mix"], sm["norm_ffn"]
    cos, sin = _rope_tables(n_x, n_tot - n_x)
    cfg_a, cfg_b = _AttnCfg("na", n_x, n_tot), _AttnCfg("swa", n_x, n_tot)

    h1 = _rowwise(_f_normmod, [xa], [_vec(nmix[0:1]), mv(0, 1), mv(0, 0)], [(d, bf16)], name="l0_norm_mix", **kw)[0]
    qkv32 = _mm(h1, wts("w_in"), "nn", f32, "l0_in_proj", comm=wts)
    qkv = _qkv_post(qkv32, cos, sin, "l0_qkv_post")
    rpb2 = jnp.zeros((128, 128), f32).at[:NA_HEADS * 15, :31].set(sm["rpb"].reshape(NA_HEADS * 15, 31))
    basis = jnp.zeros((128, GRID_W * GRID_W), f32).at[:31].set(_toeplitz_basis())
    tz = _small_dot(rpb2, basis, "rpb_expand")[:NA_HEADS * 15].reshape(NA_HEADS, 15, GRID_W, GRID_W)
    bias = jnp.stack([tz[:, 3 - a:15 - a] for a in range(4)], axis=1).transpose(0, 1, 3, 2, 4).reshape(
        NA_HEADS, BLK, 3 * BLK)
    bias_t = jnp.stack([tz[:, a:a + 12][:, ::-1] for a in range(4)], axis=2).transpose(0, 1, 3, 2, 4).reshape(
        NA_HEADS, 3 * BLK, BLK)
    sink_rows = jnp.repeat(sm["sink"].reshape(NB_KV_HEADS, NB_GROUP, 1), BLK, axis=1).reshape(
        NB_KV_HEADS, NB_GROUP * BLK, 1)
    oa, lse_a = _attn_fwd(qkv, cfg_a, bias, "l0_na_fwd", rider=wts.take(CARRIER_US["na_fwd"]))
    ob, lse_b = _attn_fwd(qkv, cfg_b, sink_rows, "l0_swa_fwd", rider=wts.take(CARRIER_US["swa_fwd"]))
    o = jnp.concatenate([oa, ob], axis=1)
    y1 = _mm(o, wts("w_out"), "nn", f32, "l0_out_proj", comm=wts)
    xb = _rowwise(_f_gated_add, [xa, y1], [mv(0, 2)], [(d, f32)], name="l0_res_mix", **kw)[0]

    def ffn_fwd(xin, l, vec_of, kk, tag):
        h = _rowwise(_f_normmod, [xin], [_vec(nffn[l:l + 1]), vec_of(l, 4), vec_of(l, 3)], [(d, bf16)],
                     name=tag + "_norm_ffn", **kk)[0]
        a, b, u = _ffn_up(h, wts("w1", l), wts("w3", l), tag + "_ffn_up", comm=wts)
        fo = _mm(u, wts("w2", l), "nn", f32, tag + "_ffn_w2", comm=wts)
        xo = _rowwise(_f_gated_add, [xin, fo], [vec_of(l, 5)], [(d, f32)], name=tag + "_res_ffn", **kk)[0]
        return xo, (h, a, b, u, fo)

    def ffn_bwd(d_out, xin, saved, l, vec_of, kk, tag):
        h, a, b, u, fo = saved
        rows = kk["nrows"]
        dfo, dg2 = _rowwise_bwd(_f_gated_add, [xin, fo], [vec_of(l, 5)], [d_out], {1: bf16},
                                name=tag + "_res_ffn_bwd", **kk)
        wts.grad("w2", l, _mm(u[:rows], dfo, "tn", f32, tag + "_ffn_w2_dw", comm=wts))
        da, db = _ffn_down_bwd(dfo, wts("w2", l), a, b, tag + "_ffn_down_bwd", comm=wts)
        dh = _mm(da, wts("w1", l), "nt", f32, tag + "_ffn_w13_dx", a2=db, b2=wts("w3", l), comm=wts)
        wts.grad("w1", l, _mm(h[:rows], da, "tn", f32, tag + "_ffn_w1_dw", comm=wts))
        wts.grad("w3", l, _mm(h[:rows], db, "tn", f32, tag + "_ffn_w3_dw", comm=wts))
        dxin, dnw, dsc, dsh = _rowwise_bwd(
            _f_normmod, [xin], [_vec(nffn[l:l + 1]), vec_of(l, 4), vec_of(l, 3)], [dh], {0: f32},
            name=tag + "_norm_ffn_bwd", residual=d_out, **kk)
        return dxin, dnw, dsc, dsh, dg2

    xc, ffn0 = ffn_fwd(xb, 0, mv, kw, "l0")

    hs = _rowwise(_f_normmod, [xc], [_vec(nmix[1:2]), mv(1, 1), mv(1, 0)], [(d, f32)], name="l1_norm_mix", **kw)[0]
    g2n = sm["a_re"].shape[1]
    nq = d // CHUNK_CH
    ar2, ai2 = sm["a_re"].reshape(2 * g2n, SSM_STATE), sm["a_im"].reshape(2 * g2n, SSM_STATE)
    ldt2 = sm["log_dt"].reshape(2 * g2n, 1)
    bt_re = sm["b_re"].transpose(3, 0, 1, 2).reshape(SSM_GROUP, 2 * g2n, SSM_STATE)
    bt_im = sm["b_im"].transpose(3, 0, 1, 2).reshape(SSM_GROUP, 2 * g2n, SSM_STATE)
    disc_in = [ar2, ai2, ldt2, bt_re, bt_im]
    lam_r, lam_i, bbar_r, bbar_i = _whole(
        _f_discretise, disc_in,
        [((2 * g2n, SSM_STATE), f32)] * 2 + [((SSM_GROUP, 2 * g2n, SSM_STATE), f32)] * 2, "s5_discretise")
    eye = jnp.eye(GROUPS_PER_CHUNK, dtype=f32)
    eye6 = eye[None, None, :, None, :, None]

    def blockdiag_b(bbar):
        t = bbar.reshape(SSM_GROUP, 2, nq, GROUPS_PER_CHUNK, SSM_STATE).transpose(1, 2, 3, 0, 4)
        return (t[:, :, :, :, None, :] * eye6).reshape(2, nq, CHUNK_CH, N_STATE)

    def blockdiag_c(cw):
        t = cw.reshape(2, nq, GROUPS_PER_CHUNK, SSM_GROUP, SSM_STATE).transpose(0, 1, 2, 4, 3)
        return (t[:, :, :, :, None, :] * eye6).reshape(2, nq, N_STATE, CHUNK_CH)

    lam = jnp.stack([lam_r.reshape(2, nq, N_STATE), lam_i.reshape(2, nq, N_STATE)], axis=2)
    bblk = jnp.stack([blockdiag_b(bbar_r), blockdiag_b(bbar_i)], axis=2)
    cblk = jnp.stack([blockdiag_c(sm["c_re"]), blockdiag_c(sm["c_im"])], axis=2)
    bblk16, cblk16 = bblk.astype(bf16), cblk.astype(bf16)
    bblk_t, cblk_t = bblk16.transpose(0, 1, 2, 4, 3), cblk16.transpose(0, 1, 2, 4, 3)
    pm, pmt = _scan_perm()
    hs_p = _row_perm(hs, pm, "l1_s5_perm", out_dtype=bf16, split=False)
    ys, cps = [], []
    for dr in range(2):
        yd, cp = _s5_fwd(hs_p, lam[dr], bblk16[dr], cblk16[dr], n_x, dr == 1, "l1_s5_fwd%d" % dr,
                         rider=wts.take(CARRIER_US["s5_fwd"]))
        ys.append(yd)
        cps.append(cp)
    dvec = sm["ssm_d"].reshape(1, d)
    gy = _glu_pre(hs, ys[0], ys[1], dvec, pmt, n_x, "l1_glu_pre")
    zv = _mm(gy, wts("w_glu_v"), "nn", f32, "l1_glu_val", comm=wts)
    zg = _mm(gy, wts("w_glu_g"), "nn", f32, "l1_glu_gate", comm=wts)
    bv, bg = sm["b_glu"][:d].reshape(1, 1, d), sm["b_glu"][d:].reshape(1, 1, d)
    xd = _rowwise(_f_glu_post, [xc, zv, zg], [mx(1, 2), bv, bg], [(d, f32)], name="l1_glu_post", **kx)[0]
    xe, ffn1 = ffn_fwd(xd, 1, mx, kx, "l1")

    d_xe, d_nfinal, loss_blk = _loss_head(xe, tgt, sm["norm_final"].reshape(1, d), nrows=n_x, name="loss_head")
    d_xd, dnffn1, dsc2_1, dsh2_1, dg2_1 = ffn_bwd(d_xe, xd, ffn1, 1, mx, kx, "l1")
    dzv, dzg, dg1_1, dbv, dbg = _rowwise_bwd(_f_glu_post, [xc, zv, zg], [mx(1, 2), bv, bg], [d_xd],
                                             {1: bf16, 2: bf16}, name="l1_glu_post_bwd", **kx)
    dgy = _mm(dzv, wts("w_glu_v"), "nt", f32, "l1_glu_dx", a2=dzg, b2=wts("w_glu_g"), comm=wts)
    dwglu_v = _mm(gy, dzv, "tn", f32, "l1_glu_val_dw", comm=wts)
    dwglu_g = _mm(gy, dzg, "tn", f32, "l1_glu_gate_dw", comm=wts)
    wts.grad("w_glu", 0, jnp.concatenate([dwglu_v, dwglu_g], axis=1))
    dy_p, du_skip, d_dvec = _glu_pre_bwd(hs, ys[0], ys[1], dvec, pm, pmt, dgy, n_x, "l1_glu_pre_bwd")
    du0, dlam0, db0, dc0 = _s5_bwd(hs_p, dy_p, cps[0], lam[0], bblk16[0], cblk16[0], bblk_t[0], cblk_t[0], n_x,
                                   False, "l1_s5_bwd0", rider=wts.take(CARRIER_US["s5_bwd"]))
    du1, dlam1, db1, dc1 = _s5_bwd(hs_p, dy_p, cps[1], lam[1], bblk16[1], cblk16[1], bblk_t[1], cblk_t[1], n_x,
                                   True, "l1_s5_bwd1", add=du0, rider=wts.take(CARRIER_US["s5_bwd"]))
    d_hs = _row_perm(du1, pmt, "l1_s5_unperm", out_dtype=f32, split=True, add=du_skip, add_rows=n_x)
    d_xc, dnmix1, dsc1_1, dsh1_1 = _rowwise_bwd(
        _f_normmod, [xc], [_vec(nmix[1:2]), mv(1, 1), mv(1, 0)], [d_hs], {0: f32},
        name="l1_norm_mix_bwd", residual=d_xd, residual_rows=n_x, **kw)
    dlam = jnp.stack([dlam0, dlam1])
    dlam_r, dlam_i = dlam[:, :, 0].reshape(2 * g2n, SSM_STATE), dlam[:, :, 1].reshape(2 * g2n, SSM_STATE)
    dbb = jnp.stack([db0, db1]).reshape(2, nq, 2, GROUPS_PER_CHUNK, SSM_GROUP, GROUPS_PER_CHUNK, SSM_STATE)
    eye7 = eye[None, None, None, :, None, :, None]
    dbbar = (dbb * eye7).sum(axis=5).transpose(2, 4, 0, 1, 3, 5).reshape(2, SSM_GROUP, 2 * g2n, SSM_STATE)
    dcc = jnp.stack([dc0, dc1]).reshape(2, nq, 2, GROUPS_PER_CHUNK, SSM_STATE, GROUPS_PER_CHUNK, SSM_GROUP)
    dcw = (dcc * eye7).sum(axis=5).transpose(2, 0, 1, 3, 5, 4).reshape(2, 2, g2n, SSM_GROUP, SSM_STATE)
    d_ar, d_ai, d_ldt, d_btr, d_bti = _whole_bwd(_f_discretise, disc_in, [dlam_r, dlam_i, dbbar[0], dbbar[1]],
                                                 "s5_discretise_bwd")
    to_b = lambda t: t.reshape(SSM_GROUP, 2, g2n, SSM_STATE).transpose(1, 2, 3, 0)
    wts.early(dict(
        loss=loss_blk[0, 0].reshape(1), a_re=d_ar.reshape(sm["a_re"].shape), a_im=d_ai.reshape(sm["a_im"].shape),
        log_dt=d_ldt.reshape(sm["log_dt"].shape), b_re=to_b(d_btr), b_im=to_b(d_bti), c_re=dcw[0], c_im=dcw[1],
        ssm_d=d_dvec.reshape(d), b_glu=jnp.concatenate([dbv.reshape(d), dbg.reshape(d)]),
        norm_final=d_nfinal.reshape(d)))

    d_xb, dnffn0, dsc2_0, dsh2_0, dg2_0 = ffn_bwd(d_xc, xb, ffn0, 0, mv, kw, "l0")
    dy1, dg1_0 = _rowwise_bwd(_f_gated_add, [xa, y1], [mv(0, 2)], [d_xb], {1: bf16}, name="l0_res_mix_bwd", **kw)
    d_o = _mm(dy1, wts("w_out"), "nt", bf16, "l0_out_proj_dx", comm=wts)
    wts.grad("w_out", 0, _mm(o, dy1, "tn", f32, "l0_out_proj_dw", comm=wts))
    dqa, delta_a, dbias = _attn_bwd_q(qkv, oa, d_o, lse_a, cfg_a, bias, "l0_na_bwd_q",
                                      rider=wts.take(CARRIER_US["na_bwd_q"]))
    dka, dva = _attn_bwd_kv(qkv, d_o, lse_a, delta_a, cfg_a, bias_t, "l0_na_bwd_kv",
                            rider=wts.take(CARRIER_US["na_bwd_kv"]))
    dqb, delta_b, dsink_rows = _attn_bwd_q(qkv, ob, d_o, lse_b, cfg_b, sink_rows, "l0_swa_bwd_q",
                                           rider=wts.take(CARRIER_US["swa_bwd_q"]))
    dkb, dvb = _attn_bwd_kv(qkv, d_o, lse_b, delta_b, cfg_b, None, "l0_swa_bwd_kv",
                            rider=wts.take(CARRIER_US["swa_bwd_kv"]))
    d_qkv = _qkv_post_bwd([dqa, dka, dva, dqb, dkb, dvb], cos, sin, "l0_qkv_post_bwd")
    dh1 = _mm(d_qkv, wts("w_in"), "nt", f32, "l0_in_proj_dx", comm=wts)
    wts.grad("w_in", 0, _mm(h1, d_qkv, "tn", f32, "l0_in_proj_dw", comm=wts))
    d_xa, dnmix0, dsc1_0, dsh1_0 = _rowwise_bwd(
        _f_normmod, [xa], [_vec(nmix[0:1]), mv(0, 1), mv(0, 0)], [dh1], {0: f32},
        name="l0_norm_mix_bwd", residual=d_xb, **kw)
    dbias5 = dbias.reshape(NA_HEADS, 4, GRID_W, 12, GRID_W).transpose(0, 1, 3, 2, 4)
    dtz = sum(jnp.pad(dbias5[:, a], ((0, 0), (3 - a, a), (0, 0), (0, 0))) for a in range(4))
    dtz2 = jnp.zeros((128, GRID_W * GRID_W), f32).at[:NA_HEADS * 15].set(dtz.reshape(NA_HEADS * 15, -1))
    d_rpb = _small_dot(dtz2, basis.T, "rpb_expand_bwd")[:NA_HEADS * 15, :31].reshape(sm["rpb"].shape)
    d_sink = dsink_rows.reshape(NB_KV_HEADS * NB_GROUP, BLK).sum(axis=1)

    zero_c = jnp.zeros((1, 1, d), f32)
    both = lambda gx: jnp.concatenate([gx, zero_c], axis=0)
    dmods = jnp.stack([
        jnp.stack([dsh1_0, dsc1_0, dg1_0, dsh2_0, dsc2_0, dg2_0], axis=2),
        jnp.stack([dsh1_1, dsc1_1, both(dg1_1), both(dsh2_1), both(dsc2_1), both(dg2_1)], axis=2),
    ])[:, :, 0]
    late = dict(norm_mix=jnp.concatenate([dnmix0[0], dnmix1[0]]), norm_ffn=jnp.concatenate([dnffn0[0], dnffn1[0]]),
                rpb=d_rpb, sink=d_sink)
    return d_xa, late, dmods


WEIGHTS = ("c_ctx", "ada_w", "ada_b", "norm_mix", "norm_ffn", "ffn_w1", "ffn_w3", "ffn_w2", "attn_w_in",
           "attn_w_out", "attn_rpb", "attn_sink", "ssm_a_re", "ssm_a_im", "ssm_log_dt", "ssm_b_re", "ssm_b_im",
           "ssm_c_re", "ssm_c_im", "ssm_d", "ssm_w_glu", "ssm_b_glu", "norm_final")
SHARDED_BIG = ("ffn_w1", "ffn_w3", "ffn_w2", "attn_w_in", "attn_w_out", "ssm_w_glu")
BIG = ("ada_w",) + SHARDED_BIG
SMALL = tuple(n for n in WEIGHTS if n not in BIG)


BIG_SLABS = dict(w_in=("attn_w_in", "cols"), w_out=("attn_w_out", "rows"), w1=("ffn_w1", "cols"),
                 w3=("ffn_w3", "cols"), w2=("ffn_w2", "rows"), w_glu=("ssm_w_glu", "cols"))
GATHER_ORDER = (("w_in", 0), ("w_out", 0), ("w1", 0), ("w3", 0), ("w2", 0), ("w_glu", 0), ("w1", 1), ("w3", 1),
                ("w2", 1))


class _Weights(_Comm):
    def __init__(self, w, pack_early):
        super().__init__()
        self.w, self.pack_early = w, pack_early
        self.gathers = {key: self.post(w[BIG_SLABS[key[0]][0]][key[1]].astype(bf16), True) for key in GATHER_ORDER}
        self.full, self.scatters, self.early_gather = {}, {}, None

    def __call__(self, name, layer=0):
        key = ("w_glu", 0) if name.startswith("w_glu") else (name, layer)
        if key not in self.full:
            blk = self.finish(self.gathers[key])
            if BIG_SLABS[key[0]][1] == "cols":
                self.full[key] = blk.transpose(1, 0, 2).reshape(blk.shape[1], -1)
            else:
                self.full[key] = blk.reshape(-1, blk.shape[2])
        full = self.full[key]
        if name.startswith("w_glu"):
            half = full.shape[1] // 2
            return full[:, :half] if name == "w_glu_v" else full[:, half:]
        return full

    def grad(self, name, layer, dw):
        if BIG_SLABS[name][1] == "cols":
            blk = dw.reshape(dw.shape[0], N_DEV, -1).transpose(1, 0, 2)
        else:
            blk = dw.reshape(N_DEV, -1, dw.shape[1])
        self.scatters[(name, layer)] = self.post(blk.astype(bf16), False)

    def early(self, parts):
        self.early_shapes = {k: v.shape for k, v in parts.items()}
        self.early_gather = self.post(self.pack_early(parts), True)

    def reduced(self, name):
        short = next(k for k, v in BIG_SLABS.items() if v[0] == name)
        layers = [l for (n, l) in self.scatters if n == short]
        parts = [_sum0(self.finish(self.scatters[(short, l)]), "sum_%s%d" % (short, l)) for l in sorted(layers)]
        return jnp.stack(parts).reshape(self.w[name].shape)


def kernel(x, c, ctx, c_ctx, ada_w, ada_b, norm_mix, norm_ffn, ffn_w1, ffn_w3, ffn_w2, attn_w_in, attn_w_out, attn_rpb, attn_sink, ssm_a_re, ssm_a_im, ssm_log_dt, ssm_b_re, ssm_b_im, ssm_c_re, ssm_c_im, ssm_d, ssm_w_glu, ssm_b_glu, norm_final, loss_target, m_c_ctx, m_ada_w, m_ada_b, m_norm_mix, m_norm_ffn, m_ffn_w1, m_ffn_w3, m_ffn_w2, m_attn_w_in, m_attn_w_out, m_attn_rpb, m_attn_sink, m_ssm_a_re, m_ssm_a_im, m_ssm_log_dt, m_ssm_b_re, m_ssm_b_im, m_ssm_c_re, m_ssm_c_im, m_ssm_d, m_ssm_w_glu, m_ssm_b_glu, m_norm_final, v_c_ctx, v_ada_w, v_ada_b, v_norm_mix, v_norm_ffn, v_ffn_w1, v_ffn_w3, v_ffn_w2, v_attn_w_in, v_attn_w_out, v_attn_rpb, v_attn_sink, v_ssm_a_re, v_ssm_a_im, v_ssm_log_dt, v_ssm_b_re, v_ssm_b_im, v_ssm_c_re, v_ssm_c_im, v_ssm_d, v_ssm_w_glu, v_ssm_b_glu, v_norm_final):
    p = dict(locals())
    w = {n: p[n] for n in WEIGHTS}
    me = 4 * lax.axis_index("x") + 2 * lax.axis_index("y") + lax.axis_index("c")
    n_x, d = x.shape[1], x.shape[2]
    cols = ada_w.shape[2]
    d8 = d // N_DEV

    first = jnp.concatenate([c[0], ssm_d[0], ssm_b_glu[0]])[None]
    g0 = _exchange(first, True, "gather_vectors")[:, 0]
    c_all, d_full, bglu_full = g0[:, :d], g0[:, d:d + d8].reshape(d), g0[:, d + d8:].reshape(2 * d)
    cc = jnp.concatenate([c_all, c_ctx[None], jnp.zeros((16 - N_DEV - 1, d), f32)])
    sc_all = _whole(_f_silu, [cc], [((16, d), f32)], "silu_c")[0]
    my_cols = lambda a: lax.dynamic_slice_in_dim(a, me * cols, cols, axis=a.ndim - 1)
    mod_loc = jnp.stack([_mm(sc_all, ada_w[l], "nn", f32, "ada_mod%d" % l) for l in range(2)])
    mod_loc = mod_loc + my_cols(ada_b)[:, None, :]
    mg = _exchange(mod_loc.reshape(32, cols), True, "gather_mod")
    mod_all = mg.reshape(N_DEV, 2, 16, cols).transpose(1, 2, 0, 3).reshape(2, 16, N_DEV * cols)
    mod_x = lax.dynamic_index_in_dim(mod_all, me, axis=1, keepdims=False)
    mods = jnp.stack([mod_x, mod_all[:, N_DEV]], axis=1).reshape(2, 2, 6, d)

    wts = _Weights(w, lambda parts: _pack(list(parts.values()), f32))
    sm = dict(norm_mix=norm_mix, norm_ffn=norm_ffn, rpb=attn_rpb[0], sink=attn_sink[0], a_re=ssm_a_re[0],
              a_im=ssm_a_im[0], log_dt=ssm_log_dt[0], b_re=ssm_b_re[0], b_im=ssm_b_im[0], c_re=ssm_c_re[0],
              c_im=ssm_c_im[0], ssm_d=d_full, b_glu=bglu_full, norm_final=norm_final)

    xa = jnp.concatenate([x[0], ctx[0]], axis=0)
    d_xa, late, dmods = _local_step(xa, loss_target[0], mods, wts, sm, n_x)

    late_names = ("norm_mix", "norm_ffn", "rpb", "sink")
    parts = [late[n] for n in late_names] + [dmods[:, 0].reshape(2, 6 * d), dmods[:, 1].reshape(2, 6 * d)]
    shapes = [q.shape for q in parts]
    late_gather = wts.post(_pack(parts, f32), True)
    grads = {n: wts.reduced(n) for n in SHARDED_BIG}
    sg = wts.finish(late_gather)
    tot = _unpack(_sum0(sg, "sum_small").reshape(-1), shapes)
    early = _unpack(_sum0(wts.finish(wts.early_gather), "sum_small_early").reshape(-1),
                    list(wts.early_shapes.values()))
    ts = dict(zip(late_names, tot[:len(late_names)]), **dict(zip(wts.early_shapes, early)))
    loss = ts["loss"][0]
    tot_dmod_x, tot_dmod_c = tot[-2], tot[-1]
    off_x = sum(int(np.prod(s)) for s in shapes[:-2])
    dmod_x_all = sg.reshape(N_DEV, -1)[:, off_x:off_x + 12 * d].reshape(N_DEV, 2, 6 * d)
    dm = jnp.concatenate([dmod_x_all.transpose(1, 0, 2), tot_dmod_c[:, None, :],
                          jnp.zeros((2, 16 - N_DEV - 1, 6 * d), f32)], axis=1)
    dm_loc = my_cols(dm)
    grads["ada_w"] = jnp.stack([_mm(sc_all, dm_loc[l], "tn", f32, "ada_w_grad%d" % l) for l in range(2)])
    grads["ada_b"] = tot_dmod_x + tot_dmod_c
    dsc_part = _mm(dm_loc[0], ada_w[0], "nt", f32, "silu_c_grad", a2=dm_loc[1], b2=ada_w[1])[N_DEV:N_DEV + 1]
    dsc = _sum0(_exchange(dsc_part, True, "gather_cctx"), "sum_cctx")
    grads["c_ctx"] = _whole_bwd(_f_silu, [c_ctx[None]], [dsc], "silu_cctx_bwd")[0][0]
    grads.update(norm_mix=ts["norm_mix"], norm_ffn=ts["norm_ffn"], attn_rpb=ts["rpb"][None],
                 attn_sink=ts["sink"][None], ssm_a_re=ts["a_re"][None], ssm_a_im=ts["a_im"][None],
                 ssm_log_dt=ts["log_dt"][None], ssm_b_re=ts["b_re"][None], ssm_b_im=ts["b_im"][None],
                 ssm_c_re=ts["c_re"][None], ssm_c_im=ts["c_im"][None],
                 ssm_d=lax.dynamic_slice_in_dim(ts["ssm_d"], me * d8, d8)[None],
                 ssm_b_glu=lax.dynamic_slice_in_dim(ts["b_glu"], me * 2 * d8, 2 * d8)[None],
                 norm_final=ts["norm_final"])

    delta, new_m, new_v = {}, {}, {}
    for n in BIG:
        two_d = lambda a: a.reshape(-1, a.shape[-1])
        dl, m2, v2 = _adamw(two_d(w[n]), two_d(grads[n]), two_d(p["m_" + n]), two_d(p["v_" + n]), "adamw_" + n)
        delta[n], new_m[n], new_v[n] = (t.reshape(w[n].shape) for t in (dl, m2, v2))
    sshapes = [w[n].shape for n in SMALL]
    packs = [_pack([src[n] for n in SMALL], f32) for src in
             (w, grads, {n: p["m_" + n] for n in SMALL}, {n: p["v_" + n] for n in SMALL})]
    for store, t in zip((delta, new_m, new_v), _adamw(*packs, "adamw_small")):
        store.update(zip(SMALL, _unpack(t.reshape(-1), sshapes)))

    return (loss, d_xa[:n_x][None], *[grads[n] for n in WEIGHTS], *[delta[n] for n in WEIGHTS],
            *[new_m[n] for n in WEIGHTS], *[new_v[n] for n in WEIGHTS])
```

```python
import functools
import math

import numpy as np
import jax
import jax.numpy as jnp
from jax import lax
from jax.experimental import pallas as pl
from jax.experimental.pallas import tpu as pltpu

f32, bf16 = jnp.float32, jnp.bfloat16

HEAD_DIM = 128
GRID_W = 64
NA_HEADS = 8
NB_Q_HEADS = 8
NB_KV_HEADS = 2
NB_GROUP = NB_Q_HEADS // NB_KV_HEADS
NA_ROWS = 8
NA_COLS = 16
SW_RADIUS = 128
ROPE_BASE = 10000.0
SSM_GROUP = 16
SSM_STATE = 64
EPS = 1e-6
NEG_INF = -1e30
ADAM_LR, ADAM_B1, ADAM_B2, ADAM_EPS, ADAM_WD, ADAM_STEP = 0.001, 0.9, 0.999, 1e-08, 0.01, 10

N_DEV = 8
BLK = 256
SCAN_SEG = 8
GROUPS_PER_CHUNK = 16
V7X_VMEM_LIMIT = 56 * 2 ** 20
MM_VMEM_BUDGET = 36 * 2 ** 20
MXU_FLOPS_PER_US = 7e8
CARRIER_US = dict(na_fwd=400, swa_fwd=250, na_bwd_q=430, na_bwd_kv=640, swa_bwd_q=220, swa_bwd_kv=320,
                  s5_fwd=600, s5_bwd=1200)

NT = (((1,), (1,)), ((), ()))
TN = (((0,), (0,)), ((), ()))


def _params(sem, side_effects=False):
    return pltpu.CompilerParams(dimension_semantics=sem, vmem_limit_bytes=V7X_VMEM_LIMIT,
                                has_side_effects=side_effects)


def _pick(n, cands):
    for c in cands:
        if n % c == 0:
            return c
    return n


MESH = pl.DeviceIdType.MESH
MIN_CHUNK_BYTES = 512 * 1024
EXCHANGE_BYTES_PER_US = 74e3
RIDER_OVERHANG = 1.5


def _peers():
    x, y, c = lax.axis_index("x"), lax.axis_index("y"), lax.axis_index("c")
    return 4 * x + 2 * y + c, lambda o: (x ^ (o >> 2), y ^ ((o >> 1) & 1), c ^ (o & 1))


class _Exchange:
    def __init__(self, src, gather):
        self.src, self.gather, self.result = src, gather, None
        self.out_shape = (N_DEV,) + src.shape if gather else src.shape
        slab = self.out_shape[1:]
        slab_bytes = int(np.prod(slab)) * src.dtype.itemsize
        self.nch = next((k for k in (4, 2) if slab[0] % (16 * k) == 0 and slab_bytes // k >= MIN_CHUNK_BYTES), 1)
        self.step = slab[0] // self.nch
        self.cost_us = (N_DEV - 1) * slab_bytes / EXCHANGE_BYTES_PER_US

    def scratch(self):
        return [pltpu.SemaphoreType.DMA((N_DEV - 1, self.nch)), pltpu.SemaphoreType.DMA((N_DEV - 1, self.nch)),
                pltpu.SemaphoreType.DMA(())]

    def _copies(self, src_ref, out_ref, send_sems, recv_sems, local_sem):
        me, peer = _peers()
        local = pltpu.make_async_copy(src_ref if self.gather else src_ref.at[me], out_ref.at[me], local_sem)

        def copy(o, k, src_idx, dst_idx):
            rows = pl.ds(k * self.step, self.step)
            return pltpu.make_async_remote_copy(
                src_ref=(src_ref if self.gather else src_ref.at[src_idx]).at[rows],
                dst_ref=out_ref.at[dst_idx].at[rows],
                send_sem=send_sems.at[o - 1, k], recv_sem=recv_sems.at[o - 1, k],
                device_id=peer(o), device_id_type=MESH)

        pairs = [(o, k) for k in range(self.nch) for o in range(1, N_DEV)]
        sends = [copy(o, k, me ^ o, me) for o, k in pairs]
        recvs = [copy(o, k, me, me ^ o) for o, k in pairs]
        return local, sends, recvs

    def start(self, *refs):
        local, sends, _ = self._copies(*refs)
        local.start()
        for cp in sends:
            cp.start()

    def wait(self, *refs):
        local, sends, recvs = self._copies(*refs)
        for cp in recvs:
            cp.wait_recv()
        for cp in sends:
            cp.wait_send()
        local.wait()


def _run_exchange(ex, name):
    def body(src_ref, out_ref, *sems):
        ex.start(src_ref, out_ref, *sems)
        ex.wait(src_ref, out_ref, *sems)

    ex.result = pl.pallas_call(
        body, name=name, out_shape=jax.ShapeDtypeStruct(ex.out_shape, ex.src.dtype),
        in_specs=[pl.BlockSpec(memory_space=pl.ANY)], out_specs=pl.BlockSpec(memory_space=pl.ANY),
        scratch_shapes=ex.scratch(), compiler_params=pltpu.CompilerParams(has_side_effects=True),
    )(ex.src)
    return ex.result


def _call(body, args, *, name, grid, in_specs, out_specs, out_shape, scratch_shapes=(), sem, rider=None):
    if not rider:
        return pl.pallas_call(body, name=name, grid=grid, in_specs=in_specs, out_specs=out_specs,
                              out_shape=out_shape, scratch_shapes=list(scratch_shapes),
                              compiler_params=_params(sem))(*args)
    n_in, n_out, n_sc, n_ex = len(in_specs), len(out_specs), len(scratch_shapes), len(rider)

    def wrapped(*refs):
        ins, ex_src = refs[:n_in], refs[n_in:n_in + n_ex]
        outs = refs[n_in + n_ex:n_in + n_ex + n_out]
        ex_out = refs[n_in + n_ex + n_out:n_in + 2 * n_ex + n_out]
        scr = refs[n_in + 2 * n_ex + n_out:n_in + 2 * n_ex + n_out + n_sc]
        sems = refs[n_in + 2 * n_ex + n_out + n_sc:]
        ids = [pl.program_id(a) for a in range(len(grid))]
        first = functools.reduce(jnp.logical_and, [i == 0 for i in ids])
        last = functools.reduce(jnp.logical_and, [i == g - 1 for i, g in zip(ids, grid)])

        @pl.when(first)
        def _():
            for e, ex in enumerate(rider):
                ex.start(ex_src[e], ex_out[e], *sems[3 * e:3 * e + 3])

        body(*ins, *outs, *scr)

        @pl.when(last)
        def _():
            for e, ex in enumerate(rider):
                ex.wait(ex_src[e], ex_out[e], *sems[3 * e:3 * e + 3])

    anyspace = pl.BlockSpec(memory_space=pl.ANY)
    res = pl.pallas_call(
        wrapped, name=name, grid=grid, in_specs=list(in_specs) + [anyspace] * n_ex,
        out_specs=list(out_specs) + [anyspace] * n_ex,
        out_shape=list(out_shape) + [jax.ShapeDtypeStruct(ex.out_shape, ex.src.dtype) for ex in rider],
        scratch_shapes=list(scratch_shapes) + [s for ex in rider for s in ex.scratch()],
        compiler_params=_params(("arbitrary",) * len(grid), side_effects=True),
    )(*args, *[ex.src for ex in rider])
    for ex, r in zip(rider, res[n_out:]):
        ex.result = r
    return list(res[:n_out])


class _Comm:
    def __init__(self):
        self.queue, self.count = [], 0

    def post(self, src, gather):
        ex = _Exchange(src, gather)
        self.queue.append(ex)
        return ex

    def take(self, budget_us):
        rider = []
        for ex in list(self.queue):
            if budget_us > 0 and ex.cost_us <= RIDER_OVERHANG * budget_us:
                budget_us -= ex.cost_us
                rider.append(ex)
                self.queue.remove(ex)
        return rider

    def finish(self, ex):
        while ex.result is None:
            self.count += 1
            _run_exchange(self.queue.pop(0), "exchange%d" % self.count)
        return ex.result


def _mm(a, b, mode, out_dtype, name, a2=None, b2=None, comm=None, gated=None):
    if mode == "tn":
        kdim, m, n = a.shape[0], a.shape[1], b.shape[1]
    elif mode == "nt":
        m, kdim, n = a.shape[0], a.shape[1], b.shape[0]
    else:
        m, kdim, n = a.shape[0], a.shape[1], b.shape[1]
    if mode == "tn":
        tm = _pick(m, (2048, 1024, 512, 256, 128))
        tk = _pick(kdim, (1024, 768, 512, 256, 128))
    else:
        tm = _pick(m, (1024, 768, 512, 256, 128))
        tk = _pick(kdim, (2048, 1024, 768, 512, 256, 128))
    in_bytes = max(a.dtype.itemsize, b.dtype.itemsize) * (1 if a2 is None else 2)
    out_bytes = jnp.dtype(out_dtype).itemsize

    def vmem(tn_):
        return 2 * in_bytes * tk * (tm + tn_) + (4 + 2 * out_bytes + (0 if gated is None else 16)) * tm * tn_

    tn = next((t for t in (2048, 1024, 512, 256, 128) if n % t == 0 and vmem(t) <= MM_VMEM_BUDGET), n)
    nk = kdim // tk
    if mode == "tn":
        a_spec = pl.BlockSpec((tk, tm), lambda i, j, k: (k, i))
        b_spec = pl.BlockSpec((tk, tn), lambda i, j, k: (k, j))
        dims = TN
    elif mode == "nt":
        a_spec = pl.BlockSpec((tm, tk), lambda i, j, k: (i, k))
        b_spec = pl.BlockSpec((tn, tk), lambda i, j, k: (j, k))
        dims = NT
    else:
        a_spec = pl.BlockSpec((tm, tk), lambda i, j, k: (i, k))
        b_spec = pl.BlockSpec((tk, tn), lambda i, j, k: (k, j))
        dims = (((1,), (0,)), ((), ()))

    pairs = 1 if a2 is None else 2
    n_in = 2 * pairs + (0 if gated is None else 2)

    def body(*refs):
        outs, acc_ref = refs[n_in:-1], refs[-1]
        i, k = pl.program_id(0), pl.program_id(2)

        @pl.when(k == 0)
        def _():
            acc_ref[...] = jnp.zeros_like(acc_ref)

        for p in range(pairs):
            acc_ref[...] += lax.dot_general(refs[2 * p][...].astype(bf16), refs[2 * p + 1][...].astype(bf16), dims,
                                            preferred_element_type=f32)

        @pl.when(k == nk - 1)
        def _():
            acc = acc_ref[...]
            outs[0][...] = acc.astype(outs[0].dtype)
            if gated is not None:
                res_ref, gate_ref = refs[2 * pairs], refs[2 * pairs + 1]
                gate = gate_ref[0]
                if gate_ref.shape[0] == 2:
                    row = i * tm + lax.broadcasted_iota(jnp.int32, (tm, 1), 0)
                    gate = jnp.where(row >= gated[2], gate_ref[1], gate)
                outs[1][...] = res_ref[...] + gate * acc

    args = (a, b) if a2 is None else (a, b, a2, b2)
    ospec = pl.BlockSpec((tm, tn), lambda i, j, k: (i, j))
    in_specs, out_specs = [a_spec, b_spec] * pairs, [ospec]
    out_shape = [jax.ShapeDtypeStruct((m, n), out_dtype)]
    if gated is not None:
        res, gate, _ = gated
        args = args + (res, gate)
        in_specs += [ospec, pl.BlockSpec((gate.shape[0], 1, tn), lambda i, j, k: (0, 0, j))]
        out_specs.append(ospec)
        out_shape.append(jax.ShapeDtypeStruct((m, n), f32))
    rider = comm.take(2.0 * pairs * m * n * kdim / MXU_FLOPS_PER_US) if comm else None
    res = _call(
        body, args, name=name, grid=(m // tm, n // tn, nk), in_specs=in_specs, out_specs=out_specs,
        out_shape=out_shape, scratch_shapes=[pltpu.VMEM((tm, tn), f32)],
        sem=("parallel", "parallel", "arbitrary"), rider=rider)
    return res[0] if gated is None else res


def _ffn_up(h, w1, w3, name, comm=None):
    m, kdim = h.shape
    n = w1.shape[1]
    tm, tn = _pick(m, (1024, 768, 512, 256, 128)), _pick(n, (512, 256, 128))

    def body(h_ref, w1_ref, w3_ref, a_ref, b_ref, u_ref):
        hv = h_ref[...]
        a = jnp.dot(hv, w1_ref[...], preferred_element_type=f32).astype(bf16)
        b = jnp.dot(hv, w3_ref[...], preferred_element_type=f32).astype(bf16)
        a_ref[...] = a
        b_ref[...] = b
        u_ref[...] = (jax.nn.silu(a.astype(f32)) * b.astype(f32)).astype(bf16)

    wspec = pl.BlockSpec((kdim, tn), lambda i, j: (0, j))
    ospec = pl.BlockSpec((tm, tn), lambda i, j: (i, j))
    rider = comm.take(4.0 * m * n * kdim / MXU_FLOPS_PER_US) if comm else None
    return _call(
        body, (h, w1, w3), name=name, grid=(m // tm, n // tn),
        in_specs=[pl.BlockSpec((tm, kdim), lambda i, j: (i, 0)), wspec, wspec], out_specs=[ospec] * 3,
        out_shape=[jax.ShapeDtypeStruct((m, n), bf16)] * 3, sem=("parallel", "parallel"), rider=rider)


def _ffn_down_bwd(g, w2, a, b, name, comm=None):
    m, kdim = g.shape
    n = w2.shape[0]
    tm, tn = _pick(m, (1024, 768, 512, 256, 128)), _pick(n, (512, 256, 128))

    def body(g_ref, w_ref, a_ref, b_ref, da_ref, db_ref):
        du = lax.dot_general(g_ref[...], w_ref[...], NT, preferred_element_type=f32)
        _, vjp = jax.vjp(lambda p, q: jax.nn.silu(p) * q, a_ref[...].astype(f32), b_ref[...].astype(f32))
        da, db = vjp(du)
        da_ref[...] = da.astype(bf16)
        db_ref[...] = db.astype(bf16)

    ospec = pl.BlockSpec((tm, tn), lambda i, j: (i, j))
    rider = comm.take(2.0 * m * n * kdim / MXU_FLOPS_PER_US) if comm else None
    return _call(
        body, (g, w2, a, b), name=name, grid=(m // tm, n // tn),
        in_specs=[pl.BlockSpec((tm, kdim), lambda i, j: (i, 0)), pl.BlockSpec((tn, kdim), lambda i, j: (j, 0)),
                  ospec, ospec],
        out_specs=[ospec] * 2, out_shape=[jax.ShapeDtypeStruct((m, n), bf16)] * 2,
        sem=("parallel", "parallel"), rider=rider)


def _small_dot(a, b, name):
    def body(a_ref, b_ref, o_ref):
        o_ref[...] = jnp.dot(a_ref[...], b_ref[...], precision=lax.Precision.HIGHEST, preferred_element_type=f32)

    return pl.pallas_call(body, name=name, out_shape=jax.ShapeDtypeStruct((a.shape[0], b.shape[1]), f32))(a, b)


def _group_of(i, n_x_tiles, n_groups):
    return jnp.where(i >= n_x_tiles, n_groups - 1, 0)


def _rowwise(f, rows, vecs, outs, *, nrows, n_x_rows, name, tm=BLK, tc=None):
    n_x_tiles = n_x_rows // tm
    grid = (nrows // tm,) if tc is None else (nrows // tm, rows[0].shape[1] // tc)

    def rspec(cols):
        if tc is None:
            return pl.BlockSpec((tm, cols), lambda i: (i, 0))
        return pl.BlockSpec((tm, tc), lambda i, j: (i, j))

    def vspec(v):
        g = v.shape[0]
        if tc is None:
            return pl.BlockSpec((None, 1, v.shape[2]), lambda i: (_group_of(i, n_x_tiles, g), 0, 0))
        return pl.BlockSpec((None, 1, tc), lambda i, j: (_group_of(i, n_x_tiles, g), 0, j))

    nr, nv = len(rows), len(vecs)

    def body(*refs):
        ins = [r[...] for r in refs[:nr + nv]]
        res = f(*ins)
        for o_ref, val in zip(refs[nr + nv:], res):
            o_ref[...] = val.astype(o_ref.dtype)

    return pl.pallas_call(
        body, name=name, grid=grid,
        in_specs=[rspec(r.shape[1]) for r in rows] + [vspec(v) for v in vecs],
        out_specs=[rspec(c) for c, _ in outs],
        out_shape=[jax.ShapeDtypeStruct((nrows, c), d) for c, d in outs],
        compiler_params=_params(("parallel",) * len(grid)),
    )(*rows, *vecs)


def _rowwise_bwd(f, rows, vecs, cts, row_grads, *, nrows, n_x_rows, name, tm=BLK, tc=None, residual=None,
                 residual_rows=None, latent_grads_only=False):
    n_x_tiles = n_x_rows // tm
    n_tiles = nrows // tm
    out_rows = n_x_rows if latent_grads_only else nrows
    res_tiles = None if residual_rows is None else residual_rows // tm
    grid = (n_tiles,) if tc is None else (rows[0].shape[1] // tc, n_tiles)
    row_of = (lambda *g: g[0]) if tc is None else (lambda *g: g[1])

    def rspec(cols):
        if tc is None:
            return pl.BlockSpec((tm, cols), lambda i: (i, 0))
        return pl.BlockSpec((tm, tc), lambda j, i: (i, j))

    def vspec(v):
        g = v.shape[0]
        if tc is None:
            return pl.BlockSpec((None, 1, v.shape[2]), lambda i: (_group_of(i, n_x_tiles, g), 0, 0))
        return pl.BlockSpec((None, 1, tc), lambda j, i: (_group_of(i, n_x_tiles, g), 0, j))

    nr, nv, nc = len(rows), len(vecs), len(cts)
    gidx = sorted(row_grads)

    def body(*refs):
        i = row_of(*[pl.program_id(d) for d in range(len(grid))])
        row_vals = [r[...].astype(f32) for r in refs[:nr]]
        vec_vals = [jnp.broadcast_to(r[...].astype(f32), (tm, r.shape[-1])) for r in refs[nr:nr + nv]]
        ct_vals = [r[...].astype(f32) for r in refs[nr + nv:nr + nv + nc]]
        out_refs = refs[nr + nv + nc + (residual is not None):]
        res, vjp = jax.vjp(lambda *a: tuple(o.astype(f32) for o in f(*a)), *row_vals, *vec_vals)
        grads = list(vjp(tuple(ct_vals)))
        if residual is not None:
            extra = refs[nr + nv + nc][...]
            if res_tiles is not None:
                extra = jnp.where(i < res_tiles, extra, 0.0)
            grads[0] = grads[0] + extra
        for o_ref, k in zip(out_refs[:len(gidx)], gidx):
            if latent_grads_only:
                @pl.when(i < n_x_tiles)
                def _():
                    o_ref[...] = grads[k].astype(o_ref.dtype)
            else:
                o_ref[...] = grads[k].astype(o_ref.dtype)
        for o_ref, g, v in zip(out_refs[len(gidx):], grads[nr:], vecs):
            part = jnp.sum(g, axis=0, keepdims=True)
            first = (i == 0) if v.shape[0] == 1 else ((i == 0) | (i == n_x_tiles))

            @pl.when(first)
            def _():
                o_ref[...] = part

            @pl.when(jnp.logical_not(first))
            def _():
                o_ref[...] += part

    sem = ("arbitrary",) if tc is None else ("parallel", "arbitrary")
    gspec = rspec
    if latent_grads_only:
        assert tc is None
        gspec = lambda cols: pl.BlockSpec((tm, cols), lambda i: (jnp.minimum(i, n_x_tiles - 1), 0))
    res_specs, res_args = [], []
    if residual is not None:
        assert tc is None
        clamp = (lambda i: i) if res_tiles is None else (lambda i: jnp.minimum(i, res_tiles - 1))
        res_specs = [pl.BlockSpec((tm, residual.shape[1]), lambda i: (clamp(i), 0))]
        res_args = [residual]
    return pl.pallas_call(
        body, name=name, grid=grid,
        in_specs=[rspec(r.shape[1]) for r in rows] + [vspec(v) for v in vecs] + [rspec(c.shape[1]) for c in cts]
        + res_specs,
        out_specs=[gspec(rows[k].shape[1]) for k in gidx] + [vspec(v) for v in vecs],
        out_shape=[jax.ShapeDtypeStruct((out_rows, rows[k].shape[1]), row_grads[k]) for k in gidx]
        + [jax.ShapeDtypeStruct(v.shape, f32) for v in vecs],
        compiler_params=_params(sem),
    )(*rows, *vecs, *cts, *res_args)


def _rms(x, w):
    return x * lax.rsqrt(jnp.mean(x * x, axis=-1, keepdims=True) + EPS) * w


def _f_normmod(x, w, sc, sh):
    return (_rms(x.astype(f32), w) * (1.0 + sc) + sh,)


def _f_gated_add(x, y, g):
    return (x + g * y.astype(f32),)


def _f_glu_post(x, zv, zg, g, bv, bg):
    return (x + g * ((zv.astype(f32) + bv) * jax.nn.sigmoid(zg.astype(f32) + bg)),)


def _f_silu(x):
    return (jax.nn.silu(x.astype(f32)),)


def _loss_head(x, tgt, w, *, nrows, name):
    d = x.shape[1]
    tm = BLK

    def body(x_ref, t_ref, w_ref, dx_ref, dw_ref, loss_ref):
        i = pl.program_id(0)
        wb = jnp.broadcast_to(w_ref[...], (tm, d))
        y, vjp = jax.vjp(_rms, x_ref[...], wb)
        e = y - t_ref[...]
        dx, dwb = vjp(e * (1.0 / d))
        dx_ref[...] = dx
        dw = jnp.sum(dwb, axis=0, keepdims=True)
        part = jnp.full((8, 128), 0.5 / d, f32) * jnp.sum(e * e)

        @pl.when(i == 0)
        def _():
            dw_ref[...] = dw
            loss_ref[...] = part

        @pl.when(i > 0)
        def _():
            dw_ref[...] += dw
            loss_ref[...] += part

    row = pl.BlockSpec((tm, d), lambda i: (i, 0))
    return pl.pallas_call(
        body, name=name, grid=(nrows // tm,),
        in_specs=[row, row, pl.BlockSpec((1, d), lambda i: (0, 0))],
        out_specs=[row, pl.BlockSpec((1, d), lambda i: (0, 0)), pl.BlockSpec((8, 128), lambda i: (0, 0))],
        out_shape=[jax.ShapeDtypeStruct((nrows, d), f32), jax.ShapeDtypeStruct((1, d), f32),
                   jax.ShapeDtypeStruct((8, 128), f32)],
        compiler_params=_params(("arbitrary",)),
    )(x, tgt, w)


def _adamw(w, g, m, v, name):
    r, c = w.shape
    tr = _pick(r, (512, 256, 128, 64, 32, 16, 8))
    tcol = _pick(c, (1024, 512)) if c % 128 == 0 else c

    def body(w_ref, g_ref, m_ref, v_ref, d_ref, m2_ref, v2_ref):
        gg = g_ref[...]
        m2 = ADAM_B1 * m_ref[...] + (1.0 - ADAM_B1) * gg
        v2 = ADAM_B2 * v_ref[...] + (1.0 - ADAM_B2) * (gg * gg)
        m_hat = m2 / (1.0 - ADAM_B1 ** ADAM_STEP)
        v_hat = v2 / (1.0 - ADAM_B2 ** ADAM_STEP)
        d_ref[...] = -ADAM_LR * (m_hat / (jnp.sqrt(v_hat) + ADAM_EPS) + ADAM_WD * w_ref[...])
        m2_ref[...] = m2
        v2_ref[...] = v2

    spec = pl.BlockSpec((tr, tcol), lambda i, j: (i, j))
    return pl.pallas_call(
        body, name=name, grid=(r // tr, c // tcol), in_specs=[spec] * 4, out_specs=[spec] * 3,
        out_shape=[jax.ShapeDtypeStruct((r, c), f32)] * 3,
        compiler_params=_params(("parallel", "parallel")),
    )(w, g, m, v)


def _swap_quarters(x):
    lane = lax.broadcasted_iota(jnp.int32, x.shape, 1)
    return jnp.where((lane & 63) < 32, pltpu.roll(x, 96, 1), pltpu.roll(x, 32, 1))


def _rope_tables(n_x, n_ctx):
    t = np.arange(n_x)
    quarter = HEAD_DIM // 4
    inv = ROPE_BASE ** (-np.arange(quarter, dtype=np.float64) / quarter)
    ar = (t // GRID_W)[:, None] * inv[None]
    ac = (t % GRID_W)[:, None] * inv[None]
    cos = np.concatenate([np.cos(ar), np.cos(ar), np.cos(ac), np.cos(ac)], axis=1)
    sin = np.concatenate([-np.sin(ar), np.sin(ar), -np.sin(ac), np.sin(ac)], axis=1)
    cos = np.concatenate([cos, np.ones((n_ctx, HEAD_DIM))], axis=0)
    sin = np.concatenate([sin, np.zeros((n_ctx, HEAD_DIM))], axis=0)
    return jnp.asarray(cos, f32), jnp.asarray(sin, f32)


A_W = NA_HEADS * HEAD_DIM
QB0, KB0, VB0 = 3 * A_W, 3 * A_W + NB_Q_HEADS * HEAD_DIM, 3 * A_W + (NB_Q_HEADS + NB_KV_HEADS) * HEAD_DIM
IN_W = VB0 + NB_KV_HEADS * HEAD_DIM


def _qkv_post(qkv, cos, sin, name):
    n = qkv.shape[0]

    def body(x_ref, c_ref, s_ref, o_ref):
        c, s = c_ref[...], s_ref[...]
        o_ref[:, :QB0] = x_ref[:, :QB0].astype(bf16)
        for col in range(QB0, VB0, HEAD_DIM):
            x = x_ref[:, col:col + HEAD_DIM]
            o_ref[:, col:col + HEAD_DIM] = (x * c + _swap_quarters(x) * s).astype(bf16)
        o_ref[:, VB0:] = x_ref[:, VB0:].astype(bf16)

    row = lambda c: pl.BlockSpec((BLK, c), lambda i: (i, 0))
    return pl.pallas_call(
        body, name=name, grid=(n // BLK,), in_specs=[row(IN_W), row(HEAD_DIM), row(HEAD_DIM)],
        out_specs=row(IN_W), out_shape=jax.ShapeDtypeStruct((n, IN_W), bf16),
        compiler_params=_params(("parallel",)),
    )(qkv, cos, sin)


def _qkv_post_bwd(parts, cos, sin, name):
    n = parts[0].shape[0]

    def body(qa, ka, va, qb, kb, vb, c_ref, s_ref, o_ref):
        c, s = c_ref[...], s_ref[...]
        o_ref[:, 0:A_W] = qa[...]
        o_ref[:, A_W:2 * A_W] = ka[...]
        o_ref[:, 2 * A_W:QB0] = va[...]
        for src, col0, width in ((qb, QB0, KB0 - QB0), (kb, KB0, VB0 - KB0)):
            for off in range(0, width, HEAD_DIM):
                g = src[:, off:off + HEAD_DIM].astype(f32)
                o_ref[:, col0 + off:col0 + off + HEAD_DIM] = (g * c + _swap_quarters(g * s)).astype(bf16)
        o_ref[:, VB0:] = vb[...]

    row = lambda c: pl.BlockSpec((BLK, c), lambda i: (i, 0))
    return pl.pallas_call(
        body, name=name, grid=(n // BLK,),
        in_specs=[row(p.shape[1]) for p in parts] + [row(HEAD_DIM), row(HEAD_DIM)],
        out_specs=row(IN_W), out_shape=jax.ShapeDtypeStruct((n, IN_W), bf16),
        compiler_params=_params(("parallel",)),
    )(*parts, cos, sin)


def _valid(kind, qpos, kpos, n_x):
    ok = (kpos >= 0) & (kpos < n_x) & (qpos >= 0) & (qpos < n_x)
    if kind == "na":
        rows = n_x // GRID_W
        qr, qc = lax.shift_right_arithmetic(qpos, 6), qpos & (GRID_W - 1)
        kr, kc = lax.shift_right_arithmetic(kpos, 6), kpos & (GRID_W - 1)
        kr0 = jnp.clip(qr - NA_ROWS // 2, 0, rows - NA_ROWS)
        ws = jnp.clip(qc - NA_COLS // 2, 0, GRID_W - NA_COLS)
        return ok & (kr >= kr0) & (kr < kr0 + NA_ROWS) & (kc >= ws) & (kc < ws + NA_COLS)
    return ok & (jnp.abs(kpos - qpos) <= SW_RADIUS)


def _stack_heads(x, g):
    if g == 1:
        return x
    return jnp.concatenate([x[:, a * HEAD_DIM:(a + 1) * HEAD_DIM] for a in range(g)], axis=0)


def _unstack_heads(x, g):
    if g == 1:
        return x
    r = x.shape[0] // g
    return jnp.concatenate([x[a * r:(a + 1) * r] for a in range(g)], axis=1)


def _add_per_head(s, m, g):
    if g == 1:
        return s + m
    return (s.reshape((g,) + m.shape) + m[None]).reshape(s.shape)


class _AttnCfg:
    def __init__(self, kind, n_x, n_tot):
        self.kind, self.n_x, self.n_tot = kind, n_x, n_tot
        self.n_xb, self.n_blk = n_x // BLK, n_tot // BLK
        if kind == "na":
            self.g, self.nkv, self.q0, self.k0, self.v0 = 1, NA_HEADS, 0, NA_HEADS, 2 * NA_HEADS
        else:
            self.g, self.nkv = NB_GROUP, NB_KV_HEADS
            self.q0, self.k0, self.v0 = QB0 // (NB_GROUP * HEAD_DIM), KB0 // HEAD_DIM, VB0 // HEAD_DIM
        self.r = BLK * self.g
        self.qw = HEAD_DIM * self.g
        self.scale = HEAD_DIM ** -0.5
        last = self.n_xb - 1
        rows = lax.broadcasted_iota(jnp.int32, (BLK, 3 * BLK), 0)
        cols = lax.broadcasted_iota(jnp.int32, (BLK, 3 * BLK), 1)
        self.mask_q = jnp.stack([jnp.where(_valid(kind, i * BLK + rows, (i - 1) * BLK + cols, n_x), 0.0, NEG_INF)
                                 for i in (0, 1, last, self.n_xb)]).astype(f32)
        self.mask_k = jnp.stack([jnp.where(_valid(kind, (j - 1) * BLK + cols.T, j * BLK + rows.T, n_x), 0.0, NEG_INF)
                                 for j in (0, 1, 2, last - 1, last)]).astype(f32)
        self.pattern = lambda i: jnp.where(i == 0, 0, jnp.where(i < last, 1, jnp.where(i == last, 2, 3)))
        self.pattern_k = lambda j: jnp.where(j <= 1, j, jnp.where(j < last - 1, 2, jnp.where(j == last - 1, 3, 4)))


def _attn_fwd(qkv, cfg, extra, name, rider=None):
    g, r, qw, n_xb, n_x = cfg.g, cfg.r, cfg.qw, cfg.n_xb, cfg.n_x
    last = n_xb - 1

    def body(q_ref, kp, ko, kn, vp, vo, vn, kc_ref, vc_ref, ex_ref, m_ref, o_ref, lse_ref):
        q = _stack_heads(q_ref[...], g)
        kw = jnp.concatenate([kp[...], ko[...], kn[...]], axis=0)
        vw = jnp.concatenate([vp[...], vo[...], vn[...]], axis=0)
        s = lax.dot_general(q, kw, NT, preferred_element_type=f32) * cfg.scale
        if cfg.kind == "na":
            s = s + ex_ref[...]
        s = _add_per_head(s, m_ref[...], g)
        sc = lax.dot_general(q, kc_ref[...], NT, preferred_element_type=f32) * cfg.scale
        m = jnp.maximum(jnp.max(s, axis=-1, keepdims=True), jnp.max(sc, axis=-1, keepdims=True))
        if cfg.kind == "swa":
            m = jnp.maximum(m, ex_ref[...])
        p, pc = jnp.exp(s - m), jnp.exp(sc - m)
        l = jnp.sum(p, axis=-1, keepdims=True) + jnp.sum(pc, axis=-1, keepdims=True)
        if cfg.kind == "swa":
            l = l + jnp.exp(ex_ref[...] - m)
        o = jnp.dot(p.astype(bf16), vw, preferred_element_type=f32) + jnp.dot(pc.astype(bf16), vc_ref[...],
                                                                             preferred_element_type=f32)
        o_ref[...] = _unstack_heads(o / l, g).astype(bf16)
        lse_ref[...] = m + jnp.log(l)

    kv = lambda col0, f: pl.BlockSpec((BLK, HEAD_DIM), lambda h, i: (f(i), col0 + h))
    prev = lambda i: jnp.clip(i - 1, 0, last)
    own = lambda i: jnp.minimum(i, last)
    nxt = lambda i: jnp.minimum(i + 1, last)
    ctx = lambda i: n_xb
    if cfg.kind == "na":
        ex_spec = pl.BlockSpec((None, BLK, 3 * BLK), lambda h, i: (h, 0, 0))
    else:
        ex_spec = pl.BlockSpec((None, r, 1), lambda h, i: (h, 0, 0))
    m_spec = pl.BlockSpec((None, BLK, 3 * BLK), lambda h, i: (cfg.pattern(i), 0, 0))
    return _call(
        body, (qkv,) * 9 + (extra, cfg.mask_q), name=name, grid=(cfg.nkv, cfg.n_blk),
        in_specs=[pl.BlockSpec((BLK, qw), lambda h, i: (i, cfg.q0 + h)),
                  kv(cfg.k0, prev), kv(cfg.k0, own), kv(cfg.k0, nxt),
                  kv(cfg.v0, prev), kv(cfg.v0, own), kv(cfg.v0, nxt),
                  kv(cfg.k0, ctx), kv(cfg.v0, ctx), ex_spec, m_spec],
        out_specs=[pl.BlockSpec((BLK, qw), lambda h, i: (i, h)),
                   pl.BlockSpec((None, None, r, 1), lambda h, i: (h, i, 0, 0))],
        out_shape=[jax.ShapeDtypeStruct((cfg.n_tot, cfg.nkv * qw), bf16),
                   jax.ShapeDtypeStruct((cfg.nkv, cfg.n_blk, r, 1), f32)],
        sem=("parallel", "parallel"), rider=rider)


def _attn_bwd_q(qkv, o, do, lse, cfg, extra, name, rider=None):
    g, r, qw, n_xb, n_x = cfg.g, cfg.r, cfg.qw, cfg.n_xb, cfg.n_x
    last = n_xb - 1
    do_col0 = 0 if cfg.kind == "na" else (NA_HEADS * HEAD_DIM) // qw

    def body(q_ref, kp, ko, kn, vp, vo, vn, kc_ref, vc_ref, ex_ref, o_ref, do_ref, lse_ref, m_ref,
             dq_ref, delta_ref, dex_ref):
        i = pl.program_id(1)
        q = _stack_heads(q_ref[...], g)
        dout = _stack_heads(do_ref[...], g)
        out = _stack_heads(o_ref[...], g)
        delta = jnp.sum(dout.astype(f32) * out.astype(f32), axis=-1, keepdims=True)
        delta_ref[...] = delta
        kw = jnp.concatenate([kp[...], ko[...], kn[...]], axis=0)
        vw = jnp.concatenate([vp[...], vo[...], vn[...]], axis=0)
        s = lax.dot_general(q, kw, NT, preferred_element_type=f32) * cfg.scale
        if cfg.kind == "na":
            s = s + ex_ref[...]
        s = _add_per_head(s, m_ref[...], g)
        sc = lax.dot_general(q, kc_ref[...], NT, preferred_element_type=f32) * cfg.scale
        lse_v = lse_ref[...]
        p, pc = jnp.exp(s - lse_v), jnp.exp(sc - lse_v)
        dp = lax.dot_general(dout, vw, NT, preferred_element_type=f32)
        dpc = lax.dot_general(dout, vc_ref[...], NT, preferred_element_type=f32)
        ds, dsc = p * (dp - delta), pc * (dpc - delta)
        dq = jnp.dot(ds.astype(bf16), kw, preferred_element_type=f32) + jnp.dot(dsc.astype(bf16), kc_ref[...],
                                                                             preferred_element_type=f32)
        dq_ref[...] = _unstack_heads(dq * cfg.scale, g).astype(bf16)
        dex = ds if cfg.kind == "na" else -jnp.exp(ex_ref[...] - lse_v) * delta

        @pl.when(i == 0)
        def _():
            dex_ref[...] = dex

        @pl.when(i > 0)
        def _():
            dex_ref[...] += dex

    kv = lambda col0, f: pl.BlockSpec((BLK, HEAD_DIM), lambda h, i: (f(i), col0 + h))
    prev = lambda i: jnp.clip(i - 1, 0, last)
    own = lambda i: jnp.minimum(i, last)
    nxt = lambda i: jnp.minimum(i + 1, last)
    ctx = lambda i: n_xb
    if cfg.kind == "na":
        ex_spec = pl.BlockSpec((None, BLK, 3 * BLK), lambda h, i: (h, 0, 0))
    else:
        ex_spec = pl.BlockSpec((None, r, 1), lambda h, i: (h, 0, 0))
    stat = pl.BlockSpec((None, None, r, 1), lambda h, i: (h, i, 0, 0))
    m_spec = pl.BlockSpec((None, BLK, 3 * BLK), lambda h, i: (cfg.pattern(i), 0, 0))
    return _call(
        body, (qkv,) * 9 + (extra, o, do, lse, cfg.mask_q), name=name, grid=(cfg.nkv, cfg.n_blk),
        in_specs=[pl.BlockSpec((BLK, qw), lambda h, i: (i, cfg.q0 + h)),
                  kv(cfg.k0, prev), kv(cfg.k0, own), kv(cfg.k0, nxt),
                  kv(cfg.v0, prev), kv(cfg.v0, own), kv(cfg.v0, nxt),
                  kv(cfg.k0, ctx), kv(cfg.v0, ctx), ex_spec,
                  pl.BlockSpec((BLK, qw), lambda h, i: (i, h)),
                  pl.BlockSpec((BLK, qw), lambda h, i: (i, do_col0 + h)), stat, m_spec],
        out_specs=[pl.BlockSpec((BLK, qw), lambda h, i: (i, h)), stat, ex_spec],
        out_shape=[jax.ShapeDtypeStruct((cfg.n_tot, cfg.nkv * qw), bf16),
                   jax.ShapeDtypeStruct((cfg.nkv, cfg.n_blk, r, 1), f32),
                   jax.ShapeDtypeStruct(extra.shape, f32)],
        sem=("parallel", "arbitrary"), rider=rider)


def _attn_bwd_kv(qkv, do, lse, delta, cfg, bias_t, name, rider=None):
    g, r, qw, n_xb, n_x, n_blk = cfg.g, cfg.r, cfg.qw, cfg.n_xb, cfg.n_x, cfg.n_blk
    last = n_xb - 1
    do_col0 = 0 if cfg.kind == "na" else (NA_HEADS * HEAD_DIM) // qw
    has_bias = cfg.kind == "na"

    def body(*refs):
        qs, dos, lses, dels = refs[0:3], refs[3:6], refs[6:9], refs[9:12]
        qj_ref, doj_ref, lsej_ref, delj_ref, k_ref, v_ref, kc_ref, vc_ref, m_ref = refs[12:21]
        rest = refs[21:]
        if has_bias:
            b_ref, rest = rest[0], rest[1:]
        dk_ref, dv_ref, dkc_acc, dvc_acc = rest
        j = pl.program_id(1)

        @pl.when(j == 0)
        def _():
            dkc_acc[...] = jnp.zeros_like(dkc_acc)
            dvc_acc[...] = jnp.zeros_like(dvc_acc)

        qj, doj = _stack_heads(qj_ref[...], g), _stack_heads(doj_ref[...], g)
        sc = lax.dot_general(qj, kc_ref[...], NT, preferred_element_type=f32) * cfg.scale
        pc = jnp.exp(sc - lsej_ref[...])
        dvc_acc[...] += lax.dot_general(pc.astype(bf16), doj, TN, preferred_element_type=f32)
        dpc = lax.dot_general(doj, vc_ref[...], NT, preferred_element_type=f32)
        dsc = pc * (dpc - delj_ref[...])
        dkc_acc[...] += lax.dot_general(dsc.astype(bf16), qj, TN, preferred_element_type=f32) * cfg.scale

        @pl.when(j < n_xb)
        def _():
            qw_all = jnp.concatenate([_stack_heads(x[...], g) for x in qs], axis=0)
            do_all = jnp.concatenate([_stack_heads(x[...], g) for x in dos], axis=0)
            lse_all = jnp.concatenate([x[...] for x in lses], axis=0)
            del_all = jnp.concatenate([x[...] for x in dels], axis=0)
            s = lax.dot_general(qw_all, k_ref[...], NT, preferred_element_type=f32) * cfg.scale
            if has_bias:
                s = s + b_ref[...]
            if g == 1:
                s = s + m_ref[...]
            else:
                s = (s.reshape(3, g, BLK, BLK) + m_ref[...].reshape(3, 1, BLK, BLK)).reshape(3 * r, BLK)
            p = jnp.exp(s - lse_all)
            dv_ref[...] = lax.dot_general(p.astype(bf16), do_all, TN, preferred_element_type=f32).astype(bf16)
            dp = lax.dot_general(do_all, v_ref[...], NT, preferred_element_type=f32)
            ds = p * (dp - del_all)
            dk_ref[...] = (lax.dot_general(ds.astype(bf16), qw_all, TN, preferred_element_type=f32)
                           * cfg.scale).astype(bf16)

        @pl.when(j == n_xb)
        def _():
            dk_ref[...] = dkc_acc[...].astype(bf16)
            dv_ref[...] = dvc_acc[...].astype(bf16)

    prev = lambda j: jnp.clip(j - 1, 0, last)
    own = lambda j: jnp.minimum(j, last)
    nxt = lambda j: jnp.minimum(j + 1, last)
    same = lambda j: j
    qspec = lambda f: pl.BlockSpec((BLK, qw), lambda h, j: (f(j), cfg.q0 + h))
    dospec = lambda f: pl.BlockSpec((BLK, qw), lambda h, j: (f(j), do_col0 + h))
    stat = lambda f: pl.BlockSpec((None, None, r, 1), lambda h, j: (h, f(j), 0, 0))
    kv = lambda col0, f: pl.BlockSpec((BLK, HEAD_DIM), lambda h, j: (f(j), col0 + h))
    ctx = lambda j: n_xb
    in_specs = ([qspec(f) for f in (prev, own, nxt)] + [dospec(f) for f in (prev, own, nxt)]
                + [stat(f) for f in (prev, own, nxt)] * 2
                + [qspec(same), dospec(same), stat(same), stat(same),
                   kv(cfg.k0, same), kv(cfg.v0, same), kv(cfg.k0, ctx), kv(cfg.v0, ctx),
                   pl.BlockSpec((None, 3 * BLK, BLK), lambda h, j: (cfg.pattern_k(j), 0, 0))])
    args = [qkv] * 3 + [do] * 3 + [lse] * 3 + [delta] * 3 + [qkv, do, lse, delta, qkv, qkv, qkv, qkv, cfg.mask_k]
    if has_bias:
        in_specs.append(pl.BlockSpec((None, 3 * BLK, BLK), lambda h, j: (h, 0, 0)))
        args.append(bias_t)
    out = pl.BlockSpec((BLK, HEAD_DIM), lambda h, j: (j, h))
    return _call(
        body, args, name=name, grid=(cfg.nkv, n_blk), in_specs=in_specs, out_specs=[out, out],
        out_shape=[jax.ShapeDtypeStruct((cfg.n_tot, cfg.nkv * HEAD_DIM), bf16)] * 2,
        scratch_shapes=[pltpu.VMEM((BLK, HEAD_DIM), f32)] * 2, sem=("parallel", "arbitrary"), rider=rider)


def _toeplitz_basis():
    qc, kc = np.meshgrid(np.arange(GRID_W), np.arange(GRID_W), indexing="ij")
    e = (kc - qc + NA_COLS - 1)[None] == np.arange(2 * NA_COLS - 1)[:, None, None]
    return e.reshape(2 * NA_COLS - 1, GRID_W * GRID_W).astype(np.float32)


def _whole(f, ins, outs, name):
    def body(*refs):
        res = f(*[r[...] for r in refs[:len(ins)]])
        for o_ref, val in zip(refs[len(ins):], res):
            o_ref[...] = val.astype(o_ref.dtype)

    return pl.pallas_call(body, name=name,
                          out_shape=[jax.ShapeDtypeStruct(s, d) for s, d in outs])(*ins)


def _whole_bwd(f, ins, cts, name):
    n = len(ins)

    def body(*refs):
        _, vjp = jax.vjp(f, *[r[...] for r in refs[:n]])
        grads = vjp(tuple(r[...] for r in refs[n:n + len(cts)]))
        for o_ref, g in zip(refs[n + len(cts):], grads):
            o_ref[...] = g

    return pl.pallas_call(body, name=name,
                          out_shape=[jax.ShapeDtypeStruct(a.shape, f32) for a in ins])(*ins, *cts)


def _f_discretise(ar, ai, ldt, br, bi):
    dt = jnp.exp(ldt)
    mag = jnp.exp(ar * dt)
    lam_r, lam_i = mag * jnp.cos(ai * dt), mag * jnp.sin(ai * dt)
    den = ar * ar + ai * ai
    nr = lam_r - 1.0
    coef_r = (nr * ar + lam_i * ai) / den
    coef_i = (lam_i * ar - nr * ai) / den
    return (lam_r, lam_i, coef_r[None] * br - coef_i[None] * bi, coef_r[None] * bi + coef_i[None] * br)


SEG_LEN = BLK // SCAN_SEG
N_STATE = GROUPS_PER_CHUNK * SSM_STATE
CHUNK_CH = GROUPS_PER_CHUNK * SSM_GROUP


def _scan_perm():
    r = np.arange(BLK)
    t = (r % SCAN_SEG) * SEG_LEN + r // SCAN_SEG
    pm = np.zeros((BLK, BLK), np.float32)
    pm[r, t] = 1.0
    return jnp.asarray(pm, bf16), jnp.asarray(pm.T, bf16)


def _row_perm(x, pm, name, *, out_dtype, split, add=None, add_rows=None):
    n, c = x.shape
    add_tiles = None if add is None else add_rows // BLK

    def body(*refs):
        x_ref, pm_ref = refs[0], refs[1]
        o_ref = refs[-1]
        xv = x_ref[...]
        if split:
            out = _unpermute(pm_ref[...], xv.astype(f32))
        else:
            out = jnp.dot(pm_ref[...], xv.astype(bf16), preferred_element_type=f32)
        if add is not None:
            out = out + jnp.where(pl.program_id(0) < add_tiles, refs[2][...], 0.0)
        o_ref[...] = out.astype(o_ref.dtype)

    row = pl.BlockSpec((BLK, c), lambda i: (i, 0))
    in_specs, args = [row, pl.BlockSpec((BLK, BLK), lambda i: (0, 0))], [x, pm]
    if add is not None:
        in_specs.append(pl.BlockSpec((BLK, c), lambda i: (jnp.minimum(i, add_tiles - 1), 0)))
        args.append(add)
    return pl.pallas_call(
        body, name=name, grid=(n // BLK,), in_specs=in_specs, out_specs=row,
        out_shape=jax.ShapeDtypeStruct((n, c), out_dtype), compiler_params=_params(("parallel",)),
    )(*args)


def _glu_pre(u, yf, yr, dvec, pmt, n_x, name):
    d = u.shape[1]

    def body(u_ref, yf_ref, yr_ref, d_ref, pmt_ref, o_ref):
        y = d_ref[...] * u_ref[...] + _unpermute(pmt_ref[...], yf_ref[...] + yr_ref[...])
        o_ref[...] = jax.nn.gelu(y).astype(bf16)

    row = pl.BlockSpec((BLK, d), lambda i: (i, 0))
    return pl.pallas_call(
        body, name=name, grid=(n_x // BLK,),
        in_specs=[row, row, row, pl.BlockSpec((1, d), lambda i: (0, 0)), pl.BlockSpec((BLK, BLK), lambda i: (0, 0))],
        out_specs=row, out_shape=jax.ShapeDtypeStruct((n_x, d), bf16), compiler_params=_params(("parallel",)),
    )(u, yf, yr, dvec, pmt)


def _glu_pre_bwd(u, yf, yr, dvec, pm, pmt, dgy, n_x, name):
    d = u.shape[1]

    def body(u_ref, yf_ref, yr_ref, d_ref, pm_ref, pmt_ref, ct_ref, dyp_ref, dud_ref, dd_ref):
        uv = u_ref[...]
        y = d_ref[...] * uv + _unpermute(pmt_ref[...], yf_ref[...] + yr_ref[...])
        _, vjp = jax.vjp(jax.nn.gelu, y)
        dy = vjp(ct_ref[...])[0]
        dyp_ref[...] = jnp.dot(pm_ref[...], dy.astype(bf16), preferred_element_type=f32).astype(bf16)
        dud_ref[...] = d_ref[...] * dy
        part = jnp.sum(dy * uv, axis=0, keepdims=True)

        @pl.when(pl.program_id(0) == 0)
        def _():
            dd_ref[...] = part

        @pl.when(pl.program_id(0) > 0)
        def _():
            dd_ref[...] += part

    row = pl.BlockSpec((BLK, d), lambda i: (i, 0))
    vec = pl.BlockSpec((1, d), lambda i: (0, 0))
    sq = pl.BlockSpec((BLK, BLK), lambda i: (0, 0))
    return pl.pallas_call(
        body, name=name, grid=(n_x // BLK,), in_specs=[row, row, row, vec, sq, sq, row],
        out_specs=[row, row, vec],
        out_shape=[jax.ShapeDtypeStruct((n_x, d), bf16), jax.ShapeDtypeStruct((n_x, d), f32),
                   jax.ShapeDtypeStruct((1, d), f32)],
        compiler_params=_params(("arbitrary",)),
    )(u, yf, yr, dvec, pm, pmt, dgy)


def _block_order(n_xb, n_blk, reverse):
    n_cb = n_blk - n_xb
    if reverse:
        return lambda cc: jnp.where(cc < n_cb, n_blk - 1 - cc, n_xb - 1 - (cc - n_cb))
    return lambda cc: jnp.where(cc < n_cb, n_xb + cc, cc - n_cb)


def _unpermute(pmt, y):
    hi = y.astype(bf16)
    lo = (y - hi.astype(f32)).astype(bf16)
    return jnp.dot(pmt, hi, preferred_element_type=f32) + jnp.dot(pmt, lo, preferred_element_type=f32)


def _lam_pow(lr, li):
    for _ in range(int(math.log2(SEG_LEN))):
        lr, li = lr * lr - li * li, 2.0 * lr * li
    return lr, li


def _s5_fwd(up, lam, bblk, cblk, n_x, reverse, name, rider=None):
    n_tot, d = up.shape
    nq, n_blk, n_xb = d // CHUNK_CH, n_tot // BLK, n_x // BLK
    order = _block_order(n_xb, n_blk, reverse)
    ns = N_STATE
    seg_order = range(SCAN_SEG - 1, -1, -1) if reverse else range(SCAN_SEG)

    def body(u_ref, lam_ref, b_ref, c_ref, y_ref, cp_ref, bu_ref, st_ref, carry_ref):
        cc = pl.program_id(1)

        @pl.when(cc == 0)
        def _():
            carry_ref[...] = jnp.zeros_like(carry_ref)

        up_v = u_ref[...]
        bu_ref[0] = jnp.dot(up_v, b_ref[0], preferred_element_type=f32)
        bu_ref[1] = jnp.dot(up_v, b_ref[1], preferred_element_type=f32)
        lr, li = lam_ref[0:1, :], lam_ref[1:2, :]
        lrb, lib = jnp.broadcast_to(lr, (SCAN_SEG, ns)), jnp.broadcast_to(li, (SCAN_SEG, ns))

        def step(p, s, store):
            sr, si = s
            j = SEG_LEN - 1 - p if reverse else p
            off = pl.multiple_of(j * SCAN_SEG, SCAN_SEG)
            nsr = lrb * sr - lib * si + bu_ref[0, pl.ds(off, SCAN_SEG), :]
            nsi = lrb * si + lib * sr + bu_ref[1, pl.ds(off, SCAN_SEG), :]
            if store:
                st_ref[0, pl.ds(off, SCAN_SEG), :] = nsr
                st_ref[1, pl.ds(off, SCAN_SEG), :] = nsi
            return nsr, nsi

        zero = jnp.zeros((SCAN_SEG, ns), f32)
        er, ei = lax.fori_loop(0, SEG_LEN, lambda j, s: step(j, s, False), (zero, zero))
        pr, pi = _lam_pow(lr, li)
        cr, ci = carry_ref[0, 0:1, :], carry_ref[1, 0:1, :]
        rows_r, rows_i = [None] * SCAN_SEG, [None] * SCAN_SEG
        for k in seg_order:
            rows_r[k], rows_i[k] = cr, ci
            cr, ci = pr * cr - pi * ci + er[k:k + 1], pr * ci + pi * cr + ei[k:k + 1]
        carry_ref[0] = jnp.broadcast_to(cr, (SCAN_SEG, ns))
        carry_ref[1] = jnp.broadcast_to(ci, (SCAN_SEG, ns))
        cpr, cpi = jnp.concatenate(rows_r, axis=0), jnp.concatenate(rows_i, axis=0)
        cp_ref[0] = cpr
        cp_ref[1] = cpi
        lax.fori_loop(0, SEG_LEN, lambda j, s: step(j, s, True), (cpr, cpi))
        y_ref[...] = (jnp.dot(st_ref[0].astype(bf16), c_ref[0], preferred_element_type=f32)
                      - jnp.dot(st_ref[1].astype(bf16), c_ref[1], preferred_element_type=f32))

    return _call(
        body, (up, lam, bblk, cblk), name=name, grid=(nq, n_blk),
        in_specs=[pl.BlockSpec((BLK, CHUNK_CH), lambda q,cc: (order(cc), q)),
                  pl.BlockSpec((None, 2, ns), lambda q, cc: (q, 0, 0)),
                  pl.BlockSpec((None, 2, CHUNK_CH, ns), lambda q,cc: (q, 0, 0, 0)),
                  pl.BlockSpec((None, 2, ns, CHUNK_CH), lambda q,cc: (q, 0, 0, 0))],
        out_specs=[pl.BlockSpec((BLK, CHUNK_CH), lambda q,cc: (order(cc), q)),
                   pl.BlockSpec((None, None, 2, SCAN_SEG, ns), lambda q, cc: (q, cc, 0, 0, 0))],
        out_shape=[jax.ShapeDtypeStruct((n_tot, d), f32),
                   jax.ShapeDtypeStruct((nq, n_blk, 2, SCAN_SEG, ns), f32)],
        scratch_shapes=[pltpu.VMEM((2, BLK, ns), f32), pltpu.VMEM((2, BLK, ns), f32),
                        pltpu.VMEM((2, SCAN_SEG, ns), f32)],
        sem=("parallel", "arbitrary"), rider=rider)


def _s5_bwd(u, dy, cprev, lam, bblk, cblk, bblk_t, cblk_t, n_x, reverse, name, add=None, rider=None):
    n_tot, d = u.shape
    nq, n_blk, n_xb = d // CHUNK_CH, n_tot // BLK, n_x // BLK
    order = _block_order(n_xb, n_blk, reverse)
    ns = N_STATE
    blk_of = lambda step: order(n_blk - 1 - step)
    has_add = add is not None
    jof = (lambda p: SEG_LEN - 1 - p) if reverse else (lambda p: p)
    adj_seg_order = range(SCAN_SEG) if reverse else range(SCAN_SEG - 1, -1, -1)

    def body(*refs):
        u_ref, dy_ref, cp_ref, lam_ref, b_ref, c_ref, bt_ref, ct_ref = refs[:8]
        rest = refs[8:]
        if has_add:
            add_ref, rest = rest[0], rest[1:]
        du_ref, dlam_ref, db_ref, dc_ref, bu_ref, st_ref, ds_ref, acarry_ref = rest
        step_id = pl.program_id(1)
        is_x = blk_of(step_id) < n_xb

        @pl.when(step_id == 0)
        def _():
            acarry_ref[...] = jnp.zeros_like(acarry_ref)
            dlam_ref[...] = jnp.zeros_like(dlam_ref)
            db_ref[...] = jnp.zeros_like(db_ref)
            dc_ref[...] = jnp.zeros_like(dc_ref)

        up = u_ref[...]
        bu_ref[0] = jnp.dot(up, b_ref[0], preferred_element_type=f32)
        bu_ref[1] = jnp.dot(up, b_ref[1], preferred_element_type=f32)
        lr, li = lam_ref[0:1, :], lam_ref[1:2, :]
        lrb, lib = jnp.broadcast_to(lr, (SCAN_SEG, ns)), jnp.broadcast_to(li, (SCAN_SEG, ns))
        cpr, cpi = cp_ref[0], cp_ref[1]

        def fstep(p, s):
            sr, si = s
            off = pl.multiple_of(jof(p) * SCAN_SEG, SCAN_SEG)
            nsr = lrb * sr - lib * si + bu_ref[0, pl.ds(off, SCAN_SEG), :]
            nsi = lrb * si + lib * sr + bu_ref[1, pl.ds(off, SCAN_SEG), :]
            st_ref[0, pl.ds(off, SCAN_SEG), :] = nsr
            st_ref[1, pl.ds(off, SCAN_SEG), :] = nsi
            return nsr, nsi

        lax.fori_loop(0, SEG_LEN, fstep, (cpr, cpi))

        dyp = jnp.where(is_x, dy_ref[...], jnp.zeros_like(dy_ref))
        ds_ref[0] = jnp.dot(dyp, ct_ref[0], preferred_element_type=f32)
        ds_ref[1] = -jnp.dot(dyp, ct_ref[1], preferred_element_type=f32)

        def adj(p, a):
            ar, ai = a
            off = pl.multiple_of(jof(p) * SCAN_SEG, SCAN_SEG)
            nar = ds_ref[0, pl.ds(off, SCAN_SEG), :] + lrb * ar + lib * ai
            nai = ds_ref[1, pl.ds(off, SCAN_SEG), :] - lib * ar + lrb * ai
            return nar, nai

        zero = jnp.zeros((SCAN_SEG, ns), f32)
        er, ei = lax.fori_loop(0, SEG_LEN, lambda jj, a: adj(SEG_LEN - 1 - jj, a), (zero, zero))
        pr, pi = _lam_pow(lr, li)
        nr_, ni_ = acarry_ref[0, 0:1, :], acarry_ref[1, 0:1, :]
        rows_r, rows_i = [None] * SCAN_SEG, [None] * SCAN_SEG
        for k in adj_seg_order:
            rows_r[k], rows_i[k] = nr_, ni_
            nr_, ni_ = er[k:k + 1] + pr * nr_ + pi * ni_, ei[k:k + 1] + pr * ni_ - pi * nr_
        acarry_ref[0] = jnp.broadcast_to(nr_, (SCAN_SEG, ns))
        acarry_ref[1] = jnp.broadcast_to(ni_, (SCAN_SEG, ns))
        an_r, an_i = jnp.concatenate(rows_r, axis=0), jnp.concatenate(rows_i, axis=0)

        def adj2(jj, carry):
            ar, ai, glr, gli = carry
            p = SEG_LEN - 1 - jj
            nar, nai = adj(p, (ar, ai))
            off = pl.multiple_of(jof(p) * SCAN_SEG, SCAN_SEG)
            ds_ref[0, pl.ds(off, SCAN_SEG), :] = nar
            ds_ref[1, pl.ds(off, SCAN_SEG), :] = nai
            poff = pl.multiple_of(jof(p - 1) * SCAN_SEG, SCAN_SEG)
            spr, spi = st_ref[0, pl.ds(poff, SCAN_SEG), :], st_ref[1, pl.ds(poff, SCAN_SEG), :]
            return nar, nai, glr + nar * spr + nai * spi, gli - nar * spi + nai * spr

        ar, ai, glr, gli = lax.fori_loop(0, SEG_LEN - 1, adj2, (an_r, an_i, zero, zero))
        nar, nai = adj(0, (ar, ai))
        first = jof(0) * SCAN_SEG
        ds_ref[0, first:first + SCAN_SEG, :] = nar
        ds_ref[1, first:first + SCAN_SEG, :] = nai
        glr = glr + nar * cpr + nai * cpi
        gli = gli - nar * cpi + nai * cpr
        dlam_ref[0:1, :] += jnp.sum(glr, axis=0, keepdims=True)
        dlam_ref[1:2, :] += jnp.sum(gli, axis=0, keepdims=True)

        a_r, a_i = ds_ref[0].astype(bf16), ds_ref[1].astype(bf16)
        du = jnp.dot(a_r, bt_ref[0], preferred_element_type=f32) + jnp.dot(a_i, bt_ref[1],
                                                                        preferred_element_type=f32)
        if has_add:
            du = du + add_ref[...]
        du_ref[...] = du
        db_ref[0] += lax.dot_general(up, a_r, TN, preferred_element_type=f32)
        db_ref[1] += lax.dot_general(up, a_i, TN, preferred_element_type=f32)
        dc_ref[0] += lax.dot_general(st_ref[0].astype(bf16), dyp, TN, preferred_element_type=f32)
        dc_ref[1] -= lax.dot_general(st_ref[1].astype(bf16), dyp, TN, preferred_element_type=f32)

    tok = pl.BlockSpec((BLK, CHUNK_CH), lambda q,s: (blk_of(s), q))
    in_specs = [tok, pl.BlockSpec((BLK, CHUNK_CH), lambda q,s: (jnp.minimum(blk_of(s), n_xb - 1), q)),
                pl.BlockSpec((None, None, 2, SCAN_SEG, ns), lambda q, s: (q, n_blk - 1 - s, 0, 0, 0)),
                pl.BlockSpec((None, 2, ns), lambda q, s: (q, 0, 0)),
                pl.BlockSpec((None, 2, CHUNK_CH, ns), lambda q,s: (q, 0, 0, 0)),
                pl.BlockSpec((None, 2, ns, CHUNK_CH), lambda q,s: (q, 0, 0, 0)),
                pl.BlockSpec((None, 2, ns, CHUNK_CH), lambda q,s: (q, 0, 0, 0)),
                pl.BlockSpec((None, 2, CHUNK_CH, ns), lambda q,s: (q, 0, 0, 0))]
    args = [u, dy, cprev, lam, bblk, cblk, bblk_t, cblk_t]
    if has_add:
        in_specs.append(tok)
        args.append(add)
    return _call(
        body, args, name=name, grid=(nq, n_blk), in_specs=in_specs,
        out_specs=[tok, pl.BlockSpec((None, 2, ns), lambda q, s: (q, 0, 0)),
                   pl.BlockSpec((None, 2, CHUNK_CH, ns), lambda q,s: (q, 0, 0, 0)),
                   pl.BlockSpec((None, 2, ns, CHUNK_CH), lambda q,s: (q, 0, 0, 0))],
        out_shape=[jax.ShapeDtypeStruct((n_tot, d), f32), jax.ShapeDtypeStruct((nq, 2, ns), f32),
                   jax.ShapeDtypeStruct((nq, 2, CHUNK_CH, ns), f32),
                   jax.ShapeDtypeStruct((nq, 2, ns, CHUNK_CH), f32)],
        scratch_shapes=[pltpu.VMEM((2, BLK, ns), f32), pltpu.VMEM((2, BLK, ns), f32),
                        pltpu.VMEM((2, BLK, ns), f32), pltpu.VMEM((2, SCAN_SEG, ns), f32)],
        sem=("parallel", "arbitrary"), rider=rider)


def _exchange(src, gather, name):
    return _run_exchange(_Exchange(src, gather), name)


def _sum0(x, name):
    n, r, c = x.shape
    tr = _pick(r, (512, 256, 128, 64, 32, 16, 8))

    def body(x_ref, o_ref):
        acc = x_ref[0].astype(f32)
        for k in range(1, n):
            acc = acc + x_ref[k].astype(f32)
        o_ref[...] = acc

    return pl.pallas_call(
        body, name=name, grid=(r // tr,), in_specs=[pl.BlockSpec((n, tr, c), lambda i: (0, i, 0))],
        out_specs=pl.BlockSpec((tr, c), lambda i: (i, 0)), out_shape=jax.ShapeDtypeStruct((r, c), f32),
        compiler_params=_params(("parallel",)),
    )(x)


LANES = 1024


def _pack(parts, dtype):
    flat = jnp.concatenate([p.astype(dtype).reshape(-1) for p in parts])
    pad = (-flat.shape[0]) % (16 * LANES)
    if pad:
        flat = jnp.concatenate([flat, jnp.zeros((pad,), dtype)])
    return flat.reshape(-1, LANES)


def _unpack(flat, shapes):
    out, off = [], 0
    for s in shapes:
        size = int(np.prod(s))
        out.append(flat[off:off + size].reshape(s))
        off += size
    return out


def _vec(a):
    return a.reshape(-1, 1, a.shape[-1])


def _local_step(xa, tgt, mods, wts, sm, n_x):
    n_tot, d = xa.shape
    kw = dict(nrows=n_tot, n_x_rows=n_x)
    kx = dict(nrows=n_x, n_x_rows=n_x)
    mv = lambda l, k: mods[l, :, k][:, None, :]
    mx = lambda l, k: mods[l, 0:1, k][:, None, :]
    nmix, nffn = sm["norm_mix"], sm["norm_ffn"]
    cos, sin = _rope_tables(n_x, n_tot - n_x)
    cfg_a, cfg_b = _AttnCfg("na", n_x, n_tot), _AttnCfg("swa", n_x, n_tot)

    h1 = _rowwise(_f_normmod, [xa], [_vec(nmix[0:1]), mv(0, 1), mv(0, 0)], [(d, bf16)], name="l0_norm_mix", **kw)[0]
    qkv32 = _mm(h1, wts("w_in"), "nn", f32, "l0_in_proj", comm=wts)
    qkv = _qkv_post(qkv32, cos, sin, "l0_qkv_post")
    rpb2 = jnp.zeros((128, 128), f32).at[:NA_HEADS * 15, :31].set(sm["rpb"].reshape(NA_HEADS * 15, 31))
    basis = jnp.zeros((128, GRID_W * GRID_W), f32).at[:31].set(_toeplitz_basis())
    tz = _small_dot(rpb2, basis, "rpb_expand")[:NA_HEADS * 15].reshape(NA_HEADS, 15, GRID_W, GRID_W)
    bias = jnp.stack([tz[:, 3 - a:15 - a] for a in range(4)], axis=1).transpose(0, 1, 3, 2, 4).reshape(
        NA_HEADS, BLK, 3 * BLK)
    bias_t = jnp.stack([tz[:, a:a + 12][:, ::-1] for a in range(4)], axis=2).transpose(0, 1, 3, 2, 4).reshape(
        NA_HEADS, 3 * BLK, BLK)
    sink_rows = jnp.repeat(sm["sink"].reshape(NB_KV_HEADS, NB_GROUP, 1), BLK, axis=1).reshape(
        NB_KV_HEADS, NB_GROUP * BLK, 1)
    oa, lse_a = _attn_fwd(qkv, cfg_a, bias, "l0_na_fwd", rider=wts.take(CARRIER_US["na_fwd"]))
    ob, lse_b = _attn_fwd(qkv, cfg_b, sink_rows, "l0_swa_fwd", rider=wts.take(CARRIER_US["swa_fwd"]))
    o = jnp.concatenate([oa, ob], axis=1)
    y1, xb = _mm(o, wts("w_out"), "nn", f32, "l0_out_proj", comm=wts, gated=(xa, mv(0, 2), n_x))

    def ffn_fwd(xin, l, vec_of, kk, tag):
        h = _rowwise(_f_normmod, [xin], [_vec(nffn[l:l + 1]), vec_of(l, 4), vec_of(l, 3)], [(d, bf16)],
                     name=tag + "_norm_ffn", **kk)[0]
        a, b, u = _ffn_up(h, wts("w1", l), wts("w3", l), tag + "_ffn_up", comm=wts)
        fo, xo = _mm(u, wts("w2", l), "nn", f32, tag + "_ffn_w2", comm=wts, gated=(xin, vec_of(l, 5), n_x))
        return xo, (h, a, b, u, fo)

    def ffn_bwd(d_out, xin, saved, l, vec_of, kk, tag):
        h, a, b, u, fo = saved
        rows = kk["nrows"]
        dfo, dg2 = _rowwise_bwd(_f_gated_add, [xin, fo], [vec_of(l, 5)], [d_out], {1: bf16},
                                name=tag + "_res_ffn_bwd", **kk)
        wts.grad("w2", l, _mm(u[:rows], dfo, "tn", f32, tag + "_ffn_w2_dw", comm=wts))
        da, db = _ffn_down_bwd(dfo, wts("w2", l), a, b, tag + "_ffn_down_bwd", comm=wts)
        dh = _mm(da, wts("w1", l), "nt", f32, tag + "_ffn_w13_dx", a2=db, b2=wts("w3", l), comm=wts)
        wts.grad("w1", l, _mm(h[:rows], da, "tn", f32, tag + "_ffn_w1_dw", comm=wts))
        wts.grad("w3", l, _mm(h[:rows], db, "tn", f32, tag + "_ffn_w3_dw", comm=wts))
        dxin, dnw, dsc, dsh = _rowwise_bwd(
            _f_normmod, [xin], [_vec(nffn[l:l + 1]), vec_of(l, 4), vec_of(l, 3)], [dh], {0: f32},
            name=tag + "_norm_ffn_bwd", residual=d_out, **kk)
        return dxin, dnw, dsc, dsh, dg2

    xc, ffn0 = ffn_fwd(xb, 0, mv, kw, "l0")

    hs = _rowwise(_f_normmod, [xc], [_vec(nmix[1:2]), mv(1, 1), mv(1, 0)], [(d, f32)], name="l1_norm_mix", **kw)[0]
    g2n = sm["a_re"].shape[1]
    nq = d // CHUNK_CH
    ar2, ai2 = sm["a_re"].reshape(2 * g2n, SSM_STATE), sm["a_im"].reshape(2 * g2n, SSM_STATE)
    ldt2 = sm["log_dt"].reshape(2 * g2n, 1)
    bt_re = sm["b_re"].transpose(3, 0, 1, 2).reshape(SSM_GROUP, 2 * g2n, SSM_STATE)
    bt_im = sm["b_im"].transpose(3, 0, 1, 2).reshape(SSM_GROUP, 2 * g2n, SSM_STATE)
    disc_in = [ar2, ai2, ldt2, bt_re, bt_im]
    lam_r, lam_i, bbar_r, bbar_i = _whole(
        _f_discretise, disc_in,
        [((2 * g2n, SSM_STATE), f32)] * 2 + [((SSM_GROUP, 2 * g2n, SSM_STATE), f32)] * 2, "s5_discretise")
    eye = jnp.eye(GROUPS_PER_CHUNK, dtype=f32)
    eye6 = eye.astype(bf16)[None, None, :, None, :, None]

    def blockdiag_b(bbar):
        t = bbar.astype(bf16).reshape(SSM_GROUP, 2, nq, GROUPS_PER_CHUNK, SSM_STATE).transpose(1, 2, 3, 0, 4)
        return (t[:, :, :, :, None, :] * eye6).reshape(2, nq, CHUNK_CH, N_STATE)

    def blockdiag_c(cw):
        t = cw.astype(bf16).reshape(2, nq, GROUPS_PER_CHUNK, SSM_GROUP, SSM_STATE).transpose(0, 1, 2, 4, 3)
        return (t[:, :, :, :, None, :] * eye6).reshape(2, nq, N_STATE, CHUNK_CH)

    lam = jnp.stack([lam_r.reshape(2, nq, N_STATE), lam_i.reshape(2, nq, N_STATE)], axis=2)
    bblk = jnp.stack([blockdiag_b(bbar_r), blockdiag_b(bbar_i)], axis=2)
    cblk = jnp.stack([blockdiag_c(sm["c_re"]), blockdiag_c(sm["c_im"])], axis=2)
    bblk16, cblk16 = bblk, cblk
    bblk_t, cblk_t = bblk16.transpose(0, 1, 2, 4, 3), cblk16.transpose(0, 1, 2, 4, 3)
    pm, pmt = _scan_perm()
    hs_p = _row_perm(hs, pm, "l1_s5_perm", out_dtype=bf16, split=False)
    ys, cps = [], []
    for dr in range(2):
        yd, cp = _s5_fwd(hs_p, lam[dr], bblk16[dr], cblk16[dr], n_x, dr == 1, "l1_s5_fwd%d" % dr,
                         rider=wts.take(CARRIER_US["s5_fwd"]))
        ys.append(yd)
        cps.append(cp)
    dvec = sm["ssm_d"].reshape(1, d)
    gy = _glu_pre(hs, ys[0], ys[1], dvec, pmt, n_x, "l1_glu_pre")
    zv = _mm(gy, wts("w_glu_v"), "nn", f32, "l1_glu_val", comm=wts)
    zg = _mm(gy, wts("w_glu_g"), "nn", f32, "l1_glu_gate", comm=wts)
    bv, bg = sm["b_glu"][:d].reshape(1, 1, d), sm["b_glu"][d:].reshape(1, 1, d)
    xd = _rowwise(_f_glu_post, [xc, zv, zg], [mx(1, 2), bv, bg], [(d, f32)], name="l1_glu_post", **kx)[0]
    xe, ffn1 = ffn_fwd(xd, 1, mx, kx, "l1")

    d_xe, d_nfinal, loss_blk = _loss_head(xe, tgt, sm["norm_final"].reshape(1, d), nrows=n_x, name="loss_head")
    d_xd, dnffn1, dsc2_1, dsh2_1, dg2_1 = ffn_bwd(d_xe, xd, ffn1, 1, mx, kx, "l1")
    dzv, dzg, dg1_1, dbv, dbg = _rowwise_bwd(_f_glu_post, [xc, zv, zg], [mx(1, 2), bv, bg], [d_xd],
                                             {1: bf16, 2: bf16}, name="l1_glu_post_bwd", **kx)
    dgy = _mm(dzv, wts("w_glu_v"), "nt", f32, "l1_glu_dx", a2=dzg, b2=wts("w_glu_g"), comm=wts)
    dwglu_v = _mm(gy, dzv, "tn", f32, "l1_glu_val_dw", comm=wts)
    dwglu_g = _mm(gy, dzg, "tn", f32, "l1_glu_gate_dw", comm=wts)
    wts.grad("w_glu", 0, jnp.concatenate([dwglu_v, dwglu_g], axis=1))
    dy_p, du_skip, d_dvec = _glu_pre_bwd(hs, ys[0], ys[1], dvec, pm, pmt, dgy, n_x, "l1_glu_pre_bwd")
    du0, dlam0, db0, dc0 = _s5_bwd(hs_p, dy_p, cps[0], lam[0], bblk16[0], cblk16[0], bblk_t[0], cblk_t[0], n_x,
                                   False, "l1_s5_bwd0", rider=wts.take(CARRIER_US["s5_bwd"]))
    du1, dlam1, db1, dc1 = _s5_bwd(hs_p, dy_p, cps[1], lam[1], bblk16[1], cblk16[1], bblk_t[1], cblk_t[1], n_x,
                                   True, "l1_s5_bwd1", add=du0, rider=wts.take(CARRIER_US["s5_bwd"]))
    d_hs = _row_perm(du1, pmt, "l1_s5_unperm", out_dtype=f32, split=True, add=du_skip, add_rows=n_x)
    d_xc, dnmix1, dsc1_1, dsh1_1 = _rowwise_bwd(
        _f_normmod, [xc], [_vec(nmix[1:2]), mv(1, 1), mv(1, 0)], [d_hs], {0: f32},
        name="l1_norm_mix_bwd", residual=d_xd, residual_rows=n_x, **kw)
    dlam = jnp.stack([dlam0, dlam1])
    dlam_r, dlam_i = dlam[:, :, 0].reshape(2 * g2n, SSM_STATE), dlam[:, :, 1].reshape(2 * g2n, SSM_STATE)
    dbb = jnp.stack([db0, db1]).reshape(2, nq, 2, GROUPS_PER_CHUNK, SSM_GROUP, GROUPS_PER_CHUNK, SSM_STATE)
    eye7 = eye[None, None, None, :, None, :, None]
    dbbar = (dbb * eye7).sum(axis=5).transpose(2, 4, 0, 1, 3, 5).reshape(2, SSM_GROUP, 2 * g2n, SSM_STATE)
    dcc = jnp.stack([dc0, dc1]).reshape(2, nq, 2, GROUPS_PER_CHUNK, SSM_STATE, GROUPS_PER_CHUNK, SSM_GROUP)
    dcw = (dcc * eye7).sum(axis=5).transpose(2, 0, 1, 3, 5, 4).reshape(2, 2, g2n, SSM_GROUP, SSM_STATE)
    d_ar, d_ai, d_ldt, d_btr, d_bti = _whole_bwd(_f_discretise, disc_in, [dlam_r, dlam_i, dbbar[0], dbbar[1]],
                                                 "s5_discretise_bwd")
    to_b = lambda t: t.reshape(SSM_GROUP, 2, g2n, SSM_STATE).transpose(1, 2, 3, 0)
    wts.early(dict(
        loss=loss_blk[0, 0].reshape(1), a_re=d_ar.reshape(sm["a_re"].shape), a_im=d_ai.reshape(sm["a_im"].shape),
        log_dt=d_ldt.reshape(sm["log_dt"].shape), b_re=to_b(d_btr), b_im=to_b(d_bti), c_re=dcw[0], c_im=dcw[1],
        ssm_d=d_dvec.reshape(d), b_glu=jnp.concatenate([dbv.reshape(d), dbg.reshape(d)]),
        norm_final=d_nfinal.reshape(d)))

    d_xb, dnffn0, dsc2_0, dsh2_0, dg2_0 = ffn_bwd(d_xc, xb, ffn0, 0, mv, kw, "l0")
    dy1, dg1_0 = _rowwise_bwd(_f_gated_add, [xa, y1], [mv(0, 2)], [d_xb], {1: bf16}, name="l0_res_mix_bwd", **kw)
    d_o = _mm(dy1, wts("w_out"), "nt", bf16, "l0_out_proj_dx", comm=wts)
    wts.grad("w_out", 0, _mm(o, dy1, "tn", f32, "l0_out_proj_dw", comm=wts))
    dqa, delta_a, dbias = _attn_bwd_q(qkv, oa, d_o, lse_a, cfg_a, bias, "l0_na_bwd_q",
                                      rider=wts.take(CARRIER_US["na_bwd_q"]))
    dka, dva = _attn_bwd_kv(qkv, d_o, lse_a, delta_a, cfg_a, bias_t, "l0_na_bwd_kv",
                            rider=wts.take(CARRIER_US["na_bwd_kv"]))
    dqb, delta_b, dsink_rows = _attn_bwd_q(qkv, ob, d_o, lse_b, cfg_b, sink_rows, "l0_swa_bwd_q",
                                           rider=wts.take(CARRIER_US["swa_bwd_q"]))
    dkb, dvb = _attn_bwd_kv(qkv, d_o, lse_b, delta_b, cfg_b, None, "l0_swa_bwd_kv",
                            rider=wts.take(CARRIER_US["swa_bwd_kv"]))
    d_qkv = _qkv_post_bwd([dqa, dka, dva, dqb, dkb, dvb], cos, sin, "l0_qkv_post_bwd")
    wts.grad("w_in", 0, _mm(h1, d_qkv, "tn", f32, "l0_in_proj_dw", comm=wts))
    dh1 = _mm(d_qkv, wts("w_in"), "nt", f32, "l0_in_proj_dx", comm=wts)
    d_xa, dnmix0, dsc1_0, dsh1_0 = _rowwise_bwd(
        _f_normmod, [xa], [_vec(nmix[0:1]), mv(0, 1), mv(0, 0)], [dh1], {0: f32},
        name="l0_norm_mix_bwd", residual=d_xb, latent_grads_only=True, **kw)
    dbias5 = dbias.reshape(NA_HEADS, 4, GRID_W, 12, GRID_W).transpose(0, 1, 3, 2, 4)
    dtz = sum(jnp.pad(dbias5[:, a], ((0, 0), (3 - a, a), (0, 0), (0, 0))) for a in range(4))
    dtz2 = jnp.zeros((128, GRID_W * GRID_W), f32).at[:NA_HEADS * 15].set(dtz.reshape(NA_HEADS * 15, -1))
    d_rpb = _small_dot(dtz2, basis.T, "rpb_expand_bwd")[:NA_HEADS * 15, :31].reshape(sm["rpb"].shape)
    d_sink = dsink_rows.reshape(NB_KV_HEADS * NB_GROUP, BLK).sum(axis=1)

    zero_c = jnp.zeros((1, 1, d), f32)
    both = lambda gx: jnp.concatenate([gx, zero_c], axis=0)
    dmods = jnp.stack([
        jnp.stack([dsh1_0, dsc1_0, dg1_0, dsh2_0, dsc2_0, dg2_0], axis=2),
        jnp.stack([dsh1_1, dsc1_1, both(dg1_1), both(dsh2_1), both(dsc2_1), both(dg2_1)], axis=2),
    ])[:, :, 0]
    late = dict(norm_mix=jnp.concatenate([dnmix0[0], dnmix1[0]]), norm_ffn=jnp.concatenate([dnffn0[0], dnffn1[0]]),
                rpb=d_rpb, sink=d_sink)
    return d_xa, late, dmods


WEIGHTS = ("c_ctx", "ada_w", "ada_b", "norm_mix", "norm_ffn", "ffn_w1", "ffn_w3", "ffn_w2", "attn_w_in",
           "attn_w_out", "attn_rpb", "attn_sink", "ssm_a_re", "ssm_a_im", "ssm_log_dt", "ssm_b_re", "ssm_b_im",
           "ssm_c_re", "ssm_c_im", "ssm_d", "ssm_w_glu", "ssm_b_glu", "norm_final")
SHARDED_BIG = ("ffn_w1", "ffn_w3", "ffn_w2", "attn_w_in", "attn_w_out", "ssm_w_glu")
BIG = ("ada_w",) + SHARDED_BIG
SMALL = tuple(n for n in WEIGHTS if n not in BIG)


BIG_SLABS = dict(w_in=("attn_w_in", "cols"), w_out=("attn_w_out", "rows"), w1=("ffn_w1", "cols"),
                 w3=("ffn_w3", "cols"), w2=("ffn_w2", "rows"), w_glu=("ssm_w_glu", "cols"))
GATHER_ORDER = (("w_in", 0), ("w_out", 0), ("w1", 0), ("w3", 0), ("w2", 0), ("w_glu", 0), ("w1", 1), ("w3", 1),
                ("w2", 1))


class _Weights(_Comm):
    def __init__(self, w, pack_early):
        super().__init__()
        self.w, self.pack_early = w, pack_early
        self.gathers = {key: self.post(w[BIG_SLABS[key[0]][0]][key[1]].astype(bf16), True) for key in GATHER_ORDER}
        self.full, self.scatters, self.early_gather = {}, {}, None

    def __call__(self, name, layer=0):
        key = ("w_glu", 0) if name.startswith("w_glu") else (name, layer)
        if key not in self.full:
            blk = self.finish(self.gathers[key])
            if BIG_SLABS[key[0]][1] == "cols":
                self.full[key] = blk.transpose(1, 0, 2).reshape(blk.shape[1], -1)
            else:
                self.full[key] = blk.reshape(-1, blk.shape[2])
        full = self.full[key]
        if name.startswith("w_glu"):
            half = full.shape[1] // 2
            return full[:, :half] if name == "w_glu_v" else full[:, half:]
        return full

    def grad(self, name, layer, dw):
        if BIG_SLABS[name][1] == "cols":
            blk = dw.reshape(dw.shape[0], N_DEV, -1).transpose(1, 0, 2)
        else:
            blk = dw.reshape(N_DEV, -1, dw.shape[1])
        self.scatters[(name, layer)] = self.post(blk.astype(bf16), False)

    def early(self, parts):
        self.early_shapes = {k: v.shape for k, v in parts.items()}
        self.early_gather = self.post(self.pack_early(parts), True)

    def reduced(self, name):
        short = next(k for k, v in BIG_SLABS.items() if v[0] == name)
        layers = [l for (n, l) in self.scatters if n == short]
        parts = [_sum0(self.finish(self.scatters[(short, l)]), "sum_%s%d" % (short, l)) for l in sorted(layers)]
        return jnp.stack(parts).reshape(self.w[name].shape)


def kernel(x, c, ctx, c_ctx, ada_w, ada_b, norm_mix, norm_ffn, ffn_w1, ffn_w3, ffn_w2, attn_w_in, attn_w_out, attn_rpb, attn_sink, ssm_a_re, ssm_a_im, ssm_log_dt, ssm_b_re, ssm_b_im, ssm_c_re, ssm_c_im, ssm_d, ssm_w_glu, ssm_b_glu, norm_final, loss_target, m_c_ctx, m_ada_w, m_ada_b, m_norm_mix, m_norm_ffn, m_ffn_w1, m_ffn_w3, m_ffn_w2, m_attn_w_in, m_attn_w_out, m_attn_rpb, m_attn_sink, m_ssm_a_re, m_ssm_a_im, m_ssm_log_dt, m_ssm_b_re, m_ssm_b_im, m_ssm_c_re, m_ssm_c_im, m_ssm_d, m_ssm_w_glu, m_ssm_b_glu, m_norm_final, v_c_ctx, v_ada_w, v_ada_b, v_norm_mix, v_norm_ffn, v_ffn_w1, v_ffn_w3, v_ffn_w2, v_attn_w_in, v_attn_w_out, v_attn_rpb, v_attn_sink, v_ssm_a_re, v_ssm_a_im, v_ssm_log_dt, v_ssm_b_re, v_ssm_b_im, v_ssm_c_re, v_ssm_c_im, v_ssm_d, v_ssm_w_glu, v_ssm_b_glu, v_norm_final):
    p = dict(locals())
    w = {n: p[n] for n in WEIGHTS}
    me = 4 * lax.axis_index("x") + 2 * lax.axis_index("y") + lax.axis_index("c")
    n_x, d = x.shape[1], x.shape[2]
    cols = ada_w.shape[2]
    d8 = d // N_DEV

    first = jnp.concatenate([c[0], ssm_d[0], ssm_b_glu[0]])[None]
    g0 = _exchange(first, True, "gather_vectors")[:, 0]
    c_all, d_full, bglu_full = g0[:, :d], g0[:, d:d + d8].reshape(d), g0[:, d + d8:].reshape(2 * d)
    cc = jnp.concatenate([c_all, c_ctx[None], jnp.zeros((16 - N_DEV - 1, d), f32)])
    sc_all = _whole(_f_silu, [cc], [((16, d), f32)], "silu_c")[0]
    my_cols = lambda a: lax.dynamic_slice_in_dim(a, me * cols, cols, axis=a.ndim - 1)
    mod_loc = jnp.stack([_mm(sc_all, ada_w[l], "nn", f32, "ada_mod%d" % l) for l in range(2)])
    mod_loc = mod_loc + my_cols(ada_b)[:, None, :]
    mg = _exchange(mod_loc.reshape(32, cols), True, "gather_mod")
    mod_all = mg.reshape(N_DEV, 2, 16, cols).transpose(1, 2, 0, 3).reshape(2, 16, N_DEV * cols)
    mod_x = lax.dynamic_index_in_dim(mod_all, me, axis=1, keepdims=False)
    mods = jnp.stack([mod_x, mod_all[:, N_DEV]], axis=1).reshape(2, 2, 6, d)

    wts = _Weights(w, lambda parts: _pack(list(parts.values()), f32))
    sm = dict(norm_mix=norm_mix, norm_ffn=norm_ffn, rpb=attn_rpb[0], sink=attn_sink[0], a_re=ssm_a_re[0],
              a_im=ssm_a_im[0], log_dt=ssm_log_dt[0], b_re=ssm_b_re[0], b_im=ssm_b_im[0], c_re=ssm_c_re[0],
              c_im=ssm_c_im[0], ssm_d=d_full, b_glu=bglu_full, norm_final=norm_final)

    xa = jnp.concatenate([x[0], ctx[0]], axis=0)
    d_xa, late, dmods = _local_step(xa, loss_target[0], mods, wts, sm, n_x)

    late_names = ("norm_mix", "norm_ffn", "rpb", "sink")
    parts = [late[n] for n in late_names] + [dmods[:, 0].reshape(2, 6 * d), dmods[:, 1].reshape(2, 6 * d)]
    shapes = [q.shape for q in parts]
    late_gather = wts.post(_pack(parts, f32), True)
    grads = {n: wts.reduced(n) for n in SHARDED_BIG}
    sg = wts.finish(late_gather)
    tot = _unpack(_sum0(sg, "sum_small").reshape(-1), shapes)
    early = _unpack(_sum0(wts.finish(wts.early_gather), "sum_small_early").reshape(-1),
                    list(wts.early_shapes.values()))
    ts = dict(zip(late_names, tot[:len(late_names)]), **dict(zip(wts.early_shapes, early)))
    loss = ts["loss"][0]
    tot_dmod_x, tot_dmod_c = tot[-2], tot[-1]
    off_x = sum(int(np.prod(s)) for s in shapes[:-2])
    dmod_x_all = sg.reshape(N_DEV, -1)[:, off_x:off_x + 12 * d].reshape(N_DEV, 2, 6 * d)
    dm = jnp.concatenate([dmod_x_all.transpose(1, 0, 2), tot_dmod_c[:, None, :],
                          jnp.zeros((2, 16 - N_DEV - 1, 6 * d), f32)], axis=1)
    dm_loc = my_cols(dm)
    grads["ada_w"] = jnp.stack([_mm(sc_all, dm_loc[l], "tn", f32, "ada_w_grad%d" % l) for l in range(2)])
    grads["ada_b"] = tot_dmod_x + tot_dmod_c
    dsc_part = _mm(dm_loc[0], ada_w[0], "nt", f32, "silu_c_grad", a2=dm_loc[1], b2=ada_w[1])[N_DEV:N_DEV + 1]
    dsc = _sum0(_exchange(dsc_part, True, "gather_cctx"), "sum_cctx")
    grads["c_ctx"] = _whole_bwd(_f_silu, [c_ctx[None]], [dsc], "silu_cctx_bwd")[0][0]
    grads.update(norm_mix=ts["norm_mix"], norm_ffn=ts["norm_ffn"], attn_rpb=ts["rpb"][None],
                 attn_sink=ts["sink"][None], ssm_a_re=ts["a_re"][None], ssm_a_im=ts["a_im"][None],
                 ssm_log_dt=ts["log_dt"][None], ssm_b_re=ts["b_re"][None], ssm_b_im=ts["b_im"][None],
                 ssm_c_re=ts["c_re"][None], ssm_c_im=ts["c_im"][None],
                 ssm_d=lax.dynamic_slice_in_dim(ts["ssm_d"], me * d8, d8)[None],
                 ssm_b_glu=lax.dynamic_slice_in_dim(ts["b_glu"], me * 2 * d8, 2 * d8)[None],
                 norm_final=ts["norm_final"])

    delta, new_m, new_v = {}, {}, {}
    for n in BIG:
        two_d = lambda a: a.reshape(-1, a.shape[-1])
        dl, m2, v2 = _adamw(two_d(w[n]), two_d(grads[n]), two_d(p["m_" + n]), two_d(p["v_" + n]), "adamw_" + n)
        delta[n], new_m[n], new_v[n] = (t.reshape(w[n].shape) for t in (dl, m2, v2))
    sshapes = [w[n].shape for n in SMALL]
    packs = [_pack([src[n] for n in SMALL], f32) for src in
             (w, grads, {n: p["m_" + n] for n in SMALL}, {n: p["v_" + n] for n in SMALL})]
    for store, t in zip((delta, new_m, new_v), _adamw(*packs, "adamw_small")):
        store.update(zip(SMALL, _unpack(t.reshape(-1), sshapes)))

    return (loss, d_xa[None], *[grads[n] for n in WEIGHTS], *[delta[n] for n in WEIGHTS],
            *[new_m[n] for n in WEIGHTS], *[new_v[n] for n in WEIGHTS])
```

```python
import functools
import math

import numpy as np
import jax
import jax.numpy as jnp
from jax import lax
from jax.experimental import pallas as pl
from jax.experimental.pallas import tpu as pltpu

f32, bf16 = jnp.float32, jnp.bfloat16

HEAD_DIM = 128
GRID_W = 64
NA_HEADS = 8
NB_Q_HEADS = 8
NB_KV_HEADS = 2
NB_GROUP = NB_Q_HEADS // NB_KV_HEADS
NA_ROWS = 8
NA_COLS = 16
SW_RADIUS = 128
ROPE_BASE = 10000.0
SSM_GROUP = 16
SSM_STATE = 64
EPS = 1e-6
NEG_INF = -1e30
ADAM_LR, ADAM_B1, ADAM_B2, ADAM_EPS, ADAM_WD, ADAM_STEP = 0.001, 0.9, 0.999, 1e-08, 0.01, 10

N_DEV = 8
BLK = 256
SCAN_SEG = 8
GROUPS_PER_CHUNK = 16
V7X_VMEM_LIMIT = 56 * 2 ** 20
MM_VMEM_BUDGET = 36 * 2 ** 20
MXU_FLOPS_PER_US = 7e8
CARRIER_US = dict(na_fwd=400, swa_fwd=250, na_bwd_q=430, na_bwd_kv=640, swa_bwd_q=220, swa_bwd_kv=320,
                  s5_fwd=600, s5_bwd=1200)

NT = (((1,), (1,)), ((), ()))
TN = (((0,), (0,)), ((), ()))


def _params(sem, side_effects=False):
    return pltpu.CompilerParams(dimension_semantics=sem, vmem_limit_bytes=V7X_VMEM_LIMIT,
                                has_side_effects=side_effects)


def _pick(n, cands):
    for c in cands:
        if n % c == 0:
            return c
    return n


MESH = pl.DeviceIdType.MESH
MIN_CHUNK_BYTES = 512 * 1024
EXCHANGE_BYTES_PER_US = 74e3
RIDER_OVERHANG = 1.1


def _peers():
    x, y, c = lax.axis_index("x"), lax.axis_index("y"), lax.axis_index("c")
    return 4 * x + 2 * y + c, lambda o: (x ^ (o >> 2), y ^ ((o >> 1) & 1), c ^ (o & 1))


class _Exchange:
    def __init__(self, src, gather):
        self.src, self.gather, self.result = src, gather, None
        self.out_shape = (N_DEV,) + src.shape if gather else src.shape
        slab = self.out_shape[1:]
        slab_bytes = int(np.prod(slab)) * src.dtype.itemsize
        self.nch = next((k for k in (4, 2) if slab[0] % (16 * k) == 0 and slab_bytes // k >= MIN_CHUNK_BYTES), 1)
        self.step = slab[0] // self.nch
        self.cost_us = (N_DEV - 1) * slab_bytes / EXCHANGE_BYTES_PER_US

    def scratch(self):
        return [pltpu.SemaphoreType.DMA((N_DEV - 1, self.nch)), pltpu.SemaphoreType.DMA((N_DEV - 1, self.nch)),
                pltpu.SemaphoreType.DMA(())]

    def _copies(self, src_ref, out_ref, send_sems, recv_sems, local_sem):
        me, peer = _peers()
        local = pltpu.make_async_copy(src_ref if self.gather else src_ref.at[me], out_ref.at[me], local_sem)

        def copy(o, k, src_idx, dst_idx):
            rows = pl.ds(k * self.step, self.step)
            return pltpu.make_async_remote_copy(
                src_ref=(src_ref if self.gather else src_ref.at[src_idx]).at[rows],
                dst_ref=out_ref.at[dst_idx].at[rows],
                send_sem=send_sems.at[o - 1, k], recv_sem=recv_sems.at[o - 1, k],
                device_id=peer(o), device_id_type=MESH)

        pairs = [(o, k) for k in range(self.nch) for o in range(1, N_DEV)]
        sends = [copy(o, k, me ^ o, me) for o, k in pairs]
        recvs = [copy(o, k, me, me ^ o) for o, k in pairs]
        return local, sends, recvs

    def start(self, *refs):
        local, sends, _ = self._copies(*refs)
        local.start()
        for cp in sends:
            cp.start()

    def wait(self, *refs):
        local, sends, recvs = self._copies(*refs)
        for cp in recvs:
            cp.wait_recv()
        for cp in sends:
            cp.wait_send()
        local.wait()


def _run_exchange(ex, name):
    def body(src_ref, out_ref, *sems):
        ex.start(src_ref, out_ref, *sems)
        ex.wait(src_ref, out_ref, *sems)

    ex.result = pl.pallas_call(
        body, name=name, out_shape=jax.ShapeDtypeStruct(ex.out_shape, ex.src.dtype),
        in_specs=[pl.BlockSpec(memory_space=pl.ANY)], out_specs=pl.BlockSpec(memory_space=pl.ANY),
        scratch_shapes=ex.scratch(), compiler_params=pltpu.CompilerParams(has_side_effects=True),
    )(ex.src)
    return ex.result


def _call(body, args, *, name, grid, in_specs, out_specs, out_shape, scratch_shapes=(), sem, rider=None):
    if not rider:
        return pl.pallas_call(body, name=name, grid=grid, in_specs=in_specs, out_specs=out_specs,
                              out_shape=out_shape, scratch_shapes=list(scratch_shapes),
                              compiler_params=_params(sem))(*args)
    n_in, n_out, n_sc, n_ex = len(in_specs), len(out_specs), len(scratch_shapes), len(rider)

    def wrapped(*refs):
        ins, ex_src = refs[:n_in], refs[n_in:n_in + n_ex]
        outs = refs[n_in + n_ex:n_in + n_ex + n_out]
        ex_out = refs[n_in + n_ex + n_out:n_in + 2 * n_ex + n_out]
        scr = refs[n_in + 2 * n_ex + n_out:n_in + 2 * n_ex + n_out + n_sc]
        sems = refs[n_in + 2 * n_ex + n_out + n_sc:]
        ids = [pl.program_id(a) for a in range(len(grid))]
        first = functools.reduce(jnp.logical_and, [i == 0 for i in ids])
        last = functools.reduce(jnp.logical_and, [i == g - 1 for i, g in zip(ids, grid)])

        @pl.when(first)
        def _():
            for e, ex in enumerate(rider):
                ex.start(ex_src[e], ex_out[e], *sems[3 * e:3 * e + 3])

        body(*ins, *outs, *scr)

        @pl.when(last)
        def _():
            for e, ex in enumerate(rider):
                ex.wait(ex_src[e], ex_out[e], *sems[3 * e:3 * e + 3])

    anyspace = pl.BlockSpec(memory_space=pl.ANY)
    res = pl.pallas_call(
        wrapped, name=name, grid=grid, in_specs=list(in_specs) + [anyspace] * n_ex,
        out_specs=list(out_specs) + [anyspace] * n_ex,
        out_shape=list(out_shape) + [jax.ShapeDtypeStruct(ex.out_shape, ex.src.dtype) for ex in rider],
        scratch_shapes=list(scratch_shapes) + [s for ex in rider for s in ex.scratch()],
        compiler_params=_params(("arbitrary",) * len(grid), side_effects=True),
    )(*args, *[ex.src for ex in rider])
    for ex, r in zip(rider, res[n_out:]):
        ex.result = r
    return list(res[:n_out])


class _Comm:
    def __init__(self):
        self.queue, self.count = [], 0

    def post(self, src, gather):
        ex = _Exchange(src, gather)
        self.queue.append(ex)
        return ex

    def take(self, budget_us):
        rider = []
        for ex in list(self.queue):
            if budget_us > 0 and ex.cost_us <= RIDER_OVERHANG * budget_us:
                budget_us -= ex.cost_us
                rider.append(ex)
                self.queue.remove(ex)
        return rider

    def finish(self, ex):
        while ex.result is None:
            self.count += 1
            _run_exchange(self.queue.pop(0), "exchange%d" % self.count)
        return ex.result


def _mm(a, b, mode, out_dtype, name, a2=None, b2=None, comm=None, gated=None):
    if mode == "tn":
        kdim, m, n = a.shape[0], a.shape[1], b.shape[1]
    elif mode == "nt":
        m, kdim, n = a.shape[0], a.shape[1], b.shape[0]
    else:
        m, kdim, n = a.shape[0], a.shape[1], b.shape[1]
    if mode == "tn":
        tm = _pick(m, (2048, 1024, 512, 256, 128))
        tk = _pick(kdim, (1024, 768, 512, 256, 128))
    else:
        tm = _pick(m, (1024, 768, 512, 256, 128))
        tk = _pick(kdim, (2048, 1024, 768, 512, 256, 128))
    in_bytes = max(a.dtype.itemsize, b.dtype.itemsize) * (1 if a2 is None else 2)
    out_bytes = jnp.dtype(out_dtype).itemsize

    def vmem(tn_):
        return 2 * in_bytes * tk * (tm + tn_) + (4 + 2 * out_bytes + (0 if gated is None else 16)) * tm * tn_

    tn = next((t for t in (2048, 1024, 512, 256, 128) if n % t == 0 and vmem(t) <= MM_VMEM_BUDGET), n)
    nk = kdim // tk
    if mode == "tn":
        a_spec = pl.BlockSpec((tk, tm), lambda i, j, k: (k, i))
        b_spec = pl.BlockSpec((tk, tn), lambda i, j, k: (k, j))
        dims = TN
    elif mode == "nt":
        a_spec = pl.BlockSpec((tm, tk), lambda i, j, k: (i, k))
        b_spec = pl.BlockSpec((tn, tk), lambda i, j, k: (j, k))
        dims = NT
    else:
        a_spec = pl.BlockSpec((tm, tk), lambda i, j, k: (i, k))
        b_spec = pl.BlockSpec((tk, tn), lambda i, j, k: (k, j))
        dims = (((1,), (0,)), ((), ()))

    pairs = 1 if a2 is None else 2
    n_in = 2 * pairs + (0 if gated is None else 2)

    def body(*refs):
        outs, acc_ref = refs[n_in:-1], refs[-1]
        i, k = pl.program_id(0), pl.program_id(2)

        @pl.when(k == 0)
        def _():
            acc_ref[...] = jnp.zeros_like(acc_ref)

        for p in range(pairs):
            acc_ref[...] += lax.dot_general(refs[2 * p][...].astype(bf16), refs[2 * p + 1][...].astype(bf16), dims,
                                            preferred_element_type=f32)

        @pl.when(k == nk - 1)
        def _():
            acc = acc_ref[...]
            outs[0][...] = acc.astype(outs[0].dtype)
            if gated is not None:
                res_ref, gate_ref = refs[2 * pairs], refs[2 * pairs + 1]
                gate = gate_ref[0]
                if gate_ref.shape[0] == 2:
                    row = i * tm + lax.broadcasted_iota(jnp.int32, (tm, 1), 0)
                    gate = jnp.where(row >= gated[2], gate_ref[1], gate)
                outs[1][...] = res_ref[...] + gate * acc

    args = (a, b) if a2 is None else (a, b, a2, b2)
    ospec = pl.BlockSpec((tm, tn), lambda i, j, k: (i, j))
    in_specs, out_specs = [a_spec, b_spec] * pairs, [ospec]
    out_shape = [jax.ShapeDtypeStruct((m, n), out_dtype)]
    if gated is not None:
        res, gate, _ = gated
        args = args + (res, gate)
        in_specs += [ospec, pl.BlockSpec((gate.shape[0], 1, tn), lambda i, j, k: (0, 0, j))]
        out_specs.append(ospec)
        out_shape.append(jax.ShapeDtypeStruct((m, n), f32))
    rider = comm.take(2.0 * pairs * m * n * kdim / MXU_FLOPS_PER_US) if comm else None
    res = _call(
        body, args, name=name, grid=(m // tm, n // tn, nk), in_specs=in_specs, out_specs=out_specs,
        out_shape=out_shape, scratch_shapes=[pltpu.VMEM((tm, tn), f32)],
        sem=("parallel", "parallel", "arbitrary"), rider=rider)
    return res[0] if gated is None else res


def _ffn_up(h, w1, w3, name, comm=None):
    m, kdim = h.shape
    n = w1.shape[1]
    tm, tn = _pick(m, (1024, 768, 512, 256, 128)), _pick(n, (512, 256, 128))

    def body(h_ref, w1_ref, w3_ref, a_ref, b_ref, u_ref):
        hv = h_ref[...]
        a = jnp.dot(hv, w1_ref[...], preferred_element_type=f32).astype(bf16)
        b = jnp.dot(hv, w3_ref[...], preferred_element_type=f32).astype(bf16)
        a_ref[...] = a
        b_ref[...] = b
        u_ref[...] = (jax.nn.silu(a.astype(f32)) * b.astype(f32)).astype(bf16)

    wspec = pl.BlockSpec((kdim, tn), lambda i, j: (0, j))
    ospec = pl.BlockSpec((tm, tn), lambda i, j: (i, j))
    rider = comm.take(4.0 * m * n * kdim / MXU_FLOPS_PER_US) if comm else None
    return _call(
        body, (h, w1, w3), name=name, grid=(m // tm, n // tn),
        in_specs=[pl.BlockSpec((tm, kdim), lambda i, j: (i, 0)), wspec, wspec], out_specs=[ospec] * 3,
        out_shape=[jax.ShapeDtypeStruct((m, n), bf16)] * 3, sem=("parallel", "parallel"), rider=rider)


def _ffn_down_bwd(g, w2, a, b, name, comm=None):
    m, kdim = g.shape
    n = w2.shape[0]
    tm, tn = _pick(m, (1024, 768, 512, 256, 128)), _pick(n, (512, 256, 128))

    def body(g_ref, w_ref, a_ref, b_ref, da_ref, db_ref):
        du = lax.dot_general(g_ref[...], w_ref[...], NT, preferred_element_type=f32)
        _, vjp = jax.vjp(lambda p, q: jax.nn.silu(p) * q, a_ref[...].astype(f32), b_ref[...].astype(f32))
        da, db = vjp(du)
        da_ref[...] = da.astype(bf16)
        db_ref[...] = db.astype(bf16)

    ospec = pl.BlockSpec((tm, tn), lambda i, j: (i, j))
    rider = comm.take(2.0 * m * n * kdim / MXU_FLOPS_PER_US) if comm else None
    return _call(
        body, (g, w2, a, b), name=name, grid=(m // tm, n // tn),
        in_specs=[pl.BlockSpec((tm, kdim), lambda i, j: (i, 0)), pl.BlockSpec((tn, kdim), lambda i, j: (j, 0)),
                  ospec, ospec],
        out_specs=[ospec] * 2, out_shape=[jax.ShapeDtypeStruct((m, n), bf16)] * 2,
        sem=("parallel", "parallel"), rider=rider)


def _small_dot(a, b, name):
    def body(a_ref, b_ref, o_ref):
        o_ref[...] = jnp.dot(a_ref[...], b_ref[...], precision=lax.Precision.HIGHEST, preferred_element_type=f32)

    return pl.pallas_call(body, name=name, out_shape=jax.ShapeDtypeStruct((a.shape[0], b.shape[1]), f32))(a, b)


def _group_of(i, n_x_tiles, n_groups):
    return jnp.where(i >= n_x_tiles, n_groups - 1, 0)


def _rowwise(f, rows, vecs, outs, *, nrows, n_x_rows, name, tm=BLK, tc=None):
    n_x_tiles = n_x_rows // tm
    grid = (nrows // tm,) if tc is None else (nrows // tm, rows[0].shape[1] // tc)

    def rspec(cols):
        if tc is None:
            return pl.BlockSpec((tm, cols), lambda i: (i, 0))
        return pl.BlockSpec((tm, tc), lambda i, j: (i, j))

    def vspec(v):
        g = v.shape[0]
        if tc is None:
            return pl.BlockSpec((None, 1, v.shape[2]), lambda i: (_group_of(i, n_x_tiles, g), 0, 0))
        return pl.BlockSpec((None, 1, tc), lambda i, j: (_group_of(i, n_x_tiles, g), 0, j))

    nr, nv = len(rows), len(vecs)

    def body(*refs):
        ins = [r[...] for r in refs[:nr + nv]]
        res = f(*ins)
        for o_ref, val in zip(refs[nr + nv:], res):
            o_ref[...] = val.astype(o_ref.dtype)

    return pl.pallas_call(
        body, name=name, grid=grid,
        in_specs=[rspec(r.shape[1]) for r in rows] + [vspec(v) for v in vecs],
        out_specs=[rspec(c) for c, _ in outs],
        out_shape=[jax.ShapeDtypeStruct((nrows, c), d) for c, d in outs],
        compiler_params=_params(("parallel",) * len(grid)),
    )(*rows, *vecs)


def _rowwise_bwd(f, rows, vecs, cts, row_grads, *, nrows, n_x_rows, name, tm=BLK, tc=None, residual=None,
                 residual_rows=None, latent_grads_only=False):
    n_x_tiles = n_x_rows // tm
    n_tiles = nrows // tm
    out_rows = n_x_rows if latent_grads_only else nrows
    res_tiles = None if residual_rows is None else residual_rows // tm
    grid = (n_tiles,) if tc is None else (rows[0].shape[1] // tc, n_tiles)
    row_of = (lambda *g: g[0]) if tc is None else (lambda *g: g[1])

    def rspec(cols):
        if tc is None:
            return pl.BlockSpec((tm, cols), lambda i: (i, 0))
        return pl.BlockSpec((tm, tc), lambda j, i: (i, j))

    def vspec(v):
        g = v.shape[0]
        if tc is None:
            return pl.BlockSpec((None, 1, v.shape[2]), lambda i: (_group_of(i, n_x_tiles, g), 0, 0))
        return pl.BlockSpec((None, 1, tc), lambda j, i: (_group_of(i, n_x_tiles, g), 0, j))

    nr, nv, nc = len(rows), len(vecs), len(cts)
    gidx = sorted(row_grads)

    def body(*refs):
        i = row_of(*[pl.program_id(d) for d in range(len(grid))])
        row_vals = [r[...].astype(f32) for r in refs[:nr]]
        vec_vals = [jnp.broadcast_to(r[...].astype(f32), (tm, r.shape[-1])) for r in refs[nr:nr + nv]]
        ct_vals = [r[...].astype(f32) for r in refs[nr + nv:nr + nv + nc]]
        out_refs = refs[nr + nv + nc + (residual is not None):]
        res, vjp = jax.vjp(lambda *a: tuple(o.astype(f32) for o in f(*a)), *row_vals, *vec_vals)
        grads = list(vjp(tuple(ct_vals)))
        if residual is not None:
            extra = refs[nr + nv + nc][...]
            if res_tiles is not None:
                extra = jnp.where(i < res_tiles, extra, 0.0)
            grads[0] = grads[0] + extra
        for o_ref, k in zip(out_refs[:len(gidx)], gidx):
            if latent_grads_only:
                @pl.when(i < n_x_tiles)
                def _():
                    o_ref[...] = grads[k].astype(o_ref.dtype)
            else:
                o_ref[...] = grads[k].astype(o_ref.dtype)
        for o_ref, g, v in zip(out_refs[len(gidx):], grads[nr:], vecs):
            part = jnp.sum(g, axis=0, keepdims=True)
            first = (i == 0) if v.shape[0] == 1 else ((i == 0) | (i == n_x_tiles))

            @pl.when(first)
            def _():
                o_ref[...] = part

            @pl.when(jnp.logical_not(first))
            def _():
                o_ref[...] += part

    sem = ("arbitrary",) if tc is None else ("parallel", "arbitrary")
    gspec = rspec
    if latent_grads_only:
        assert tc is None
        gspec = lambda cols: pl.BlockSpec((tm, cols), lambda i: (jnp.minimum(i, n_x_tiles - 1), 0))
    res_specs, res_args = [], []
    if residual is not None:
        assert tc is None
        clamp = (lambda i: i) if res_tiles is None else (lambda i: jnp.minimum(i, res_tiles - 1))
        res_specs = [pl.BlockSpec((tm, residual.shape[1]), lambda i: (clamp(i), 0))]
        res_args = [residual]
    return pl.pallas_call(
        body, name=name, grid=grid,
        in_specs=[rspec(r.shape[1]) for r in rows] + [vspec(v) for v in vecs] + [rspec(c.shape[1]) for c in cts]
        + res_specs,
        out_specs=[gspec(rows[k].shape[1]) for k in gidx] + [vspec(v) for v in vecs],
        out_shape=[jax.ShapeDtypeStruct((out_rows, rows[k].shape[1]), row_grads[k]) for k in gidx]
        + [jax.ShapeDtypeStruct(v.shape, f32) for v in vecs],
        compiler_params=_params(sem),
    )(*rows, *vecs, *cts, *res_args)


def _rms(x, w):
    return x * lax.rsqrt(jnp.mean(x * x, axis=-1, keepdims=True) + EPS) * w


def _f_normmod(x, w, sc, sh):
    return (_rms(x.astype(f32), w) * (1.0 + sc) + sh,)


def _f_gated_add(x, y, g):
    return (x + g * y.astype(f32),)


def _f_glu_post(x, zv, zg, g, bv, bg):
    return (x + g * ((zv.astype(f32) + bv) * jax.nn.sigmoid(zg.astype(f32) + bg)),)


def _f_silu(x):
    return (jax.nn.silu(x.astype(f32)),)


def _loss_head(x, tgt, w, *, nrows, name):
    d = x.shape[1]
    tm = BLK

    def body(x_ref, t_ref, w_ref, dx_ref, dw_ref, loss_ref):
        i = pl.program_id(0)
        wb = jnp.broadcast_to(w_ref[...], (tm, d))
        y, vjp = jax.vjp(_rms, x_ref[...], wb)
        e = y - t_ref[...]
        dx, dwb = vjp(e * (1.0 / d))
        dx_ref[...] = dx
        dw = jnp.sum(dwb, axis=0, keepdims=True)
        part = jnp.full((8, 128), 0.5 / d, f32) * jnp.sum(e * e)

        @pl.when(i == 0)
        def _():
            dw_ref[...] = dw
            loss_ref[...] = part

        @pl.when(i > 0)
        def _():
            dw_ref[...] += dw
            loss_ref[...] += part

    row = pl.BlockSpec((tm, d), lambda i: (i, 0))
    return pl.pallas_call(
        body, name=name, grid=(nrows // tm,),
        in_specs=[row, row, pl.BlockSpec((1, d), lambda i: (0, 0))],
        out_specs=[row, pl.BlockSpec((1, d), lambda i: (0, 0)), pl.BlockSpec((8, 128), lambda i: (0, 0))],
        out_shape=[jax.ShapeDtypeStruct((nrows, d), f32), jax.ShapeDtypeStruct((1, d), f32),
                   jax.ShapeDtypeStruct((8, 128), f32)],
        compiler_params=_params(("arbitrary",)),
    )(x, tgt, w)


def _adamw(w, g, m, v, name):
    r, c = w.shape
    tr = _pick(r, (512, 256, 128, 64, 32, 16, 8))
    tcol = _pick(c, (1024, 512)) if c % 128 == 0 else c

    def body(w_ref, g_ref, m_ref, v_ref, d_ref, m2_ref, v2_ref):
        gg = g_ref[...]
        m2 = ADAM_B1 * m_ref[...] + (1.0 - ADAM_B1) * gg
        v2 = ADAM_B2 * v_ref[...] + (1.0 - ADAM_B2) * (gg * gg)
        m_hat = m2 / (1.0 - ADAM_B1 ** ADAM_STEP)
        v_hat = v2 / (1.0 - ADAM_B2 ** ADAM_STEP)
        d_ref[...] = -ADAM_LR * (m_hat / (jnp.sqrt(v_hat) + ADAM_EPS) + ADAM_WD * w_ref[...])
        m2_ref[...] = m2
        v2_ref[...] = v2

    spec = pl.BlockSpec((tr, tcol), lambda i, j: (i, j))
    return pl.pallas_call(
        body, name=name, grid=(r // tr, c // tcol), in_specs=[spec] * 4, out_specs=[spec] * 3,
        out_shape=[jax.ShapeDtypeStruct((r, c), f32)] * 3,
        compiler_params=_params(("parallel", "parallel")),
    )(w, g, m, v)


def _swap_quarters(x):
    lane = lax.broadcasted_iota(jnp.int32, x.shape, 1)
    return jnp.where((lane & 63) < 32, pltpu.roll(x, 96, 1), pltpu.roll(x, 32, 1))


def _rope_tables(n_x, n_ctx):
    t = np.arange(n_x)
    quarter = HEAD_DIM // 4
    inv = ROPE_BASE ** (-np.arange(quarter, dtype=np.float64) / quarter)
    ar = (t // GRID_W)[:, None] * inv[None]
    ac = (t % GRID_W)[:, None] * inv[None]
    cos = np.concatenate([np.cos(ar), np.cos(ar), np.cos(ac), np.cos(ac)], axis=1)
    sin = np.concatenate([-np.sin(ar), np.sin(ar), -np.sin(ac), np.sin(ac)], axis=1)
    cos = np.concatenate([cos, np.ones((n_ctx, HEAD_DIM))], axis=0)
    sin = np.concatenate([sin, np.zeros((n_ctx, HEAD_DIM))], axis=0)
    return jnp.asarray(cos, f32), jnp.asarray(sin, f32)


A_W = NA_HEADS * HEAD_DIM
QB0, KB0, VB0 = 3 * A_W, 3 * A_W + NB_Q_HEADS * HEAD_DIM, 3 * A_W + (NB_Q_HEADS + NB_KV_HEADS) * HEAD_DIM
IN_W = VB0 + NB_KV_HEADS * HEAD_DIM


def _qkv_post(qkv, cos, sin, name):
    n = qkv.shape[0]

    def body(x_ref, c_ref, s_ref, o_ref):
        c, s = c_ref[...], s_ref[...]
        o_ref[:, :QB0] = x_ref[:, :QB0].astype(bf16)
        for col in range(QB0, VB0, HEAD_DIM):
            x = x_ref[:, col:col + HEAD_DIM]
            o_ref[:, col:col + HEAD_DIM] = (x * c + _swap_quarters(x) * s).astype(bf16)
        o_ref[:, VB0:] = x_ref[:, VB0:].astype(bf16)

    row = lambda c: pl.BlockSpec((BLK, c), lambda i: (i, 0))
    return pl.pallas_call(
        body, name=name, grid=(n // BLK,), in_specs=[row(IN_W), row(HEAD_DIM), row(HEAD_DIM)],
        out_specs=row(IN_W), out_shape=jax.ShapeDtypeStruct((n, IN_W), bf16),
        compiler_params=_params(("parallel",)),
    )(qkv, cos, sin)


def _qkv_post_bwd(parts, cos, sin, name):
    n = parts[0].shape[0]

    def body(qa, ka, va, qb, kb, vb, c_ref, s_ref, o_ref):
        c, s = c_ref[...], s_ref[...]
        o_ref[:, 0:A_W] = qa[...]
        o_ref[:, A_W:2 * A_W] = ka[...]
        o_ref[:, 2 * A_W:QB0] = va[...]
        for src, col0, width in ((qb, QB0, KB0 - QB0), (kb, KB0, VB0 - KB0)):
            for off in range(0, width, HEAD_DIM):
                g = src[:, off:off + HEAD_DIM].astype(f32)
                o_ref[:, col0 + off:col0 + off + HEAD_DIM] = (g * c + _swap_quarters(g * s)).astype(bf16)
        o_ref[:, VB0:] = vb[...]

    row = lambda c: pl.BlockSpec((BLK, c), lambda i: (i, 0))
    return pl.pallas_call(
        body, name=name, grid=(n // BLK,),
        in_specs=[row(p.shape[1]) for p in parts] + [row(HEAD_DIM), row(HEAD_DIM)],
        out_specs=row(IN_W), out_shape=jax.ShapeDtypeStruct((n, IN_W), bf16),
        compiler_params=_params(("parallel",)),
    )(*parts, cos, sin)


def _valid(kind, qpos, kpos, n_x):
    ok = (kpos >= 0) & (kpos < n_x) & (qpos >= 0) & (qpos < n_x)
    if kind == "na":
        rows = n_x // GRID_W
        qr, qc = lax.shift_right_arithmetic(qpos, 6), qpos & (GRID_W - 1)
        kr, kc = lax.shift_right_arithmetic(kpos, 6), kpos & (GRID_W - 1)
        kr0 = jnp.clip(qr - NA_ROWS // 2, 0, rows - NA_ROWS)
        ws = jnp.clip(qc - NA_COLS // 2, 0, GRID_W - NA_COLS)
        return ok & (kr >= kr0) & (kr < kr0 + NA_ROWS) & (kc >= ws) & (kc < ws + NA_COLS)
    return ok & (jnp.abs(kpos - qpos) <= SW_RADIUS)


def _stack_heads(x, g):
    if g == 1:
        return x
    return jnp.concatenate([x[:, a * HEAD_DIM:(a + 1) * HEAD_DIM] for a in range(g)], axis=0)


def _unstack_heads(x, g):
    if g == 1:
        return x
    r = x.shape[0] // g
    return jnp.concatenate([x[a * r:(a + 1) * r] for a in range(g)], axis=1)


def _add_per_head(s, m, g):
    if g == 1:
        return s + m
    return (s.reshape((g,) + m.shape) + m[None]).reshape(s.shape)


class _AttnCfg:
    def __init__(self, kind, n_x, n_tot):
        self.kind, self.n_x, self.n_tot = kind, n_x, n_tot
        self.n_xb, self.n_blk = n_x // BLK, n_tot // BLK
        if kind == "na":
            self.g, self.nkv, self.q0, self.k0, self.v0 = 1, NA_HEADS, 0, NA_HEADS, 2 * NA_HEADS
        else:
            self.g, self.nkv = NB_GROUP, NB_KV_HEADS
            self.q0, self.k0, self.v0 = QB0 // (NB_GROUP * HEAD_DIM), KB0 // HEAD_DIM, VB0 // HEAD_DIM
        self.r = BLK * self.g
        self.qw = HEAD_DIM * self.g
        self.scale = HEAD_DIM ** -0.5
        last = self.n_xb - 1
        rows = lax.broadcasted_iota(jnp.int32, (BLK, 3 * BLK), 0)
        cols = lax.broadcasted_iota(jnp.int32, (BLK, 3 * BLK), 1)
        self.mask_q = jnp.stack([jnp.where(_valid(kind, i * BLK + rows, (i - 1) * BLK + cols, n_x), 0.0, NEG_INF)
                                 for i in (0, 1, last, self.n_xb)]).astype(f32)
        self.mask_k = jnp.stack([jnp.where(_valid(kind, (j - 1) * BLK + cols.T, j * BLK + rows.T, n_x), 0.0, NEG_INF)
                                 for j in (0, 1, 2, last - 1, last)]).astype(f32)
        self.pattern = lambda i: jnp.where(i == 0, 0, jnp.where(i < last, 1, jnp.where(i == last, 2, 3)))
        self.pattern_k = lambda j: jnp.where(j <= 1, j, jnp.where(j < last - 1, 2, jnp.where(j == last - 1, 3, 4)))


def _attn_fwd(qkv, cfg, extra, name, rider=None):
    g, r, qw, n_xb, n_x = cfg.g, cfg.r, cfg.qw, cfg.n_xb, cfg.n_x
    last = n_xb - 1

    def body(q_ref, kp, ko, kn, vp, vo, vn, kc_ref, vc_ref, ex_ref, m_ref, o_ref, lse_ref):
        q = _stack_heads(q_ref[...], g)
        kw = jnp.concatenate([kp[...], ko[...], kn[...]], axis=0)
        vw = jnp.concatenate([vp[...], vo[...], vn[...]], axis=0)
        s = lax.dot_general(q, kw, NT, preferred_element_type=f32) * cfg.scale
        if cfg.kind == "na":
            s = s + ex_ref[...]
        s = _add_per_head(s, m_ref[...], g)
        sc = lax.dot_general(q, kc_ref[...], NT, preferred_element_type=f32) * cfg.scale
        m = jnp.maximum(jnp.max(s, axis=-1, keepdims=True), jnp.max(sc, axis=-1, keepdims=True))
        if cfg.kind == "swa":
            m = jnp.maximum(m, ex_ref[...])
        p, pc = jnp.exp(s - m), jnp.exp(sc - m)
        l = jnp.sum(p, axis=-1, keepdims=True) + jnp.sum(pc, axis=-1, keepdims=True)
        if cfg.kind == "swa":
            l = l + jnp.exp(ex_ref[...] - m)
        o = jnp.dot(p.astype(bf16), vw, preferred_element_type=f32) + jnp.dot(pc.astype(bf16), vc_ref[...],
                                                                             preferred_element_type=f32)
        o_ref[...] = _unstack_heads(o / l, g).astype(bf16)
        lse_ref[...] = m + jnp.log(l)

    kv = lambda col0, f: pl.BlockSpec((BLK, HEAD_DIM), lambda h, i: (f(i), col0 + h))
    prev = lambda i: jnp.clip(i - 1, 0, last)
    own = lambda i: jnp.minimum(i, last)
    nxt = lambda i: jnp.minimum(i + 1, last)
    ctx = lambda i: n_xb
    if cfg.kind == "na":
        ex_spec = pl.BlockSpec((None, BLK, 3 * BLK), lambda h, i: (h, 0, 0))
    else:
        ex_spec = pl.BlockSpec((None, r, 1), lambda h, i: (h, 0, 0))
    m_spec = pl.BlockSpec((None, BLK, 3 * BLK), lambda h, i: (cfg.pattern(i), 0, 0))
    return _call(
        body, (qkv,) * 9 + (extra, cfg.mask_q), name=name, grid=(cfg.nkv, cfg.n_blk),
        in_specs=[pl.BlockSpec((BLK, qw), lambda h, i: (i, cfg.q0 + h)),
                  kv(cfg.k0, prev), kv(cfg.k0, own), kv(cfg.k0, nxt),
                  kv(cfg.v0, prev), kv(cfg.v0, own), kv(cfg.v0, nxt),
                  kv(cfg.k0, ctx), kv(cfg.v0, ctx), ex_spec, m_spec],
        out_specs=[pl.BlockSpec((BLK, qw), lambda h, i: (i, h)),
                   pl.BlockSpec((None, None, r, 1), lambda h, i: (h, i, 0, 0))],
        out_shape=[jax.ShapeDtypeStruct((cfg.n_tot, cfg.nkv * qw), bf16),
                   jax.ShapeDtypeStruct((cfg.nkv, cfg.n_blk, r, 1), f32)],
        sem=("parallel", "parallel"), rider=rider)


def _attn_bwd_q(qkv, o, do, lse, cfg, extra, name, rider=None):
    g, r, qw, n_xb, n_x = cfg.g, cfg.r, cfg.qw, cfg.n_xb, cfg.n_x
    last = n_xb - 1
    do_col0 = 0 if cfg.kind == "na" else (NA_HEADS * HEAD_DIM) // qw

    def body(q_ref, kp, ko, kn, vp, vo, vn, kc_ref, vc_ref, ex_ref, o_ref, do_ref, lse_ref, m_ref,
             dq_ref, delta_ref, dex_ref):
        i = pl.program_id(1)
        q = _stack_heads(q_ref[...], g)
        dout = _stack_heads(do_ref[...], g)
        out = _stack_heads(o_ref[...], g)
        delta = jnp.sum(dout.astype(f32) * out.astype(f32), axis=-1, keepdims=True)
        delta_ref[...] = delta
        kw = jnp.concatenate([kp[...], ko[...], kn[...]], axis=0)
        vw = jnp.concatenate([vp[...], vo[...], vn[...]], axis=0)
        s = lax.dot_general(q, kw, NT, preferred_element_type=f32) * cfg.scale
        if cfg.kind == "na":
            s = s + ex_ref[...]
        s = _add_per_head(s, m_ref[...], g)
        sc = lax.dot_general(q, kc_ref[...], NT, preferred_element_type=f32) * cfg.scale
        lse_v = lse_ref[...]
        p, pc = jnp.exp(s - lse_v), jnp.exp(sc - lse_v)
        dp = lax.dot_general(dout, vw, NT, preferred_element_type=f32)
        dpc = lax.dot_general(dout, vc_ref[...], NT, preferred_element_type=f32)
        ds, dsc = p * (dp - delta), pc * (dpc - delta)
        dq = jnp.dot(ds.astype(bf16), kw, preferred_element_type=f32) + jnp.dot(dsc.astype(bf16), kc_ref[...],
                                                                             preferred_element_type=f32)
        dq_ref[...] = _unstack_heads(dq * cfg.scale, g).astype(bf16)
        dex = ds if cfg.kind == "na" else -jnp.exp(ex_ref[...] - lse_v) * delta

        @pl.when(i == 0)
        def _():
            dex_ref[...] = dex

        @pl.when(i > 0)
        def _():
            dex_ref[...] += dex

    kv = lambda col0, f: pl.BlockSpec((BLK, HEAD_DIM), lambda h, i: (f(i), col0 + h))
    prev = lambda i: jnp.clip(i - 1, 0, last)
    own = lambda i: jnp.minimum(i, last)
    nxt = lambda i: jnp.minimum(i + 1, last)
    ctx = lambda i: n_xb
    if cfg.kind == "na":
        ex_spec = pl.BlockSpec((None, BLK, 3 * BLK), lambda h, i: (h, 0, 0))
    else:
        ex_spec = pl.BlockSpec((None, r, 1), lambda h, i: (h, 0, 0))
    stat = pl.BlockSpec((None, None, r, 1), lambda h, i: (h, i, 0, 0))
    m_spec = pl.BlockSpec((None, BLK, 3 * BLK), lambda h, i: (cfg.pattern(i), 0, 0))
    return _call(
        body, (qkv,) * 9 + (extra, o, do, lse, cfg.mask_q), name=name, grid=(cfg.nkv, cfg.n_blk),
        in_specs=[pl.BlockSpec((BLK, qw), lambda h, i: (i, cfg.q0 + h)),
                  kv(cfg.k0, prev), kv(cfg.k0, own), kv(cfg.k0, nxt),
                  kv(cfg.v0, prev), kv(cfg.v0, own), kv(cfg.v0, nxt),
                  kv(cfg.k0, ctx), kv(cfg.v0, ctx), ex_spec,
                  pl.BlockSpec((BLK, qw), lambda h, i: (i, h)),
                  pl.BlockSpec((BLK, qw), lambda h, i: (i, do_col0 + h)), stat, m_spec],
        out_specs=[pl.BlockSpec((BLK, qw), lambda h, i: (i, h)), stat, ex_spec],
        out_shape=[jax.ShapeDtypeStruct((cfg.n_tot, cfg.nkv * qw), bf16),
                   jax.ShapeDtypeStruct((cfg.nkv, cfg.n_blk, r, 1), f32),
                   jax.ShapeDtypeStruct(extra.shape, f32)],
        sem=("parallel", "arbitrary"), rider=rider)


def _attn_bwd_kv(qkv, do, lse, delta, cfg, bias_t, name, rider=None):
    g, r, qw, n_xb, n_x, n_blk = cfg.g, cfg.r, cfg.qw, cfg.n_xb, cfg.n_x, cfg.n_blk
    last = n_xb - 1
    do_col0 = 0 if cfg.kind == "na" else (NA_HEADS * HEAD_DIM) // qw
    has_bias = cfg.kind == "na"

    def body(*refs):
        qs, dos, lses, dels = refs[0:3], refs[3:6], refs[6:9], refs[9:12]
        qj_ref, doj_ref, lsej_ref, delj_ref, k_ref, v_ref, kc_ref, vc_ref, m_ref = refs[12:21]
        rest = refs[21:]
        if has_bias:
            b_ref, rest = rest[0], rest[1:]
        dk_ref, dv_ref, dkc_acc, dvc_acc = rest
        j = pl.program_id(1)

        @pl.when(j == 0)
        def _():
            dkc_acc[...] = jnp.zeros_like(dkc_acc)
            dvc_acc[...] = jnp.zeros_like(dvc_acc)

        qj, doj = _stack_heads(qj_ref[...], g), _stack_heads(doj_ref[...], g)
        sc = lax.dot_general(qj, kc_ref[...], NT, preferred_element_type=f32) * cfg.scale
        pc = jnp.exp(sc - lsej_ref[...])
        dvc_acc[...] += lax.dot_general(pc.astype(bf16), doj, TN, preferred_element_type=f32)
        dpc = lax.dot_general(doj, vc_ref[...], NT, preferred_element_type=f32)
        dsc = pc * (dpc - delj_ref[...])
        dkc_acc[...] += lax.dot_general(dsc.astype(bf16), qj, TN, preferred_element_type=f32) * cfg.scale

        @pl.when(j < n_xb)
        def _():
            qw_all = jnp.concatenate([_stack_heads(x[...], g) for x in qs], axis=0)
            do_all = jnp.concatenate([_stack_heads(x[...], g) for x in dos], axis=0)
            lse_all = jnp.concatenate([x[...] for x in lses], axis=0)
            del_all = jnp.concatenate([x[...] for x in dels], axis=0)
            s = lax.dot_general(qw_all, k_ref[...], NT, preferred_element_type=f32) * cfg.scale
            if has_bias:
                s = s + b_ref[...]
            if g == 1:
                s = s + m_ref[...]
            else:
                s = (s.reshape(3, g, BLK, BLK) + m_ref[...].reshape(3, 1, BLK, BLK)).reshape(3 * r, BLK)
            p = jnp.exp(s - lse_all)
            dv_ref[...] = lax.dot_general(p.astype(bf16), do_all, TN, preferred_element_type=f32).astype(bf16)
            dp = lax.dot_general(do_all, v_ref[...], NT, preferred_element_type=f32)
            ds = p * (dp - del_all)
            dk_ref[...] = (lax.dot_general(ds.astype(bf16), qw_all, TN, preferred_element_type=f32)
                           * cfg.scale).astype(bf16)

        @pl.when(j == n_xb)
        def _():
            dk_ref[...] = dkc_acc[...].astype(bf16)
            dv_ref[...] = dvc_acc[...].astype(bf16)

    prev = lambda j: jnp.clip(j - 1, 0, last)
    own = lambda j: jnp.minimum(j, last)
    nxt = lambda j: jnp.minimum(j + 1, last)
    same = lambda j: j
    qspec = lambda f: pl.BlockSpec((BLK, qw), lambda h, j: (f(j), cfg.q0 + h))
    dospec = lambda f: pl.BlockSpec((BLK, qw), lambda h, j: (f(j), do_col0 + h))
    stat = lambda f: pl.BlockSpec((None, None, r, 1), lambda h, j: (h, f(j), 0, 0))
    kv = lambda col0, f: pl.BlockSpec((BLK, HEAD_DIM), lambda h, j: (f(j), col0 + h))
    ctx = lambda j: n_xb
    in_specs = ([qspec(f) for f in (prev, own, nxt)] + [dospec(f) for f in (prev, own, nxt)]
                + [stat(f) for f in (prev, own, nxt)] * 2
                + [qspec(same), dospec(same), stat(same), stat(same),
                   kv(cfg.k0, same), kv(cfg.v0, same), kv(cfg.k0, ctx), kv(cfg.v0, ctx),
                   pl.BlockSpec((None, 3 * BLK, BLK), lambda h, j: (cfg.pattern_k(j), 0, 0))])
    args = [qkv] * 3 + [do] * 3 + [lse] * 3 + [delta] * 3 + [qkv, do, lse, delta, qkv, qkv, qkv, qkv, cfg.mask_k]
    if has_bias:
        in_specs.append(pl.BlockSpec((None, 3 * BLK, BLK), lambda h, j: (h, 0, 0)))
        args.append(bias_t)
    out = pl.BlockSpec((BLK, HEAD_DIM), lambda h, j: (j, h))
    return _call(
        body, args, name=name, grid=(cfg.nkv, n_blk), in_specs=in_specs, out_specs=[out, out],
        out_shape=[jax.ShapeDtypeStruct((cfg.n_tot, cfg.nkv * HEAD_DIM), bf16)] * 2,
        scratch_shapes=[pltpu.VMEM((BLK, HEAD_DIM), f32)] * 2, sem=("parallel", "arbitrary"), rider=rider)


def _toeplitz_basis():
    qc, kc = np.meshgrid(np.arange(GRID_W), np.arange(GRID_W), indexing="ij")
    e = (kc - qc + NA_COLS - 1)[None] == np.arange(2 * NA_COLS - 1)[:, None, None]
    return e.reshape(2 * NA_COLS - 1, GRID_W * GRID_W).astype(np.float32)


def _whole(f, ins, outs, name):
    def body(*refs):
        res = f(*[r[...] for r in refs[:len(ins)]])
        for o_ref, val in zip(refs[len(ins):], res):
            o_ref[...] = val.astype(o_ref.dtype)

    return pl.pallas_call(body, name=name,
                          out_shape=[jax.ShapeDtypeStruct(s, d) for s, d in outs])(*ins)


def _whole_bwd(f, ins, cts, name):
    n = len(ins)

    def body(*refs):
        _, vjp = jax.vjp(f, *[r[...] for r in refs[:n]])
        grads = vjp(tuple(r[...] for r in refs[n:n + len(cts)]))
        for o_ref, g in zip(refs[n + len(cts):], grads):
            o_ref[...] = g

    return pl.pallas_call(body, name=name,
                          out_shape=[jax.ShapeDtypeStruct(a.shape, f32) for a in ins])(*ins, *cts)


def _f_discretise(ar, ai, ldt, br, bi):
    dt = jnp.exp(ldt)
    mag = jnp.exp(ar * dt)
    lam_r, lam_i = mag * jnp.cos(ai * dt), mag * jnp.sin(ai * dt)
    den = ar * ar + ai * ai
    nr = lam_r - 1.0
    coef_r = (nr * ar + lam_i * ai) / den
    coef_i = (lam_i * ar - nr * ai) / den
    return (lam_r, lam_i, coef_r[None] * br - coef_i[None] * bi, coef_r[None] * bi + coef_i[None] * br)


SEG_LEN = BLK // SCAN_SEG
N_STATE = GROUPS_PER_CHUNK * SSM_STATE
CHUNK_CH = GROUPS_PER_CHUNK * SSM_GROUP


def _scan_perm():
    r = np.arange(BLK)
    t = (r % SCAN_SEG) * SEG_LEN + r // SCAN_SEG
    pm = np.zeros((BLK, BLK), np.float32)
    pm[r, t] = 1.0
    return jnp.asarray(pm, bf16), jnp.asarray(pm.T, bf16)


def _row_perm(x, pm, name, *, out_dtype, split, add=None, add_rows=None):
    n, c = x.shape
    add_tiles = None if add is None else add_rows // BLK

    def body(*refs):
        x_ref, pm_ref = refs[0], refs[1]
        o_ref = refs[-1]
        xv = x_ref[...]
        if split:
            out = _unpermute(pm_ref[...], xv.astype(f32))
        else:
            out = jnp.dot(pm_ref[...], xv.astype(bf16), preferred_element_type=f32)
        if add is not None:
            out = out + jnp.where(pl.program_id(0) < add_tiles, refs[2][...], 0.0)
        o_ref[...] = out.astype(o_ref.dtype)

    row = pl.BlockSpec((BLK, c), lambda i: (i, 0))
    in_specs, args = [row, pl.BlockSpec((BLK, BLK), lambda i: (0, 0))], [x, pm]
    if add is not None:
        in_specs.append(pl.BlockSpec((BLK, c), lambda i: (jnp.minimum(i, add_tiles - 1), 0)))
        args.append(add)
    return pl.pallas_call(
        body, name=name, grid=(n // BLK,), in_specs=in_specs, out_specs=row,
        out_shape=jax.ShapeDtypeStruct((n, c), out_dtype), compiler_params=_params(("parallel",)),
    )(*args)


def _glu_pre(u, yf, yr, dvec, pmt, n_x, name):
    d = u.shape[1]

    def body(u_ref, yf_ref, yr_ref, d_ref, pmt_ref, o_ref):
        y = d_ref[...] * u_ref[...] + _unpermute(pmt_ref[...], yf_ref[...] + yr_ref[...])
        o_ref[...] = jax.nn.gelu(y).astype(bf16)

    row = pl.BlockSpec((BLK, d), lambda i: (i, 0))
    return pl.pallas_call(
        body, name=name, grid=(n_x // BLK,),
        in_specs=[row, row, row, pl.BlockSpec((1, d), lambda i: (0, 0)), pl.BlockSpec((BLK, BLK), lambda i: (0, 0))],
        out_specs=row, out_shape=jax.ShapeDtypeStruct((n_x, d), bf16), compiler_params=_params(("parallel",)),
    )(u, yf, yr, dvec, pmt)


def _glu_pre_bwd(u, yf, yr, dvec, pm, pmt, dgy, n_x, name):
    d = u.shape[1]

    def body(u_ref, yf_ref, yr_ref, d_ref, pm_ref, pmt_ref, ct_ref, dyp_ref, dud_ref, dd_ref):
        uv = u_ref[...]
        y = d_ref[...] * uv + _unpermute(pmt_ref[...], yf_ref[...] + yr_ref[...])
        _, vjp = jax.vjp(jax.nn.gelu, y)
        dy = vjp(ct_ref[...])[0]
        dyp_ref[...] = jnp.dot(pm_ref[...], dy.astype(bf16), preferred_element_type=f32).astype(bf16)
        dud_ref[...] = d_ref[...] * dy
        part = jnp.sum(dy * uv, axis=0, keepdims=True)

        @pl.when(pl.program_id(0) == 0)
        def _():
            dd_ref[...] = part

        @pl.when(pl.program_id(0) > 0)
        def _():
            dd_ref[...] += part

    row = pl.BlockSpec((BLK, d), lambda i: (i, 0))
    vec = pl.BlockSpec((1, d), lambda i: (0, 0))
    sq = pl.BlockSpec((BLK, BLK), lambda i: (0, 0))
    return pl.pallas_call(
        body, name=name, grid=(n_x // BLK,), in_specs=[row, row, row, vec, sq, sq, row],
        out_specs=[row, row, vec],
        out_shape=[jax.ShapeDtypeStruct((n_x, d), bf16), jax.ShapeDtypeStruct((n_x, d), f32),
                   jax.ShapeDtypeStruct((1, d), f32)],
        compiler_params=_params(("arbitrary",)),
    )(u, yf, yr, dvec, pm, pmt, dgy)


def _block_order(n_xb, n_blk, reverse):
    n_cb = n_blk - n_xb
    if reverse:
        return lambda cc: jnp.where(cc < n_cb, n_blk - 1 - cc, n_xb - 1 - (cc - n_cb))
    return lambda cc: jnp.where(cc < n_cb, n_xb + cc, cc - n_cb)


def _unpermute(pmt, y):
    hi = y.astype(bf16)
    lo = (y - hi.astype(f32)).astype(bf16)
    return jnp.dot(pmt, hi, preferred_element_type=f32) + jnp.dot(pmt, lo, preferred_element_type=f32)


def _lam_pow(lr, li):
    for _ in range(int(math.log2(SEG_LEN))):
        lr, li = lr * lr - li * li, 2.0 * lr * li
    return lr, li


def _s5_fwd(up, lam, bblk, cblk, n_x, reverse, name, rider=None):
    n_tot, d = up.shape
    nq, n_blk, n_xb = d // CHUNK_CH, n_tot // BLK, n_x // BLK
    order = _block_order(n_xb, n_blk, reverse)
    ns = N_STATE
    seg_order = range(SCAN_SEG - 1, -1, -1) if reverse else range(SCAN_SEG)

    def body(u_ref, lam_ref, b_ref, c_ref, y_ref, cp_ref, bu_ref, st_ref, carry_ref):
        cc = pl.program_id(1)

        @pl.when(cc == 0)
        def _():
            carry_ref[...] = jnp.zeros_like(carry_ref)

        up_v = u_ref[...]
        bu_ref[0] = jnp.dot(up_v, b_ref[0], preferred_element_type=f32)
        bu_ref[1] = jnp.dot(up_v, b_ref[1], preferred_element_type=f32)
        lr, li = lam_ref[0:1, :], lam_ref[1:2, :]
        lrb, lib = jnp.broadcast_to(lr, (SCAN_SEG, ns)), jnp.broadcast_to(li, (SCAN_SEG, ns))

        def step(p, s, store):
            sr, si = s
            j = SEG_LEN - 1 - p if reverse else p
            off = pl.multiple_of(j * SCAN_SEG, SCAN_SEG)
            nsr = lrb * sr - lib * si + bu_ref[0, pl.ds(off, SCAN_SEG), :]
            nsi = lrb * si + lib * sr + bu_ref[1, pl.ds(off, SCAN_SEG), :]
            if store:
                st_ref[0, pl.ds(off, SCAN_SEG), :] = nsr
                st_ref[1, pl.ds(off, SCAN_SEG), :] = nsi
            return nsr, nsi

        zero = jnp.zeros((SCAN_SEG, ns), f32)
        er, ei = lax.fori_loop(0, SEG_LEN, lambda j, s: step(j, s, False), (zero, zero))
        pr, pi = _lam_pow(lr, li)
        cr, ci = carry_ref[0, 0:1, :], carry_ref[1, 0:1, :]
        rows_r, rows_i = [None] * SCAN_SEG, [None] * SCAN_SEG
        for k in seg_order:
            rows_r[k], rows_i[k] = cr, ci
            cr, ci = pr * cr - pi * ci + er[k:k + 1], pr * ci + pi * cr + ei[k:k + 1]
        carry_ref[0] = jnp.broadcast_to(cr, (SCAN_SEG, ns))
        carry_ref[1] = jnp.broadcast_to(ci, (SCAN_SEG, ns))
        cpr, cpi = jnp.concatenate(rows_r, axis=0), jnp.concatenate(rows_i, axis=0)
        cp_ref[0] = cpr
        cp_ref[1] = cpi
        lax.fori_loop(0, SEG_LEN, lambda j, s: step(j, s, True), (cpr, cpi))
        y_ref[...] = (jnp.dot(st_ref[0].astype(bf16), c_ref[0], preferred_element_type=f32)
                      - jnp.dot(st_ref[1].astype(bf16), c_ref[1], preferred_element_type=f32))

    return _call(
        body, (up, lam, bblk, cblk), name=name, grid=(nq, n_blk),
        in_specs=[pl.BlockSpec((BLK, CHUNK_CH), lambda q,cc: (order(cc), q)),
                  pl.BlockSpec((None, 2, ns), lambda q, cc: (q, 0, 0)),
                  pl.BlockSpec((None, 2, CHUNK_CH, ns), lambda q,cc: (q, 0, 0, 0)),
                  pl.BlockSpec((None, 2, ns, CHUNK_CH), lambda q,cc: (q, 0, 0, 0))],
        out_specs=[pl.BlockSpec((BLK, CHUNK_CH), lambda q,cc: (order(cc), q)),
                   pl.BlockSpec((None, None, 2, SCAN_SEG, ns), lambda q, cc: (q, cc, 0, 0, 0))],
        out_shape=[jax.ShapeDtypeStruct((n_tot, d), f32),
                   jax.ShapeDtypeStruct((nq, n_blk, 2, SCAN_SEG, ns), f32)],
        scratch_shapes=[pltpu.VMEM((2, BLK, ns), f32), pltpu.VMEM((2, BLK, ns), f32),
                        pltpu.VMEM((2, SCAN_SEG, ns), f32)],
        sem=("parallel", "arbitrary"), rider=rider)


def _s5_bwd(u, dy, cprev, lam, bblk, cblk, bblk_t, cblk_t, n_x, reverse, name, add=None, rider=None):
    n_tot, d = u.shape
    nq, n_blk, n_xb = d // CHUNK_CH, n_tot // BLK, n_x // BLK
    order = _block_order(n_xb, n_blk, reverse)
    ns = N_STATE
    blk_of = lambda step: order(n_blk - 1 - step)
    has_add = add is not None
    jof = (lambda p: SEG_LEN - 1 - p) if reverse else (lambda p: p)
    adj_seg_order = range(SCAN_SEG) if reverse else range(SCAN_SEG - 1, -1, -1)

    def body(*refs):
        u_ref, dy_ref, cp_ref, lam_ref, b_ref, c_ref, bt_ref, ct_ref = refs[:8]
        rest = refs[8:]
        if has_add:
            add_ref, rest = rest[0], rest[1:]
        du_ref, dlam_ref, db_ref, dc_ref, bu_ref, st_ref, ds_ref, acarry_ref = rest
        step_id = pl.program_id(1)
        is_x = blk_of(step_id) < n_xb

        @pl.when(step_id == 0)
        def _():
            acarry_ref[...] = jnp.zeros_like(acarry_ref)
            dlam_ref[...] = jnp.zeros_like(dlam_ref)
            db_ref[...] = jnp.zeros_like(db_ref)
            dc_ref[...] = jnp.zeros_like(dc_ref)

        up = u_ref[...]
        bu_ref[0] = jnp.dot(up, b_ref[0], preferred_element_type=f32)
        bu_ref[1] = jnp.dot(up, b_ref[1], preferred_element_type=f32)
        lr, li = lam_ref[0:1, :], lam_ref[1:2, :]
        lrb, lib = jnp.broadcast_to(lr, (SCAN_SEG, ns)), jnp.broadcast_to(li, (SCAN_SEG, ns))
        cpr, cpi = cp_ref[0], cp_ref[1]

        def fstep(p, s):
            sr, si = s
            off = pl.multiple_of(jof(p) * SCAN_SEG, SCAN_SEG)
            nsr = lrb * sr - lib * si + bu_ref[0, pl.ds(off, SCAN_SEG), :]
            nsi = lrb * si + lib * sr + bu_ref[1, pl.ds(off, SCAN_SEG), :]
            st_ref[0, pl.ds(off, SCAN_SEG), :] = nsr
            st_ref[1, pl.ds(off, SCAN_SEG), :] = nsi
            return nsr, nsi

        lax.fori_loop(0, SEG_LEN, fstep, (cpr, cpi))

        dyp = jnp.where(is_x, dy_ref[...], jnp.zeros_like(dy_ref))
        ds_ref[0] = jnp.dot(dyp, ct_ref[0], preferred_element_type=f32)
        ds_ref[1] = -jnp.dot(dyp, ct_ref[1], preferred_element_type=f32)

        def adj(p, a):
            ar, ai = a
            off = pl.multiple_of(jof(p) * SCAN_SEG, SCAN_SEG)
            nar = ds_ref[0, pl.ds(off, SCAN_SEG), :] + lrb * ar + lib * ai
            nai = ds_ref[1, pl.ds(off, SCAN_SEG), :] - lib * ar + lrb * ai
            return nar, nai

        zero = jnp.zeros((SCAN_SEG, ns), f32)
        er, ei = lax.fori_loop(0, SEG_LEN, lambda jj, a: adj(SEG_LEN - 1 - jj, a), (zero, zero))
        pr, pi = _lam_pow(lr, li)
        nr_, ni_ = acarry_ref[0, 0:1, :], acarry_ref[1, 0:1, :]
        rows_r, rows_i = [None] * SCAN_SEG, [None] * SCAN_SEG
        for k in adj_seg_order:
            rows_r[k], rows_i[k] = nr_, ni_
            nr_, ni_ = er[k:k + 1] + pr * nr_ + pi * ni_, ei[k:k + 1] + pr * ni_ - pi * nr_
        acarry_ref[0] = jnp.broadcast_to(nr_, (SCAN_SEG, ns))
        acarry_ref[1] = jnp.broadcast_to(ni_, (SCAN_SEG, ns))
        an_r, an_i = jnp.concatenate(rows_r, axis=0), jnp.concatenate(rows_i, axis=0)

        def adj2(jj, carry):
            ar, ai, glr, gli = carry
            p = SEG_LEN - 1 - jj
            nar, nai = adj(p, (ar, ai))
            off = pl.multiple_of(jof(p) * SCAN_SEG, SCAN_SEG)
            ds_ref[0, pl.ds(off, SCAN_SEG), :] = nar
            ds_ref[1, pl.ds(off, SCAN_SEG), :] = nai
            poff = pl.multiple_of(jof(p - 1) * SCAN_SEG, SCAN_SEG)
            spr, spi = st_ref[0, pl.ds(poff, SCAN_SEG), :], st_ref[1, pl.ds(poff, SCAN_SEG), :]
            return nar, nai, glr + nar * spr + nai * spi, gli - nar * spi + nai * spr

        ar, ai, glr, gli = lax.fori_loop(0, SEG_LEN - 1, adj2, (an_r, an_i, zero, zero))
        nar, nai = adj(0, (ar, ai))
        first = jof(0) * SCAN_SEG
        ds_ref[0, first:first + SCAN_SEG, :] = nar
        ds_ref[1, first:first + SCAN_SEG, :] = nai
        glr = glr + nar * cpr + nai * cpi
        gli = gli - nar * cpi + nai * cpr
        dlam_ref[0:1, :] += jnp.sum(glr, axis=0, keepdims=True)
        dlam_ref[1:2, :] += jnp.sum(gli, axis=0, keepdims=True)

        a_r, a_i = ds_ref[0].astype(bf16), ds_ref[1].astype(bf16)
        du = jnp.dot(a_r, bt_ref[0], preferred_element_type=f32) + jnp.dot(a_i, bt_ref[1],
                                                                        preferred_element_type=f32)
        if has_add:
            du = du + add_ref[...]
        du_ref[...] = du
        db_ref[0] += lax.dot_general(up, a_r, TN, preferred_element_type=f32)
        db_ref[1] += lax.dot_general(up, a_i, TN, preferred_element_type=f32)
        dc_ref[0] += lax.dot_general(st_ref[0].astype(bf16), dyp, TN, preferred_element_type=f32)
        dc_ref[1] -= lax.dot_general(st_ref[1].astype(bf16), dyp, TN, preferred_element_type=f32)

    tok = pl.BlockSpec((BLK, CHUNK_CH), lambda q,s: (blk_of(s), q))
    in_specs = [tok, pl.BlockSpec((BLK, CHUNK_CH), lambda q,s: (jnp.minimum(blk_of(s), n_xb - 1), q)),
                pl.BlockSpec((None, None, 2, SCAN_SEG, ns), lambda q, s: (q, n_blk - 1 - s, 0, 0, 0)),
                pl.BlockSpec((None, 2, ns), lambda q, s: (q, 0, 0)),
                pl.BlockSpec((None, 2, CHUNK_CH, ns), lambda q,s: (q, 0, 0, 0)),
                pl.BlockSpec((None, 2, ns, CHUNK_CH), lambda q,s: (q, 0, 0, 0)),
                pl.BlockSpec((None, 2, ns, CHUNK_CH), lambda q,s: (q, 0, 0, 0)),
                pl.BlockSpec((None, 2, CHUNK_CH, ns), lambda q,s: (q, 0, 0, 0))]
    args = [u, dy, cprev, lam, bblk, cblk, bblk_t, cblk_t]
    if has_add:
        in_specs.append(tok)
        args.append(add)
    return _call(
        body, args, name=name, grid=(nq, n_blk), in_specs=in_specs,
        out_specs=[tok, pl.BlockSpec((None, 2, ns), lambda q, s: (q, 0, 0)),
                   pl.BlockSpec((None, 2, CHUNK_CH, ns), lambda q,s: (q, 0, 0, 0)),
                   pl.BlockSpec((None, 2, ns, CHUNK_CH), lambda q,s: (q, 0, 0, 0))],
        out_shape=[jax.ShapeDtypeStruct((n_tot, d), f32), jax.ShapeDtypeStruct((nq, 2, ns), f32),
                   jax.ShapeDtypeStruct((nq, 2, CHUNK_CH, ns), f32),
                   jax.ShapeDtypeStruct((nq, 2, ns, CHUNK_CH), f32)],
        scratch_shapes=[pltpu.VMEM((2, BLK, ns), f32), pltpu.VMEM((2, BLK, ns), f32),
                        pltpu.VMEM((2, BLK, ns), f32), pltpu.VMEM((2, SCAN_SEG, ns), f32)],
        sem=("parallel", "arbitrary"), rider=rider)


def _exchange(src, gather, name):
    return _run_exchange(_Exchange(src, gather), name)


def _sum0(x, name):
    n, r, c = x.shape
    tr = _pick(r, (512, 256, 128, 64, 32, 16, 8))

    def body(x_ref, o_ref):
        acc = x_ref[0].astype(f32)
        for k in range(1, n):
            acc = acc + x_ref[k].astype(f32)
        o_ref[...] = acc

    return pl.pallas_call(
        body, name=name, grid=(r // tr,), in_specs=[pl.BlockSpec((n, tr, c), lambda i: (0, i, 0))],
        out_specs=pl.BlockSpec((tr, c), lambda i: (i, 0)), out_shape=jax.ShapeDtypeStruct((r, c), f32),
        compiler_params=_params(("parallel",)),
    )(x)


LANES = 1024


def _pack(parts, dtype):
    flat = jnp.concatenate([p.astype(dtype).reshape(-1) for p in parts])
    pad = (-flat.shape[0]) % (16 * LANES)
    if pad:
        flat = jnp.concatenate([flat, jnp.zeros((pad,), dtype)])
    return flat.reshape(-1, LANES)


def _unpack(flat, shapes):
    out, off = [], 0
    for s in shapes:
        size = int(np.prod(s))
        out.append(flat[off:off + size].reshape(s))
        off += size
    return out


def _vec(a):
    return a.reshape(-1, 1, a.shape[-1])


def _local_step(xa, tgt, mods, wts, sm, n_x):
    n_tot, d = xa.shape
    kw = dict(nrows=n_tot, n_x_rows=n_x)
    kx = dict(nrows=n_x, n_x_rows=n_x)
    mv = lambda l, k: mods[l, :, k][:, None, :]
    mx = lambda l, k: mods[l, 0:1, k][:, None, :]
    nmix, nffn = sm["norm_mix"], sm["norm_ffn"]
    cos, sin = _rope_tables(n_x, n_tot - n_x)
    cfg_a, cfg_b = _AttnCfg("na", n_x, n_tot), _AttnCfg("swa", n_x, n_tot)

    h1 = _rowwise(_f_normmod, [xa], [_vec(nmix[0:1]), mv(0, 1), mv(0, 0)], [(d, bf16)], name="l0_norm_mix", **kw)[0]
    qkv32 = _mm(h1, wts("w_in"), "nn", f32, "l0_in_proj", comm=wts)
    qkv = _qkv_post(qkv32, cos, sin, "l0_qkv_post")
    rpb2 = jnp.zeros((128, 128), f32).at[:NA_HEADS * 15, :31].set(sm["rpb"].reshape(NA_HEADS * 15, 31))
    basis = jnp.zeros((128, GRID_W * GRID_W), f32).at[:31].set(_toeplitz_basis())
    tz = _small_dot(rpb2, basis, "rpb_expand")[:NA_HEADS * 15].reshape(NA_HEADS, 15, GRID_W, GRID_W)
    bias = jnp.stack([tz[:, 3 - a:15 - a] for a in range(4)], axis=1).transpose(0, 1, 3, 2, 4).reshape(
        NA_HEADS, BLK, 3 * BLK)
    bias_t = jnp.stack([tz[:, a:a + 12][:, ::-1] for a in range(4)], axis=2).transpose(0, 1, 3, 2, 4).reshape(
        NA_HEADS, 3 * BLK, BLK)
    sink_rows = jnp.repeat(sm["sink"].reshape(NB_KV_HEADS, NB_GROUP, 1), BLK, axis=1).reshape(
        NB_KV_HEADS, NB_GROUP * BLK, 1)
    oa, lse_a = _attn_fwd(qkv, cfg_a, bias, "l0_na_fwd", rider=wts.take(CARRIER_US["na_fwd"]))
    ob, lse_b = _attn_fwd(qkv, cfg_b, sink_rows, "l0_swa_fwd", rider=wts.take(CARRIER_US["swa_fwd"]))
    w_out_a, w_out_b = wts("w_out")[:A_W], wts("w_out")[A_W:]
    y1, xb = _mm(oa, w_out_a, "nn", f32, "l0_out_proj", a2=ob, b2=w_out_b, comm=wts, gated=(xa, mv(0, 2), n_x))

    def ffn_fwd(xin, l, vec_of, kk, tag):
        h = _rowwise(_f_normmod, [xin], [_vec(nffn[l:l + 1]), vec_of(l, 4), vec_of(l, 3)], [(d, bf16)],
                     name=tag + "_norm_ffn", **kk)[0]
        a, b, u = _ffn_up(h, wts("w1", l), wts("w3", l), tag + "_ffn_up", comm=wts)
        fo = _mm(u, wts("w2", l), "nn", f32, tag + "_ffn_w2", comm=wts)
        xo = _rowwise(_f_gated_add, [xin, fo], [vec_of(l, 5)], [(d, f32)], name=tag + "_res_ffn", **kk)[0]
        return xo, (h, a, b, u, fo)

    def ffn_bwd(d_out, xin, saved, l, vec_of, kk, tag):
        h, a, b, u, fo = saved
        rows = kk["nrows"]
        dfo, dg2 = _rowwise_bwd(_f_gated_add, [xin, fo], [vec_of(l, 5)], [d_out], {1: bf16},
                                name=tag + "_res_ffn_bwd", **kk)
        wts.grad("w2", l, _mm(u[:rows], dfo, "tn", f32, tag + "_ffn_w2_dw", comm=wts))
        da, db = _ffn_down_bwd(dfo, wts("w2", l), a, b, tag + "_ffn_down_bwd", comm=wts)
        dh = _mm(da, wts("w1", l), "nt", f32, tag + "_ffn_w13_dx", a2=db, b2=wts("w3", l), comm=wts)
        wts.grad("w1", l, _mm(h[:rows], da, "tn", f32, tag + "_ffn_w1_dw", comm=wts))
        wts.grad("w3", l, _mm(h[:rows], db, "tn", f32, tag + "_ffn_w3_dw", comm=wts))
        dxin, dnw, dsc, dsh = _rowwise_bwd(
            _f_normmod, [xin], [_vec(nffn[l:l + 1]), vec_of(l, 4), vec_of(l, 3)], [dh], {0: f32},
            name=tag + "_norm_ffn_bwd", residual=d_out, **kk)
        return dxin, dnw, dsc, dsh, dg2

    xc, ffn0 = ffn_fwd(xb, 0, mv, kw, "l0")

    hs = _rowwise(_f_normmod, [xc], [_vec(nmix[1:2]), mv(1, 1), mv(1, 0)], [(d, f32)], name="l1_norm_mix", **kw)[0]
    g2n = sm["a_re"].shape[1]
    nq = d // CHUNK_CH
    ar2, ai2 = sm["a_re"].reshape(2 * g2n, SSM_STATE), sm["a_im"].reshape(2 * g2n, SSM_STATE)
    ldt2 = sm["log_dt"].reshape(2 * g2n, 1)
    bt_re = sm["b_re"].transpose(3, 0, 1, 2).reshape(SSM_GROUP, 2 * g2n, SSM_STATE)
    bt_im = sm["b_im"].transpose(3, 0, 1, 2).reshape(SSM_GROUP, 2 * g2n, SSM_STATE)
    disc_in = [ar2, ai2, ldt2, bt_re, bt_im]
    lam_r, lam_i, bbar_r, bbar_i = _whole(
        _f_discretise, disc_in,
        [((2 * g2n, SSM_STATE), f32)] * 2 + [((SSM_GROUP, 2 * g2n, SSM_STATE), f32)] * 2, "s5_discretise")
    eye = jnp.eye(GROUPS_PER_CHUNK, dtype=f32)
    eye6 = eye.astype(bf16)[None, None, :, None, :, None]

    def blockdiag_b(bbar):
        t = bbar.astype(bf16).reshape(SSM_GROUP, 2, nq, GROUPS_PER_CHUNK, SSM_STATE).transpose(1, 2, 3, 0, 4)
        return (t[:, :, :, :, None, :] * eye6).reshape(2, nq, CHUNK_CH, N_STATE)

    def blockdiag_c(cw):
        t = cw.astype(bf16).reshape(2, nq, GROUPS_PER_CHUNK, SSM_GROUP, SSM_STATE).transpose(0, 1, 2, 4, 3)
        return (t[:, :, :, :, None, :] * eye6).reshape(2, nq, N_STATE, CHUNK_CH)

    lam = jnp.stack([lam_r.reshape(2, nq, N_STATE), lam_i.reshape(2, nq, N_STATE)], axis=2)
    bblk = jnp.stack([blockdiag_b(bbar_r), blockdiag_b(bbar_i)], axis=2)
    cblk = jnp.stack([blockdiag_c(sm["c_re"]), blockdiag_c(sm["c_im"])], axis=2)
    bblk16, cblk16 = bblk, cblk
    bblk_t, cblk_t = bblk16.transpose(0, 1, 2, 4, 3), cblk16.transpose(0, 1, 2, 4, 3)
    pm, pmt = _scan_perm()
    hs_p = _row_perm(hs, pm, "l1_s5_perm", out_dtype=bf16, split=False)
    ys, cps = [], []
    for dr in range(2):
        yd, cp = _s5_fwd(hs_p, lam[dr], bblk16[dr], cblk16[dr], n_x, dr == 1, "l1_s5_fwd%d" % dr,
                         rider=wts.take(CARRIER_US["s5_fwd"]))
        ys.append(yd)
        cps.append(cp)
    dvec = sm["ssm_d"].reshape(1, d)
    gy = _glu_pre(hs, ys[0], ys[1], dvec, pmt, n_x, "l1_glu_pre")
    zv = _mm(gy, wts("w_glu_v"), "nn", f32, "l1_glu_val", comm=wts)
    zg = _mm(gy, wts("w_glu_g"), "nn", f32, "l1_glu_gate", comm=wts)
    bv, bg = sm["b_glu"][:d].reshape(1, 1, d), sm["b_glu"][d:].reshape(1, 1, d)
    xd = _rowwise(_f_glu_post, [xc, zv, zg], [mx(1, 2), bv, bg], [(d, f32)], name="l1_glu_post", **kx)[0]
    xe, ffn1 = ffn_fwd(xd, 1, mx, kx, "l1")

    d_xe, d_nfinal, loss_blk = _loss_head(xe, tgt, sm["norm_final"].reshape(1, d), nrows=n_x, name="loss_head")
    d_xd, dnffn1, dsc2_1, dsh2_1, dg2_1 = ffn_bwd(d_xe, xd, ffn1, 1, mx, kx, "l1")
    dzv, dzg, dg1_1, dbv, dbg = _rowwise_bwd(_f_glu_post, [xc, zv, zg], [mx(1, 2), bv, bg], [d_xd],
                                             {1: bf16, 2: bf16}, name="l1_glu_post_bwd", **kx)
    dgy = _mm(dzv, wts("w_glu_v"), "nt", f32, "l1_glu_dx", a2=dzg, b2=wts("w_glu_g"), comm=wts)
    dwglu_v = _mm(gy, dzv, "tn", f32, "l1_glu_val_dw", comm=wts)
    dwglu_g = _mm(gy, dzg, "tn", f32, "l1_glu_gate_dw", comm=wts)
    wts.grad("w_glu", 0, jnp.concatenate([dwglu_v, dwglu_g], axis=1))
    dy_p, du_skip, d_dvec = _glu_pre_bwd(hs, ys[0], ys[1], dvec, pm, pmt, dgy, n_x, "l1_glu_pre_bwd")
    du0, dlam0, db0, dc0 = _s5_bwd(hs_p, dy_p, cps[0], lam[0], bblk16[0], cblk16[0], bblk_t[0], cblk_t[0], n_x,
                                   False, "l1_s5_bwd0", rider=wts.take(CARRIER_US["s5_bwd"]))
    du1, dlam1, db1, dc1 = _s5_bwd(hs_p, dy_p, cps[1], lam[1], bblk16[1], cblk16[1], bblk_t[1], cblk_t[1], n_x,
                                   True, "l1_s5_bwd1", add=du0, rider=wts.take(CARRIER_US["s5_bwd"]))
    d_hs = _row_perm(du1, pmt, "l1_s5_unperm", out_dtype=f32, split=True, add=du_skip, add_rows=n_x)
    d_xc, dnmix1, dsc1_1, dsh1_1 = _rowwise_bwd(
        _f_normmod, [xc], [_vec(nmix[1:2]), mv(1, 1), mv(1, 0)], [d_hs], {0: f32},
        name="l1_norm_mix_bwd", residual=d_xd, residual_rows=n_x, **kw)
    dlam = jnp.stack([dlam0, dlam1])
    dlam_r, dlam_i = dlam[:, :, 0].reshape(2 * g2n, SSM_STATE), dlam[:, :, 1].reshape(2 * g2n, SSM_STATE)
    dbb = jnp.stack([db0, db1]).reshape(2, nq, 2, GROUPS_PER_CHUNK, SSM_GROUP, GROUPS_PER_CHUNK, SSM_STATE)
    eye7 = eye[None, None, None, :, None, :, None]
    dbbar = (dbb * eye7).sum(axis=5).transpose(2, 4, 0, 1, 3, 5).reshape(2, SSM_GROUP, 2 * g2n, SSM_STATE)
    dcc = jnp.stack([dc0, dc1]).reshape(2, nq, 2, GROUPS_PER_CHUNK, SSM_STATE, GROUPS_PER_CHUNK, SSM_GROUP)
    dcw = (dcc * eye7).sum(axis=5).transpose(2, 0, 1, 3, 5, 4).reshape(2, 2, g2n, SSM_GROUP, SSM_STATE)
    d_ar, d_ai, d_ldt, d_btr, d_bti = _whole_bwd(_f_discretise, disc_in, [dlam_r, dlam_i, dbbar[0], dbbar[1]],
                                                 "s5_discretise_bwd")
    to_b = lambda t: t.reshape(SSM_GROUP, 2, g2n, SSM_STATE).transpose(1, 2, 3, 0)
    wts.early(dict(
        loss=loss_blk[0, 0].reshape(1), a_re=d_ar.reshape(sm["a_re"].shape), a_im=d_ai.reshape(sm["a_im"].shape),
        log_dt=d_ldt.reshape(sm["log_dt"].shape), b_re=to_b(d_btr), b_im=to_b(d_bti), c_re=dcw[0], c_im=dcw[1],
        ssm_d=d_dvec.reshape(d), b_glu=jnp.concatenate([dbv.reshape(d), dbg.reshape(d)]),
        norm_final=d_nfinal.reshape(d)))

    d_xb, dnffn0, dsc2_0, dsh2_0, dg2_0 = ffn_bwd(d_xc, xb, ffn0, 0, mv, kw, "l0")
    dy1, dg1_0 = _rowwise_bwd(_f_gated_add, [xa, y1], [mv(0, 2)], [d_xb], {1: bf16}, name="l0_res_mix_bwd", **kw)
    d_o = _mm(dy1, wts("w_out"), "nt", bf16, "l0_out_proj_dx", comm=wts)
    wts.grad("w_out", 0, jnp.concatenate([_mm(oa, dy1, "tn", f32, "l0_out_proj_dw_a", comm=wts),
                                          _mm(ob, dy1, "tn", f32, "l0_out_proj_dw_b", comm=wts)], axis=0))
    dqa, delta_a, dbias = _attn_bwd_q(qkv, oa, d_o, lse_a, cfg_a, bias, "l0_na_bwd_q",
                                      rider=wts.take(CARRIER_US["na_bwd_q"]))
    dka, dva = _attn_bwd_kv(qkv, d_o, lse_a, delta_a, cfg_a, bias_t, "l0_na_bwd_kv",
                            rider=wts.take(CARRIER_US["na_bwd_kv"]))
    dqb, delta_b, dsink_rows = _attn_bwd_q(qkv, ob, d_o, lse_b, cfg_b, sink_rows, "l0_swa_bwd_q",
                                           rider=wts.take(CARRIER_US["swa_bwd_q"]))
    dkb, dvb = _attn_bwd_kv(qkv, d_o, lse_b, delta_b, cfg_b, None, "l0_swa_bwd_kv",
                            rider=wts.take(CARRIER_US["swa_bwd_kv"]))
    d_qkv = _qkv_post_bwd([dqa, dka, dva, dqb, dkb, dvb], cos, sin, "l0_qkv_post_bwd")
    wts.grad("w_in", 0, _mm(h1, d_qkv, "tn", f32, "l0_in_proj_dw", comm=wts))
    dh1 = _mm(d_qkv, wts("w_in"), "nt", f32, "l0_in_proj_dx", comm=wts)
    d_xa, dnmix0, dsc1_0, dsh1_0 = _rowwise_bwd(
        _f_normmod, [xa], [_vec(nmix[0:1]), mv(0, 1), mv(0, 0)], [dh1], {0: f32},
        name="l0_norm_mix_bwd", residual=d_xb, latent_grads_only=True, **kw)
    dbias5 = dbias.reshape(NA_HEADS, 4, GRID_W, 12, GRID_W).transpose(0, 1, 3, 2, 4)
    dtz = sum(jnp.pad(dbias5[:, a], ((0, 0), (3 - a, a), (0, 0), (0, 0))) for a in range(4))
    dtz2 = jnp.zeros((128, GRID_W * GRID_W), f32).at[:NA_HEADS * 15].set(dtz.reshape(NA_HEADS * 15, -1))
    d_rpb = _small_dot(dtz2, basis.T, "rpb_expand_bwd")[:NA_HEADS * 15, :31].reshape(sm["rpb"].shape)
    d_sink = dsink_rows.reshape(NB_KV_HEADS * NB_GROUP, BLK).sum(axis=1)

    zero_c = jnp.zeros((1, 1, d), f32)
    both = lambda gx: jnp.concatenate([gx, zero_c], axis=0)
    dmods = jnp.stack([
        jnp.stack([dsh1_0, dsc1_0, dg1_0, dsh2_0, dsc2_0, dg2_0], axis=2),
        jnp.stack([dsh1_1, dsc1_1, both(dg1_1), both(dsh2_1), both(dsc2_1), both(dg2_1)], axis=2),
    ])[:, :, 0]
    late = dict(norm_mix=jnp.concatenate([dnmix0[0], dnmix1[0]]), norm_ffn=jnp.concatenate([dnffn0[0], dnffn1[0]]),
                rpb=d_rpb, sink=d_sink)
    return d_xa, late, dmods


WEIGHTS = ("c_ctx", "ada_w", "ada_b", "norm_mix", "norm_ffn", "ffn_w1", "ffn_w3", "ffn_w2", "attn_w_in",
           "attn_w_out", "attn_rpb", "attn_sink", "ssm_a_re", "ssm_a_im", "ssm_log_dt", "ssm_b_re", "ssm_b_im",
           "ssm_c_re", "ssm_c_im", "ssm_d", "ssm_w_glu", "ssm_b_glu", "norm_final")
SHARDED_BIG = ("ffn_w1", "ffn_w3", "ffn_w2", "attn_w_in", "attn_w_out", "ssm_w_glu")
BIG = ("ada_w",) + SHARDED_BIG
SMALL = tuple(n for n in WEIGHTS if n not in BIG)


BIG_SLABS = dict(w_in=("attn_w_in", "cols"), w_out=("attn_w_out", "rows"), w1=("ffn_w1", "cols"),
                 w3=("ffn_w3", "cols"), w2=("ffn_w2", "rows"), w_glu=("ssm_w_glu", "cols"))
GATHER_ORDER = (("w_in", 0), ("w_out", 0), ("w1", 0), ("w3", 0), ("w2", 0), ("w_glu", 0), ("w1", 1), ("w3", 1),
                ("w2", 1))


class _Weights(_Comm):
    def __init__(self, w, pack_early):
        super().__init__()
        self.w, self.pack_early = w, pack_early
        self.gathers = {key: self.post(w[BIG_SLABS[key[0]][0]][key[1]].astype(bf16), True) for key in GATHER_ORDER}
        self.full, self.scatters, self.early_gather = {}, {}, None

    def __call__(self, name, layer=0):
        key = ("w_glu", 0) if name.startswith("w_glu") else (name, layer)
        if key not in self.full:
            blk = self.finish(self.gathers[key])
            if BIG_SLABS[key[0]][1] == "cols":
                self.full[key] = blk.transpose(1, 0, 2).reshape(blk.shape[1], -1)
            else:
                self.full[key] = blk.reshape(-1, blk.shape[2])
        full = self.full[key]
        if name.startswith("w_glu"):
            half = full.shape[1] // 2
            return full[:, :half] if name == "w_glu_v" else full[:, half:]
        return full

    def grad(self, name, layer, dw):
        if BIG_SLABS[name][1] == "cols":
            blk = dw.reshape(dw.shape[0], N_DEV, -1).transpose(1, 0, 2)
        else:
            blk = dw.reshape(N_DEV, -1, dw.shape[1])
        self.scatters[(name, layer)] = self.post(blk.astype(bf16), False)

    def early(self, parts):
        self.early_shapes = {k: v.shape for k, v in parts.items()}
        self.early_gather = self.post(self.pack_early(parts), True)

    def reduced(self, name):
        short = next(k for k, v in BIG_SLABS.items() if v[0] == name)
        layers = [l for (n, l) in self.scatters if n == short]
        parts = [_sum0(self.finish(self.scatters[(short, l)]), "sum_%s%d" % (short, l)) for l in sorted(layers)]
        return jnp.stack(parts).reshape(self.w[name].shape)


def kernel(x, c, ctx, c_ctx, ada_w, ada_b, norm_mix, norm_ffn, ffn_w1, ffn_w3, ffn_w2, attn_w_in, attn_w_out, attn_rpb, attn_sink, ssm_a_re, ssm_a_im, ssm_log_dt, ssm_b_re, ssm_b_im, ssm_c_re, ssm_c_im, ssm_d, ssm_w_glu, ssm_b_glu, norm_final, loss_target, m_c_ctx, m_ada_w, m_ada_b, m_norm_mix, m_norm_ffn, m_ffn_w1, m_ffn_w3, m_ffn_w2, m_attn_w_in, m_attn_w_out, m_attn_rpb, m_attn_sink, m_ssm_a_re, m_ssm_a_im, m_ssm_log_dt, m_ssm_b_re, m_ssm_b_im, m_ssm_c_re, m_ssm_c_im, m_ssm_d, m_ssm_w_glu, m_ssm_b_glu, m_norm_final, v_c_ctx, v_ada_w, v_ada_b, v_norm_mix, v_norm_ffn, v_ffn_w1, v_ffn_w3, v_ffn_w2, v_attn_w_in, v_attn_w_out, v_attn_rpb, v_attn_sink, v_ssm_a_re, v_ssm_a_im, v_ssm_log_dt, v_ssm_b_re, v_ssm_b_im, v_ssm_c_re, v_ssm_c_im, v_ssm_d, v_ssm_w_glu, v_ssm_b_glu, v_norm_final):
    p = dict(locals())
    w = {n: p[n] for n in WEIGHTS}
    me = 4 * lax.axis_index("x") + 2 * lax.axis_index("y") + lax.axis_index("c")
    n_x, d = x.shape[1], x.shape[2]
    cols = ada_w.shape[2]
    d8 = d // N_DEV

    first = jnp.concatenate([c[0], ssm_d[0], ssm_b_glu[0]])[None]
    g0 = _exchange(first, True, "gather_vectors")[:, 0]
    c_all, d_full, bglu_full = g0[:, :d], g0[:, d:d + d8].reshape(d), g0[:, d + d8:].reshape(2 * d)
    cc = jnp.concatenate([c_all, c_ctx[None], jnp.zeros((16 - N_DEV - 1, d), f32)])
    sc_all = _whole(_f_silu, [cc], [((16, d), f32)], "silu_c")[0]
    my_cols = lambda a: lax.dynamic_slice_in_dim(a, me * cols, cols, axis=a.ndim - 1)
    mod_loc = jnp.stack([_mm(sc_all, ada_w[l], "nn", f32, "ada_mod%d" % l) for l in range(2)])
    mod_loc = mod_loc + my_cols(ada_b)[:, None, :]
    mg = _exchange(mod_loc.reshape(32, cols), True, "gather_mod")
    mod_all = mg.reshape(N_DEV, 2, 16, cols).transpose(1, 2, 0, 3).reshape(2, 16, N_DEV * cols)
    mod_x = lax.dynamic_index_in_dim(mod_all, me, axis=1, keepdims=False)
    mods = jnp.stack([mod_x, mod_all[:, N_DEV]], axis=1).reshape(2, 2, 6, d)

    wts = _Weights(w, lambda parts: _pack(list(parts.values()), f32))
    sm = dict(norm_mix=norm_mix, norm_ffn=norm_ffn, rpb=attn_rpb[0], sink=attn_sink[0], a_re=ssm_a_re[0],
              a_im=ssm_a_im[0], log_dt=ssm_log_dt[0], b_re=ssm_b_re[0], b_im=ssm_b_im[0], c_re=ssm_c_re[0],
              c_im=ssm_c_im[0], ssm_d=d_full, b_glu=bglu_full, norm_final=norm_final)

    xa = jnp.concatenate([x[0], ctx[0]], axis=0)
    d_xa, late, dmods = _local_step(xa, loss_target[0], mods, wts, sm, n_x)

    late_names = ("norm_mix", "norm_ffn", "rpb", "sink")
    parts = [late[n] for n in late_names] + [dmods[:, 0].reshape(2, 6 * d), dmods[:, 1].reshape(2, 6 * d)]
    shapes = [q.shape for q in parts]
    late_gather = wts.post(_pack(parts, f32), True)
    grads = {n: wts.reduced(n) for n in SHARDED_BIG}
    sg = wts.finish(late_gather)
    tot = _unpack(_sum0(sg, "sum_small").reshape(-1), shapes)
    early = _unpack(_sum0(wts.finish(wts.early_gather), "sum_small_early").reshape(-1),
                    list(wts.early_shapes.values()))
    ts = dict(zip(late_names, tot[:len(late_names)]), **dict(zip(wts.early_shapes, early)))
    loss = ts["loss"][0]
    tot_dmod_x, tot_dmod_c = tot[-2], tot[-1]
    off_x = sum(int(np.prod(s)) for s in shapes[:-2])
    dmod_x_all = sg.reshape(N_DEV, -1)[:, off_x:off_x + 12 * d].reshape(N_DEV, 2, 6 * d)
    dm = jnp.concatenate([dmod_x_all.transpose(1, 0, 2), tot_dmod_c[:, None, :],
                          jnp.zeros((2, 16 - N_DEV - 1, 6 * d), f32)], axis=1)
    dm_loc = my_cols(dm)
    grads["ada_w"] = jnp.stack([_mm(sc_all, dm_loc[l], "tn", f32, "ada_w_grad%d" % l) for l in range(2)])
    grads["ada_b"] = tot_dmod_x + tot_dmod_c
    dsc_part = _mm(dm_loc[0], ada_w[0], "nt", f32, "silu_c_grad", a2=dm_loc[1], b2=ada_w[1])[N_DEV:N_DEV + 1]
    dsc = _sum0(_exchange(dsc_part, True, "gather_cctx"), "sum_cctx")
    grads["c_ctx"] = _whole_bwd(_f_silu, [c_ctx[None]], [dsc], "silu_cctx_bwd")[0][0]
    grads.update(norm_mix=ts["norm_mix"], norm_ffn=ts["norm_ffn"], attn_rpb=ts["rpb"][None],
                 attn_sink=ts["sink"][None], ssm_a_re=ts["a_re"][None], ssm_a_im=ts["a_im"][None],
                 ssm_log_dt=ts["log_dt"][None], ssm_b_re=ts["b_re"][None], ssm_b_im=ts["b_im"][None],
                 ssm_c_re=ts["c_re"][None], ssm_c_im=ts["c_im"][None],
                 ssm_d=lax.dynamic_slice_in_dim(ts["ssm_d"], me * d8, d8)[None],
                 ssm_b_glu=lax.dynamic_slice_in_dim(ts["b_glu"], me * 2 * d8, 2 * d8)[None],
                 norm_final=ts["norm_final"])

    delta, new_m, new_v = {}, {}, {}
    for n in BIG:
        two_d = lambda a: a.reshape(-1, a.shape[-1])
        dl, m2, v2 = _adamw(two_d(w[n]), two_d(grads[n]), two_d(p["m_" + n]), two_d(p["v_" + n]), "adamw_" + n)
        delta[n], new_m[n], new_v[n] = (t.reshape(w[n].shape) for t in (dl, m2, v2))
    sshapes = [w[n].shape for n in SMALL]
    packs = [_pack([src[n] for n in SMALL], f32) for src in
             (w, grads, {n: p["m_" + n] for n in SMALL}, {n: p["v_" + n] for n in SMALL})]
    for store, t in zip((delta, new_m, new_v), _adamw(*packs, "adamw_small")):
        store.update(zip(SMALL, _unpack(t.reshape(-1), sshapes)))

    return (loss, d_xa[None], *[grads[n] for n in WEIGHTS], *[delta[n] for n in WEIGHTS],
            *[new_m[n] for n in WEIGHTS], *[new_v[n] for n in WEIGHTS])
```

```python
import functools
import math

import numpy as np
import jax
import jax.numpy as jnp
from jax import lax
from jax.experimental import pallas as pl
from jax.experimental.pallas import tpu as pltpu

f32, bf16 = jnp.float32, jnp.bfloat16

HEAD_DIM = 128
GRID_W = 64
NA_HEADS = 8
NB_Q_HEADS = 8
NB_KV_HEADS = 2
NB_GROUP = NB_Q_HEADS // NB_KV_HEADS
NA_ROWS = 8
NA_COLS = 16
SW_RADIUS = 128
ROPE_BASE = 10000.0
SSM_GROUP = 16
SSM_STATE = 64
EPS = 1e-6
NEG_INF = -1e30
ADAM_LR, ADAM_B1, ADAM_B2, ADAM_EPS, ADAM_WD, ADAM_STEP = 0.001, 0.9, 0.999, 1e-08, 0.01, 10

N_DEV = 8
BLK = 256
SCAN_SEG = 8
GROUPS_PER_CHUNK = 16
V7X_VMEM_LIMIT = 56 * 2 ** 20
MM_VMEM_BUDGET = 36 * 2 ** 20
MXU_FLOPS_PER_US = 7e8
CARRIER_US = dict(na_fwd=300, swa_fwd=280, na_bwd=500, swa_bwd=300, s5_fwd=540, s5_bwd=1100)

NT = (((1,), (1,)), ((), ()))
TN = (((0,), (0,)), ((), ()))


def _params(sem, side_effects=False):
    return pltpu.CompilerParams(dimension_semantics=sem, vmem_limit_bytes=V7X_VMEM_LIMIT,
                                has_side_effects=side_effects)


def _pick(n, cands):
    for c in cands:
        if n % c == 0:
            return c
    return n


MESH = pl.DeviceIdType.MESH
MIN_CHUNK_BYTES = 512 * 1024
EXCHANGE_BYTES_PER_US = 74e3
RIDER_OVERHANG = 1.1


def _peers():
    x, y, c = lax.axis_index("x"), lax.axis_index("y"), lax.axis_index("c")
    return 4 * x + 2 * y + c, lambda o: (x ^ (o >> 2), y ^ ((o >> 1) & 1), c ^ (o & 1))


class _Exchange:
    def __init__(self, src, gather):
        self.src, self.gather, self.result = src, gather, None
        self.out_shape = (N_DEV,) + src.shape if gather else src.shape
        slab = self.out_shape[1:]
        slab_bytes = int(np.prod(slab)) * src.dtype.itemsize
        self.nch = next((k for k in (4, 2) if slab[0] % (16 * k) == 0 and slab_bytes // k >= MIN_CHUNK_BYTES), 1)
        self.step = slab[0] // self.nch
        self.cost_us = (N_DEV - 1) * slab_bytes / EXCHANGE_BYTES_PER_US

    def scratch(self):
        return [pltpu.SemaphoreType.DMA((N_DEV - 1, self.nch)), pltpu.SemaphoreType.DMA((N_DEV - 1, self.nch)),
                pltpu.SemaphoreType.DMA(())]

    def _copies(self, src_ref, out_ref, send_sems, recv_sems, local_sem):
        me, peer = _peers()
        local = pltpu.make_async_copy(src_ref if self.gather else src_ref.at[me], out_ref.at[me], local_sem)

        def copy(o, k, src_idx, dst_idx):
            rows = pl.ds(k * self.step, self.step)
            return pltpu.make_async_remote_copy(
                src_ref=(src_ref if self.gather else src_ref.at[src_idx]).at[rows],
                dst_ref=out_ref.at[dst_idx].at[rows],
                send_sem=send_sems.at[o - 1, k], recv_sem=recv_sems.at[o - 1, k],
                device_id=peer(o), device_id_type=MESH)

        pairs = [(o, k) for k in range(self.nch) for o in range(1, N_DEV)]
        sends = [copy(o, k, me ^ o, me) for o, k in pairs]
        recvs = [copy(o, k, me, me ^ o) for o, k in pairs]
        return local, sends, recvs

    def start(self, *refs):
        local, sends, _ = self._copies(*refs)
        local.start()
        for cp in sends:
            cp.start()

    def wait(self, *refs):
        local, sends, recvs = self._copies(*refs)
        for cp in recvs:
            cp.wait_recv()
        for cp in sends:
            cp.wait_send()
        local.wait()


def _run_exchange(ex, name):
    def body(src_ref, out_ref, *sems):
        ex.start(src_ref, out_ref, *sems)
        ex.wait(src_ref, out_ref, *sems)

    ex.result = pl.pallas_call(
        body, name=name, out_shape=jax.ShapeDtypeStruct(ex.out_shape, ex.src.dtype),
        in_specs=[pl.BlockSpec(memory_space=pl.ANY)], out_specs=pl.BlockSpec(memory_space=pl.ANY),
        scratch_shapes=ex.scratch(), compiler_params=pltpu.CompilerParams(has_side_effects=True),
    )(ex.src)
    return ex.result


def _call(body, args, *, name, grid, in_specs, out_specs, out_shape, scratch_shapes=(), sem, rider=None):
    if not rider:
        return pl.pallas_call(body, name=name, grid=grid, in_specs=in_specs, out_specs=out_specs,
                              out_shape=out_shape, scratch_shapes=list(scratch_shapes),
                              compiler_params=_params(sem))(*args)
    n_in, n_out, n_sc, n_ex = len(in_specs), len(out_specs), len(scratch_shapes), len(rider)

    def wrapped(*refs):
        ins, ex_src = refs[:n_in], refs[n_in:n_in + n_ex]
        outs = refs[n_in + n_ex:n_in + n_ex + n_out]
        ex_out = refs[n_in + n_ex + n_out:n_in + 2 * n_ex + n_out]
        scr = refs[n_in + 2 * n_ex + n_out:n_in + 2 * n_ex + n_out + n_sc]
        sems = refs[n_in + 2 * n_ex + n_out + n_sc:]
        ids = [pl.program_id(a) for a in range(len(grid))]
        first = functools.reduce(jnp.logical_and, [i == 0 for i in ids])
        last = functools.reduce(jnp.logical_and, [i == g - 1 for i, g in zip(ids, grid)])

        @pl.when(first)
        def _():
            for e, ex in enumerate(rider):
                ex.start(ex_src[e], ex_out[e], *sems[3 * e:3 * e + 3])

        body(*ins, *outs, *scr)

        @pl.when(last)
        def _():
            for e, ex in enumerate(rider):
                ex.wait(ex_src[e], ex_out[e], *sems[3 * e:3 * e + 3])

    anyspace = pl.BlockSpec(memory_space=pl.ANY)
    res = pl.pallas_call(
        wrapped, name=name, grid=grid, in_specs=list(in_specs) + [anyspace] * n_ex,
        out_specs=list(out_specs) + [anyspace] * n_ex,
        out_shape=list(out_shape) + [jax.ShapeDtypeStruct(ex.out_shape, ex.src.dtype) for ex in rider],
        scratch_shapes=list(scratch_shapes) + [s for ex in rider for s in ex.scratch()],
        compiler_params=_params(("arbitrary",) * len(grid), side_effects=True),
    )(*args, *[ex.src for ex in rider])
    for ex, r in zip(rider, res[n_out:]):
        ex.result = r
    return list(res[:n_out])


class _Comm:
    def __init__(self):
        self.queue, self.count = [], 0

    def post(self, src, gather):
        ex = _Exchange(src, gather)
        self.queue.append(ex)
        return ex

    def take(self, budget_us):
        rider = []
        for ex in list(self.queue):
            if budget_us > 0 and ex.cost_us <= RIDER_OVERHANG * budget_us:
                budget_us -= ex.cost_us
                rider.append(ex)
                self.queue.remove(ex)
        return rider

    def finish(self, ex):
        while ex.result is None:
            self.count += 1
            _run_exchange(self.queue.pop(0), "exchange%d" % self.count)
        return ex.result


def _mm(a, b, mode, out_dtype, name, a2=None, b2=None, comm=None, gated=None):
    if mode == "tn":
        kdim, m, n = a.shape[0], a.shape[1], b.shape[1]
    elif mode == "nt":
        m, kdim, n = a.shape[0], a.shape[1], b.shape[0]
    else:
        m, kdim, n = a.shape[0], a.shape[1], b.shape[1]
    if mode == "tn":
        tm = _pick(m, (2048, 1024, 512, 256, 128))
        tk = _pick(kdim, (1024, 768, 512, 256, 128))
    else:
        tm = _pick(m, (1024, 768, 512, 256, 128))
        tk = _pick(kdim, (2048, 1024, 768, 512, 256, 128))
    in_bytes = max(a.dtype.itemsize, b.dtype.itemsize) * (1 if a2 is None else 2)
    out_bytes = jnp.dtype(out_dtype).itemsize

    def vmem(tn_):
        return 2 * in_bytes * tk * (tm + tn_) + (4 + 2 * out_bytes + (0 if gated is None else 16)) * tm * tn_

    tn = next((t for t in (2048, 1024, 512, 256, 128) if n % t == 0 and vmem(t) <= MM_VMEM_BUDGET), n)
    nk = kdim // tk
    if mode == "tn":
        a_spec = pl.BlockSpec((tk, tm), lambda i, j, k: (k, i))
        b_spec = pl.BlockSpec((tk, tn), lambda i, j, k: (k, j))
        dims = TN
    elif mode == "nt":
        a_spec = pl.BlockSpec((tm, tk), lambda i, j, k: (i, k))
        b_spec = pl.BlockSpec((tn, tk), lambda i, j, k: (j, k))
        dims = NT
    else:
        a_spec = pl.BlockSpec((tm, tk), lambda i, j, k: (i, k))
        b_spec = pl.BlockSpec((tk, tn), lambda i, j, k: (k, j))
        dims = (((1,), (0,)), ((), ()))

    pairs = 1 if a2 is None else 2
    n_in = 2 * pairs + (0 if gated is None else 2)

    def body(*refs):
        outs, acc_ref = refs[n_in:-1], refs[-1]
        i, k = pl.program_id(0), pl.program_id(2)

        @pl.when(k == 0)
        def _():
            acc_ref[...] = jnp.zeros_like(acc_ref)

        for p in range(pairs):
            acc_ref[...] += lax.dot_general(refs[2 * p][...].astype(bf16), refs[2 * p + 1][...].astype(bf16), dims,
                                            preferred_element_type=f32)

        @pl.when(k == nk - 1)
        def _():
            acc = acc_ref[...]
            outs[0][...] = acc.astype(outs[0].dtype)
            if gated is not None:
                res_ref, gate_ref = refs[2 * pairs], refs[2 * pairs + 1]
                gate = gate_ref[0]
                if gate_ref.shape[0] == 2:
                    row = i * tm + lax.broadcasted_iota(jnp.int32, (tm, 1), 0)
                    gate = jnp.where(row >= gated[2], gate_ref[1], gate)
                outs[1][...] = res_ref[...] + gate * acc

    args = (a, b) if a2 is None else (a, b, a2, b2)
    ospec = pl.BlockSpec((tm, tn), lambda i, j, k: (i, j))
    in_specs, out_specs = [a_spec, b_spec] * pairs, [ospec]
    out_shape = [jax.ShapeDtypeStruct((m, n), out_dtype)]
    if gated is not None:
        res, gate, _ = gated
        args = args + (res, gate)
        in_specs += [ospec, pl.BlockSpec((gate.shape[0], 1, tn), lambda i, j, k: (0, 0, j))]
        out_specs.append(ospec)
        out_shape.append(jax.ShapeDtypeStruct((m, n), f32))
    rider = comm.take(2.0 * pairs * m * n * kdim / MXU_FLOPS_PER_US) if comm else None
    res = _call(
        body, args, name=name, grid=(m // tm, n // tn, nk), in_specs=in_specs, out_specs=out_specs,
        out_shape=out_shape, scratch_shapes=[pltpu.VMEM((tm, tn), f32)],
        sem=("parallel", "parallel", "arbitrary"), rider=rider)
    return res[0] if gated is None else res


def _ffn_up(h, w1, w3, name, comm=None):
    m, kdim = h.shape
    n = w1.shape[1]
    tm, tn = _pick(m, (1024, 768, 512, 256, 128)), _pick(n, (512, 256, 128))

    def body(h_ref, w1_ref, w3_ref, a_ref, b_ref, u_ref):
        hv = h_ref[...]
        a = jnp.dot(hv, w1_ref[...], preferred_element_type=f32).astype(bf16)
        b = jnp.dot(hv, w3_ref[...], preferred_element_type=f32).astype(bf16)
        a_ref[...] = a
        b_ref[...] = b
        u_ref[...] = (jax.nn.silu(a.astype(f32)) * b.astype(f32)).astype(bf16)

    wspec = pl.BlockSpec((kdim, tn), lambda i, j: (0, j))
    ospec = pl.BlockSpec((tm, tn), lambda i, j: (i, j))
    rider = comm.take(4.0 * m * n * kdim / MXU_FLOPS_PER_US) if comm else None
    return _call(
        body, (h, w1, w3), name=name, grid=(m // tm, n // tn),
        in_specs=[pl.BlockSpec((tm, kdim), lambda i, j: (i, 0)), wspec, wspec], out_specs=[ospec] * 3,
        out_shape=[jax.ShapeDtypeStruct((m, n), bf16)] * 3, sem=("parallel", "parallel"), rider=rider)


def _ffn_down_bwd(g, w2, a, b, name, comm=None):
    m, kdim = g.shape
    n = w2.shape[0]
    tm, tn = _pick(m, (1024, 768, 512, 256, 128)), _pick(n, (512, 256, 128))

    def body(g_ref, w_ref, a_ref, b_ref, da_ref, db_ref):
        du = lax.dot_general(g_ref[...], w_ref[...], NT, preferred_element_type=f32)
        _, vjp = jax.vjp(lambda p, q: jax.nn.silu(p) * q, a_ref[...].astype(f32), b_ref[...].astype(f32))
        da, db = vjp(du)
        da_ref[...] = da.astype(bf16)
        db_ref[...] = db.astype(bf16)

    ospec = pl.BlockSpec((tm, tn), lambda i, j: (i, j))
    rider = comm.take(2.0 * m * n * kdim / MXU_FLOPS_PER_US) if comm else None
    return _call(
        body, (g, w2, a, b), name=name, grid=(m // tm, n // tn),
        in_specs=[pl.BlockSpec((tm, kdim), lambda i, j: (i, 0)), pl.BlockSpec((tn, kdim), lambda i, j: (j, 0)),
                  ospec, ospec],
        out_specs=[ospec] * 2, out_shape=[jax.ShapeDtypeStruct((m, n), bf16)] * 2,
        sem=("parallel", "parallel"), rider=rider)


def _small_dot(a, b, name):
    def body(a_ref, b_ref, o_ref):
        o_ref[...] = jnp.dot(a_ref[...], b_ref[...], precision=lax.Precision.HIGHEST, preferred_element_type=f32)

    return pl.pallas_call(body, name=name, out_shape=jax.ShapeDtypeStruct((a.shape[0], b.shape[1]), f32))(a, b)


def _group_of(i, n_x_tiles, n_groups):
    return jnp.where(i >= n_x_tiles, n_groups - 1, 0)


def _rowwise(f, rows, vecs, outs, *, nrows, n_x_rows, name, tm=BLK, tc=None):
    n_x_tiles = n_x_rows // tm
    grid = (nrows // tm,) if tc is None else (nrows // tm, rows[0].shape[1] // tc)

    def rspec(cols):
        if tc is None:
            return pl.BlockSpec((tm, cols), lambda i: (i, 0))
        return pl.BlockSpec((tm, tc), lambda i, j: (i, j))

    def vspec(v):
        g = v.shape[0]
        if tc is None:
            return pl.BlockSpec((None, 1, v.shape[2]), lambda i: (_group_of(i, n_x_tiles, g), 0, 0))
        return pl.BlockSpec((None, 1, tc), lambda i, j: (_group_of(i, n_x_tiles, g), 0, j))

    nr, nv = len(rows), len(vecs)

    def body(*refs):
        ins = [r[...] for r in refs[:nr + nv]]
        res = f(*ins)
        for o_ref, val in zip(refs[nr + nv:], res):
            o_ref[...] = val.astype(o_ref.dtype)

    return pl.pallas_call(
        body, name=name, grid=grid,
        in_specs=[rspec(r.shape[1]) for r in rows] + [vspec(v) for v in vecs],
        out_specs=[rspec(c) for c, _ in outs],
        out_shape=[jax.ShapeDtypeStruct((nrows, c), d) for c, d in outs],
        compiler_params=_params(("parallel",) * len(grid)),
    )(*rows, *vecs)


def _rowwise_bwd(f, rows, vecs, cts, row_grads, *, nrows, n_x_rows, name, tm=BLK, tc=None, residual=None,
                 residual_rows=None, latent_grads_only=False):
    n_x_tiles = n_x_rows // tm
    n_tiles = nrows // tm
    out_rows = n_x_rows if latent_grads_only else nrows
    res_tiles = None if residual_rows is None else residual_rows // tm
    grid = (n_tiles,) if tc is None else (rows[0].shape[1] // tc, n_tiles)
    row_of = (lambda *g: g[0]) if tc is None else (lambda *g: g[1])

    def rspec(cols):
        if tc is None:
            return pl.BlockSpec((tm, cols), lambda i: (i, 0))
        return pl.BlockSpec((tm, tc), lambda j, i: (i, j))

    def vspec(v):
        g = v.shape[0]
        if tc is None:
            return pl.BlockSpec((None, 1, v.shape[2]), lambda i: (_group_of(i, n_x_tiles, g), 0, 0))
        return pl.BlockSpec((None, 1, tc), lambda j, i: (_group_of(i, n_x_tiles, g), 0, j))

    nr, nv, nc = len(rows), len(vecs), len(cts)
    gidx = sorted(row_grads)

    def body(*refs):
        i = row_of(*[pl.program_id(d) for d in range(len(grid))])
        row_vals = [r[...].astype(f32) for r in refs[:nr]]
        vec_vals = [jnp.broadcast_to(r[...].astype(f32), (tm, r.shape[-1])) for r in refs[nr:nr + nv]]
        ct_vals = [r[...].astype(f32) for r in refs[nr + nv:nr + nv + nc]]
        out_refs = refs[nr + nv + nc + (residual is not None):]
        res, vjp = jax.vjp(lambda *a: tuple(o.astype(f32) for o in f(*a)), *row_vals, *vec_vals)
        grads = list(vjp(tuple(ct_vals)))
        if residual is not None:
            extra = refs[nr + nv + nc][...]
            if res_tiles is not None:
                extra = jnp.where(i < res_tiles, extra, 0.0)
            grads[0] = grads[0] + extra
        for o_ref, k in zip(out_refs[:len(gidx)], gidx):
            if latent_grads_only:
                @pl.when(i < n_x_tiles)
                def _():
                    o_ref[...] = grads[k].astype(o_ref.dtype)
            else:
                o_ref[...] = grads[k].astype(o_ref.dtype)
        for o_ref, g, v in zip(out_refs[len(gidx):], grads[nr:], vecs):
            part = jnp.sum(g, axis=0, keepdims=True)
            first = (i == 0) if v.shape[0] == 1 else ((i == 0) | (i == n_x_tiles))

            @pl.when(first)
            def _():
                o_ref[...] = part

            @pl.when(jnp.logical_not(first))
            def _():
                o_ref[...] += part

    sem = ("arbitrary",) if tc is None else ("parallel", "arbitrary")
    gspec = rspec
    if latent_grads_only:
        assert tc is None
        gspec = lambda cols: pl.BlockSpec((tm, cols), lambda i: (jnp.minimum(i, n_x_tiles - 1), 0))
    res_specs, res_args = [], []
    if residual is not None:
        assert tc is None
        clamp = (lambda i: i) if res_tiles is None else (lambda i: jnp.minimum(i, res_tiles - 1))
        res_specs = [pl.BlockSpec((tm, residual.shape[1]), lambda i: (clamp(i), 0))]
        res_args = [residual]
    return pl.pallas_call(
        body, name=name, grid=grid,
        in_specs=[rspec(r.shape[1]) for r in rows] + [vspec(v) for v in vecs] + [rspec(c.shape[1]) for c in cts]
        + res_specs,
        out_specs=[gspec(rows[k].shape[1]) for k in gidx] + [vspec(v) for v in vecs],
        out_shape=[jax.ShapeDtypeStruct((out_rows, rows[k].shape[1]), row_grads[k]) for k in gidx]
        + [jax.ShapeDtypeStruct(v.shape, f32) for v in vecs],
        compiler_params=_params(sem),
    )(*rows, *vecs, *cts, *res_args)


def _rms(x, w):
    return x * lax.rsqrt(jnp.mean(x * x, axis=-1, keepdims=True) + EPS) * w


def _f_normmod(x, w, sc, sh):
    return (_rms(x.astype(f32), w) * (1.0 + sc) + sh,)


def _f_gated_add(x, y, g):
    return (x + g * y.astype(f32),)


def _f_glu_post(x, zv, zg, g, bv, bg):
    return (x + g * ((zv.astype(f32) + bv) * jax.nn.sigmoid(zg.astype(f32) + bg)),)


def _f_silu(x):
    return (jax.nn.silu(x.astype(f32)),)


def _loss_head(x, tgt, w, *, nrows, name):
    d = x.shape[1]
    tm = BLK

    def body(x_ref, t_ref, w_ref, dx_ref, dw_ref, loss_ref):
        i = pl.program_id(0)
        wb = jnp.broadcast_to(w_ref[...], (tm, d))
        y, vjp = jax.vjp(_rms, x_ref[...], wb)
        e = y - t_ref[...]
        dx, dwb = vjp(e * (1.0 / d))
        dx_ref[...] = dx
        dw = jnp.sum(dwb, axis=0, keepdims=True)
        part = jnp.full((8, 128), 0.5 / d, f32) * jnp.sum(e * e)

        @pl.when(i == 0)
        def _():
            dw_ref[...] = dw
            loss_ref[...] = part

        @pl.when(i > 0)
        def _():
            dw_ref[...] += dw
            loss_ref[...] += part

    row = pl.BlockSpec((tm, d), lambda i: (i, 0))
    return pl.pallas_call(
        body, name=name, grid=(nrows // tm,),
        in_specs=[row, row, pl.BlockSpec((1, d), lambda i: (0, 0))],
        out_specs=[row, pl.BlockSpec((1, d), lambda i: (0, 0)), pl.BlockSpec((8, 128), lambda i: (0, 0))],
        out_shape=[jax.ShapeDtypeStruct((nrows, d), f32), jax.ShapeDtypeStruct((1, d), f32),
                   jax.ShapeDtypeStruct((8, 128), f32)],
        compiler_params=_params(("arbitrary",)),
    )(x, tgt, w)


def _adamw(w, g, m, v, name):
    r, c = w.shape
    tr = _pick(r, (512, 256, 128, 64, 32, 16, 8))
    tcol = _pick(c, (1024, 512)) if c % 128 == 0 else c

    def body(w_ref, g_ref, m_ref, v_ref, d_ref, m2_ref, v2_ref):
        gg = g_ref[...]
        m2 = ADAM_B1 * m_ref[...] + (1.0 - ADAM_B1) * gg
        v2 = ADAM_B2 * v_ref[...] + (1.0 - ADAM_B2) * (gg * gg)
        m_hat = m2 / (1.0 - ADAM_B1 ** ADAM_STEP)
        v_hat = v2 / (1.0 - ADAM_B2 ** ADAM_STEP)
        d_ref[...] = -ADAM_LR * (m_hat / (jnp.sqrt(v_hat) + ADAM_EPS) + ADAM_WD * w_ref[...])
        m2_ref[...] = m2
        v2_ref[...] = v2

    spec = pl.BlockSpec((tr, tcol), lambda i, j: (i, j))
    return pl.pallas_call(
        body, name=name, grid=(r // tr, c // tcol), in_specs=[spec] * 4, out_specs=[spec] * 3,
        out_shape=[jax.ShapeDtypeStruct((r, c), f32)] * 3,
        compiler_params=_params(("parallel", "parallel")),
    )(w, g, m, v)


def _swap_quarters(x):
    lane = lax.broadcasted_iota(jnp.int32, x.shape, 1)
    return jnp.where((lane & 63) < 32, pltpu.roll(x, 96, 1), pltpu.roll(x, 32, 1))


def _rope_tables(n_x, n_ctx):
    t = np.arange(n_x)
    quarter = HEAD_DIM // 4
    inv = ROPE_BASE ** (-np.arange(quarter, dtype=np.float64) / quarter)
    ar = (t // GRID_W)[:, None] * inv[None]
    ac = (t % GRID_W)[:, None] * inv[None]
    cos = np.concatenate([np.cos(ar), np.cos(ar), np.cos(ac), np.cos(ac)], axis=1)
    sin = np.concatenate([-np.sin(ar), np.sin(ar), -np.sin(ac), np.sin(ac)], axis=1)
    cos = np.concatenate([cos, np.ones((n_ctx, HEAD_DIM))], axis=0)
    sin = np.concatenate([sin, np.zeros((n_ctx, HEAD_DIM))], axis=0)
    return jnp.asarray(cos, f32), jnp.asarray(sin, f32)


A_W = NA_HEADS * HEAD_DIM
QB0, KB0, VB0 = 3 * A_W, 3 * A_W + NB_Q_HEADS * HEAD_DIM, 3 * A_W + (NB_Q_HEADS + NB_KV_HEADS) * HEAD_DIM
IN_W = VB0 + NB_KV_HEADS * HEAD_DIM


def _qkv_post(qkv, cos, sin, name):
    n = qkv.shape[0]

    def body(x_ref, c_ref, s_ref, o_ref):
        c, s = c_ref[...], s_ref[...]
        o_ref[:, :QB0] = x_ref[:, :QB0].astype(bf16)
        for col in range(QB0, VB0, HEAD_DIM):
            x = x_ref[:, col:col + HEAD_DIM]
            o_ref[:, col:col + HEAD_DIM] = (x * c + _swap_quarters(x) * s).astype(bf16)
        o_ref[:, VB0:] = x_ref[:, VB0:].astype(bf16)

    row = lambda c: pl.BlockSpec((BLK, c), lambda i: (i, 0))
    return pl.pallas_call(
        body, name=name, grid=(n // BLK,), in_specs=[row(IN_W), row(HEAD_DIM), row(HEAD_DIM)],
        out_specs=row(IN_W), out_shape=jax.ShapeDtypeStruct((n, IN_W), bf16),
        compiler_params=_params(("parallel",)),
    )(qkv, cos, sin)


def _qkv_post_bwd(parts, cos, sin, name):
    n = parts[0].shape[0]

    def body(qa, ka, va, qb, kb, vb, c_ref, s_ref, o_ref):
        c, s = c_ref[...], s_ref[...]
        o_ref[:, 0:A_W] = qa[...].astype(bf16)
        o_ref[:, A_W:2 * A_W] = ka[...].astype(bf16)
        o_ref[:, 2 * A_W:QB0] = va[...].astype(bf16)
        for src, col0, width in ((qb, QB0, KB0 - QB0), (kb, KB0, VB0 - KB0)):
            for off in range(0, width, HEAD_DIM):
                g = src[:, off:off + HEAD_DIM].astype(f32)
                o_ref[:, col0 + off:col0 + off + HEAD_DIM] = (g * c + _swap_quarters(g * s)).astype(bf16)
        o_ref[:, VB0:] = vb[...].astype(bf16)

    row = lambda c: pl.BlockSpec((BLK, c), lambda i: (i, 0))
    return pl.pallas_call(
        body, name=name, grid=(n // BLK,),
        in_specs=[row(p.shape[1]) for p in parts] + [row(HEAD_DIM), row(HEAD_DIM)],
        out_specs=row(IN_W), out_shape=jax.ShapeDtypeStruct((n, IN_W), bf16),
        compiler_params=_params(("parallel",)),
    )(*parts, cos, sin)


def _valid(kind, qpos, kpos, n_x):
    ok = (kpos >= 0) & (kpos < n_x) & (qpos >= 0) & (qpos < n_x)
    if kind == "na":
        rows = n_x // GRID_W
        qr, qc = lax.shift_right_arithmetic(qpos, 6), qpos & (GRID_W - 1)
        kr, kc = lax.shift_right_arithmetic(kpos, 6), kpos & (GRID_W - 1)
        kr0 = jnp.clip(qr - NA_ROWS // 2, 0, rows - NA_ROWS)
        ws = jnp.clip(qc - NA_COLS // 2, 0, GRID_W - NA_COLS)
        return ok & (kr >= kr0) & (kr < kr0 + NA_ROWS) & (kc >= ws) & (kc < ws + NA_COLS)
    return ok & (jnp.abs(kpos - qpos) <= SW_RADIUS)


def _stack_heads(x, g):
    if g == 1:
        return x
    return jnp.concatenate([x[:, a * HEAD_DIM:(a + 1) * HEAD_DIM] for a in range(g)], axis=0)


def _unstack_heads(x, g):
    if g == 1:
        return x
    r = x.shape[0] // g
    return jnp.concatenate([x[a * r:(a + 1) * r] for a in range(g)], axis=1)


def _add_per_head(s, m, g):
    if g == 1:
        return s + m
    return (s.reshape((g,) + m.shape) + m[None]).reshape(s.shape)


class _AttnCfg:
    def __init__(self, kind, n_x, n_tot):
        self.kind, self.n_x, self.n_tot = kind, n_x, n_tot
        self.n_xb, self.n_blk = n_x // BLK, n_tot // BLK
        if kind == "na":
            self.g, self.nkv, self.q0, self.k0, self.v0 = 1, NA_HEADS, 0, NA_HEADS, 2 * NA_HEADS
        else:
            self.g, self.nkv = NB_GROUP, NB_KV_HEADS
            self.q0, self.k0, self.v0 = QB0 // (NB_GROUP * HEAD_DIM), KB0 // HEAD_DIM, VB0 // HEAD_DIM
        self.r = BLK * self.g
        self.qw = HEAD_DIM * self.g
        self.scale = HEAD_DIM ** -0.5
        last = self.n_xb - 1
        rows = lax.broadcasted_iota(jnp.int32, (BLK, 3 * BLK), 0)
        cols = lax.broadcasted_iota(jnp.int32, (BLK, 3 * BLK), 1)
        self.mask_q = jnp.stack([jnp.where(_valid(kind, i * BLK + rows, (i - 1) * BLK + cols, n_x), 0.0, NEG_INF)
                                 for i in (0, 1, last, self.n_xb)]).astype(f32)
        self.pattern = lambda i: jnp.where(i == 0, 0, jnp.where(i < last, 1, jnp.where(i == last, 2, 3)))


def _attn_fwd(qkv, cfg, extra, name, rider=None):
    g, r, qw, n_xb, n_x = cfg.g, cfg.r, cfg.qw, cfg.n_xb, cfg.n_x
    last = n_xb - 1

    def body(q_ref, kp, ko, kn, vp, vo, vn, kc_ref, vc_ref, ex_ref, m_ref, o_ref, lse_ref):
        q = _stack_heads(q_ref[...], g)
        kw = jnp.concatenate([kp[...], ko[...], kn[...]], axis=0)
        vw = jnp.concatenate([vp[...], vo[...], vn[...]], axis=0)
        s = lax.dot_general(q, kw, NT, preferred_element_type=f32) * cfg.scale
        if cfg.kind == "na":
            s = s + ex_ref[...]
        s = _add_per_head(s, m_ref[...], g)
        sc = lax.dot_general(q, kc_ref[...], NT, preferred_element_type=f32) * cfg.scale
        m = jnp.maximum(jnp.max(s, axis=-1, keepdims=True), jnp.max(sc, axis=-1, keepdims=True))
        if cfg.kind == "swa":
            m = jnp.maximum(m, ex_ref[...])
        p, pc = jnp.exp(s - m), jnp.exp(sc - m)
        l = jnp.sum(p, axis=-1, keepdims=True) + jnp.sum(pc, axis=-1, keepdims=True)
        if cfg.kind == "swa":
            l = l + jnp.exp(ex_ref[...] - m)
        o = jnp.dot(p.astype(bf16), vw, preferred_element_type=f32) + jnp.dot(pc.astype(bf16), vc_ref[...],
                                                                             preferred_element_type=f32)
        o_ref[...] = _unstack_heads(o / l, g).astype(bf16)
        lse_ref[...] = m + jnp.log(l)

    kv = lambda col0, f: pl.BlockSpec((BLK, HEAD_DIM), lambda h, i: (f(i), col0 + h))
    prev = lambda i: jnp.clip(i - 1, 0, last)
    own = lambda i: jnp.minimum(i, last)
    nxt = lambda i: jnp.minimum(i + 1, last)
    ctx = lambda i: n_xb
    if cfg.kind == "na":
        ex_spec = pl.BlockSpec((None, BLK, 3 * BLK), lambda h, i: (h, 0, 0))
    else:
        ex_spec = pl.BlockSpec((None, r, 1), lambda h, i: (h, 0, 0))
    m_spec = pl.BlockSpec((None, BLK, 3 * BLK), lambda h, i: (cfg.pattern(i), 0, 0))
    return _call(
        body, (qkv,) * 9 + (extra, cfg.mask_q), name=name, grid=(cfg.nkv, cfg.n_blk),
        in_specs=[pl.BlockSpec((BLK, qw), lambda h, i: (i, cfg.q0 + h)),
                  kv(cfg.k0, prev), kv(cfg.k0, own), kv(cfg.k0, nxt),
                  kv(cfg.v0, prev), kv(cfg.v0, own), kv(cfg.v0, nxt),
                  kv(cfg.k0, ctx), kv(cfg.v0, ctx), ex_spec, m_spec],
        out_specs=[pl.BlockSpec((BLK, qw), lambda h, i: (i, h)),
                   pl.BlockSpec((None, None, r, 1), lambda h, i: (h, i, 0, 0))],
        out_shape=[jax.ShapeDtypeStruct((cfg.n_tot, cfg.nkv * qw), bf16),
                   jax.ShapeDtypeStruct((cfg.nkv, cfg.n_blk, r, 1), f32)],
        sem=("parallel", "parallel"), rider=rider)


def _attn_bwd(qkv, o, do, lse, cfg, extra, name, rider=None):
    g, r, qw, n_xb, n_x = cfg.g, cfg.r, cfg.qw, cfg.n_xb, cfg.n_x
    last = n_xb - 1
    do_col0 = 0 if cfg.kind == "na" else (NA_HEADS * HEAD_DIM) // qw

    def body(q_ref, kp, ko, kn, vp, vo, vn, kc_ref, vc_ref, ex_ref, o_ref, do_ref, lse_ref, m_ref,
             dq_ref, delta_ref, dex_ref, dk_ref, dv_ref):
        i = pl.program_id(1)
        q = _stack_heads(q_ref[...], g)
        dout = _stack_heads(do_ref[...], g)
        out = _stack_heads(o_ref[...], g)
        delta = jnp.sum(dout.astype(f32) * out.astype(f32), axis=-1, keepdims=True)
        delta_ref[...] = delta
        kw = jnp.concatenate([kp[...], ko[...], kn[...]], axis=0)
        vw = jnp.concatenate([vp[...], vo[...], vn[...]], axis=0)
        s = lax.dot_general(q, kw, NT, preferred_element_type=f32) * cfg.scale
        if cfg.kind == "na":
            s = s + ex_ref[...]
        s = _add_per_head(s, m_ref[...], g)
        sc = lax.dot_general(q, kc_ref[...], NT, preferred_element_type=f32) * cfg.scale
        lse_v = lse_ref[...]
        p, pc = jnp.exp(s - lse_v), jnp.exp(sc - lse_v)
        dp = lax.dot_general(dout, vw, NT, preferred_element_type=f32)
        dpc = lax.dot_general(dout, vc_ref[...], NT, preferred_element_type=f32)
        ds, dsc = p * (dp - delta), pc * (dpc - delta)
        ds16, dsc16 = ds.astype(bf16), dsc.astype(bf16)
        dq = jnp.dot(ds16, kw, preferred_element_type=f32) + jnp.dot(dsc16, kc_ref[...], preferred_element_type=f32)
        dq_ref[...] = _unstack_heads(dq * cfg.scale, g).astype(bf16)
        dex = ds if cfg.kind == "na" else -jnp.exp(ex_ref[...] - lse_v) * delta

        @pl.when(i == 0)
        def _():
            dex_ref[...] = dex
            dk_ref[...] = jnp.zeros_like(dk_ref)
            dv_ref[...] = jnp.zeros_like(dv_ref)

        @pl.when(i > 0)
        def _():
            dex_ref[...] += dex

        dkw = lax.dot_general(ds16, q, TN, preferred_element_type=f32) * cfg.scale
        dvw = lax.dot_general(p.astype(bf16), dout, TN, preferred_element_type=f32)
        for w, blk in enumerate((jnp.clip(i - 1, 0, last), jnp.minimum(i, last), jnp.minimum(i + 1, last))):
            rows = pl.ds(pl.multiple_of(blk * BLK, BLK), BLK)
            dk_ref[rows, :] += dkw[w * BLK:(w + 1) * BLK]
            dv_ref[rows, :] += dvw[w * BLK:(w + 1) * BLK]
        crow = pl.ds(n_xb * BLK, BLK)
        dk_ref[crow, :] += lax.dot_general(dsc16, q, TN, preferred_element_type=f32) * cfg.scale
        dv_ref[crow, :] += lax.dot_general(pc.astype(bf16), dout, TN, preferred_element_type=f32)

    kv = lambda col0, f: pl.BlockSpec((BLK, HEAD_DIM), lambda h, i: (f(i), col0 + h))
    prev = lambda i: jnp.clip(i - 1, 0, last)
    own = lambda i: jnp.minimum(i, last)
    nxt = lambda i: jnp.minimum(i + 1, last)
    ctx = lambda i: n_xb
    if cfg.kind == "na":
        ex_spec = pl.BlockSpec((None, BLK, 3 * BLK), lambda h, i: (h, 0, 0))
    else:
        ex_spec = pl.BlockSpec((None, r, 1), lambda h, i: (h, 0, 0))
    stat = pl.BlockSpec((None, None, r, 1), lambda h, i: (h, i, 0, 0))
    m_spec = pl.BlockSpec((None, BLK, 3 * BLK), lambda h, i: (cfg.pattern(i), 0, 0))
    kv_all = pl.BlockSpec((cfg.n_tot, HEAD_DIM), lambda h, i: (0, h))
    return _call(
        body, (qkv,) * 9 + (extra, o, do, lse, cfg.mask_q), name=name, grid=(cfg.nkv, cfg.n_blk),
        in_specs=[pl.BlockSpec((BLK, qw), lambda h, i: (i, cfg.q0 + h)),
                  kv(cfg.k0, prev), kv(cfg.k0, own), kv(cfg.k0, nxt),
                  kv(cfg.v0, prev), kv(cfg.v0, own), kv(cfg.v0, nxt),
                  kv(cfg.k0, ctx), kv(cfg.v0, ctx), ex_spec,
                  pl.BlockSpec((BLK, qw), lambda h, i: (i, h)),
                  pl.BlockSpec((BLK, qw), lambda h, i: (i, do_col0 + h)), stat, m_spec],
        out_specs=[pl.BlockSpec((BLK, qw), lambda h, i: (i, h)), stat, ex_spec, kv_all, kv_all],
        out_shape=[jax.ShapeDtypeStruct((cfg.n_tot, cfg.nkv * qw), bf16),
                   jax.ShapeDtypeStruct((cfg.nkv, cfg.n_blk, r, 1), f32),
                   jax.ShapeDtypeStruct(extra.shape, f32)]
        + [jax.ShapeDtypeStruct((cfg.n_tot, cfg.nkv * HEAD_DIM), f32)] * 2,
        sem=("parallel", "arbitrary"), rider=rider)


def _toeplitz_basis():
    qc, kc = np.meshgrid(np.arange(GRID_W), np.arange(GRID_W), indexing="ij")
    e = (kc - qc + NA_COLS - 1)[None] == np.arange(2 * NA_COLS - 1)[:, None, None]
    return e.reshape(2 * NA_COLS - 1, GRID_W * GRID_W).astype(np.float32)


def _whole(f, ins, outs, name):
    def body(*refs):
        res = f(*[r[...] for r in refs[:len(ins)]])
        for o_ref, val in zip(refs[len(ins):], res):
            o_ref[...] = val.astype(o_ref.dtype)

    return pl.pallas_call(body, name=name,
                          out_shape=[jax.ShapeDtypeStruct(s, d) for s, d in outs])(*ins)


def _whole_bwd(f, ins, cts, name):
    n = len(ins)

    def body(*refs):
        _, vjp = jax.vjp(f, *[r[...] for r in refs[:n]])
        grads = vjp(tuple(r[...] for r in refs[n:n + len(cts)]))
        for o_ref, g in zip(refs[n + len(cts):], grads):
            o_ref[...] = g

    return pl.pallas_call(body, name=name,
                          out_shape=[jax.ShapeDtypeStruct(a.shape, f32) for a in ins])(*ins, *cts)


def _f_discretise(ar, ai, ldt, br, bi):
    dt = jnp.exp(ldt)
    mag = jnp.exp(ar * dt)
    lam_r, lam_i = mag * jnp.cos(ai * dt), mag * jnp.sin(ai * dt)
    den = ar * ar + ai * ai
    nr = lam_r - 1.0
    coef_r = (nr * ar + lam_i * ai) / den
    coef_i = (lam_i * ar - nr * ai) / den
    return (lam_r, lam_i, coef_r[None] * br - coef_i[None] * bi, coef_r[None] * bi + coef_i[None] * br)


SEG_LEN = BLK // SCAN_SEG
N_STATE = GROUPS_PER_CHUNK * SSM_STATE
CHUNK_CH = GROUPS_PER_CHUNK * SSM_GROUP


def _scan_perm():
    r = np.arange(BLK)
    t = (r % SCAN_SEG) * SEG_LEN + r // SCAN_SEG
    pm = np.zeros((BLK, BLK), np.float32)
    pm[r, t] = 1.0
    return jnp.asarray(pm, bf16), jnp.asarray(pm.T, bf16)


def _row_perm(x, pm, name, *, out_dtype, split, add=None, add_rows=None):
    n, c = x.shape
    add_tiles = None if add is None else add_rows // BLK

    def body(*refs):
        x_ref, pm_ref = refs[0], refs[1]
        o_ref = refs[-1]
        xv = x_ref[...]
        if split:
            out = _unpermute(pm_ref[...], xv.astype(f32))
        else:
            out = jnp.dot(pm_ref[...], xv.astype(bf16), preferred_element_type=f32)
        if add is not None:
            out = out + jnp.where(pl.program_id(0) < add_tiles, refs[2][...], 0.0)
        o_ref[...] = out.astype(o_ref.dtype)

    row = pl.BlockSpec((BLK, c), lambda i: (i, 0))
    in_specs, args = [row, pl.BlockSpec((BLK, BLK), lambda i: (0, 0))], [x, pm]
    if add is not None:
        in_specs.append(pl.BlockSpec((BLK, c), lambda i: (jnp.minimum(i, add_tiles - 1), 0)))
        args.append(add)
    return pl.pallas_call(
        body, name=name, grid=(n // BLK,), in_specs=in_specs, out_specs=row,
        out_shape=jax.ShapeDtypeStruct((n, c), out_dtype), compiler_params=_params(("parallel",)),
    )(*args)


def _glu_pre(u, yf, yr, dvec, pmt, n_x, name):
    d = u.shape[1]

    def body(u_ref, yf_ref, yr_ref, d_ref, pmt_ref, o_ref):
        y = d_ref[...] * u_ref[...] + _unpermute(pmt_ref[...], yf_ref[...] + yr_ref[...])
        o_ref[...] = jax.nn.gelu(y).astype(bf16)

    row = pl.BlockSpec((BLK, d), lambda i: (i, 0))
    return pl.pallas_call(
        body, name=name, grid=(n_x // BLK,),
        in_specs=[row, row, row, pl.BlockSpec((1, d), lambda i: (0, 0)), pl.BlockSpec((BLK, BLK), lambda i: (0, 0))],
        out_specs=row, out_shape=jax.ShapeDtypeStruct((n_x, d), bf16), compiler_params=_params(("parallel",)),
    )(u, yf, yr, dvec, pmt)


def _glu_pre_bwd(u, yf, yr, dvec, pm, pmt, dgy, n_x, name):
    d = u.shape[1]

    def body(u_ref, yf_ref, yr_ref, d_ref, pm_ref, pmt_ref, ct_ref, dyp_ref, dud_ref, dd_ref):
        uv = u_ref[...]
        y = d_ref[...] * uv + _unpermute(pmt_ref[...], yf_ref[...] + yr_ref[...])
        _, vjp = jax.vjp(jax.nn.gelu, y)
        dy = vjp(ct_ref[...])[0]
        dyp_ref[...] = jnp.dot(pm_ref[...], dy.astype(bf16), preferred_element_type=f32).astype(bf16)
        dud_ref[...] = d_ref[...] * dy
        part = jnp.sum(dy * uv, axis=0, keepdims=True)

        @pl.when(pl.program_id(0) == 0)
        def _():
            dd_ref[...] = part

        @pl.when(pl.program_id(0) > 0)
        def _():
            dd_ref[...] += part

    row = pl.BlockSpec((BLK, d), lambda i: (i, 0))
    vec = pl.BlockSpec((1, d), lambda i: (0, 0))
    sq = pl.BlockSpec((BLK, BLK), lambda i: (0, 0))
    return pl.pallas_call(
        body, name=name, grid=(n_x // BLK,), in_specs=[row, row, row, vec, sq, sq, row],
        out_specs=[row, row, vec],
        out_shape=[jax.ShapeDtypeStruct((n_x, d), bf16), jax.ShapeDtypeStruct((n_x, d), f32),
                   jax.ShapeDtypeStruct((1, d), f32)],
        compiler_params=_params(("arbitrary",)),
    )(u, yf, yr, dvec, pm, pmt, dgy)


def _block_order(n_xb, n_blk, reverse):
    n_cb = n_blk - n_xb
    if reverse:
        return lambda cc: jnp.where(cc < n_cb, n_blk - 1 - cc, n_xb - 1 - (cc - n_cb))
    return lambda cc: jnp.where(cc < n_cb, n_xb + cc, cc - n_cb)


def _unpermute(pmt, y):
    hi = y.astype(bf16)
    lo = (y - hi.astype(f32)).astype(bf16)
    return jnp.dot(pmt, hi, preferred_element_type=f32) + jnp.dot(pmt, lo, preferred_element_type=f32)


def _lam_pow(lr, li):
    for _ in range(int(math.log2(SEG_LEN))):
        lr, li = lr * lr - li * li, 2.0 * lr * li
    return lr, li


def _s5_fwd(up, lam, bblk, cblk, n_x, reverse, name, rider=None):
    n_tot, d = up.shape
    nq, n_blk, n_xb = d // CHUNK_CH, n_tot // BLK, n_x // BLK
    order = _block_order(n_xb, n_blk, reverse)
    ns = N_STATE
    seg_order = range(SCAN_SEG - 1, -1, -1) if reverse else range(SCAN_SEG)

    def body(u_ref, lam_ref, b_ref, c_ref, y_ref, cp_ref, bu_ref, st_ref, carry_ref):
        cc = pl.program_id(1)

        @pl.when(cc == 0)
        def _():
            carry_ref[...] = jnp.zeros_like(carry_ref)

        up_v = u_ref[...]
        bu_ref[0] = jnp.dot(up_v, b_ref[0], preferred_element_type=f32)
        bu_ref[1] = jnp.dot(up_v, b_ref[1], preferred_element_type=f32)
        lr, li = lam_ref[0:1, :], lam_ref[1:2, :]
        lrb, lib = jnp.broadcast_to(lr, (SCAN_SEG, ns)), jnp.broadcast_to(li, (SCAN_SEG, ns))

        def step(p, s, store):
            sr, si = s
            j = SEG_LEN - 1 - p if reverse else p
            off = pl.multiple_of(j * SCAN_SEG, SCAN_SEG)
            nsr = lrb * sr - lib * si + bu_ref[0, pl.ds(off, SCAN_SEG), :]
            nsi = lrb * si + lib * sr + bu_ref[1, pl.ds(off, SCAN_SEG), :]
            if store:
                st_ref[0, pl.ds(off, SCAN_SEG), :] = nsr
                st_ref[1, pl.ds(off, SCAN_SEG), :] = nsi
            return nsr, nsi

        zero = jnp.zeros((SCAN_SEG, ns), f32)
        er, ei = lax.fori_loop(0, SEG_LEN, lambda j, s: step(j, s, False), (zero, zero))
        pr, pi = _lam_pow(lr, li)
        cr, ci = carry_ref[0, 0:1, :], carry_ref[1, 0:1, :]
        rows_r, rows_i = [None] * SCAN_SEG, [None] * SCAN_SEG
        for k in seg_order:
            rows_r[k], rows_i[k] = cr, ci
            cr, ci = pr * cr - pi * ci + er[k:k + 1], pr * ci + pi * cr + ei[k:k + 1]
        carry_ref[0] = jnp.broadcast_to(cr, (SCAN_SEG, ns))
        carry_ref[1] = jnp.broadcast_to(ci, (SCAN_SEG, ns))
        cpr, cpi = jnp.concatenate(rows_r, axis=0), jnp.concatenate(rows_i, axis=0)
        cp_ref[0] = cpr
        cp_ref[1] = cpi
        lax.fori_loop(0, SEG_LEN, lambda j, s: step(j, s, True), (cpr, cpi))
        y_ref[...] = (jnp.dot(st_ref[0].astype(bf16), c_ref[0], preferred_element_type=f32)
                      - jnp.dot(st_ref[1].astype(bf16), c_ref[1], preferred_element_type=f32))

    return _call(
        body, (up, lam, bblk, cblk), name=name, grid=(nq, n_blk),
        in_specs=[pl.BlockSpec((BLK, CHUNK_CH), lambda q,cc: (order(cc), q)),
                  pl.BlockSpec((None, 2, ns), lambda q, cc: (q, 0, 0)),
                  pl.BlockSpec((None, 2, CHUNK_CH, ns), lambda q,cc: (q, 0, 0, 0)),
                  pl.BlockSpec((None, 2, ns, CHUNK_CH), lambda q,cc: (q, 0, 0, 0))],
        out_specs=[pl.BlockSpec((BLK, CHUNK_CH), lambda q,cc: (order(cc), q)),
                   pl.BlockSpec((None, None, 2, SCAN_SEG, ns), lambda q, cc: (q, cc, 0, 0, 0))],
        out_shape=[jax.ShapeDtypeStruct((n_tot, d), f32),
                   jax.ShapeDtypeStruct((nq, n_blk, 2, SCAN_SEG, ns), f32)],
        scratch_shapes=[pltpu.VMEM((2, BLK, ns), f32), pltpu.VMEM((2, BLK, ns), f32),
                        pltpu.VMEM((2, SCAN_SEG, ns), f32)],
        sem=("parallel", "arbitrary"), rider=rider)


def _s5_bwd(u, dy, cprev, lam, bblk, cblk, bblk_t, cblk_t, n_x, reverse, name, add=None, rider=None):
    n_tot, d = u.shape
    nq, n_blk, n_xb = d // CHUNK_CH, n_tot // BLK, n_x // BLK
    order = _block_order(n_xb, n_blk, reverse)
    ns = N_STATE
    blk_of = lambda step: order(n_blk - 1 - step)
    has_add = add is not None
    jof = (lambda p: SEG_LEN - 1 - p) if reverse else (lambda p: p)
    adj_seg_order = range(SCAN_SEG) if reverse else range(SCAN_SEG - 1, -1, -1)

    def body(*refs):
        u_ref, dy_ref, cp_ref, lam_ref, b_ref, c_ref, bt_ref, ct_ref = refs[:8]
        rest = refs[8:]
        if has_add:
            add_ref, rest = rest[0], rest[1:]
        du_ref, dlam_ref, db_ref, dc_ref, bu_ref, st_ref, ds_ref, acarry_ref = rest
        step_id = pl.program_id(1)
        is_x = blk_of(step_id) < n_xb

        @pl.when(step_id == 0)
        def _():
            acarry_ref[...] = jnp.zeros_like(acarry_ref)
            dlam_ref[...] = jnp.zeros_like(dlam_ref)
            db_ref[...] = jnp.zeros_like(db_ref)
            dc_ref[...] = jnp.zeros_like(dc_ref)

        up = u_ref[...]
        bu_ref[0] = jnp.dot(up, b_ref[0], preferred_element_type=f32)
        bu_ref[1] = jnp.dot(up, b_ref[1], preferred_element_type=f32)
        lr, li = lam_ref[0:1, :], lam_ref[1:2, :]
        lrb, lib = jnp.broadcast_to(lr, (SCAN_SEG, ns)), jnp.broadcast_to(li, (SCAN_SEG, ns))
        cpr, cpi = cp_ref[0], cp_ref[1]

        def fstep(p, s):
            sr, si = s
            off = pl.multiple_of(jof(p) * SCAN_SEG, SCAN_SEG)
            nsr = lrb * sr - lib * si + bu_ref[0, pl.ds(off, SCAN_SEG), :]
            nsi = lrb * si + lib * sr + bu_ref[1, pl.ds(off, SCAN_SEG), :]
            st_ref[0, pl.ds(off, SCAN_SEG), :] = nsr
            st_ref[1, pl.ds(off, SCAN_SEG), :] = nsi
            return nsr, nsi

        lax.fori_loop(0, SEG_LEN, fstep, (cpr, cpi))

        dyp = jnp.where(is_x, dy_ref[...], jnp.zeros_like(dy_ref))
        ds_ref[0] = jnp.dot(dyp, ct_ref[0], preferred_element_type=f32)
        ds_ref[1] = -jnp.dot(dyp, ct_ref[1], preferred_element_type=f32)

        def adj(p, a):
            ar, ai = a
            off = pl.multiple_of(jof(p) * SCAN_SEG, SCAN_SEG)
            nar = ds_ref[0, pl.ds(off, SCAN_SEG), :] + lrb * ar + lib * ai
            nai = ds_ref[1, pl.ds(off, SCAN_SEG), :] - lib * ar + lrb * ai
            return nar, nai

        zero = jnp.zeros((SCAN_SEG, ns), f32)
        er, ei = lax.fori_loop(0, SEG_LEN, lambda jj, a: adj(SEG_LEN - 1 - jj, a), (zero, zero))
        pr, pi = _lam_pow(lr, li)
        nr_, ni_ = acarry_ref[0, 0:1, :], acarry_ref[1, 0:1, :]
        rows_r, rows_i = [None] * SCAN_SEG, [None] * SCAN_SEG
        for k in adj_seg_order:
            rows_r[k], rows_i[k] = nr_, ni_
            nr_, ni_ = er[k:k + 1] + pr * nr_ + pi * ni_, ei[k:k + 1] + pr * ni_ - pi * nr_
        acarry_ref[0] = jnp.broadcast_to(nr_, (SCAN_SEG, ns))
        acarry_ref[1] = jnp.broadcast_to(ni_, (SCAN_SEG, ns))
        an_r, an_i = jnp.concatenate(rows_r, axis=0), jnp.concatenate(rows_i, axis=0)

        def adj2(jj, carry):
            ar, ai, glr, gli = carry
            p = SEG_LEN - 1 - jj
            nar, nai = adj(p, (ar, ai))
            off = pl.multiple_of(jof(p) * SCAN_SEG, SCAN_SEG)
            ds_ref[0, pl.ds(off, SCAN_SEG), :] = nar
            ds_ref[1, pl.ds(off, SCAN_SEG), :] = nai
            poff = pl.multiple_of(jof(p - 1) * SCAN_SEG, SCAN_SEG)
            spr, spi = st_ref[0, pl.ds(poff, SCAN_SEG), :], st_ref[1, pl.ds(poff, SCAN_SEG), :]
            return nar, nai, glr + nar * spr + nai * spi, gli - nar * spi + nai * spr

        ar, ai, glr, gli = lax.fori_loop(0, SEG_LEN - 1, adj2, (an_r, an_i, zero, zero))
        nar, nai = adj(0, (ar, ai))
        first = jof(0) * SCAN_SEG
        ds_ref[0, first:first + SCAN_SEG, :] = nar
        ds_ref[1, first:first + SCAN_SEG, :] = nai
        glr = glr + nar * cpr + nai * cpi
        gli = gli - nar * cpi + nai * cpr
        dlam_ref[0:1, :] += jnp.sum(glr, axis=0, keepdims=True)
        dlam_ref[1:2, :] += jnp.sum(gli, axis=0, keepdims=True)

        a_r, a_i = ds_ref[0].astype(bf16), ds_ref[1].astype(bf16)
        du = jnp.dot(a_r, bt_ref[0], preferred_element_type=f32) + jnp.dot(a_i, bt_ref[1],
                                                                        preferred_element_type=f32)
        if has_add:
            du = du + add_ref[...]
        du_ref[...] = du
        db_ref[0] += lax.dot_general(up, a_r, TN, preferred_element_type=f32)
        db_ref[1] += lax.dot_general(up, a_i, TN, preferred_element_type=f32)
        dc_ref[0] += lax.dot_general(st_ref[0].astype(bf16), dyp, TN, preferred_element_type=f32)
        dc_ref[1] -= lax.dot_general(st_ref[1].astype(bf16), dyp, TN, preferred_element_type=f32)

    tok = pl.BlockSpec((BLK, CHUNK_CH), lambda q,s: (blk_of(s), q))
    in_specs = [tok, pl.BlockSpec((BLK, CHUNK_CH), lambda q,s: (jnp.minimum(blk_of(s), n_xb - 1), q)),
                pl.BlockSpec((None, None, 2, SCAN_SEG, ns), lambda q, s: (q, n_blk - 1 - s, 0, 0, 0)),
                pl.BlockSpec((None, 2, ns), lambda q, s: (q, 0, 0)),
                pl.BlockSpec((None, 2, CHUNK_CH, ns), lambda q,s: (q, 0, 0, 0)),
                pl.BlockSpec((None, 2, ns, CHUNK_CH), lambda q,s: (q, 0, 0, 0)),
                pl.BlockSpec((None, 2, ns, CHUNK_CH), lambda q,s: (q, 0, 0, 0)),
                pl.BlockSpec((None, 2, CHUNK_CH, ns), lambda q,s: (q, 0, 0, 0))]
    args = [u, dy, cprev, lam, bblk, cblk, bblk_t, cblk_t]
    if has_add:
        in_specs.append(tok)
        args.append(add)
    return _call(
        body, args, name=name, grid=(nq, n_blk), in_specs=in_specs,
        out_specs=[tok, pl.BlockSpec((None, 2, ns), lambda q, s: (q, 0, 0)),
                   pl.BlockSpec((None, 2, CHUNK_CH, ns), lambda q,s: (q, 0, 0, 0)),
                   pl.BlockSpec((None, 2, ns, CHUNK_CH), lambda q,s: (q, 0, 0, 0))],
        out_shape=[jax.ShapeDtypeStruct((n_tot, d), f32), jax.ShapeDtypeStruct((nq, 2, ns), f32),
                   jax.ShapeDtypeStruct((nq, 2, CHUNK_CH, ns), f32),
                   jax.ShapeDtypeStruct((nq, 2, ns, CHUNK_CH), f32)],
        scratch_shapes=[pltpu.VMEM((2, BLK, ns), f32), pltpu.VMEM((2, BLK, ns), f32),
                        pltpu.VMEM((2, BLK, ns), f32), pltpu.VMEM((2, SCAN_SEG, ns), f32)],
        sem=("parallel", "arbitrary"), rider=rider)


def _exchange(src, gather, name):
    return _run_exchange(_Exchange(src, gather), name)


def _sum0(x, name):
    n, r, c = x.shape
    tr = _pick(r, (512, 256, 128, 64, 32, 16, 8))

    def body(x_ref, o_ref):
        acc = x_ref[0].astype(f32)
        for k in range(1, n):
            acc = acc + x_ref[k].astype(f32)
        o_ref[...] = acc

    return pl.pallas_call(
        body, name=name, grid=(r // tr,), in_specs=[pl.BlockSpec((n, tr, c), lambda i: (0, i, 0))],
        out_specs=pl.BlockSpec((tr, c), lambda i: (i, 0)), out_shape=jax.ShapeDtypeStruct((r, c), f32),
        compiler_params=_params(("parallel",)),
    )(x)


LANES = 1024


def _pack(parts, dtype):
    flat = jnp.concatenate([p.astype(dtype).reshape(-1) for p in parts])
    pad = (-flat.shape[0]) % (16 * LANES)
    if pad:
        flat = jnp.concatenate([flat, jnp.zeros((pad,), dtype)])
    return flat.reshape(-1, LANES)


def _unpack(flat, shapes):
    out, off = [], 0
    for s in shapes:
        size = int(np.prod(s))
        out.append(flat[off:off + size].reshape(s))
        off += size
    return out


def _vec(a):
    return a.reshape(-1, 1, a.shape[-1])


def _local_step(xa, tgt, mods, wts, sm, n_x):
    n_tot, d = xa.shape
    kw = dict(nrows=n_tot, n_x_rows=n_x)
    kx = dict(nrows=n_x, n_x_rows=n_x)
    mv = lambda l, k: mods[l, :, k][:, None, :]
    mx = lambda l, k: mods[l, 0:1, k][:, None, :]
    nmix, nffn = sm["norm_mix"], sm["norm_ffn"]
    cos, sin = _rope_tables(n_x, n_tot - n_x)
    cfg_a, cfg_b = _AttnCfg("na", n_x, n_tot), _AttnCfg("swa", n_x, n_tot)

    h1 = _rowwise(_f_normmod, [xa], [_vec(nmix[0:1]), mv(0, 1), mv(0, 0)], [(d, bf16)], name="l0_norm_mix", **kw)[0]
    qkv32 = _mm(h1, wts("w_in"), "nn", f32, "l0_in_proj", comm=wts)
    qkv = _qkv_post(qkv32, cos, sin, "l0_qkv_post")
    rpb2 = jnp.zeros((128, 128), f32).at[:NA_HEADS * 15, :31].set(sm["rpb"].reshape(NA_HEADS * 15, 31))
    basis = jnp.zeros((128, GRID_W * GRID_W), f32).at[:31].set(_toeplitz_basis())
    tz = _small_dot(rpb2, basis, "rpb_expand")[:NA_HEADS * 15].reshape(NA_HEADS, 15, GRID_W, GRID_W)
    bias = jnp.stack([tz[:, 3 - a:15 - a] for a in range(4)], axis=1).transpose(0, 1, 3, 2, 4).reshape(
        NA_HEADS, BLK, 3 * BLK)
    sink_rows = jnp.repeat(sm["sink"].reshape(NB_KV_HEADS, NB_GROUP, 1), BLK, axis=1).reshape(
        NB_KV_HEADS, NB_GROUP * BLK, 1)
    oa, lse_a = _attn_fwd(qkv, cfg_a, bias, "l0_na_fwd", rider=wts.take(CARRIER_US["na_fwd"]))
    ob, lse_b = _attn_fwd(qkv, cfg_b, sink_rows, "l0_swa_fwd", rider=wts.take(CARRIER_US["swa_fwd"]))
    w_out_a, w_out_b = wts("w_out")[:A_W], wts("w_out")[A_W:]
    y1, xb = _mm(oa, w_out_a, "nn", f32, "l0_out_proj", a2=ob, b2=w_out_b, comm=wts, gated=(xa, mv(0, 2), n_x))

    def ffn_fwd(xin, l, vec_of, kk, tag):
        h = _rowwise(_f_normmod, [xin], [_vec(nffn[l:l + 1]), vec_of(l, 4), vec_of(l, 3)], [(d, bf16)],
                     name=tag + "_norm_ffn", **kk)[0]
        a, b, u = _ffn_up(h, wts("w1", l), wts("w3", l), tag + "_ffn_up", comm=wts)
        fo = _mm(u, wts("w2", l), "nn", f32, tag + "_ffn_w2", comm=wts)
        xo = _rowwise(_f_gated_add, [xin, fo], [vec_of(l, 5)], [(d, f32)], name=tag + "_res_ffn", **kk)[0]
        return xo, (h, a, b, u, fo)

    def ffn_bwd(d_out, xin, saved, l, vec_of, kk, tag):
        h, a, b, u, fo = saved
        rows = kk["nrows"]
        dfo, dg2 = _rowwise_bwd(_f_gated_add, [xin, fo], [vec_of(l, 5)], [d_out], {1: bf16},
                                name=tag + "_res_ffn_bwd", **kk)
        wts.grad("w2", l, _mm(u[:rows], dfo, "tn", f32, tag + "_ffn_w2_dw", comm=wts))
        da, db = _ffn_down_bwd(dfo, wts("w2", l), a, b, tag + "_ffn_down_bwd", comm=wts)
        dh = _mm(da, wts("w1", l), "nt", f32, tag + "_ffn_w13_dx", a2=db, b2=wts("w3", l), comm=wts)
        wts.grad("w1", l, _mm(h[:rows], da, "tn", f32, tag + "_ffn_w1_dw", comm=wts))
        wts.grad("w3", l, _mm(h[:rows], db, "tn", f32, tag + "_ffn_w3_dw", comm=wts))
        dxin, dnw, dsc, dsh = _rowwise_bwd(
            _f_normmod, [xin], [_vec(nffn[l:l + 1]), vec_of(l, 4), vec_of(l, 3)], [dh], {0: f32},
            name=tag + "_norm_ffn_bwd", residual=d_out, **kk)
        return dxin, dnw, dsc, dsh, dg2

    xc, ffn0 = ffn_fwd(xb, 0, mv, kw, "l0")

    hs = _rowwise(_f_normmod, [xc], [_vec(nmix[1:2]), mv(1, 1), mv(1, 0)], [(d, f32)], name="l1_norm_mix", **kw)[0]
    g2n = sm["a_re"].shape[1]
    nq = d // CHUNK_CH
    ar2, ai2 = sm["a_re"].reshape(2 * g2n, SSM_STATE), sm["a_im"].reshape(2 * g2n, SSM_STATE)
    ldt2 = sm["log_dt"].reshape(2 * g2n, 1)
    bt_re = sm["b_re"].transpose(3, 0, 1, 2).reshape(SSM_GROUP, 2 * g2n, SSM_STATE)
    bt_im = sm["b_im"].transpose(3, 0, 1, 2).reshape(SSM_GROUP, 2 * g2n, SSM_STATE)
    disc_in = [ar2, ai2, ldt2, bt_re, bt_im]
    lam_r, lam_i, bbar_r, bbar_i = _whole(
        _f_discretise, disc_in,
        [((2 * g2n, SSM_STATE), f32)] * 2 + [((SSM_GROUP, 2 * g2n, SSM_STATE), f32)] * 2, "s5_discretise")
    eye = jnp.eye(GROUPS_PER_CHUNK, dtype=f32)
    eye6 = eye.astype(bf16)[None, None, :, None, :, None]

    def blockdiag_b(bbar):
        t = bbar.astype(bf16).reshape(SSM_GROUP, 2, nq, GROUPS_PER_CHUNK, SSM_STATE).transpose(1, 2, 3, 0, 4)
        return (t[:, :, :, :, None, :] * eye6).reshape(2, nq, CHUNK_CH, N_STATE)

    def blockdiag_c(cw):
        t = cw.astype(bf16).reshape(2, nq, GROUPS_PER_CHUNK, SSM_GROUP, SSM_STATE).transpose(0, 1, 2, 4, 3)
        return (t[:, :, :, :, None, :] * eye6).reshape(2, nq, N_STATE, CHUNK_CH)

    lam = jnp.stack([lam_r.reshape(2, nq, N_STATE), lam_i.reshape(2, nq, N_STATE)], axis=2)
    bblk = jnp.stack([blockdiag_b(bbar_r), blockdiag_b(bbar_i)], axis=2)
    cblk = jnp.stack([blockdiag_c(sm["c_re"]), blockdiag_c(sm["c_im"])], axis=2)
    bblk16, cblk16 = bblk, cblk
    bblk_t, cblk_t = bblk16.transpose(0, 1, 2, 4, 3), cblk16.transpose(0, 1, 2, 4, 3)
    pm, pmt = _scan_perm()
    hs_p = _row_perm(hs, pm, "l1_s5_perm", out_dtype=bf16, split=False)
    ys, cps = [], []
    for dr in range(2):
        yd, cp = _s5_fwd(hs_p, lam[dr], bblk16[dr], cblk16[dr], n_x, dr == 1, "l1_s5_fwd%d" % dr,
                         rider=wts.take(CARRIER_US["s5_fwd"]))
        ys.append(yd)
        cps.append(cp)
    dvec = sm["ssm_d"].reshape(1, d)
    gy = _glu_pre(hs, ys[0], ys[1], dvec, pmt, n_x, "l1_glu_pre")
    zv = _mm(gy, wts("w_glu_v"), "nn", f32, "l1_glu_val", comm=wts)
    zg = _mm(gy, wts("w_glu_g"), "nn", f32, "l1_glu_gate", comm=wts)
    bv, bg = sm["b_glu"][:d].reshape(1, 1, d), sm["b_glu"][d:].reshape(1, 1, d)
    xd = _rowwise(_f_glu_post, [xc, zv, zg], [mx(1, 2), bv, bg], [(d, f32)], name="l1_glu_post", **kx)[0]
    xe, ffn1 = ffn_fwd(xd, 1, mx, kx, "l1")

    d_xe, d_nfinal, loss_blk = _loss_head(xe, tgt, sm["norm_final"].reshape(1, d), nrows=n_x, name="loss_head")
    d_xd, dnffn1, dsc2_1, dsh2_1, dg2_1 = ffn_bwd(d_xe, xd, ffn1, 1, mx, kx, "l1")
    dzv, dzg, dg1_1, dbv, dbg = _rowwise_bwd(_f_glu_post, [xc, zv, zg], [mx(1, 2), bv, bg], [d_xd],
                                             {1: bf16, 2: bf16}, name="l1_glu_post_bwd", **kx)
    dgy = _mm(dzv, wts("w_glu_v"), "nt", f32, "l1_glu_dx", a2=dzg, b2=wts("w_glu_g"), comm=wts)
    dwglu_v = _mm(gy, dzv, "tn", f32, "l1_glu_val_dw", comm=wts)
    dwglu_g = _mm(gy, dzg, "tn", f32, "l1_glu_gate_dw", comm=wts)
    wts.grad("w_glu", 0, jnp.concatenate([dwglu_v, dwglu_g], axis=1))
    dy_p, du_skip, d_dvec = _glu_pre_bwd(hs, ys[0], ys[1], dvec, pm, pmt, dgy, n_x, "l1_glu_pre_bwd")
    du0, dlam0, db0, dc0 = _s5_bwd(hs_p, dy_p, cps[0], lam[0], bblk16[0], cblk16[0], bblk_t[0], cblk_t[0], n_x,
                                   False, "l1_s5_bwd0", rider=wts.take(CARRIER_US["s5_bwd"]))
    du1, dlam1, db1, dc1 = _s5_bwd(hs_p, dy_p, cps[1], lam[1], bblk16[1], cblk16[1], bblk_t[1], cblk_t[1], n_x,
                                   True, "l1_s5_bwd1", add=du0, rider=wts.take(CARRIER_US["s5_bwd"]))
    d_hs = _row_perm(du1, pmt, "l1_s5_unperm", out_dtype=f32, split=True, add=du_skip, add_rows=n_x)
    d_xc, dnmix1, dsc1_1, dsh1_1 = _rowwise_bwd(
        _f_normmod, [xc], [_vec(nmix[1:2]), mv(1, 1), mv(1, 0)], [d_hs], {0: f32},
        name="l1_norm_mix_bwd", residual=d_xd, residual_rows=n_x, **kw)
    dlam = jnp.stack([dlam0, dlam1])
    dlam_r, dlam_i = dlam[:, :, 0].reshape(2 * g2n, SSM_STATE), dlam[:, :, 1].reshape(2 * g2n, SSM_STATE)
    dbb = jnp.stack([db0, db1]).reshape(2, nq, 2, GROUPS_PER_CHUNK, SSM_GROUP, GROUPS_PER_CHUNK, SSM_STATE)
    eye7 = eye[None, None, None, :, None, :, None]
    dbbar = (dbb * eye7).sum(axis=5).transpose(2, 4, 0, 1, 3, 5).reshape(2, SSM_GROUP, 2 * g2n, SSM_STATE)
    dcc = jnp.stack([dc0, dc1]).reshape(2, nq, 2, GROUPS_PER_CHUNK, SSM_STATE, GROUPS_PER_CHUNK, SSM_GROUP)
    dcw = (dcc * eye7).sum(axis=5).transpose(2, 0, 1, 3, 5, 4).reshape(2, 2, g2n, SSM_GROUP, SSM_STATE)
    d_ar, d_ai, d_ldt, d_btr, d_bti = _whole_bwd(_f_discretise, disc_in, [dlam_r, dlam_i, dbbar[0], dbbar[1]],
                                                 "s5_discretise_bwd")
    to_b = lambda t: t.reshape(SSM_GROUP, 2, g2n, SSM_STATE).transpose(1, 2, 3, 0)
    wts.early(dict(
        loss=loss_blk[0, 0].reshape(1), a_re=d_ar.reshape(sm["a_re"].shape), a_im=d_ai.reshape(sm["a_im"].shape),
        log_dt=d_ldt.reshape(sm["log_dt"].shape), b_re=to_b(d_btr), b_im=to_b(d_bti), c_re=dcw[0], c_im=dcw[1],
        ssm_d=d_dvec.reshape(d), b_glu=jnp.concatenate([dbv.reshape(d), dbg.reshape(d)]),
        norm_final=d_nfinal.reshape(d)))

    d_xb, dnffn0, dsc2_0, dsh2_0, dg2_0 = ffn_bwd(d_xc, xb, ffn0, 0, mv, kw, "l0")
    dy1, dg1_0 = _rowwise_bwd(_f_gated_add, [xa, y1], [mv(0, 2)], [d_xb], {1: bf16}, name="l0_res_mix_bwd", **kw)
    d_o = _mm(dy1, wts("w_out"), "nt", bf16, "l0_out_proj_dx", comm=wts)
    wts.grad("w_out", 0, jnp.concatenate([_mm(oa, dy1, "tn", f32, "l0_out_proj_dw_a", comm=wts),
                                          _mm(ob, dy1, "tn", f32, "l0_out_proj_dw_b", comm=wts)], axis=0))
    dqa, _, dbias, dka, dva = _attn_bwd(qkv, oa, d_o, lse_a, cfg_a, bias, "l0_na_bwd",
                                          rider=wts.take(CARRIER_US["na_bwd"]))
    dqb, _, dsink_rows, dkb, dvb = _attn_bwd(qkv, ob, d_o, lse_b, cfg_b, sink_rows, "l0_swa_bwd",
                                               rider=wts.take(CARRIER_US["swa_bwd"]))
    d_qkv = _qkv_post_bwd([dqa, dka, dva, dqb, dkb, dvb], cos, sin, "l0_qkv_post_bwd")
    wts.grad("w_in", 0, _mm(h1, d_qkv, "tn", f32, "l0_in_proj_dw", comm=wts))
    dh1 = _mm(d_qkv, wts("w_in"), "nt", f32, "l0_in_proj_dx", comm=wts)
    d_xa, dnmix0, dsc1_0, dsh1_0 = _rowwise_bwd(
        _f_normmod, [xa], [_vec(nmix[0:1]), mv(0, 1), mv(0, 0)], [dh1], {0: f32},
        name="l0_norm_mix_bwd", residual=d_xb, latent_grads_only=True, **kw)
    dbias5 = dbias.reshape(NA_HEADS, 4, GRID_W, 12, GRID_W).transpose(0, 1, 3, 2, 4)
    dtz = sum(jnp.pad(dbias5[:, a], ((0, 0), (3 - a, a), (0, 0), (0, 0))) for a in range(4))
    dtz2 = jnp.zeros((128, GRID_W * GRID_W), f32).at[:NA_HEADS * 15].set(dtz.reshape(NA_HEADS * 15, -1))
    d_rpb = _small_dot(dtz2, basis.T, "rpb_expand_bwd")[:NA_HEADS * 15, :31].reshape(sm["rpb"].shape)
    d_sink = dsink_rows.reshape(NB_KV_HEADS * NB_GROUP, BLK).sum(axis=1)

    zero_c = jnp.zeros((1, 1, d), f32)
    both = lambda gx: jnp.concatenate([gx, zero_c], axis=0)
    dmods = jnp.stack([
        jnp.stack([dsh1_0, dsc1_0, dg1_0, dsh2_0, dsc2_0, dg2_0], axis=2),
        jnp.stack([dsh1_1, dsc1_1, both(dg1_1), both(dsh2_1), both(dsc2_1), both(dg2_1)], axis=2),
    ])[:, :, 0]
    late = dict(norm_mix=jnp.concatenate([dnmix0[0], dnmix1[0]]), norm_ffn=jnp.concatenate([dnffn0[0], dnffn1[0]]),
                rpb=d_rpb, sink=d_sink)
    return d_xa, late, dmods


WEIGHTS = ("c_ctx", "ada_w", "ada_b", "norm_mix", "norm_ffn", "ffn_w1", "ffn_w3", "ffn_w2", "attn_w_in",
           "attn_w_out", "attn_rpb", "attn_sink", "ssm_a_re", "ssm_a_im", "ssm_log_dt", "ssm_b_re", "ssm_b_im",
           "ssm_c_re", "ssm_c_im", "ssm_d", "ssm_w_glu", "ssm_b_glu", "norm_final")
SHARDED_BIG = ("ffn_w1", "ffn_w3", "ffn_w2", "attn_w_in", "attn_w_out", "ssm_w_glu")
BIG = ("ada_w",) + SHARDED_BIG
SMALL = tuple(n for n in WEIGHTS if n not in BIG)


BIG_SLABS = dict(w_in=("attn_w_in", "cols"), w_out=("attn_w_out", "rows"), w1=("ffn_w1", "cols"),
                 w3=("ffn_w3", "cols"), w2=("ffn_w2", "rows"), w_glu=("ssm_w_glu", "cols"))
GATHER_ORDER = (("w_in", 0), ("w_out", 0), ("w1", 0), ("w3", 0), ("w2", 0), ("w_glu", 0), ("w1", 1), ("w3", 1),
                ("w2", 1))


class _Weights(_Comm):
    def __init__(self, w, pack_early):
        super().__init__()
        self.w, self.pack_early = w, pack_early
        self.gathers = {key: self.post(w[BIG_SLABS[key[0]][0]][key[1]].astype(bf16), True) for key in GATHER_ORDER}
        self.full, self.scatters, self.early_gather = {}, {}, None

    def __call__(self, name, layer=0):
        key = ("w_glu", 0) if name.startswith("w_glu") else (name, layer)
        if key not in self.full:
            blk = self.finish(self.gathers[key])
            if BIG_SLABS[key[0]][1] == "cols":
                self.full[key] = blk.transpose(1, 0, 2).reshape(blk.shape[1], -1)
            else:
                self.full[key] = blk.reshape(-1, blk.shape[2])
        full = self.full[key]
        if name.startswith("w_glu"):
            half = full.shape[1] // 2
            return full[:, :half] if name == "w_glu_v" else full[:, half:]
        return full

    def grad(self, name, layer, dw):
        if BIG_SLABS[name][1] == "cols":
            blk = dw.reshape(dw.shape[0], N_DEV, -1).transpose(1, 0, 2)
        else:
            blk = dw.reshape(N_DEV, -1, dw.shape[1])
        self.scatters[(name, layer)] = self.post(blk.astype(bf16), False)

    def early(self, parts):
        self.early_shapes = {k: v.shape for k, v in parts.items()}
        self.early_gather = self.post(self.pack_early(parts), True)

    def reduced(self, name):
        short = next(k for k, v in BIG_SLABS.items() if v[0] == name)
        layers = [l for (n, l) in self.scatters if n == short]
        parts = [_sum0(self.finish(self.scatters[(short, l)]), "sum_%s%d" % (short, l)) for l in sorted(layers)]
        return jnp.stack(parts).reshape(self.w[name].shape)


def kernel(x, c, ctx, c_ctx, ada_w, ada_b, norm_mix, norm_ffn, ffn_w1, ffn_w3, ffn_w2, attn_w_in, attn_w_out, attn_rpb, attn_sink, ssm_a_re, ssm_a_im, ssm_log_dt, ssm_b_re, ssm_b_im, ssm_c_re, ssm_c_im, ssm_d, ssm_w_glu, ssm_b_glu, norm_final, loss_target, m_c_ctx, m_ada_w, m_ada_b, m_norm_mix, m_norm_ffn, m_ffn_w1, m_ffn_w3, m_ffn_w2, m_attn_w_in, m_attn_w_out, m_attn_rpb, m_attn_sink, m_ssm_a_re, m_ssm_a_im, m_ssm_log_dt, m_ssm_b_re, m_ssm_b_im, m_ssm_c_re, m_ssm_c_im, m_ssm_d, m_ssm_w_glu, m_ssm_b_glu, m_norm_final, v_c_ctx, v_ada_w, v_ada_b, v_norm_mix, v_norm_ffn, v_ffn_w1, v_ffn_w3, v_ffn_w2, v_attn_w_in, v_attn_w_out, v_attn_rpb, v_attn_sink, v_ssm_a_re, v_ssm_a_im, v_ssm_log_dt, v_ssm_b_re, v_ssm_b_im, v_ssm_c_re, v_ssm_c_im, v_ssm_d, v_ssm_w_glu, v_ssm_b_glu, v_norm_final):
    p = dict(locals())
    w = {n: p[n] for n in WEIGHTS}
    me = 4 * lax.axis_index("x") + 2 * lax.axis_index("y") + lax.axis_index("c")
    n_x, d = x.shape[1], x.shape[2]
    cols = ada_w.shape[2]
    d8 = d // N_DEV

    first = jnp.concatenate([c[0], ssm_d[0], ssm_b_glu[0]])[None]
    g0 = _exchange(first, True, "gather_vectors")[:, 0]
    c_all, d_full, bglu_full = g0[:, :d], g0[:, d:d + d8].reshape(d), g0[:, d + d8:].reshape(2 * d)
    cc = jnp.concatenate([c_all, c_ctx[None], jnp.zeros((16 - N_DEV - 1, d), f32)])
    sc_all = _whole(_f_silu, [cc], [((16, d), f32)], "silu_c")[0]
    my_cols = lambda a: lax.dynamic_slice_in_dim(a, me * cols, cols, axis=a.ndim - 1)
    mod_loc = jnp.stack([_mm(sc_all, ada_w[l], "nn", f32, "ada_mod%d" % l) for l in range(2)])
    mod_loc = mod_loc + my_cols(ada_b)[:, None, :]
    mg = _exchange(mod_loc.reshape(32, cols), True, "gather_mod")
    mod_all = mg.reshape(N_DEV, 2, 16, cols).transpose(1, 2, 0, 3).reshape(2, 16, N_DEV * cols)
    mod_x = lax.dynamic_index_in_dim(mod_all, me, axis=1, keepdims=False)
    mods = jnp.stack([mod_x, mod_all[:, N_DEV]], axis=1).reshape(2, 2, 6, d)

    wts = _Weights(w, lambda parts: _pack(list(parts.values()), f32))
    sm = dict(norm_mix=norm_mix, norm_ffn=norm_ffn, rpb=attn_rpb[0], sink=attn_sink[0], a_re=ssm_a_re[0],
              a_im=ssm_a_im[0], log_dt=ssm_log_dt[0], b_re=ssm_b_re[0], b_im=ssm_b_im[0], c_re=ssm_c_re[0],
              c_im=ssm_c_im[0], ssm_d=d_full, b_glu=bglu_full, norm_final=norm_final)

    xa = jnp.concatenate([x[0], ctx[0]], axis=0)
    d_xa, late, dmods = _local_step(xa, loss_target[0], mods, wts, sm, n_x)

    late_names = ("norm_mix", "norm_ffn", "rpb", "sink")
    parts = [late[n] for n in late_names] + [dmods[:, 0].reshape(2, 6 * d), dmods[:, 1].reshape(2, 6 * d)]
    shapes = [q.shape for q in parts]
    late_gather = wts.post(_pack(parts, f32), True)
    grads = {n: wts.reduced(n) for n in SHARDED_BIG}
    sg = wts.finish(late_gather)
    tot = _unpack(_sum0(sg, "sum_small").reshape(-1), shapes)
    early = _unpack(_sum0(wts.finish(wts.early_gather), "sum_small_early").reshape(-1),
                    list(wts.early_shapes.values()))
    ts = dict(zip(late_names, tot[:len(late_names)]), **dict(zip(wts.early_shapes, early)))
    loss = ts["loss"][0]
    tot_dmod_x, tot_dmod_c = tot[-2], tot[-1]
    off_x = sum(int(np.prod(s)) for s in shapes[:-2])
    dmod_x_all = sg.reshape(N_DEV, -1)[:, off_x:off_x + 12 * d].reshape(N_DEV, 2, 6 * d)
    dm = jnp.concatenate([dmod_x_all.transpose(1, 0, 2), tot_dmod_c[:, None, :],
                          jnp.zeros((2, 16 - N_DEV - 1, 6 * d), f32)], axis=1)
    dm_loc = my_cols(dm)
    grads["ada_w"] = jnp.stack([_mm(sc_all, dm_loc[l], "tn", f32, "ada_w_grad%d" % l) for l in range(2)])
    grads["ada_b"] = tot_dmod_x + tot_dmod_c
    dsc_part = _mm(dm_loc[0], ada_w[0], "nt", f32, "silu_c_grad", a2=dm_loc[1], b2=ada_w[1])[N_DEV:N_DEV + 1]
    dsc = _sum0(_exchange(dsc_part, True, "gather_cctx"), "sum_cctx")
    grads["c_ctx"] = _whole_bwd(_f_silu, [c_ctx[None]], [dsc], "silu_cctx_bwd")[0][0]
    grads.update(norm_mix=ts["norm_mix"], norm_ffn=ts["norm_ffn"], attn_rpb=ts["rpb"][None],
                 attn_sink=ts["sink"][None], ssm_a_re=ts["a_re"][None], ssm_a_im=ts["a_im"][None],
                 ssm_log_dt=ts["log_dt"][None], ssm_b_re=ts["b_re"][None], ssm_b_im=ts["b_im"][None],
                 ssm_c_re=ts["c_re"][None], ssm_c_im=ts["c_im"][None],
                 ssm_d=lax.dynamic_slice_in_dim(ts["ssm_d"], me * d8, d8)[None],
                 ssm_b_glu=lax.dynamic_slice_in_dim(ts["b_glu"], me * 2 * d8, 2 * d8)[None],
                 norm_final=ts["norm_final"])

    delta, new_m, new_v = {}, {}, {}
    for n in BIG:
        two_d = lambda a: a.reshape(-1, a.shape[-1])
        dl, m2, v2 = _adamw(two_d(w[n]), two_d(grads[n]), two_d(p["m_" + n]), two_d(p["v_" + n]), "adamw_" + n)
        delta[n], new_m[n], new_v[n] = (t.reshape(w[n].shape) for t in (dl, m2, v2))
    sshapes = [w[n].shape for n in SMALL]
    packs = [_pack([src[n] for n in SMALL], f32) for src in
             (w, grads, {n: p["m_" + n] for n in SMALL}, {n: p["v_" + n] for n in SMALL})]
    for store, t in zip((delta, new_m, new_v), _adamw(*packs, "adamw_small")):
        store.update(zip(SMALL, _unpack(t.reshape(-1), sshapes)))

    return (loss, d_xa[None], *[grads[n] for n in WEIGHTS], *[delta[n] for n in WEIGHTS],
            *[new_m[n] for n in WEIGHTS], *[new_v[n] for n in WEIGHTS])
```

```python
import functools
import math

import numpy as np
import jax
import jax.numpy as jnp
from jax import lax
from jax.experimental import pallas as pl
from jax.experimental.pallas import tpu as pltpu

f32, bf16 = jnp.float32, jnp.bfloat16

HEAD_DIM = 128
GRID_W = 64
NA_HEADS = 8
NB_Q_HEADS = 8
NB_KV_HEADS = 2
NB_GROUP = NB_Q_HEADS // NB_KV_HEADS
NA_ROWS = 8
NA_COLS = 16
SW_RADIUS = 128
ROPE_BASE = 10000.0
SSM_GROUP = 16
SSM_STATE = 64
EPS = 1e-6
NEG_INF = -1e30
ADAM_LR, ADAM_B1, ADAM_B2, ADAM_EPS, ADAM_WD, ADAM_STEP = 0.001, 0.9, 0.999, 1e-08, 0.01, 10

N_DEV = 8
BLK = 256
SCAN_SEG = 8
GROUPS_PER_CHUNK = 16
V7X_VMEM_LIMIT = 56 * 2 ** 20
MM_VMEM_BUDGET = 36 * 2 ** 20
MXU_FLOPS_PER_US = 7e8
CARRIER_US = dict(na_fwd=300, swa_fwd=280, na_bwd=500, swa_bwd=300, s5_fwd=540, s5_bwd=1100)

NT = (((1,), (1,)), ((), ()))
TN = (((0,), (0,)), ((), ()))


def _params(sem, side_effects=False):
    return pltpu.CompilerParams(dimension_semantics=sem, vmem_limit_bytes=V7X_VMEM_LIMIT,
                                has_side_effects=side_effects)


def _pick(n, cands):
    for c in cands:
        if n % c == 0:
            return c
    return n


MESH = pl.DeviceIdType.MESH
MIN_CHUNK_BYTES = 512 * 1024
EXCHANGE_BYTES_PER_US = 74e3
RIDER_OVERHANG = 1.1


def _peers():
    x, y, c = lax.axis_index("x"), lax.axis_index("y"), lax.axis_index("c")
    return 4 * x + 2 * y + c, lambda o: (x ^ (o >> 2), y ^ ((o >> 1) & 1), c ^ (o & 1))


class _Exchange:
    def __init__(self, src, gather):
        self.src, self.gather, self.result = src, gather, None
        self.out_shape = (N_DEV,) + src.shape if gather else src.shape
        slab = self.out_shape[1:]
        slab_bytes = int(np.prod(slab)) * src.dtype.itemsize
        self.nch = next((k for k in (4, 2) if slab[0] % (16 * k) == 0 and slab_bytes // k >= MIN_CHUNK_BYTES), 1)
        self.step = slab[0] // self.nch
        self.cost_us = (N_DEV - 1) * slab_bytes / EXCHANGE_BYTES_PER_US

    def scratch(self):
        return [pltpu.SemaphoreType.DMA((N_DEV - 1, self.nch)), pltpu.SemaphoreType.DMA((N_DEV - 1, self.nch)),
                pltpu.SemaphoreType.DMA(())]

    def _copies(self, src_ref, out_ref, send_sems, recv_sems, local_sem):
        me, peer = _peers()
        local = pltpu.make_async_copy(src_ref if self.gather else src_ref.at[me], out_ref.at[me], local_sem)

        def copy(o, k, src_idx, dst_idx):
            rows = pl.ds(k * self.step, self.step)
            return pltpu.make_async_remote_copy(
                src_ref=(src_ref if self.gather else src_ref.at[src_idx]).at[rows],
                dst_ref=out_ref.at[dst_idx].at[rows],
                send_sem=send_sems.at[o - 1, k], recv_sem=recv_sems.at[o - 1, k],
                device_id=peer(o), device_id_type=MESH)

        pairs = [(o, k) for k in range(self.nch) for o in range(1, N_DEV)]
        sends = [copy(o, k, me ^ o, me) for o, k in pairs]
        recvs = [copy(o, k, me, me ^ o) for o, k in pairs]
        return local, sends, recvs

    def start(self, *refs):
        local, sends, _ = self._copies(*refs)
        local.start()
        for cp in sends:
            cp.start()

    def wait(self, *refs):
        local, sends, recvs = self._copies(*refs)
        for cp in recvs:
            cp.wait_recv()
        for cp in sends:
            cp.wait_send()
        local.wait()


def _run_exchange(ex, name):
    def body(src_ref, out_ref, *sems):
        ex.start(src_ref, out_ref, *sems)
        ex.wait(src_ref, out_ref, *sems)

    ex.result = pl.pallas_call(
        body, name=name, out_shape=jax.ShapeDtypeStruct(ex.out_shape, ex.src.dtype),
        in_specs=[pl.BlockSpec(memory_space=pl.ANY)], out_specs=pl.BlockSpec(memory_space=pl.ANY),
        scratch_shapes=ex.scratch(), compiler_params=pltpu.CompilerParams(has_side_effects=True),
    )(ex.src)
    return ex.result


def _call(body, args, *, name, grid, in_specs, out_specs, out_shape, scratch_shapes=(), sem, rider=None):
    if not rider:
        return pl.pallas_call(body, name=name, grid=grid, in_specs=in_specs, out_specs=out_specs,
                              out_shape=out_shape, scratch_shapes=list(scratch_shapes),
                              compiler_params=_params(sem))(*args)
    n_in, n_out, n_sc, n_ex = len(in_specs), len(out_specs), len(scratch_shapes), len(rider)

    def wrapped(*refs):
        ins, ex_src = refs[:n_in], refs[n_in:n_in + n_ex]
        outs = refs[n_in + n_ex:n_in + n_ex + n_out]
        ex_out = refs[n_in + n_ex + n_out:n_in + 2 * n_ex + n_out]
        scr = refs[n_in + 2 * n_ex + n_out:n_in + 2 * n_ex + n_out + n_sc]
        sems = refs[n_in + 2 * n_ex + n_out + n_sc:]
        ids = [pl.program_id(a) for a in range(len(grid))]
        first = functools.reduce(jnp.logical_and, [i == 0 for i in ids])
        last = functools.reduce(jnp.logical_and, [i == g - 1 for i, g in zip(ids, grid)])

        @pl.when(first)
        def _():
            for e, ex in enumerate(rider):
                ex.start(ex_src[e], ex_out[e], *sems[3 * e:3 * e + 3])

        body(*ins, *outs, *scr)

        @pl.when(last)
        def _():
            for e, ex in enumerate(rider):
                ex.wait(ex_src[e], ex_out[e], *sems[3 * e:3 * e + 3])

    anyspace = pl.BlockSpec(memory_space=pl.ANY)
    res = pl.pallas_call(
        wrapped, name=name, grid=grid, in_specs=list(in_specs) + [anyspace] * n_ex,
        out_specs=list(out_specs) + [anyspace] * n_ex,
        out_shape=list(out_shape) + [jax.ShapeDtypeStruct(ex.out_shape, ex.src.dtype) for ex in rider],
        scratch_shapes=list(scratch_shapes) + [s for ex in rider for s in ex.scratch()],
        compiler_params=_params(("arbitrary",) * len(grid), side_effects=True),
    )(*args, *[ex.src for ex in rider])
    for ex, r in zip(rider, res[n_out:]):
        ex.result = r
    return list(res[:n_out])


class _Comm:
    def __init__(self):
        self.queue, self.count = [], 0

    def post(self, src, gather):
        ex = _Exchange(src, gather)
        self.queue.append(ex)
        return ex

    def take(self, budget_us):
        rider = []
        for ex in list(self.queue):
            if budget_us > 0 and ex.cost_us <= RIDER_OVERHANG * budget_us:
                budget_us -= ex.cost_us
                rider.append(ex)
                self.queue.remove(ex)
        return rider

    def finish(self, ex):
        while ex.result is None:
            self.count += 1
            _run_exchange(self.queue.pop(0), "exchange%d" % self.count)
        return ex.result


def _mm(a, b, mode, out_dtype, name, a2=None, b2=None, comm=None, gated=None):
    if mode == "tn":
        kdim, m, n = a.shape[0], a.shape[1], b.shape[1]
    elif mode == "nt":
        m, kdim, n = a.shape[0], a.shape[1], b.shape[0]
    else:
        m, kdim, n = a.shape[0], a.shape[1], b.shape[1]
    if mode == "tn":
        tm = _pick(m, (2048, 1024, 512, 256, 128))
        tk = _pick(kdim, (1024, 768, 512, 256, 128))
    else:
        tm = _pick(m, (1024, 768, 512, 256, 128))
        tk = _pick(kdim, (2048, 1024, 768, 512, 256, 128))
    in_bytes = max(a.dtype.itemsize, b.dtype.itemsize) * (1 if a2 is None else 2)
    out_bytes = jnp.dtype(out_dtype).itemsize

    def vmem(tn_):
        return 2 * in_bytes * tk * (tm + tn_) + (4 + 2 * out_bytes + (0 if gated is None else 16)) * tm * tn_

    tn = next((t for t in (2048, 1024, 512, 256, 128) if n % t == 0 and vmem(t) <= MM_VMEM_BUDGET), n)
    nk = kdim // tk
    if mode == "tn":
        a_spec = pl.BlockSpec((tk, tm), lambda i, j, k: (k, i))
        b_spec = pl.BlockSpec((tk, tn), lambda i, j, k: (k, j))
        dims = TN
    elif mode == "nt":
        a_spec = pl.BlockSpec((tm, tk), lambda i, j, k: (i, k))
        b_spec = pl.BlockSpec((tn, tk), lambda i, j, k: (j, k))
        dims = NT
    else:
        a_spec = pl.BlockSpec((tm, tk), lambda i, j, k: (i, k))
        b_spec = pl.BlockSpec((tk, tn), lambda i, j, k: (k, j))
        dims = (((1,), (0,)), ((), ()))

    pairs = 1 if a2 is None else 2
    n_in = 2 * pairs + (0 if gated is None else 2)

    def body(*refs):
        outs, acc_ref = refs[n_in:-1], refs[-1]
        i, k = pl.program_id(0), pl.program_id(2)

        @pl.when(k == 0)
        def _():
            acc_ref[...] = jnp.zeros_like(acc_ref)

        for p in range(pairs):
            acc_ref[...] += lax.dot_general(refs[2 * p][...].astype(bf16), refs[2 * p + 1][...].astype(bf16), dims,
                                            preferred_element_type=f32)

        @pl.when(k == nk - 1)
        def _():
            acc = acc_ref[...]
            outs[0][...] = acc.astype(outs[0].dtype)
            if gated is not None:
                res_ref, gate_ref = refs[2 * pairs], refs[2 * pairs + 1]
                gate = gate_ref[0]
                if gate_ref.shape[0] == 2:
                    row = i * tm + lax.broadcasted_iota(jnp.int32, (tm, 1), 0)
                    gate = jnp.where(row >= gated[2], gate_ref[1], gate)
                outs[1][...] = res_ref[...] + gate * acc

    args = (a, b) if a2 is None else (a, b, a2, b2)
    ospec = pl.BlockSpec((tm, tn), lambda i, j, k: (i, j))
    in_specs, out_specs = [a_spec, b_spec] * pairs, [ospec]
    out_shape = [jax.ShapeDtypeStruct((m, n), out_dtype)]
    if gated is not None:
        res, gate, _ = gated
        args = args + (res, gate)
        in_specs += [ospec, pl.BlockSpec((gate.shape[0], 1, tn), lambda i, j, k: (0, 0, j))]
        out_specs.append(ospec)
        out_shape.append(jax.ShapeDtypeStruct((m, n), f32))
    rider = comm.take(2.0 * pairs * m * n * kdim / MXU_FLOPS_PER_US) if comm else None
    res = _call(
        body, args, name=name, grid=(m // tm, n // tn, nk), in_specs=in_specs, out_specs=out_specs,
        out_shape=out_shape, scratch_shapes=[pltpu.VMEM((tm, tn), f32)],
        sem=("parallel", "parallel", "arbitrary"), rider=rider)
    return res[0] if gated is None else res


def _ffn_up(h, w1, w3, name, comm=None):
    m, kdim = h.shape
    n = w1.shape[1]
    tm, tn = _pick(m, (1024, 768, 512, 256, 128)), _pick(n, (512, 256, 128))

    def body(h_ref, w1_ref, w3_ref, a_ref, b_ref, u_ref):
        hv = h_ref[...]
        a = jnp.dot(hv, w1_ref[...], preferred_element_type=f32).astype(bf16)
        b = jnp.dot(hv, w3_ref[...], preferred_element_type=f32).astype(bf16)
        a_ref[...] = a
        b_ref[...] = b
        u_ref[...] = (jax.nn.silu(a.astype(f32)) * b.astype(f32)).astype(bf16)

    wspec = pl.BlockSpec((kdim, tn), lambda i, j: (0, j))
    ospec = pl.BlockSpec((tm, tn), lambda i, j: (i, j))
    rider = comm.take(4.0 * m * n * kdim / MXU_FLOPS_PER_US) if comm else None
    return _call(
        body, (h, w1, w3), name=name, grid=(m // tm, n // tn),
        in_specs=[pl.BlockSpec((tm, kdim), lambda i, j: (i, 0)), wspec, wspec], out_specs=[ospec] * 3,
        out_shape=[jax.ShapeDtypeStruct((m, n), bf16)] * 3, sem=("parallel", "parallel"), rider=rider)


def _ffn_down_bwd(g, w2, a, b, name, comm=None):
    m, kdim = g.shape
    n = w2.shape[0]
    tm, tn = _pick(m, (1024, 768, 512, 256, 128)), _pick(n, (512, 256, 128))

    def body(g_ref, w_ref, a_ref, b_ref, da_ref, db_ref):
        du = lax.dot_general(g_ref[...], w_ref[...], NT, preferred_element_type=f32)
        _, vjp = jax.vjp(lambda p, q: jax.nn.silu(p) * q, a_ref[...].astype(f32), b_ref[...].astype(f32))
        da, db = vjp(du)
        da_ref[...] = da.astype(bf16)
        db_ref[...] = db.astype(bf16)

    ospec = pl.BlockSpec((tm, tn), lambda i, j: (i, j))
    rider = comm.take(2.0 * m * n * kdim / MXU_FLOPS_PER_US) if comm else None
    return _call(
        body, (g, w2, a, b), name=name, grid=(m // tm, n // tn),
        in_specs=[pl.BlockSpec((tm, kdim), lambda i, j: (i, 0)), pl.BlockSpec((tn, kdim), lambda i, j: (j, 0)),
                  ospec, ospec],
        out_specs=[ospec] * 2, out_shape=[jax.ShapeDtypeStruct((m, n), bf16)] * 2,
        sem=("parallel", "parallel"), rider=rider)


def _small_dot(a, b, name):
    def body(a_ref, b_ref, o_ref):
        o_ref[...] = jnp.dot(a_ref[...], b_ref[...], precision=lax.Precision.HIGHEST, preferred_element_type=f32)

    return pl.pallas_call(body, name=name, out_shape=jax.ShapeDtypeStruct((a.shape[0], b.shape[1]), f32))(a, b)


def _group_of(i, n_x_tiles, n_groups):
    return jnp.where(i >= n_x_tiles, n_groups - 1, 0)


def _rowwise(f, rows, vecs, outs, *, nrows, n_x_rows, name, tm=BLK, tc=None):
    n_x_tiles = n_x_rows // tm
    grid = (nrows // tm,) if tc is None else (nrows // tm, rows[0].shape[1] // tc)

    def rspec(cols):
        if tc is None:
            return pl.BlockSpec((tm, cols), lambda i: (i, 0))
        return pl.BlockSpec((tm, tc), lambda i, j: (i, j))

    def vspec(v):
        g = v.shape[0]
        if tc is None:
            return pl.BlockSpec((None, 1, v.shape[2]), lambda i: (_group_of(i, n_x_tiles, g), 0, 0))
        return pl.BlockSpec((None, 1, tc), lambda i, j: (_group_of(i, n_x_tiles, g), 0, j))

    nr, nv = len(rows), len(vecs)

    def body(*refs):
        ins = [r[...] for r in refs[:nr + nv]]
        res = f(*ins)
        for o_ref, val in zip(refs[nr + nv:], res):
            o_ref[...] = val.astype(o_ref.dtype)

    return pl.pallas_call(
        body, name=name, grid=grid,
        in_specs=[rspec(r.shape[1]) for r in rows] + [vspec(v) for v in vecs],
        out_specs=[rspec(c) for c, _ in outs],
        out_shape=[jax.ShapeDtypeStruct((nrows, c), d) for c, d in outs],
        compiler_params=_params(("parallel",) * len(grid)),
    )(*rows, *vecs)


def _rowwise_bwd(f, rows, vecs, cts, row_grads, *, nrows, n_x_rows, name, tm=BLK, tc=None, residual=None,
                 residual_rows=None, latent_grads_only=False):
    n_x_tiles = n_x_rows // tm
    n_tiles = nrows // tm
    out_rows = n_x_rows if latent_grads_only else nrows
    res_tiles = None if residual_rows is None else residual_rows // tm
    grid = (n_tiles,) if tc is None else (rows[0].shape[1] // tc, n_tiles)
    row_of = (lambda *g: g[0]) if tc is None else (lambda *g: g[1])

    def rspec(cols):
        if tc is None:
            return pl.BlockSpec((tm, cols), lambda i: (i, 0))
        return pl.BlockSpec((tm, tc), lambda j, i: (i, j))

    def vspec(v):
        g = v.shape[0]
        if tc is None:
            return pl.BlockSpec((None, 1, v.shape[2]), lambda i: (_group_of(i, n_x_tiles, g), 0, 0))
        return pl.BlockSpec((None, 1, tc), lambda j, i: (_group_of(i, n_x_tiles, g), 0, j))

    nr, nv, nc = len(rows), len(vecs), len(cts)
    gidx = sorted(row_grads)

    def body(*refs):
        i = row_of(*[pl.program_id(d) for d in range(len(grid))])
        row_vals = [r[...].astype(f32) for r in refs[:nr]]
        vec_vals = [jnp.broadcast_to(r[...].astype(f32), (tm, r.shape[-1])) for r in refs[nr:nr + nv]]
        ct_vals = [r[...].astype(f32) for r in refs[nr + nv:nr + nv + nc]]
        out_refs = refs[nr + nv + nc + (residual is not None):]
        res, vjp = jax.vjp(lambda *a: tuple(o.astype(f32) for o in f(*a)), *row_vals, *vec_vals)
        grads = list(vjp(tuple(ct_vals)))
        if residual is not None:
            extra = refs[nr + nv + nc][...]
            if res_tiles is not None:
                extra = jnp.where(i < res_tiles, extra, 0.0)
            grads[0] = grads[0] + extra
        for o_ref, k in zip(out_refs[:len(gidx)], gidx):
            if latent_grads_only:
                @pl.when(i < n_x_tiles)
                def _():
                    o_ref[...] = grads[k].astype(o_ref.dtype)
            else:
                o_ref[...] = grads[k].astype(o_ref.dtype)
        for o_ref, g, v in zip(out_refs[len(gidx):], grads[nr:], vecs):
            part = jnp.sum(g, axis=0, keepdims=True)
            first = (i == 0) if v.shape[0] == 1 else ((i == 0) | (i == n_x_tiles))

            @pl.when(first)
            def _():
                o_ref[...] = part

            @pl.when(jnp.logical_not(first))
            def _():
                o_ref[...] += part

    sem = ("arbitrary",) if tc is None else ("parallel", "arbitrary")
    gspec = rspec
    if latent_grads_only:
        assert tc is None
        gspec = lambda cols: pl.BlockSpec((tm, cols), lambda i: (jnp.minimum(i, n_x_tiles - 1), 0))
    res_specs, res_args = [], []
    if residual is not None:
        assert tc is None
        clamp = (lambda i: i) if res_tiles is None else (lambda i: jnp.minimum(i, res_tiles - 1))
        res_specs = [pl.BlockSpec((tm, residual.shape[1]), lambda i: (clamp(i), 0))]
        res_args = [residual]
    return pl.pallas_call(
        body, name=name, grid=grid,
        in_specs=[rspec(r.shape[1]) for r in rows] + [vspec(v) for v in vecs] + [rspec(c.shape[1]) for c in cts]
        + res_specs,
        out_specs=[gspec(rows[k].shape[1]) for k in gidx] + [vspec(v) for v in vecs],
        out_shape=[jax.ShapeDtypeStruct((out_rows, rows[k].shape[1]), row_grads[k]) for k in gidx]
        + [jax.ShapeDtypeStruct(v.shape, f32) for v in vecs],
        compiler_params=_params(sem),
    )(*rows, *vecs, *cts, *res_args)


def _rms(x, w):
    return x * lax.rsqrt(jnp.mean(x * x, axis=-1, keepdims=True) + EPS) * w


def _f_normmod(x, w, sc, sh):
    return (_rms(x.astype(f32), w) * (1.0 + sc) + sh,)


def _f_gated_add(x, y, g):
    return (x + g * y.astype(f32),)


def _f_glu_post(x, zv, zg, g, bv, bg):
    return (x + g * ((zv.astype(f32) + bv) * jax.nn.sigmoid(zg.astype(f32) + bg)),)


def _f_silu(x):
    return (jax.nn.silu(x.astype(f32)),)


def _loss_head(x, tgt, w, *, nrows, name):
    d = x.shape[1]
    tm = BLK

    def body(x_ref, t_ref, w_ref, dx_ref, dw_ref, loss_ref):
        i = pl.program_id(0)
        wb = jnp.broadcast_to(w_ref[...], (tm, d))
        y, vjp = jax.vjp(_rms, x_ref[...], wb)
        e = y - t_ref[...]
        dx, dwb = vjp(e * (1.0 / d))
        dx_ref[...] = dx
        dw = jnp.sum(dwb, axis=0, keepdims=True)
        part = jnp.full((8, 128), 0.5 / d, f32) * jnp.sum(e * e)

        @pl.when(i == 0)
        def _():
            dw_ref[...] = dw
            loss_ref[...] = part

        @pl.when(i > 0)
        def _():
            dw_ref[...] += dw
            loss_ref[...] += part

    row = pl.BlockSpec((tm, d), lambda i: (i, 0))
    return pl.pallas_call(
        body, name=name, grid=(nrows // tm,),
        in_specs=[row, row, pl.BlockSpec((1, d), lambda i: (0, 0))],
        out_specs=[row, pl.BlockSpec((1, d), lambda i: (0, 0)), pl.BlockSpec((8, 128), lambda i: (0, 0))],
        out_shape=[jax.ShapeDtypeStruct((nrows, d), f32), jax.ShapeDtypeStruct((1, d), f32),
                   jax.ShapeDtypeStruct((8, 128), f32)],
        compiler_params=_params(("arbitrary",)),
    )(x, tgt, w)


def _adamw(w, g, m, v, name):
    r, c = w.shape
    tr = _pick(r, (512, 256, 128, 64, 32, 16, 8))
    tcol = _pick(c, (1024, 512)) if c % 128 == 0 else c

    def body(w_ref, g_ref, m_ref, v_ref, d_ref, m2_ref, v2_ref):
        gg = g_ref[...]
        m2 = ADAM_B1 * m_ref[...] + (1.0 - ADAM_B1) * gg
        v2 = ADAM_B2 * v_ref[...] + (1.0 - ADAM_B2) * (gg * gg)
        m_hat = m2 / (1.0 - ADAM_B1 ** ADAM_STEP)
        v_hat = v2 / (1.0 - ADAM_B2 ** ADAM_STEP)
        d_ref[...] = -ADAM_LR * (m_hat / (jnp.sqrt(v_hat) + ADAM_EPS) + ADAM_WD * w_ref[...])
        m2_ref[...] = m2
        v2_ref[...] = v2

    spec = pl.BlockSpec((tr, tcol), lambda i, j: (i, j))
    return pl.pallas_call(
        body, name=name, grid=(r // tr, c // tcol), in_specs=[spec] * 4, out_specs=[spec] * 3,
        out_shape=[jax.ShapeDtypeStruct((r, c), f32)] * 3,
        compiler_params=_params(("parallel", "parallel")),
    )(w, g, m, v)


def _swap_quarters(x):
    lane = lax.broadcasted_iota(jnp.int32, x.shape, 1)
    return jnp.where((lane & 63) < 32, pltpu.roll(x, 96, 1), pltpu.roll(x, 32, 1))


def _rope_tables(n_x, n_ctx):
    t = np.arange(n_x)
    quarter = HEAD_DIM // 4
    inv = ROPE_BASE ** (-np.arange(quarter, dtype=np.float64) / quarter)
    ar = (t // GRID_W)[:, None] * inv[None]
    ac = (t % GRID_W)[:, None] * inv[None]
    cos = np.concatenate([np.cos(ar), np.cos(ar), np.cos(ac), np.cos(ac)], axis=1)
    sin = np.concatenate([-np.sin(ar), np.sin(ar), -np.sin(ac), np.sin(ac)], axis=1)
    cos = np.concatenate([cos, np.ones((n_ctx, HEAD_DIM))], axis=0)
    sin = np.concatenate([sin, np.zeros((n_ctx, HEAD_DIM))], axis=0)
    return jnp.asarray(cos, f32), jnp.asarray(sin, f32)


A_W = NA_HEADS * HEAD_DIM
QB0, KB0, VB0 = 3 * A_W, 3 * A_W + NB_Q_HEADS * HEAD_DIM, 3 * A_W + (NB_Q_HEADS + NB_KV_HEADS) * HEAD_DIM
IN_W = VB0 + NB_KV_HEADS * HEAD_DIM


def _qkv_post(qkv, cos, sin, name):
    n = qkv.shape[0]

    def body(x_ref, c_ref, s_ref, o_ref):
        c, s = c_ref[...], s_ref[...]
        o_ref[:, :QB0] = x_ref[:, :QB0].astype(bf16)
        for col in range(QB0, VB0, HEAD_DIM):
            x = x_ref[:, col:col + HEAD_DIM]
            o_ref[:, col:col + HEAD_DIM] = (x * c + _swap_quarters(x) * s).astype(bf16)
        o_ref[:, VB0:] = x_ref[:, VB0:].astype(bf16)

    row = lambda c: pl.BlockSpec((BLK, c), lambda i: (i, 0))
    return pl.pallas_call(
        body, name=name, grid=(n // BLK,), in_specs=[row(IN_W), row(HEAD_DIM), row(HEAD_DIM)],
        out_specs=row(IN_W), out_shape=jax.ShapeDtypeStruct((n, IN_W), bf16),
        compiler_params=_params(("parallel",)),
    )(qkv, cos, sin)


def _qkv_post_bwd(parts, cos, sin, name):
    n = parts[0].shape[0]

    def body(qa, ka, va, qb, kb, vb, c_ref, s_ref, o_ref):
        c, s = c_ref[...], s_ref[...]
        o_ref[:, 0:A_W] = qa[...].astype(bf16)
        o_ref[:, A_W:2 * A_W] = ka[...].astype(bf16)
        o_ref[:, 2 * A_W:QB0] = va[...].astype(bf16)
        for src, col0, width in ((qb, QB0, KB0 - QB0), (kb, KB0, VB0 - KB0)):
            for off in range(0, width, HEAD_DIM):
                g = src[:, off:off + HEAD_DIM].astype(f32)
                o_ref[:, col0 + off:col0 + off + HEAD_DIM] = (g * c + _swap_quarters(g * s)).astype(bf16)
        o_ref[:, VB0:] = vb[...].astype(bf16)

    row = lambda c: pl.BlockSpec((BLK, c), lambda i: (i, 0))
    return pl.pallas_call(
        body, name=name, grid=(n // BLK,),
        in_specs=[row(p.shape[1]) for p in parts] + [row(HEAD_DIM), row(HEAD_DIM)],
        out_specs=row(IN_W), out_shape=jax.ShapeDtypeStruct((n, IN_W), bf16),
        compiler_params=_params(("parallel",)),
    )(*parts, cos, sin)


def _valid(kind, qpos, kpos, n_x):
    ok = (kpos >= 0) & (kpos < n_x) & (qpos >= 0) & (qpos < n_x)
    if kind == "na":
        rows = n_x // GRID_W
        qr, qc = lax.shift_right_arithmetic(qpos, 6), qpos & (GRID_W - 1)
        kr, kc = lax.shift_right_arithmetic(kpos, 6), kpos & (GRID_W - 1)
        kr0 = jnp.clip(qr - NA_ROWS // 2, 0, rows - NA_ROWS)
        ws = jnp.clip(qc - NA_COLS // 2, 0, GRID_W - NA_COLS)
        return ok & (kr >= kr0) & (kr < kr0 + NA_ROWS) & (kc >= ws) & (kc < ws + NA_COLS)
    return ok & (jnp.abs(kpos - qpos) <= SW_RADIUS)


def _stack_heads(x, g):
    if g == 1:
        return x
    return jnp.concatenate([x[:, a * HEAD_DIM:(a + 1) * HEAD_DIM] for a in range(g)], axis=0)


def _unstack_heads(x, g):
    if g == 1:
        return x
    r = x.shape[0] // g
    return jnp.concatenate([x[a * r:(a + 1) * r] for a in range(g)], axis=1)


def _add_per_head(s, m, g):
    if g == 1:
        return s + m
    return (s.reshape((g,) + m.shape) + m[None]).reshape(s.shape)


class _AttnCfg:
    def __init__(self, kind, n_x, n_tot):
        self.kind, self.n_x, self.n_tot = kind, n_x, n_tot
        self.n_xb, self.n_blk = n_x // BLK, n_tot // BLK
        if kind == "na":
            self.g, self.nkv, self.q0, self.k0, self.v0 = 1, NA_HEADS, 0, NA_HEADS, 2 * NA_HEADS
        else:
            self.g, self.nkv = NB_GROUP, NB_KV_HEADS
            self.q0, self.k0, self.v0 = QB0 // (NB_GROUP * HEAD_DIM), KB0 // HEAD_DIM, VB0 // HEAD_DIM
        self.r = BLK * self.g
        self.qw = HEAD_DIM * self.g
        self.scale = HEAD_DIM ** -0.5
        last = self.n_xb - 1
        rows = lax.broadcasted_iota(jnp.int32, (BLK, 3 * BLK), 0)
        cols = lax.broadcasted_iota(jnp.int32, (BLK, 3 * BLK), 1)
        self.mask_q = jnp.stack([jnp.where(_valid(kind, i * BLK + rows, (i - 1) * BLK + cols, n_x), 0.0, NEG_INF)
                                 for i in (0, 1, last, self.n_xb)]).astype(f32)
        self.pattern = lambda i: jnp.where(i == 0, 0, jnp.where(i < last, 1, jnp.where(i == last, 2, 3)))


def _attn_fwd(qkv, cfg, extra, name, rider=None):
    g, r, qw, n_xb, n_x = cfg.g, cfg.r, cfg.qw, cfg.n_xb, cfg.n_x
    last = n_xb - 1

    def body(q_ref, kp, ko, kn, vp, vo, vn, kc_ref, vc_ref, ex_ref, m_ref, o_ref, lse_ref):
        q = _stack_heads(q_ref[...], g)
        kw = jnp.concatenate([kp[...], ko[...], kn[...]], axis=0)
        vw = jnp.concatenate([vp[...], vo[...], vn[...]], axis=0)
        s = lax.dot_general(q, kw, NT, preferred_element_type=f32) * cfg.scale
        if cfg.kind == "na":
            s = s + ex_ref[...]
        s = _add_per_head(s, m_ref[...], g)
        sc = lax.dot_general(q, kc_ref[...], NT, preferred_element_type=f32) * cfg.scale
        m = jnp.maximum(jnp.max(s, axis=-1, keepdims=True), jnp.max(sc, axis=-1, keepdims=True))
        if cfg.kind == "swa":
            m = jnp.maximum(m, ex_ref[...])
        p, pc = jnp.exp(s - m), jnp.exp(sc - m)
        l = jnp.sum(p, axis=-1, keepdims=True) + jnp.sum(pc, axis=-1, keepdims=True)
        if cfg.kind == "swa":
            l = l + jnp.exp(ex_ref[...] - m)
        o = jnp.dot(p.astype(bf16), vw, preferred_element_type=f32) + jnp.dot(pc.astype(bf16), vc_ref[...],
                                                                             preferred_element_type=f32)
        o_ref[...] = _unstack_heads(o / l, g).astype(bf16)
        lse_ref[...] = m + jnp.log(l)

    kv = lambda col0, f: pl.BlockSpec((BLK, HEAD_DIM), lambda h, i: (f(i), col0 + h))
    prev = lambda i: jnp.clip(i - 1, 0, last)
    own = lambda i: jnp.minimum(i, last)
    nxt = lambda i: jnp.minimum(i + 1, last)
    ctx = lambda i: n_xb
    if cfg.kind == "na":
        ex_spec = pl.BlockSpec((None, BLK, 3 * BLK), lambda h, i: (h, 0, 0))
    else:
        ex_spec = pl.BlockSpec((None, r, 1), lambda h, i: (h, 0, 0))
    m_spec = pl.BlockSpec((None, BLK, 3 * BLK), lambda h, i: (cfg.pattern(i), 0, 0))
    return _call(
        body, (qkv,) * 9 + (extra, cfg.mask_q), name=name, grid=(cfg.nkv, cfg.n_blk),
        in_specs=[pl.BlockSpec((BLK, qw), lambda h, i: (i, cfg.q0 + h)),
                  kv(cfg.k0, prev), kv(cfg.k0, own), kv(cfg.k0, nxt),
                  kv(cfg.v0, prev), kv(cfg.v0, own), kv(cfg.v0, nxt),
                  kv(cfg.k0, ctx), kv(cfg.v0, ctx), ex_spec, m_spec],
        out_specs=[pl.BlockSpec((BLK, qw), lambda h, i: (i, h)),
                   pl.BlockSpec((None, None, r, 1), lambda h, i: (h, i, 0, 0))],
        out_shape=[jax.ShapeDtypeStruct((cfg.n_tot, cfg.nkv * qw), bf16),
                   jax.ShapeDtypeStruct((cfg.nkv, cfg.n_blk, r, 1), f32)],
        sem=("parallel", "parallel"), rider=rider)


def _attn_bwd(qkv, o, do, lse, cfg, extra, name, rider=None):
    g, r, qw, n_xb, n_x = cfg.g, cfg.r, cfg.qw, cfg.n_xb, cfg.n_x
    last = n_xb - 1
    do_col0 = 0 if cfg.kind == "na" else (NA_HEADS * HEAD_DIM) // qw

    def body(q_ref, kp, ko, kn, vp, vo, vn, kc_ref, vc_ref, ex_ref, o_ref, do_ref, lse_ref, m_ref,
             dq_ref, delta_ref, dex_ref, dk_ref, dv_ref):
        i = pl.program_id(1)
        q = _stack_heads(q_ref[...], g)
        dout = _stack_heads(do_ref[...], g)
        out = _stack_heads(o_ref[...], g)
        delta = jnp.sum(dout.astype(f32) * out.astype(f32), axis=-1, keepdims=True)
        delta_ref[...] = delta
        kw = jnp.concatenate([kp[...], ko[...], kn[...]], axis=0)
        vw = jnp.concatenate([vp[...], vo[...], vn[...]], axis=0)
        s = lax.dot_general(q, kw, NT, preferred_element_type=f32) * cfg.scale
        if cfg.kind == "na":
            s = s + ex_ref[...]
        s = _add_per_head(s, m_ref[...], g)
        sc = lax.dot_general(q, kc_ref[...], NT, preferred_element_type=f32) * cfg.scale
        lse_v = lse_ref[...]
        p, pc = jnp.exp(s - lse_v), jnp.exp(sc - lse_v)
        dp = lax.dot_general(dout, vw, NT, preferred_element_type=f32)
        dpc = lax.dot_general(dout, vc_ref[...], NT, preferred_element_type=f32)
        ds, dsc = p * (dp - delta), pc * (dpc - delta)
        ds16, dsc16 = ds.astype(bf16), dsc.astype(bf16)
        dq = jnp.dot(ds16, kw, preferred_element_type=f32) + jnp.dot(dsc16, kc_ref[...], preferred_element_type=f32)
        dq_ref[...] = _unstack_heads(dq * cfg.scale, g).astype(bf16)
        dex = ds if cfg.kind == "na" else -jnp.exp(ex_ref[...] - lse_v) * delta

        @pl.when(i == 0)
        def _():
            dex_ref[...] = dex
            dk_ref[...] = jnp.zeros_like(dk_ref)
            dv_ref[...] = jnp.zeros_like(dv_ref)

        @pl.when(i > 0)
        def _():
            dex_ref[...] += dex

        dkw = lax.dot_general(ds16, q, TN, preferred_element_type=f32) * cfg.scale
        dvw = lax.dot_general(p.astype(bf16), dout, TN, preferred_element_type=f32)
        for w, blk in enumerate((jnp.clip(i - 1, 0, last), jnp.minimum(i, last), jnp.minimum(i + 1, last))):
            rows = pl.ds(pl.multiple_of(blk * BLK, BLK), BLK)
            dk_ref[rows, :] += dkw[w * BLK:(w + 1) * BLK]
            dv_ref[rows, :] += dvw[w * BLK:(w + 1) * BLK]
        crow = pl.ds(n_xb * BLK, BLK)
        dk_ref[crow, :] += lax.dot_general(dsc16, q, TN, preferred_element_type=f32) * cfg.scale
        dv_ref[crow, :] += lax.dot_general(pc.astype(bf16), dout, TN, preferred_element_type=f32)

    kv = lambda col0, f: pl.BlockSpec((BLK, HEAD_DIM), lambda h, i: (f(i), col0 + h))
    prev = lambda i: jnp.clip(i - 1, 0, last)
    own = lambda i: jnp.minimum(i, last)
    nxt = lambda i: jnp.minimum(i + 1, last)
    ctx = lambda i: n_xb
    if cfg.kind == "na":
        ex_spec = pl.BlockSpec((None, BLK, 3 * BLK), lambda h, i: (h, 0, 0))
    else:
        ex_spec = pl.BlockSpec((None, r, 1), lambda h, i: (h, 0, 0))
    stat = pl.BlockSpec((None, None, r, 1), lambda h, i: (h, i, 0, 0))
    m_spec = pl.BlockSpec((None, BLK, 3 * BLK), lambda h, i: (cfg.pattern(i), 0, 0))
    kv_all = pl.BlockSpec((cfg.n_tot, HEAD_DIM), lambda h, i: (0, h))
    return _call(
        body, (qkv,) * 9 + (extra, o, do, lse, cfg.mask_q), name=name, grid=(cfg.nkv, cfg.n_blk),
        in_specs=[pl.BlockSpec((BLK, qw), lambda h, i: (i, cfg.q0 + h)),
                  kv(cfg.k0, prev), kv(cfg.k0, own), kv(cfg.k0, nxt),
                  kv(cfg.v0, prev), kv(cfg.v0, own), kv(cfg.v0, nxt),
                  kv(cfg.k0, ctx), kv(cfg.v0, ctx), ex_spec,
                  pl.BlockSpec((BLK, qw), lambda h, i: (i, h)),
                  pl.BlockSpec((BLK, qw), lambda h, i: (i, do_col0 + h)), stat, m_spec],
        out_specs=[pl.BlockSpec((BLK, qw), lambda h, i: (i, h)), stat, ex_spec, kv_all, kv_all],
        out_shape=[jax.ShapeDtypeStruct((cfg.n_tot, cfg.nkv * qw), bf16),
                   jax.ShapeDtypeStruct((cfg.nkv, cfg.n_blk, r, 1), f32),
                   jax.ShapeDtypeStruct(extra.shape, f32)]
        + [jax.ShapeDtypeStruct((cfg.n_tot, cfg.nkv * HEAD_DIM), f32)] * 2,
        sem=("parallel", "arbitrary"), rider=rider)


def _toeplitz_basis():
    qc, kc = np.meshgrid(np.arange(GRID_W), np.arange(GRID_W), indexing="ij")
    e = (kc - qc + NA_COLS - 1)[None] == np.arange(2 * NA_COLS - 1)[:, None, None]
    return e.reshape(2 * NA_COLS - 1, GRID_W * GRID_W).astype(np.float32)


def _whole(f, ins, outs, name):
    def body(*refs):
        res = f(*[r[...] for r in refs[:len(ins)]])
        for o_ref, val in zip(refs[len(ins):], res):
            o_ref[...] = val.astype(o_ref.dtype)

    return pl.pallas_call(body, name=name,
                          out_shape=[jax.ShapeDtypeStruct(s, d) for s, d in outs])(*ins)


def _whole_bwd(f, ins, cts, name):
    n = len(ins)

    def body(*refs):
        _, vjp = jax.vjp(f, *[r[...] for r in refs[:n]])
        grads = vjp(tuple(r[...] for r in refs[n:n + len(cts)]))
        for o_ref, g in zip(refs[n + len(cts):], grads):
            o_ref[...] = g

    return pl.pallas_call(body, name=name,
                          out_shape=[jax.ShapeDtypeStruct(a.shape, f32) for a in ins])(*ins, *cts)


def _f_discretise(ar, ai, ldt, br, bi):
    dt = jnp.exp(ldt)
    mag = jnp.exp(ar * dt)
    lam_r, lam_i = mag * jnp.cos(ai * dt), mag * jnp.sin(ai * dt)
    den = ar * ar + ai * ai
    nr = lam_r - 1.0
    coef_r = (nr * ar + lam_i * ai) / den
    coef_i = (lam_i * ar - nr * ai) / den
    return (lam_r, lam_i, coef_r[None] * br - coef_i[None] * bi, coef_r[None] * bi + coef_i[None] * br)


SEG_LEN = BLK // SCAN_SEG
N_STATE = GROUPS_PER_CHUNK * SSM_STATE
CHUNK_CH = GROUPS_PER_CHUNK * SSM_GROUP


def _scan_perm():
    r = np.arange(BLK)
    t = (r % SCAN_SEG) * SEG_LEN + r // SCAN_SEG
    pm = np.zeros((BLK, BLK), np.float32)
    pm[r, t] = 1.0
    return jnp.asarray(pm, bf16), jnp.asarray(pm.T, bf16)


def _row_perm(x, pm, name, *, out_dtype, split, add=None, add_rows=None):
    n, c = x.shape
    add_tiles = None if add is None else add_rows // BLK

    def body(*refs):
        x_ref, pm_ref = refs[0], refs[1]
        o_ref = refs[-1]
        xv = x_ref[...]
        if split:
            out = _unpermute(pm_ref[...], xv.astype(f32))
        else:
            out = jnp.dot(pm_ref[...], xv.astype(bf16), preferred_element_type=f32)
        if add is not None:
            out = out + jnp.where(pl.program_id(0) < add_tiles, refs[2][...], 0.0)
        o_ref[...] = out.astype(o_ref.dtype)

    row = pl.BlockSpec((BLK, c), lambda i: (i, 0))
    in_specs, args = [row, pl.BlockSpec((BLK, BLK), lambda i: (0, 0))], [x, pm]
    if add is not None:
        in_specs.append(pl.BlockSpec((BLK, c), lambda i: (jnp.minimum(i, add_tiles - 1), 0)))
        args.append(add)
    return pl.pallas_call(
        body, name=name, grid=(n // BLK,), in_specs=in_specs, out_specs=row,
        out_shape=jax.ShapeDtypeStruct((n, c), out_dtype), compiler_params=_params(("parallel",)),
    )(*args)


def _glu_pre(u, yf, yr, dvec, pmt, n_x, name):
    d = u.shape[1]

    def body(u_ref, yf_ref, yr_ref, d_ref, pmt_ref, o_ref):
        y = d_ref[...] * u_ref[...] + _unpermute(pmt_ref[...], yf_ref[...] + yr_ref[...])
        o_ref[...] = jax.nn.gelu(y).astype(bf16)

    row = pl.BlockSpec((BLK, d), lambda i: (i, 0))
    return pl.pallas_call(
        body, name=name, grid=(n_x // BLK,),
        in_specs=[row, row, row, pl.BlockSpec((1, d), lambda i: (0, 0)), pl.BlockSpec((BLK, BLK), lambda i: (0, 0))],
        out_specs=row, out_shape=jax.ShapeDtypeStruct((n_x, d), bf16), compiler_params=_params(("parallel",)),
    )(u, yf, yr, dvec, pmt)


def _glu_pre_bwd(u, yf, yr, dvec, pm, pmt, dgy, n_x, name):
    d = u.shape[1]

    def body(u_ref, yf_ref, yr_ref, d_ref, pm_ref, pmt_ref, ct_ref, dyp_ref, dud_ref, dd_ref):
        uv = u_ref[...]
        y = d_ref[...] * uv + _unpermute(pmt_ref[...], yf_ref[...] + yr_ref[...])
        _, vjp = jax.vjp(jax.nn.gelu, y)
        dy = vjp(ct_ref[...])[0]
        dyp_ref[...] = jnp.dot(pm_ref[...], dy.astype(bf16), preferred_element_type=f32).astype(bf16)
        dud_ref[...] = d_ref[...] * dy
        part = jnp.sum(dy * uv, axis=0, keepdims=True)

        @pl.when(pl.program_id(0) == 0)
        def _():
            dd_ref[...] = part

        @pl.when(pl.program_id(0) > 0)
        def _():
            dd_ref[...] += part

    row = pl.BlockSpec((BLK, d), lambda i: (i, 0))
    vec = pl.BlockSpec((1, d), lambda i: (0, 0))
    sq = pl.BlockSpec((BLK, BLK), lambda i: (0, 0))
    return pl.pallas_call(
        body, name=name, grid=(n_x // BLK,), in_specs=[row, row, row, vec, sq, sq, row],
        out_specs=[row, row, vec],
        out_shape=[jax.ShapeDtypeStruct((n_x, d), bf16), jax.ShapeDtypeStruct((n_x, d), f32),
                   jax.ShapeDtypeStruct((1, d), f32)],
        compiler_params=_params(("arbitrary",)),
    )(u, yf, yr, dvec, pm, pmt, dgy)


def _block_order(n_xb, n_blk, reverse):
    n_cb = n_blk - n_xb
    if reverse:
        return lambda cc: jnp.where(cc < n_cb, n_blk - 1 - cc, n_xb - 1 - (cc - n_cb))
    return lambda cc: jnp.where(cc < n_cb, n_xb + cc, cc - n_cb)


def _unpermute(pmt, y):
    hi = y.astype(bf16)
    lo = (y - hi.astype(f32)).astype(bf16)
    return jnp.dot(pmt, hi, preferred_element_type=f32) + jnp.dot(pmt, lo, preferred_element_type=f32)


def _lam_pow(lr, li):
    for _ in range(int(math.log2(SEG_LEN))):
        lr, li = lr * lr - li * li, 2.0 * lr * li
    return lr, li


def _s5_fwd(up, lam, bblk, cblk, n_x, reverse, name, rider=None):
    n_tot, d = up.shape
    nq, n_blk, n_xb = d // CHUNK_CH, n_tot // BLK, n_x // BLK
    order = _block_order(n_xb, n_blk, reverse)
    ns = N_STATE
    seg_order = range(SCAN_SEG - 1, -1, -1) if reverse else range(SCAN_SEG)

    def body(u_ref, lam_ref, b_ref, c_ref, y_ref, cp_ref, bu_ref, st_ref, carry_ref):
        cc = pl.program_id(1)

        @pl.when(cc == 0)
        def _():
            carry_ref[...] = jnp.zeros_like(carry_ref)

        up_v = u_ref[...]
        bu_ref[0] = jnp.dot(up_v, b_ref[0], preferred_element_type=f32)
        bu_ref[1] = jnp.dot(up_v, b_ref[1], preferred_element_type=f32)
        lr, li = lam_ref[0:1, :], lam_ref[1:2, :]
        lrb, lib = jnp.broadcast_to(lr, (SCAN_SEG, ns)), jnp.broadcast_to(li, (SCAN_SEG, ns))

        def step(p, s, store):
            sr, si = s
            j = SEG_LEN - 1 - p if reverse else p
            off = pl.multiple_of(j * SCAN_SEG, SCAN_SEG)
            nsr = lrb * sr - lib * si + bu_ref[0, pl.ds(off, SCAN_SEG), :]
            nsi = lrb * si + lib * sr + bu_ref[1, pl.ds(off, SCAN_SEG), :]
            if store:
                st_ref[0, pl.ds(off, SCAN_SEG), :] = nsr
                st_ref[1, pl.ds(off, SCAN_SEG), :] = nsi
            return nsr, nsi

        zero = jnp.zeros((SCAN_SEG, ns), f32)
        er, ei = lax.fori_loop(0, SEG_LEN, lambda j, s: step(j, s, False), (zero, zero))
        pr, pi = _lam_pow(lr, li)
        cr, ci = carry_ref[0, 0:1, :], carry_ref[1, 0:1, :]
        rows_r, rows_i = [None] * SCAN_SEG, [None] * SCAN_SEG
        for k in seg_order:
            rows_r[k], rows_i[k] = cr, ci
            cr, ci = pr * cr - pi * ci + er[k:k + 1], pr * ci + pi * cr + ei[k:k + 1]
        carry_ref[0] = jnp.broadcast_to(cr, (SCAN_SEG, ns))
        carry_ref[1] = jnp.broadcast_to(ci, (SCAN_SEG, ns))
        cpr, cpi = jnp.concatenate(rows_r, axis=0), jnp.concatenate(rows_i, axis=0)
        cp_ref[0] = cpr
        cp_ref[1] = cpi
        lax.fori_loop(0, SEG_LEN, lambda j, s: step(j, s, True), (cpr, cpi))
        y_ref[...] = (jnp.dot(st_ref[0].astype(bf16), c_ref[0], preferred_element_type=f32)
                      - jnp.dot(st_ref[1].astype(bf16), c_ref[1], preferred_element_type=f32))

    return _call(
        body, (up, lam, bblk, cblk), name=name, grid=(nq, n_blk),
        in_specs=[pl.BlockSpec((BLK, CHUNK_CH), lambda q,cc: (order(cc), q)),
                  pl.BlockSpec((None, 2, ns), lambda q, cc: (q, 0, 0)),
                  pl.BlockSpec((None, 2, CHUNK_CH, ns), lambda q,cc: (q, 0, 0, 0)),
                  pl.BlockSpec((None, 2, ns, CHUNK_CH), lambda q,cc: (q, 0, 0, 0))],
        out_specs=[pl.BlockSpec((BLK, CHUNK_CH), lambda q,cc: (order(cc), q)),
                   pl.BlockSpec((None, None, 2, SCAN_SEG, ns), lambda q, cc: (q, cc, 0, 0, 0))],
        out_shape=[jax.ShapeDtypeStruct((n_tot, d), f32),
                   jax.ShapeDtypeStruct((nq, n_blk, 2, SCAN_SEG, ns), f32)],
        scratch_shapes=[pltpu.VMEM((2, BLK, ns), f32), pltpu.VMEM((2, BLK, ns), f32),
                        pltpu.VMEM((2, SCAN_SEG, ns), f32)],
        sem=("parallel", "arbitrary"), rider=rider)


def _s5_bwd(u, dy, cprev, lam, bblk, cblk, bblk_t, cblk_t, n_x, reverse, name, add=None, rider=None):
    n_tot, d = u.shape
    nq, n_blk, n_xb = d // CHUNK_CH, n_tot // BLK, n_x // BLK
    order = _block_order(n_xb, n_blk, reverse)
    ns = N_STATE
    blk_of = lambda step: order(n_blk - 1 - step)
    has_add = add is not None
    jof = (lambda p: SEG_LEN - 1 - p) if reverse else (lambda p: p)
    adj_seg_order = range(SCAN_SEG) if reverse else range(SCAN_SEG - 1, -1, -1)

    def body(*refs):
        u_ref, dy_ref, cp_ref, lam_ref, b_ref, c_ref, bt_ref, ct_ref = refs[:8]
        rest = refs[8:]
        if has_add:
            add_ref, rest = rest[0], rest[1:]
        du_ref, dlam_ref, db_ref, dc_ref, bu_ref, st_ref, ds_ref, acarry_ref = rest
        step_id = pl.program_id(1)
        is_x = blk_of(step_id) < n_xb

        @pl.when(step_id == 0)
        def _():
            acarry_ref[...] = jnp.zeros_like(acarry_ref)
            dlam_ref[...] = jnp.zeros_like(dlam_ref)
            db_ref[...] = jnp.zeros_like(db_ref)
            dc_ref[...] = jnp.zeros_like(dc_ref)

        up = u_ref[...]
        bu_ref[0] = jnp.dot(up, b_ref[0], preferred_element_type=f32)
        bu_ref[1] = jnp.dot(up, b_ref[1], preferred_element_type=f32)
        lr, li = lam_ref[0:1, :], lam_ref[1:2, :]
        lrb, lib = jnp.broadcast_to(lr, (SCAN_SEG, ns)), jnp.broadcast_to(li, (SCAN_SEG, ns))
        cpr, cpi = cp_ref[0], cp_ref[1]

        def fstep(p, s):
            sr, si = s
            off = pl.multiple_of(jof(p) * SCAN_SEG, SCAN_SEG)
            nsr = lrb * sr - lib * si + bu_ref[0, pl.ds(off, SCAN_SEG), :]
            nsi = lrb * si + lib * sr + bu_ref[1, pl.ds(off, SCAN_SEG), :]
            st_ref[0, pl.ds(off, SCAN_SEG), :] = nsr
            st_ref[1, pl.ds(off, SCAN_SEG), :] = nsi
            return nsr, nsi

        lax.fori_loop(0, SEG_LEN, fstep, (cpr, cpi))

        dyp = jnp.where(is_x, dy_ref[...], jnp.zeros_like(dy_ref))
        ds_ref[0] = jnp.dot(dyp, ct_ref[0], preferred_element_type=f32)
        ds_ref[1] = -jnp.dot(dyp, ct_ref[1], preferred_element_type=f32)

        def adj(p, a):
            ar, ai = a
            off = pl.multiple_of(jof(p) * SCAN_SEG, SCAN_SEG)
            nar = ds_ref[0, pl.ds(off, SCAN_SEG), :] + lrb * ar + lib * ai
            nai = ds_ref[1, pl.ds(off, SCAN_SEG), :] - lib * ar + lrb * ai
            return nar, nai

        zero = jnp.zeros((SCAN_SEG, ns), f32)
        er, ei = lax.fori_loop(0, SEG_LEN, lambda jj, a: adj(SEG_LEN - 1 - jj, a), (zero, zero))
        pr, pi = _lam_pow(lr, li)
        nr_, ni_ = acarry_ref[0, 0:1, :], acarry_ref[1, 0:1, :]
        rows_r, rows_i = [None] * SCAN_SEG, [None] * SCAN_SEG
        for k in adj_seg_order:
            rows_r[k], rows_i[k] = nr_, ni_
            nr_, ni_ = er[k:k + 1] + pr * nr_ + pi * ni_, ei[k:k + 1] + pr * ni_ - pi * nr_
        acarry_ref[0] = jnp.broadcast_to(nr_, (SCAN_SEG, ns))
        acarry_ref[1] = jnp.broadcast_to(ni_, (SCAN_SEG, ns))
        an_r, an_i = jnp.concatenate(rows_r, axis=0), jnp.concatenate(rows_i, axis=0)

        def adj2(jj, carry):
            ar, ai, glr, gli = carry
            p = SEG_LEN - 1 - jj
            nar, nai = adj(p, (ar, ai))
            off = pl.multiple_of(jof(p) * SCAN_SEG, SCAN_SEG)
            ds_ref[0, pl.ds(off, SCAN_SEG), :] = nar
            ds_ref[1, pl.ds(off, SCAN_SEG), :] = nai
            poff = pl.multiple_of(jof(p - 1) * SCAN_SEG, SCAN_SEG)
            spr, spi = st_ref[0, pl.ds(poff, SCAN_SEG), :], st_ref[1, pl.ds(poff, SCAN_SEG), :]
            return nar, nai, glr + nar * spr + nai * spi, gli - nar * spi + nai * spr

        ar, ai, glr, gli = lax.fori_loop(0, SEG_LEN - 1, adj2, (an_r, an_i, zero, zero))
        nar, nai = adj(0, (ar, ai))
        first = jof(0) * SCAN_SEG
        ds_ref[0, first:first + SCAN_SEG, :] = nar
        ds_ref[1, first:first + SCAN_SEG, :] = nai
        glr = glr + nar * cpr + nai * cpi
        gli = gli - nar * cpi + nai * cpr
        dlam_ref[0:1, :] += jnp.sum(glr, axis=0, keepdims=True)
        dlam_ref[1:2, :] += jnp.sum(gli, axis=0, keepdims=True)

        a_r, a_i = ds_ref[0].astype(bf16), ds_ref[1].astype(bf16)
        du = jnp.dot(a_r, bt_ref[0], preferred_element_type=f32) + jnp.dot(a_i, bt_ref[1],
                                                                        preferred_element_type=f32)
        if has_add:
            du = du + add_ref[...]
        du_ref[...] = du
        db_ref[0] += lax.dot_general(up, a_r, TN, preferred_element_type=f32)
        db_ref[1] += lax.dot_general(up, a_i, TN, preferred_element_type=f32)
        dc_ref[0] += lax.dot_general(st_ref[0].astype(bf16), dyp, TN, preferred_element_type=f32)
        dc_ref[1] -= lax.dot_general(st_ref[1].astype(bf16), dyp, TN, preferred_element_type=f32)

    tok = pl.BlockSpec((BLK, CHUNK_CH), lambda q,s: (blk_of(s), q))
    in_specs = [tok, pl.BlockSpec((BLK, CHUNK_CH), lambda q,s: (jnp.minimum(blk_of(s), n_xb - 1), q)),
                pl.BlockSpec((None, None, 2, SCAN_SEG, ns), lambda q, s: (q, n_blk - 1 - s, 0, 0, 0)),
                pl.BlockSpec((None, 2, ns), lambda q, s: (q, 0, 0)),
                pl.BlockSpec((None, 2, CHUNK_CH, ns), lambda q,s: (q, 0, 0, 0)),
                pl.BlockSpec((None, 2, ns, CHUNK_CH), lambda q,s: (q, 0, 0, 0)),
                pl.BlockSpec((None, 2, ns, CHUNK_CH), lambda q,s: (q, 0, 0, 0)),
                pl.BlockSpec((None, 2, CHUNK_CH, ns), lambda q,s: (q, 0, 0, 0))]
    args = [u, dy, cprev, lam, bblk, cblk, bblk_t, cblk_t]
    if has_add:
        in_specs.append(tok)
        args.append(add)
    return _call(
        body, args, name=name, grid=(nq, n_blk), in_specs=in_specs,
        out_specs=[tok, pl.BlockSpec((None, 2, ns), lambda q, s: (q, 0, 0)),
                   pl.BlockSpec((None, 2, CHUNK_CH, ns), lambda q,s: (q, 0, 0, 0)),
                   pl.BlockSpec((None, 2, ns, CHUNK_CH), lambda q,s: (q, 0, 0, 0))],
        out_shape=[jax.ShapeDtypeStruct((n_tot, d), f32), jax.ShapeDtypeStruct((nq, 2, ns), f32),
                   jax.ShapeDtypeStruct((nq, 2, CHUNK_CH, ns), f32),
                   jax.ShapeDtypeStruct((nq, 2, ns, CHUNK_CH), f32)],
        scratch_shapes=[pltpu.VMEM((2, BLK, ns), f32), pltpu.VMEM((2, BLK, ns), f32),
                        pltpu.VMEM((2, BLK, ns), f32), pltpu.VMEM((2, SCAN_SEG, ns), f32)],
        sem=("parallel", "arbitrary"), rider=rider)


def _exchange(src, gather, name):
    return _run_exchange(_Exchange(src, gather), name)


def _sum0(x, name):
    n, r, c = x.shape
    tr = _pick(r, (512, 256, 128, 64, 32, 16, 8))

    def body(x_ref, o_ref):
        acc = x_ref[0].astype(f32)
        for k in range(1, n):
            acc = acc + x_ref[k].astype(f32)
        o_ref[...] = acc

    return pl.pallas_call(
        body, name=name, grid=(r // tr,), in_specs=[pl.BlockSpec((n, tr, c), lambda i: (0, i, 0))],
        out_specs=pl.BlockSpec((tr, c), lambda i: (i, 0)), out_shape=jax.ShapeDtypeStruct((r, c), f32),
        compiler_params=_params(("parallel",)),
    )(x)


LANES = 1024


def _rows_of(shape):
    return -(-int(np.prod(shape)) // LANES)


def _pack(parts, dtype):
    rows = []
    for p in parts:
        flat = p.astype(dtype).reshape(-1)
        rows.append(jnp.pad(flat, (0, _rows_of(p.shape) * LANES - flat.shape[0])).reshape(-1, LANES))
    out = jnp.concatenate(rows, axis=0)
    return jnp.pad(out, ((0, (-out.shape[0]) % 16), (0, 0)))


def _unpack(packed, shapes):
    out, row = [], 0
    for s in shapes:
        n = _rows_of(s)
        out.append(packed[row:row + n].reshape(-1)[:int(np.prod(s))].reshape(s))
        row += n
    return out


def _vec(a):
    return a.reshape(-1, 1, a.shape[-1])


def _local_step(xa, tgt, mods, wts, sm, n_x):
    n_tot, d = xa.shape
    kw = dict(nrows=n_tot, n_x_rows=n_x)
    kx = dict(nrows=n_x, n_x_rows=n_x)
    mv = lambda l, k: mods[l, :, k][:, None, :]
    mx = lambda l, k: mods[l, 0:1, k][:, None, :]
    nmix, nffn = sm["norm_mix"], sm["norm_ffn"]
    cos, sin = _rope_tables(n_x, n_tot - n_x)
    cfg_a, cfg_b = _AttnCfg("na", n_x, n_tot), _AttnCfg("swa", n_x, n_tot)

    h1 = _rowwise(_f_normmod, [xa], [_vec(nmix[0:1]), mv(0, 1), mv(0, 0)], [(d, bf16)], name="l0_norm_mix", **kw)[0]
    qkv32 = _mm(h1, wts("w_in"), "nn", f32, "l0_in_proj", comm=wts)
    qkv = _qkv_post(qkv32, cos, sin, "l0_qkv_post")
    rpb2 = jnp.zeros((128, 128), f32).at[:NA_HEADS * 15, :31].set(sm["rpb"].reshape(NA_HEADS * 15, 31))
    basis = jnp.zeros((128, GRID_W * GRID_W), f32).at[:31].set(_toeplitz_basis())
    tz = _small_dot(rpb2, basis, "rpb_expand")[:NA_HEADS * 15].reshape(NA_HEADS, 15, GRID_W, GRID_W)
    bias = jnp.stack([tz[:, 3 - a:15 - a] for a in range(4)], axis=1).transpose(0, 1, 3, 2, 4).reshape(
        NA_HEADS, BLK, 3 * BLK)
    sink_rows = jnp.repeat(sm["sink"].reshape(NB_KV_HEADS, NB_GROUP, 1), BLK, axis=1).reshape(
        NB_KV_HEADS, NB_GROUP * BLK, 1)
    oa, lse_a = _attn_fwd(qkv, cfg_a, bias, "l0_na_fwd", rider=wts.take(CARRIER_US["na_fwd"]))
    ob, lse_b = _attn_fwd(qkv, cfg_b, sink_rows, "l0_swa_fwd", rider=wts.take(CARRIER_US["swa_fwd"]))
    w_out_a, w_out_b = wts("w_out")[:A_W], wts("w_out")[A_W:]
    y1, xb = _mm(oa, w_out_a, "nn", f32, "l0_out_proj", a2=ob, b2=w_out_b, comm=wts, gated=(xa, mv(0, 2), n_x))

    def ffn_fwd(xin, l, vec_of, kk, tag):
        h = _rowwise(_f_normmod, [xin], [_vec(nffn[l:l + 1]), vec_of(l, 4), vec_of(l, 3)], [(d, bf16)],
                     name=tag + "_norm_ffn", **kk)[0]
        a, b, u = _ffn_up(h, wts("w1", l), wts("w3", l), tag + "_ffn_up", comm=wts)
        fo = _mm(u, wts("w2", l), "nn", f32, tag + "_ffn_w2", comm=wts)
        xo = _rowwise(_f_gated_add, [xin, fo], [vec_of(l, 5)], [(d, f32)], name=tag + "_res_ffn", **kk)[0]
        return xo, (h, a, b, u, fo)

    def ffn_bwd(d_out, xin, saved, l, vec_of, kk, tag):
        h, a, b, u, fo = saved
        rows = kk["nrows"]
        dfo, dg2 = _rowwise_bwd(_f_gated_add, [xin, fo], [vec_of(l, 5)], [d_out], {1: bf16},
                                name=tag + "_res_ffn_bwd", **kk)
        wts.grad("w2", l, _mm(u[:rows], dfo, "tn", f32, tag + "_ffn_w2_dw", comm=wts))
        da, db = _ffn_down_bwd(dfo, wts("w2", l), a, b, tag + "_ffn_down_bwd", comm=wts)
        dh = _mm(da, wts("w1", l), "nt", f32, tag + "_ffn_w13_dx", a2=db, b2=wts("w3", l), comm=wts)
        wts.grad("w1", l, _mm(h[:rows], da, "tn", f32, tag + "_ffn_w1_dw", comm=wts))
        wts.grad("w3", l, _mm(h[:rows], db, "tn", f32, tag + "_ffn_w3_dw", comm=wts))
        dxin, dnw, dsc, dsh = _rowwise_bwd(
            _f_normmod, [xin], [_vec(nffn[l:l + 1]), vec_of(l, 4), vec_of(l, 3)], [dh], {0: f32},
            name=tag + "_norm_ffn_bwd", residual=d_out, **kk)
        return dxin, dnw, dsc, dsh, dg2

    xc, ffn0 = ffn_fwd(xb, 0, mv, kw, "l0")

    hs = _rowwise(_f_normmod, [xc], [_vec(nmix[1:2]), mv(1, 1), mv(1, 0)], [(d, f32)], name="l1_norm_mix", **kw)[0]
    g2n = sm["a_re"].shape[1]
    nq = d // CHUNK_CH
    ar2, ai2 = sm["a_re"].reshape(2 * g2n, SSM_STATE), sm["a_im"].reshape(2 * g2n, SSM_STATE)
    ldt2 = sm["log_dt"].reshape(2 * g2n, 1)
    bt_re = sm["b_re"].transpose(3, 0, 1, 2).reshape(SSM_GROUP, 2 * g2n, SSM_STATE)
    bt_im = sm["b_im"].transpose(3, 0, 1, 2).reshape(SSM_GROUP, 2 * g2n, SSM_STATE)
    disc_in = [ar2, ai2, ldt2, bt_re, bt_im]
    lam_r, lam_i, bbar_r, bbar_i = _whole(
        _f_discretise, disc_in,
        [((2 * g2n, SSM_STATE), f32)] * 2 + [((SSM_GROUP, 2 * g2n, SSM_STATE), f32)] * 2, "s5_discretise")
    eye = jnp.eye(GROUPS_PER_CHUNK, dtype=f32)
    eye6 = eye.astype(bf16)[None, None, :, None, :, None]

    def blockdiag_b(bbar):
        t = bbar.astype(bf16).reshape(SSM_GROUP, 2, nq, GROUPS_PER_CHUNK, SSM_STATE).transpose(1, 2, 3, 0, 4)
        return (t[:, :, :, :, None, :] * eye6).reshape(2, nq, CHUNK_CH, N_STATE)

    def blockdiag_c(cw):
        t = cw.astype(bf16).reshape(2, nq, GROUPS_PER_CHUNK, SSM_GROUP, SSM_STATE).transpose(0, 1, 2, 4, 3)
        return (t[:, :, :, :, None, :] * eye6).reshape(2, nq, N_STATE, CHUNK_CH)

    lam = jnp.stack([lam_r.reshape(2, nq, N_STATE), lam_i.reshape(2, nq, N_STATE)], axis=2)
    bblk = jnp.stack([blockdiag_b(bbar_r), blockdiag_b(bbar_i)], axis=2)
    cblk = jnp.stack([blockdiag_c(sm["c_re"]), blockdiag_c(sm["c_im"])], axis=2)
    bblk16, cblk16 = bblk, cblk
    bblk_t, cblk_t = bblk16.transpose(0, 1, 2, 4, 3), cblk16.transpose(0, 1, 2, 4, 3)
    pm, pmt = _scan_perm()
    hs_p = _row_perm(hs, pm, "l1_s5_perm", out_dtype=bf16, split=False)
    ys, cps = [], []
    for dr in range(2):
        yd, cp = _s5_fwd(hs_p, lam[dr], bblk16[dr], cblk16[dr], n_x, dr == 1, "l1_s5_fwd%d" % dr,
                         rider=wts.take(CARRIER_US["s5_fwd"]))
        ys.append(yd)
        cps.append(cp)
    dvec = sm["ssm_d"].reshape(1, d)
    gy = _glu_pre(hs, ys[0], ys[1], dvec, pmt, n_x, "l1_glu_pre")
    zv = _mm(gy, wts("w_glu_v"), "nn", f32, "l1_glu_val", comm=wts)
    zg = _mm(gy, wts("w_glu_g"), "nn", f32, "l1_glu_gate", comm=wts)
    bv, bg = sm["b_glu"][:d].reshape(1, 1, d), sm["b_glu"][d:].reshape(1, 1, d)
    xd = _rowwise(_f_glu_post, [xc, zv, zg], [mx(1, 2), bv, bg], [(d, f32)], name="l1_glu_post", **kx)[0]
    xe, ffn1 = ffn_fwd(xd, 1, mx, kx, "l1")

    d_xe, d_nfinal, loss_blk = _loss_head(xe, tgt, sm["norm_final"].reshape(1, d), nrows=n_x, name="loss_head")
    d_xd, dnffn1, dsc2_1, dsh2_1, dg2_1 = ffn_bwd(d_xe, xd, ffn1, 1, mx, kx, "l1")
    dzv, dzg, dg1_1, dbv, dbg = _rowwise_bwd(_f_glu_post, [xc, zv, zg], [mx(1, 2), bv, bg], [d_xd],
                                             {1: bf16, 2: bf16}, name="l1_glu_post_bwd", **kx)
    dgy = _mm(dzv, wts("w_glu_v"), "nt", f32, "l1_glu_dx", a2=dzg, b2=wts("w_glu_g"), comm=wts)
    dwglu_v = _mm(gy, dzv, "tn", f32, "l1_glu_val_dw", comm=wts)
    dwglu_g = _mm(gy, dzg, "tn", f32, "l1_glu_gate_dw", comm=wts)
    wts.grad("w_glu", 0, jnp.concatenate([dwglu_v, dwglu_g], axis=1))
    dy_p, du_skip, d_dvec = _glu_pre_bwd(hs, ys[0], ys[1], dvec, pm, pmt, dgy, n_x, "l1_glu_pre_bwd")
    du0, dlam0, db0, dc0 = _s5_bwd(hs_p, dy_p, cps[0], lam[0], bblk16[0], cblk16[0], bblk_t[0], cblk_t[0], n_x,
                                   False, "l1_s5_bwd0", rider=wts.take(CARRIER_US["s5_bwd"]))
    du1, dlam1, db1, dc1 = _s5_bwd(hs_p, dy_p, cps[1], lam[1], bblk16[1], cblk16[1], bblk_t[1], cblk_t[1], n_x,
                                   True, "l1_s5_bwd1", add=du0, rider=wts.take(CARRIER_US["s5_bwd"]))
    d_hs = _row_perm(du1, pmt, "l1_s5_unperm", out_dtype=f32, split=True, add=du_skip, add_rows=n_x)
    d_xc, dnmix1, dsc1_1, dsh1_1 = _rowwise_bwd(
        _f_normmod, [xc], [_vec(nmix[1:2]), mv(1, 1), mv(1, 0)], [d_hs], {0: f32},
        name="l1_norm_mix_bwd", residual=d_xd, residual_rows=n_x, **kw)
    dlam = jnp.stack([dlam0, dlam1])
    dlam_r, dlam_i = dlam[:, :, 0].reshape(2 * g2n, SSM_STATE), dlam[:, :, 1].reshape(2 * g2n, SSM_STATE)
    dbb = jnp.stack([db0, db1]).reshape(2, nq, 2, GROUPS_PER_CHUNK, SSM_GROUP, GROUPS_PER_CHUNK, SSM_STATE)
    eye7 = eye[None, None, None, :, None, :, None]
    dbbar = (dbb * eye7).sum(axis=5).transpose(2, 4, 0, 1, 3, 5).reshape(2, SSM_GROUP, 2 * g2n, SSM_STATE)
    dcc = jnp.stack([dc0, dc1]).reshape(2, nq, 2, GROUPS_PER_CHUNK, SSM_STATE, GROUPS_PER_CHUNK, SSM_GROUP)
    dcw = (dcc * eye7).sum(axis=5).transpose(2, 0, 1, 3, 5, 4).reshape(2, 2, g2n, SSM_GROUP, SSM_STATE)
    d_ar, d_ai, d_ldt, d_btr, d_bti = _whole_bwd(_f_discretise, disc_in, [dlam_r, dlam_i, dbbar[0], dbbar[1]],
                                                 "s5_discretise_bwd")
    to_b = lambda t: t.reshape(SSM_GROUP, 2, g2n, SSM_STATE).transpose(1, 2, 3, 0)
    wts.early(dict(
        loss=loss_blk[0, 0].reshape(1), a_re=d_ar.reshape(sm["a_re"].shape), a_im=d_ai.reshape(sm["a_im"].shape),
        log_dt=d_ldt.reshape(sm["log_dt"].shape), b_re=to_b(d_btr), b_im=to_b(d_bti), c_re=dcw[0], c_im=dcw[1],
        ssm_d=d_dvec.reshape(d), b_glu=jnp.concatenate([dbv.reshape(d), dbg.reshape(d)]),
        norm_final=d_nfinal.reshape(d)))

    d_xb, dnffn0, dsc2_0, dsh2_0, dg2_0 = ffn_bwd(d_xc, xb, ffn0, 0, mv, kw, "l0")
    dy1, dg1_0 = _rowwise_bwd(_f_gated_add, [xa, y1], [mv(0, 2)], [d_xb], {1: bf16}, name="l0_res_mix_bwd", **kw)
    d_o = _mm(dy1, wts("w_out"), "nt", bf16, "l0_out_proj_dx", comm=wts)
    wts.grad("w_out", 0, jnp.concatenate([_mm(oa, dy1, "tn", f32, "l0_out_proj_dw_a", comm=wts),
                                          _mm(ob, dy1, "tn", f32, "l0_out_proj_dw_b", comm=wts)], axis=0))
    dqa, _, dbias, dka, dva = _attn_bwd(qkv, oa, d_o, lse_a, cfg_a, bias, "l0_na_bwd",
                                          rider=wts.take(CARRIER_US["na_bwd"]))
    dqb, _, dsink_rows, dkb, dvb = _attn_bwd(qkv, ob, d_o, lse_b, cfg_b, sink_rows, "l0_swa_bwd",
                                               rider=wts.take(CARRIER_US["swa_bwd"]))
    d_qkv = _qkv_post_bwd([dqa, dka, dva, dqb, dkb, dvb], cos, sin, "l0_qkv_post_bwd")
    wts.grad("w_in", 0, _mm(h1, d_qkv, "tn", f32, "l0_in_proj_dw", comm=wts))
    dh1 = _mm(d_qkv, wts("w_in"), "nt", f32, "l0_in_proj_dx", comm=wts)
    d_xa, dnmix0, dsc1_0, dsh1_0 = _rowwise_bwd(
        _f_normmod, [xa], [_vec(nmix[0:1]), mv(0, 1), mv(0, 0)], [dh1], {0: f32},
        name="l0_norm_mix_bwd", residual=d_xb, latent_grads_only=True, **kw)
    dbias5 = dbias.reshape(NA_HEADS, 4, GRID_W, 12, GRID_W).transpose(0, 1, 3, 2, 4)
    dtz = sum(jnp.pad(dbias5[:, a], ((0, 0), (3 - a, a), (0, 0), (0, 0))) for a in range(4))
    dtz2 = jnp.zeros((128, GRID_W * GRID_W), f32).at[:NA_HEADS * 15].set(dtz.reshape(NA_HEADS * 15, -1))
    d_rpb = _small_dot(dtz2, basis.T, "rpb_expand_bwd")[:NA_HEADS * 15, :31].reshape(sm["rpb"].shape)
    d_sink = dsink_rows.reshape(NB_KV_HEADS * NB_GROUP, BLK).sum(axis=1)

    zero_c = jnp.zeros((1, 1, d), f32)
    both = lambda gx: jnp.concatenate([gx, zero_c], axis=0)
    dmods = jnp.stack([
        jnp.stack([dsh1_0, dsc1_0, dg1_0, dsh2_0, dsc2_0, dg2_0], axis=2),
        jnp.stack([dsh1_1, dsc1_1, both(dg1_1), both(dsh2_1), both(dsc2_1), both(dg2_1)], axis=2),
    ])[:, :, 0]
    late = dict(norm_mix=jnp.concatenate([dnmix0[0], dnmix1[0]]), norm_ffn=jnp.concatenate([dnffn0[0], dnffn1[0]]),
                rpb=d_rpb, sink=d_sink)
    return d_xa, late, dmods


WEIGHTS = ("c_ctx", "ada_w", "ada_b", "norm_mix", "norm_ffn", "ffn_w1", "ffn_w3", "ffn_w2", "attn_w_in",
           "attn_w_out", "attn_rpb", "attn_sink", "ssm_a_re", "ssm_a_im", "ssm_log_dt", "ssm_b_re", "ssm_b_im",
           "ssm_c_re", "ssm_c_im", "ssm_d", "ssm_w_glu", "ssm_b_glu", "norm_final")
SHARDED_BIG = ("ffn_w1", "ffn_w3", "ffn_w2", "attn_w_in", "attn_w_out", "ssm_w_glu")
BIG = ("ada_w",) + SHARDED_BIG
SMALL = tuple(n for n in WEIGHTS if n not in BIG)


BIG_SLABS = dict(w_in=("attn_w_in", "cols"), w_out=("attn_w_out", "rows"), w1=("ffn_w1", "cols"),
                 w3=("ffn_w3", "cols"), w2=("ffn_w2", "rows"), w_glu=("ssm_w_glu", "cols"))
GATHER_ORDER = (("w_in", 0), ("w_out", 0), ("w1", 0), ("w3", 0), ("w2", 0), ("w_glu", 0), ("w1", 1), ("w3", 1),
                ("w2", 1))


class _Weights(_Comm):
    def __init__(self, w, pack_early):
        super().__init__()
        self.w, self.pack_early = w, pack_early
        self.gathers = {key: self.post(w[BIG_SLABS[key[0]][0]][key[1]].astype(bf16), True) for key in GATHER_ORDER}
        self.full, self.scatters, self.early_gather = {}, {}, None

    def __call__(self, name, layer=0):
        key = ("w_glu", 0) if name.startswith("w_glu") else (name, layer)
        if key not in self.full:
            blk = self.finish(self.gathers[key])
            if BIG_SLABS[key[0]][1] == "cols":
                self.full[key] = blk.transpose(1, 0, 2).reshape(blk.shape[1], -1)
            else:
                self.full[key] = blk.reshape(-1, blk.shape[2])
        full = self.full[key]
        if name.startswith("w_glu"):
            half = full.shape[1] // 2
            return full[:, :half] if name == "w_glu_v" else full[:, half:]
        return full

    def grad(self, name, layer, dw):
        if BIG_SLABS[name][1] == "cols":
            blk = dw.reshape(dw.shape[0], N_DEV, -1).transpose(1, 0, 2)
        else:
            blk = dw.reshape(N_DEV, -1, dw.shape[1])
        self.scatters[(name, layer)] = self.post(blk.astype(bf16), False)

    def early(self, parts):
        self.early_shapes = {k: v.shape for k, v in parts.items()}
        self.early_gather = self.post(self.pack_early(parts), True)

    def reduced(self, name):
        short = next(k for k, v in BIG_SLABS.items() if v[0] == name)
        layers = [l for (n, l) in self.scatters if n == short]
        parts = [_sum0(self.finish(self.scatters[(short, l)]), "sum_%s%d" % (short, l)) for l in sorted(layers)]
        return jnp.stack(parts).reshape(self.w[name].shape)


def kernel(x, c, ctx, c_ctx, ada_w, ada_b, norm_mix, norm_ffn, ffn_w1, ffn_w3, ffn_w2, attn_w_in, attn_w_out, attn_rpb, attn_sink, ssm_a_re, ssm_a_im, ssm_log_dt, ssm_b_re, ssm_b_im, ssm_c_re, ssm_c_im, ssm_d, ssm_w_glu, ssm_b_glu, norm_final, loss_target, m_c_ctx, m_ada_w, m_ada_b, m_norm_mix, m_norm_ffn, m_ffn_w1, m_ffn_w3, m_ffn_w2, m_attn_w_in, m_attn_w_out, m_attn_rpb, m_attn_sink, m_ssm_a_re, m_ssm_a_im, m_ssm_log_dt, m_ssm_b_re, m_ssm_b_im, m_ssm_c_re, m_ssm_c_im, m_ssm_d, m_ssm_w_glu, m_ssm_b_glu, m_norm_final, v_c_ctx, v_ada_w, v_ada_b, v_norm_mix, v_norm_ffn, v_ffn_w1, v_ffn_w3, v_ffn_w2, v_attn_w_in, v_attn_w_out, v_attn_rpb, v_attn_sink, v_ssm_a_re, v_ssm_a_im, v_ssm_log_dt, v_ssm_b_re, v_ssm_b_im, v_ssm_c_re, v_ssm_c_im, v_ssm_d, v_ssm_w_glu, v_ssm_b_glu, v_norm_final):
    p = dict(locals())
    w = {n: p[n] for n in WEIGHTS}
    me = 4 * lax.axis_index("x") + 2 * lax.axis_index("y") + lax.axis_index("c")
    n_x, d = x.shape[1], x.shape[2]
    cols = ada_w.shape[2]
    d8 = d // N_DEV

    first = jnp.concatenate([c[0], ssm_d[0], ssm_b_glu[0]])[None]
    g0 = _exchange(first, True, "gather_vectors")[:, 0]
    c_all, d_full, bglu_full = g0[:, :d], g0[:, d:d + d8].reshape(d), g0[:, d + d8:].reshape(2 * d)
    cc = jnp.concatenate([c_all, c_ctx[None], jnp.zeros((16 - N_DEV - 1, d), f32)])
    sc_all = _whole(_f_silu, [cc], [((16, d), f32)], "silu_c")[0]
    my_cols = lambda a: lax.dynamic_slice_in_dim(a, me * cols, cols, axis=a.ndim - 1)
    mod_loc = jnp.stack([_mm(sc_all, ada_w[l], "nn", f32, "ada_mod%d" % l) for l in range(2)])
    mod_loc = mod_loc + my_cols(ada_b)[:, None, :]
    mg = _exchange(mod_loc.reshape(32, cols), True, "gather_mod")
    mod_all = mg.reshape(N_DEV, 2, 16, cols).transpose(1, 2, 0, 3).reshape(2, 16, N_DEV * cols)
    mod_x = lax.dynamic_index_in_dim(mod_all, me, axis=1, keepdims=False)
    mods = jnp.stack([mod_x, mod_all[:, N_DEV]], axis=1).reshape(2, 2, 6, d)

    wts = _Weights(w, lambda parts: _pack(list(parts.values()), f32))
    sm = dict(norm_mix=norm_mix, norm_ffn=norm_ffn, rpb=attn_rpb[0], sink=attn_sink[0], a_re=ssm_a_re[0],
              a_im=ssm_a_im[0], log_dt=ssm_log_dt[0], b_re=ssm_b_re[0], b_im=ssm_b_im[0], c_re=ssm_c_re[0],
              c_im=ssm_c_im[0], ssm_d=d_full, b_glu=bglu_full, norm_final=norm_final)

    xa = jnp.concatenate([x[0], ctx[0]], axis=0)
    d_xa, late, dmods = _local_step(xa, loss_target[0], mods, wts, sm, n_x)

    late_names = ("norm_mix", "norm_ffn", "rpb", "sink")
    parts = [late[n] for n in late_names] + [dmods[:, 0].reshape(2, 6 * d), dmods[:, 1].reshape(2, 6 * d)]
    shapes = [q.shape for q in parts]
    late_gather = wts.post(_pack(parts, f32), True)
    grads = {n: wts.reduced(n) for n in SHARDED_BIG}
    sg = wts.finish(late_gather)
    tot = _unpack(_sum0(sg, "sum_small"), shapes)
    early = _unpack(_sum0(wts.finish(wts.early_gather), "sum_small_early"), list(wts.early_shapes.values()))
    ts = dict(zip(late_names, tot[:len(late_names)]), **dict(zip(wts.early_shapes, early)))
    loss = ts["loss"][0]
    tot_dmod_x, tot_dmod_c = tot[-2], tot[-1]
    row_x = sum(_rows_of(s) for s in shapes[:-2])
    dmod_x_all = sg[:, row_x:row_x + _rows_of(shapes[-2])].reshape(N_DEV, -1)[:, :12 * d].reshape(N_DEV, 2, 6 * d)
    dm = jnp.concatenate([dmod_x_all.transpose(1, 0, 2), tot_dmod_c[:, None, :],
                          jnp.zeros((2, 16 - N_DEV - 1, 6 * d), f32)], axis=1)
    dm_loc = my_cols(dm)
    grads["ada_w"] = jnp.stack([_mm(sc_all, dm_loc[l], "tn", f32, "ada_w_grad%d" % l) for l in range(2)])
    grads["ada_b"] = tot_dmod_x + tot_dmod_c
    dsc_part = _mm(dm_loc[0], ada_w[0], "nt", f32, "silu_c_grad", a2=dm_loc[1], b2=ada_w[1])[N_DEV:N_DEV + 1]
    dsc = _sum0(_exchange(dsc_part, True, "gather_cctx"), "sum_cctx")
    grads["c_ctx"] = _whole_bwd(_f_silu, [c_ctx[None]], [dsc], "silu_cctx_bwd")[0][0]
    grads.update(norm_mix=ts["norm_mix"], norm_ffn=ts["norm_ffn"], attn_rpb=ts["rpb"][None],
                 attn_sink=ts["sink"][None], ssm_a_re=ts["a_re"][None], ssm_a_im=ts["a_im"][None],
                 ssm_log_dt=ts["log_dt"][None], ssm_b_re=ts["b_re"][None], ssm_b_im=ts["b_im"][None],
                 ssm_c_re=ts["c_re"][None], ssm_c_im=ts["c_im"][None],
                 ssm_d=lax.dynamic_slice_in_dim(ts["ssm_d"], me * d8, d8)[None],
                 ssm_b_glu=lax.dynamic_slice_in_dim(ts["b_glu"], me * 2 * d8, 2 * d8)[None],
                 norm_final=ts["norm_final"])

    delta, new_m, new_v = {}, {}, {}
    for n in BIG:
        two_d = lambda a: a.reshape(-1, a.shape[-1])
        dl, m2, v2 = _adamw(two_d(w[n]), two_d(grads[n]), two_d(p["m_" + n]), two_d(p["v_" + n]), "adamw_" + n)
        delta[n], new_m[n], new_v[n] = (t.reshape(w[n].shape) for t in (dl, m2, v2))
    sshapes = [w[n].shape for n in SMALL]
    packs = [_pack([src[n] for n in SMALL], f32) for src in
             (w, grads, {n: p["m_" + n] for n in SMALL}, {n: p["v_" + n] for n in SMALL})]
    for store, t in zip((delta, new_m, new_v), _adamw(*packs, "adamw_small")):
        store.update(zip(SMALL, _unpack(t, sshapes)))

    return (loss, d_xa[None], *[grads[n] for n in WEIGHTS], *[delta[n] for n in WEIGHTS],
            *[new_m[n] for n in WEIGHTS], *[new_v[n] for n in WEIGHTS])
```

```python
import functools
import math

import numpy as np
import jax
import jax.numpy as jnp
from jax import lax
from jax.experimental import pallas as pl
from jax.experimental.pallas import tpu as pltpu

f32, bf16 = jnp.float32, jnp.bfloat16

HEAD_DIM = 128
GRID_W = 64
NA_HEADS = 8
NB_Q_HEADS = 8
NB_KV_HEADS = 2
NB_GROUP = NB_Q_HEADS // NB_KV_HEADS
NA_ROWS = 8
NA_COLS = 16
SW_RADIUS = 128
ROPE_BASE = 10000.0
SSM_GROUP = 16
SSM_STATE = 64
EPS = 1e-6
NEG_INF = -1e30
ADAM_LR, ADAM_B1, ADAM_B2, ADAM_EPS, ADAM_WD, ADAM_STEP = 0.001, 0.9, 0.999, 1e-08, 0.01, 10

N_DEV = 8
BLK = 256
SCAN_SEG = 8
GROUPS_PER_CHUNK = 16
V7X_VMEM_LIMIT = 56 * 2 ** 20
MM_VMEM_BUDGET = 36 * 2 ** 20
MXU_FLOPS_PER_US = 7e8
CARRIER_US = dict(na_fwd=300, swa_fwd=280, na_bwd=500, swa_bwd=300, s5_fwd=540, s5_bwd=1100)

NT = (((1,), (1,)), ((), ()))
TN = (((0,), (0,)), ((), ()))


def _params(sem, side_effects=False):
    return pltpu.CompilerParams(dimension_semantics=sem, vmem_limit_bytes=V7X_VMEM_LIMIT,
                                has_side_effects=side_effects)


def _pick(n, cands):
    for c in cands:
        if n % c == 0:
            return c
    return n


MESH = pl.DeviceIdType.MESH
MIN_CHUNK_BYTES = 512 * 1024
EXCHANGE_BYTES_PER_US = 74e3
RIDER_OVERHANG = 1.1


def _peers():
    x, y, c = lax.axis_index("x"), lax.axis_index("y"), lax.axis_index("c")
    return 4 * x + 2 * y + c, lambda o: (x ^ (o >> 2), y ^ ((o >> 1) & 1), c ^ (o & 1))


class _Exchange:
    def __init__(self, src, gather):
        self.src, self.gather, self.result = src, gather, None
        self.out_shape = (N_DEV,) + src.shape if gather else src.shape
        slab = self.out_shape[1:]
        slab_bytes = int(np.prod(slab)) * src.dtype.itemsize
        self.nch = next((k for k in (4, 2) if slab[0] % (16 * k) == 0 and slab_bytes // k >= MIN_CHUNK_BYTES), 1)
        self.step = slab[0] // self.nch
        self.cost_us = (N_DEV - 1) * slab_bytes / EXCHANGE_BYTES_PER_US

    def scratch(self):
        return [pltpu.SemaphoreType.DMA((N_DEV - 1, self.nch)), pltpu.SemaphoreType.DMA((N_DEV - 1, self.nch)),
                pltpu.SemaphoreType.DMA(())]

    def _copies(self, src_ref, out_ref, send_sems, recv_sems, local_sem):
        me, peer = _peers()
        local = pltpu.make_async_copy(src_ref if self.gather else src_ref.at[me], out_ref.at[me], local_sem)

        def copy(o, k, src_idx, dst_idx):
            rows = pl.ds(k * self.step, self.step)
            return pltpu.make_async_remote_copy(
                src_ref=(src_ref if self.gather else src_ref.at[src_idx]).at[rows],
                dst_ref=out_ref.at[dst_idx].at[rows],
                send_sem=send_sems.at[o - 1, k], recv_sem=recv_sems.at[o - 1, k],
                device_id=peer(o), device_id_type=MESH)

        pairs = [(o, k) for k in range(self.nch) for o in range(1, N_DEV)]
        sends = [copy(o, k, me ^ o, me) for o, k in pairs]
        recvs = [copy(o, k, me, me ^ o) for o, k in pairs]
        return local, sends, recvs

    def start(self, *refs):
        local, sends, _ = self._copies(*refs)
        local.start()
        for cp in sends:
            cp.start()

    def wait(self, *refs):
        local, sends, recvs = self._copies(*refs)
        for cp in recvs:
            cp.wait_recv()
        for cp in sends:
            cp.wait_send()
        local.wait()


def _run_exchange(ex, name):
    def body(src_ref, out_ref, *sems):
        ex.start(src_ref, out_ref, *sems)
        ex.wait(src_ref, out_ref, *sems)

    ex.result = pl.pallas_call(
        body, name=name, out_shape=jax.ShapeDtypeStruct(ex.out_shape, ex.src.dtype),
        in_specs=[pl.BlockSpec(memory_space=pl.ANY)], out_specs=pl.BlockSpec(memory_space=pl.ANY),
        scratch_shapes=ex.scratch(), compiler_params=pltpu.CompilerParams(has_side_effects=True),
    )(ex.src)
    return ex.result


def _call(body, args, *, name, grid, in_specs, out_specs, out_shape, scratch_shapes=(), sem, rider=None):
    if not rider:
        return pl.pallas_call(body, name=name, grid=grid, in_specs=in_specs, out_specs=out_specs,
                              out_shape=out_shape, scratch_shapes=list(scratch_shapes),
                              compiler_params=_params(sem))(*args)
    n_in, n_out, n_sc, n_ex = len(in_specs), len(out_specs), len(scratch_shapes), len(rider)

    def wrapped(*refs):
        ins, ex_src = refs[:n_in], refs[n_in:n_in + n_ex]
        outs = refs[n_in + n_ex:n_in + n_ex + n_out]
        ex_out = refs[n_in + n_ex + n_out:n_in + 2 * n_ex + n_out]
        scr = refs[n_in + 2 * n_ex + n_out:n_in + 2 * n_ex + n_out + n_sc]
        sems = refs[n_in + 2 * n_ex + n_out + n_sc:]
        ids = [pl.program_id(a) for a in range(len(grid))]
        first = functools.reduce(jnp.logical_and, [i == 0 for i in ids])
        last = functools.reduce(jnp.logical_and, [i == g - 1 for i, g in zip(ids, grid)])

        @pl.when(first)
        def _():
            for e, ex in enumerate(rider):
                ex.start(ex_src[e], ex_out[e], *sems[3 * e:3 * e + 3])

        body(*ins, *outs, *scr)

        @pl.when(last)
        def _():
            for e, ex in enumerate(rider):
                ex.wait(ex_src[e], ex_out[e], *sems[3 * e:3 * e + 3])

    anyspace = pl.BlockSpec(memory_space=pl.ANY)
    res = pl.pallas_call(
        wrapped, name=name, grid=grid, in_specs=list(in_specs) + [anyspace] * n_ex,
        out_specs=list(out_specs) + [anyspace] * n_ex,
        out_shape=list(out_shape) + [jax.ShapeDtypeStruct(ex.out_shape, ex.src.dtype) for ex in rider],
        scratch_shapes=list(scratch_shapes) + [s for ex in rider for s in ex.scratch()],
        compiler_params=_params(("arbitrary",) * len(grid), side_effects=True),
    )(*args, *[ex.src for ex in rider])
    for ex, r in zip(rider, res[n_out:]):
        ex.result = r
    return list(res[:n_out])


class _Comm:
    def __init__(self):
        self.queue, self.count = [], 0

    def post(self, src, gather):
        ex = _Exchange(src, gather)
        self.queue.append(ex)
        return ex

    def take(self, budget_us):
        rider = []
        for ex in list(self.queue):
            if budget_us > 0 and ex.cost_us <= RIDER_OVERHANG * budget_us:
                budget_us -= ex.cost_us
                rider.append(ex)
                self.queue.remove(ex)
        return rider

    def finish(self, ex):
        while ex.result is None:
            self.count += 1
            _run_exchange(self.queue.pop(0), "exchange%d" % self.count)
        return ex.result


def _mm(a, b, mode, out_dtype, name, a2=None, b2=None, comm=None, gated=None):
    if mode == "tn":
        kdim, m, n = a.shape[0], a.shape[1], b.shape[1]
    elif mode == "nt":
        m, kdim, n = a.shape[0], a.shape[1], b.shape[0]
    else:
        m, kdim, n = a.shape[0], a.shape[1], b.shape[1]
    if mode == "tn":
        tm = _pick(m, (2048, 1024, 512, 256, 128))
        tk = _pick(kdim, (1024, 768, 512, 256, 128))
    else:
        tm = _pick(m, (1024, 768, 512, 256, 128))
        tk = _pick(kdim, (2048, 1024, 768, 512, 256, 128))
    in_bytes = max(a.dtype.itemsize, b.dtype.itemsize) * (1 if a2 is None else 2)
    out_bytes = jnp.dtype(out_dtype).itemsize

    def vmem(tn_):
        return 2 * in_bytes * tk * (tm + tn_) + (4 + 2 * out_bytes + (0 if gated is None else 16)) * tm * tn_

    tn = next((t for t in (2048, 1024, 512, 256, 128) if n % t == 0 and vmem(t) <= MM_VMEM_BUDGET), n)
    nk = kdim // tk
    if mode == "tn":
        a_spec = pl.BlockSpec((tk, tm), lambda i, j, k: (k, i))
        b_spec = pl.BlockSpec((tk, tn), lambda i, j, k: (k, j))
        dims = TN
    elif mode == "nt":
        a_spec = pl.BlockSpec((tm, tk), lambda i, j, k: (i, k))
        b_spec = pl.BlockSpec((tn, tk), lambda i, j, k: (j, k))
        dims = NT
    else:
        a_spec = pl.BlockSpec((tm, tk), lambda i, j, k: (i, k))
        b_spec = pl.BlockSpec((tk, tn), lambda i, j, k: (k, j))
        dims = (((1,), (0,)), ((), ()))

    pairs = 1 if a2 is None else 2
    n_in = 2 * pairs + (0 if gated is None else 2)

    def body(*refs):
        outs, acc_ref = refs[n_in:-1], refs[-1]
        i, k = pl.program_id(0), pl.program_id(2)

        @pl.when(k == 0)
        def _():
            acc_ref[...] = jnp.zeros_like(acc_ref)

        for p in range(pairs):
            acc_ref[...] += lax.dot_general(refs[2 * p][...].astype(bf16), refs[2 * p + 1][...].astype(bf16), dims,
                                            preferred_element_type=f32)

        @pl.when(k == nk - 1)
        def _():
            acc = acc_ref[...]
            outs[0][...] = acc.astype(outs[0].dtype)
            if gated is not None:
                res_ref, gate_ref = refs[2 * pairs], refs[2 * pairs + 1]
                gate = gate_ref[0]
                if gate_ref.shape[0] == 2:
                    row = i * tm + lax.broadcasted_iota(jnp.int32, (tm, 1), 0)
                    gate = jnp.where(row >= gated[2], gate_ref[1], gate)
                outs[1][...] = res_ref[...] + gate * acc

    args = (a, b) if a2 is None else (a, b, a2, b2)
    ospec = pl.BlockSpec((tm, tn), lambda i, j, k: (i, j))
    in_specs, out_specs = [a_spec, b_spec] * pairs, [ospec]
    out_shape = [jax.ShapeDtypeStruct((m, n), out_dtype)]
    if gated is not None:
        res, gate, _ = gated
        args = args + (res, gate)
        in_specs += [ospec, pl.BlockSpec((gate.shape[0], 1, tn), lambda i, j, k: (0, 0, j))]
        out_specs.append(ospec)
        out_shape.append(jax.ShapeDtypeStruct((m, n), f32))
    rider = comm.take(2.0 * pairs * m * n * kdim / MXU_FLOPS_PER_US) if comm else None
    res = _call(
        body, args, name=name, grid=(m // tm, n // tn, nk), in_specs=in_specs, out_specs=out_specs,
        out_shape=out_shape, scratch_shapes=[pltpu.VMEM((tm, tn), f32)],
        sem=("parallel", "parallel", "arbitrary"), rider=rider)
    return res[0] if gated is None else res


def _ffn_up(h, w1, w3, name, comm=None):
    m, kdim = h.shape
    n = w1.shape[1]
    tm, tn = _pick(m, (1024, 768, 512, 256, 128)), _pick(n, (512, 256, 128))

    def body(h_ref, w1_ref, w3_ref, a_ref, b_ref, u_ref):
        hv = h_ref[...]
        a = jnp.dot(hv, w1_ref[...], preferred_element_type=f32).astype(bf16)
        b = jnp.dot(hv, w3_ref[...], preferred_element_type=f32).astype(bf16)
        a_ref[...] = a
        b_ref[...] = b
        u_ref[...] = (jax.nn.silu(a.astype(f32)) * b.astype(f32)).astype(bf16)

    wspec = pl.BlockSpec((kdim, tn), lambda i, j: (0, j))
    ospec = pl.BlockSpec((tm, tn), lambda i, j: (i, j))
    rider = comm.take(4.0 * m * n * kdim / MXU_FLOPS_PER_US) if comm else None
    return _call(
        body, (h, w1, w3), name=name, grid=(m // tm, n // tn),
        in_specs=[pl.BlockSpec((tm, kdim), lambda i, j: (i, 0)), wspec, wspec], out_specs=[ospec] * 3,
        out_shape=[jax.ShapeDtypeStruct((m, n), bf16)] * 3, sem=("parallel", "parallel"), rider=rider)


def _ffn_down_bwd(g, w2, a, b, name, comm=None):
    m, kdim = g.shape
    n = w2.shape[0]
    tm, tn = _pick(m, (1024, 768, 512, 256, 128)), _pick(n, (512, 256, 128))

    def body(g_ref, w_ref, a_ref, b_ref, da_ref, db_ref):
        du = lax.dot_general(g_ref[...], w_ref[...], NT, preferred_element_type=f32)
        _, vjp = jax.vjp(lambda p, q: jax.nn.silu(p) * q, a_ref[...].astype(f32), b_ref[...].astype(f32))
        da, db = vjp(du)
        da_ref[...] = da.astype(bf16)
        db_ref[...] = db.astype(bf16)

    ospec = pl.BlockSpec((tm, tn), lambda i, j: (i, j))
    rider = comm.take(2.0 * m * n * kdim / MXU_FLOPS_PER_US) if comm else None
    return _call(
        body, (g, w2, a, b), name=name, grid=(m // tm, n // tn),
        in_specs=[pl.BlockSpec((tm, kdim), lambda i, j: (i, 0)), pl.BlockSpec((tn, kdim), lambda i, j: (j, 0)),
                  ospec, ospec],
        out_specs=[ospec] * 2, out_shape=[jax.ShapeDtypeStruct((m, n), bf16)] * 2,
        sem=("parallel", "parallel"), rider=rider)


def _small_dot(a, b, name):
    def body(a_ref, b_ref, o_ref):
        o_ref[...] = jnp.dot(a_ref[...], b_ref[...], precision=lax.Precision.HIGHEST, preferred_element_type=f32)

    return pl.pallas_call(body, name=name, out_shape=jax.ShapeDtypeStruct((a.shape[0], b.shape[1]), f32))(a, b)


def _group_of(i, n_x_tiles, n_groups):
    return jnp.where(i >= n_x_tiles, n_groups - 1, 0)


def _rowwise(f, rows, vecs, outs, *, nrows, n_x_rows, name, tm=BLK, tc=None):
    n_x_tiles = n_x_rows // tm
    grid = (nrows // tm,) if tc is None else (nrows // tm, rows[0].shape[1] // tc)

    def rspec(cols):
        if tc is None:
            return pl.BlockSpec((tm, cols), lambda i: (i, 0))
        return pl.BlockSpec((tm, tc), lambda i, j: (i, j))

    def vspec(v):
        g = v.shape[0]
        if tc is None:
            return pl.BlockSpec((None, 1, v.shape[2]), lambda i: (_group_of(i, n_x_tiles, g), 0, 0))
        return pl.BlockSpec((None, 1, tc), lambda i, j: (_group_of(i, n_x_tiles, g), 0, j))

    nr, nv = len(rows), len(vecs)

    def body(*refs):
        ins = [r[...] for r in refs[:nr + nv]]
        res = f(*ins)
        for o_ref, val in zip(refs[nr + nv:], res):
            o_ref[...] = val.astype(o_ref.dtype)

    return pl.pallas_call(
        body, name=name, grid=grid,
        in_specs=[rspec(r.shape[1]) for r in rows] + [vspec(v) for v in vecs],
        out_specs=[rspec(c) for c, _ in outs],
        out_shape=[jax.ShapeDtypeStruct((nrows, c), d) for c, d in outs],
        compiler_params=_params(("parallel",) * len(grid)),
    )(*rows, *vecs)


def _rowwise_bwd(f, rows, vecs, cts, row_grads, *, nrows, n_x_rows, name, tm=BLK, tc=None, residual=None,
                 residual_rows=None, latent_grads_only=False):
    n_x_tiles = n_x_rows // tm
    n_tiles = nrows // tm
    out_rows = n_x_rows if latent_grads_only else nrows
    res_tiles = None if residual_rows is None else residual_rows // tm
    grid = (n_tiles,) if tc is None else (rows[0].shape[1] // tc, n_tiles)
    row_of = (lambda *g: g[0]) if tc is None else (lambda *g: g[1])

    def rspec(cols):
        if tc is None:
            return pl.BlockSpec((tm, cols), lambda i: (i, 0))
        return pl.BlockSpec((tm, tc), lambda j, i: (i, j))

    def vspec(v):
        g = v.shape[0]
        if tc is None:
            return pl.BlockSpec((None, 1, v.shape[2]), lambda i: (_group_of(i, n_x_tiles, g), 0, 0))
        return pl.BlockSpec((None, 1, tc), lambda j, i: (_group_of(i, n_x_tiles, g), 0, j))

    nr, nv, nc = len(rows), len(vecs), len(cts)
    gidx = sorted(row_grads)

    def body(*refs):
        i = row_of(*[pl.program_id(d) for d in range(len(grid))])
        row_vals = [r[...].astype(f32) for r in refs[:nr]]
        vec_vals = [jnp.broadcast_to(r[...].astype(f32), (tm, r.shape[-1])) for r in refs[nr:nr + nv]]
        ct_vals = [r[...].astype(f32) for r in refs[nr + nv:nr + nv + nc]]
        out_refs = refs[nr + nv + nc + (residual is not None):]
        res, vjp = jax.vjp(lambda *a: tuple(o.astype(f32) for o in f(*a)), *row_vals, *vec_vals)
        grads = list(vjp(tuple(ct_vals)))
        if residual is not None:
            extra = refs[nr + nv + nc][...]
            if res_tiles is not None:
                extra = jnp.where(i < res_tiles, extra, 0.0)
            grads[0] = grads[0] + extra
        for o_ref, k in zip(out_refs[:len(gidx)], gidx):
            if latent_grads_only:
                @pl.when(i < n_x_tiles)
                def _():
                    o_ref[...] = grads[k].astype(o_ref.dtype)
            else:
                o_ref[...] = grads[k].astype(o_ref.dtype)
        for o_ref, g, v in zip(out_refs[len(gidx):], grads[nr:], vecs):
            part = jnp.sum(g, axis=0, keepdims=True)
            first = (i == 0) if v.shape[0] == 1 else ((i == 0) | (i == n_x_tiles))

            @pl.when(first)
            def _():
                o_ref[...] = part

            @pl.when(jnp.logical_not(first))
            def _():
                o_ref[...] += part

    sem = ("arbitrary",) if tc is None else ("parallel", "arbitrary")
    gspec = rspec
    if latent_grads_only:
        assert tc is None
        gspec = lambda cols: pl.BlockSpec((tm, cols), lambda i: (jnp.minimum(i, n_x_tiles - 1), 0))
    res_specs, res_args = [], []
    if residual is not None:
        assert tc is None
        clamp = (lambda i: i) if res_tiles is None else (lambda i: jnp.minimum(i, res_tiles - 1))
        res_specs = [pl.BlockSpec((tm, residual.shape[1]), lambda i: (clamp(i), 0))]
        res_args = [residual]
    return pl.pallas_call(
        body, name=name, grid=grid,
        in_specs=[rspec(r.shape[1]) for r in rows] + [vspec(v) for v in vecs] + [rspec(c.shape[1]) for c in cts]
        + res_specs,
        out_specs=[gspec(rows[k].shape[1]) for k in gidx] + [vspec(v) for v in vecs],
        out_shape=[jax.ShapeDtypeStruct((out_rows, rows[k].shape[1]), row_grads[k]) for k in gidx]
        + [jax.ShapeDtypeStruct(v.shape, f32) for v in vecs],
        compiler_params=_params(sem),
    )(*rows, *vecs, *cts, *res_args)


def _rms(x, w):
    return x * lax.rsqrt(jnp.mean(x * x, axis=-1, keepdims=True) + EPS) * w


def _f_normmod(x, w, sc, sh):
    return (_rms(x.astype(f32), w) * (1.0 + sc) + sh,)


def _f_gated_add(x, y, g):
    return (x + g * y.astype(f32),)


def _f_glu_post(x, zv, zg, g, bv, bg):
    return (x + g * ((zv.astype(f32) + bv) * jax.nn.sigmoid(zg.astype(f32) + bg)),)


def _f_silu(x):
    return (jax.nn.silu(x.astype(f32)),)


def _loss_head(x, tgt, w, *, nrows, name):
    d = x.shape[1]
    tm = BLK

    def body(x_ref, t_ref, w_ref, dx_ref, dw_ref, loss_ref):
        i = pl.program_id(0)
        wb = jnp.broadcast_to(w_ref[...], (tm, d))
        y, vjp = jax.vjp(_rms, x_ref[...], wb)
        e = y - t_ref[...]
        dx, dwb = vjp(e * (1.0 / d))
        dx_ref[...] = dx
        dw = jnp.sum(dwb, axis=0, keepdims=True)
        part = jnp.full((8, 128), 0.5 / d, f32) * jnp.sum(e * e)

        @pl.when(i == 0)
        def _():
            dw_ref[...] = dw
            loss_ref[...] = part

        @pl.when(i > 0)
        def _():
            dw_ref[...] += dw
            loss_ref[...] += part

    row = pl.BlockSpec((tm, d), lambda i: (i, 0))
    return pl.pallas_call(
        body, name=name, grid=(nrows // tm,),
        in_specs=[row, row, pl.BlockSpec((1, d), lambda i: (0, 0))],
        out_specs=[row, pl.BlockSpec((1, d), lambda i: (0, 0)), pl.BlockSpec((8, 128), lambda i: (0, 0))],
        out_shape=[jax.ShapeDtypeStruct((nrows, d), f32), jax.ShapeDtypeStruct((1, d), f32),
                   jax.ShapeDtypeStruct((8, 128), f32)],
        compiler_params=_params(("arbitrary",)),
    )(x, tgt, w)


def _adamw(w, g, m, v, name):
    r, c = w.shape
    tr = _pick(r, (512, 256, 128, 64, 32, 16, 8))
    tcol = _pick(c, (1024, 512)) if c % 128 == 0 else c

    def body(w_ref, g_ref, m_ref, v_ref, d_ref, m2_ref, v2_ref):
        gg = g_ref[...]
        m2 = ADAM_B1 * m_ref[...] + (1.0 - ADAM_B1) * gg
        v2 = ADAM_B2 * v_ref[...] + (1.0 - ADAM_B2) * (gg * gg)
        m_hat = m2 / (1.0 - ADAM_B1 ** ADAM_STEP)
        v_hat = v2 / (1.0 - ADAM_B2 ** ADAM_STEP)
        d_ref[...] = -ADAM_LR * (m_hat / (jnp.sqrt(v_hat) + ADAM_EPS) + ADAM_WD * w_ref[...])
        m2_ref[...] = m2
        v2_ref[...] = v2

    spec = pl.BlockSpec((tr, tcol), lambda i, j: (i, j))
    return pl.pallas_call(
        body, name=name, grid=(r // tr, c // tcol), in_specs=[spec] * 4, out_specs=[spec] * 3,
        out_shape=[jax.ShapeDtypeStruct((r, c), f32)] * 3,
        compiler_params=_params(("parallel", "parallel")),
    )(w, g, m, v)


def _swap_quarters(x):
    lane = lax.broadcasted_iota(jnp.int32, x.shape, 1)
    return jnp.where((lane & 63) < 32, pltpu.roll(x, 96, 1), pltpu.roll(x, 32, 1))


def _rope_tables(n_x, n_ctx):
    t = np.arange(n_x)
    quarter = HEAD_DIM // 4
    inv = ROPE_BASE ** (-np.arange(quarter, dtype=np.float64) / quarter)
    ar = (t // GRID_W)[:, None] * inv[None]
    ac = (t % GRID_W)[:, None] * inv[None]
    cos = np.concatenate([np.cos(ar), np.cos(ar), np.cos(ac), np.cos(ac)], axis=1)
    sin = np.concatenate([-np.sin(ar), np.sin(ar), -np.sin(ac), np.sin(ac)], axis=1)
    cos = np.concatenate([cos, np.ones((n_ctx, HEAD_DIM))], axis=0)
    sin = np.concatenate([sin, np.zeros((n_ctx, HEAD_DIM))], axis=0)
    return jnp.asarray(cos, f32), jnp.asarray(sin, f32)


A_W = NA_HEADS * HEAD_DIM
QB0, KB0, VB0 = 3 * A_W, 3 * A_W + NB_Q_HEADS * HEAD_DIM, 3 * A_W + (NB_Q_HEADS + NB_KV_HEADS) * HEAD_DIM
IN_W = VB0 + NB_KV_HEADS * HEAD_DIM


def _qkv_post(qkv, cos, sin, name):
    n = qkv.shape[0]

    def body(x_ref, c_ref, s_ref, o_ref):
        c, s = c_ref[...], s_ref[...]
        o_ref[:, :QB0] = x_ref[:, :QB0].astype(bf16)
        for col in range(QB0, VB0, HEAD_DIM):
            x = x_ref[:, col:col + HEAD_DIM]
            o_ref[:, col:col + HEAD_DIM] = (x * c + _swap_quarters(x) * s).astype(bf16)
        o_ref[:, VB0:] = x_ref[:, VB0:].astype(bf16)

    row = lambda c: pl.BlockSpec((BLK, c), lambda i: (i, 0))
    return pl.pallas_call(
        body, name=name, grid=(n // BLK,), in_specs=[row(IN_W), row(HEAD_DIM), row(HEAD_DIM)],
        out_specs=row(IN_W), out_shape=jax.ShapeDtypeStruct((n, IN_W), bf16),
        compiler_params=_params(("parallel",)),
    )(qkv, cos, sin)


def _qkv_post_bwd(parts, cos, sin, name):
    n = parts[0].shape[0]

    def body(qa, ka, va, qb, kb, vb, c_ref, s_ref, o_ref):
        c, s = c_ref[...], s_ref[...]
        o_ref[:, 0:A_W] = qa[...].astype(bf16)
        o_ref[:, A_W:2 * A_W] = ka[...].astype(bf16)
        o_ref[:, 2 * A_W:QB0] = va[...].astype(bf16)
        for src, col0, width in ((qb, QB0, KB0 - QB0), (kb, KB0, VB0 - KB0)):
            for off in range(0, width, HEAD_DIM):
                g = src[:, off:off + HEAD_DIM].astype(f32)
                o_ref[:, col0 + off:col0 + off + HEAD_DIM] = (g * c + _swap_quarters(g * s)).astype(bf16)
        o_ref[:, VB0:] = vb[...].astype(bf16)

    row = lambda c: pl.BlockSpec((BLK, c), lambda i: (i, 0))
    return pl.pallas_call(
        body, name=name, grid=(n // BLK,),
        in_specs=[row(p.shape[1]) for p in parts] + [row(HEAD_DIM), row(HEAD_DIM)],
        out_specs=row(IN_W), out_shape=jax.ShapeDtypeStruct((n, IN_W), bf16),
        compiler_params=_params(("parallel",)),
    )(*parts, cos, sin)


def _valid(kind, qpos, kpos, n_x):
    ok = (kpos >= 0) & (kpos < n_x) & (qpos >= 0) & (qpos < n_x)
    if kind == "na":
        rows = n_x // GRID_W
        qr, qc = lax.shift_right_arithmetic(qpos, 6), qpos & (GRID_W - 1)
        kr, kc = lax.shift_right_arithmetic(kpos, 6), kpos & (GRID_W - 1)
        kr0 = jnp.clip(qr - NA_ROWS // 2, 0, rows - NA_ROWS)
        ws = jnp.clip(qc - NA_COLS // 2, 0, GRID_W - NA_COLS)
        return ok & (kr >= kr0) & (kr < kr0 + NA_ROWS) & (kc >= ws) & (kc < ws + NA_COLS)
    return ok & (jnp.abs(kpos - qpos) <= SW_RADIUS)


def _stack_heads(x, g):
    if g == 1:
        return x
    return jnp.concatenate([x[:, a * HEAD_DIM:(a + 1) * HEAD_DIM] for a in range(g)], axis=0)


def _unstack_heads(x, g):
    if g == 1:
        return x
    r = x.shape[0] // g
    return jnp.concatenate([x[a * r:(a + 1) * r] for a in range(g)], axis=1)


def _add_per_head(s, m, g):
    if g == 1:
        return s + m
    return (s.reshape((g,) + m.shape) + m[None]).reshape(s.shape)


class _AttnCfg:
    def __init__(self, kind, n_x, n_tot):
        self.kind, self.n_x, self.n_tot = kind, n_x, n_tot
        self.n_xb, self.n_blk = n_x // BLK, n_tot // BLK
        if kind == "na":
            self.g, self.nkv, self.q0, self.k0, self.v0 = 1, NA_HEADS, 0, NA_HEADS, 2 * NA_HEADS
        else:
            self.g, self.nkv = NB_GROUP, NB_KV_HEADS
            self.q0, self.k0, self.v0 = QB0 // (NB_GROUP * HEAD_DIM), KB0 // HEAD_DIM, VB0 // HEAD_DIM
        self.r = BLK * self.g
        self.qw = HEAD_DIM * self.g
        self.scale = HEAD_DIM ** -0.5
        last = self.n_xb - 1
        rows = lax.broadcasted_iota(jnp.int32, (BLK, 3 * BLK), 0)
        cols = lax.broadcasted_iota(jnp.int32, (BLK, 3 * BLK), 1)
        self.mask_q = jnp.stack([jnp.where(_valid(kind, i * BLK + rows, (i - 1) * BLK + cols, n_x), 0.0, NEG_INF)
                                 for i in (0, 1, last, self.n_xb)]).astype(f32)
        self.pattern = lambda i: jnp.where(i == 0, 0, jnp.where(i < last, 1, jnp.where(i == last, 2, 3)))


def _attn_fwd(qkv, cfg, extra, name, rider=None):
    g, r, qw, n_xb, n_x = cfg.g, cfg.r, cfg.qw, cfg.n_xb, cfg.n_x
    last = n_xb - 1

    def body(q_ref, kp, ko, kn, vp, vo, vn, kc_ref, vc_ref, ex_ref, m_ref, o_ref, lse_ref):
        q = _stack_heads(q_ref[...], g)
        kw = jnp.concatenate([kp[...], ko[...], kn[...]], axis=0)
        vw = jnp.concatenate([vp[...], vo[...], vn[...]], axis=0)
        s = lax.dot_general(q, kw, NT, preferred_element_type=f32) * cfg.scale
        if cfg.kind == "na":
            s = s + ex_ref[...]
        s = _add_per_head(s, m_ref[...], g)
        sc = lax.dot_general(q, kc_ref[...], NT, preferred_element_type=f32) * cfg.scale
        m = jnp.maximum(jnp.max(s, axis=-1, keepdims=True), jnp.max(sc, axis=-1, keepdims=True))
        if cfg.kind == "swa":
            m = jnp.maximum(m, ex_ref[...])
        p, pc = jnp.exp(s - m), jnp.exp(sc - m)
        l = jnp.sum(p, axis=-1, keepdims=True) + jnp.sum(pc, axis=-1, keepdims=True)
        if cfg.kind == "swa":
            l = l + jnp.exp(ex_ref[...] - m)
        o = jnp.dot(p.astype(bf16), vw, preferred_element_type=f32) + jnp.dot(pc.astype(bf16), vc_ref[...],
                                                                             preferred_element_type=f32)
        o_ref[...] = _unstack_heads(o / l, g).astype(bf16)
        lse_ref[...] = m + jnp.log(l)

    kv = lambda col0, f: pl.BlockSpec((BLK, HEAD_DIM), lambda h, i: (f(i), col0 + h))
    prev = lambda i: jnp.clip(i - 1, 0, last)
    own = lambda i: jnp.minimum(i, last)
    nxt = lambda i: jnp.minimum(i + 1, last)
    ctx = lambda i: n_xb
    if cfg.kind == "na":
        ex_spec = pl.BlockSpec((None, BLK, 3 * BLK), lambda h, i: (h, 0, 0))
    else:
        ex_spec = pl.BlockSpec((None, r, 1), lambda h, i: (h, 0, 0))
    m_spec = pl.BlockSpec((None, BLK, 3 * BLK), lambda h, i: (cfg.pattern(i), 0, 0))
    return _call(
        body, (qkv,) * 9 + (extra, cfg.mask_q), name=name, grid=(cfg.nkv, cfg.n_blk),
        in_specs=[pl.BlockSpec((BLK, qw), lambda h, i: (i, cfg.q0 + h)),
                  kv(cfg.k0, prev), kv(cfg.k0, own), kv(cfg.k0, nxt),
                  kv(cfg.v0, prev), kv(cfg.v0, own), kv(cfg.v0, nxt),
                  kv(cfg.k0, ctx), kv(cfg.v0, ctx), ex_spec, m_spec],
        out_specs=[pl.BlockSpec((BLK, qw), lambda h, i: (i, h)),
                   pl.BlockSpec((None, None, r, 1), lambda h, i: (h, i, 0, 0))],
        out_shape=[jax.ShapeDtypeStruct((cfg.n_tot, cfg.nkv * qw), bf16),
                   jax.ShapeDtypeStruct((cfg.nkv, cfg.n_blk, r, 1), f32)],
        sem=("parallel", "parallel"), rider=rider)


def _attn_bwd(qkv, o, do, lse, cfg, extra, name, rider=None):
    g, r, qw, n_xb, n_x = cfg.g, cfg.r, cfg.qw, cfg.n_xb, cfg.n_x
    last = n_xb - 1
    do_col0 = 0 if cfg.kind == "na" else (NA_HEADS * HEAD_DIM) // qw

    def body(q_ref, kp, ko, kn, vp, vo, vn, kc_ref, vc_ref, ex_ref, o_ref, do_ref, lse_ref, m_ref,
             dq_ref, delta_ref, dex_ref, dk_ref, dv_ref):
        i = pl.program_id(1)
        q = _stack_heads(q_ref[...], g)
        dout = _stack_heads(do_ref[...], g)
        out = _stack_heads(o_ref[...], g)
        delta = jnp.sum(dout.astype(f32) * out.astype(f32), axis=-1, keepdims=True)
        delta_ref[...] = delta
        kw = jnp.concatenate([kp[...], ko[...], kn[...]], axis=0)
        vw = jnp.concatenate([vp[...], vo[...], vn[...]], axis=0)
        s = lax.dot_general(q, kw, NT, preferred_element_type=f32) * cfg.scale
        if cfg.kind == "na":
            s = s + ex_ref[...]
        s = _add_per_head(s, m_ref[...], g)
        sc = lax.dot_general(q, kc_ref[...], NT, preferred_element_type=f32) * cfg.scale
        lse_v = lse_ref[...]
        p, pc = jnp.exp(s - lse_v), jnp.exp(sc - lse_v)
        dp = lax.dot_general(dout, vw, NT, preferred_element_type=f32)
        dpc = lax.dot_general(dout, vc_ref[...], NT, preferred_element_type=f32)
        ds, dsc = p * (dp - delta), pc * (dpc - delta)
        ds16, dsc16 = ds.astype(bf16), dsc.astype(bf16)
        dq = jnp.dot(ds16, kw, preferred_element_type=f32) + jnp.dot(dsc16, kc_ref[...], preferred_element_type=f32)
        dq_ref[...] = _unstack_heads(dq * cfg.scale, g).astype(bf16)
        dex = ds if cfg.kind == "na" else -jnp.exp(ex_ref[...] - lse_v) * delta

        @pl.when(i == 0)
        def _():
            dex_ref[...] = dex
            dk_ref[...] = jnp.zeros_like(dk_ref)
            dv_ref[...] = jnp.zeros_like(dv_ref)

        @pl.when(i > 0)
        def _():
            dex_ref[...] += dex

        dkw = lax.dot_general(ds16, q, TN, preferred_element_type=f32) * cfg.scale
        dvw = lax.dot_general(p.astype(bf16), dout, TN, preferred_element_type=f32)
        for w, blk in enumerate((jnp.clip(i - 1, 0, last), jnp.minimum(i, last), jnp.minimum(i + 1, last))):
            rows = pl.ds(pl.multiple_of(blk * BLK, BLK), BLK)
            dk_ref[rows, :] += dkw[w * BLK:(w + 1) * BLK]
            dv_ref[rows, :] += dvw[w * BLK:(w + 1) * BLK]
        crow = pl.ds(n_xb * BLK, BLK)
        dk_ref[crow, :] += lax.dot_general(dsc16, q, TN, preferred_element_type=f32) * cfg.scale
        dv_ref[crow, :] += lax.dot_general(pc.astype(bf16), dout, TN, preferred_element_type=f32)

    kv = lambda col0, f: pl.BlockSpec((BLK, HEAD_DIM), lambda h, i: (f(i), col0 + h))
    prev = lambda i: jnp.clip(i - 1, 0, last)
    own = lambda i: jnp.minimum(i, last)
    nxt = lambda i: jnp.minimum(i + 1, last)
    ctx = lambda i: n_xb
    if cfg.kind == "na":
        ex_spec = pl.BlockSpec((None, BLK, 3 * BLK), lambda h, i: (h, 0, 0))
    else:
        ex_spec = pl.BlockSpec((None, r, 1), lambda h, i: (h, 0, 0))
    stat = pl.BlockSpec((None, None, r, 1), lambda h, i: (h, i, 0, 0))
    m_spec = pl.BlockSpec((None, BLK, 3 * BLK), lambda h, i: (cfg.pattern(i), 0, 0))
    kv_all = pl.BlockSpec((cfg.n_tot, HEAD_DIM), lambda h, i: (0, h))
    return _call(
        body, (qkv,) * 9 + (extra, o, do, lse, cfg.mask_q), name=name, grid=(cfg.nkv, cfg.n_blk),
        in_specs=[pl.BlockSpec((BLK, qw), lambda h, i: (i, cfg.q0 + h)),
                  kv(cfg.k0, prev), kv(cfg.k0, own), kv(cfg.k0, nxt),
                  kv(cfg.v0, prev), kv(cfg.v0, own), kv(cfg.v0, nxt),
                  kv(cfg.k0, ctx), kv(cfg.v0, ctx), ex_spec,
                  pl.BlockSpec((BLK, qw), lambda h, i: (i, h)),
                  pl.BlockSpec((BLK, qw), lambda h, i: (i, do_col0 + h)), stat, m_spec],
        out_specs=[pl.BlockSpec((BLK, qw), lambda h, i: (i, h)), stat, ex_spec, kv_all, kv_all],
        out_shape=[jax.ShapeDtypeStruct((cfg.n_tot, cfg.nkv * qw), bf16),
                   jax.ShapeDtypeStruct((cfg.nkv, cfg.n_blk, r, 1), f32),
                   jax.ShapeDtypeStruct(extra.shape, f32)]
        + [jax.ShapeDtypeStruct((cfg.n_tot, cfg.nkv * HEAD_DIM), f32)] * 2,
        sem=("parallel", "arbitrary"), rider=rider)


def _toeplitz_basis():
    qc, kc = np.meshgrid(np.arange(GRID_W), np.arange(GRID_W), indexing="ij")
    e = (kc - qc + NA_COLS - 1)[None] == np.arange(2 * NA_COLS - 1)[:, None, None]
    return e.reshape(2 * NA_COLS - 1, GRID_W * GRID_W).astype(np.float32)


def _whole(f, ins, outs, name):
    def body(*refs):
        res = f(*[r[...] for r in refs[:len(ins)]])
        for o_ref, val in zip(refs[len(ins):], res):
            o_ref[...] = val.astype(o_ref.dtype)

    return pl.pallas_call(body, name=name,
                          out_shape=[jax.ShapeDtypeStruct(s, d) for s, d in outs])(*ins)


def _whole_bwd(f, ins, cts, name):
    n = len(ins)

    def body(*refs):
        _, vjp = jax.vjp(f, *[r[...] for r in refs[:n]])
        grads = vjp(tuple(r[...] for r in refs[n:n + len(cts)]))
        for o_ref, g in zip(refs[n + len(cts):], grads):
            o_ref[...] = g

    return pl.pallas_call(body, name=name,
                          out_shape=[jax.ShapeDtypeStruct(a.shape, f32) for a in ins])(*ins, *cts)


def _f_discretise(ar, ai, ldt, br, bi):
    dt = jnp.exp(ldt)
    mag = jnp.exp(ar * dt)
    lam_r, lam_i = mag * jnp.cos(ai * dt), mag * jnp.sin(ai * dt)
    den = ar * ar + ai * ai
    nr = lam_r - 1.0
    coef_r = (nr * ar + lam_i * ai) / den
    coef_i = (lam_i * ar - nr * ai) / den
    return (lam_r, lam_i, coef_r[None] * br - coef_i[None] * bi, coef_r[None] * bi + coef_i[None] * br)


SEG_LEN = BLK // SCAN_SEG
N_STATE = GROUPS_PER_CHUNK * SSM_STATE
CHUNK_CH = GROUPS_PER_CHUNK * SSM_GROUP


def _scan_perm():
    r = np.arange(BLK)
    t = (r % SCAN_SEG) * SEG_LEN + r // SCAN_SEG
    pm = np.zeros((BLK, BLK), np.float32)
    pm[r, t] = 1.0
    return jnp.asarray(pm, bf16), jnp.asarray(pm.T, bf16)


def _row_perm(x, pm, name, *, out_dtype, split, add=None, add_rows=None):
    n, c = x.shape
    add_tiles = None if add is None else add_rows // BLK

    def body(*refs):
        x_ref, pm_ref = refs[0], refs[1]
        o_ref = refs[-1]
        xv = x_ref[...]
        if split:
            out = _unpermute(pm_ref[...], xv.astype(f32))
        else:
            out = jnp.dot(pm_ref[...], xv.astype(bf16), preferred_element_type=f32)
        if add is not None:
            out = out + jnp.where(pl.program_id(0) < add_tiles, refs[2][...], 0.0)
        o_ref[...] = out.astype(o_ref.dtype)

    row = pl.BlockSpec((BLK, c), lambda i: (i, 0))
    in_specs, args = [row, pl.BlockSpec((BLK, BLK), lambda i: (0, 0))], [x, pm]
    if add is not None:
        in_specs.append(pl.BlockSpec((BLK, c), lambda i: (jnp.minimum(i, add_tiles - 1), 0)))
        args.append(add)
    return pl.pallas_call(
        body, name=name, grid=(n // BLK,), in_specs=in_specs, out_specs=row,
        out_shape=jax.ShapeDtypeStruct((n, c), out_dtype), compiler_params=_params(("parallel",)),
    )(*args)


def _glu_pre(u, yf, yr, dvec, pmt, n_x, name):
    d = u.shape[1]

    def body(u_ref, yf_ref, yr_ref, d_ref, pmt_ref, o_ref):
        y = d_ref[...] * u_ref[...] + _unpermute(pmt_ref[...], yf_ref[...] + yr_ref[...])
        o_ref[...] = jax.nn.gelu(y).astype(bf16)

    row = pl.BlockSpec((BLK, d), lambda i: (i, 0))
    return pl.pallas_call(
        body, name=name, grid=(n_x // BLK,),
        in_specs=[row, row, row, pl.BlockSpec((1, d), lambda i: (0, 0)), pl.BlockSpec((BLK, BLK), lambda i: (0, 0))],
        out_specs=row, out_shape=jax.ShapeDtypeStruct((n_x, d), bf16), compiler_params=_params(("parallel",)),
    )(u, yf, yr, dvec, pmt)


def _glu_pre_bwd(u, yf, yr, dvec, pm, pmt, dgy, n_x, name):
    d = u.shape[1]

    def body(u_ref, yf_ref, yr_ref, d_ref, pm_ref, pmt_ref, ct_ref, dyp_ref, dud_ref, dd_ref):
        uv = u_ref[...]
        y = d_ref[...] * uv + _unpermute(pmt_ref[...], yf_ref[...] + yr_ref[...])
        _, vjp = jax.vjp(jax.nn.gelu, y)
        dy = vjp(ct_ref[...])[0]
        dyp_ref[...] = jnp.dot(pm_ref[...], dy.astype(bf16), preferred_element_type=f32).astype(bf16)
        dud_ref[...] = d_ref[...] * dy
        part = jnp.sum(dy * uv, axis=0, keepdims=True)

        @pl.when(pl.program_id(0) == 0)
        def _():
            dd_ref[...] = part

        @pl.when(pl.program_id(0) > 0)
        def _():
            dd_ref[...] += part

    row = pl.BlockSpec((BLK, d), lambda i: (i, 0))
    vec = pl.BlockSpec((1, d), lambda i: (0, 0))
    sq = pl.BlockSpec((BLK, BLK), lambda i: (0, 0))
    return pl.pallas_call(
        body, name=name, grid=(n_x // BLK,), in_specs=[row, row, row, vec, sq, sq, row],
        out_specs=[row, row, vec],
        out_shape=[jax.ShapeDtypeStruct((n_x, d), bf16), jax.ShapeDtypeStruct((n_x, d), f32),
                   jax.ShapeDtypeStruct((1, d), f32)],
        compiler_params=_params(("arbitrary",)),
    )(u, yf, yr, dvec, pm, pmt, dgy)


def _block_order(n_xb, n_blk, reverse):
    n_cb = n_blk - n_xb
    if reverse:
        return lambda cc: jnp.where(cc < n_cb, n_blk - 1 - cc, n_xb - 1 - (cc - n_cb))
    return lambda cc: jnp.where(cc < n_cb, n_xb + cc, cc - n_cb)


def _unpermute(pmt, y):
    hi = y.astype(bf16)
    lo = (y - hi.astype(f32)).astype(bf16)
    return jnp.dot(pmt, hi, preferred_element_type=f32) + jnp.dot(pmt, lo, preferred_element_type=f32)


def _lam_pow(lr, li):
    for _ in range(int(math.log2(SEG_LEN))):
        lr, li = lr * lr - li * li, 2.0 * lr * li
    return lr, li


def _s5_fwd(up, lam, bblk, cblk, n_x, reverse, name, rider=None):
    n_tot, d = up.shape
    nq, n_blk, n_xb = d // CHUNK_CH, n_tot // BLK, n_x // BLK
    order = _block_order(n_xb, n_blk, reverse)
    ns = N_STATE
    seg_order = range(SCAN_SEG - 1, -1, -1) if reverse else range(SCAN_SEG)

    def body(u_ref, lam_ref, b_ref, c_ref, y_ref, cp_ref, bu_ref, st_ref, carry_ref):
        cc = pl.program_id(1)

        @pl.when(cc == 0)
        def _():
            carry_ref[...] = jnp.zeros_like(carry_ref)

        up_v = u_ref[...]
        bu_ref[0] = jnp.dot(up_v, b_ref[0], preferred_element_type=f32)
        bu_ref[1] = jnp.dot(up_v, b_ref[1], preferred_element_type=f32)
        lr, li = lam_ref[0:1, :], lam_ref[1:2, :]
        lrb, lib = jnp.broadcast_to(lr, (SCAN_SEG, ns)), jnp.broadcast_to(li, (SCAN_SEG, ns))

        def step(p, s, store):
            sr, si = s
            j = SEG_LEN - 1 - p if reverse else p
            off = pl.multiple_of(j * SCAN_SEG, SCAN_SEG)
            nsr = lrb * sr - lib * si + bu_ref[0, pl.ds(off, SCAN_SEG), :]
            nsi = lrb * si + lib * sr + bu_ref[1, pl.ds(off, SCAN_SEG), :]
            if store:
                st_ref[0, pl.ds(off, SCAN_SEG), :] = nsr
                st_ref[1, pl.ds(off, SCAN_SEG), :] = nsi
            return nsr, nsi

        zero = jnp.zeros((SCAN_SEG, ns), f32)
        er, ei = lax.fori_loop(0, SEG_LEN, lambda j, s: step(j, s, False), (zero, zero))
        pr, pi = _lam_pow(lr, li)
        cr, ci = carry_ref[0, 0:1, :], carry_ref[1, 0:1, :]
        rows_r, rows_i = [None] * SCAN_SEG, [None] * SCAN_SEG
        for k in seg_order:
            rows_r[k], rows_i[k] = cr, ci
            cr, ci = pr * cr - pi * ci + er[k:k + 1], pr * ci + pi * cr + ei[k:k + 1]
        carry_ref[0] = jnp.broadcast_to(cr, (SCAN_SEG, ns))
        carry_ref[1] = jnp.broadcast_to(ci, (SCAN_SEG, ns))
        cpr, cpi = jnp.concatenate(rows_r, axis=0), jnp.concatenate(rows_i, axis=0)
        cp_ref[0] = cpr
        cp_ref[1] = cpi
        lax.fori_loop(0, SEG_LEN, lambda j, s: step(j, s, True), (cpr, cpi))
        y_ref[...] = (jnp.dot(st_ref[0].astype(bf16), c_ref[0], preferred_element_type=f32)
                      - jnp.dot(st_ref[1].astype(bf16), c_ref[1], preferred_element_type=f32))

    return _call(
        body, (up, lam, bblk, cblk), name=name, grid=(nq, n_blk),
        in_specs=[pl.BlockSpec((BLK, CHUNK_CH), lambda q,cc: (order(cc), q)),
                  pl.BlockSpec((None, 2, ns), lambda q, cc: (q, 0, 0)),
                  pl.BlockSpec((None, 2, CHUNK_CH, ns), lambda q,cc: (q, 0, 0, 0)),
                  pl.BlockSpec((None, 2, ns, CHUNK_CH), lambda q,cc: (q, 0, 0, 0))],
        out_specs=[pl.BlockSpec((BLK, CHUNK_CH), lambda q,cc: (order(cc), q)),
                   pl.BlockSpec((None, None, 2, SCAN_SEG, ns), lambda q, cc: (q, cc, 0, 0, 0))],
        out_shape=[jax.ShapeDtypeStruct((n_tot, d), f32),
                   jax.ShapeDtypeStruct((nq, n_blk, 2, SCAN_SEG, ns), f32)],
        scratch_shapes=[pltpu.VMEM((2, BLK, ns), f32), pltpu.VMEM((2, BLK, ns), f32),
                        pltpu.VMEM((2, SCAN_SEG, ns), f32)],
        sem=("parallel", "arbitrary"), rider=rider)


def _s5_bwd(u, dy, cprev, lam, bblk, cblk, bblk_t, cblk_t, n_x, reverse, name, add=None, rider=None):
    n_tot, d = u.shape
    nq, n_blk, n_xb = d // CHUNK_CH, n_tot // BLK, n_x // BLK
    order = _block_order(n_xb, n_blk, reverse)
    ns = N_STATE
    blk_of = lambda step: order(n_blk - 1 - step)
    has_add = add is not None
    jof = (lambda p: SEG_LEN - 1 - p) if reverse else (lambda p: p)
    adj_seg_order = range(SCAN_SEG) if reverse else range(SCAN_SEG - 1, -1, -1)

    def body(*refs):
        u_ref, dy_ref, cp_ref, lam_ref, b_ref, c_ref, bt_ref, ct_ref = refs[:8]
        rest = refs[8:]
        if has_add:
            add_ref, rest = rest[0], rest[1:]
        du_ref, dlam_ref, db_ref, dc_ref, bu_ref, st_ref, ds_ref, acarry_ref = rest
        step_id = pl.program_id(1)
        is_x = blk_of(step_id) < n_xb

        @pl.when(step_id == 0)
        def _():
            acarry_ref[...] = jnp.zeros_like(acarry_ref)
            dlam_ref[...] = jnp.zeros_like(dlam_ref)
            db_ref[...] = jnp.zeros_like(db_ref)
            dc_ref[...] = jnp.zeros_like(dc_ref)

        up = u_ref[...]
        bu_ref[0] = jnp.dot(up, b_ref[0], preferred_element_type=f32)
        bu_ref[1] = jnp.dot(up, b_ref[1], preferred_element_type=f32)
        lr, li = lam_ref[0:1, :], lam_ref[1:2, :]
        lrb, lib = jnp.broadcast_to(lr, (SCAN_SEG, ns)), jnp.broadcast_to(li, (SCAN_SEG, ns))
        cpr, cpi = cp_ref[0], cp_ref[1]

        def fstep(p, s):
            sr, si = s
            off = pl.multiple_of(jof(p) * SCAN_SEG, SCAN_SEG)
            nsr = lrb * sr - lib * si + bu_ref[0, pl.ds(off, SCAN_SEG), :]
            nsi = lrb * si + lib * sr + bu_ref[1, pl.ds(off, SCAN_SEG), :]
            st_ref[0, pl.ds(off, SCAN_SEG), :] = nsr
            st_ref[1, pl.ds(off, SCAN_SEG), :] = nsi
            return nsr, nsi

        lax.fori_loop(0, SEG_LEN, fstep, (cpr, cpi))

        dyp = jnp.where(is_x, dy_ref[...], jnp.zeros_like(dy_ref))
        ds_ref[0] = jnp.dot(dyp, ct_ref[0], preferred_element_type=f32)
        ds_ref[1] = -jnp.dot(dyp, ct_ref[1], preferred_element_type=f32)

        def adj(p, a):
            ar, ai = a
            off = pl.multiple_of(jof(p) * SCAN_SEG, SCAN_SEG)
            nar = ds_ref[0, pl.ds(off, SCAN_SEG), :] + lrb * ar + lib * ai
            nai = ds_ref[1, pl.ds(off, SCAN_SEG), :] - lib * ar + lrb * ai
            return nar, nai

        zero = jnp.zeros((SCAN_SEG, ns), f32)
        er, ei = lax.fori_loop(0, SEG_LEN, lambda jj, a: adj(SEG_LEN - 1 - jj, a), (zero, zero))
        pr, pi = _lam_pow(lr, li)
        nr_, ni_ = acarry_ref[0, 0:1, :], acarry_ref[1, 0:1, :]
        rows_r, rows_i = [None] * SCAN_SEG, [None] * SCAN_SEG
        for k in adj_seg_order:
            rows_r[k], rows_i[k] = nr_, ni_
            nr_, ni_ = er[k:k + 1] + pr * nr_ + pi * ni_, ei[k:k + 1] + pr * ni_ - pi * nr_
        acarry_ref[0] = jnp.broadcast_to(nr_, (SCAN_SEG, ns))
        acarry_ref[1] = jnp.broadcast_to(ni_, (SCAN_SEG, ns))
        an_r, an_i = jnp.concatenate(rows_r, axis=0), jnp.concatenate(rows_i, axis=0)

        def adj2(jj, carry):
            ar, ai, glr, gli = carry
            p = SEG_LEN - 1 - jj
            nar, nai = adj(p, (ar, ai))
            off = pl.multiple_of(jof(p) * SCAN_SEG, SCAN_SEG)
            ds_ref[0, pl.ds(off, SCAN_SEG), :] = nar
            ds_ref[1, pl.ds(off, SCAN_SEG), :] = nai
            poff = pl.multiple_of(jof(p - 1) * SCAN_SEG, SCAN_SEG)
            spr, spi = st_ref[0, pl.ds(poff, SCAN_SEG), :], st_ref[1, pl.ds(poff, SCAN_SEG), :]
            return nar, nai, glr + nar * spr + nai * spi, gli - nar * spi + nai * spr

        ar, ai, glr, gli = lax.fori_loop(0, SEG_LEN - 1, adj2, (an_r, an_i, zero, zero))
        nar, nai = adj(0, (ar, ai))
        first = jof(0) * SCAN_SEG
        ds_ref[0, first:first + SCAN_SEG, :] = nar
        ds_ref[1, first:first + SCAN_SEG, :] = nai
        glr = glr + nar * cpr + nai * cpi
        gli = gli - nar * cpi + nai * cpr
        dlam_ref[0:1, :] += jnp.sum(glr, axis=0, keepdims=True)
        dlam_ref[1:2, :] += jnp.sum(gli, axis=0, keepdims=True)

        a_r, a_i = ds_ref[0].astype(bf16), ds_ref[1].astype(bf16)
        du = jnp.dot(a_r, bt_ref[0], preferred_element_type=f32) + jnp.dot(a_i, bt_ref[1],
                                                                        preferred_element_type=f32)
        if has_add:
            du = du + add_ref[...]
        du_ref[...] = du
        db_ref[0] += lax.dot_general(up, a_r, TN, preferred_element_type=f32)
        db_ref[1] += lax.dot_general(up, a_i, TN, preferred_element_type=f32)
        dc_ref[0] += lax.dot_general(st_ref[0].astype(bf16), dyp, TN, preferred_element_type=f32)
        dc_ref[1] -= lax.dot_general(st_ref[1].astype(bf16), dyp, TN, preferred_element_type=f32)

    tok = pl.BlockSpec((BLK, CHUNK_CH), lambda q,s: (blk_of(s), q))
    in_specs = [tok, pl.BlockSpec((BLK, CHUNK_CH), lambda q,s: (jnp.minimum(blk_of(s), n_xb - 1), q)),
                pl.BlockSpec((None, None, 2, SCAN_SEG, ns), lambda q, s: (q, n_blk - 1 - s, 0, 0, 0)),
                pl.BlockSpec((None, 2, ns), lambda q, s: (q, 0, 0)),
                pl.BlockSpec((None, 2, CHUNK_CH, ns), lambda q,s: (q, 0, 0, 0)),
                pl.BlockSpec((None, 2, ns, CHUNK_CH), lambda q,s: (q, 0, 0, 0)),
                pl.BlockSpec((None, 2, ns, CHUNK_CH), lambda q,s: (q, 0, 0, 0)),
                pl.BlockSpec((None, 2, CHUNK_CH, ns), lambda q,s: (q, 0, 0, 0))]
    args = [u, dy, cprev, lam, bblk, cblk, bblk_t, cblk_t]
    if has_add:
        in_specs.append(tok)
        args.append(add)
    return _call(
        body, args, name=name, grid=(nq, n_blk), in_specs=in_specs,
        out_specs=[tok, pl.BlockSpec((None, 2, ns), lambda q, s: (q, 0, 0)),
                   pl.BlockSpec((None, 2, CHUNK_CH, ns), lambda q,s: (q, 0, 0, 0)),
                   pl.BlockSpec((None, 2, ns, CHUNK_CH), lambda q,s: (q, 0, 0, 0))],
        out_shape=[jax.ShapeDtypeStruct((n_tot, d), f32), jax.ShapeDtypeStruct((nq, 2, ns), f32),
                   jax.ShapeDtypeStruct((nq, 2, CHUNK_CH, ns), f32),
                   jax.ShapeDtypeStruct((nq, 2, ns, CHUNK_CH), f32)],
        scratch_shapes=[pltpu.VMEM((2, BLK, ns), f32), pltpu.VMEM((2, BLK, ns), f32),
                        pltpu.VMEM((2, BLK, ns), f32), pltpu.VMEM((2, SCAN_SEG, ns), f32)],
        sem=("parallel", "arbitrary"), rider=rider)


def _exchange(src, gather, name):
    return _run_exchange(_Exchange(src, gather), name)


def _sum0(x, name):
    n, r, c = x.shape
    tr = _pick(r, (512, 256, 128, 64, 32, 16, 8))

    def body(x_ref, o_ref):
        acc = x_ref[0].astype(f32)
        for k in range(1, n):
            acc = acc + x_ref[k].astype(f32)
        o_ref[...] = acc

    return pl.pallas_call(
        body, name=name, grid=(r // tr,), in_specs=[pl.BlockSpec((n, tr, c), lambda i: (0, i, 0))],
        out_specs=pl.BlockSpec((tr, c), lambda i: (i, 0)), out_shape=jax.ShapeDtypeStruct((r, c), f32),
        compiler_params=_params(("parallel",)),
    )(x)


LANES = 1024


def _rows_of(shape):
    return -(-int(np.prod(shape)) // LANES)


def _pack(parts, dtype):
    rows = []
    for p in parts:
        flat = p.astype(dtype).reshape(-1)
        rows.append(jnp.pad(flat, (0, _rows_of(p.shape) * LANES - flat.shape[0])).reshape(-1, LANES))
    out = jnp.concatenate(rows, axis=0)
    return jnp.pad(out, ((0, (-out.shape[0]) % 16), (0, 0)))


def _unpack(packed, shapes):
    out, row = [], 0
    for s in shapes:
        n = _rows_of(s)
        out.append(packed[row:row + n].reshape(-1)[:int(np.prod(s))].reshape(s))
        row += n
    return out


def _vec(a):
    return a.reshape(-1, 1, a.shape[-1])


def _local_step(xa, tgt, mods, wts, sm, n_x):
    n_tot, d = xa.shape
    kw = dict(nrows=n_tot, n_x_rows=n_x)
    kx = dict(nrows=n_x, n_x_rows=n_x)
    mv = lambda l, k: mods[l, :, k][:, None, :]
    mx = lambda l, k: mods[l, 0:1, k][:, None, :]
    nmix, nffn = sm["norm_mix"], sm["norm_ffn"]
    cos, sin = _rope_tables(n_x, n_tot - n_x)
    cfg_a, cfg_b = _AttnCfg("na", n_x, n_tot), _AttnCfg("swa", n_x, n_tot)

    h1 = _rowwise(_f_normmod, [xa], [_vec(nmix[0:1]), mv(0, 1), mv(0, 0)], [(d, bf16)], name="l0_norm_mix", **kw)[0]
    qkv32 = _mm(h1, wts("w_in"), "nn", f32, "l0_in_proj", comm=wts)
    qkv = _qkv_post(qkv32, cos, sin, "l0_qkv_post")
    rpb2 = jnp.zeros((128, 128), f32).at[:NA_HEADS * 15, :31].set(sm["rpb"].reshape(NA_HEADS * 15, 31))
    basis = jnp.zeros((128, GRID_W * GRID_W), f32).at[:31].set(_toeplitz_basis())
    tz = _small_dot(rpb2, basis, "rpb_expand")[:NA_HEADS * 15].reshape(NA_HEADS, 15, GRID_W, GRID_W)
    bias = jnp.stack([tz[:, 3 - a:15 - a] for a in range(4)], axis=1).transpose(0, 1, 3, 2, 4).reshape(
        NA_HEADS, BLK, 3 * BLK)
    sink_rows = jnp.repeat(sm["sink"].reshape(NB_KV_HEADS, NB_GROUP, 1), BLK, axis=1).reshape(
        NB_KV_HEADS, NB_GROUP * BLK, 1)
    oa, lse_a = _attn_fwd(qkv, cfg_a, bias, "l0_na_fwd", rider=wts.take(CARRIER_US["na_fwd"]))
    ob, lse_b = _attn_fwd(qkv, cfg_b, sink_rows, "l0_swa_fwd", rider=wts.take(CARRIER_US["swa_fwd"]))
    w_out_a, w_out_b = wts("w_out")[:A_W], wts("w_out")[A_W:]
    y1, xb = _mm(oa, w_out_a, "nn", f32, "l0_out_proj", a2=ob, b2=w_out_b, comm=wts, gated=(xa, mv(0, 2), n_x))

    def ffn_fwd(xin, l, vec_of, kk, tag):
        h = _rowwise(_f_normmod, [xin], [_vec(nffn[l:l + 1]), vec_of(l, 4), vec_of(l, 3)], [(d, bf16)],
                     name=tag + "_norm_ffn", **kk)[0]
        a, b, u = _ffn_up(h, wts("w1", l), wts("w3", l), tag + "_ffn_up", comm=wts)
        fo = _mm(u, wts("w2", l), "nn", f32, tag + "_ffn_w2", comm=wts)
        xo = _rowwise(_f_gated_add, [xin, fo], [vec_of(l, 5)], [(d, f32)], name=tag + "_res_ffn", **kk)[0]
        return xo, (h, a, b, u, fo)

    def ffn_bwd(d_out, xin, saved, l, vec_of, kk, tag):
        h, a, b, u, fo = saved
        rows = kk["nrows"]
        dfo, dg2 = _rowwise_bwd(_f_gated_add, [xin, fo], [vec_of(l, 5)], [d_out], {1: bf16},
                                name=tag + "_res_ffn_bwd", **kk)
        wts.grad("w2", l, _mm(u[:rows], dfo, "tn", f32, tag + "_ffn_w2_dw", comm=wts))
        da, db = _ffn_down_bwd(dfo, wts("w2", l), a, b, tag + "_ffn_down_bwd", comm=wts)
        dh = _mm(da, wts("w1", l), "nt", f32, tag + "_ffn_w13_dx", a2=db, b2=wts("w3", l), comm=wts)
        wts.grad("w1", l, _mm(h[:rows], da, "tn", f32, tag + "_ffn_w1_dw", comm=wts))
        wts.grad("w3", l, _mm(h[:rows], db, "tn", f32, tag + "_ffn_w3_dw", comm=wts))
        dxin, dnw, dsc, dsh = _rowwise_bwd(
            _f_normmod, [xin], [_vec(nffn[l:l + 1]), vec_of(l, 4), vec_of(l, 3)], [dh], {0: f32},
            name=tag + "_norm_ffn_bwd", residual=d_out, **kk)
        return dxin, dnw, dsc, dsh, dg2

    xc, ffn0 = ffn_fwd(xb, 0, mv, kw, "l0")

    hs = _rowwise(_f_normmod, [xc], [_vec(nmix[1:2]), mv(1, 1), mv(1, 0)], [(d, f32)], name="l1_norm_mix", **kw)[0]
    g2n = sm["a_re"].shape[1]
    nq = d // CHUNK_CH
    ar2, ai2 = sm["a_re"].reshape(2 * g2n, SSM_STATE), sm["a_im"].reshape(2 * g2n, SSM_STATE)
    ldt2 = sm["log_dt"].reshape(2 * g2n, 1)
    bt_re = sm["b_re"].transpose(3, 0, 1, 2).reshape(SSM_GROUP, 2 * g2n, SSM_STATE)
    bt_im = sm["b_im"].transpose(3, 0, 1, 2).reshape(SSM_GROUP, 2 * g2n, SSM_STATE)
    disc_in = [ar2, ai2, ldt2, bt_re, bt_im]
    lam_r, lam_i, bbar_r, bbar_i = _whole(
        _f_discretise, disc_in,
        [((2 * g2n, SSM_STATE), f32)] * 2 + [((SSM_GROUP, 2 * g2n, SSM_STATE), f32)] * 2, "s5_discretise")
    eye = jnp.eye(GROUPS_PER_CHUNK, dtype=f32)
    eye6 = eye.astype(bf16)[None, None, :, None, :, None]

    def blockdiag_b(bbar):
        t = bbar.astype(bf16).reshape(SSM_GROUP, 2, nq, GROUPS_PER_CHUNK, SSM_STATE).transpose(1, 2, 3, 0, 4)
        return (t[:, :, :, :, None, :] * eye6).reshape(2, nq, CHUNK_CH, N_STATE)

    def blockdiag_c(cw):
        t = cw.astype(bf16).reshape(2, nq, GROUPS_PER_CHUNK, SSM_GROUP, SSM_STATE).transpose(0, 1, 2, 4, 3)
        return (t[:, :, :, :, None, :] * eye6).reshape(2, nq, N_STATE, CHUNK_CH)

    lam = jnp.stack([lam_r.reshape(2, nq, N_STATE), lam_i.reshape(2, nq, N_STATE)], axis=2)
    bblk = jnp.stack([blockdiag_b(bbar_r), blockdiag_b(bbar_i)], axis=2)
    cblk = jnp.stack([blockdiag_c(sm["c_re"]), blockdiag_c(sm["c_im"])], axis=2)
    bblk16, cblk16 = bblk, cblk
    bblk_t, cblk_t = bblk16.transpose(0, 1, 2, 4, 3), cblk16.transpose(0, 1, 2, 4, 3)
    pm, pmt = _scan_perm()
    hs_p = _row_perm(hs, pm, "l1_s5_perm", out_dtype=bf16, split=False)
    ys, cps = [], []
    for dr in range(2):
        yd, cp = _s5_fwd(hs_p, lam[dr], bblk16[dr], cblk16[dr], n_x, dr == 1, "l1_s5_fwd%d" % dr,
                         rider=wts.take(CARRIER_US["s5_fwd"]))
        ys.append(yd)
        cps.append(cp)
    dvec = sm["ssm_d"].reshape(1, d)
    gy = _glu_pre(hs, ys[0], ys[1], dvec, pmt, n_x, "l1_glu_pre")
    zv = _mm(gy, wts("w_glu_v"), "nn", bf16, "l1_glu_val", comm=wts)
    zg = _mm(gy, wts("w_glu_g"), "nn", bf16, "l1_glu_gate", comm=wts)
    bv, bg = sm["b_glu"][:d].reshape(1, 1, d), sm["b_glu"][d:].reshape(1, 1, d)
    xd = _rowwise(_f_glu_post, [xc, zv, zg], [mx(1, 2), bv, bg], [(d, f32)], name="l1_glu_post", **kx)[0]
    xe, ffn1 = ffn_fwd(xd, 1, mx, kx, "l1")

    d_xe, d_nfinal, loss_blk = _loss_head(xe, tgt, sm["norm_final"].reshape(1, d), nrows=n_x, name="loss_head")
    d_xd, dnffn1, dsc2_1, dsh2_1, dg2_1 = ffn_bwd(d_xe, xd, ffn1, 1, mx, kx, "l1")
    dzv, dzg, dg1_1, dbv, dbg = _rowwise_bwd(_f_glu_post, [xc, zv, zg], [mx(1, 2), bv, bg], [d_xd],
                                             {1: bf16, 2: bf16}, name="l1_glu_post_bwd", **kx)
    dgy = _mm(dzv, wts("w_glu_v"), "nt", f32, "l1_glu_dx", a2=dzg, b2=wts("w_glu_g"), comm=wts)
    dwglu_v = _mm(gy, dzv, "tn", f32, "l1_glu_val_dw", comm=wts)
    dwglu_g = _mm(gy, dzg, "tn", f32, "l1_glu_gate_dw", comm=wts)
    wts.grad("w_glu", 0, jnp.concatenate([dwglu_v, dwglu_g], axis=1))
    dy_p, du_skip, d_dvec = _glu_pre_bwd(hs, ys[0], ys[1], dvec, pm, pmt, dgy, n_x, "l1_glu_pre_bwd")
    du0, dlam0, db0, dc0 = _s5_bwd(hs_p, dy_p, cps[0], lam[0], bblk16[0], cblk16[0], bblk_t[0], cblk_t[0], n_x,
                                   False, "l1_s5_bwd0", rider=wts.take(CARRIER_US["s5_bwd"]))
    du1, dlam1, db1, dc1 = _s5_bwd(hs_p, dy_p, cps[1], lam[1], bblk16[1], cblk16[1], bblk_t[1], cblk_t[1], n_x,
                                   True, "l1_s5_bwd1", add=du0, rider=wts.take(CARRIER_US["s5_bwd"]))
    d_hs = _row_perm(du1, pmt, "l1_s5_unperm", out_dtype=f32, split=True, add=du_skip, add_rows=n_x)
    d_xc, dnmix1, dsc1_1, dsh1_1 = _rowwise_bwd(
        _f_normmod, [xc], [_vec(nmix[1:2]), mv(1, 1), mv(1, 0)], [d_hs], {0: f32},
        name="l1_norm_mix_bwd", residual=d_xd, residual_rows=n_x, **kw)
    dlam = jnp.stack([dlam0, dlam1])
    dlam_r, dlam_i = dlam[:, :, 0].reshape(2 * g2n, SSM_STATE), dlam[:, :, 1].reshape(2 * g2n, SSM_STATE)
    dbb = jnp.stack([db0, db1]).reshape(2, nq, 2, GROUPS_PER_CHUNK, SSM_GROUP, GROUPS_PER_CHUNK, SSM_STATE)
    eye7 = eye[None, None, None, :, None, :, None]
    dbbar = (dbb * eye7).sum(axis=5).transpose(2, 4, 0, 1, 3, 5).reshape(2, SSM_GROUP, 2 * g2n, SSM_STATE)
    dcc = jnp.stack([dc0, dc1]).reshape(2, nq, 2, GROUPS_PER_CHUNK, SSM_STATE, GROUPS_PER_CHUNK, SSM_GROUP)
    dcw = (dcc * eye7).sum(axis=5).transpose(2, 0, 1, 3, 5, 4).reshape(2, 2, g2n, SSM_GROUP, SSM_STATE)
    d_ar, d_ai, d_ldt, d_btr, d_bti = _whole_bwd(_f_discretise, disc_in, [dlam_r, dlam_i, dbbar[0], dbbar[1]],
                                                 "s5_discretise_bwd")
    to_b = lambda t: t.reshape(SSM_GROUP, 2, g2n, SSM_STATE).transpose(1, 2, 3, 0)
    wts.early(dict(
        loss=loss_blk[0, 0].reshape(1), a_re=d_ar.reshape(sm["a_re"].shape), a_im=d_ai.reshape(sm["a_im"].shape),
        log_dt=d_ldt.reshape(sm["log_dt"].shape), b_re=to_b(d_btr), b_im=to_b(d_bti), c_re=dcw[0], c_im=dcw[1],
        ssm_d=d_dvec.reshape(d), b_glu=jnp.concatenate([dbv.reshape(d), dbg.reshape(d)]),
        norm_final=d_nfinal.reshape(d)))

    d_xb, dnffn0, dsc2_0, dsh2_0, dg2_0 = ffn_bwd(d_xc, xb, ffn0, 0, mv, kw, "l0")
    dy1, dg1_0 = _rowwise_bwd(_f_gated_add, [xa, y1], [mv(0, 2)], [d_xb], {1: bf16}, name="l0_res_mix_bwd", **kw)
    d_o = _mm(dy1, wts("w_out"), "nt", bf16, "l0_out_proj_dx", comm=wts)
    wts.grad("w_out", 0, jnp.concatenate([_mm(oa, dy1, "tn", f32, "l0_out_proj_dw_a", comm=wts),
                                          _mm(ob, dy1, "tn", f32, "l0_out_proj_dw_b", comm=wts)], axis=0))
    dqa, _, dbias, dka, dva = _attn_bwd(qkv, oa, d_o, lse_a, cfg_a, bias, "l0_na_bwd",
                                          rider=wts.take(CARRIER_US["na_bwd"]))
    dqb, _, dsink_rows, dkb, dvb = _attn_bwd(qkv, ob, d_o, lse_b, cfg_b, sink_rows, "l0_swa_bwd",
                                               rider=wts.take(CARRIER_US["swa_bwd"]))
    d_qkv = _qkv_post_bwd([dqa, dka, dva, dqb, dkb, dvb], cos, sin, "l0_qkv_post_bwd")
    wts.grad("w_in", 0, _mm(h1, d_qkv, "tn", f32, "l0_in_proj_dw", comm=wts))
    dh1 = _mm(d_qkv, wts("w_in"), "nt", f32, "l0_in_proj_dx", comm=wts)
    d_xa, dnmix0, dsc1_0, dsh1_0 = _rowwise_bwd(
        _f_normmod, [xa], [_vec(nmix[0:1]), mv(0, 1), mv(0, 0)], [dh1], {0: f32},
        name="l0_norm_mix_bwd", residual=d_xb, latent_grads_only=True, **kw)
    dbias5 = dbias.reshape(NA_HEADS, 4, GRID_W, 12, GRID_W).transpose(0, 1, 3, 2, 4)
    dtz = sum(jnp.pad(dbias5[:, a], ((0, 0), (3 - a, a), (0, 0), (0, 0))) for a in range(4))
    dtz2 = jnp.zeros((128, GRID_W * GRID_W), f32).at[:NA_HEADS * 15].set(dtz.reshape(NA_HEADS * 15, -1))
    d_rpb = _small_dot(dtz2, basis.T, "rpb_expand_bwd")[:NA_HEADS * 15, :31].reshape(sm["rpb"].shape)
    d_sink = dsink_rows.reshape(NB_KV_HEADS * NB_GROUP, BLK).sum(axis=1)

    zero_c = jnp.zeros((1, 1, d), f32)
    both = lambda gx: jnp.concatenate([gx, zero_c], axis=0)
    dmods = jnp.stack([
        jnp.stack([dsh1_0, dsc1_0, dg1_0, dsh2_0, dsc2_0, dg2_0], axis=2),
        jnp.stack([dsh1_1, dsc1_1, both(dg1_1), both(dsh2_1), both(dsc2_1), both(dg2_1)], axis=2),
    ])[:, :, 0]
    late = dict(norm_mix=jnp.concatenate([dnmix0[0], dnmix1[0]]), norm_ffn=jnp.concatenate([dnffn0[0], dnffn1[0]]),
                rpb=d_rpb, sink=d_sink)
    return d_xa, late, dmods


WEIGHTS = ("c_ctx", "ada_w", "ada_b", "norm_mix", "norm_ffn", "ffn_w1", "ffn_w3", "ffn_w2", "attn_w_in",
           "attn_w_out", "attn_rpb", "attn_sink", "ssm_a_re", "ssm_a_im", "ssm_log_dt", "ssm_b_re", "ssm_b_im",
           "ssm_c_re", "ssm_c_im", "ssm_d", "ssm_w_glu", "ssm_b_glu", "norm_final")
SHARDED_BIG = ("ffn_w1", "ffn_w3", "ffn_w2", "attn_w_in", "attn_w_out", "ssm_w_glu")
BIG = ("ada_w",) + SHARDED_BIG
SMALL = tuple(n for n in WEIGHTS if n not in BIG)


BIG_SLABS = dict(w_in=("attn_w_in", "cols"), w_out=("attn_w_out", "rows"), w1=("ffn_w1", "cols"),
                 w3=("ffn_w3", "cols"), w2=("ffn_w2", "rows"), w_glu=("ssm_w_glu", "cols"))
GATHER_ORDER = (("w_in", 0), ("w_out", 0), ("w1", 0), ("w3", 0), ("w2", 0), ("w_glu", 0), ("w1", 1), ("w3", 1),
                ("w2", 1))


class _Weights(_Comm):
    def __init__(self, w, pack_early):
        super().__init__()
        self.w, self.pack_early = w, pack_early
        self.gathers = {key: self.post(w[BIG_SLABS[key[0]][0]][key[1]].astype(bf16), True) for key in GATHER_ORDER}
        self.full, self.scatters, self.early_gather = {}, {}, None

    def __call__(self, name, layer=0):
        key = ("w_glu", 0) if name.startswith("w_glu") else (name, layer)
        if key not in self.full:
            blk = self.finish(self.gathers[key])
            if BIG_SLABS[key[0]][1] == "cols":
                self.full[key] = blk.transpose(1, 0, 2).reshape(blk.shape[1], -1)
            else:
                self.full[key] = blk.reshape(-1, blk.shape[2])
        full = self.full[key]
        if name.startswith("w_glu"):
            half = full.shape[1] // 2
            return full[:, :half] if name == "w_glu_v" else full[:, half:]
        return full

    def grad(self, name, layer, dw):
        if BIG_SLABS[name][1] == "cols":
            blk = dw.reshape(dw.shape[0], N_DEV, -1).transpose(1, 0, 2)
        else:
            blk = dw.reshape(N_DEV, -1, dw.shape[1])
        self.scatters[(name, layer)] = self.post(blk.astype(bf16), False)

    def early(self, parts):
        self.early_shapes = {k: v.shape for k, v in parts.items()}
        self.early_gather = self.post(self.pack_early(parts), True)

    def reduced(self, name):
        short = next(k for k, v in BIG_SLABS.items() if v[0] == name)
        layers = [l for (n, l) in self.scatters if n == short]
        parts = [_sum0(self.finish(self.scatters[(short, l)]), "sum_%s%d" % (short, l)) for l in sorted(layers)]
        return jnp.stack(parts).reshape(self.w[name].shape)


def kernel(x, c, ctx, c_ctx, ada_w, ada_b, norm_mix, norm_ffn, ffn_w1, ffn_w3, ffn_w2, attn_w_in, attn_w_out, attn_rpb, attn_sink, ssm_a_re, ssm_a_im, ssm_log_dt, ssm_b_re, ssm_b_im, ssm_c_re, ssm_c_im, ssm_d, ssm_w_glu, ssm_b_glu, norm_final, loss_target, m_c_ctx, m_ada_w, m_ada_b, m_norm_mix, m_norm_ffn, m_ffn_w1, m_ffn_w3, m_ffn_w2, m_attn_w_in, m_attn_w_out, m_attn_rpb, m_attn_sink, m_ssm_a_re, m_ssm_a_im, m_ssm_log_dt, m_ssm_b_re, m_ssm_b_im, m_ssm_c_re, m_ssm_c_im, m_ssm_d, m_ssm_w_glu, m_ssm_b_glu, m_norm_final, v_c_ctx, v_ada_w, v_ada_b, v_norm_mix, v_norm_ffn, v_ffn_w1, v_ffn_w3, v_ffn_w2, v_attn_w_in, v_attn_w_out, v_attn_rpb, v_attn_sink, v_ssm_a_re, v_ssm_a_im, v_ssm_log_dt, v_ssm_b_re, v_ssm_b_im, v_ssm_c_re, v_ssm_c_im, v_ssm_d, v_ssm_w_glu, v_ssm_b_glu, v_norm_final):
    p = dict(locals())
    w = {n: p[n] for n in WEIGHTS}
    me = 4 * lax.axis_index("x") + 2 * lax.axis_index("y") + lax.axis_index("c")
    n_x, d = x.shape[1], x.shape[2]
    cols = ada_w.shape[2]
    d8 = d // N_DEV

    first = jnp.concatenate([c[0], ssm_d[0], ssm_b_glu[0]])[None]
    g0 = _exchange(first, True, "gather_vectors")[:, 0]
    c_all, d_full, bglu_full = g0[:, :d], g0[:, d:d + d8].reshape(d), g0[:, d + d8:].reshape(2 * d)
    cc = jnp.concatenate([c_all, c_ctx[None], jnp.zeros((16 - N_DEV - 1, d), f32)])
    sc_all = _whole(_f_silu, [cc], [((16, d), f32)], "silu_c")[0]
    my_cols = lambda a: lax.dynamic_slice_in_dim(a, me * cols, cols, axis=a.ndim - 1)
    mod_loc = jnp.stack([_mm(sc_all, ada_w[l], "nn", f32, "ada_mod%d" % l) for l in range(2)])
    mod_loc = mod_loc + my_cols(ada_b)[:, None, :]
    mg = _exchange(mod_loc.reshape(32, cols), True, "gather_mod")
    mod_all = mg.reshape(N_DEV, 2, 16, cols).transpose(1, 2, 0, 3).reshape(2, 16, N_DEV * cols)
    mod_x = lax.dynamic_index_in_dim(mod_all, me, axis=1, keepdims=False)
    mods = jnp.stack([mod_x, mod_all[:, N_DEV]], axis=1).reshape(2, 2, 6, d)

    wts = _Weights(w, lambda parts: _pack(list(parts.values()), f32))
    sm = dict(norm_mix=norm_mix, norm_ffn=norm_ffn, rpb=attn_rpb[0], sink=attn_sink[0], a_re=ssm_a_re[0],
              a_im=ssm_a_im[0], log_dt=ssm_log_dt[0], b_re=ssm_b_re[0], b_im=ssm_b_im[0], c_re=ssm_c_re[0],
              c_im=ssm_c_im[0], ssm_d=d_full, b_glu=bglu_full, norm_final=norm_final)

    xa = jnp.concatenate([x[0], ctx[0]], axis=0)
    d_xa, late, dmods = _local_step(xa, loss_target[0], mods, wts, sm, n_x)

    late_names = ("norm_mix", "norm_ffn", "rpb", "sink")
    parts = [late[n] for n in late_names] + [dmods[:, 0].reshape(2, 6 * d), dmods[:, 1].reshape(2, 6 * d)]
    shapes = [q.shape for q in parts]
    late_gather = wts.post(_pack(parts, f32), True)
    grads = {n: wts.reduced(n) for n in SHARDED_BIG}
    sg = wts.finish(late_gather)
    tot = _unpack(_sum0(sg, "sum_small"), shapes)
    early = _unpack(_sum0(wts.finish(wts.early_gather), "sum_small_early"), list(wts.early_shapes.values()))
    ts = dict(zip(late_names, tot[:len(late_names)]), **dict(zip(wts.early_shapes, early)))
    loss = ts["loss"][0]
    tot_dmod_x, tot_dmod_c = tot[-2], tot[-1]
    row_x = sum(_rows_of(s) for s in shapes[:-2])
    dmod_x_all = sg[:, row_x:row_x + _rows_of(shapes[-2])].reshape(N_DEV, -1)[:, :12 * d].reshape(N_DEV, 2, 6 * d)
    dm = jnp.concatenate([dmod_x_all.transpose(1, 0, 2), tot_dmod_c[:, None, :],
                          jnp.zeros((2, 16 - N_DEV - 1, 6 * d), f32)], axis=1)
    dm_loc = my_cols(dm)
    grads["ada_w"] = jnp.stack([_mm(sc_all, dm_loc[l], "tn", f32, "ada_w_grad%d" % l) for l in range(2)])
    grads["ada_b"] = tot_dmod_x + tot_dmod_c
    dsc_part = _mm(dm_loc[0], ada_w[0], "nt", f32, "silu_c_grad", a2=dm_loc[1], b2=ada_w[1])[N_DEV:N_DEV + 1]
    dsc = _sum0(_exchange(dsc_part, True, "gather_cctx"), "sum_cctx")
    grads["c_ctx"] = _whole_bwd(_f_silu, [c_ctx[None]], [dsc], "silu_cctx_bwd")[0][0]
    grads.update(norm_mix=ts["norm_mix"], norm_ffn=ts["norm_ffn"], attn_rpb=ts["rpb"][None],
                 attn_sink=ts["sink"][None], ssm_a_re=ts["a_re"][None], ssm_a_im=ts["a_im"][None],
                 ssm_log_dt=ts["log_dt"][None], ssm_b_re=ts["b_re"][None], ssm_b_im=ts["b_im"][None],
                 ssm_c_re=ts["c_re"][None], ssm_c_im=ts["c_im"][None],
                 ssm_d=lax.dynamic_slice_in_dim(ts["ssm_d"], me * d8, d8)[None],
                 ssm_b_glu=lax.dynamic_slice_in_dim(ts["b_glu"], me * 2 * d8, 2 * d8)[None],
                 norm_final=ts["norm_final"])

    delta, new_m, new_v = {}, {}, {}
    for n in BIG:
        two_d = lambda a: a.reshape(-1, a.shape[-1])
        dl, m2, v2 = _adamw(two_d(w[n]), two_d(grads[n]), two_d(p["m_" + n]), two_d(p["v_" + n]), "adamw_" + n)
        delta[n], new_m[n], new_v[n] = (t.reshape(w[n].shape) for t in (dl, m2, v2))
    sshapes = [w[n].shape for n in SMALL]
    packs = [_pack([src[n] for n in SMALL], f32) for src in
             (w, grads, {n: p["m_" + n] for n in SMALL}, {n: p["v_" + n] for n in SMALL})]
    for store, t in zip((delta, new_m, new_v), _adamw(*packs, "adamw_small")):
        store.update(zip(SMALL, _unpack(t, sshapes)))

    return (loss, d_xa[None], *[grads[n] for n in WEIGHTS], *[delta[n] for n in WEIGHTS],
            *[new_m[n] for n in WEIGHTS], *[new_v[n] for n in WEIGHTS])
```
